```python
import jax, jax.numpy as jnp
from jax import lax
import numpy as np

D_MODEL = 1024
BATCH = 8
SEQ = 2048
DEPTH = 1
DEC_BATCH = 32
DEC_SEQ = 16
PAST_LEN = 4096

CHUNK = 64
N_LEFT_CHUNKS = 8
ATTN_WINDOW = N_LEFT_CHUNKS * CHUNK

MIX_WIDTH = D_MODEL
ATTN_WIDTH = MIX_WIDTH // 2
LRU_WIDTH = MIX_WIDTH - ATTN_WIDTH
HEAD_DIM = 64
N_HEADS_A = ATTN_WIDTH // HEAD_DIM
REL_CLIP = 128
N_LRU_BLOCKS = 8
LRU_BLOCK = LRU_WIDTH // N_LRU_BLOCKS
CONV_WIDTH = 4
LRU_C = 8.0
IN_PROJ_WIDTH = 3 * ATTN_WIDTH + 2 * LRU_WIDTH

N_EXPERTS = 32
TOP_K = 4
D_FF = D_MODEL
SWIGLU_LIMIT = 7.0
SWIGLU_ALPHA = 1.702
MOE_BLOCK = 256

NORM_EPS = 1e-6
NEG_INF = -1e30

kernel_name = "hymba_chunkband_rglru_moe_stream_step"


def rms_norm(x, w):
    xf = x.astype(jnp.float32)
    xf = xf * lax.rsqrt(jnp.mean(xf * xf, axis=-1, keepdims=True) + NORM_EPS)
    return (xf * w.astype(jnp.float32)).astype(x.dtype)


def ada_terms(c, w_ada, b_ada):
    mod = jax.nn.silu(c) @ w_ada + b_ada
    return jnp.split(mod[:, None, :], 6, axis=-1)


def rel_bias_lookup(rel_bias, rel):
    return rel_bias[:, jnp.clip(rel, -REL_CLIP, REL_CLIP) + REL_CLIP].astype(jnp.float32)


def band_attention_prompt(q, k, v, rel_bias):
    B, S, H, Dh = q.shape
    nc = S // CHUNK
    band = ATTN_WINDOW + CHUNK
    pad = ((0, 0), (ATTN_WINDOW, 0), (0, 0), (0, 0))
    kp = jnp.pad(k, pad)
    vp = jnp.pad(v, pad)
    idx = (jnp.arange(nc) * CHUNK)[:, None] + jnp.arange(band)[None, :]
    kb = kp[:, idx]
    vb = vp[:, idx]
    qc = q.reshape(B, nc, CHUNK, H, Dh)
    s = jnp.einsum('bcqhd,bckhd->bhcqk', qc, kb).astype(jnp.float32) * (Dh ** -0.5)
    rel = jnp.arange(band)[None, :] - ATTN_WINDOW - jnp.arange(CHUNK)[:, None]
    s = s + rel_bias_lookup(rel_bias, rel)[None, :, None]
    valid = idx >= ATTN_WINDOW
    s = jnp.where(valid[None, None, :, None, :], s, NEG_INF)
    p = jax.nn.softmax(s, axis=-1).astype(v.dtype)
    o = jnp.einsum('bhcqk,bckhd->bcqhd', p, vb)
    return o.reshape(B, S, H * Dh)


def band_attention_step(q, k_new, v_new, k_cache, v_cache, rel_bias):
    B, T, H, Dh = q.shape
    R = k_cache.shape[1]
    k = jnp.concatenate([k_cache, k_new], axis=1)
    v = jnp.concatenate([v_cache, v_new], axis=1)
    key_off = jnp.concatenate([jnp.arange(R) - R, jnp.arange(T)])
    rel = key_off[None, :] - jnp.arange(T)[:, None]
    s = jnp.einsum('bqhd,bkhd->bhqk', q, k).astype(jnp.float32) * (Dh ** -0.5)
    s = s + rel_bias_lookup(rel_bias, rel)[None]
    p = jax.nn.softmax(s, axis=-1).astype(v.dtype)
    o = jnp.einsum('bhqk,bkhd->bqhd', p, v)
    return o.reshape(B, T, H * Dh)


def causal_conv(x, prefix, w, b):
    T = x.shape[1]
    xp = jnp.concatenate([prefix.astype(x.dtype), x], axis=1)
    y = b + xp[:, 0:T] * w[0]
    for tap in range(1, CONV_WIDTH):
        y = y + xp[:, tap:tap + T] * w[tap]
    return y, xp[:, -(CONV_WIDTH - 1):]


def rg_lru(x, h0, w_a, b_a, w_x, b_x, lam):
    B, T, C = x.shape
    xb = x.reshape(B, T, N_LRU_BLOCKS, LRU_BLOCK)
    r = jax.nn.sigmoid(jnp.einsum('bthi,hij->bthj', xb, w_a) + b_a).reshape(B, T, C)
    i = jax.nn.sigmoid(jnp.einsum('bthi,hij->bthj', xb, w_x) + b_x).reshape(B, T, C)
    log_a = LRU_C * r.astype(jnp.float32) * jax.nn.log_sigmoid(lam.astype(jnp.float32))
    a = jnp.exp(log_a)
    u = jnp.sqrt(-jnp.expm1(2.0 * log_a)) * (i * x).astype(jnp.float32)
    u = u.at[:, 0].add(a[:, 0] * h0.astype(jnp.float32))

    def combine(left, right):
        a1, b1 = left
        a2, b2 = right
        return a1 * a2, a2 * b1 + b2

    _, h = lax.associative_scan(combine, (a, u), axis=1)
    return h.astype(x.dtype), h[:, -1].astype(x.dtype)


def moe_ffn(h, w_router, b_router, w_gu, b_gu, w_down, b_down):
    B, T, D = h.shape
    x = h.reshape(-1, D)
    n_tok = x.shape[0]
    logits = (x @ w_router + b_router).astype(jnp.float32)
    top_val, top_idx = lax.top_k(logits, TOP_K)
    gates = jax.nn.softmax(top_val, axis=-1)
    n_asg = n_tok * TOP_K
    e_flat = top_idx.reshape(-1)
    tok_flat = jnp.arange(n_asg) // TOP_K
    g_flat = gates.reshape(-1)
    order = jnp.argsort(e_flat)
    e_sorted = e_flat[order]
    counts = jnp.bincount(e_flat, length=N_EXPERTS)
    start = jnp.cumsum(counts) - counts
    padded = (counts + MOE_BLOCK - 1) // MOE_BLOCK * MOE_BLOCK
    pend = jnp.cumsum(padded)
    pstart = pend - padded
    dest = pstart[e_sorted] + jnp.arange(n_asg) - start[e_sorted]
    n_blocks = -(-n_asg // MOE_BLOCK) + N_EXPERTS
    n_slots = n_blocks * MOE_BLOCK
    slot_tok = jnp.zeros((n_slots,), jnp.int32).at[dest].set(tok_flat[order].astype(jnp.int32))
    slot_gate = jnp.zeros((n_slots,), jnp.float32).at[dest].set(g_flat[order])
    block_expert = jnp.minimum(
        jnp.searchsorted(pend, jnp.arange(n_blocks) * MOE_BLOCK, side='right'), N_EXPERTS - 1)
    xs = x[slot_tok].reshape(n_blocks, MOE_BLOCK, D)

    def expert_block(args):
        xb, e = args
        gu = xb @ w_gu[e] + b_gu[e]
        gate, up = gu[..., :D_FF], gu[..., D_FF:]
        gate = jnp.minimum(gate, SWIGLU_LIMIT)
        up = jnp.clip(up, -SWIGLU_LIMIT, SWIGLU_LIMIT)
        glu = gate * jax.nn.sigmoid(gate * SWIGLU_ALPHA)
        return ((up + 1.0) * glu) @ w_down[e] + b_down[e]

    ys = lax.map(expert_block, (xs, block_expert)).reshape(n_slots, D)
    out = jnp.zeros((n_tok, D), jnp.float32).at[slot_tok].add(ys.astype(jnp.float32) * slot_gate[:, None])
    return out.astype(h.dtype).reshape(B, T, D)


def layer_forward(x, c, k_cache, v_cache, conv_prefix, h0, lw):
    (ln_mix, ln_ffn, w_ada, b_ada, w_in, q_norm, k_norm, rel_bias, conv_w, conv_b,
     w_rg, b_rg, w_ig, b_ig, lam, w_out, w_router, b_router, w_gu, b_gu, w_dn, b_dn) = lw
    B, T, _ = x.shape
    sh_m, sc_m, g_m, sh_f, sc_f, g_f = ada_terms(c, w_ada, b_ada)
    h = rms_norm(x, ln_mix) * (1.0 + sc_m) + sh_m
    proj = h @ w_in
    q, k, v, xr, yg = jnp.split(
        proj, [ATTN_WIDTH, 2 * ATTN_WIDTH, 3 * ATTN_WIDTH, 3 * ATTN_WIDTH + LRU_WIDTH], axis=-1)
    q = rms_norm(q.reshape(B, T, N_HEADS_A, HEAD_DIM), q_norm)
    k = rms_norm(k.reshape(B, T, N_HEADS_A, HEAD_DIM), k_norm)
    v = v.reshape(B, T, N_HEADS_A, HEAD_DIM)
    if k_cache is None:
        attn_out = band_attention_prompt(q, k, v, rel_bias)
    else:
        attn_out = band_attention_step(q, k, v, k_cache, v_cache, rel_bias)
    xc, conv_new = causal_conv(xr, conv_prefix, conv_w, conv_b)
    hr, h_last = rg_lru(xc, h0, w_rg, b_rg, w_ig, b_ig, lam)
    lru_out = hr * jax.nn.gelu(yg)
    mix = jnp.concatenate([attn_out, lru_out], axis=-1) @ w_out
    x = x + g_m * mix
    h2 = rms_norm(x, ln_ffn) * (1.0 + sc_f) + sh_f
    x = x + g_f * moe_ffn(h2, w_router, b_router, w_gu, b_gu, w_dn, b_dn)
    return x, k, v, conv_new, h_last


def setup_inputs(seed: int = 0) -> dict:
    key = jax.random.key(seed)
    ks = jax.random.split(key, 32)
    f32 = jnp.float32
    D = D_MODEL
    cache_rows = min(ATTN_WINDOW, PAST_LEN)

    def nrm(k, shape, scale):
        return jax.random.normal(k, shape, f32) * scale

    u = jax.random.uniform(ks[15], (DEPTH, LRU_WIDTH), f32, 0.9, 0.999)
    a_base = u ** (1.0 / LRU_C)
    lru_lambda = jnp.log(a_base) - jnp.log1p(-a_base)
    return {
        "x_prompt": nrm(ks[0], (BATCH, SEQ, D), 1.0),
        "x_sample": nrm(ks[1], (DEC_BATCH, DEC_SEQ, D), 1.0),
        "c_prompt": nrm(ks[2], (BATCH, D), 1.0),
        "c_sample": nrm(ks[3], (DEC_BATCH, D), 1.0),
        "cache_k": nrm(ks[4], (DEPTH, DEC_BATCH, cache_rows, N_HEADS_A, HEAD_DIM), 1.0),
        "cache_v": nrm(ks[5], (DEPTH, DEC_BATCH, cache_rows, N_HEADS_A, HEAD_DIM), 1.0),
        "state_conv": nrm(ks[6], (DEPTH, DEC_BATCH, CONV_WIDTH - 1, LRU_WIDTH), 1.0),
        "state_lru": nrm(ks[7], (DEPTH, DEC_BATCH, LRU_WIDTH), 0.5),
        "ln_mix_w": 1.0 + nrm(ks[8], (DEPTH, D), 0.05),
        "ln_ffn_w": 1.0 + nrm(ks[9], (DEPTH, D), 0.05),
        "w_ada": nrm(ks[10], (DEPTH, D, 6 * D), 0.5 * D ** -0.5),
        "b_ada": nrm(ks[11], (DEPTH, 6 * D), 0.02),
        "w_in": nrm(ks[12], (DEPTH, D, IN_PROJ_WIDTH), D ** -0.5),
        "q_norm_w": 1.0 + nrm(ks[13], (DEPTH, HEAD_DIM), 0.05),
        "k_norm_w": 1.0 + nrm(ks[14], (DEPTH, HEAD_DIM), 0.05),
        "rel_bias": nrm(ks[16], (DEPTH, N_HEADS_A, 2 * REL_CLIP + 1), 0.2),
        "conv_w": nrm(ks[17], (DEPTH, CONV_WIDTH, LRU_WIDTH), CONV_WIDTH ** -0.5),
        "conv_b": nrm(ks[18], (DEPTH, LRU_WIDTH), 0.02),
        "w_rgate": nrm(ks[19], (DEPTH, N_LRU_BLOCKS, LRU_BLOCK, LRU_BLOCK), LRU_BLOCK ** -0.5),
        "b_rgate": nrm(ks[20], (DEPTH, N_LRU_BLOCKS, LRU_BLOCK), 0.02),
        "w_igate": nrm(ks[21], (DEPTH, N_LRU_BLOCKS, LRU_BLOCK, LRU_BLOCK), LRU_BLOCK ** -0.5),
        "b_igate": nrm(ks[22], (DEPTH, N_LRU_BLOCKS, LRU_BLOCK), 0.02),
        "lru_lambda": lru_lambda,
        "w_out": nrm(ks[23], (DEPTH, MIX_WIDTH, D), MIX_WIDTH ** -0.5),
        "w_router": nrm(ks[24], (DEPTH, D, N_EXPERTS), D ** -0.5),
        "b_router": nrm(ks[25], (DEPTH, N_EXPERTS), 0.01),
        "w_gate_up": nrm(ks[26], (DEPTH, N_EXPERTS, D, 2 * D_FF), D ** -0.5),
        "b_gate_up": nrm(ks[27], (DEPTH, N_EXPERTS, 2 * D_FF), 0.02),
        "w_down": nrm(ks[28], (DEPTH, N_EXPERTS, D_FF, D), D_FF ** -0.5),
        "b_down": nrm(ks[29], (DEPTH, N_EXPERTS, D), 0.02),
    }


def reference(x_prompt, x_sample, c_prompt, c_sample, cache_k, cache_v, state_conv, state_lru,
              ln_mix_w, ln_ffn_w, w_ada, b_ada, w_in, q_norm_w, k_norm_w, rel_bias, conv_w, conv_b,
              w_rgate, b_rgate, w_igate, b_igate, lru_lambda, w_out, w_router, b_router,
              w_gate_up, b_gate_up, w_down, b_down):
    yp, ys = x_prompt, x_sample
    bp = x_prompt.shape[0]
    kp_l, vp_l, cp_l, hp_l, ks_l, vs_l, cs_l, hs_l = [], [], [], [], [], [], [], []
    for l in range(DEPTH):
        lw = (ln_mix_w[l], ln_ffn_w[l], w_ada[l], b_ada[l], w_in[l], q_norm_w[l], k_norm_w[l],
              rel_bias[l], conv_w[l], conv_b[l], w_rgate[l], b_rgate[l], w_igate[l], b_igate[l],
              lru_lambda[l], w_out[l], w_router[l], b_router[l], w_gate_up[l], b_gate_up[l],
              w_down[l], b_down[l])
        conv0 = jnp.zeros((bp, CONV_WIDTH - 1, LRU_WIDTH), x_prompt.dtype)
        h0 = jnp.zeros((bp, LRU_WIDTH), x_prompt.dtype)
        yp, k_p, v_p, conv_p, h_p = layer_forward(yp, c_prompt, None, None, conv0, h0, lw)
        keep = min(ATTN_WINDOW, k_p.shape[1])
        kp_l.append(k_p[:, -keep:])
        vp_l.append(v_p[:, -keep:])
        cp_l.append(conv_p)
        hp_l.append(h_p)
        ys, k_s, v_s, conv_s, h_s = layer_forward(ys, c_sample, cache_k[l], cache_v[l],
                                                  state_conv[l], state_lru[l], lw)
        ks_l.append(k_s)
        vs_l.append(v_s)
        cs_l.append(conv_s)
        hs_l.append(h_s)
    return (yp, ys, jnp.stack(kp_l), jnp.stack(vp_l), jnp.stack(cp_l), jnp.stack(hp_l),
            jnp.stack(ks_l), jnp.stack(vs_l), jnp.stack(cs_l), jnp.stack(hs_l))
```

```python
import functools

import jax
import jax.numpy as jnp
from jax import lax
from jax.experimental import pallas as pl
from jax.experimental.pallas import tpu as pltpu

F32 = jnp.float32
BF16 = jnp.bfloat16

CHUNK = 64
N_LEFT_CHUNKS = 8
ATTN_WINDOW = N_LEFT_CHUNKS * CHUNK
HEAD_DIM = 64
REL_CLIP = 128
CONV_WIDTH = 4
LRU_C = 8.0
N_EXPERTS = 32
TOP_K = 4
SWIGLU_LIMIT = 7.0
SWIGLU_ALPHA = 1.702
NORM_EPS = 1e-6
NEG_INF = -1e30

LANES = 128
SUBLANES = 8
BF16_ROWS = 16
MXU_DIM = 256

TOKEN_TILE = 512
ATTN_Q_TILE = 256
LRU_TILE = 256
EXPERT_ROWS = 256
BIAS_TABLE = 1024
VMEM_LIMIT = 56 * 1024 * 1024


def _params(sem, vmem=VMEM_LIMIT):
    return pltpu.CompilerParams(dimension_semantics=sem, vmem_limit_bytes=vmem)


def _ada_kernel(c_ref, w_ref, b_ref, o_ref):
    c = c_ref[...]
    s = (c * jax.nn.sigmoid(c)).astype(BF16)
    o_ref[...] = jnp.dot(s, w_ref[...].astype(BF16), preferred_element_type=F32) + b_ref[...]


def _ada(c_all, w_ada, b_ada):
    n, d = c_all.shape
    nout = w_ada.shape[1]
    tn = 1024
    return pl.pallas_call(
        _ada_kernel,
        grid=(nout // tn,),
        in_specs=[pl.BlockSpec((n, d), lambda j: (0, 0)),
                  pl.BlockSpec((d, tn), lambda j: (0, j)),
                  pl.BlockSpec((1, tn), lambda j: (0, j))],
        out_specs=pl.BlockSpec((n, tn), lambda j: (0, j)),
        out_shape=jax.ShapeDtypeStruct((n, nout), F32),
        compiler_params=_params(("arbitrary",)),
        name="ada",
    )(c_all, w_ada, b_ada.reshape(1, nout))


def _mixin_kernel(x_ref, sh_ref, sc_ref, ln_ref, win_ref, qn_ref, kn_ref, bd_ref,
                  q_ref, k_ref, v_ref, xr_ref, yg_ref, k32_ref, v32_ref):
    nb, tr, d = x_ref.shape
    m = nb * tr
    aw = q_ref.shape[-1]
    x = x_ref[...]
    ms = jnp.mean(x * x, axis=-1, keepdims=True)
    h = x * lax.rsqrt(ms + NORM_EPS) * ln_ref[...]
    h = h * (1.0 + sc_ref[...]) + sh_ref[...]
    hb = h.reshape(m, d).astype(BF16)

    def proj(part):
        return jnp.dot(hb, win_ref[:, part * aw:(part + 1) * aw], preferred_element_type=F32)

    def head_norm(t, w_ref):
        msq = jnp.dot((t * t).astype(BF16), bd_ref[...], preferred_element_type=F32)
        return t * lax.rsqrt(msq + NORM_EPS) * w_ref[...]

    q = head_norm(proj(0), qn_ref)
    k = head_norm(proj(1), kn_ref)
    v = proj(2)
    q_ref[...] = (q * (HEAD_DIM ** -0.5)).astype(BF16).reshape(nb, tr, aw)
    k_ref[...] = k.astype(BF16).reshape(nb, tr, aw)
    v_ref[...] = v.astype(BF16).reshape(nb, tr, aw)
    k32_ref[...] = k.reshape(nb, tr, aw)
    v32_ref[...] = v.reshape(nb, tr, aw)
    xr_ref[...] = proj(3).reshape(nb, tr, aw)
    yg_ref[...] = proj(4).reshape(nb, tr, aw)


def _mixin(x, sh, sc, ln_w, w_in_bf, qn_t, kn_t, bd, nb, tr):
    nbt, t, d = x.shape
    aw = qn_t.shape[-1]
    keep = min(ATTN_WINDOW, t)
    assert tr == keep or t == tr
    grid = (nbt // nb, t // tr)
    xmap = lambda b, i: (b, i, 0)
    mmap = lambda b, i: (b, 0, 0)
    cmap = lambda b, i: (0, 0)
    tmap = lambda b, i: (b, 0, 0)
    big = pl.BlockSpec((nb, tr, aw), xmap)
    tail = pl.BlockSpec((nb, keep, aw), tmap)
    return pl.pallas_call(
        _mixin_kernel,
        grid=grid,
        in_specs=[pl.BlockSpec((nb, tr, d), xmap),
                  pl.BlockSpec((nb, 1, d), mmap), pl.BlockSpec((nb, 1, d), mmap),
                  pl.BlockSpec((1, d), cmap),
                  pl.BlockSpec(w_in_bf.shape, cmap),
                  pl.BlockSpec((1, aw), cmap), pl.BlockSpec((1, aw), cmap),
                  pl.BlockSpec(bd.shape, cmap)],
        out_specs=[big, big, big, big, big, tail, tail],
        out_shape=[jax.ShapeDtypeStruct((nbt, t, aw), BF16)] * 3
        + [jax.ShapeDtypeStruct((nbt, t, aw), F32)] * 2
        + [jax.ShapeDtypeStruct((nbt, keep, aw), F32)] * 2,
        compiler_params=_params(("arbitrary", "arbitrary")),
        name="mixin",
    )(x, sh, sc, ln_w, w_in_bf, qn_t, kn_t, bd)


def _bias_table(rel_bias, off):
    h = rel_bias.shape[0]
    left = off - REL_CLIP
    right = BIAS_TABLE - left - (2 * REL_CLIP + 1)
    assert left >= 0 and right >= 0
    return jnp.concatenate([jnp.broadcast_to(rel_bias[:, :1], (h, left)), rel_bias,
                            jnp.broadcast_to(rel_bias[:, -1:], (h, right))], axis=1)


def _toeplitz(tab_row, rows, cols):
    t = jnp.broadcast_to(tab_row, (rows, BIAS_TABLE))
    t = pltpu.roll(t, BIAS_TABLE - (rows - 1), 1, stride=1, stride_axis=0)
    return t[:, :cols]


def _attn_kernel(q_ref, k0_ref, k1_ref, k2_ref, v0_ref, v1_ref, v2_ref, tab_ref, o_ref, bias_ref):
    b = pl.program_id(0)
    s = pl.program_id(1)
    qt = q_ref.shape[1]
    nk = 3 * qt
    nh = bias_ref.shape[0]

    @pl.when((b == 0) & (s == 0))
    def _():
        qi = lax.broadcasted_iota(jnp.int32, (qt, nk), 0) // CHUNK
        kc = lax.broadcasted_iota(jnp.int32, (qt, nk), 1) // CHUNK
        for h in range(nh):
            band = jnp.where(kc <= qi + N_LEFT_CHUNKS, _toeplitz(tab_ref[h:h + 1, :], qt, nk), NEG_INF)
            bias_ref[h] = jnp.where(kc >= qi, band, NEG_INF)

    q = q_ref[0]
    kcat = jnp.concatenate([k0_ref[0], k1_ref[0], k2_ref[0]], axis=0)
    vcat = jnp.concatenate([v0_ref[0], v1_ref[0], v2_ref[0]], axis=0)
    in_seq = lax.broadcasted_iota(jnp.int32, (qt, nk), 1) >= (2 - s) * qt
    outs = []
    for h in range(nh):
        sl = slice(h * HEAD_DIM, (h + 1) * HEAD_DIM)
        sc = lax.dot_general(q[:, sl], kcat[:, sl], (((1,), (1,)), ((), ())), preferred_element_type=F32)
        sc = jnp.where(in_seq, sc + bias_ref[h], NEG_INF)
        mx = jnp.max(sc, axis=-1, keepdims=True)
        p = jnp.exp(sc - mx)
        l = jnp.sum(p, axis=-1, keepdims=True)
        o = jnp.dot(p.astype(BF16), vcat[:, sl], preferred_element_type=F32)
        outs.append(o / l)
    o_ref[0] = jnp.concatenate(outs, axis=-1).astype(BF16)


def _attn_prompt(q, k, v, tab):
    b, s, aw = q.shape
    qt = ATTN_Q_TILE
    nh = aw // HEAD_DIM
    qspec = pl.BlockSpec((1, qt, aw), lambda i, j: (i, j, 0))

    def kspec(back):
        return pl.BlockSpec((1, qt, aw), lambda i, j: (i, jnp.maximum(j - back, 0), 0))

    return pl.pallas_call(
        _attn_kernel,
        grid=(b, s // qt),
        in_specs=[qspec, kspec(2), kspec(1), kspec(0), kspec(2), kspec(1), kspec(0),
                  pl.BlockSpec(tab.shape, lambda i, j: (0, 0))],
        out_specs=qspec,
        out_shape=jax.ShapeDtypeStruct((b, s, aw), BF16),
        scratch_shapes=[pltpu.VMEM((nh, qt, 3 * qt), F32)],
        compiler_params=_params(("arbitrary", "arbitrary")),
        name="attn_prompt",
    )(q, k, k, k, v, v, v, tab)


def _attn_step_kernel(q_ref, kn_ref, vn_ref, ck_ref, cv_ref, tab_ref, o_ref, bias_ref):
    b = pl.program_id(0)
    t = q_ref.shape[1]
    r = ck_ref.shape[1]
    nk = r + LANES
    nh = bias_ref.shape[0]
    aw = q_ref.shape[-1]

    @pl.when(b == 0)
    def _():
        ok = lax.broadcasted_iota(jnp.int32, (t, nk), 1) < r + t
        for h in range(nh):
            bias_ref[h] = jnp.where(ok, _toeplitz(tab_ref[h:h + 1, :], t, nk), NEG_INF)

    q = q_ref[0]
    pad = jnp.zeros((LANES - t, aw), BF16)
    kcat = jnp.concatenate([ck_ref[0].astype(BF16), kn_ref[0], pad], axis=0)
    vcat = jnp.concatenate([cv_ref[0].astype(BF16), vn_ref[0], pad], axis=0)
    outs = []
    for h in range(nh):
        sl = slice(h * HEAD_DIM, (h + 1) * HEAD_DIM)
        sc = lax.dot_general(q[:, sl], kcat[:, sl], (((1,), (1,)), ((), ())), preferred_element_type=F32)
        sc = sc + bias_ref[h]
        mx = jnp.max(sc, axis=-1, keepdims=True)
        p = jnp.exp(sc - mx)
        l = jnp.sum(p, axis=-1, keepdims=True)
        o = jnp.dot(p.astype(BF16), vcat[:, sl], preferred_element_type=F32)
        outs.append(o / l)
    o_ref[0] = jnp.concatenate(outs, axis=-1).astype(BF16)


def _attn_step(q, kn, vn, ck, cv, tab):
    b, t, aw = q.shape
    r = ck.shape[1]
    nh = aw // HEAD_DIM
    new = pl.BlockSpec((1, t, aw), lambda i: (i, 0, 0))
    old = pl.BlockSpec((1, r, aw), lambda i: (i, 0, 0))
    return pl.pallas_call(
        _attn_step_kernel,
        grid=(b,),
        in_specs=[new, new, new, old, old, pl.BlockSpec(tab.shape, lambda i: (0, 0))],
        out_specs=new,
        out_shape=jax.ShapeDtypeStruct((b, t, aw), BF16),
        scratch_shapes=[pltpu.VMEM((nh, t, r + LANES), F32)],
        compiler_params=_params(("arbitrary",)),
        name="attn_step",
    )(q, kn, vn, ck, cv, tab)


def _gelu_tanh(x):
    return x * (0.5 * (1.0 + jnp.tanh(0.7978845608028654 * (x + 0.044715 * (x * x * x)))))


def _lru_kernel(xr_ref, yg_ref, pre_ref, h0_ref, cw_ref, cb_ref, wa_ref, wx_ref, ba_ref, bx_ref, lam_ref,
                o_ref, tail_ref, hl_ref, cx_ref, ch_ref):
    step = pl.program_id(1)
    nb, tr, c = xr_ref.shape
    m = nb * tr
    half = c // 2

    @pl.when(step == 0)
    def _():
        cx_ref[...] = pre_ref[...]
        ch_ref[...] = h0_ref[...]

    x = xr_ref[...]
    xp = jnp.concatenate([cx_ref[...], x], axis=1)
    new_tail = xp[:, tr:tr + SUBLANES, :]
    xp2 = xp.reshape(nb * (tr + SUBLANES), c)
    y = cb_ref[...] + cw_ref[CONV_WIDTH - 1:CONV_WIDTH, :] * x
    for back in range(1, CONV_WIDTH):
        shifted = pltpu.roll(xp2, back, 0).reshape(nb, tr + SUBLANES, c)[:, SUBLANES:, :]
        y = y + cw_ref[CONV_WIDTH - 1 - back:CONV_WIDTH - back, :] * shifted
    y2 = y.reshape(m, c)
    yb = y2.astype(BF16)

    def gate(w_ref, b_ref):
        g = jnp.concatenate(
            [jnp.dot(yb[:, :half], w_ref[0], preferred_element_type=F32),
             jnp.dot(yb[:, half:], w_ref[1], preferred_element_type=F32)], axis=1)
        return jax.nn.sigmoid(g + b_ref[...])

    rg = gate(wa_ref, ba_ref)
    ig = gate(wx_ref, bx_ref)
    lam = lam_ref[...]
    log_sig = jnp.minimum(lam, 0.0) - jnp.log1p(jnp.exp(-jnp.abs(lam)))
    log_a = LRU_C * rg * log_sig
    a_cum = jnp.exp(log_a)
    b_cum = jnp.sqrt(-jnp.tanh(log_a) * (a_cum * a_cum + 1.0)) * (ig * y2)
    row = lax.broadcasted_iota(jnp.int32, (m, c), 0) % tr
    dist = 1
    while dist < tr:
        keep = row >= dist
        a_sh = jnp.where(keep, pltpu.roll(a_cum, dist, 0), 1.0)
        b_sh = jnp.where(keep, pltpu.roll(b_cum, dist, 0), 0.0)
        b_cum = a_cum * b_sh + b_cum
        a_cum = a_cum * a_sh
        dist *= 2
    h = a_cum.reshape(nb, tr, c) * ch_ref[...] + b_cum.reshape(nb, tr, c)
    o_ref[...] = (h * _gelu_tanh(yg_ref[...])).astype(BF16)
    h_last = h[:, tr - 1:tr, :]
    ch_ref[...] = h_last
    hl_ref[...] = h_last
    cx_ref[...] = new_tail
    tail_ref[...] = new_tail


def _lru(xr, yg, pre, h0, conv_w, conv_b, wa_bd, wx_bd, b_a, b_x, lam, nb, tr):
    nbt, t, c = xr.shape
    xmap = lambda b, i: (b, i, 0)
    smap = lambda b, i: (b, 0, 0)
    c2 = lambda b, i: (0, 0)
    c3 = lambda b, i: (0, 0, 0)
    big = pl.BlockSpec((nb, tr, c), xmap)
    return pl.pallas_call(
        _lru_kernel,
        grid=(nbt // nb, t // tr),
        in_specs=[big, big,
                  pl.BlockSpec((nb, SUBLANES, c), smap), pl.BlockSpec((nb, 1, c), smap),
                  pl.BlockSpec(conv_w.shape, c2), pl.BlockSpec((1, c), c2),
                  pl.BlockSpec(wa_bd.shape, c3), pl.BlockSpec(wx_bd.shape, c3),
                  pl.BlockSpec((1, c), c2), pl.BlockSpec((1, c), c2), pl.BlockSpec((1, c), c2)],
        out_specs=[big, pl.BlockSpec((nb, SUBLANES, c), smap), pl.BlockSpec((nb, 1, c), smap)],
        out_shape=[jax.ShapeDtypeStruct((nbt, t, c), BF16),
                   jax.ShapeDtypeStruct((nbt, SUBLANES, c), F32),
                   jax.ShapeDtypeStruct((nbt, 1, c), F32)],
        scratch_shapes=[pltpu.VMEM((nb, SUBLANES, c), F32), pltpu.VMEM((nb, 1, c), F32)],
        compiler_params=_params(("arbitrary", "arbitrary")),
        name="lru",
    )(xr, yg, pre, h0, conv_w, conv_b, wa_bd, wx_bd, b_a, b_x, lam)


def _outproj_kernel(*refs, aliased):
    (at_ref, lr_ref, x_ref, gm_ref, shf_ref, scf_ref, lnf_ref, wo_ref, wr_ref, br_ref) = refs[:10]
    x1_ref, h2_ref, route_ref, cnt_ref = refs[10 + aliased:]
    nb, tr, d = x_ref.shape
    m = nb * tr
    aw = at_ref.shape[-1]
    ne = wr_ref.shape[0]
    at = at_ref[...].reshape(m, aw)
    lr = lr_ref[...].reshape(m, aw)
    mix = (jnp.dot(at, wo_ref[0:aw, :], preferred_element_type=F32)
           + jnp.dot(lr, wo_ref[aw:2 * aw, :], preferred_element_type=F32))
    x1 = x_ref[...] + gm_ref[...] * mix.reshape(nb, tr, d)
    x1_ref[...] = x1
    ms = jnp.mean(x1 * x1, axis=-1, keepdims=True)
    h2 = x1 * lax.rsqrt(ms + NORM_EPS) * lnf_ref[...]
    h2 = (h2 * (1.0 + scf_ref[...]) + shf_ref[...]).reshape(m, d)
    h2_ref[...] = h2.astype(BF16)

    logits = lax.dot_general(wr_ref[...], h2, (((1,), (1,)), ((), ())),
                             precision=lax.Precision.HIGHEST, preferred_element_type=F32) + br_ref[...]
    e_iota = lax.broadcasted_iota(jnp.int32, (ne, m), 0).astype(F32)
    vals = logits
    top_v, sels = [], []
    for k in range(TOP_K):
        mx = jnp.max(vals, axis=0, keepdims=True)
        idx = jnp.min(jnp.where(vals == mx, e_iota, float(ne)), axis=0, keepdims=True)
        sel = e_iota == idx
        vals = jnp.where(sel, -jnp.inf, vals)
        top_v.append(mx)
        sels.append(sel)
        route_ref[0, k:k + 1, :] = idx
    ex = [jnp.exp(v - top_v[0]) for v in top_v]
    den = ex[0] + ex[1] + ex[2] + ex[3]
    chosen = jnp.zeros((ne, m), F32)
    for k in range(TOP_K):
        route_ref[0, 2 * TOP_K + k:2 * TOP_K + k + 1, :] = ex[k] / den
        chosen = chosen + jnp.where(sels[k], 1.0, 0.0)
    before = (lax.broadcasted_iota(jnp.int32, (m, m), 0) < lax.broadcasted_iota(jnp.int32, (m, m), 1))
    rank = jnp.dot(chosen.astype(BF16), jnp.where(before, 1.0, 0.0).astype(BF16), preferred_element_type=F32)
    for k in range(TOP_K):
        route_ref[0, TOP_K + k:TOP_K + k + 1, :] = jnp.sum(jnp.where(sels[k], rank, 0.0), axis=0, keepdims=True)
    route_ref[0, 3 * TOP_K:4 * TOP_K, :] = jnp.zeros((TOP_K, m), F32)
    cnt_ref[0] = jnp.broadcast_to(jnp.sum(chosen, axis=1, keepdims=True), (ne, LANES))


def _outproj(attn, lru_o, x, gm, shf, scf, lnf, w_out_bf, wr_t, br, nb, tr, n_tiles, tile0, prev):
    nbt, t, d = x.shape
    aw = attn.shape[-1]
    m = nb * tr
    assert m == TOKEN_TILE
    ne = wr_t.shape[0]
    tiles_per_seq = t // tr
    xmap = lambda b, i: (b, i, 0)
    mmap = lambda b, i: (b, 0, 0)
    c2 = lambda b, i: (0, 0)
    tile = lambda b, i: (tile0 + b * tiles_per_seq + i, 0)
    tile3 = lambda b, i: (tile0 + b * tiles_per_seq + i, 0, 0)
    mod = pl.BlockSpec((nb, 1, d), mmap)
    in_specs = [pl.BlockSpec((nb, tr, aw), xmap), pl.BlockSpec((nb, tr, aw), xmap),
                pl.BlockSpec((nb, tr, d), xmap), mod, mod, mod,
                pl.BlockSpec((1, d), c2), pl.BlockSpec(w_out_bf.shape, c2),
                pl.BlockSpec(wr_t.shape, c2), pl.BlockSpec((ne, 1), c2)]
    args = [attn, lru_o, x, gm, shf, scf, lnf, w_out_bf, wr_t, br]
    aliases = {}
    if prev is not None:
        in_specs += [pl.BlockSpec(memory_space=pl.ANY)] * 3
        args += list(prev)
        aliases = {10: 1, 11: 2, 12: 3}
    return pl.pallas_call(
        functools.partial(_outproj_kernel, aliased=len(aliases)),
        grid=(nbt // nb, tiles_per_seq),
        in_specs=in_specs,
        out_specs=[pl.BlockSpec((nb, tr, d), xmap), pl.BlockSpec((m, d), tile),
                   pl.BlockSpec((1, 4 * TOP_K, m), tile3), pl.BlockSpec((1, ne, LANES), tile3)],
        out_shape=[jax.ShapeDtypeStruct((nbt, t, d), F32),
                   jax.ShapeDtypeStruct((n_tiles * m, d), BF16),
                   jax.ShapeDtypeStruct((n_tiles, 4 * TOP_K, m), F32),
                   jax.ShapeDtypeStruct((n_tiles, ne, LANES), F32)],
        input_output_aliases=aliases,
        compiler_params=_params(("arbitrary", "arbitrary")),
        name="outproj",
    )(*args)


def _tile_rows(m):
    cap = TOP_K * m + N_EXPERTS * (BF16_ROWS - 1) + BF16_ROWS
    return -(-cap // TOKEN_TILE) * TOKEN_TILE


def _route_tables(cnt, n_blocks):
    nt = cnt.shape[0]
    g = BF16_ROWS
    bm = EXPERT_ROWS
    n_chunks = _tile_rows(TOKEN_TILE) // g
    n_gap = -(-(N_EXPERTS * (bm // g - 1)) // nt)
    pc = (cnt + g - 1) // g * g
    ctile = jnp.cumsum(pc, axis=1)
    toff = ctile - pc
    trow = ctile[:, -1]
    tot = jnp.sum(pc, axis=0)
    reg = (tot + bm - 1) // bm * bm
    creg = jnp.cumsum(reg)
    base = creg - reg
    goff = base[None, :] + jnp.cumsum(pc, axis=0) - pc
    n_used = creg[-1] // bm
    blk_row = jnp.minimum(jnp.arange(n_blocks, dtype=jnp.int32), n_used - 1) * bm
    blk_expert = jnp.minimum(jnp.searchsorted(creg, blk_row, side="right"), N_EXPERTS - 1).astype(jnp.int32)
    r = jnp.arange(n_chunks, dtype=jnp.int32) * g
    ce = jax.vmap(lambda c: jnp.searchsorted(c, r, side="right"))(ctile)
    ce = jnp.minimum(ce, N_EXPERTS - 1)
    dst = jnp.take_along_axis(goff, ce, axis=1) + r[None, :] - jnp.take_along_axis(toff, ce, axis=1)
    dst = jnp.where(r[None, :] < trow[:, None], dst, -1)
    gcnt = (reg - tot) // g
    gcum = jnp.cumsum(gcnt)
    s = jnp.arange(nt * n_gap, dtype=jnp.int32)
    ge = jnp.minimum(jnp.searchsorted(gcum, s, side="right"), N_EXPERTS - 1)
    gdst = base[ge] + tot[ge] + g * (s - (gcum[ge] - gcnt[ge]))
    gdst = jnp.where(s < gcum[-1], gdst, -1).reshape(nt, n_gap)
    table = jnp.concatenate([dst, gdst], axis=1).astype(jnp.int32)
    toff_b = jnp.broadcast_to(toff.astype(F32)[:, :, None], (nt, N_EXPERTS, LANES))
    return table, toff_b, blk_expert, n_used.astype(jnp.int32).reshape(1), n_chunks


def _slot_rows(route_ref, toff_ref, m):
    ne = toff_ref.shape[1]
    e_iota = lax.broadcasted_iota(jnp.int32, (ne, m), 0).astype(F32)
    toff_col = toff_ref[0][:, 0:1]
    pos = []
    for k in range(TOP_K):
        sel = e_iota == route_ref[0, k:k + 1, :]
        start = jnp.sum(jnp.where(sel, toff_col, 0.0), axis=0, keepdims=True)
        pos.append(start + route_ref[0, TOP_K + k:TOP_K + k + 1, :])
    return pos


def _dispatch_kernel(tab_ref, h2_ref, route_ref, toff_ref, xs_hbm, buf_ref, sem, *, n_chunks):
    t = pl.program_id(0)
    m = h2_ref.shape[0]
    rows = buf_ref.shape[0]
    n_entries = tab_ref.shape[1]
    pos = _slot_rows(route_ref, toff_ref, m)
    h2 = h2_ref[...]
    for rc in range(rows // m):
        r_iota = (lax.broadcasted_iota(jnp.int32, (m, m), 0) + rc * m).astype(F32)
        onehot = jnp.zeros((m, m), F32)
        for k in range(TOP_K):
            onehot = jnp.where(r_iota == pos[k], 1.0, onehot)
        buf_ref[rc * m:(rc + 1) * m, :] = jnp.dot(onehot.astype(BF16), h2, preferred_element_type=F32).astype(BF16)

    def copy(c):
        src = jnp.where(c < n_chunks, c * BF16_ROWS, rows - BF16_ROWS)
        return pltpu.make_async_copy(
            buf_ref.at[pl.ds(pl.multiple_of(src, BF16_ROWS), BF16_ROWS)],
            xs_hbm.at[pl.ds(pl.multiple_of(tab_ref[t, c], BF16_ROWS), BF16_ROWS)], sem)

    def start(c, carry):
        @pl.when(tab_ref[t, c] >= 0)
        def _():
            copy(c).start()
        return carry

    def wait(c, carry):
        @pl.when(tab_ref[t, c] >= 0)
        def _():
            copy(c).wait()
        return carry

    lax.fori_loop(0, n_entries, start, 0)
    lax.fori_loop(0, n_entries, wait, 0)


def _dispatch(table, h2, route, toff_b, n_rows, n_chunks):
    nt = route.shape[0]
    m = TOKEN_TILE
    d = h2.shape[1]
    rows = _tile_rows(m)
    grid_spec = pltpu.PrefetchScalarGridSpec(
        num_scalar_prefetch=1,
        grid=(nt,),
        in_specs=[pl.BlockSpec((m, d), lambda t, tab: (t, 0)),
                  pl.BlockSpec((1, 4 * TOP_K, m), lambda t, tab: (t, 0, 0)),
                  pl.BlockSpec((1, N_EXPERTS, LANES), lambda t, tab: (t, 0, 0))],
        out_specs=pl.BlockSpec(memory_space=pl.ANY),
        scratch_shapes=[pltpu.VMEM((rows, d), BF16), pltpu.SemaphoreType.DMA(())],
    )
    return pl.pallas_call(
        functools.partial(_dispatch_kernel, n_chunks=n_chunks),
        grid_spec=grid_spec,
        out_shape=jax.ShapeDtypeStruct((n_rows, d), BF16),
        compiler_params=_params(("arbitrary",)),
        name="dispatch",
    )(table, h2, route, toff_b)


def _expert_kernel(be_ref, nu_ref, xs_ref, wgu_ref, bgu_ref, wdn_ref, bdn_ref, ys_ref, wgu_bf, wdn_bf):
    w = pl.program_id(0)
    dff = wdn_ref.shape[1]
    prev = be_ref[jnp.maximum(w - 1, 0)]

    @pl.when((w == 0) | (be_ref[w] != prev))
    def _():
        wgu_bf[...] = wgu_ref[0].astype(BF16)
        wdn_bf[...] = wdn_ref[0].astype(BF16)

    @pl.when(w < nu_ref[0])
    def _():
        gu = jnp.dot(xs_ref[...], wgu_bf[...], preferred_element_type=F32) + bgu_ref[0]
        gate = jnp.minimum(gu[:, :dff], SWIGLU_LIMIT)
        up = jnp.clip(gu[:, dff:], -SWIGLU_LIMIT, SWIGLU_LIMIT)
        glu = gate * jax.nn.sigmoid(gate * SWIGLU_ALPHA)
        act = ((up + 1.0) * glu).astype(BF16)
        ys_ref[...] = (jnp.dot(act, wdn_bf[...], preferred_element_type=F32) + bdn_ref[0]).astype(BF16)


def _experts(blk_expert, n_used, xs, w_gu, b_gu, w_dn, b_dn):
    n_rows, d = xs.shape
    bm = EXPERT_ROWS
    ne, _, dff2 = w_gu.shape
    dff = w_dn.shape[1]
    row = lambda w, be, nu: (jnp.minimum(w, nu[0] - 1), 0)
    exp3 = lambda w, be, nu: (be[w], 0, 0)
    grid_spec = pltpu.PrefetchScalarGridSpec(
        num_scalar_prefetch=2,
        grid=(n_rows // bm,),
        in_specs=[pl.BlockSpec((bm, d), row),
                  pl.BlockSpec((1, d, dff2), exp3), pl.BlockSpec((1, 1, dff2), exp3),
                  pl.BlockSpec((1, dff, d), exp3), pl.BlockSpec((1, 1, d), exp3)],
        out_specs=pl.BlockSpec((bm, d), row),
        scratch_shapes=[pltpu.VMEM((d, dff2), BF16), pltpu.VMEM((dff, d), BF16)],
    )
    return pl.pallas_call(
        _expert_kernel,
        grid_spec=grid_spec,
        out_shape=jax.ShapeDtypeStruct((n_rows, d), BF16),
        compiler_params=_params(("arbitrary",)),
        name="experts",
    )(blk_expert, n_used, xs, w_gu, b_gu.reshape(ne, 1, dff2), w_dn, b_dn.reshape(ne, 1, d))


def _combine_kernel(tab_ref, ys_hbm, route_ref, toff_ref, x1_ref, gf_ref, o_ref, buf_ref, sem, *, n_chunks, tile0):
    step = pl.program_id(0)
    t = tile0 + step
    nb, tr, d = x1_ref.shape
    m = nb * tr
    rows = buf_ref.shape[0]

    @pl.when(step == 0)
    def _():
        buf_ref[...] = jnp.zeros(buf_ref.shape, BF16)

    def copy(c):
        return pltpu.make_async_copy(
            ys_hbm.at[pl.ds(pl.multiple_of(tab_ref[t, c], BF16_ROWS), BF16_ROWS)],
            buf_ref.at[pl.ds(pl.multiple_of(c * BF16_ROWS, BF16_ROWS), BF16_ROWS)], sem)

    def start(c, carry):
        @pl.when(tab_ref[t, c] >= 0)
        def _():
            copy(c).start()
        return carry

    def wait(c, carry):
        @pl.when(tab_ref[t, c] >= 0)
        def _():
            copy(c).wait()
        return carry

    lax.fori_loop(0, n_chunks, start, 0)

    pos = _slot_rows(route_ref, toff_ref, m)
    gates = [route_ref[0, 2 * TOP_K + k:2 * TOP_K + k + 1, :] for k in range(TOP_K)]
    stacked = jnp.concatenate(pos + gates + [jnp.zeros((LANES - 2 * TOP_K, m), F32)], axis=0)
    cols = stacked.T
    lax.fori_loop(0, n_chunks, wait, 0)

    acc = jnp.zeros((m, d), F32)
    for rc in range(rows // m):
        c_iota = (lax.broadcasted_iota(jnp.int32, (m, m), 1) + rc * m).astype(F32)
        weights = jnp.zeros((m, m), F32)
        for k in range(TOP_K):
            weights = jnp.where(c_iota == cols[:, k:k + 1], cols[:, TOP_K + k:TOP_K + k + 1], weights)
        acc = acc + jnp.dot(weights.astype(BF16), buf_ref[rc * m:(rc + 1) * m, :], preferred_element_type=F32)
    o_ref[...] = x1_ref[...] + gf_ref[...] * acc.reshape(nb, tr, d)


def _combine(table, ys, route, toff_b, x1, gf, nb, tr, tile0, n_chunks):
    nbt, t, d = x1.shape
    m = nb * tr
    assert m == TOKEN_TILE
    rows = _tile_rows(m)
    tiles_per_seq = t // tr
    n_steps = (nbt // nb) * tiles_per_seq
    xmap = lambda s, tab: (s // tiles_per_seq, s % tiles_per_seq, 0)
    grid_spec = pltpu.PrefetchScalarGridSpec(
        num_scalar_prefetch=1,
        grid=(n_steps,),
        in_specs=[pl.BlockSpec(memory_space=pl.ANY),
                  pl.BlockSpec((1, 4 * TOP_K, m), lambda s, tab: (tile0 + s, 0, 0)),
                  pl.BlockSpec((1, N_EXPERTS, LANES), lambda s, tab: (tile0 + s, 0, 0)),
                  pl.BlockSpec((nb, tr, d), xmap),
                  pl.BlockSpec((nb, 1, d), lambda s, tab: (s // tiles_per_seq, 0, 0))],
        out_specs=pl.BlockSpec((nb, tr, d), xmap),
        scratch_shapes=[pltpu.VMEM((rows, d), BF16), pltpu.SemaphoreType.DMA(())],
    )
    return pl.pallas_call(
        functools.partial(_combine_kernel, n_chunks=n_chunks, tile0=tile0),
        grid_spec=grid_spec,
        out_shape=jax.ShapeDtypeStruct((nbt, t, d), F32),
        compiler_params=_params(("arbitrary",)),
        name="combine",
    )(table, ys, route, toff_b, x1, gf)


def _block_diag(w, groups):
    n, k, _ = w.shape
    w = w.reshape(n // groups, groups, k, k)
    eye = jnp.eye(groups, dtype=w.dtype)
    return jnp.einsum("ngij,gh->ngihj", w, eye).reshape(n // groups, groups * k, groups * k)


def _layer(xp, xs, mod, k_cache, v_cache, conv_state, lru_state, lw):
    (ln_mix, ln_ffn, w_in, q_norm, k_norm, rel_bias, conv_w, conv_b, w_rg, b_rg, w_ig, b_ig, lam,
     w_out, w_router, b_router, w_gu, b_gu, w_dn, b_dn) = lw
    bp, s, d = xp.shape
    bs, ts, _ = xs.shape
    aw = w_out.shape[0] // 2
    nh = aw // HEAD_DIM
    m = TOKEN_TILE
    assert s % m == 0 and bs * ts == m and s % ATTN_Q_TILE == 0 and s % LRU_TILE == 0

    terms = [mod[:, i * d:(i + 1) * d][:, None, :] for i in range(6)]
    tp = [t[:bp] for t in terms]
    tsm = [t[bp:] for t in terms]

    w_in_bf = w_in.astype(BF16)
    w_out_bf = w_out.astype(BF16)
    qn_t = jnp.tile(q_norm, nh).reshape(1, aw)
    kn_t = jnp.tile(k_norm, nh).reshape(1, aw)
    head_mean = _block_diag(jnp.full((nh, HEAD_DIM, HEAD_DIM), 1.0 / HEAD_DIM, F32), nh)[0].astype(BF16)
    groups = MXU_DIM // w_rg.shape[-1]
    wa_bd = _block_diag(w_rg, groups).astype(BF16)
    wx_bd = _block_diag(w_ig, groups).astype(BF16)
    lw_c = b_rg.size
    b_a = b_rg.reshape(1, lw_c)
    b_x = b_ig.reshape(1, lw_c)
    lam2 = lam.reshape(1, lw_c)
    cb2 = conv_b.reshape(1, lw_c)
    ln_mix2 = ln_mix.reshape(1, d)
    ln_ffn2 = ln_ffn.reshape(1, d)
    wr_t = w_router.T
    br = b_router.reshape(-1, 1)
    tab_p = _bias_table(rel_bias, 3 * ATTN_Q_TILE - 1)
    r_cache = k_cache.shape[1]
    tab_s = _bias_table(rel_bias, r_cache + ts - 1)

    qp, kp, vp, xrp, ygp, k32p, v32p = _mixin(xp, tp[0], tp[1], ln_mix2, w_in_bf, qn_t, kn_t, head_mean, 1, m)
    qs, ks, vs, xrs, ygs, k32s, v32s = _mixin(xs, tsm[0], tsm[1], ln_mix2, w_in_bf, qn_t, kn_t, head_mean, bs, ts)
    attn_p = _attn_prompt(qp, kp, vp, tab_p)
    attn_s = _attn_step(qs, ks, vs, k_cache.reshape(bs, r_cache, aw), v_cache.reshape(bs, r_cache, aw), tab_s)

    zeros_pre = jnp.zeros((bp, SUBLANES, lw_c), F32)
    zeros_h = jnp.zeros((bp, 1, lw_c), F32)
    pre_s = jnp.pad(conv_state, ((0, 0), (SUBLANES - (CONV_WIDTH - 1), 0), (0, 0)))
    lru_p, tail_p, hl_p = _lru(xrp, ygp, zeros_pre, zeros_h, conv_w, cb2, wa_bd, wx_bd, b_a, b_x, lam2, 1, LRU_TILE)
    lru_s, tail_s, hl_s = _lru(xrs, ygs, pre_s, lru_state[:, None, :], conv_w, cb2, wa_bd, wx_bd, b_a, b_x, lam2,
                               bs, ts)

    n_tiles = bp * (s // m) + 1
    x1p, h2, route, cnt = _outproj(attn_p, lru_p, xp, tp[2], tp[3], tp[4], ln_ffn2, w_out_bf, wr_t, br,
                                   1, m, n_tiles, 0, None)
    x1s, h2, route, cnt = _outproj(attn_s, lru_s, xs, tsm[2], tsm[3], tsm[4], ln_ffn2, w_out_bf, wr_t, br,
                                   bs, ts, n_tiles, n_tiles - 1, (h2, route, cnt))

    g = BF16_ROWS
    bound = TOP_K * m * n_tiles + n_tiles * N_EXPERTS * (g - 1) + N_EXPERTS * (EXPERT_ROWS - g)
    n_blocks = -(-bound // EXPERT_ROWS)
    table, toff_b, blk_expert, n_used, n_chunks = _route_tables(cnt[:, :, 0].astype(jnp.int32), n_blocks)
    xs_sorted = _dispatch(table, h2, route, toff_b, n_blocks * EXPERT_ROWS, n_chunks)
    ys_sorted = _experts(blk_expert, n_used, xs_sorted, w_gu, b_gu, w_dn, b_dn)
    yp = _combine(table, ys_sorted, route, toff_b, x1p, tp[5], 1, m, 0, n_chunks)
    ysm = _combine(table, ys_sorted, route, toff_b, x1s, tsm[5], bs, ts, n_tiles - 1, n_chunks)

    keep = k32p.shape[1]
    new = (k32p.reshape(bp, keep, nh, HEAD_DIM), v32p.reshape(bp, keep, nh, HEAD_DIM),
           tail_p[:, SUBLANES - (CONV_WIDTH - 1):, :], hl_p[:, 0, :],
           k32s.reshape(bs, ts, nh, HEAD_DIM), v32s.reshape(bs, ts, nh, HEAD_DIM),
           tail_s[:, SUBLANES - (CONV_WIDTH - 1):, :], hl_s[:, 0, :])
    return yp, ysm, new


def kernel(x_prompt, x_sample, c_prompt, c_sample, cache_k, cache_v, state_conv, state_lru, ln_mix_w, ln_ffn_w, w_ada, b_ada, w_in, q_norm_w, k_norm_w, rel_bias, conv_w, conv_b, w_rgate, b_rgate, w_igate, b_igate, lru_lambda, w_out, w_router, b_router, w_gate_up, b_gate_up, w_down, b_down):
    depth = w_in.shape[0]
    yp, ys = x_prompt, x_sample
    c_all = jnp.concatenate([c_prompt, c_sample], axis=0)
    collected = [[] for _ in range(8)]
    for l in range(depth):
        mod = _ada(c_all, w_ada[l], b_ada[l])
        lw = (ln_mix_w[l], ln_ffn_w[l], w_in[l], q_norm_w[l], k_norm_w[l], rel_bias[l], conv_w[l], conv_b[l],
              w_rgate[l], b_rgate[l], w_igate[l], b_igate[l], lru_lambda[l], w_out[l], w_router[l], b_router[l],
              w_gate_up[l], b_gate_up[l], w_down[l], b_down[l])
        yp, ys, new = _layer(yp, ys, mod, cache_k[l], cache_v[l], state_conv[l], state_lru[l], lw)
        for acc, val in zip(collected, new):
            acc.append(val)
    return (yp, ys) + tuple(jnp.stack(vals) for vals in collected)
```

```python
import functools

import jax
import jax.numpy as jnp
from jax import lax
from jax.experimental import pallas as pl
from jax.experimental.pallas import tpu as pltpu

F32 = jnp.float32
BF16 = jnp.bfloat16

CHUNK = 64
N_LEFT_CHUNKS = 8
ATTN_WINDOW = N_LEFT_CHUNKS * CHUNK
HEAD_DIM = 64
REL_CLIP = 128
CONV_WIDTH = 4
LRU_C = 8.0
N_EXPERTS = 32
TOP_K = 4
SWIGLU_LIMIT = 7.0
SWIGLU_ALPHA = 1.702
NORM_EPS = 1e-6
NEG_INF = -1e30

LANES = 128
SUBLANES = 8
BF16_ROWS = 16
MXU_DIM = 256

TOKEN_TILE = 512
ATTN_Q_TILE = 256
LRU_TILE = 256
EXPERT_ROWS = 256
BIAS_TABLE = 1024
VMEM_LIMIT = 56 * 1024 * 1024


def _params(sem, vmem=VMEM_LIMIT):
    return pltpu.CompilerParams(dimension_semantics=sem, vmem_limit_bytes=vmem)


def _ada_kernel(c_ref, w_ref, b_ref, o_ref):
    c = c_ref[...]
    s = (c * jax.nn.sigmoid(c)).astype(BF16)
    o_ref[...] = jnp.dot(s, w_ref[...].astype(BF16), preferred_element_type=F32) + b_ref[...]


def _ada(c_all, w_ada, b_ada):
    n, d = c_all.shape
    nout = w_ada.shape[1]
    tn = 1024
    return pl.pallas_call(
        _ada_kernel,
        grid=(nout // tn,),
        in_specs=[pl.BlockSpec((n, d), lambda j: (0, 0)),
                  pl.BlockSpec((d, tn), lambda j: (0, j)),
                  pl.BlockSpec((1, tn), lambda j: (0, j))],
        out_specs=pl.BlockSpec((n, tn), lambda j: (0, j)),
        out_shape=jax.ShapeDtypeStruct((n, nout), F32),
        compiler_params=_params(("arbitrary",)),
        name="ada",
    )(c_all, w_ada, b_ada.reshape(1, nout))


def _mixin_kernel(x_ref, sh_ref, sc_ref, ln_ref, win_ref, qn_ref, kn_ref, bd_ref,
                  q_ref, k_ref, v_ref, xr_ref, yg_ref, k32_ref, v32_ref):
    nb, tr, d = x_ref.shape
    m = nb * tr
    aw = q_ref.shape[-1]
    x = x_ref[...]
    ms = jnp.mean(x * x, axis=-1, keepdims=True)
    h = x * lax.rsqrt(ms + NORM_EPS) * ln_ref[...]
    h = h * (1.0 + sc_ref[...]) + sh_ref[...]
    hb = h.reshape(m, d).astype(BF16)

    def proj(part):
        return jnp.dot(hb, win_ref[:, part * aw:(part + 1) * aw], preferred_element_type=F32)

    def head_norm(t, w_ref):
        msq = jnp.dot((t * t).astype(BF16), bd_ref[...], preferred_element_type=F32)
        return t * lax.rsqrt(msq + NORM_EPS) * w_ref[...]

    q = head_norm(proj(0), qn_ref)
    k = head_norm(proj(1), kn_ref)
    v = proj(2)
    q_ref[...] = (q * (HEAD_DIM ** -0.5)).astype(BF16).reshape(nb, tr, aw)
    k_ref[...] = k.astype(BF16).reshape(nb, tr, aw)
    v_ref[...] = v.astype(BF16).reshape(nb, tr, aw)
    k32_ref[...] = k.reshape(nb, tr, aw)
    v32_ref[...] = v.reshape(nb, tr, aw)
    xr_ref[...] = proj(3).reshape(nb, tr, aw)
    yg_ref[...] = proj(4).reshape(nb, tr, aw)


def _mixin(x, sh, sc, ln_w, w_in_bf, qn_t, kn_t, bd, nb, tr):
    nbt, t, d = x.shape
    aw = qn_t.shape[-1]
    keep = min(ATTN_WINDOW, t)
    assert tr == keep or t == tr
    grid = (nbt // nb, t // tr)
    xmap = lambda b, i: (b, i, 0)
    mmap = lambda b, i: (b, 0, 0)
    cmap = lambda b, i: (0, 0)
    tmap = lambda b, i: (b, 0, 0)
    big = pl.BlockSpec((nb, tr, aw), xmap)
    tail = pl.BlockSpec((nb, keep, aw), tmap)
    return pl.pallas_call(
        _mixin_kernel,
        grid=grid,
        in_specs=[pl.BlockSpec((nb, tr, d), xmap),
                  pl.BlockSpec((nb, 1, d), mmap), pl.BlockSpec((nb, 1, d), mmap),
                  pl.BlockSpec((1, d), cmap),
                  pl.BlockSpec(w_in_bf.shape, cmap),
                  pl.BlockSpec((1, aw), cmap), pl.BlockSpec((1, aw), cmap),
                  pl.BlockSpec(bd.shape, cmap)],
        out_specs=[big, big, big, big, big, tail, tail],
        out_shape=[jax.ShapeDtypeStruct((nbt, t, aw), BF16)] * 3
        + [jax.ShapeDtypeStruct((nbt, t, aw), F32)] * 2
        + [jax.ShapeDtypeStruct((nbt, keep, aw), F32)] * 2,
        compiler_params=_params(("arbitrary", "arbitrary")),
        name="mixin",
    )(x, sh, sc, ln_w, w_in_bf, qn_t, kn_t, bd)


def _bias_table(rel_bias, off):
    h = rel_bias.shape[0]
    left = off - REL_CLIP
    right = BIAS_TABLE - left - (2 * REL_CLIP + 1)
    assert left >= 0 and right >= 0
    return jnp.concatenate([jnp.broadcast_to(rel_bias[:, :1], (h, left)), rel_bias,
                            jnp.broadcast_to(rel_bias[:, -1:], (h, right))], axis=1)


def _toeplitz(tab_row, rows, cols):
    t = jnp.broadcast_to(tab_row, (rows, BIAS_TABLE))
    t = pltpu.roll(t, BIAS_TABLE - (rows - 1), 1, stride=1, stride_axis=0)
    return t[:, :cols]


def _attn_kernel(q_ref, k0_ref, k1_ref, k2_ref, v0_ref, v1_ref, v2_ref, tab_ref, o_ref, bias_ref):
    b = pl.program_id(0)
    s = pl.program_id(1)
    qt = q_ref.shape[1]
    nk = 3 * qt
    nh = bias_ref.shape[0]

    @pl.when((b == 0) & (s == 0))
    def _():
        qi = lax.broadcasted_iota(jnp.int32, (qt, nk), 0) // CHUNK
        kc = lax.broadcasted_iota(jnp.int32, (qt, nk), 1) // CHUNK
        for h in range(nh):
            band = jnp.where(kc <= qi + N_LEFT_CHUNKS, _toeplitz(tab_ref[h:h + 1, :], qt, nk), NEG_INF)
            bias_ref[h] = jnp.where(kc >= qi, band, NEG_INF)

    q = q_ref[0]
    kcat = jnp.concatenate([k0_ref[0], k1_ref[0], k2_ref[0]], axis=0)
    vcat = jnp.concatenate([v0_ref[0], v1_ref[0], v2_ref[0]], axis=0)
    in_seq = lax.broadcasted_iota(jnp.int32, (qt, nk), 1) >= (2 - s) * qt
    outs = []
    for h in range(nh):
        sl = slice(h * HEAD_DIM, (h + 1) * HEAD_DIM)
        sc = lax.dot_general(q[:, sl], kcat[:, sl], (((1,), (1,)), ((), ())), preferred_element_type=F32)
        sc = jnp.where(in_seq, sc + bias_ref[h], NEG_INF)
        mx = jnp.max(sc, axis=-1, keepdims=True)
        p = jnp.exp(sc - mx)
        l = jnp.sum(p, axis=-1, keepdims=True)
        o = jnp.dot(p.astype(BF16), vcat[:, sl], preferred_element_type=F32)
        outs.append(o / l)
    o_ref[0] = jnp.concatenate(outs, axis=-1).astype(BF16)


def _attn_prompt(q, k, v, tab):
    b, s, aw = q.shape
    qt = ATTN_Q_TILE
    nh = aw // HEAD_DIM
    qspec = pl.BlockSpec((1, qt, aw), lambda i, j: (i, j, 0))

    def kspec(back):
        return pl.BlockSpec((1, qt, aw), lambda i, j: (i, jnp.maximum(j - back, 0), 0))

    return pl.pallas_call(
        _attn_kernel,
        grid=(b, s // qt),
        in_specs=[qspec, kspec(2), kspec(1), kspec(0), kspec(2), kspec(1), kspec(0),
                  pl.BlockSpec(tab.shape, lambda i, j: (0, 0))],
        out_specs=qspec,
        out_shape=jax.ShapeDtypeStruct((b, s, aw), BF16),
        scratch_shapes=[pltpu.VMEM((nh, qt, 3 * qt), F32)],
        compiler_params=_params(("arbitrary", "arbitrary")),
        name="attn_prompt",
    )(q, k, k, k, v, v, v, tab)


def _attn_step_kernel(q_ref, kn_ref, vn_ref, ck_ref, cv_ref, tab_ref, o_ref, bias_ref):
    b = pl.program_id(0)
    t = q_ref.shape[1]
    r = ck_ref.shape[1]
    nk = r + LANES
    nh = bias_ref.shape[0]
    aw = q_ref.shape[-1]

    @pl.when(b == 0)
    def _():
        ok = lax.broadcasted_iota(jnp.int32, (t, nk), 1) < r + t
        for h in range(nh):
            bias_ref[h] = jnp.where(ok, _toeplitz(tab_ref[h:h + 1, :], t, nk), NEG_INF)

    q = q_ref[0]
    pad = jnp.zeros((LANES - t, aw), BF16)
    kcat = jnp.concatenate([ck_ref[0].astype(BF16), kn_ref[0], pad], axis=0)
    vcat = jnp.concatenate([cv_ref[0].astype(BF16), vn_ref[0], pad], axis=0)
    outs = []
    for h in range(nh):
        sl = slice(h * HEAD_DIM, (h + 1) * HEAD_DIM)
        sc = lax.dot_general(q[:, sl], kcat[:, sl], (((1,), (1,)), ((), ())), preferred_element_type=F32)
        sc = sc + bias_ref[h]
        mx = jnp.max(sc, axis=-1, keepdims=True)
        p = jnp.exp(sc - mx)
        l = jnp.sum(p, axis=-1, keepdims=True)
        o = jnp.dot(p.astype(BF16), vcat[:, sl], preferred_element_type=F32)
        outs.append(o / l)
    o_ref[0] = jnp.concatenate(outs, axis=-1).astype(BF16)


def _attn_step(q, kn, vn, ck, cv, tab):
    b, t, aw = q.shape
    r = ck.shape[1]
    nh = aw // HEAD_DIM
    new = pl.BlockSpec((1, t, aw), lambda i: (i, 0, 0))
    old = pl.BlockSpec((1, r, aw), lambda i: (i, 0, 0))
    return pl.pallas_call(
        _attn_step_kernel,
        grid=(b,),
        in_specs=[new, new, new, old, old, pl.BlockSpec(tab.shape, lambda i: (0, 0))],
        out_specs=new,
        out_shape=jax.ShapeDtypeStruct((b, t, aw), BF16),
        scratch_shapes=[pltpu.VMEM((nh, t, r + LANES), F32)],
        compiler_params=_params(("arbitrary",)),
        name="attn_step",
    )(q, kn, vn, ck, cv, tab)


def _gelu_tanh(x):
    return x * (0.5 * (1.0 + jnp.tanh(0.7978845608028654 * (x + 0.044715 * (x * x * x)))))


def _lru_kernel(xr_ref, yg_ref, pre_ref, h0_ref, cw_ref, cb_ref, wa_ref, wx_ref, ba_ref, bx_ref, lam_ref,
                o_ref, tail_ref, hl_ref, cx_ref, ch_ref):
    step = pl.program_id(1)
    nb, tr, c = xr_ref.shape
    m = nb * tr
    half = c // 2

    @pl.when(step == 0)
    def _():
        cx_ref[...] = pre_ref[...]
        ch_ref[...] = h0_ref[...]

    x = xr_ref[...]
    xp = jnp.concatenate([cx_ref[...], x], axis=1)
    new_tail = xp[:, tr:tr + SUBLANES, :]
    xp2 = xp.reshape(nb * (tr + SUBLANES), c)
    y = cb_ref[...] + cw_ref[CONV_WIDTH - 1:CONV_WIDTH, :] * x
    for back in range(1, CONV_WIDTH):
        shifted = pltpu.roll(xp2, back, 0).reshape(nb, tr + SUBLANES, c)[:, SUBLANES:, :]
        y = y + cw_ref[CONV_WIDTH - 1 - back:CONV_WIDTH - back, :] * shifted
    y2 = y.reshape(m, c)
    yb = y2.astype(BF16)

    def gate(w_ref, b_ref):
        g = jnp.concatenate(
            [jnp.dot(yb[:, :half], w_ref[0], preferred_element_type=F32),
             jnp.dot(yb[:, half:], w_ref[1], preferred_element_type=F32)], axis=1)
        return jax.nn.sigmoid(g + b_ref[...])

    rg = gate(wa_ref, ba_ref)
    ig = gate(wx_ref, bx_ref)
    lam = lam_ref[...]
    log_sig = jnp.minimum(lam, 0.0) - jnp.log1p(jnp.exp(-jnp.abs(lam)))
    log_a = LRU_C * rg * log_sig
    a_cum = jnp.exp(log_a)
    b_cum = jnp.sqrt(-jnp.tanh(log_a) * (a_cum * a_cum + 1.0)) * (ig * y2)
    row = lax.broadcasted_iota(jnp.int32, (m, c), 0) % tr
    dist = 1
    while dist < tr:
        keep = row >= dist
        a_sh = jnp.where(keep, pltpu.roll(a_cum, dist, 0), 1.0)
        b_sh = jnp.where(keep, pltpu.roll(b_cum, dist, 0), 0.0)
        b_cum = a_cum * b_sh + b_cum
        a_cum = a_cum * a_sh
        dist *= 2
    h = a_cum.reshape(nb, tr, c) * ch_ref[...] + b_cum.reshape(nb, tr, c)
    o_ref[...] = (h * _gelu_tanh(yg_ref[...])).astype(BF16)
    h_last = h[:, tr - 1:tr, :]
    ch_ref[...] = h_last
    hl_ref[...] = h_last
    cx_ref[...] = new_tail
    tail_ref[...] = new_tail


def _lru(xr, yg, pre, h0, conv_w, conv_b, wa_bd, wx_bd, b_a, b_x, lam, nb, tr):
    nbt, t, c = xr.shape
    xmap = lambda b, i: (b, i, 0)
    smap = lambda b, i: (b, 0, 0)
    c2 = lambda b, i: (0, 0)
    c3 = lambda b, i: (0, 0, 0)
    big = pl.BlockSpec((nb, tr, c), xmap)
    return pl.pallas_call(
        _lru_kernel,
        grid=(nbt // nb, t // tr),
        in_specs=[big, big,
                  pl.BlockSpec((nb, SUBLANES, c), smap), pl.BlockSpec((nb, 1, c), smap),
                  pl.BlockSpec(conv_w.shape, c2), pl.BlockSpec((1, c), c2),
                  pl.BlockSpec(wa_bd.shape, c3), pl.BlockSpec(wx_bd.shape, c3),
                  pl.BlockSpec((1, c), c2), pl.BlockSpec((1, c), c2), pl.BlockSpec((1, c), c2)],
        out_specs=[big, pl.BlockSpec((nb, SUBLANES, c), smap), pl.BlockSpec((nb, 1, c), smap)],
        out_shape=[jax.ShapeDtypeStruct((nbt, t, c), BF16),
                   jax.ShapeDtypeStruct((nbt, SUBLANES, c), F32),
                   jax.ShapeDtypeStruct((nbt, 1, c), F32)],
        scratch_shapes=[pltpu.VMEM((nb, SUBLANES, c), F32), pltpu.VMEM((nb, 1, c), F32)],
        compiler_params=_params(("arbitrary", "arbitrary")),
        name="lru",
    )(xr, yg, pre, h0, conv_w, conv_b, wa_bd, wx_bd, b_a, b_x, lam)


def _outproj_kernel(*refs, aliased):
    (at_ref, lr_ref, x_ref, gm_ref, shf_ref, scf_ref, lnf_ref, wo_ref, wr_ref, br_ref) = refs[:10]
    x1_ref, h2_ref, route_ref, cnt_ref = refs[10 + aliased:]
    nb, tr, d = x_ref.shape
    m = nb * tr
    aw = at_ref.shape[-1]
    ne = wr_ref.shape[0]
    at = at_ref[...].reshape(m, aw)
    lr = lr_ref[...].reshape(m, aw)
    mix = (jnp.dot(at, wo_ref[0:aw, :], preferred_element_type=F32)
           + jnp.dot(lr, wo_ref[aw:2 * aw, :], preferred_element_type=F32))
    x1 = x_ref[...] + gm_ref[...] * mix.reshape(nb, tr, d)
    x1_ref[...] = x1
    ms = jnp.mean(x1 * x1, axis=-1, keepdims=True)
    h2 = x1 * lax.rsqrt(ms + NORM_EPS) * lnf_ref[...]
    h2 = (h2 * (1.0 + scf_ref[...]) + shf_ref[...]).reshape(m, d)
    h2_ref[...] = h2.astype(BF16)

    logits = lax.dot_general(wr_ref[...], h2, (((1,), (1,)), ((), ())),
                             precision=lax.Precision.HIGHEST, preferred_element_type=F32) + br_ref[...]
    e_iota = lax.broadcasted_iota(jnp.int32, (ne, m), 0).astype(F32)
    vals = logits
    top_v, sels = [], []
    for k in range(TOP_K):
        mx = jnp.max(vals, axis=0, keepdims=True)
        idx = jnp.min(jnp.where(vals == mx, e_iota, float(ne)), axis=0, keepdims=True)
        sel = e_iota == idx
        vals = jnp.where(sel, -jnp.inf, vals)
        top_v.append(mx)
        sels.append(sel)
        route_ref[0, k:k + 1, :] = idx
    ex = [jnp.exp(v - top_v[0]) for v in top_v]
    den = ex[0] + ex[1] + ex[2] + ex[3]
    chosen = jnp.zeros((ne, m), F32)
    for k in range(TOP_K):
        route_ref[0, 2 * TOP_K + k:2 * TOP_K + k + 1, :] = ex[k] / den
        chosen = chosen + jnp.where(sels[k], 1.0, 0.0)
    before = (lax.broadcasted_iota(jnp.int32, (m, m), 0) < lax.broadcasted_iota(jnp.int32, (m, m), 1))
    rank = jnp.dot(chosen.astype(BF16), jnp.where(before, 1.0, 0.0).astype(BF16), preferred_element_type=F32)
    for k in range(TOP_K):
        route_ref[0, TOP_K + k:TOP_K + k + 1, :] = jnp.sum(jnp.where(sels[k], rank, 0.0), axis=0, keepdims=True)
    route_ref[0, 3 * TOP_K:4 * TOP_K, :] = jnp.zeros((TOP_K, m), F32)
    cnt_ref[0] = jnp.broadcast_to(jnp.sum(chosen, axis=1, keepdims=True), (ne, LANES))


def _outproj(attn, lru_o, x, gm, shf, scf, lnf, w_out_bf, wr_t, br, nb, tr, n_tiles, tile0, prev):
    nbt, t, d = x.shape
    aw = attn.shape[-1]
    m = nb * tr
    assert m == TOKEN_TILE
    ne = wr_t.shape[0]
    tiles_per_seq = t // tr
    xmap = lambda b, i: (b, i, 0)
    mmap = lambda b, i: (b, 0, 0)
    c2 = lambda b, i: (0, 0)
    tile = lambda b, i: (tile0 + b * tiles_per_seq + i, 0)
    tile3 = lambda b, i: (tile0 + b * tiles_per_seq + i, 0, 0)
    mod = pl.BlockSpec((nb, 1, d), mmap)
    in_specs = [pl.BlockSpec((nb, tr, aw), xmap), pl.BlockSpec((nb, tr, aw), xmap),
                pl.BlockSpec((nb, tr, d), xmap), mod, mod, mod,
                pl.BlockSpec((1, d), c2), pl.BlockSpec(w_out_bf.shape, c2),
                pl.BlockSpec(wr_t.shape, c2), pl.BlockSpec((ne, 1), c2)]
    args = [attn, lru_o, x, gm, shf, scf, lnf, w_out_bf, wr_t, br]
    aliases = {}
    if prev is not None:
        in_specs += [pl.BlockSpec(memory_space=pl.ANY)] * 3
        args += list(prev)
        aliases = {10: 1, 11: 2, 12: 3}
    return pl.pallas_call(
        functools.partial(_outproj_kernel, aliased=len(aliases)),
        grid=(nbt // nb, tiles_per_seq),
        in_specs=in_specs,
        out_specs=[pl.BlockSpec((nb, tr, d), xmap), pl.BlockSpec((m, d), tile),
                   pl.BlockSpec((1, 4 * TOP_K, m), tile3), pl.BlockSpec((1, ne, LANES), tile3)],
        out_shape=[jax.ShapeDtypeStruct((nbt, t, d), F32),
                   jax.ShapeDtypeStruct((n_tiles * m, d), BF16),
                   jax.ShapeDtypeStruct((n_tiles, 4 * TOP_K, m), F32),
                   jax.ShapeDtypeStruct((n_tiles, ne, LANES), F32)],
        input_output_aliases=aliases,
        compiler_params=_params(("arbitrary", "arbitrary")),
        name="outproj",
    )(*args)


def _tile_rows(m):
    cap = TOP_K * m + N_EXPERTS * (BF16_ROWS - 1) + BF16_ROWS
    return -(-cap // TOKEN_TILE) * TOKEN_TILE


def _route_tables(cnt, n_blocks):
    nt = cnt.shape[0]
    g = BF16_ROWS
    bm = EXPERT_ROWS
    n_chunks = _tile_rows(TOKEN_TILE) // g
    n_gap = -(-(N_EXPERTS * (bm // g - 1)) // nt)
    e_ids = jnp.arange(N_EXPERTS, dtype=jnp.int32)
    t_ids = jnp.arange(nt, dtype=jnp.int32)
    upto = (e_ids[:, None] <= e_ids[None, :]).astype(jnp.int32)
    pc = (cnt + g - 1) // g * g
    ctile = jnp.sum(pc[:, :, None] * upto[None], axis=1)
    toff = ctile - pc
    trow = ctile[:, -1]
    tot = jnp.sum(pc, axis=0)
    reg = (tot + bm - 1) // bm * bm
    creg = jnp.sum(reg[:, None] * upto, axis=0)
    base = creg - reg
    earlier = (t_ids[:, None] < t_ids[None, :]).astype(jnp.int32)
    goff = base[None, :] + jnp.sum(pc[:, None, :] * earlier[:, :, None], axis=0)
    n_used = creg[-1] // bm
    blk_row = jnp.minimum(jnp.arange(n_blocks, dtype=jnp.int32), n_used - 1) * bm
    blk_expert = jnp.minimum(jnp.sum((creg[None, :] <= blk_row[:, None]).astype(jnp.int32), axis=1),
                             N_EXPERTS - 1)
    r = jnp.arange(n_chunks, dtype=jnp.int32) * g
    r3 = r[None, :, None]
    in_seg = (toff[:, None, :] <= r3) & (r3 < ctile[:, None, :])
    dst = jnp.sum(jnp.where(in_seg, (goff - toff)[:, None, :], 0), axis=2) + r[None, :]
    dst = jnp.where(r[None, :] < trow[:, None], dst, -1)
    gcnt = (reg - tot) // g
    gcum = jnp.sum(gcnt[:, None] * upto, axis=0)
    gstart = gcum - gcnt
    s = jnp.arange(nt * n_gap, dtype=jnp.int32)
    in_gap = (gstart[None, :] <= s[:, None]) & (s[:, None] < gcum[None, :])
    gdst = jnp.sum(jnp.where(in_gap, (base + tot - g * gstart)[None, :] + g * s[:, None], 0), axis=1)
    gdst = jnp.where(s < gcum[-1], gdst, -1).reshape(nt, n_gap)
    table = jnp.concatenate([dst, gdst], axis=1).astype(jnp.int32)
    toff_b = jnp.broadcast_to(toff.astype(F32)[:, :, None], (nt, N_EXPERTS, LANES))
    return table, toff_b, blk_expert, n_used.astype(jnp.int32).reshape(1), n_chunks


def _slot_rows(route_ref, toff_ref, m):
    ne = toff_ref.shape[1]
    e_iota = lax.broadcasted_iota(jnp.int32, (ne, m), 0).astype(F32)
    toff_col = toff_ref[0][:, 0:1]
    pos = []
    for k in range(TOP_K):
        sel = e_iota == route_ref[0, k:k + 1, :]
        start = jnp.sum(jnp.where(sel, toff_col, 0.0), axis=0, keepdims=True)
        pos.append(start + route_ref[0, TOP_K + k:TOP_K + k + 1, :])
    return pos


def _dispatch_kernel(tab_ref, h2_ref, route_ref, toff_ref, xs_hbm, buf_ref, sem, *, n_chunks):
    t = pl.program_id(0)
    m = h2_ref.shape[0]
    rows = buf_ref.shape[0]
    n_entries = tab_ref.shape[1]
    pos = _slot_rows(route_ref, toff_ref, m)
    h2 = h2_ref[...]
    for rc in range(rows // m):
        r_iota = (lax.broadcasted_iota(jnp.int32, (m, m), 0) + rc * m).astype(F32)
        onehot = jnp.zeros((m, m), F32)
        for k in range(TOP_K):
            onehot = jnp.where(r_iota == pos[k], 1.0, onehot)
        buf_ref[rc * m:(rc + 1) * m, :] = jnp.dot(onehot.astype(BF16), h2, preferred_element_type=F32).astype(BF16)

    def copy(c):
        src = jnp.where(c < n_chunks, c * BF16_ROWS, rows - BF16_ROWS)
        return pltpu.make_async_copy(
            buf_ref.at[pl.ds(pl.multiple_of(src, BF16_ROWS), BF16_ROWS)],
            xs_hbm.at[pl.ds(pl.multiple_of(tab_ref[t, c], BF16_ROWS), BF16_ROWS)], sem)

    def start(c, carry):
        @pl.when(tab_ref[t, c] >= 0)
        def _():
            copy(c).start()
        return carry

    def wait(c, carry):
        @pl.when(tab_ref[t, c] >= 0)
        def _():
            copy(c).wait()
        return carry

    lax.fori_loop(0, n_entries, start, 0)
    lax.fori_loop(0, n_entries, wait, 0)


def _dispatch(table, h2, route, toff_b, n_rows, n_chunks):
    nt = route.shape[0]
    m = TOKEN_TILE
    d = h2.shape[1]
    rows = _tile_rows(m)
    grid_spec = pltpu.PrefetchScalarGridSpec(
        num_scalar_prefetch=1,
        grid=(nt,),
        in_specs=[pl.BlockSpec((m, d), lambda t, tab: (t, 0)),
                  pl.BlockSpec((1, 4 * TOP_K, m), lambda t, tab: (t, 0, 0)),
                  pl.BlockSpec((1, N_EXPERTS, LANES), lambda t, tab: (t, 0, 0))],
        out_specs=pl.BlockSpec(memory_space=pl.ANY),
        scratch_shapes=[pltpu.VMEM((rows, d), BF16), pltpu.SemaphoreType.DMA(())],
    )
    return pl.pallas_call(
        functools.partial(_dispatch_kernel, n_chunks=n_chunks),
        grid_spec=grid_spec,
        out_shape=jax.ShapeDtypeStruct((n_rows, d), BF16),
        compiler_params=_params(("arbitrary",)),
        name="dispatch",
    )(table, h2, route, toff_b)


def _expert_kernel(be_ref, nu_ref, xs_ref, wgu_ref, bgu_ref, wdn_ref, bdn_ref, ys_ref, wgu_bf, wdn_bf):
    w = pl.program_id(0)
    dff = wdn_ref.shape[1]
    prev = be_ref[jnp.maximum(w - 1, 0)]

    @pl.when((w == 0) | (be_ref[w] != prev))
    def _():
        wgu_bf[...] = wgu_ref[0].astype(BF16)
        wdn_bf[...] = wdn_ref[0].astype(BF16)

    @pl.when(w < nu_ref[0])
    def _():
        gu = jnp.dot(xs_ref[...], wgu_bf[...], preferred_element_type=F32) + bgu_ref[0]
        gate = jnp.minimum(gu[:, :dff], SWIGLU_LIMIT)
        up = jnp.clip(gu[:, dff:], -SWIGLU_LIMIT, SWIGLU_LIMIT)
        glu = gate * jax.nn.sigmoid(gate * SWIGLU_ALPHA)
        act = ((up + 1.0) * glu).astype(BF16)
        ys_ref[...] = (jnp.dot(act, wdn_bf[...], preferred_element_type=F32) + bdn_ref[0]).astype(BF16)


def _experts(blk_expert, n_used, xs, w_gu, b_gu, w_dn, b_dn):
    n_rows, d = xs.shape
    bm = EXPERT_ROWS
    ne, _, dff2 = w_gu.shape
    dff = w_dn.shape[1]
    row = lambda w, be, nu: (jnp.minimum(w, nu[0] - 1), 0)
    exp3 = lambda w, be, nu: (be[w], 0, 0)
    grid_spec = pltpu.PrefetchScalarGridSpec(
        num_scalar_prefetch=2,
        grid=(n_rows // bm,),
        in_specs=[pl.BlockSpec((bm, d), row),
                  pl.BlockSpec((1, d, dff2), exp3), pl.BlockSpec((1, 1, dff2), exp3),
                  pl.BlockSpec((1, dff, d), exp3), pl.BlockSpec((1, 1, d), exp3)],
        out_specs=pl.BlockSpec((bm, d), row),
        scratch_shapes=[pltpu.VMEM((d, dff2), BF16), pltpu.VMEM((dff, d), BF16)],
    )
    return pl.pallas_call(
        _expert_kernel,
        grid_spec=grid_spec,
        out_shape=jax.ShapeDtypeStruct((n_rows, d), BF16),
        compiler_params=_params(("arbitrary",)),
        name="experts",
    )(blk_expert, n_used, xs, w_gu, b_gu.reshape(ne, 1, dff2), w_dn, b_dn.reshape(ne, 1, d))


def _combine_kernel(tab_ref, ys_hbm, route_ref, toff_ref, x1_ref, gf_ref, o_ref, buf_ref, sem, *, n_chunks, tile0):
    step = pl.program_id(0)
    t = tile0 + step
    nb, tr, d = x1_ref.shape
    m = nb * tr
    rows = buf_ref.shape[0]

    @pl.when(step == 0)
    def _():
        buf_ref[...] = jnp.zeros(buf_ref.shape, BF16)

    def copy(c):
        return pltpu.make_async_copy(
            ys_hbm.at[pl.ds(pl.multiple_of(tab_ref[t, c], BF16_ROWS), BF16_ROWS)],
            buf_ref.at[pl.ds(pl.multiple_of(c * BF16_ROWS, BF16_ROWS), BF16_ROWS)], sem)

    def start(c, carry):
        @pl.when(tab_ref[t, c] >= 0)
        def _():
            copy(c).start()
        return carry

    def wait(c, carry):
        @pl.when(tab_ref[t, c] >= 0)
        def _():
            copy(c).wait()
        return carry

    lax.fori_loop(0, n_chunks, start, 0)

    pos = _slot_rows(route_ref, toff_ref, m)
    gates = [route_ref[0, 2 * TOP_K + k:2 * TOP_K + k + 1, :] for k in range(TOP_K)]
    stacked = jnp.concatenate(pos + gates + [jnp.zeros((LANES - 2 * TOP_K, m), F32)], axis=0)
    cols = stacked.T
    lax.fori_loop(0, n_chunks, wait, 0)

    acc = jnp.zeros((m, d), F32)
    for rc in range(rows // m):
        c_iota = (lax.broadcasted_iota(jnp.int32, (m, m), 1) + rc * m).astype(F32)
        weights = jnp.zeros((m, m), F32)
        for k in range(TOP_K):
            weights = jnp.where(c_iota == cols[:, k:k + 1], cols[:, TOP_K + k:TOP_K + k + 1], weights)
        acc = acc + jnp.dot(weights.astype(BF16), buf_ref[rc * m:(rc + 1) * m, :], preferred_element_type=F32)
    o_ref[...] = x1_ref[...] + gf_ref[...] * acc.reshape(nb, tr, d)


def _combine(table, ys, route, toff_b, x1, gf, nb, tr, tile0, n_chunks):
    nbt, t, d = x1.shape
    m = nb * tr
    assert m == TOKEN_TILE
    rows = _tile_rows(m)
    tiles_per_seq = t // tr
    n_steps = (nbt // nb) * tiles_per_seq
    xmap = lambda s, tab: (s // tiles_per_seq, s % tiles_per_seq, 0)
    grid_spec = pltpu.PrefetchScalarGridSpec(
        num_scalar_prefetch=1,
        grid=(n_steps,),
        in_specs=[pl.BlockSpec(memory_space=pl.ANY),
                  pl.BlockSpec((1, 4 * TOP_K, m), lambda s, tab: (tile0 + s, 0, 0)),
                  pl.BlockSpec((1, N_EXPERTS, LANES), lambda s, tab: (tile0 + s, 0, 0)),
                  pl.BlockSpec((nb, tr, d), xmap),
                  pl.BlockSpec((nb, 1, d), lambda s, tab: (s // tiles_per_seq, 0, 0))],
        out_specs=pl.BlockSpec((nb, tr, d), xmap),
        scratch_shapes=[pltpu.VMEM((rows, d), BF16), pltpu.SemaphoreType.DMA(())],
    )
    return pl.pallas_call(
        functools.partial(_combine_kernel, n_chunks=n_chunks, tile0=tile0),
        grid_spec=grid_spec,
        out_shape=jax.ShapeDtypeStruct((nbt, t, d), F32),
        compiler_params=_params(("arbitrary",)),
        name="combine",
    )(table, ys, route, toff_b, x1, gf)


def _block_diag(w, groups):
    n, k, _ = w.shape
    w = w.reshape(n // groups, groups, k, k)
    eye = jnp.eye(groups, dtype=w.dtype)
    return jnp.einsum("ngij,gh->ngihj", w, eye).reshape(n // groups, groups * k, groups * k)


def _layer(xp, xs, mod, k_cache, v_cache, conv_state, lru_state, lw):
    (ln_mix, ln_ffn, w_in, q_norm, k_norm, rel_bias, conv_w, conv_b, w_rg, b_rg, w_ig, b_ig, lam,
     w_out, w_router, b_router, w_gu, b_gu, w_dn, b_dn) = lw
    bp, s, d = xp.shape
    bs, ts, _ = xs.shape
    aw = w_out.shape[0] // 2
    nh = aw // HEAD_DIM
    m = TOKEN_TILE
    assert s % m == 0 and bs * ts == m and s % ATTN_Q_TILE == 0 and s % LRU_TILE == 0

    terms = [mod[:, i * d:(i + 1) * d][:, None, :] for i in range(6)]
    tp = [t[:bp] for t in terms]
    tsm = [t[bp:] for t in terms]

    w_in_bf = w_in.astype(BF16)
    w_out_bf = w_out.astype(BF16)
    qn_t = jnp.tile(q_norm, nh).reshape(1, aw)
    kn_t = jnp.tile(k_norm, nh).reshape(1, aw)
    head_mean = _block_diag(jnp.full((nh, HEAD_DIM, HEAD_DIM), 1.0 / HEAD_DIM, F32), nh)[0].astype(BF16)
    groups = MXU_DIM // w_rg.shape[-1]
    wa_bd = _block_diag(w_rg, groups).astype(BF16)
    wx_bd = _block_diag(w_ig, groups).astype(BF16)
    lw_c = b_rg.size
    b_a = b_rg.reshape(1, lw_c)
    b_x = b_ig.reshape(1, lw_c)
    lam2 = lam.reshape(1, lw_c)
    cb2 = conv_b.reshape(1, lw_c)
    ln_mix2 = ln_mix.reshape(1, d)
    ln_ffn2 = ln_ffn.reshape(1, d)
    wr_t = w_router.T
    br = b_router.reshape(-1, 1)
    tab_p = _bias_table(rel_bias, 3 * ATTN_Q_TILE - 1)
    r_cache = k_cache.shape[1]
    tab_s = _bias_table(rel_bias, r_cache + ts - 1)

    qp, kp, vp, xrp, ygp, k32p, v32p = _mixin(xp, tp[0], tp[1], ln_mix2, w_in_bf, qn_t, kn_t, head_mean, 1, m)
    qs, ks, vs, xrs, ygs, k32s, v32s = _mixin(xs, tsm[0], tsm[1], ln_mix2, w_in_bf, qn_t, kn_t, head_mean, bs, ts)
    attn_p = _attn_prompt(qp, kp, vp, tab_p)
    attn_s = _attn_step(qs, ks, vs, k_cache.reshape(bs, r_cache, aw), v_cache.reshape(bs, r_cache, aw), tab_s)

    zeros_pre = jnp.zeros((bp, SUBLANES, lw_c), F32)
    zeros_h = jnp.zeros((bp, 1, lw_c), F32)
    pre_s = jnp.pad(conv_state, ((0, 0), (SUBLANES - (CONV_WIDTH - 1), 0), (0, 0)))
    lru_p, tail_p, hl_p = _lru(xrp, ygp, zeros_pre, zeros_h, conv_w, cb2, wa_bd, wx_bd, b_a, b_x, lam2, 1, LRU_TILE)
    lru_s, tail_s, hl_s = _lru(xrs, ygs, pre_s, lru_state[:, None, :], conv_w, cb2, wa_bd, wx_bd, b_a, b_x, lam2,
                               bs, ts)

    n_tiles = bp * (s // m) + 1
    x1p, h2, route, cnt = _outproj(attn_p, lru_p, xp, tp[2], tp[3], tp[4], ln_ffn2, w_out_bf, wr_t, br,
                                   1, m, n_tiles, 0, None)
    x1s, h2, route, cnt = _outproj(attn_s, lru_s, xs, tsm[2], tsm[3], tsm[4], ln_ffn2, w_out_bf, wr_t, br,
                                   bs, ts, n_tiles, n_tiles - 1, (h2, route, cnt))

    g = BF16_ROWS
    bound = TOP_K * m * n_tiles + n_tiles * N_EXPERTS * (g - 1) + N_EXPERTS * (EXPERT_ROWS - g)
    n_blocks = -(-bound // EXPERT_ROWS)
    table, toff_b, blk_expert, n_used, n_chunks = _route_tables(cnt[:, :, 0].astype(jnp.int32), n_blocks)
    xs_sorted = _dispatch(table, h2, route, toff_b, n_blocks * EXPERT_ROWS, n_chunks)
    ys_sorted = _experts(blk_expert, n_used, xs_sorted, w_gu, b_gu, w_dn, b_dn)
    yp = _combine(table, ys_sorted, route, toff_b, x1p, tp[5], 1, m, 0, n_chunks)
    ysm = _combine(table, ys_sorted, route, toff_b, x1s, tsm[5], bs, ts, n_tiles - 1, n_chunks)

    keep = k32p.shape[1]
    new = (k32p.reshape(bp, keep, nh, HEAD_DIM), v32p.reshape(bp, keep, nh, HEAD_DIM),
           tail_p[:, SUBLANES - (CONV_WIDTH - 1):, :], hl_p[:, 0, :],
           k32s.reshape(bs, ts, nh, HEAD_DIM), v32s.reshape(bs, ts, nh, HEAD_DIM),
           tail_s[:, SUBLANES - (CONV_WIDTH - 1):, :], hl_s[:, 0, :])
    return yp, ysm, new


def kernel(x_prompt, x_sample, c_prompt, c_sample, cache_k, cache_v, state_conv, state_lru, ln_mix_w, ln_ffn_w, w_ada, b_ada, w_in, q_norm_w, k_norm_w, rel_bias, conv_w, conv_b, w_rgate, b_rgate, w_igate, b_igate, lru_lambda, w_out, w_router, b_router, w_gate_up, b_gate_up, w_down, b_down):
    depth = w_in.shape[0]
    yp, ys = x_prompt, x_sample
    c_all = jnp.concatenate([c_prompt, c_sample], axis=0)
    collected = [[] for _ in range(8)]
    for l in range(depth):
        mod = _ada(c_all, w_ada[l], b_ada[l])
        lw = (ln_mix_w[l], ln_ffn_w[l], w_in[l], q_norm_w[l], k_norm_w[l], rel_bias[l], conv_w[l], conv_b[l],
              w_rgate[l], b_rgate[l], w_igate[l], b_igate[l], lru_lambda[l], w_out[l], w_router[l], b_router[l],
              w_gate_up[l], b_gate_up[l], w_down[l], b_down[l])
        yp, ys, new = _layer(yp, ys, mod, cache_k[l], cache_v[l], state_conv[l], state_lru[l], lw)
        for acc, val in zip(collected, new):
            acc.append(val)
    return (yp, ys) + tuple(jnp.stack(vals) for vals in collected)
```

```python
import functools

import jax
import jax.numpy as jnp
from jax import lax
from jax.experimental import pallas as pl
from jax.experimental.pallas import tpu as pltpu

F32 = jnp.float32
BF16 = jnp.bfloat16

CHUNK = 64
N_LEFT_CHUNKS = 8
ATTN_WINDOW = N_LEFT_CHUNKS * CHUNK
HEAD_DIM = 64
REL_CLIP = 128
CONV_WIDTH = 4
LRU_C = 8.0
N_EXPERTS = 32
TOP_K = 4
SWIGLU_LIMIT = 7.0
SWIGLU_ALPHA = 1.702
NORM_EPS = 1e-6
NEG_INF = -1e30

LANES = 128
SUBLANES = 8
BF16_ROWS = 16
MXU_DIM = 256

TOKEN_TILE = 512
ATTN_Q_TILE = 256
LRU_TILE = 256
EXPERT_ROWS = 256
BIAS_TABLE = 1024
VMEM_LIMIT = 56 * 1024 * 1024


def _params(sem, vmem=VMEM_LIMIT):
    return pltpu.CompilerParams(dimension_semantics=sem, vmem_limit_bytes=vmem)


def _ada_kernel(c_ref, w_ref, b_ref, o_ref):
    c = c_ref[...]
    s = (c * jax.nn.sigmoid(c)).astype(BF16)
    o_ref[...] = jnp.dot(s, w_ref[...].astype(BF16), preferred_element_type=F32) + b_ref[...]


def _ada(c_all, w_ada, b_ada):
    n, d = c_all.shape
    nout = w_ada.shape[1]
    tn = 1024
    return pl.pallas_call(
        _ada_kernel,
        grid=(nout // tn,),
        in_specs=[pl.BlockSpec((n, d), lambda j: (0, 0)),
                  pl.BlockSpec((d, tn), lambda j: (0, j)),
                  pl.BlockSpec((1, tn), lambda j: (0, j))],
        out_specs=pl.BlockSpec((n, tn), lambda j: (0, j)),
        out_shape=jax.ShapeDtypeStruct((n, nout), F32),
        compiler_params=_params(("arbitrary",)),
        name="ada",
    )(c_all, w_ada, b_ada.reshape(1, nout))


def _mixin_kernel(x_ref, sh_ref, sc_ref, ln_ref, win_ref, qn_ref, kn_ref, bd_ref,
                  q_ref, k_ref, v_ref, xr_ref, yg_ref, k32_ref, v32_ref):
    nb, tr, d = x_ref.shape
    m = nb * tr
    aw = q_ref.shape[-1]
    x = x_ref[...]
    ms = jnp.mean(x * x, axis=-1, keepdims=True)
    h = x * lax.rsqrt(ms + NORM_EPS) * ln_ref[...]
    h = h * (1.0 + sc_ref[...]) + sh_ref[...]
    hb = h.reshape(m, d).astype(BF16)

    def proj(part):
        return jnp.dot(hb, win_ref[:, part * aw:(part + 1) * aw], preferred_element_type=F32)

    def head_norm(t, w_ref):
        msq = jnp.dot((t * t).astype(BF16), bd_ref[...], preferred_element_type=F32)
        return t * lax.rsqrt(msq + NORM_EPS) * w_ref[...]

    q = head_norm(proj(0), qn_ref)
    k = head_norm(proj(1), kn_ref)
    v = proj(2)
    q_ref[...] = (q * (HEAD_DIM ** -0.5)).astype(BF16).reshape(nb, tr, aw)
    k_ref[...] = k.astype(BF16).reshape(nb, tr, aw)
    v_ref[...] = v.astype(BF16).reshape(nb, tr, aw)
    k32_ref[...] = k.reshape(nb, tr, aw)
    v32_ref[...] = v.reshape(nb, tr, aw)
    xr_ref[...] = proj(3).reshape(nb, tr, aw)
    yg_ref[...] = proj(4).reshape(nb, tr, aw)


def _mixin(x, sh, sc, ln_w, w_in_bf, qn_t, kn_t, bd, nb, tr):
    nbt, t, d = x.shape
    aw = qn_t.shape[-1]
    keep = min(ATTN_WINDOW, t)
    assert tr == keep or t == tr
    grid = (nbt // nb, t // tr)
    xmap = lambda b, i: (b, i, 0)
    mmap = lambda b, i: (b, 0, 0)
    cmap = lambda b, i: (0, 0)
    tmap = lambda b, i: (b, 0, 0)
    big = pl.BlockSpec((nb, tr, aw), xmap)
    tail = pl.BlockSpec((nb, keep, aw), tmap)
    return pl.pallas_call(
        _mixin_kernel,
        grid=grid,
        in_specs=[pl.BlockSpec((nb, tr, d), xmap),
                  pl.BlockSpec((nb, 1, d), mmap), pl.BlockSpec((nb, 1, d), mmap),
                  pl.BlockSpec((1, d), cmap),
                  pl.BlockSpec(w_in_bf.shape, cmap),
                  pl.BlockSpec((1, aw), cmap), pl.BlockSpec((1, aw), cmap),
                  pl.BlockSpec(bd.shape, cmap)],
        out_specs=[big, big, big, big, big, tail, tail],
        out_shape=[jax.ShapeDtypeStruct((nbt, t, aw), BF16)] * 3
        + [jax.ShapeDtypeStruct((nbt, t, aw), F32)] * 2
        + [jax.ShapeDtypeStruct((nbt, keep, aw), F32)] * 2,
        compiler_params=_params(("arbitrary", "arbitrary")),
        name="mixin",
    )(x, sh, sc, ln_w, w_in_bf, qn_t, kn_t, bd)


def _bias_table(rel_bias, off):
    h = rel_bias.shape[0]
    left = off - REL_CLIP
    right = BIAS_TABLE - left - (2 * REL_CLIP + 1)
    assert left >= 0 and right >= 0
    return jnp.concatenate([jnp.broadcast_to(rel_bias[:, :1], (h, left)), rel_bias,
                            jnp.broadcast_to(rel_bias[:, -1:], (h, right))], axis=1)


def _toeplitz(tab_row, rows, cols):
    t = jnp.broadcast_to(tab_row, (rows, BIAS_TABLE))
    t = pltpu.roll(t, BIAS_TABLE - (rows - 1), 1, stride=1, stride_axis=0)
    return t[:, :cols]


def _attn_kernel(q_ref, k0_ref, k1_ref, k2_ref, v0_ref, v1_ref, v2_ref, tab_ref, o_ref, bias_ref):
    b = pl.program_id(0)
    s = pl.program_id(1)
    qt = q_ref.shape[1]
    nk = 3 * qt
    nh = bias_ref.shape[0]

    @pl.when((b == 0) & (s == 0))
    def _():
        qi = lax.broadcasted_iota(jnp.int32, (qt, nk), 0) // CHUNK
        kc = lax.broadcasted_iota(jnp.int32, (qt, nk), 1) // CHUNK
        for h in range(nh):
            band = jnp.where(kc <= qi + N_LEFT_CHUNKS, _toeplitz(tab_ref[h:h + 1, :], qt, nk), NEG_INF)
            bias_ref[h] = jnp.where(kc >= qi, band, NEG_INF)

    q = q_ref[0]
    kcat = jnp.concatenate([k0_ref[0], k1_ref[0], k2_ref[0]], axis=0)
    vcat = jnp.concatenate([v0_ref[0], v1_ref[0], v2_ref[0]], axis=0)
    in_seq = lax.broadcasted_iota(jnp.int32, (qt, nk), 1) >= (2 - s) * qt
    outs = []
    for h in range(nh):
        sl = slice(h * HEAD_DIM, (h + 1) * HEAD_DIM)
        sc = lax.dot_general(q[:, sl], kcat[:, sl], (((1,), (1,)), ((), ())), preferred_element_type=F32)
        sc = jnp.where(in_seq, sc + bias_ref[h], NEG_INF)
        mx = jnp.max(sc, axis=-1, keepdims=True)
        p = jnp.exp(sc - mx)
        l = jnp.sum(p, axis=-1, keepdims=True)
        o = jnp.dot(p.astype(BF16), vcat[:, sl], preferred_element_type=F32)
        outs.append(o / l)
    o_ref[0] = jnp.concatenate(outs, axis=-1).astype(BF16)


def _attn_prompt(q, k, v, tab):
    b, s, aw = q.shape
    qt = ATTN_Q_TILE
    nh = aw // HEAD_DIM
    qspec = pl.BlockSpec((1, qt, aw), lambda i, j: (i, j, 0))

    def kspec(back):
        return pl.BlockSpec((1, qt, aw), lambda i, j: (i, jnp.maximum(j - back, 0), 0))

    return pl.pallas_call(
        _attn_kernel,
        grid=(b, s // qt),
        in_specs=[qspec, kspec(2), kspec(1), kspec(0), kspec(2), kspec(1), kspec(0),
                  pl.BlockSpec(tab.shape, lambda i, j: (0, 0))],
        out_specs=qspec,
        out_shape=jax.ShapeDtypeStruct((b, s, aw), BF16),
        scratch_shapes=[pltpu.VMEM((nh, qt, 3 * qt), F32)],
        compiler_params=_params(("arbitrary", "arbitrary")),
        name="attn_prompt",
    )(q, k, k, k, v, v, v, tab)


def _attn_step_kernel(q_ref, kn_ref, vn_ref, ck_ref, cv_ref, tab_ref, o_ref, bias_ref):
    b = pl.program_id(0)
    t = q_ref.shape[1]
    r = ck_ref.shape[1]
    nk = r + LANES
    nh = bias_ref.shape[0]
    aw = q_ref.shape[-1]

    @pl.when(b == 0)
    def _():
        ok = lax.broadcasted_iota(jnp.int32, (t, nk), 1) < r + t
        for h in range(nh):
            bias_ref[h] = jnp.where(ok, _toeplitz(tab_ref[h:h + 1, :], t, nk), NEG_INF)

    q = q_ref[0]
    pad = jnp.zeros((LANES - t, aw), BF16)
    kcat = jnp.concatenate([ck_ref[0].astype(BF16), kn_ref[0], pad], axis=0)
    vcat = jnp.concatenate([cv_ref[0].astype(BF16), vn_ref[0], pad], axis=0)
    outs = []
    for h in range(nh):
        sl = slice(h * HEAD_DIM, (h + 1) * HEAD_DIM)
        sc = lax.dot_general(q[:, sl], kcat[:, sl], (((1,), (1,)), ((), ())), preferred_element_type=F32)
        sc = sc + bias_ref[h]
        mx = jnp.max(sc, axis=-1, keepdims=True)
        p = jnp.exp(sc - mx)
        l = jnp.sum(p, axis=-1, keepdims=True)
        o = jnp.dot(p.astype(BF16), vcat[:, sl], preferred_element_type=F32)
        outs.append(o / l)
    o_ref[0] = jnp.concatenate(outs, axis=-1).astype(BF16)


def _attn_step(q, kn, vn, ck, cv, tab):
    b, t, aw = q.shape
    r = ck.shape[1]
    nh = aw // HEAD_DIM
    new = pl.BlockSpec((1, t, aw), lambda i: (i, 0, 0))
    old = pl.BlockSpec((1, r, aw), lambda i: (i, 0, 0))
    return pl.pallas_call(
        _attn_step_kernel,
        grid=(b,),
        in_specs=[new, new, new, old, old, pl.BlockSpec(tab.shape, lambda i: (0, 0))],
        out_specs=new,
        out_shape=jax.ShapeDtypeStruct((b, t, aw), BF16),
        scratch_shapes=[pltpu.VMEM((nh, t, r + LANES), F32)],
        compiler_params=_params(("arbitrary",)),
        name="attn_step",
    )(q, kn, vn, ck, cv, tab)


def _gelu_tanh(x):
    return x * (0.5 * (1.0 + jnp.tanh(0.7978845608028654 * (x + 0.044715 * (x * x * x)))))


def _lru_kernel(xr_ref, yg_ref, pre_ref, h0_ref, cw_ref, cb_ref, wa_ref, wx_ref, ba_ref, bx_ref, lam_ref,
                o_ref, tail_ref, hl_ref, cx_ref, ch_ref):
    step = pl.program_id(1)
    nb, tr, c = xr_ref.shape
    m = nb * tr
    half = c // 2

    @pl.when(step == 0)
    def _():
        cx_ref[...] = pre_ref[...]
        ch_ref[...] = h0_ref[...]

    x = xr_ref[...]
    xp = jnp.concatenate([cx_ref[...], x], axis=1)
    new_tail = xp[:, tr:tr + SUBLANES, :]
    xp2 = xp.reshape(nb * (tr + SUBLANES), c)
    y = cb_ref[...] + cw_ref[CONV_WIDTH - 1:CONV_WIDTH, :] * x
    for back in range(1, CONV_WIDTH):
        shifted = pltpu.roll(xp2, back, 0).reshape(nb, tr + SUBLANES, c)[:, SUBLANES:, :]
        y = y + cw_ref[CONV_WIDTH - 1 - back:CONV_WIDTH - back, :] * shifted
    y2 = y.reshape(m, c)
    yb = y2.astype(BF16)

    def gate(w_ref, b_ref):
        g = jnp.concatenate(
            [jnp.dot(yb[:, :half], w_ref[0], preferred_element_type=F32),
             jnp.dot(yb[:, half:], w_ref[1], preferred_element_type=F32)], axis=1)
        return jax.nn.sigmoid(g + b_ref[...])

    rg = gate(wa_ref, ba_ref)
    ig = gate(wx_ref, bx_ref)
    lam = lam_ref[...]
    log_sig = jnp.minimum(lam, 0.0) - jnp.log1p(jnp.exp(-jnp.abs(lam)))
    log_a = LRU_C * rg * log_sig
    a_cum = jnp.exp(log_a)
    b_cum = jnp.sqrt(-jnp.tanh(log_a) * (a_cum * a_cum + 1.0)) * (ig * y2)
    row = lax.broadcasted_iota(jnp.int32, (m, c), 0) % tr
    dist = 1
    while dist < tr:
        keep = row >= dist
        a_sh = jnp.where(keep, pltpu.roll(a_cum, dist, 0), 1.0)
        b_sh = jnp.where(keep, pltpu.roll(b_cum, dist, 0), 0.0)
        b_cum = a_cum * b_sh + b_cum
        a_cum = a_cum * a_sh
        dist *= 2
    h = a_cum.reshape(nb, tr, c) * ch_ref[...] + b_cum.reshape(nb, tr, c)
    o_ref[...] = (h * _gelu_tanh(yg_ref[...])).astype(BF16)
    h_last = h[:, tr - 1:tr, :]
    ch_ref[...] = h_last
    hl_ref[...] = h_last
    cx_ref[...] = new_tail
    tail_ref[...] = new_tail


def _lru(xr, yg, pre, h0, conv_w, conv_b, wa_bd, wx_bd, b_a, b_x, lam, nb, tr):
    nbt, t, c = xr.shape
    xmap = lambda b, i: (b, i, 0)
    smap = lambda b, i: (b, 0, 0)
    c2 = lambda b, i: (0, 0)
    c3 = lambda b, i: (0, 0, 0)
    big = pl.BlockSpec((nb, tr, c), xmap)
    return pl.pallas_call(
        _lru_kernel,
        grid=(nbt // nb, t // tr),
        in_specs=[big, big,
                  pl.BlockSpec((nb, SUBLANES, c), smap), pl.BlockSpec((nb, 1, c), smap),
                  pl.BlockSpec(conv_w.shape, c2), pl.BlockSpec((1, c), c2),
                  pl.BlockSpec(wa_bd.shape, c3), pl.BlockSpec(wx_bd.shape, c3),
                  pl.BlockSpec((1, c), c2), pl.BlockSpec((1, c), c2), pl.BlockSpec((1, c), c2)],
        out_specs=[big, pl.BlockSpec((nb, SUBLANES, c), smap), pl.BlockSpec((nb, 1, c), smap)],
        out_shape=[jax.ShapeDtypeStruct((nbt, t, c), BF16),
                   jax.ShapeDtypeStruct((nbt, SUBLANES, c), F32),
                   jax.ShapeDtypeStruct((nbt, 1, c), F32)],
        scratch_shapes=[pltpu.VMEM((nb, SUBLANES, c), F32), pltpu.VMEM((nb, 1, c), F32)],
        compiler_params=_params(("arbitrary", "arbitrary")),
        name="lru",
    )(xr, yg, pre, h0, conv_w, conv_b, wa_bd, wx_bd, b_a, b_x, lam)


def _outproj_kernel(*refs, aliased):
    (at_ref, lr_ref, x_ref, gm_ref, shf_ref, scf_ref, lnf_ref, wo_ref, wr_ref, br_ref) = refs[:10]
    x1_ref, h2_ref, route_ref, cnt_ref = refs[10 + aliased:]
    nb, tr, d = x_ref.shape
    m = nb * tr
    aw = at_ref.shape[-1]
    ne = wr_ref.shape[0]
    at = at_ref[...].reshape(m, aw)
    lr = lr_ref[...].reshape(m, aw)
    mix = (jnp.dot(at, wo_ref[0:aw, :], preferred_element_type=F32)
           + jnp.dot(lr, wo_ref[aw:2 * aw, :], preferred_element_type=F32))
    x1 = x_ref[...] + gm_ref[...] * mix.reshape(nb, tr, d)
    x1_ref[...] = x1
    ms = jnp.mean(x1 * x1, axis=-1, keepdims=True)
    h2 = x1 * lax.rsqrt(ms + NORM_EPS) * lnf_ref[...]
    h2 = (h2 * (1.0 + scf_ref[...]) + shf_ref[...]).reshape(m, d)
    h2_ref[...] = h2.astype(BF16)

    logits = lax.dot_general(wr_ref[...], h2, (((1,), (1,)), ((), ())),
                             precision=lax.Precision.HIGHEST, preferred_element_type=F32) + br_ref[...]
    e_iota = lax.broadcasted_iota(jnp.int32, (ne, m), 0).astype(F32)
    vals = logits
    top_v, sels = [], []
    for k in range(TOP_K):
        mx = jnp.max(vals, axis=0, keepdims=True)
        idx = jnp.min(jnp.where(vals == mx, e_iota, float(ne)), axis=0, keepdims=True)
        sel = e_iota == idx
        vals = jnp.where(sel, -jnp.inf, vals)
        top_v.append(mx)
        sels.append(sel)
        route_ref[0, k:k + 1, :] = idx
    ex = [jnp.exp(v - top_v[0]) for v in top_v]
    den = ex[0] + ex[1] + ex[2] + ex[3]
    chosen = jnp.zeros((ne, m), F32)
    for k in range(TOP_K):
        route_ref[0, 2 * TOP_K + k:2 * TOP_K + k + 1, :] = ex[k] / den
        chosen = chosen + jnp.where(sels[k], 1.0, 0.0)
    before = (lax.broadcasted_iota(jnp.int32, (m, m), 0) < lax.broadcasted_iota(jnp.int32, (m, m), 1))
    rank = jnp.dot(chosen.astype(BF16), jnp.where(before, 1.0, 0.0).astype(BF16), preferred_element_type=F32)
    for k in range(TOP_K):
        route_ref[0, TOP_K + k:TOP_K + k + 1, :] = jnp.sum(jnp.where(sels[k], rank, 0.0), axis=0, keepdims=True)
    route_ref[0, 3 * TOP_K:4 * TOP_K, :] = jnp.zeros((TOP_K, m), F32)
    cnt_ref[0] = jnp.broadcast_to(jnp.sum(chosen, axis=1, keepdims=True), (ne, LANES))


def _outproj(attn, lru_o, x, gm, shf, scf, lnf, w_out_bf, wr_t, br, nb, tr, n_tiles, tile0, prev):
    nbt, t, d = x.shape
    aw = attn.shape[-1]
    m = nb * tr
    assert m == TOKEN_TILE
    ne = wr_t.shape[0]
    tiles_per_seq = t // tr
    xmap = lambda b, i: (b, i, 0)
    mmap = lambda b, i: (b, 0, 0)
    c2 = lambda b, i: (0, 0)
    tile = lambda b, i: (tile0 + b * tiles_per_seq + i, 0)
    tile3 = lambda b, i: (tile0 + b * tiles_per_seq + i, 0, 0)
    mod = pl.BlockSpec((nb, 1, d), mmap)
    in_specs = [pl.BlockSpec((nb, tr, aw), xmap), pl.BlockSpec((nb, tr, aw), xmap),
                pl.BlockSpec((nb, tr, d), xmap), mod, mod, mod,
                pl.BlockSpec((1, d), c2), pl.BlockSpec(w_out_bf.shape, c2),
                pl.BlockSpec(wr_t.shape, c2), pl.BlockSpec((ne, 1), c2)]
    args = [attn, lru_o, x, gm, shf, scf, lnf, w_out_bf, wr_t, br]
    aliases = {}
    if prev is not None:
        in_specs += [pl.BlockSpec(memory_space=pl.ANY)] * 3
        args += list(prev)
        aliases = {10: 1, 11: 2, 12: 3}
    return pl.pallas_call(
        functools.partial(_outproj_kernel, aliased=len(aliases)),
        grid=(nbt // nb, tiles_per_seq),
        in_specs=in_specs,
        out_specs=[pl.BlockSpec((nb, tr, d), xmap), pl.BlockSpec((m, d), tile),
                   pl.BlockSpec((1, 4 * TOP_K, m), tile3), pl.BlockSpec((1, ne, LANES), tile3)],
        out_shape=[jax.ShapeDtypeStruct((nbt, t, d), F32),
                   jax.ShapeDtypeStruct((n_tiles * m, d), BF16),
                   jax.ShapeDtypeStruct((n_tiles, 4 * TOP_K, m), F32),
                   jax.ShapeDtypeStruct((n_tiles, ne, LANES), F32)],
        input_output_aliases=aliases,
        compiler_params=_params(("arbitrary", "arbitrary")),
        name="outproj",
    )(*args)


def _tile_rows(m):
    cap = TOP_K * m + N_EXPERTS * (BF16_ROWS - 1) + BF16_ROWS
    return -(-cap // TOKEN_TILE) * TOKEN_TILE


def _route_tables(cnt):
    nt = cnt.shape[0]
    g = BF16_ROWS
    bm = EXPERT_ROWS
    n_chunks = _tile_rows(TOKEN_TILE) // g
    n_gap = -(-(N_EXPERTS * (bm // g - 1)) // nt)
    e_ids = jnp.arange(N_EXPERTS, dtype=jnp.int32)
    t_ids = jnp.arange(nt, dtype=jnp.int32)
    upto = (e_ids[:, None] <= e_ids[None, :]).astype(jnp.int32)
    pc = (cnt + g - 1) // g * g
    ctile = jnp.sum(pc[:, :, None] * upto[None], axis=1)
    toff = ctile - pc
    trow = ctile[:, -1]
    tot = jnp.sum(pc, axis=0)
    reg = (tot + bm - 1) // bm * bm
    creg = jnp.sum(reg[:, None] * upto, axis=0)
    base = creg - reg
    earlier = (t_ids[:, None] < t_ids[None, :]).astype(jnp.int32)
    goff = base[None, :] + jnp.sum(pc[:, None, :] * earlier[:, :, None], axis=0)
    r = jnp.arange(n_chunks, dtype=jnp.int32) * g
    r3 = r[None, :, None]
    in_seg = (toff[:, None, :] <= r3) & (r3 < ctile[:, None, :])
    dst = jnp.sum(jnp.where(in_seg, (goff - toff)[:, None, :], 0), axis=2) + r[None, :]
    dst = jnp.where(r[None, :] < trow[:, None], dst, -1)
    gcnt = (reg - tot) // g
    gcum = jnp.sum(gcnt[:, None] * upto, axis=0)
    gstart = gcum - gcnt
    s = jnp.arange(nt * n_gap, dtype=jnp.int32)
    in_gap = (gstart[None, :] <= s[:, None]) & (s[:, None] < gcum[None, :])
    gdst = jnp.sum(jnp.where(in_gap, (base + tot - g * gstart)[None, :] + g * s[:, None], 0), axis=1)
    gdst = jnp.where(s < gcum[-1], gdst, -1).reshape(nt, n_gap)
    table = jnp.concatenate([dst, gdst], axis=1).astype(jnp.int32)
    toff_b = jnp.broadcast_to(toff.astype(F32)[:, :, None], (nt, N_EXPERTS, LANES))
    return table, toff_b, base.astype(jnp.int32), (reg // bm).astype(jnp.int32), n_chunks


def _slot_rows(route_ref, toff_ref, m):
    ne = toff_ref.shape[1]
    e_iota = lax.broadcasted_iota(jnp.int32, (ne, m), 0).astype(F32)
    toff_col = toff_ref[0][:, 0:1]
    pos = []
    for k in range(TOP_K):
        sel = e_iota == route_ref[0, k:k + 1, :]
        start = jnp.sum(jnp.where(sel, toff_col, 0.0), axis=0, keepdims=True)
        pos.append(start + route_ref[0, TOP_K + k:TOP_K + k + 1, :])
    return pos


def _dispatch_kernel(tab_ref, h2_ref, route_ref, toff_ref, xs_hbm, buf_ref, sem, *, n_chunks):
    t = pl.program_id(0)
    m = h2_ref.shape[0]
    rows = buf_ref.shape[0]
    n_entries = tab_ref.shape[1]
    pos = _slot_rows(route_ref, toff_ref, m)
    h2 = h2_ref[...]
    for rc in range(rows // m):
        r_iota = (lax.broadcasted_iota(jnp.int32, (m, m), 0) + rc * m).astype(F32)
        onehot = jnp.zeros((m, m), F32)
        for k in range(TOP_K):
            onehot = jnp.where(r_iota == pos[k], 1.0, onehot)
        buf_ref[rc * m:(rc + 1) * m, :] = jnp.dot(onehot.astype(BF16), h2, preferred_element_type=F32).astype(BF16)

    def copy(c):
        src = jnp.where(c < n_chunks, c * BF16_ROWS, rows - BF16_ROWS)
        return pltpu.make_async_copy(
            buf_ref.at[pl.ds(pl.multiple_of(src, BF16_ROWS), BF16_ROWS)],
            xs_hbm.at[pl.ds(pl.multiple_of(tab_ref[t, c], BF16_ROWS), BF16_ROWS)], sem)

    def start(c, carry):
        @pl.when(tab_ref[t, c] >= 0)
        def _():
            copy(c).start()
        return carry

    def wait(c, carry):
        @pl.when(tab_ref[t, c] >= 0)
        def _():
            copy(c).wait()
        return carry

    lax.fori_loop(0, n_entries, start, 0)
    lax.fori_loop(0, n_entries, wait, 0)


def _dispatch(table, h2, route, toff_b, n_rows, n_chunks):
    nt = route.shape[0]
    m = TOKEN_TILE
    d = h2.shape[1]
    rows = _tile_rows(m)
    grid_spec = pltpu.PrefetchScalarGridSpec(
        num_scalar_prefetch=1,
        grid=(nt,),
        in_specs=[pl.BlockSpec((m, d), lambda t, tab: (t, 0)),
                  pl.BlockSpec((1, 4 * TOP_K, m), lambda t, tab: (t, 0, 0)),
                  pl.BlockSpec((1, N_EXPERTS, LANES), lambda t, tab: (t, 0, 0))],
        out_specs=pl.BlockSpec(memory_space=pl.ANY),
        scratch_shapes=[pltpu.VMEM((rows, d), BF16), pltpu.SemaphoreType.DMA(())],
    )
    return pl.pallas_call(
        functools.partial(_dispatch_kernel, n_chunks=n_chunks),
        grid_spec=grid_spec,
        out_shape=jax.ShapeDtypeStruct((n_rows, d), BF16),
        compiler_params=_params(("arbitrary",)),
        name="dispatch",
    )(table, h2, route, toff_b)


def _expert_kernel(row0_ref, nblk_ref, xs_hbm, wgu_ref, bgu_ref, wdn_ref, bdn_ref, ys_hbm,
                   wgu_bf, wdn_bf, xbuf, ybuf, sem_in, sem_out):
    e = pl.program_id(0)
    ne = pl.num_programs(0)
    bm = xbuf.shape[1]
    dff = wdn_ref.shape[1]
    nblk = nblk_ref[e]

    def in_copy(expert, j, slot):
        start = pl.multiple_of(row0_ref[expert] + j * bm, bm)
        return pltpu.make_async_copy(xs_hbm.at[pl.ds(start, bm)], xbuf.at[slot], sem_in.at[slot])

    def out_copy(j, slot):
        start = pl.multiple_of(row0_ref[e] + j * bm, bm)
        return pltpu.make_async_copy(ybuf.at[slot], ys_hbm.at[pl.ds(start, bm)], sem_out.at[slot])

    @pl.when((e == 0) & (nblk > 0))
    def _():
        in_copy(e, 0, 0).start()

    wgu_bf[...] = wgu_ref[0].astype(BF16)
    wdn_bf[...] = wdn_ref[0].astype(BF16)

    def block(j, carry):
        slot = j % 2
        in_copy(e, j, slot).wait()

        @pl.when(j + 1 < nblk)
        def _():
            in_copy(e, j + 1, 1 - slot).start()

        @pl.when(j >= 2)
        def _():
            out_copy(j - 2, slot).wait()

        gu = jnp.dot(xbuf[slot], wgu_bf[...], preferred_element_type=F32) + bgu_ref[0]
        gate = jnp.minimum(gu[:, :dff], SWIGLU_LIMIT)
        up = jnp.clip(gu[:, dff:], -SWIGLU_LIMIT, SWIGLU_LIMIT)
        glu = gate * jax.nn.sigmoid(gate * SWIGLU_ALPHA)
        act = ((up + 1.0) * glu).astype(BF16)
        ybuf[slot] = (jnp.dot(act, wdn_bf[...], preferred_element_type=F32) + bdn_ref[0]).astype(BF16)
        out_copy(j, slot).start()
        return carry

    lax.fori_loop(0, nblk, block, 0)

    nxt = jnp.minimum(e + 1, ne - 1)

    @pl.when((e + 1 < ne) & (nblk_ref[nxt] > 0))
    def _():
        in_copy(nxt, 0, 0).start()

    @pl.when(nblk >= 2)
    def _():
        out_copy(nblk - 2, nblk % 2).wait()

    @pl.when(nblk >= 1)
    def _():
        out_copy(nblk - 1, (nblk - 1) % 2).wait()


def _experts(row0, nblk, xs, w_gu, b_gu, w_dn, b_dn):
    n_rows, d = xs.shape
    bm = EXPERT_ROWS
    ne, _, dff2 = w_gu.shape
    dff = w_dn.shape[1]
    exp3 = lambda e, r0, nb: (e, 0, 0)
    grid_spec = pltpu.PrefetchScalarGridSpec(
        num_scalar_prefetch=2,
        grid=(ne,),
        in_specs=[pl.BlockSpec(memory_space=pl.ANY),
                  pl.BlockSpec((1, d, dff2), exp3), pl.BlockSpec((1, 1, dff2), exp3),
                  pl.BlockSpec((1, dff, d), exp3), pl.BlockSpec((1, 1, d), exp3)],
        out_specs=pl.BlockSpec(memory_space=pl.ANY),
        scratch_shapes=[pltpu.VMEM((d, dff2), BF16), pltpu.VMEM((dff, d), BF16),
                        pltpu.VMEM((2, bm, d), BF16), pltpu.VMEM((2, bm, d), BF16),
                        pltpu.SemaphoreType.DMA((2,)), pltpu.SemaphoreType.DMA((2,))],
    )
    return pl.pallas_call(
        _expert_kernel,
        grid_spec=grid_spec,
        out_shape=jax.ShapeDtypeStruct((n_rows, d), BF16),
        compiler_params=_params(("arbitrary",)),
        name="experts",
    )(row0, nblk, xs, w_gu, b_gu.reshape(ne, 1, dff2), w_dn, b_dn.reshape(ne, 1, d))


def _combine_kernel(tab_ref, ys_hbm, route_ref, toff_ref, x1_ref, gf_ref, o_ref, buf_ref, sem, *, n_chunks, tile0):
    step = pl.program_id(0)
    t = tile0 + step
    nb, tr, d = x1_ref.shape
    m = nb * tr
    rows = buf_ref.shape[0]

    @pl.when(step == 0)
    def _():
        buf_ref[...] = jnp.zeros(buf_ref.shape, BF16)

    def copy(c):
        return pltpu.make_async_copy(
            ys_hbm.at[pl.ds(pl.multiple_of(tab_ref[t, c], BF16_ROWS), BF16_ROWS)],
            buf_ref.at[pl.ds(pl.multiple_of(c * BF16_ROWS, BF16_ROWS), BF16_ROWS)], sem)

    def start(c, carry):
        @pl.when(tab_ref[t, c] >= 0)
        def _():
            copy(c).start()
        return carry

    def wait(c, carry):
        @pl.when(tab_ref[t, c] >= 0)
        def _():
            copy(c).wait()
        return carry

    lax.fori_loop(0, n_chunks, start, 0)

    pos = _slot_rows(route_ref, toff_ref, m)
    gates = [route_ref[0, 2 * TOP_K + k:2 * TOP_K + k + 1, :] for k in range(TOP_K)]
    stacked = jnp.concatenate(pos + gates + [jnp.zeros((LANES - 2 * TOP_K, m), F32)], axis=0)
    cols = stacked.T
    lax.fori_loop(0, n_chunks, wait, 0)

    acc = jnp.zeros((m, d), F32)
    for rc in range(rows // m):
        c_iota = (lax.broadcasted_iota(jnp.int32, (m, m), 1) + rc * m).astype(F32)
        weights = jnp.zeros((m, m), F32)
        for k in range(TOP_K):
            weights = jnp.where(c_iota == cols[:, k:k + 1], cols[:, TOP_K + k:TOP_K + k + 1], weights)
        acc = acc + jnp.dot(weights.astype(BF16), buf_ref[rc * m:(rc + 1) * m, :], preferred_element_type=F32)
    o_ref[...] = x1_ref[...] + gf_ref[...] * acc.reshape(nb, tr, d)


def _combine(table, ys, route, toff_b, x1, gf, nb, tr, tile0, n_chunks):
    nbt, t, d = x1.shape
    m = nb * tr
    assert m == TOKEN_TILE
    rows = _tile_rows(m)
    tiles_per_seq = t // tr
    n_steps = (nbt // nb) * tiles_per_seq
    xmap = lambda s, tab: (s // tiles_per_seq, s % tiles_per_seq, 0)
    grid_spec = pltpu.PrefetchScalarGridSpec(
        num_scalar_prefetch=1,
        grid=(n_steps,),
        in_specs=[pl.BlockSpec(memory_space=pl.ANY),
                  pl.BlockSpec((1, 4 * TOP_K, m), lambda s, tab: (tile0 + s, 0, 0)),
                  pl.BlockSpec((1, N_EXPERTS, LANES), lambda s, tab: (tile0 + s, 0, 0)),
                  pl.BlockSpec((nb, tr, d), xmap),
                  pl.BlockSpec((nb, 1, d), lambda s, tab: (s // tiles_per_seq, 0, 0))],
        out_specs=pl.BlockSpec((nb, tr, d), xmap),
        scratch_shapes=[pltpu.VMEM((rows, d), BF16), pltpu.SemaphoreType.DMA(())],
    )
    return pl.pallas_call(
        functools.partial(_combine_kernel, n_chunks=n_chunks, tile0=tile0),
        grid_spec=grid_spec,
        out_shape=jax.ShapeDtypeStruct((nbt, t, d), F32),
        compiler_params=_params(("arbitrary",)),
        name="combine",
    )(table, ys, route, toff_b, x1, gf)


def _block_diag(w, groups):
    n, k, _ = w.shape
    w = w.reshape(n // groups, groups, k, k)
    eye = jnp.eye(groups, dtype=w.dtype)
    return jnp.einsum("ngij,gh->ngihj", w, eye).reshape(n // groups, groups * k, groups * k)


def _layer(xp, xs, mod, k_cache, v_cache, conv_state, lru_state, lw):
    (ln_mix, ln_ffn, w_in, q_norm, k_norm, rel_bias, conv_w, conv_b, w_rg, b_rg, w_ig, b_ig, lam,
     w_out, w_router, b_router, w_gu, b_gu, w_dn, b_dn) = lw
    bp, s, d = xp.shape
    bs, ts, _ = xs.shape
    aw = w_out.shape[0] // 2
    nh = aw // HEAD_DIM
    m = TOKEN_TILE
    assert s % m == 0 and bs * ts == m and s % ATTN_Q_TILE == 0 and s % LRU_TILE == 0

    terms = [mod[:, i * d:(i + 1) * d][:, None, :] for i in range(6)]
    tp = [t[:bp] for t in terms]
    tsm = [t[bp:] for t in terms]

    w_in_bf = w_in.astype(BF16)
    w_out_bf = w_out.astype(BF16)
    qn_t = jnp.tile(q_norm, nh).reshape(1, aw)
    kn_t = jnp.tile(k_norm, nh).reshape(1, aw)
    head_mean = _block_diag(jnp.full((nh, HEAD_DIM, HEAD_DIM), 1.0 / HEAD_DIM, F32), nh)[0].astype(BF16)
    groups = MXU_DIM // w_rg.shape[-1]
    wa_bd = _block_diag(w_rg, groups).astype(BF16)
    wx_bd = _block_diag(w_ig, groups).astype(BF16)
    lw_c = b_rg.size
    b_a = b_rg.reshape(1, lw_c)
    b_x = b_ig.reshape(1, lw_c)
    lam2 = lam.reshape(1, lw_c)
    cb2 = conv_b.reshape(1, lw_c)
    ln_mix2 = ln_mix.reshape(1, d)
    ln_ffn2 = ln_ffn.reshape(1, d)
    wr_t = w_router.T
    br = b_router.reshape(-1, 1)
    tab_p = _bias_table(rel_bias, 3 * ATTN_Q_TILE - 1)
    r_cache = k_cache.shape[1]
    tab_s = _bias_table(rel_bias, r_cache + ts - 1)

    qp, kp, vp, xrp, ygp, k32p, v32p = _mixin(xp, tp[0], tp[1], ln_mix2, w_in_bf, qn_t, kn_t, head_mean, 1, m)
    qs, ks, vs, xrs, ygs, k32s, v32s = _mixin(xs, tsm[0], tsm[1], ln_mix2, w_in_bf, qn_t, kn_t, head_mean, bs, ts)
    attn_p = _attn_prompt(qp, kp, vp, tab_p)
    attn_s = _attn_step(qs, ks, vs, k_cache.reshape(bs, r_cache, aw), v_cache.reshape(bs, r_cache, aw), tab_s)

    zeros_pre = jnp.zeros((bp, SUBLANES, lw_c), F32)
    zeros_h = jnp.zeros((bp, 1, lw_c), F32)
    pre_s = jnp.pad(conv_state, ((0, 0), (SUBLANES - (CONV_WIDTH - 1), 0), (0, 0)))
    lru_p, tail_p, hl_p = _lru(xrp, ygp, zeros_pre, zeros_h, conv_w, cb2, wa_bd, wx_bd, b_a, b_x, lam2, 1, LRU_TILE)
    lru_s, tail_s, hl_s = _lru(xrs, ygs, pre_s, lru_state[:, None, :], conv_w, cb2, wa_bd, wx_bd, b_a, b_x, lam2,
                               bs, ts)

    n_tiles = bp * (s // m) + 1
    x1p, h2, route, cnt = _outproj(attn_p, lru_p, xp, tp[2], tp[3], tp[4], ln_ffn2, w_out_bf, wr_t, br,
                                   1, m, n_tiles, 0, None)
    x1s, h2, route, cnt = _outproj(attn_s, lru_s, xs, tsm[2], tsm[3], tsm[4], ln_ffn2, w_out_bf, wr_t, br,
                                   bs, ts, n_tiles, n_tiles - 1, (h2, route, cnt))

    g = BF16_ROWS
    bound = TOP_K * m * n_tiles + n_tiles * N_EXPERTS * (g - 1) + N_EXPERTS * (EXPERT_ROWS - g)
    n_blocks = -(-bound // EXPERT_ROWS)
    table, toff_b, row0, nblk, n_chunks = _route_tables(cnt[:, :, 0].astype(jnp.int32))
    xs_sorted = _dispatch(table, h2, route, toff_b, n_blocks * EXPERT_ROWS, n_chunks)
    ys_sorted = _experts(row0, nblk, xs_sorted, w_gu, b_gu, w_dn, b_dn)
    yp = _combine(table, ys_sorted, route, toff_b, x1p, tp[5], 1, m, 0, n_chunks)
    ysm = _combine(table, ys_sorted, route, toff_b, x1s, tsm[5], bs, ts, n_tiles - 1, n_chunks)

    keep = k32p.shape[1]
    new = (k32p.reshape(bp, keep, nh, HEAD_DIM), v32p.reshape(bp, keep, nh, HEAD_DIM),
           tail_p[:, SUBLANES - (CONV_WIDTH - 1):, :], hl_p[:, 0, :],
           k32s.reshape(bs, ts, nh, HEAD_DIM), v32s.reshape(bs, ts, nh, HEAD_DIM),
           tail_s[:, SUBLANES - (CONV_WIDTH - 1):, :], hl_s[:, 0, :])
    return yp, ysm, new


def kernel(x_prompt, x_sample, c_prompt, c_sample, cache_k, cache_v, state_conv, state_lru, ln_mix_w, ln_ffn_w, w_ada, b_ada, w_in, q_norm_w, k_norm_w, rel_bias, conv_w, conv_b, w_rgate, b_rgate, w_igate, b_igate, lru_lambda, w_out, w_router, b_router, w_gate_up, b_gate_up, w_down, b_down):
    depth = w_in.shape[0]
    yp, ys = x_prompt, x_sample
    c_all = jnp.concatenate([c_prompt, c_sample], axis=0)
    collected = [[] for _ in range(8)]
    for l in range(depth):
        mod = _ada(c_all, w_ada[l], b_ada[l])
        lw = (ln_mix_w[l], ln_ffn_w[l], w_in[l], q_norm_w[l], k_norm_w[l], rel_bias[l], conv_w[l], conv_b[l],
              w_rgate[l], b_rgate[l], w_igate[l], b_igate[l], lru_lambda[l], w_out[l], w_router[l], b_router[l],
              w_gate_up[l], b_gate_up[l], w_down[l], b_down[l])
        yp, ys, new = _layer(yp, ys, mod, cache_k[l], cache_v[l], state_conv[l], state_lru[l], lw)
        for acc, val in zip(collected, new):
            acc.append(val)
    return (yp, ys) + tuple(jnp.stack(vals) for vals in collected)
```

```python
import functools

import jax
import jax.numpy as jnp
from jax import lax
from jax.experimental import pallas as pl
from jax.experimental.pallas import tpu as pltpu

F32 = jnp.float32
BF16 = jnp.bfloat16

CHUNK = 64
N_LEFT_CHUNKS = 8
ATTN_WINDOW = N_LEFT_CHUNKS * CHUNK
HEAD_DIM = 64
REL_CLIP = 128
CONV_WIDTH = 4
LRU_C = 8.0
N_EXPERTS = 32
TOP_K = 4
SWIGLU_LIMIT = 7.0
SWIGLU_ALPHA = 1.702
NORM_EPS = 1e-6
NEG_INF = -1e30

LANES = 128
SUBLANES = 8
BF16_ROWS = 16
MXU_DIM = 256

TOKEN_TILE = 512
ATTN_Q_TILE = 256
LRU_TILE = 256
EXPERT_ROWS = 256
BIAS_TABLE = 1024
VMEM_LIMIT = 56 * 1024 * 1024


def _params(sem, vmem=VMEM_LIMIT):
    return pltpu.CompilerParams(dimension_semantics=sem, vmem_limit_bytes=vmem)


def _ada_kernel(c_ref, w_ref, b_ref, o_ref):
    c = c_ref[...]
    s = (c * jax.nn.sigmoid(c)).astype(BF16)
    o_ref[...] = jnp.dot(s, w_ref[...].astype(BF16), preferred_element_type=F32) + b_ref[...]


def _ada(c_all, w_ada, b_ada):
    n, d = c_all.shape
    nout = w_ada.shape[1]
    tn = 1024
    return pl.pallas_call(
        _ada_kernel,
        grid=(nout // tn,),
        in_specs=[pl.BlockSpec((n, d), lambda j: (0, 0)),
                  pl.BlockSpec((d, tn), lambda j: (0, j)),
                  pl.BlockSpec((1, tn), lambda j: (0, j))],
        out_specs=pl.BlockSpec((n, tn), lambda j: (0, j)),
        out_shape=jax.ShapeDtypeStruct((n, nout), F32),
        compiler_params=_params(("arbitrary",)),
        name="ada",
    )(c_all, w_ada, b_ada.reshape(1, nout))


def _mixin_kernel(x_ref, sh_ref, sc_ref, ln_ref, win_ref, qn_ref, kn_ref, bd_ref,
                  q_ref, k_ref, v_ref, xr_ref, yg_ref, k32_ref, v32_ref):
    nb, tr, d = x_ref.shape
    m = nb * tr
    aw = q_ref.shape[-1]
    x = x_ref[...]
    ms = jnp.mean(x * x, axis=-1, keepdims=True)
    h = x * lax.rsqrt(ms + NORM_EPS) * ln_ref[...]
    h = h * (1.0 + sc_ref[...]) + sh_ref[...]
    hb = h.reshape(m, d).astype(BF16)

    def proj(part):
        return jnp.dot(hb, win_ref[:, part * aw:(part + 1) * aw], preferred_element_type=F32)

    def head_norm(t, w_ref):
        msq = jnp.dot((t * t).astype(BF16), bd_ref[...], preferred_element_type=F32)
        return t * lax.rsqrt(msq + NORM_EPS) * w_ref[...]

    q = head_norm(proj(0), qn_ref)
    k = head_norm(proj(1), kn_ref)
    v = proj(2)
    q_ref[...] = (q * (HEAD_DIM ** -0.5)).astype(BF16).reshape(nb, tr, aw)
    k_ref[...] = k.astype(BF16).reshape(nb, tr, aw)
    v_ref[...] = v.astype(BF16).reshape(nb, tr, aw)
    k32_ref[...] = k.reshape(nb, tr, aw)
    v32_ref[...] = v.reshape(nb, tr, aw)
    xr_ref[...] = proj(3).reshape(nb, tr, aw)
    yg_ref[...] = proj(4).reshape(nb, tr, aw)


def _mixin(x, sh, sc, ln_w, w_in_bf, qn_t, kn_t, bd, nb, tr):
    nbt, t, d = x.shape
    aw = qn_t.shape[-1]
    keep = min(ATTN_WINDOW, t)
    assert tr == keep or t == tr
    grid = (nbt // nb, t // tr)
    xmap = lambda b, i: (b, i, 0)
    mmap = lambda b, i: (b, 0, 0)
    cmap = lambda b, i: (0, 0)
    tmap = lambda b, i: (b, 0, 0)
    big = pl.BlockSpec((nb, tr, aw), xmap)
    tail = pl.BlockSpec((nb, keep, aw), tmap)
    return pl.pallas_call(
        _mixin_kernel,
        grid=grid,
        in_specs=[pl.BlockSpec((nb, tr, d), xmap),
                  pl.BlockSpec((nb, 1, d), mmap), pl.BlockSpec((nb, 1, d), mmap),
                  pl.BlockSpec((1, d), cmap),
                  pl.BlockSpec(w_in_bf.shape, cmap),
                  pl.BlockSpec((1, aw), cmap), pl.BlockSpec((1, aw), cmap),
                  pl.BlockSpec(bd.shape, cmap)],
        out_specs=[big, big, big, big, big, tail, tail],
        out_shape=[jax.ShapeDtypeStruct((nbt, t, aw), BF16)] * 3
        + [jax.ShapeDtypeStruct((nbt, t, aw), F32)] * 2
        + [jax.ShapeDtypeStruct((nbt, keep, aw), F32)] * 2,
        compiler_params=_params(("arbitrary", "arbitrary")),
        name="mixin",
    )(x, sh, sc, ln_w, w_in_bf, qn_t, kn_t, bd)


def _bias_table(rel_bias, off):
    h = rel_bias.shape[0]
    left = off - REL_CLIP
    right = BIAS_TABLE - left - (2 * REL_CLIP + 1)
    assert left >= 0 and right >= 0
    return jnp.concatenate([jnp.broadcast_to(rel_bias[:, :1], (h, left)), rel_bias,
                            jnp.broadcast_to(rel_bias[:, -1:], (h, right))], axis=1)


def _toeplitz(tab_row, rows, cols):
    t = jnp.broadcast_to(tab_row, (rows, BIAS_TABLE))
    t = pltpu.roll(t, BIAS_TABLE - (rows - 1), 1, stride=1, stride_axis=0)
    return t[:, :cols]


def _attn_kernel(q_ref, k0_ref, k1_ref, k2_ref, v0_ref, v1_ref, v2_ref, tab_ref, o_ref, bias_ref):
    b = pl.program_id(0)
    s = pl.program_id(1)
    qt = q_ref.shape[1]
    nk = 3 * qt
    nh = bias_ref.shape[0]

    @pl.when((b == 0) & (s == 0))
    def _():
        qi = lax.broadcasted_iota(jnp.int32, (qt, nk), 0) // CHUNK
        kc = lax.broadcasted_iota(jnp.int32, (qt, nk), 1) // CHUNK
        for h in range(nh):
            band = jnp.where(kc <= qi + N_LEFT_CHUNKS, _toeplitz(tab_ref[h:h + 1, :], qt, nk), NEG_INF)
            bias_ref[h] = jnp.where(kc >= qi, band, NEG_INF)

    q = q_ref[0]
    kcat = jnp.concatenate([k0_ref[0], k1_ref[0], k2_ref[0]], axis=0)
    vcat = jnp.concatenate([v0_ref[0], v1_ref[0], v2_ref[0]], axis=0)
    in_seq = lax.broadcasted_iota(jnp.int32, (qt, nk), 1) >= (2 - s) * qt
    outs = []
    for h in range(nh):
        sl = slice(h * HEAD_DIM, (h + 1) * HEAD_DIM)
        sc = lax.dot_general(q[:, sl], kcat[:, sl], (((1,), (1,)), ((), ())), preferred_element_type=F32)
        sc = jnp.where(in_seq, sc + bias_ref[h], NEG_INF)
        mx = jnp.max(sc, axis=-1, keepdims=True)
        p = jnp.exp(sc - mx)
        l = jnp.sum(p, axis=-1, keepdims=True)
        o = jnp.dot(p.astype(BF16), vcat[:, sl], preferred_element_type=F32)
        outs.append(o / l)
    o_ref[0] = jnp.concatenate(outs, axis=-1).astype(BF16)


def _attn_prompt(q, k, v, tab):
    b, s, aw = q.shape
    qt = ATTN_Q_TILE
    nh = aw // HEAD_DIM
    qspec = pl.BlockSpec((1, qt, aw), lambda i, j: (i, j, 0))

    def kspec(back):
        return pl.BlockSpec((1, qt, aw), lambda i, j: (i, jnp.maximum(j - back, 0), 0))

    return pl.pallas_call(
        _attn_kernel,
        grid=(b, s // qt),
        in_specs=[qspec, kspec(2), kspec(1), kspec(0), kspec(2), kspec(1), kspec(0),
                  pl.BlockSpec(tab.shape, lambda i, j: (0, 0))],
        out_specs=qspec,
        out_shape=jax.ShapeDtypeStruct((b, s, aw), BF16),
        scratch_shapes=[pltpu.VMEM((nh, qt, 3 * qt), F32)],
        compiler_params=_params(("arbitrary", "arbitrary")),
        name="attn_prompt",
    )(q, k, k, k, v, v, v, tab)


def _attn_step_kernel(q_ref, kn_ref, vn_ref, ck_ref, cv_ref, tab_ref, o_ref, bias_ref):
    b = pl.program_id(0)
    t = q_ref.shape[1]
    r = ck_ref.shape[1]
    nk = r + LANES
    nh = bias_ref.shape[0]
    aw = q_ref.shape[-1]

    @pl.when(b == 0)
    def _():
        ok = lax.broadcasted_iota(jnp.int32, (t, nk), 1) < r + t
        for h in range(nh):
            bias_ref[h] = jnp.where(ok, _toeplitz(tab_ref[h:h + 1, :], t, nk), NEG_INF)

    q = q_ref[0]
    pad = jnp.zeros((LANES - t, aw), BF16)
    kcat = jnp.concatenate([ck_ref[0].astype(BF16), kn_ref[0], pad], axis=0)
    vcat = jnp.concatenate([cv_ref[0].astype(BF16), vn_ref[0], pad], axis=0)
    outs = []
    for h in range(nh):
        sl = slice(h * HEAD_DIM, (h + 1) * HEAD_DIM)
        sc = lax.dot_general(q[:, sl], kcat[:, sl], (((1,), (1,)), ((), ())), preferred_element_type=F32)
        sc = sc + bias_ref[h]
        mx = jnp.max(sc, axis=-1, keepdims=True)
        p = jnp.exp(sc - mx)
        l = jnp.sum(p, axis=-1, keepdims=True)
        o = jnp.dot(p.astype(BF16), vcat[:, sl], preferred_element_type=F32)
        outs.append(o / l)
    o_ref[0] = jnp.concatenate(outs, axis=-1).astype(BF16)


def _attn_step(q, kn, vn, ck, cv, tab):
    b, t, aw = q.shape
    r = ck.shape[1]
    nh = aw // HEAD_DIM
    new = pl.BlockSpec((1, t, aw), lambda i: (i, 0, 0))
    old = pl.BlockSpec((1, r, aw), lambda i: (i, 0, 0))
    return pl.pallas_call(
        _attn_step_kernel,
        grid=(b,),
        in_specs=[new, new, new, old, old, pl.BlockSpec(tab.shape, lambda i: (0, 0))],
        out_specs=new,
        out_shape=jax.ShapeDtypeStruct((b, t, aw), BF16),
        scratch_shapes=[pltpu.VMEM((nh, t, r + LANES), F32)],
        compiler_params=_params(("arbitrary",)),
        name="attn_step",
    )(q, kn, vn, ck, cv, tab)


def _gelu_tanh(x):
    return x * (0.5 * (1.0 + jnp.tanh(0.7978845608028654 * (x + 0.044715 * (x * x * x)))))


def _lru_kernel(xr_ref, yg_ref, pre_ref, h0_ref, cw_ref, cb_ref, wa_ref, wx_ref, ba_ref, bx_ref, lam_ref,
                o_ref, tail_ref, hl_ref, cx_ref, ch_ref):
    step = pl.program_id(1)
    nb, tr, c = xr_ref.shape
    m = nb * tr
    half = c // 2

    @pl.when(step == 0)
    def _():
        cx_ref[...] = pre_ref[...]
        ch_ref[...] = h0_ref[...]

    x = xr_ref[...]
    xp = jnp.concatenate([cx_ref[...], x], axis=1)
    new_tail = xp[:, tr:tr + SUBLANES, :]
    xp2 = xp.reshape(nb * (tr + SUBLANES), c)
    y = cb_ref[...] + cw_ref[CONV_WIDTH - 1:CONV_WIDTH, :] * x
    for back in range(1, CONV_WIDTH):
        shifted = pltpu.roll(xp2, back, 0).reshape(nb, tr + SUBLANES, c)[:, SUBLANES:, :]
        y = y + cw_ref[CONV_WIDTH - 1 - back:CONV_WIDTH - back, :] * shifted
    y2 = y.reshape(m, c)
    yb = y2.astype(BF16)

    def gate(w_ref, b_ref):
        g = jnp.concatenate(
            [jnp.dot(yb[:, :half], w_ref[0], preferred_element_type=F32),
             jnp.dot(yb[:, half:], w_ref[1], preferred_element_type=F32)], axis=1)
        return jax.nn.sigmoid(g + b_ref[...])

    rg = gate(wa_ref, ba_ref)
    ig = gate(wx_ref, bx_ref)
    lam = lam_ref[...]
    log_sig = jnp.minimum(lam, 0.0) - jnp.log1p(jnp.exp(-jnp.abs(lam)))
    log_a = LRU_C * rg * log_sig
    a_cum = jnp.exp(log_a)
    b_cum = jnp.sqrt(-jnp.tanh(log_a) * (a_cum * a_cum + 1.0)) * (ig * y2)
    row = lax.broadcasted_iota(jnp.int32, (m, c), 0) % tr
    dist = 1
    while dist < tr:
        keep = row >= dist
        a_sh = jnp.where(keep, pltpu.roll(a_cum, dist, 0), 1.0)
        b_sh = jnp.where(keep, pltpu.roll(b_cum, dist, 0), 0.0)
        b_cum = a_cum * b_sh + b_cum
        a_cum = a_cum * a_sh
        dist *= 2
    h = a_cum.reshape(nb, tr, c) * ch_ref[...] + b_cum.reshape(nb, tr, c)
    o_ref[...] = (h * _gelu_tanh(yg_ref[...])).astype(BF16)
    h_last = h[:, tr - 1:tr, :]
    ch_ref[...] = h_last
    hl_ref[...] = h_last
    cx_ref[...] = new_tail
    tail_ref[...] = new_tail


def _lru(xr, yg, pre, h0, conv_w, conv_b, wa_bd, wx_bd, b_a, b_x, lam, nb, tr):
    nbt, t, c = xr.shape
    xmap = lambda b, i: (b, i, 0)
    smap = lambda b, i: (b, 0, 0)
    c2 = lambda b, i: (0, 0)
    c3 = lambda b, i: (0, 0, 0)
    big = pl.BlockSpec((nb, tr, c), xmap)
    return pl.pallas_call(
        _lru_kernel,
        grid=(nbt // nb, t // tr),
        in_specs=[big, big,
                  pl.BlockSpec((nb, SUBLANES, c), smap), pl.BlockSpec((nb, 1, c), smap),
                  pl.BlockSpec(conv_w.shape, c2), pl.BlockSpec((1, c), c2),
                  pl.BlockSpec(wa_bd.shape, c3), pl.BlockSpec(wx_bd.shape, c3),
                  pl.BlockSpec((1, c), c2), pl.BlockSpec((1, c), c2), pl.BlockSpec((1, c), c2)],
        out_specs=[big, pl.BlockSpec((nb, SUBLANES, c), smap), pl.BlockSpec((nb, 1, c), smap)],
        out_shape=[jax.ShapeDtypeStruct((nbt, t, c), BF16),
                   jax.ShapeDtypeStruct((nbt, SUBLANES, c), F32),
                   jax.ShapeDtypeStruct((nbt, 1, c), F32)],
        scratch_shapes=[pltpu.VMEM((nb, SUBLANES, c), F32), pltpu.VMEM((nb, 1, c), F32)],
        compiler_params=_params(("arbitrary", "arbitrary")),
        name="lru",
    )(xr, yg, pre, h0, conv_w, conv_b, wa_bd, wx_bd, b_a, b_x, lam)


def _outproj_kernel(*refs, aliased):
    (at_ref, lr_ref, x_ref, gm_ref, shf_ref, scf_ref, lnf_ref, wo_ref, wr_ref, br_ref) = refs[:10]
    x1_ref, h2_ref, route_ref, cnt_ref = refs[10 + aliased:]
    nb, tr, d = x_ref.shape
    m = nb * tr
    aw = at_ref.shape[-1]
    ne = wr_ref.shape[0]
    at = at_ref[...].reshape(m, aw)
    lr = lr_ref[...].reshape(m, aw)
    mix = (jnp.dot(at, wo_ref[0:aw, :], preferred_element_type=F32)
           + jnp.dot(lr, wo_ref[aw:2 * aw, :], preferred_element_type=F32))
    x1 = x_ref[...] + gm_ref[...] * mix.reshape(nb, tr, d)
    x1_ref[...] = x1
    ms = jnp.mean(x1 * x1, axis=-1, keepdims=True)
    h2 = x1 * lax.rsqrt(ms + NORM_EPS) * lnf_ref[...]
    h2 = (h2 * (1.0 + scf_ref[...]) + shf_ref[...]).reshape(m, d)
    h2_ref[...] = h2.astype(BF16)

    logits = lax.dot_general(wr_ref[...], h2, (((1,), (1,)), ((), ())),
                             precision=lax.Precision.HIGHEST, preferred_element_type=F32) + br_ref[...]
    e_iota = lax.broadcasted_iota(jnp.int32, (ne, m), 0).astype(F32)
    vals = logits
    top_v, sels = [], []
    for k in range(TOP_K):
        mx = jnp.max(vals, axis=0, keepdims=True)
        idx = jnp.min(jnp.where(vals == mx, e_iota, float(ne)), axis=0, keepdims=True)
        sel = e_iota == idx
        vals = jnp.where(sel, -jnp.inf, vals)
        top_v.append(mx)
        sels.append(sel)
        route_ref[0, k:k + 1, :] = idx
    ex = [jnp.exp(v - top_v[0]) for v in top_v]
    den = ex[0] + ex[1] + ex[2] + ex[3]
    chosen = jnp.zeros((ne, m), F32)
    for k in range(TOP_K):
        route_ref[0, 2 * TOP_K + k:2 * TOP_K + k + 1, :] = ex[k] / den
        chosen = chosen + jnp.where(sels[k], 1.0, 0.0)
    before = (lax.broadcasted_iota(jnp.int32, (m, m), 0) < lax.broadcasted_iota(jnp.int32, (m, m), 1))
    rank = jnp.dot(chosen.astype(BF16), jnp.where(before, 1.0, 0.0).astype(BF16), preferred_element_type=F32)
    for k in range(TOP_K):
        route_ref[0, TOP_K + k:TOP_K + k + 1, :] = jnp.sum(jnp.where(sels[k], rank, 0.0), axis=0, keepdims=True)
    route_ref[0, 3 * TOP_K:4 * TOP_K, :] = jnp.zeros((TOP_K, m), F32)
    cnt_ref[0] = jnp.broadcast_to(jnp.sum(chosen, axis=1, keepdims=True), (ne, LANES))


def _outproj(attn, lru_o, x, gm, shf, scf, lnf, w_out_bf, wr_t, br, nb, tr, n_tiles, tile0, prev):
    nbt, t, d = x.shape
    aw = attn.shape[-1]
    m = nb * tr
    assert m == TOKEN_TILE
    ne = wr_t.shape[0]
    tiles_per_seq = t // tr
    xmap = lambda b, i: (b, i, 0)
    mmap = lambda b, i: (b, 0, 0)
    c2 = lambda b, i: (0, 0)
    tile = lambda b, i: (tile0 + b * tiles_per_seq + i, 0)
    tile3 = lambda b, i: (tile0 + b * tiles_per_seq + i, 0, 0)
    mod = pl.BlockSpec((nb, 1, d), mmap)
    in_specs = [pl.BlockSpec((nb, tr, aw), xmap), pl.BlockSpec((nb, tr, aw), xmap),
                pl.BlockSpec((nb, tr, d), xmap), mod, mod, mod,
                pl.BlockSpec((1, d), c2), pl.BlockSpec(w_out_bf.shape, c2),
                pl.BlockSpec(wr_t.shape, c2), pl.BlockSpec((ne, 1), c2)]
    args = [attn, lru_o, x, gm, shf, scf, lnf, w_out_bf, wr_t, br]
    aliases = {}
    if prev is not None:
        in_specs += [pl.BlockSpec(memory_space=pl.ANY)] * 3
        args += list(prev)
        aliases = {10: 1, 11: 2, 12: 3}
    return pl.pallas_call(
        functools.partial(_outproj_kernel, aliased=len(aliases)),
        grid=(nbt // nb, tiles_per_seq),
        in_specs=in_specs,
        out_specs=[pl.BlockSpec((nb, tr, d), xmap), pl.BlockSpec((m, d), tile),
                   pl.BlockSpec((1, 4 * TOP_K, m), tile3), pl.BlockSpec((1, ne, LANES), tile3)],
        out_shape=[jax.ShapeDtypeStruct((nbt, t, d), F32),
                   jax.ShapeDtypeStruct((n_tiles * m, d), BF16),
                   jax.ShapeDtypeStruct((n_tiles, 4 * TOP_K, m), F32),
                   jax.ShapeDtypeStruct((n_tiles, ne, LANES), F32)],
        input_output_aliases=aliases,
        compiler_params=_params(("arbitrary", "arbitrary")),
        name="outproj",
    )(*args)


def _tile_rows(m):
    cap = TOP_K * m + N_EXPERTS * (BF16_ROWS - 1) + BF16_ROWS
    return -(-cap // TOKEN_TILE) * TOKEN_TILE


def _table_sizes(nt):
    g = BF16_ROWS
    m = TOKEN_TILE
    n_chunks = _tile_rows(m) // g
    n_gap = -(-(N_EXPERTS * (EXPERT_ROWS // g - 1)) // nt)
    bound = TOP_K * m * nt + nt * N_EXPERTS * (g - 1) + N_EXPERTS * (EXPERT_ROWS - g)
    n_sorted = -(-bound // EXPERT_ROWS) * EXPERT_ROWS
    return n_chunks, n_gap, n_sorted, n_sorted + 2 * (n_chunks + n_gap) * g


def _route_tables(cnt):
    nt = cnt.shape[0]
    g = BF16_ROWS
    bm = EXPERT_ROWS
    n_chunks, n_gap, n_sorted, _ = _table_sizes(nt)
    e_ids = jnp.arange(N_EXPERTS, dtype=jnp.int32)
    t_ids = jnp.arange(nt, dtype=jnp.int32)
    upto = (e_ids[:, None] <= e_ids[None, :]).astype(jnp.int32)
    pc = (cnt + g - 1) // g * g
    ctile = jnp.sum(pc[:, :, None] * upto[None], axis=1)
    toff = ctile - pc
    trow = ctile[:, -1]
    tot = jnp.sum(pc, axis=0)
    reg = (tot + bm - 1) // bm * bm
    creg = jnp.sum(reg[:, None] * upto, axis=0)
    base = creg - reg
    earlier = (t_ids[:, None] < t_ids[None, :]).astype(jnp.int32)
    goff = base[None, :] + jnp.sum(pc[:, None, :] * earlier[:, :, None], axis=0)
    r = jnp.arange(n_chunks, dtype=jnp.int32) * g
    r3 = r[None, :, None]
    in_seg = (toff[:, None, :] <= r3) & (r3 < ctile[:, None, :])
    dst = jnp.sum(jnp.where(in_seg, (goff - toff)[:, None, :], 0), axis=2) + r[None, :]
    dst = jnp.where(r[None, :] < trow[:, None], dst, -1)
    gcnt = (reg - tot) // g
    gcum = jnp.sum(gcnt[:, None] * upto, axis=0)
    gstart = gcum - gcnt
    s = jnp.arange(nt * n_gap, dtype=jnp.int32)
    in_gap = (gstart[None, :] <= s[:, None]) & (s[:, None] < gcum[None, :])
    gdst = jnp.sum(jnp.where(in_gap, (base + tot - g * gstart)[None, :] + g * s[:, None], 0), axis=1)
    gdst = jnp.where(s < gcum[-1], gdst, -1).reshape(nt, n_gap)
    table = jnp.concatenate([dst, gdst], axis=1).astype(jnp.int32)
    n_entries = n_chunks + n_gap
    spare = n_sorted + ((t_ids % 2)[:, None] * n_entries + jnp.arange(n_entries, dtype=jnp.int32)[None, :]) * g
    dispatch_tab = jnp.where(table >= 0, table, spare).astype(jnp.int32)
    combine_tab = jnp.maximum(dst, 0).astype(jnp.int32)
    toff_b = jnp.broadcast_to(toff.astype(F32)[:, :, None], (nt, N_EXPERTS, LANES))
    return dispatch_tab, combine_tab, toff_b, base.astype(jnp.int32), (reg // bm).astype(jnp.int32)


def _slot_rows(route_ref, toff_ref, m):
    ne = toff_ref.shape[1]
    e_iota = lax.broadcasted_iota(jnp.int32, (ne, m), 0).astype(F32)
    toff_col = toff_ref[0][:, 0:1]
    pos = []
    for k in range(TOP_K):
        sel = e_iota == route_ref[0, k:k + 1, :]
        start = jnp.sum(jnp.where(sel, toff_col, 0.0), axis=0, keepdims=True)
        pos.append(start + route_ref[0, TOP_K + k:TOP_K + k + 1, :])
    return pos


def _dispatch_kernel(tab_ref, h2_ref, route_ref, toff_ref, xs_hbm, buf_ref, sem, *, n_chunks, n_tiles):
    t = pl.program_id(0)
    slot = t % 2
    m = h2_ref.shape[0]
    rows = buf_ref.shape[1]
    n_entries = tab_ref.shape[1]
    g = BF16_ROWS
    per_chunk = m // g

    def start(c):
        src = c * g if c < n_chunks else rows - g
        pltpu.make_async_copy(
            buf_ref.at[slot, pl.ds(src, g)],
            xs_hbm.at[pl.ds(pl.multiple_of(tab_ref[t, c], g), g)], sem.at[slot]).start()

    def wait_all(which):
        for _ in range(n_entries):
            pltpu.make_async_copy(buf_ref.at[which, pl.ds(0, g)], xs_hbm.at[pl.ds(0, g)], sem.at[which]).wait()

    @pl.when(t >= 2)
    def _():
        wait_all(slot)

    pos = _slot_rows(route_ref, toff_ref, m)
    h2 = h2_ref[...]
    for rc in range(rows // m):
        r_iota = (lax.broadcasted_iota(jnp.int32, (m, m), 0) + rc * m).astype(F32)
        onehot = jnp.zeros((m, m), F32)
        for k in range(TOP_K):
            onehot = jnp.where(r_iota == pos[k], 1.0, onehot)
        buf_ref[slot, rc * m:(rc + 1) * m, :] = jnp.dot(
            onehot.astype(BF16), h2, preferred_element_type=F32).astype(BF16)
        for c in range((rc - 1) * per_chunk, rc * per_chunk) if rc > 0 else ():
            start(c)
    for c in range(n_chunks - per_chunk, n_entries):
        start(c)

    @pl.when(t == n_tiles - 1)
    def _():
        if n_tiles > 1:
            wait_all(1 - slot)
        wait_all(slot)


def _dispatch(table, h2, route, toff_b, n_rows, n_chunks):
    nt = route.shape[0]
    m = TOKEN_TILE
    d = h2.shape[1]
    rows = _tile_rows(m)
    grid_spec = pltpu.PrefetchScalarGridSpec(
        num_scalar_prefetch=1,
        grid=(nt,),
        in_specs=[pl.BlockSpec((m, d), lambda t, tab: (t, 0)),
                  pl.BlockSpec((1, 4 * TOP_K, m), lambda t, tab: (t, 0, 0)),
                  pl.BlockSpec((1, N_EXPERTS, LANES), lambda t, tab: (t, 0, 0))],
        out_specs=pl.BlockSpec(memory_space=pl.ANY),
        scratch_shapes=[pltpu.VMEM((2, rows, d), BF16), pltpu.SemaphoreType.DMA((2,))],
    )
    return pl.pallas_call(
        functools.partial(_dispatch_kernel, n_chunks=n_chunks, n_tiles=nt),
        grid_spec=grid_spec,
        out_shape=jax.ShapeDtypeStruct((n_rows, d), BF16),
        compiler_params=_params(("arbitrary",)),
        name="dispatch",
    )(table, h2, route, toff_b)


def _expert_kernel(row0_ref, nblk_ref, xs_hbm, wgu_ref, bgu_ref, wdn_ref, bdn_ref, ys_hbm,
                   wgu_bf, wdn_bf, xbuf, ybuf, sem_in, sem_out):
    e = pl.program_id(0)
    ne = pl.num_programs(0)
    bm = xbuf.shape[1]
    dff = wdn_ref.shape[1]
    nblk = nblk_ref[e]

    def in_copy(expert, j, slot):
        start = pl.multiple_of(row0_ref[expert] + j * bm, bm)
        return pltpu.make_async_copy(xs_hbm.at[pl.ds(start, bm)], xbuf.at[slot], sem_in.at[slot])

    def out_copy(j, slot):
        start = pl.multiple_of(row0_ref[e] + j * bm, bm)
        return pltpu.make_async_copy(ybuf.at[slot], ys_hbm.at[pl.ds(start, bm)], sem_out.at[slot])

    @pl.when((e == 0) & (nblk > 0))
    def _():
        in_copy(e, 0, 0).start()

    wgu_bf[...] = wgu_ref[0].astype(BF16)
    wdn_bf[...] = wdn_ref[0].astype(BF16)

    def block(j, carry):
        slot = j % 2
        in_copy(e, j, slot).wait()

        @pl.when(j + 1 < nblk)
        def _():
            in_copy(e, j + 1, 1 - slot).start()

        @pl.when(j >= 2)
        def _():
            out_copy(j - 2, slot).wait()

        gu = jnp.dot(xbuf[slot], wgu_bf[...], preferred_element_type=F32) + bgu_ref[0]
        gate = jnp.minimum(gu[:, :dff], SWIGLU_LIMIT)
        up = jnp.clip(gu[:, dff:], -SWIGLU_LIMIT, SWIGLU_LIMIT)
        glu = gate * jax.nn.sigmoid(gate * SWIGLU_ALPHA)
        act = ((up + 1.0) * glu).astype(BF16)
        ybuf[slot] = (jnp.dot(act, wdn_bf[...], preferred_element_type=F32) + bdn_ref[0]).astype(BF16)
        out_copy(j, slot).start()
        return carry

    lax.fori_loop(0, nblk, block, 0)

    nxt = jnp.minimum(e + 1, ne - 1)

    @pl.when((e + 1 < ne) & (nblk_ref[nxt] > 0))
    def _():
        in_copy(nxt, 0, 0).start()

    @pl.when(nblk >= 2)
    def _():
        out_copy(nblk - 2, nblk % 2).wait()

    @pl.when(nblk >= 1)
    def _():
        out_copy(nblk - 1, (nblk - 1) % 2).wait()


def _experts(row0, nblk, xs, w_gu, b_gu, w_dn, b_dn):
    n_rows, d = xs.shape
    bm = EXPERT_ROWS
    ne, _, dff2 = w_gu.shape
    dff = w_dn.shape[1]
    exp3 = lambda e, r0, nb: (e, 0, 0)
    grid_spec = pltpu.PrefetchScalarGridSpec(
        num_scalar_prefetch=2,
        grid=(ne,),
        in_specs=[pl.BlockSpec(memory_space=pl.ANY),
                  pl.BlockSpec((1, d, dff2), exp3), pl.BlockSpec((1, 1, dff2), exp3),
                  pl.BlockSpec((1, dff, d), exp3), pl.BlockSpec((1, 1, d), exp3)],
        out_specs=pl.BlockSpec(memory_space=pl.ANY),
        scratch_shapes=[pltpu.VMEM((d, dff2), BF16), pltpu.VMEM((dff, d), BF16),
                        pltpu.VMEM((2, bm, d), BF16), pltpu.VMEM((2, bm, d), BF16),
                        pltpu.SemaphoreType.DMA((2,)), pltpu.SemaphoreType.DMA((2,))],
    )
    return pl.pallas_call(
        _expert_kernel,
        grid_spec=grid_spec,
        out_shape=jax.ShapeDtypeStruct((n_rows, d), BF16),
        compiler_params=_params(("arbitrary",)),
        name="experts",
    )(row0, nblk, xs, w_gu, b_gu.reshape(ne, 1, dff2), w_dn, b_dn.reshape(ne, 1, d))


def _combine_kernel(tab_ref, ys_hbm, route_ref, toff_ref, x1_ref, gf_ref, o_ref, buf_ref, sem, *, n_chunks, tile0):
    step = pl.program_id(0)
    n_steps = pl.num_programs(0)
    t = tile0 + step
    slot = step % 2
    nb, tr, d = x1_ref.shape
    m = nb * tr
    rows = buf_ref.shape[1]
    g = BF16_ROWS

    def fetch(tile, which):
        for c in range(n_chunks):
            pltpu.make_async_copy(
                ys_hbm.at[pl.ds(pl.multiple_of(tab_ref[tile, c], g), g)],
                buf_ref.at[which, pl.ds(c * g, g)], sem.at[which]).start()

    @pl.when(step == 0)
    def _():
        fetch(t, slot)

    @pl.when(step + 1 < n_steps)
    def _():
        fetch(t + 1, 1 - slot)

    pos = _slot_rows(route_ref, toff_ref, m)
    gates = [route_ref[0, 2 * TOP_K + k:2 * TOP_K + k + 1, :] for k in range(TOP_K)]
    stacked = jnp.concatenate(pos + gates + [jnp.zeros((LANES - 2 * TOP_K, m), F32)], axis=0)
    cols = stacked.T
    for _ in range(n_chunks):
        pltpu.make_async_copy(ys_hbm.at[pl.ds(0, g)], buf_ref.at[slot, pl.ds(0, g)], sem.at[slot]).wait()

    acc = jnp.zeros((m, d), F32)
    for rc in range(rows // m):
        c_iota = (lax.broadcasted_iota(jnp.int32, (m, m), 1) + rc * m).astype(F32)
        weights = jnp.zeros((m, m), F32)
        for k in range(TOP_K):
            weights = jnp.where(c_iota == cols[:, k:k + 1], cols[:, TOP_K + k:TOP_K + k + 1], weights)
        acc = acc + jnp.dot(weights.astype(BF16), buf_ref[slot, rc * m:(rc + 1) * m, :],
                            preferred_element_type=F32)
    o_ref[...] = x1_ref[...] + gf_ref[...] * acc.reshape(nb, tr, d)


def _combine(table, ys, route, toff_b, x1, gf, nb, tr, tile0, n_chunks):
    nbt, t, d = x1.shape
    m = nb * tr
    assert m == TOKEN_TILE
    rows = _tile_rows(m)
    tiles_per_seq = t // tr
    n_steps = (nbt // nb) * tiles_per_seq
    xmap = lambda s, tab: (s // tiles_per_seq, s % tiles_per_seq, 0)
    grid_spec = pltpu.PrefetchScalarGridSpec(
        num_scalar_prefetch=1,
        grid=(n_steps,),
        in_specs=[pl.BlockSpec(memory_space=pl.ANY),
                  pl.BlockSpec((1, 4 * TOP_K, m), lambda s, tab: (tile0 + s, 0, 0)),
                  pl.BlockSpec((1, N_EXPERTS, LANES), lambda s, tab: (tile0 + s, 0, 0)),
                  pl.BlockSpec((nb, tr, d), xmap),
                  pl.BlockSpec((nb, 1, d), lambda s, tab: (s // tiles_per_seq, 0, 0))],
        out_specs=pl.BlockSpec((nb, tr, d), xmap),
        scratch_shapes=[pltpu.VMEM((2, rows, d), BF16), pltpu.SemaphoreType.DMA((2,))],
    )
    return pl.pallas_call(
        functools.partial(_combine_kernel, n_chunks=n_chunks, tile0=tile0),
        grid_spec=grid_spec,
        out_shape=jax.ShapeDtypeStruct((nbt, t, d), F32),
        compiler_params=_params(("arbitrary",)),
        name="combine",
    )(table, ys, route, toff_b, x1, gf)


def _block_diag(w, groups):
    n, k, _ = w.shape
    w = w.reshape(n // groups, groups, k, k)
    eye = jnp.eye(groups, dtype=w.dtype)
    return jnp.einsum("ngij,gh->ngihj", w, eye).reshape(n // groups, groups * k, groups * k)


def _layer(xp, xs, mod, k_cache, v_cache, conv_state, lru_state, lw):
    (ln_mix, ln_ffn, w_in, q_norm, k_norm, rel_bias, conv_w, conv_b, w_rg, b_rg, w_ig, b_ig, lam,
     w_out, w_router, b_router, w_gu, b_gu, w_dn, b_dn) = lw
    bp, s, d = xp.shape
    bs, ts, _ = xs.shape
    aw = w_out.shape[0] // 2
    nh = aw // HEAD_DIM
    m = TOKEN_TILE
    assert s % m == 0 and bs * ts == m and s % ATTN_Q_TILE == 0 and s % LRU_TILE == 0

    terms = [mod[:, i * d:(i + 1) * d][:, None, :] for i in range(6)]
    tp = [t[:bp] for t in terms]
    tsm = [t[bp:] for t in terms]

    w_in_bf = w_in.astype(BF16)
    w_out_bf = w_out.astype(BF16)
    qn_t = jnp.tile(q_norm, nh).reshape(1, aw)
    kn_t = jnp.tile(k_norm, nh).reshape(1, aw)
    head_mean = _block_diag(jnp.full((nh, HEAD_DIM, HEAD_DIM), 1.0 / HEAD_DIM, F32), nh)[0].astype(BF16)
    groups = MXU_DIM // w_rg.shape[-1]
    wa_bd = _block_diag(w_rg, groups).astype(BF16)
    wx_bd = _block_diag(w_ig, groups).astype(BF16)
    lw_c = b_rg.size
    b_a = b_rg.reshape(1, lw_c)
    b_x = b_ig.reshape(1, lw_c)
    lam2 = lam.reshape(1, lw_c)
    cb2 = conv_b.reshape(1, lw_c)
    ln_mix2 = ln_mix.reshape(1, d)
    ln_ffn2 = ln_ffn.reshape(1, d)
    wr_t = w_router.T
    br = b_router.reshape(-1, 1)
    tab_p = _bias_table(rel_bias, 3 * ATTN_Q_TILE - 1)
    r_cache = k_cache.shape[1]
    tab_s = _bias_table(rel_bias, r_cache + ts - 1)

    qp, kp, vp, xrp, ygp, k32p, v32p = _mixin(xp, tp[0], tp[1], ln_mix2, w_in_bf, qn_t, kn_t, head_mean, 1, m)
    qs, ks, vs, xrs, ygs, k32s, v32s = _mixin(xs, tsm[0], tsm[1], ln_mix2, w_in_bf, qn_t, kn_t, head_mean, bs, ts)
    attn_p = _attn_prompt(qp, kp, vp, tab_p)
    attn_s = _attn_step(qs, ks, vs, k_cache.reshape(bs, r_cache, aw), v_cache.reshape(bs, r_cache, aw), tab_s)

    zeros_pre = jnp.zeros((bp, SUBLANES, lw_c), F32)
    zeros_h = jnp.zeros((bp, 1, lw_c), F32)
    pre_s = jnp.pad(conv_state, ((0, 0), (SUBLANES - (CONV_WIDTH - 1), 0), (0, 0)))
    lru_p, tail_p, hl_p = _lru(xrp, ygp, zeros_pre, zeros_h, conv_w, cb2, wa_bd, wx_bd, b_a, b_x, lam2, 1, LRU_TILE)
    lru_s, tail_s, hl_s = _lru(xrs, ygs, pre_s, lru_state[:, None, :], conv_w, cb2, wa_bd, wx_bd, b_a, b_x, lam2,
                               bs, ts)

    n_tiles = bp * (s // m) + 1
    x1p, h2, route, cnt = _outproj(attn_p, lru_p, xp, tp[2], tp[3], tp[4], ln_ffn2, w_out_bf, wr_t, br,
                                   1, m, n_tiles, 0, None)
    x1s, h2, route, cnt = _outproj(attn_s, lru_s, xs, tsm[2], tsm[3], tsm[4], ln_ffn2, w_out_bf, wr_t, br,
                                   bs, ts, n_tiles, n_tiles - 1, (h2, route, cnt))

    n_chunks, _, _, n_rows = _table_sizes(n_tiles)
    dispatch_tab, combine_tab, toff_b, row0, nblk = _route_tables(cnt[:, :, 0].astype(jnp.int32))
    xs_sorted = _dispatch(dispatch_tab, h2, route, toff_b, n_rows, n_chunks)
    ys_sorted = _experts(row0, nblk, xs_sorted, w_gu, b_gu, w_dn, b_dn)
    yp = _combine(combine_tab, ys_sorted, route, toff_b, x1p, tp[5], 1, m, 0, n_chunks)
    ysm = _combine(combine_tab, ys_sorted, route, toff_b, x1s, tsm[5], bs, ts, n_tiles - 1, n_chunks)

    keep = k32p.shape[1]
    new = (k32p.reshape(bp, keep, nh, HEAD_DIM), v32p.reshape(bp, keep, nh, HEAD_DIM),
           tail_p[:, SUBLANES - (CONV_WIDTH - 1):, :], hl_p[:, 0, :],
           k32s.reshape(bs, ts, nh, HEAD_DIM), v32s.reshape(bs, ts, nh, HEAD_DIM),
           tail_s[:, SUBLANES - (CONV_WIDTH - 1):, :], hl_s[:, 0, :])
    return yp, ysm, new


def kernel(x_prompt, x_sample, c_prompt, c_sample, cache_k, cache_v, state_conv, state_lru, ln_mix_w, ln_ffn_w, w_ada, b_ada, w_in, q_norm_w, k_norm_w, rel_bias, conv_w, conv_b, w_rgate, b_rgate, w_igate, b_igate, lru_lambda, w_out, w_router, b_router, w_gate_up, b_gate_up, w_down, b_down):
    depth = w_in.shape[0]
    yp, ys = x_prompt, x_sample
    c_all = jnp.concatenate([c_prompt, c_sample], axis=0)
    collected = [[] for _ in range(8)]
    for l in range(depth):
        mod = _ada(c_all, w_ada[l], b_ada[l])
        lw = (ln_mix_w[l], ln_ffn_w[l], w_in[l], q_norm_w[l], k_norm_w[l], rel_bias[l], conv_w[l], conv_b[l],
              w_rgate[l], b_rgate[l], w_igate[l], b_igate[l], lru_lambda[l], w_out[l], w_router[l], b_router[l],
              w_gate_up[l], b_gate_up[l], w_down[l], b_down[l])
        yp, ys, new = _layer(yp, ys, mod, cache_k[l], cache_v[l], state_conv[l], state_lru[l], lw)
        for acc, val in zip(collected, new):
            acc.append(val)
    return (yp, ys) + tuple(jnp.stack(vals) for vals in collected)
```

```python
import functools

import jax
import jax.numpy as jnp
from jax import lax
from jax.experimental import pallas as pl
from jax.experimental.pallas import tpu as pltpu

F32 = jnp.float32
BF16 = jnp.bfloat16

CHUNK = 64
N_LEFT_CHUNKS = 8
ATTN_WINDOW = N_LEFT_CHUNKS * CHUNK
HEAD_DIM = 64
REL_CLIP = 128
CONV_WIDTH = 4
LRU_C = 8.0
N_EXPERTS = 32
TOP_K = 4
SWIGLU_LIMIT = 7.0
SWIGLU_ALPHA = 1.702
NORM_EPS = 1e-6
NEG_INF = -1e30

LANES = 128
SUBLANES = 8
BF16_ROWS = 16
MXU_DIM = 256

TOKEN_TILE = 512
ATTN_Q_TILE = 256
LRU_TILE = 256
ATTN_STEP_BATCH = 4
EXPERT_ROWS = 256
BIAS_TABLE = 1024
VMEM_LIMIT = 56 * 1024 * 1024


def _params(sem, vmem=VMEM_LIMIT):
    return pltpu.CompilerParams(dimension_semantics=sem, vmem_limit_bytes=vmem)


def _ada_kernel(c_ref, w_ref, b_ref, o_ref):
    c = c_ref[...]
    s = (c * jax.nn.sigmoid(c)).astype(BF16)
    o_ref[...] = jnp.dot(s, w_ref[...].astype(BF16), preferred_element_type=F32) + b_ref[...]


def _ada(c_all, w_ada, b_ada):
    n, d = c_all.shape
    nout = w_ada.shape[1]
    tn = 1024
    return pl.pallas_call(
        _ada_kernel,
        grid=(nout // tn,),
        in_specs=[pl.BlockSpec((n, d), lambda j: (0, 0)),
                  pl.BlockSpec((d, tn), lambda j: (0, j)),
                  pl.BlockSpec((1, tn), lambda j: (0, j))],
        out_specs=pl.BlockSpec((n, tn), lambda j: (0, j)),
        out_shape=jax.ShapeDtypeStruct((n, nout), F32),
        compiler_params=_params(("arbitrary",)),
        name="ada",
    )(c_all, w_ada, b_ada.reshape(1, nout))


def _mixin_kernel(x_ref, sh_ref, sc_ref, ln_ref, win_ref, qn_ref, kn_ref, bd_ref,
                  q_ref, k_ref, v_ref, xr_ref, yg_ref, k32_ref, v32_ref):
    nb, tr, d = x_ref.shape
    m = nb * tr
    aw = q_ref.shape[-1]
    x = x_ref[...]
    ms = jnp.mean(x * x, axis=-1, keepdims=True)
    h = x * lax.rsqrt(ms + NORM_EPS) * ln_ref[...]
    h = h * (1.0 + sc_ref[...]) + sh_ref[...]
    hb = h.reshape(m, d).astype(BF16)

    def proj(part):
        return jnp.dot(hb, win_ref[:, part * aw:(part + 1) * aw], preferred_element_type=F32)

    def head_norm(t, w_ref):
        msq = jnp.dot((t * t).astype(BF16), bd_ref[...], preferred_element_type=F32)
        return t * lax.rsqrt(msq + NORM_EPS) * w_ref[...]

    q = head_norm(proj(0), qn_ref)
    k = head_norm(proj(1), kn_ref)
    v = proj(2)
    q_ref[...] = (q * (HEAD_DIM ** -0.5)).astype(BF16).reshape(nb, tr, aw)
    k_ref[...] = k.astype(BF16).reshape(nb, tr, aw)
    v_ref[...] = v.astype(BF16).reshape(nb, tr, aw)
    k32_ref[...] = k.reshape(nb, tr, aw)
    v32_ref[...] = v.reshape(nb, tr, aw)
    xr_ref[...] = proj(3).reshape(nb, tr, aw)
    yg_ref[...] = proj(4).reshape(nb, tr, aw)


def _mixin(x, sh, sc, ln_w, w_in_bf, qn_t, kn_t, bd, nb, tr):
    nbt, t, d = x.shape
    aw = qn_t.shape[-1]
    keep = min(ATTN_WINDOW, t)
    assert tr == keep or t == tr
    grid = (nbt // nb, t // tr)
    xmap = lambda b, i: (b, i, 0)
    mmap = lambda b, i: (b, 0, 0)
    cmap = lambda b, i: (0, 0)
    tmap = lambda b, i: (b, 0, 0)
    big = pl.BlockSpec((nb, tr, aw), xmap)
    tail = pl.BlockSpec((nb, keep, aw), tmap)
    return pl.pallas_call(
        _mixin_kernel,
        grid=grid,
        in_specs=[pl.BlockSpec((nb, tr, d), xmap),
                  pl.BlockSpec((nb, 1, d), mmap), pl.BlockSpec((nb, 1, d), mmap),
                  pl.BlockSpec((1, d), cmap),
                  pl.BlockSpec(w_in_bf.shape, cmap),
                  pl.BlockSpec((1, aw), cmap), pl.BlockSpec((1, aw), cmap),
                  pl.BlockSpec(bd.shape, cmap)],
        out_specs=[big, big, big, big, big, tail, tail],
        out_shape=[jax.ShapeDtypeStruct((nbt, t, aw), BF16)] * 3
        + [jax.ShapeDtypeStruct((nbt, t, aw), F32)] * 2
        + [jax.ShapeDtypeStruct((nbt, keep, aw), F32)] * 2,
        compiler_params=_params(("arbitrary", "arbitrary")),
        name="mixin",
    )(x, sh, sc, ln_w, w_in_bf, qn_t, kn_t, bd)


def _bias_table(rel_bias, off):
    h = rel_bias.shape[0]
    left = off - REL_CLIP
    right = BIAS_TABLE - left - (2 * REL_CLIP + 1)
    assert left >= 0 and right >= 0
    return jnp.concatenate([jnp.broadcast_to(rel_bias[:, :1], (h, left)), rel_bias,
                            jnp.broadcast_to(rel_bias[:, -1:], (h, right))], axis=1)


def _toeplitz(tab_row, rows, cols):
    t = jnp.broadcast_to(tab_row, (rows, BIAS_TABLE))
    t = pltpu.roll(t, BIAS_TABLE - (rows - 1), 1, stride=1, stride_axis=0)
    return t[:, :cols]


def _attn_kernel(q_ref, k0_ref, k1_ref, k2_ref, v0_ref, v1_ref, v2_ref, tab_ref, o_ref, bias_ref):
    b = pl.program_id(0)
    s = pl.program_id(1)
    qt = q_ref.shape[1]
    nk = 3 * qt
    nh = bias_ref.shape[0]

    @pl.when((b == 0) & (s == 0))
    def _():
        qi = lax.broadcasted_iota(jnp.int32, (qt, nk), 0) // CHUNK
        kc = lax.broadcasted_iota(jnp.int32, (qt, nk), 1) // CHUNK
        for h in range(nh):
            band = jnp.where(kc <= qi + N_LEFT_CHUNKS, _toeplitz(tab_ref[h:h + 1, :], qt, nk), NEG_INF)
            bias_ref[h] = jnp.where(kc >= qi, band, NEG_INF)

    q = q_ref[0]
    kcat = jnp.concatenate([k0_ref[0], k1_ref[0], k2_ref[0]], axis=0)
    vcat = jnp.concatenate([v0_ref[0], v1_ref[0], v2_ref[0]], axis=0)
    in_seq = lax.broadcasted_iota(jnp.int32, (qt, nk), 1) >= (2 - s) * qt
    outs = []
    for h in range(nh):
        sl = slice(h * HEAD_DIM, (h + 1) * HEAD_DIM)
        sc = lax.dot_general(q[:, sl], kcat[:, sl], (((1,), (1,)), ((), ())), preferred_element_type=F32)
        sc = jnp.where(in_seq, sc + bias_ref[h], NEG_INF)
        mx = jnp.max(sc, axis=-1, keepdims=True)
        p = jnp.exp(sc - mx)
        l = jnp.sum(p, axis=-1, keepdims=True)
        o = jnp.dot(p.astype(BF16), vcat[:, sl], preferred_element_type=F32)
        outs.append(o / l)
    o_ref[0] = jnp.concatenate(outs, axis=-1).astype(BF16)


def _attn_prompt(q, k, v, tab):
    b, s, aw = q.shape
    qt = ATTN_Q_TILE
    nh = aw // HEAD_DIM
    qspec = pl.BlockSpec((1, qt, aw), lambda i, j: (i, j, 0))

    def kspec(back):
        return pl.BlockSpec((1, qt, aw), lambda i, j: (i, jnp.maximum(j - back, 0), 0))

    return pl.pallas_call(
        _attn_kernel,
        grid=(b, s // qt),
        in_specs=[qspec, kspec(2), kspec(1), kspec(0), kspec(2), kspec(1), kspec(0),
                  pl.BlockSpec(tab.shape, lambda i, j: (0, 0))],
        out_specs=qspec,
        out_shape=jax.ShapeDtypeStruct((b, s, aw), BF16),
        scratch_shapes=[pltpu.VMEM((nh, qt, 3 * qt), F32)],
        compiler_params=_params(("arbitrary", "arbitrary")),
        name="attn_prompt",
    )(q, k, k, k, v, v, v, tab)


def _attn_step_kernel(q_ref, kn_ref, vn_ref, ck_ref, cv_ref, tab_ref, o_ref, bias_ref):
    step = pl.program_id(0)
    nbs, t, _ = q_ref.shape
    nh = bias_ref.shape[0]
    r = ck_ref.shape[1] // nh
    nk = r + LANES

    @pl.when(step == 0)
    def _():
        ok = lax.broadcasted_iota(jnp.int32, (t, nk), 1) < r + t
        for h in range(nh):
            bias_ref[h] = jnp.where(ok, _toeplitz(tab_ref[h:h + 1, :], t, nk), NEG_INF)

    pad = jnp.zeros((LANES - t, HEAD_DIM), BF16)
    for b in range(nbs):
        q = q_ref[b]
        kn = kn_ref[b]
        vn = vn_ref[b]
        outs = []
        for h in range(nh):
            sl = slice(h * HEAD_DIM, (h + 1) * HEAD_DIM)
            kc = ck_ref[b, pl.ds(h, r, stride=nh), :].astype(BF16)
            vc = cv_ref[b, pl.ds(h, r, stride=nh), :].astype(BF16)
            kcat = jnp.concatenate([kc, kn[:, sl], pad], axis=0)
            vcat = jnp.concatenate([vc, vn[:, sl], pad], axis=0)
            sc = lax.dot_general(q[:, sl], kcat, (((1,), (1,)), ((), ())), preferred_element_type=F32)
            sc = sc + bias_ref[h]
            mx = jnp.max(sc, axis=-1, keepdims=True)
            p = jnp.exp(sc - mx)
            l = jnp.sum(p, axis=-1, keepdims=True)
            o = jnp.dot(p.astype(BF16), vcat, preferred_element_type=F32)
            outs.append(o / l)
        o_ref[b] = jnp.concatenate(outs, axis=-1).astype(BF16)


def _attn_step(q, kn, vn, ck, cv, tab, nbs):
    b, t, aw = q.shape
    nh = aw // HEAD_DIM
    r = ck.shape[1] // nh
    new = pl.BlockSpec((nbs, t, aw), lambda i: (i, 0, 0))
    old = pl.BlockSpec((nbs, r * nh, HEAD_DIM), lambda i: (i, 0, 0))
    return pl.pallas_call(
        _attn_step_kernel,
        grid=(b // nbs,),
        in_specs=[new, new, new, old, old, pl.BlockSpec(tab.shape, lambda i: (0, 0))],
        out_specs=new,
        out_shape=jax.ShapeDtypeStruct((b, t, aw), BF16),
        scratch_shapes=[pltpu.VMEM((nh, t, r + LANES), F32)],
        compiler_params=_params(("arbitrary",)),
        name="attn_step",
    )(q, kn, vn, ck, cv, tab)


def _gelu_tanh(x):
    return x * (0.5 * (1.0 + jnp.tanh(0.7978845608028654 * (x + 0.044715 * (x * x * x)))))


def _lru_kernel(xr_ref, yg_ref, pre_ref, h0_ref, cw_ref, cb_ref, wa_ref, wx_ref, ba_ref, bx_ref, lam_ref,
                o_ref, tail_ref, hl_ref, cx_ref, ch_ref):
    step = pl.program_id(1)
    nb, tr, c = xr_ref.shape
    m = nb * tr
    half = c // 2

    @pl.when(step == 0)
    def _():
        cx_ref[...] = pre_ref[...]
        ch_ref[...] = h0_ref[...]

    x = xr_ref[...]
    xp = jnp.concatenate([cx_ref[...], x], axis=1)
    new_tail = xp[:, tr:tr + SUBLANES, :]
    xp2 = xp.reshape(nb * (tr + SUBLANES), c)
    y = cb_ref[...] + cw_ref[CONV_WIDTH - 1:CONV_WIDTH, :] * x
    for back in range(1, CONV_WIDTH):
        shifted = pltpu.roll(xp2, back, 0).reshape(nb, tr + SUBLANES, c)[:, SUBLANES:, :]
        y = y + cw_ref[CONV_WIDTH - 1 - back:CONV_WIDTH - back, :] * shifted
    y2 = y.reshape(m, c)
    yb = y2.astype(BF16)

    def gate(w_ref, b_ref):
        g = jnp.concatenate(
            [jnp.dot(yb[:, :half], w_ref[0], preferred_element_type=F32),
             jnp.dot(yb[:, half:], w_ref[1], preferred_element_type=F32)], axis=1)
        return jax.nn.sigmoid(g + b_ref[...])

    rg = gate(wa_ref, ba_ref)
    ig = gate(wx_ref, bx_ref)
    lam = lam_ref[...]
    log_sig = jnp.minimum(lam, 0.0) - jnp.log1p(jnp.exp(-jnp.abs(lam)))
    log_a = LRU_C * rg * log_sig
    a_cum = jnp.exp(log_a)
    b_cum = jnp.sqrt(-jnp.tanh(log_a) * (a_cum * a_cum + 1.0)) * (ig * y2)
    row = lax.broadcasted_iota(jnp.int32, (m, c), 0) % SUBLANES
    dist = 1
    while dist < SUBLANES:
        keep = row >= dist
        a_sh = jnp.where(keep, pltpu.roll(a_cum, dist, 0), 1.0)
        b_sh = jnp.where(keep, pltpu.roll(b_cum, dist, 0), 0.0)
        b_cum = a_cum * b_sh + b_cum
        a_cum = a_cum * a_sh
        dist *= 2
    groups = tr // SUBLANES
    a_grp = a_cum.reshape(nb, groups, SUBLANES, c)
    b_grp = b_cum.reshape(nb, groups, SUBLANES, c)
    carry = ch_ref[...]
    pieces = []
    for grp in range(groups):
        h_grp = a_grp[:, grp] * carry + b_grp[:, grp]
        carry = h_grp[:, SUBLANES - 1:SUBLANES, :]
        pieces.append(h_grp)
    h = jnp.concatenate(pieces, axis=1)
    o_ref[...] = (h * _gelu_tanh(yg_ref[...])).astype(BF16)
    h_last = carry
    ch_ref[...] = h_last
    hl_ref[...] = h_last
    cx_ref[...] = new_tail
    tail_ref[...] = new_tail


def _lru(xr, yg, pre, h0, conv_w, conv_b, wa_bd, wx_bd, b_a, b_x, lam, nb, tr):
    nbt, t, c = xr.shape
    xmap = lambda b, i: (b, i, 0)
    smap = lambda b, i: (b, 0, 0)
    c2 = lambda b, i: (0, 0)
    c3 = lambda b, i: (0, 0, 0)
    big = pl.BlockSpec((nb, tr, c), xmap)
    return pl.pallas_call(
        _lru_kernel,
        grid=(nbt // nb, t // tr),
        in_specs=[big, big,
                  pl.BlockSpec((nb, SUBLANES, c), smap), pl.BlockSpec((nb, 1, c), smap),
                  pl.BlockSpec(conv_w.shape, c2), pl.BlockSpec((1, c), c2),
                  pl.BlockSpec(wa_bd.shape, c3), pl.BlockSpec(wx_bd.shape, c3),
                  pl.BlockSpec((1, c), c2), pl.BlockSpec((1, c), c2), pl.BlockSpec((1, c), c2)],
        out_specs=[big, pl.BlockSpec((nb, SUBLANES, c), smap), pl.BlockSpec((nb, 1, c), smap)],
        out_shape=[jax.ShapeDtypeStruct((nbt, t, c), BF16),
                   jax.ShapeDtypeStruct((nbt, SUBLANES, c), F32),
                   jax.ShapeDtypeStruct((nbt, 1, c), F32)],
        scratch_shapes=[pltpu.VMEM((nb, SUBLANES, c), F32), pltpu.VMEM((nb, 1, c), F32)],
        compiler_params=_params(("arbitrary", "arbitrary")),
        name="lru",
    )(xr, yg, pre, h0, conv_w, conv_b, wa_bd, wx_bd, b_a, b_x, lam)


def _outproj_kernel(*refs, aliased):
    (at_ref, lr_ref, x_ref, gm_ref, shf_ref, scf_ref, lnf_ref, wo_ref, wr_ref, br_ref) = refs[:10]
    x1_ref, h2_ref, route_ref, cnt_ref = refs[10 + aliased:]
    nb, tr, d = x_ref.shape
    m = nb * tr
    aw = at_ref.shape[-1]
    ne = wr_ref.shape[0]
    at = at_ref[...].reshape(m, aw)
    lr = lr_ref[...].reshape(m, aw)
    mix = (jnp.dot(at, wo_ref[0:aw, :], preferred_element_type=F32)
           + jnp.dot(lr, wo_ref[aw:2 * aw, :], preferred_element_type=F32))
    x1 = x_ref[...] + gm_ref[...] * mix.reshape(nb, tr, d)
    x1_ref[...] = x1
    ms = jnp.mean(x1 * x1, axis=-1, keepdims=True)
    h2 = x1 * lax.rsqrt(ms + NORM_EPS) * lnf_ref[...]
    h2 = (h2 * (1.0 + scf_ref[...]) + shf_ref[...]).reshape(m, d)
    h2_ref[...] = h2.astype(BF16)

    logits = lax.dot_general(wr_ref[...], h2, (((1,), (1,)), ((), ())),
                             precision=lax.Precision.HIGHEST, preferred_element_type=F32) + br_ref[...]
    e_iota = lax.broadcasted_iota(jnp.int32, (ne, m), 0).astype(F32)
    vals = logits
    top_v, sels = [], []
    for k in range(TOP_K):
        mx = jnp.max(vals, axis=0, keepdims=True)
        idx = jnp.min(jnp.where(vals == mx, e_iota, float(ne)), axis=0, keepdims=True)
        sel = e_iota == idx
        vals = jnp.where(sel, -jnp.inf, vals)
        top_v.append(mx)
        sels.append(sel)
        route_ref[0, k:k + 1, :] = idx
    ex = [jnp.exp(v - top_v[0]) for v in top_v]
    den = ex[0] + ex[1] + ex[2] + ex[3]
    chosen = jnp.zeros((ne, m), F32)
    for k in range(TOP_K):
        route_ref[0, 2 * TOP_K + k:2 * TOP_K + k + 1, :] = ex[k] / den
        chosen = chosen + jnp.where(sels[k], 1.0, 0.0)
    before = (lax.broadcasted_iota(jnp.int32, (m, m), 0) < lax.broadcasted_iota(jnp.int32, (m, m), 1))
    rank = jnp.dot(chosen.astype(BF16), jnp.where(before, 1.0, 0.0).astype(BF16), preferred_element_type=F32)
    for k in range(TOP_K):
        route_ref[0, TOP_K + k:TOP_K + k + 1, :] = jnp.sum(jnp.where(sels[k], rank, 0.0), axis=0, keepdims=True)
    route_ref[0, 3 * TOP_K:4 * TOP_K, :] = jnp.zeros((TOP_K, m), F32)
    cnt_ref[0] = jnp.broadcast_to(jnp.sum(chosen, axis=1, keepdims=True), (ne, LANES))


def _outproj(attn, lru_o, x, gm, shf, scf, lnf, w_out_bf, wr_t, br, nb, tr, n_tiles, tile0, prev):
    nbt, t, d = x.shape
    aw = attn.shape[-1]
    m = nb * tr
    assert m == TOKEN_TILE
    ne = wr_t.shape[0]
    tiles_per_seq = t // tr
    xmap = lambda b, i: (b, i, 0)
    mmap = lambda b, i: (b, 0, 0)
    c2 = lambda b, i: (0, 0)
    tile = lambda b, i: (tile0 + b * tiles_per_seq + i, 0)
    tile3 = lambda b, i: (tile0 + b * tiles_per_seq + i, 0, 0)
    mod = pl.BlockSpec((nb, 1, d), mmap)
    in_specs = [pl.BlockSpec((nb, tr, aw), xmap), pl.BlockSpec((nb, tr, aw), xmap),
                pl.BlockSpec((nb, tr, d), xmap), mod, mod, mod,
                pl.BlockSpec((1, d), c2), pl.BlockSpec(w_out_bf.shape, c2),
                pl.BlockSpec(wr_t.shape, c2), pl.BlockSpec((ne, 1), c2)]
    args = [attn, lru_o, x, gm, shf, scf, lnf, w_out_bf, wr_t, br]
    aliases = {}
    if prev is not None:
        in_specs += [pl.BlockSpec(memory_space=pl.ANY)] * 3
        args += list(prev)
        aliases = {10: 1, 11: 2, 12: 3}
    return pl.pallas_call(
        functools.partial(_outproj_kernel, aliased=len(aliases)),
        grid=(nbt // nb, tiles_per_seq),
        in_specs=in_specs,
        out_specs=[pl.BlockSpec((nb, tr, d), xmap), pl.BlockSpec((m, d), tile),
                   pl.BlockSpec((1, 4 * TOP_K, m), tile3), pl.BlockSpec((1, ne, LANES), tile3)],
        out_shape=[jax.ShapeDtypeStruct((nbt, t, d), F32),
                   jax.ShapeDtypeStruct((n_tiles * m, d), BF16),
                   jax.ShapeDtypeStruct((n_tiles, 4 * TOP_K, m), F32),
                   jax.ShapeDtypeStruct((n_tiles, ne, LANES), F32)],
        input_output_aliases=aliases,
        compiler_params=_params(("arbitrary", "arbitrary")),
        name="outproj",
    )(*args)


def _tile_rows(m):
    cap = TOP_K * m + N_EXPERTS * (BF16_ROWS - 1) + BF16_ROWS
    return -(-cap // TOKEN_TILE) * TOKEN_TILE


def _table_sizes(nt):
    g = BF16_ROWS
    m = TOKEN_TILE
    n_chunks = _tile_rows(m) // g
    n_gap = -(-(N_EXPERTS * (EXPERT_ROWS // g - 1)) // nt)
    bound = TOP_K * m * nt + nt * N_EXPERTS * (g - 1) + N_EXPERTS * (EXPERT_ROWS - g)
    n_sorted = -(-bound // EXPERT_ROWS) * EXPERT_ROWS
    return n_chunks, n_gap, n_sorted, n_sorted + 2 * (n_chunks + n_gap) * g


def _route_tables(cnt):
    nt = cnt.shape[0]
    g = BF16_ROWS
    bm = EXPERT_ROWS
    n_chunks, n_gap, n_sorted, _ = _table_sizes(nt)
    e_ids = jnp.arange(N_EXPERTS, dtype=jnp.int32)
    t_ids = jnp.arange(nt, dtype=jnp.int32)
    upto = (e_ids[:, None] <= e_ids[None, :]).astype(jnp.int32)
    pc = (cnt + g - 1) // g * g
    ctile = jnp.sum(pc[:, :, None] * upto[None], axis=1)
    toff = ctile - pc
    trow = ctile[:, -1]
    tot = jnp.sum(pc, axis=0)
    reg = (tot + bm - 1) // bm * bm
    creg = jnp.sum(reg[:, None] * upto, axis=0)
    base = creg - reg
    earlier = (t_ids[:, None] < t_ids[None, :]).astype(jnp.int32)
    goff = base[None, :] + jnp.sum(pc[:, None, :] * earlier[:, :, None], axis=0)
    r = jnp.arange(n_chunks, dtype=jnp.int32) * g
    r3 = r[None, :, None]
    in_seg = (toff[:, None, :] <= r3) & (r3 < ctile[:, None, :])
    dst = jnp.sum(jnp.where(in_seg, (goff - toff)[:, None, :], 0), axis=2) + r[None, :]
    dst = jnp.where(r[None, :] < trow[:, None], dst, -1)
    gcnt = (reg - tot) // g
    gcum = jnp.sum(gcnt[:, None] * upto, axis=0)
    gstart = gcum - gcnt
    s = jnp.arange(nt * n_gap, dtype=jnp.int32)
    in_gap = (gstart[None, :] <= s[:, None]) & (s[:, None] < gcum[None, :])
    gdst = jnp.sum(jnp.where(in_gap, (base + tot - g * gstart)[None, :] + g * s[:, None], 0), axis=1)
    gdst = jnp.where(s < gcum[-1], gdst, -1).reshape(nt, n_gap)
    table = jnp.concatenate([dst, gdst], axis=1).astype(jnp.int32)
    n_entries = n_chunks + n_gap
    spare = n_sorted + ((t_ids % 2)[:, None] * n_entries + jnp.arange(n_entries, dtype=jnp.int32)[None, :]) * g
    dispatch_tab = jnp.where(table >= 0, table, spare).astype(jnp.int32)
    combine_tab = jnp.maximum(dst, 0).astype(jnp.int32)
    toff_b = jnp.broadcast_to(toff.astype(F32)[:, :, None], (nt, N_EXPERTS, LANES))
    return dispatch_tab, combine_tab, toff_b, base.astype(jnp.int32), (reg // bm).astype(jnp.int32)


def _slot_rows(route_ref, toff_ref, m):
    ne = toff_ref.shape[1]
    e_iota = lax.broadcasted_iota(jnp.int32, (ne, m), 0).astype(F32)
    toff_col = toff_ref[0][:, 0:1]
    pos = []
    for k in range(TOP_K):
        sel = e_iota == route_ref[0, k:k + 1, :]
        start = jnp.sum(jnp.where(sel, toff_col, 0.0), axis=0, keepdims=True)
        pos.append(start + route_ref[0, TOP_K + k:TOP_K + k + 1, :])
    return pos


def _dispatch_kernel(tab_ref, h2_ref, route_ref, toff_ref, xs_hbm, buf_ref, sem, *, n_chunks, n_tiles):
    t = pl.program_id(0)
    slot = t % 2
    m = h2_ref.shape[0]
    rows = buf_ref.shape[1]
    n_entries = tab_ref.shape[1]
    g = BF16_ROWS
    per_chunk = m // g

    def start(c):
        src = c * g if c < n_chunks else rows - g
        pltpu.make_async_copy(
            buf_ref.at[slot, pl.ds(src, g)],
            xs_hbm.at[pl.ds(pl.multiple_of(tab_ref[t, c], g), g)], sem.at[slot]).start()

    def wait_all(which):
        for _ in range(n_entries):
            pltpu.make_async_copy(buf_ref.at[which, pl.ds(0, g)], xs_hbm.at[pl.ds(0, g)], sem.at[which]).wait()

    @pl.when(t >= 2)
    def _():
        wait_all(slot)

    pos = _slot_rows(route_ref, toff_ref, m)
    h2 = h2_ref[...]
    for rc in range(rows // m):
        r_iota = (lax.broadcasted_iota(jnp.int32, (m, m), 0) + rc * m).astype(F32)
        onehot = jnp.zeros((m, m), F32)
        for k in range(TOP_K):
            onehot = jnp.where(r_iota == pos[k], 1.0, onehot)
        buf_ref[slot, rc * m:(rc + 1) * m, :] = jnp.dot(
            onehot.astype(BF16), h2, preferred_element_type=F32).astype(BF16)
        for c in range((rc - 1) * per_chunk, rc * per_chunk) if rc > 0 else ():
            start(c)
    for c in range(n_chunks - per_chunk, n_entries):
        start(c)

    @pl.when(t == n_tiles - 1)
    def _():
        if n_tiles > 1:
            wait_all(1 - slot)
        wait_all(slot)


def _dispatch(table, h2, route, toff_b, n_rows, n_chunks):
    nt = route.shape[0]
    m = TOKEN_TILE
    d = h2.shape[1]
    rows = _tile_rows(m)
    grid_spec = pltpu.PrefetchScalarGridSpec(
        num_scalar_prefetch=1,
        grid=(nt,),
        in_specs=[pl.BlockSpec((m, d), lambda t, tab: (t, 0)),
                  pl.BlockSpec((1, 4 * TOP_K, m), lambda t, tab: (t, 0, 0)),
                  pl.BlockSpec((1, N_EXPERTS, LANES), lambda t, tab: (t, 0, 0))],
        out_specs=pl.BlockSpec(memory_space=pl.ANY),
        scratch_shapes=[pltpu.VMEM((2, rows, d), BF16), pltpu.SemaphoreType.DMA((2,))],
    )
    return pl.pallas_call(
        functools.partial(_dispatch_kernel, n_chunks=n_chunks, n_tiles=nt),
        grid_spec=grid_spec,
        out_shape=jax.ShapeDtypeStruct((n_rows, d), BF16),
        compiler_params=_params(("arbitrary",)),
        name="dispatch",
    )(table, h2, route, toff_b)


def _expert_kernel(row0_ref, nblk_ref, xs_hbm, wgu_ref, bgu_ref, wdn_ref, bdn_ref, ys_hbm,
                   wgu_bf, wdn_bf, xbuf, ybuf, sem_in, sem_out):
    e = pl.program_id(0)
    ne = pl.num_programs(0)
    bm = xbuf.shape[1]
    dff = wdn_ref.shape[1]
    nblk = nblk_ref[e]

    def in_copy(expert, j, slot):
        start = pl.multiple_of(row0_ref[expert] + j * bm, bm)
        return pltpu.make_async_copy(xs_hbm.at[pl.ds(start, bm)], xbuf.at[slot], sem_in.at[slot])

    def out_copy(j, slot):
        start = pl.multiple_of(row0_ref[e] + j * bm, bm)
        return pltpu.make_async_copy(ybuf.at[slot], ys_hbm.at[pl.ds(start, bm)], sem_out.at[slot])

    @pl.when((e == 0) & (nblk > 0))
    def _():
        in_copy(e, 0, 0).start(priority=1)

    wgu_bf[...] = wgu_ref[0].astype(BF16)
    wdn_bf[...] = wdn_ref[0].astype(BF16)

    def block(j, carry):
        slot = j % 2
        in_copy(e, j, slot).wait()

        @pl.when(j + 1 < nblk)
        def _():
            in_copy(e, j + 1, 1 - slot).start(priority=1)

        @pl.when(j >= 2)
        def _():
            out_copy(j - 2, slot).wait()

        gu = jnp.dot(xbuf[slot], wgu_bf[...], preferred_element_type=F32) + bgu_ref[0]
        gate = jnp.minimum(gu[:, :dff], SWIGLU_LIMIT)
        up = jnp.clip(gu[:, dff:], -SWIGLU_LIMIT, SWIGLU_LIMIT)
        glu = gate * jax.nn.sigmoid(gate * SWIGLU_ALPHA)
        act = ((up + 1.0) * glu).astype(BF16)
        ybuf[slot] = (jnp.dot(act, wdn_bf[...], preferred_element_type=F32) + bdn_ref[0]).astype(BF16)
        out_copy(j, slot).start(priority=1)
        return carry

    lax.fori_loop(0, nblk, block, 0)

    nxt = jnp.minimum(e + 1, ne - 1)

    @pl.when((e + 1 < ne) & (nblk_ref[nxt] > 0))
    def _():
        in_copy(nxt, 0, 0).start(priority=1)

    @pl.when(nblk >= 2)
    def _():
        out_copy(nblk - 2, nblk % 2).wait()

    @pl.when(nblk >= 1)
    def _():
        out_copy(nblk - 1, (nblk - 1) % 2).wait()


def _experts(row0, nblk, xs, w_gu, b_gu, w_dn, b_dn):
    n_rows, d = xs.shape
    bm = EXPERT_ROWS
    ne, _, dff2 = w_gu.shape
    dff = w_dn.shape[1]
    exp3 = lambda e, r0, nb: (e, 0, 0)
    grid_spec = pltpu.PrefetchScalarGridSpec(
        num_scalar_prefetch=2,
        grid=(ne,),
        in_specs=[pl.BlockSpec(memory_space=pl.ANY),
                  pl.BlockSpec((1, d, dff2), exp3), pl.BlockSpec((1, 1, dff2), exp3),
                  pl.BlockSpec((1, dff, d), exp3), pl.BlockSpec((1, 1, d), exp3)],
        out_specs=pl.BlockSpec(memory_space=pl.ANY),
        scratch_shapes=[pltpu.VMEM((d, dff2), BF16), pltpu.VMEM((dff, d), BF16),
                        pltpu.VMEM((2, bm, d), BF16), pltpu.VMEM((2, bm, d), BF16),
                        pltpu.SemaphoreType.DMA((2,)), pltpu.SemaphoreType.DMA((2,))],
    )
    return pl.pallas_call(
        _expert_kernel,
        grid_spec=grid_spec,
        out_shape=jax.ShapeDtypeStruct((n_rows, d), BF16),
        compiler_params=_params(("arbitrary",)),
        name="experts",
    )(row0, nblk, xs, w_gu, b_gu.reshape(ne, 1, dff2), w_dn, b_dn.reshape(ne, 1, d))


def _combine_kernel(tab_ref, ys_hbm, route_ref, toff_ref, x1_ref, gf_ref, o_ref, buf_ref, sem, *, n_chunks, tile0):
    step = pl.program_id(0)
    n_steps = pl.num_programs(0)
    t = tile0 + step
    slot = step % 2
    nb, tr, d = x1_ref.shape
    m = nb * tr
    rows = buf_ref.shape[1]
    g = BF16_ROWS

    def fetch(tile, which):
        for c in range(n_chunks):
            pltpu.make_async_copy(
                ys_hbm.at[pl.ds(pl.multiple_of(tab_ref[tile, c], g), g)],
                buf_ref.at[which, pl.ds(c * g, g)], sem.at[which]).start()

    @pl.when(step == 0)
    def _():
        fetch(t, slot)

    @pl.when(step + 1 < n_steps)
    def _():
        fetch(t + 1, 1 - slot)

    pos = _slot_rows(route_ref, toff_ref, m)
    gates = [route_ref[0, 2 * TOP_K + k:2 * TOP_K + k + 1, :] for k in range(TOP_K)]
    stacked = jnp.concatenate(pos + gates + [jnp.zeros((LANES - 2 * TOP_K, m), F32)], axis=0)
    cols = stacked.T
    for _ in range(n_chunks):
        pltpu.make_async_copy(ys_hbm.at[pl.ds(0, g)], buf_ref.at[slot, pl.ds(0, g)], sem.at[slot]).wait()

    acc = jnp.zeros((m, d), F32)
    for rc in range(rows // m):
        c_iota = (lax.broadcasted_iota(jnp.int32, (m, m), 1) + rc * m).astype(F32)
        weights = jnp.zeros((m, m), F32)
        for k in range(TOP_K):
            weights = jnp.where(c_iota == cols[:, k:k + 1], cols[:, TOP_K + k:TOP_K + k + 1], weights)
        acc = acc + jnp.dot(weights.astype(BF16), buf_ref[slot, rc * m:(rc + 1) * m, :],
                            preferred_element_type=F32)
    o_ref[...] = x1_ref[...] + gf_ref[...] * acc.reshape(nb, tr, d)


def _combine(table, ys, route, toff_b, x1, gf, nb, tr, tile0, n_chunks):
    nbt, t, d = x1.shape
    m = nb * tr
    assert m == TOKEN_TILE
    rows = _tile_rows(m)
    tiles_per_seq = t // tr
    n_steps = (nbt // nb) * tiles_per_seq
    xmap = lambda s, tab: (s // tiles_per_seq, s % tiles_per_seq, 0)
    grid_spec = pltpu.PrefetchScalarGridSpec(
        num_scalar_prefetch=1,
        grid=(n_steps,),
        in_specs=[pl.BlockSpec(memory_space=pl.ANY),
                  pl.BlockSpec((1, 4 * TOP_K, m), lambda s, tab: (tile0 + s, 0, 0)),
                  pl.BlockSpec((1, N_EXPERTS, LANES), lambda s, tab: (tile0 + s, 0, 0)),
                  pl.BlockSpec((nb, tr, d), xmap),
                  pl.BlockSpec((nb, 1, d), lambda s, tab: (s // tiles_per_seq, 0, 0))],
        out_specs=pl.BlockSpec((nb, tr, d), xmap),
        scratch_shapes=[pltpu.VMEM((2, rows, d), BF16), pltpu.SemaphoreType.DMA((2,))],
    )
    return pl.pallas_call(
        functools.partial(_combine_kernel, n_chunks=n_chunks, tile0=tile0),
        grid_spec=grid_spec,
        out_shape=jax.ShapeDtypeStruct((nbt, t, d), F32),
        compiler_params=_params(("arbitrary",)),
        name="combine",
    )(table, ys, route, toff_b, x1, gf)


def _block_diag(w, groups):
    n, k, _ = w.shape
    w = w.reshape(n // groups, groups, k, k)
    eye = jnp.eye(groups, dtype=w.dtype)
    return jnp.einsum("ngij,gh->ngihj", w, eye).reshape(n // groups, groups * k, groups * k)


def _layer(xp, xs, mod, k_cache, v_cache, conv_state, lru_state, lw):
    (ln_mix, ln_ffn, w_in, q_norm, k_norm, rel_bias, conv_w, conv_b, w_rg, b_rg, w_ig, b_ig, lam,
     w_out, w_router, b_router, w_gu, b_gu, w_dn, b_dn) = lw
    bp, s, d = xp.shape
    bs, ts, _ = xs.shape
    aw = w_out.shape[0] // 2
    nh = aw // HEAD_DIM
    m = TOKEN_TILE
    assert s % m == 0 and bs * ts == m and s % ATTN_Q_TILE == 0 and s % LRU_TILE == 0

    terms = [mod[:, i * d:(i + 1) * d][:, None, :] for i in range(6)]
    tp = [t[:bp] for t in terms]
    tsm = [t[bp:] for t in terms]

    w_in_bf = w_in.astype(BF16)
    w_out_bf = w_out.astype(BF16)
    qn_t = jnp.tile(q_norm, nh).reshape(1, aw)
    kn_t = jnp.tile(k_norm, nh).reshape(1, aw)
    head_mean = _block_diag(jnp.full((nh, HEAD_DIM, HEAD_DIM), 1.0 / HEAD_DIM, F32), nh)[0].astype(BF16)
    groups = MXU_DIM // w_rg.shape[-1]
    wa_bd = _block_diag(w_rg, groups).astype(BF16)
    wx_bd = _block_diag(w_ig, groups).astype(BF16)
    lw_c = b_rg.size
    b_a = b_rg.reshape(1, lw_c)
    b_x = b_ig.reshape(1, lw_c)
    lam2 = lam.reshape(1, lw_c)
    cb2 = conv_b.reshape(1, lw_c)
    ln_mix2 = ln_mix.reshape(1, d)
    ln_ffn2 = ln_ffn.reshape(1, d)
    wr_t = w_router.T
    br = b_router.reshape(-1, 1)
    tab_p = _bias_table(rel_bias, 3 * ATTN_Q_TILE - 1)
    r_cache = k_cache.shape[1]
    tab_s = _bias_table(rel_bias, r_cache + ts - 1)

    qp, kp, vp, xrp, ygp, k32p, v32p = _mixin(xp, tp[0], tp[1], ln_mix2, w_in_bf, qn_t, kn_t, head_mean, 1, m)
    qs, ks, vs, xrs, ygs, k32s, v32s = _mixin(xs, tsm[0], tsm[1], ln_mix2, w_in_bf, qn_t, kn_t, head_mean, bs, ts)
    attn_p = _attn_prompt(qp, kp, vp, tab_p)
    attn_s = _attn_step(qs, ks, vs, k_cache.reshape(bs, r_cache * nh, HEAD_DIM),
                        v_cache.reshape(bs, r_cache * nh, HEAD_DIM), tab_s, ATTN_STEP_BATCH)

    zeros_pre = jnp.zeros((bp, SUBLANES, lw_c), F32)
    zeros_h = jnp.zeros((bp, 1, lw_c), F32)
    pre_s = jnp.pad(conv_state, ((0, 0), (SUBLANES - (CONV_WIDTH - 1), 0), (0, 0)))
    lru_p, tail_p, hl_p = _lru(xrp, ygp, zeros_pre, zeros_h, conv_w, cb2, wa_bd, wx_bd, b_a, b_x, lam2, 1, LRU_TILE)
    lru_s, tail_s, hl_s = _lru(xrs, ygs, pre_s, lru_state[:, None, :], conv_w, cb2, wa_bd, wx_bd, b_a, b_x, lam2,
                               bs, ts)

    n_tiles = bp * (s // m) + 1
    x1p, h2, route, cnt = _outproj(attn_p, lru_p, xp, tp[2], tp[3], tp[4], ln_ffn2, w_out_bf, wr_t, br,
                                   1, m, n_tiles, 0, None)
    x1s, h2, route, cnt = _outproj(attn_s, lru_s, xs, tsm[2], tsm[3], tsm[4], ln_ffn2, w_out_bf, wr_t, br,
                                   bs, ts, n_tiles, n_tiles - 1, (h2, route, cnt))

    n_chunks, _, _, n_rows = _table_sizes(n_tiles)
    dispatch_tab, combine_tab, toff_b, row0, nblk = _route_tables(cnt[:, :, 0].astype(jnp.int32))
    xs_sorted = _dispatch(dispatch_tab, h2, route, toff_b, n_rows, n_chunks)
    ys_sorted = _experts(row0, nblk, xs_sorted, w_gu, b_gu, w_dn, b_dn)
    yp = _combine(combine_tab, ys_sorted, route, toff_b, x1p, tp[5], 1, m, 0, n_chunks)
    ysm = _combine(combine_tab, ys_sorted, route, toff_b, x1s, tsm[5], bs, ts, n_tiles - 1, n_chunks)

    keep = k32p.shape[1]
    new = (k32p.reshape(bp, keep, nh, HEAD_DIM), v32p.reshape(bp, keep, nh, HEAD_DIM),
           tail_p[:, SUBLANES - (CONV_WIDTH - 1):, :], hl_p[:, 0, :],
           k32s.reshape(bs, ts, nh, HEAD_DIM), v32s.reshape(bs, ts, nh, HEAD_DIM),
           tail_s[:, SUBLANES - (CONV_WIDTH - 1):, :], hl_s[:, 0, :])
    return yp, ysm, new


def kernel(x_prompt, x_sample, c_prompt, c_sample, cache_k, cache_v, state_conv, state_lru, ln_mix_w, ln_ffn_w, w_ada, b_ada, w_in, q_norm_w, k_norm_w, rel_bias, conv_w, conv_b, w_rgate, b_rgate, w_igate, b_igate, lru_lambda, w_out, w_router, b_router, w_gate_up, b_gate_up, w_down, b_down):
    depth = w_in.shape[0]
    yp, ys = x_prompt, x_sample
    c_all = jnp.concatenate([c_prompt, c_sample], axis=0)
    collected = [[] for _ in range(8)]
    for l in range(depth):
        mod = _ada(c_all, w_ada[l], b_ada[l])
        lw = (ln_mix_w[l], ln_ffn_w[l], w_in[l], q_norm_w[l], k_norm_w[l], rel_bias[l], conv_w[l], conv_b[l],
              w_rgate[l], b_rgate[l], w_igate[l], b_igate[l], lru_lambda[l], w_out[l], w_router[l], b_router[l],
              w_gate_up[l], b_gate_up[l], w_down[l], b_down[l])
        yp, ys, new = _layer(yp, ys, mod, cache_k[l], cache_v[l], state_conv[l], state_lru[l], lw)
        for acc, val in zip(collected, new):
            acc.append(val)
    return (yp, ys) + tuple(jnp.stack(vals) for vals in collected)
```

```python
import functools

import jax
import jax.numpy as jnp
from jax import lax
from jax.experimental import pallas as pl
from jax.experimental.pallas import tpu as pltpu

F32 = jnp.float32
BF16 = jnp.bfloat16

CHUNK = 64
N_LEFT_CHUNKS = 8
ATTN_WINDOW = N_LEFT_CHUNKS * CHUNK
HEAD_DIM = 64
REL_CLIP = 128
CONV_WIDTH = 4
LRU_C = 8.0
N_EXPERTS = 32
TOP_K = 4
SWIGLU_LIMIT = 7.0
SWIGLU_ALPHA = 1.702
NORM_EPS = 1e-6
NEG_INF = -1e30

LANES = 128
SUBLANES = 8
BF16_ROWS = 16
MXU_DIM = 256

TOKEN_TILE = 512
ATTN_Q_TILE = 256
LRU_TILE = 256
ATTN_STEP_BATCH = 4
EXPERT_ROWS = 256
BIAS_TABLE = 1024
VMEM_LIMIT = 56 * 1024 * 1024


def _params(sem, vmem=VMEM_LIMIT):
    return pltpu.CompilerParams(dimension_semantics=sem, vmem_limit_bytes=vmem)


def _ada_kernel(c_ref, w_ref, b_ref, o_ref):
    c = c_ref[...]
    s = (c * jax.nn.sigmoid(c)).astype(BF16)
    o_ref[...] = jnp.dot(s, w_ref[...].astype(BF16), preferred_element_type=F32) + b_ref[...]


def _ada(c_all, w_ada, b_ada):
    n, d = c_all.shape
    nout = w_ada.shape[1]
    tn = 1024
    return pl.pallas_call(
        _ada_kernel,
        grid=(nout // tn,),
        in_specs=[pl.BlockSpec((n, d), lambda j: (0, 0)),
                  pl.BlockSpec((d, tn), lambda j: (0, j)),
                  pl.BlockSpec((1, tn), lambda j: (0, j))],
        out_specs=pl.BlockSpec((n, tn), lambda j: (0, j)),
        out_shape=jax.ShapeDtypeStruct((n, nout), F32),
        compiler_params=_params(("arbitrary",)),
        name="ada",
    )(c_all, w_ada, b_ada.reshape(1, nout))


def _mixin_kernel(x_ref, sh_ref, sc_ref, ln_ref, win_ref, qn_ref, kn_ref, bd_ref,
                  q_ref, k_ref, v_ref, xr_ref, yg_ref, k32_ref, v32_ref):
    nb, tr, d = x_ref.shape
    m = nb * tr
    aw = q_ref.shape[-1]
    x = x_ref[...]
    ms = jnp.mean(x * x, axis=-1, keepdims=True)
    h = x * lax.rsqrt(ms + NORM_EPS) * ln_ref[...]
    h = h * (1.0 + sc_ref[...]) + sh_ref[...]
    hb = h.reshape(m, d).astype(BF16)

    def proj(part):
        return jnp.dot(hb, win_ref[:, part * aw:(part + 1) * aw], preferred_element_type=F32)

    def head_norm(t, w_ref):
        msq = jnp.dot((t * t).astype(BF16), bd_ref[...], preferred_element_type=F32)
        return t * lax.rsqrt(msq + NORM_EPS) * w_ref[...]

    q = head_norm(proj(0), qn_ref)
    k = head_norm(proj(1), kn_ref)
    v = proj(2)
    q_ref[...] = (q * (HEAD_DIM ** -0.5)).astype(BF16).reshape(nb, tr, aw)
    k_ref[...] = k.astype(BF16).reshape(nb, tr, aw)
    v_ref[...] = v.astype(BF16).reshape(nb, tr, aw)
    k32_ref[...] = k.reshape(nb, tr, aw)
    v32_ref[...] = v.reshape(nb, tr, aw)
    xr_ref[...] = proj(3).reshape(nb, tr, aw)
    yg_ref[...] = proj(4).reshape(nb, tr, aw)


def _mixin(x, sh, sc, ln_w, w_in_bf, qn_t, kn_t, bd, nb, tr):
    nbt, t, d = x.shape
    aw = qn_t.shape[-1]
    keep = min(ATTN_WINDOW, t)
    assert tr == keep or t == tr
    grid = (nbt // nb, t // tr)
    xmap = lambda b, i: (b, i, 0)
    mmap = lambda b, i: (b, 0, 0)
    cmap = lambda b, i: (0, 0)
    tmap = lambda b, i: (b, 0, 0)
    big = pl.BlockSpec((nb, tr, aw), xmap)
    tail = pl.BlockSpec((nb, keep, aw), tmap)
    return pl.pallas_call(
        _mixin_kernel,
        grid=grid,
        in_specs=[pl.BlockSpec((nb, tr, d), xmap),
                  pl.BlockSpec((nb, 1, d), mmap), pl.BlockSpec((nb, 1, d), mmap),
                  pl.BlockSpec((1, d), cmap),
                  pl.BlockSpec(w_in_bf.shape, cmap),
                  pl.BlockSpec((1, aw), cmap), pl.BlockSpec((1, aw), cmap),
                  pl.BlockSpec(bd.shape, cmap)],
        out_specs=[big, big, big, big, big, tail, tail],
        out_shape=[jax.ShapeDtypeStruct((nbt, t, aw), BF16)] * 3
        + [jax.ShapeDtypeStruct((nbt, t, aw), F32)] * 2
        + [jax.ShapeDtypeStruct((nbt, keep, aw), F32)] * 2,
        compiler_params=_params(("arbitrary", "arbitrary")),
        name="mixin",
    )(x, sh, sc, ln_w, w_in_bf, qn_t, kn_t, bd)


def _bias_table(rel_bias, off):
    h = rel_bias.shape[0]
    left = off - REL_CLIP
    right = BIAS_TABLE - left - (2 * REL_CLIP + 1)
    assert left >= 0 and right >= 0
    return jnp.concatenate([jnp.broadcast_to(rel_bias[:, :1], (h, left)), rel_bias,
                            jnp.broadcast_to(rel_bias[:, -1:], (h, right))], axis=1)


def _toeplitz(tab_row, rows, cols):
    t = jnp.broadcast_to(tab_row, (rows, BIAS_TABLE))
    t = pltpu.roll(t, BIAS_TABLE - (rows - 1), 1, stride=1, stride_axis=0)
    return t[:, :cols]


def _attn_kernel(q_ref, k0_ref, k1_ref, k2_ref, v0_ref, v1_ref, v2_ref, tab_ref, o_ref, bias_ref):
    b = pl.program_id(0)
    s = pl.program_id(1)
    qt = q_ref.shape[1]
    nk = 3 * qt
    nh = bias_ref.shape[0]

    @pl.when((b == 0) & (s == 0))
    def _():
        qi = lax.broadcasted_iota(jnp.int32, (qt, nk), 0) // CHUNK
        kc = lax.broadcasted_iota(jnp.int32, (qt, nk), 1) // CHUNK
        for h in range(nh):
            band = jnp.where(kc <= qi + N_LEFT_CHUNKS, _toeplitz(tab_ref[h:h + 1, :], qt, nk), NEG_INF)
            bias_ref[h] = jnp.where(kc >= qi, band, NEG_INF)

    q = q_ref[0]
    kcat = jnp.concatenate([k0_ref[0], k1_ref[0], k2_ref[0]], axis=0)
    vcat = jnp.concatenate([v0_ref[0], v1_ref[0], v2_ref[0]], axis=0)
    in_seq = lax.broadcasted_iota(jnp.int32, (qt, nk), 1) >= (2 - s) * qt
    outs = []
    for h in range(nh):
        sl = slice(h * HEAD_DIM, (h + 1) * HEAD_DIM)
        sc = lax.dot_general(q[:, sl], kcat[:, sl], (((1,), (1,)), ((), ())), preferred_element_type=F32)
        sc = jnp.where(in_seq, sc + bias_ref[h], NEG_INF)
        mx = jnp.max(sc, axis=-1, keepdims=True)
        p = jnp.exp(sc - mx)
        l = jnp.sum(p, axis=-1, keepdims=True)
        o = jnp.dot(p.astype(BF16), vcat[:, sl], preferred_element_type=F32)
        outs.append(o / l)
    o_ref[0] = jnp.concatenate(outs, axis=-1).astype(BF16)


def _attn_prompt(q, k, v, tab):
    b, s, aw = q.shape
    qt = ATTN_Q_TILE
    nh = aw // HEAD_DIM
    qspec = pl.BlockSpec((1, qt, aw), lambda i, j: (i, j, 0))

    def kspec(back):
        return pl.BlockSpec((1, qt, aw), lambda i, j: (i, jnp.maximum(j - back, 0), 0))

    return pl.pallas_call(
        _attn_kernel,
        grid=(b, s // qt),
        in_specs=[qspec, kspec(2), kspec(1), kspec(0), kspec(2), kspec(1), kspec(0),
                  pl.BlockSpec(tab.shape, lambda i, j: (0, 0))],
        out_specs=qspec,
        out_shape=jax.ShapeDtypeStruct((b, s, aw), BF16),
        scratch_shapes=[pltpu.VMEM((nh, qt, 3 * qt), F32)],
        compiler_params=_params(("arbitrary", "arbitrary")),
        name="attn_prompt",
    )(q, k, k, k, v, v, v, tab)


def _attn_step_kernel(q_ref, kn_ref, vn_ref, ck_ref, cv_ref, tab_ref, o_ref, bias_ref):
    step = pl.program_id(0)
    nbs, t, _ = q_ref.shape
    nh = bias_ref.shape[0]
    r = ck_ref.shape[-1]
    nk = r + LANES
    nt_dims = (((1,), (1,)), ((), ()))

    @pl.when(step == 0)
    def _():
        ok = lax.broadcasted_iota(jnp.int32, (t, nk), 1) < r + t
        for h in range(nh):
            bias_ref[h] = jnp.where(ok, _toeplitz(tab_ref[h:h + 1, :], t, nk), NEG_INF)

    pad = jnp.zeros((LANES - t, HEAD_DIM), BF16)
    for b in range(nbs):
        q = q_ref[b]
        kn = kn_ref[b]
        vn = vn_ref[b]
        outs = []
        for h in range(nh):
            sl = slice(h * HEAD_DIM, (h + 1) * HEAD_DIM)
            qh = q[:, sl]
            k_new = jnp.concatenate([kn[:, sl], pad], axis=0)
            v_new = jnp.concatenate([vn[:, sl], pad], axis=0)
            s_old = jnp.dot(qh, ck_ref[b, h].astype(BF16), preferred_element_type=F32) + bias_ref[h, :, :r]
            s_new = lax.dot_general(qh, k_new, nt_dims, preferred_element_type=F32) + bias_ref[h, :, r:]
            mx = jnp.maximum(jnp.max(s_old, axis=-1, keepdims=True), jnp.max(s_new, axis=-1, keepdims=True))
            p_old = jnp.exp(s_old - mx)
            p_new = jnp.exp(s_new - mx)
            l = jnp.sum(p_old, axis=-1, keepdims=True) + jnp.sum(p_new, axis=-1, keepdims=True)
            o = (lax.dot_general(p_old.astype(BF16), cv_ref[b, h].astype(BF16), nt_dims,
                                 preferred_element_type=F32)
                 + jnp.dot(p_new.astype(BF16), v_new, preferred_element_type=F32))
            outs.append(o / l)
        o_ref[b] = jnp.concatenate(outs, axis=-1).astype(BF16)


def _attn_step(q, kn, vn, ck, cv, tab, nbs):
    b, t, aw = q.shape
    nh = aw // HEAD_DIM
    r = ck.shape[-1]
    new = pl.BlockSpec((nbs, t, aw), lambda i: (i, 0, 0))
    old = pl.BlockSpec((nbs, nh, HEAD_DIM, r), lambda i: (i, 0, 0, 0))
    return pl.pallas_call(
        _attn_step_kernel,
        grid=(b // nbs,),
        in_specs=[new, new, new, old, old, pl.BlockSpec(tab.shape, lambda i: (0, 0))],
        out_specs=new,
        out_shape=jax.ShapeDtypeStruct((b, t, aw), BF16),
        scratch_shapes=[pltpu.VMEM((nh, t, r + LANES), F32)],
        compiler_params=_params(("arbitrary",)),
        name="attn_step",
    )(q, kn, vn, ck, cv, tab)


def _gelu_tanh(x):
    return x * (0.5 * (1.0 + jnp.tanh(0.7978845608028654 * (x + 0.044715 * (x * x * x)))))


def _lru_kernel(xr_ref, yg_ref, pre_ref, h0_ref, cw_ref, cb_ref, wa_ref, wx_ref, ba_ref, bx_ref, lam_ref,
                o_ref, tail_ref, hl_ref, cx_ref, ch_ref):
    step = pl.program_id(1)
    nb, tr, c = xr_ref.shape
    m = nb * tr
    half = c // 2

    @pl.when(step == 0)
    def _():
        cx_ref[...] = pre_ref[...]
        ch_ref[...] = h0_ref[...]

    x = xr_ref[...]
    xp = jnp.concatenate([cx_ref[...], x], axis=1)
    new_tail = xp[:, tr:tr + SUBLANES, :]
    xp2 = xp.reshape(nb * (tr + SUBLANES), c)
    y = cb_ref[...] + cw_ref[CONV_WIDTH - 1:CONV_WIDTH, :] * x
    for back in range(1, CONV_WIDTH):
        shifted = pltpu.roll(xp2, back, 0).reshape(nb, tr + SUBLANES, c)[:, SUBLANES:, :]
        y = y + cw_ref[CONV_WIDTH - 1 - back:CONV_WIDTH - back, :] * shifted
    y2 = y.reshape(m, c)
    yb = y2.astype(BF16)

    def gate(w_ref, b_ref):
        g = jnp.concatenate(
            [jnp.dot(yb[:, :half], w_ref[0], preferred_element_type=F32),
             jnp.dot(yb[:, half:], w_ref[1], preferred_element_type=F32)], axis=1)
        return jax.nn.sigmoid(g + b_ref[...])

    rg = gate(wa_ref, ba_ref)
    ig = gate(wx_ref, bx_ref)
    lam = lam_ref[...]
    log_sig = jnp.minimum(lam, 0.0) - jnp.log1p(jnp.exp(-jnp.abs(lam)))
    log_a = LRU_C * rg * log_sig
    a_cum = jnp.exp(log_a)
    b_cum = jnp.sqrt(-jnp.tanh(log_a) * (a_cum * a_cum + 1.0)) * (ig * y2)
    row = lax.broadcasted_iota(jnp.int32, (m, c), 0) % SUBLANES
    dist = 1
    while dist < SUBLANES:
        keep = row >= dist
        a_sh = jnp.where(keep, pltpu.roll(a_cum, dist, 0), 1.0)
        b_sh = jnp.where(keep, pltpu.roll(b_cum, dist, 0), 0.0)
        b_cum = a_cum * b_sh + b_cum
        a_cum = a_cum * a_sh
        dist *= 2
    groups = tr // SUBLANES
    a_grp = a_cum.reshape(nb, groups, SUBLANES, c)
    b_grp = b_cum.reshape(nb, groups, SUBLANES, c)
    carry = ch_ref[...]
    pieces = []
    for grp in range(groups):
        h_grp = a_grp[:, grp] * carry + b_grp[:, grp]
        carry = h_grp[:, SUBLANES - 1:SUBLANES, :]
        pieces.append(h_grp)
    h = jnp.concatenate(pieces, axis=1)
    o_ref[...] = (h * _gelu_tanh(yg_ref[...])).astype(BF16)
    h_last = carry
    ch_ref[...] = h_last
    hl_ref[...] = h_last
    cx_ref[...] = new_tail
    tail_ref[...] = new_tail


def _lru(xr, yg, pre, h0, conv_w, conv_b, wa_bd, wx_bd, b_a, b_x, lam, nb, tr):
    nbt, t, c = xr.shape
    xmap = lambda b, i: (b, i, 0)
    smap = lambda b, i: (b, 0, 0)
    c2 = lambda b, i: (0, 0)
    c3 = lambda b, i: (0, 0, 0)
    big = pl.BlockSpec((nb, tr, c), xmap)
    return pl.pallas_call(
        _lru_kernel,
        grid=(nbt // nb, t // tr),
        in_specs=[big, big,
                  pl.BlockSpec((nb, SUBLANES, c), smap), pl.BlockSpec((nb, 1, c), smap),
                  pl.BlockSpec(conv_w.shape, c2), pl.BlockSpec((1, c), c2),
                  pl.BlockSpec(wa_bd.shape, c3), pl.BlockSpec(wx_bd.shape, c3),
                  pl.BlockSpec((1, c), c2), pl.BlockSpec((1, c), c2), pl.BlockSpec((1, c), c2)],
        out_specs=[big, pl.BlockSpec((nb, SUBLANES, c), smap), pl.BlockSpec((nb, 1, c), smap)],
        out_shape=[jax.ShapeDtypeStruct((nbt, t, c), BF16),
                   jax.ShapeDtypeStruct((nbt, SUBLANES, c), F32),
                   jax.ShapeDtypeStruct((nbt, 1, c), F32)],
        scratch_shapes=[pltpu.VMEM((nb, SUBLANES, c), F32), pltpu.VMEM((nb, 1, c), F32)],
        compiler_params=_params(("arbitrary", "arbitrary")),
        name="lru",
    )(xr, yg, pre, h0, conv_w, conv_b, wa_bd, wx_bd, b_a, b_x, lam)


def _outproj_kernel(*refs, aliased):
    (at_ref, lr_ref, x_ref, gm_ref, shf_ref, scf_ref, lnf_ref, wo_ref, wr_ref, br_ref) = refs[:10]
    x1_ref, h2_ref, route_ref, cnt_ref = refs[10 + aliased:]
    nb, tr, d = x_ref.shape
    m = nb * tr
    aw = at_ref.shape[-1]
    ne = wr_ref.shape[0]
    at = at_ref[...].reshape(m, aw)
    lr = lr_ref[...].reshape(m, aw)
    mix = (jnp.dot(at, wo_ref[0:aw, :], preferred_element_type=F32)
           + jnp.dot(lr, wo_ref[aw:2 * aw, :], preferred_element_type=F32))
    x1 = x_ref[...] + gm_ref[...] * mix.reshape(nb, tr, d)
    x1_ref[...] = x1
    ms = jnp.mean(x1 * x1, axis=-1, keepdims=True)
    h2 = x1 * lax.rsqrt(ms + NORM_EPS) * lnf_ref[...]
    h2 = (h2 * (1.0 + scf_ref[...]) + shf_ref[...]).reshape(m, d)
    h2_ref[...] = h2.astype(BF16)

    logits = lax.dot_general(wr_ref[...], h2, (((1,), (1,)), ((), ())),
                             precision=lax.Precision.HIGHEST, preferred_element_type=F32) + br_ref[...]
    e_iota = lax.broadcasted_iota(jnp.int32, (ne, m), 0).astype(F32)
    vals = logits
    top_v, sels = [], []
    for k in range(TOP_K):
        mx = jnp.max(vals, axis=0, keepdims=True)
        idx = jnp.min(jnp.where(vals == mx, e_iota, float(ne)), axis=0, keepdims=True)
        sel = e_iota == idx
        vals = jnp.where(sel, -jnp.inf, vals)
        top_v.append(mx)
        sels.append(sel)
        route_ref[0, k:k + 1, :] = idx
    ex = [jnp.exp(v - top_v[0]) for v in top_v]
    den = ex[0] + ex[1] + ex[2] + ex[3]
    chosen = jnp.zeros((ne, m), F32)
    for k in range(TOP_K):
        route_ref[0, 2 * TOP_K + k:2 * TOP_K + k + 1, :] = ex[k] / den
        chosen = chosen + jnp.where(sels[k], 1.0, 0.0)
    before = (lax.broadcasted_iota(jnp.int32, (m, m), 0) < lax.broadcasted_iota(jnp.int32, (m, m), 1))
    rank = jnp.dot(chosen.astype(BF16), jnp.where(before, 1.0, 0.0).astype(BF16), preferred_element_type=F32)
    for k in range(TOP_K):
        route_ref[0, TOP_K + k:TOP_K + k + 1, :] = jnp.sum(jnp.where(sels[k], rank, 0.0), axis=0, keepdims=True)
    route_ref[0, 3 * TOP_K:4 * TOP_K, :] = jnp.zeros((TOP_K, m), F32)
    cnt_ref[0] = jnp.broadcast_to(jnp.sum(chosen, axis=1, keepdims=True), (ne, LANES))


def _outproj(attn, lru_o, x, gm, shf, scf, lnf, w_out_bf, wr_t, br, nb, tr, n_tiles, tile0, prev):
    nbt, t, d = x.shape
    aw = attn.shape[-1]
    m = nb * tr
    assert m == TOKEN_TILE
    ne = wr_t.shape[0]
    tiles_per_seq = t // tr
    xmap = lambda b, i: (b, i, 0)
    mmap = lambda b, i: (b, 0, 0)
    c2 = lambda b, i: (0, 0)
    tile = lambda b, i: (tile0 + b * tiles_per_seq + i, 0)
    tile3 = lambda b, i: (tile0 + b * tiles_per_seq + i, 0, 0)
    mod = pl.BlockSpec((nb, 1, d), mmap)
    in_specs = [pl.BlockSpec((nb, tr, aw), xmap), pl.BlockSpec((nb, tr, aw), xmap),
                pl.BlockSpec((nb, tr, d), xmap), mod, mod, mod,
                pl.BlockSpec((1, d), c2), pl.BlockSpec(w_out_bf.shape, c2),
                pl.BlockSpec(wr_t.shape, c2), pl.BlockSpec((ne, 1), c2)]
    args = [attn, lru_o, x, gm, shf, scf, lnf, w_out_bf, wr_t, br]
    aliases = {}
    if prev is not None:
        in_specs += [pl.BlockSpec(memory_space=pl.ANY)] * 3
        args += list(prev)
        aliases = {10: 1, 11: 2, 12: 3}
    return pl.pallas_call(
        functools.partial(_outproj_kernel, aliased=len(aliases)),
        grid=(nbt // nb, tiles_per_seq),
        in_specs=in_specs,
        out_specs=[pl.BlockSpec((nb, tr, d), xmap), pl.BlockSpec((m, d), tile),
                   pl.BlockSpec((1, 4 * TOP_K, m), tile3), pl.BlockSpec((1, ne, LANES), tile3)],
        out_shape=[jax.ShapeDtypeStruct((nbt, t, d), F32),
                   jax.ShapeDtypeStruct((n_tiles * m, d), BF16),
                   jax.ShapeDtypeStruct((n_tiles, 4 * TOP_K, m), F32),
                   jax.ShapeDtypeStruct((n_tiles, ne, LANES), F32)],
        input_output_aliases=aliases,
        compiler_params=_params(("arbitrary", "arbitrary")),
        name="outproj",
    )(*args)


def _tile_rows(m):
    cap = TOP_K * m + N_EXPERTS * (BF16_ROWS - 1) + BF16_ROWS
    return -(-cap // TOKEN_TILE) * TOKEN_TILE


def _table_sizes(nt):
    g = BF16_ROWS
    m = TOKEN_TILE
    n_chunks = _tile_rows(m) // g
    n_gap = -(-(N_EXPERTS * (EXPERT_ROWS // g - 1)) // nt)
    bound = TOP_K * m * nt + nt * N_EXPERTS * (g - 1) + N_EXPERTS * (EXPERT_ROWS - g)
    n_sorted = -(-bound // EXPERT_ROWS) * EXPERT_ROWS
    return n_chunks, n_gap, n_sorted, n_sorted + 2 * (n_chunks + n_gap) * g


def _route_tables(cnt):
    nt = cnt.shape[0]
    g = BF16_ROWS
    bm = EXPERT_ROWS
    n_chunks, n_gap, n_sorted, _ = _table_sizes(nt)
    e_ids = jnp.arange(N_EXPERTS, dtype=jnp.int32)
    t_ids = jnp.arange(nt, dtype=jnp.int32)
    upto = (e_ids[:, None] <= e_ids[None, :]).astype(jnp.int32)
    pc = (cnt + g - 1) // g * g
    ctile = jnp.sum(pc[:, :, None] * upto[None], axis=1)
    toff = ctile - pc
    trow = ctile[:, -1]
    tot = jnp.sum(pc, axis=0)
    reg = (tot + bm - 1) // bm * bm
    creg = jnp.sum(reg[:, None] * upto, axis=0)
    base = creg - reg
    earlier = (t_ids[:, None] < t_ids[None, :]).astype(jnp.int32)
    goff = base[None, :] + jnp.sum(pc[:, None, :] * earlier[:, :, None], axis=0)
    r = jnp.arange(n_chunks, dtype=jnp.int32) * g
    r3 = r[None, :, None]
    in_seg = (toff[:, None, :] <= r3) & (r3 < ctile[:, None, :])
    dst = jnp.sum(jnp.where(in_seg, (goff - toff)[:, None, :], 0), axis=2) + r[None, :]
    dst = jnp.where(r[None, :] < trow[:, None], dst, -1)
    gcnt = (reg - tot) // g
    gcum = jnp.sum(gcnt[:, None] * upto, axis=0)
    gstart = gcum - gcnt
    s = jnp.arange(nt * n_gap, dtype=jnp.int32)
    in_gap = (gstart[None, :] <= s[:, None]) & (s[:, None] < gcum[None, :])
    gdst = jnp.sum(jnp.where(in_gap, (base + tot - g * gstart)[None, :] + g * s[:, None], 0), axis=1)
    gdst = jnp.where(s < gcum[-1], gdst, -1).reshape(nt, n_gap)
    table = jnp.concatenate([dst, gdst], axis=1).astype(jnp.int32)
    n_entries = n_chunks + n_gap
    spare = n_sorted + ((t_ids % 2)[:, None] * n_entries + jnp.arange(n_entries, dtype=jnp.int32)[None, :]) * g
    dispatch_tab = jnp.where(table >= 0, table, spare).astype(jnp.int32)
    combine_tab = jnp.maximum(dst, 0).astype(jnp.int32)
    toff_b = jnp.broadcast_to(toff.astype(F32)[:, :, None], (nt, N_EXPERTS, LANES))
    return dispatch_tab, combine_tab, toff_b, base.astype(jnp.int32), (reg // bm).astype(jnp.int32)


def _slot_rows(route_ref, toff_ref, m):
    ne = toff_ref.shape[1]
    e_iota = lax.broadcasted_iota(jnp.int32, (ne, m), 0).astype(F32)
    toff_col = toff_ref[0][:, 0:1]
    pos = []
    for k in range(TOP_K):
        sel = e_iota == route_ref[0, k:k + 1, :]
        start = jnp.sum(jnp.where(sel, toff_col, 0.0), axis=0, keepdims=True)
        pos.append(start + route_ref[0, TOP_K + k:TOP_K + k + 1, :])
    return pos


def _dispatch_kernel(tab_ref, h2_ref, route_ref, toff_ref, xs_hbm, buf_ref, sem, *, n_chunks, n_tiles):
    t = pl.program_id(0)
    slot = t % 2
    m = h2_ref.shape[0]
    rows = buf_ref.shape[1]
    n_entries = tab_ref.shape[1]
    g = BF16_ROWS
    per_chunk = m // g

    def start(c):
        src = c * g if c < n_chunks else rows - g
        pltpu.make_async_copy(
            buf_ref.at[slot, pl.ds(src, g)],
            xs_hbm.at[pl.ds(pl.multiple_of(tab_ref[t, c], g), g)], sem.at[slot]).start()

    def wait_all(which):
        for _ in range(n_entries):
            pltpu.make_async_copy(buf_ref.at[which, pl.ds(0, g)], xs_hbm.at[pl.ds(0, g)], sem.at[which]).wait()

    @pl.when(t >= 2)
    def _():
        wait_all(slot)

    pos = _slot_rows(route_ref, toff_ref, m)
    h2 = h2_ref[...]
    for rc in range(rows // m):
        r_iota = (lax.broadcasted_iota(jnp.int32, (m, m), 0) + rc * m).astype(F32)
        onehot = jnp.zeros((m, m), F32)
        for k in range(TOP_K):
            onehot = jnp.where(r_iota == pos[k], 1.0, onehot)
        buf_ref[slot, rc * m:(rc + 1) * m, :] = jnp.dot(
            onehot.astype(BF16), h2, preferred_element_type=F32).astype(BF16)
        for c in range((rc - 1) * per_chunk, rc * per_chunk) if rc > 0 else ():
            start(c)
    for c in range(n_chunks - per_chunk, n_entries):
        start(c)

    @pl.when(t == n_tiles - 1)
    def _():
        if n_tiles > 1:
            wait_all(1 - slot)
        wait_all(slot)


def _dispatch(table, h2, route, toff_b, n_rows, n_chunks):
    nt = route.shape[0]
    m = TOKEN_TILE
    d = h2.shape[1]
    rows = _tile_rows(m)
    grid_spec = pltpu.PrefetchScalarGridSpec(
        num_scalar_prefetch=1,
        grid=(nt,),
        in_specs=[pl.BlockSpec((m, d), lambda t, tab: (t, 0)),
                  pl.BlockSpec((1, 4 * TOP_K, m), lambda t, tab: (t, 0, 0)),
                  pl.BlockSpec((1, N_EXPERTS, LANES), lambda t, tab: (t, 0, 0))],
        out_specs=pl.BlockSpec(memory_space=pl.ANY),
        scratch_shapes=[pltpu.VMEM((2, rows, d), BF16), pltpu.SemaphoreType.DMA((2,))],
    )
    return pl.pallas_call(
        functools.partial(_dispatch_kernel, n_chunks=n_chunks, n_tiles=nt),
        grid_spec=grid_spec,
        out_shape=jax.ShapeDtypeStruct((n_rows, d), BF16),
        compiler_params=_params(("arbitrary",)),
        name="dispatch",
    )(table, h2, route, toff_b)


def _expert_kernel(row0_ref, nblk_ref, xs_hbm, wgu_ref, bgu_ref, wdn_ref, bdn_ref, ys_hbm,
                   wgu_bf, wdn_bf, xbuf, ybuf, sem_in, sem_out):
    e = pl.program_id(0)
    ne = pl.num_programs(0)
    bm = xbuf.shape[1]
    dff = wdn_ref.shape[1]
    nblk = nblk_ref[e]

    def in_copy(expert, j, slot):
        start = pl.multiple_of(row0_ref[expert] + j * bm, bm)
        return pltpu.make_async_copy(xs_hbm.at[pl.ds(start, bm)], xbuf.at[slot], sem_in.at[slot])

    def out_copy(j, slot):
        start = pl.multiple_of(row0_ref[e] + j * bm, bm)
        return pltpu.make_async_copy(ybuf.at[slot], ys_hbm.at[pl.ds(start, bm)], sem_out.at[slot])

    @pl.when((e == 0) & (nblk > 0))
    def _():
        in_copy(e, 0, 0).start()

    wgu_bf[...] = wgu_ref[0].astype(BF16)
    wdn_bf[...] = wdn_ref[0].astype(BF16)

    def block(j, carry):
        slot = j % 2
        in_copy(e, j, slot).wait()

        @pl.when(j + 1 < nblk)
        def _():
            in_copy(e, j + 1, 1 - slot).start()

        @pl.when(j >= 2)
        def _():
            out_copy(j - 2, slot).wait()

        gu = jnp.dot(xbuf[slot], wgu_bf[...], preferred_element_type=F32) + bgu_ref[0]
        gate = jnp.minimum(gu[:, :dff], SWIGLU_LIMIT)
        up = jnp.clip(gu[:, dff:], -SWIGLU_LIMIT, SWIGLU_LIMIT)
        glu = gate * jax.nn.sigmoid(gate * SWIGLU_ALPHA)
        act = ((up + 1.0) * glu).astype(BF16)
        ybuf[slot] = (jnp.dot(act, wdn_bf[...], preferred_element_type=F32) + bdn_ref[0]).astype(BF16)
        out_copy(j, slot).start()
        return carry

    lax.fori_loop(0, nblk, block, 0)

    nxt = jnp.minimum(e + 1, ne - 1)

    @pl.when((e + 1 < ne) & (nblk_ref[nxt] > 0))
    def _():
        in_copy(nxt, 0, 0).start()

    @pl.when(nblk >= 2)
    def _():
        out_copy(nblk - 2, nblk % 2).wait()

    @pl.when(nblk >= 1)
    def _():
        out_copy(nblk - 1, (nblk - 1) % 2).wait()


def _experts(row0, nblk, xs, w_gu, b_gu, w_dn, b_dn):
    n_rows, d = xs.shape
    bm = EXPERT_ROWS
    ne, _, dff2 = w_gu.shape
    dff = w_dn.shape[1]
    exp3 = lambda e, r0, nb: (e, 0, 0)
    grid_spec = pltpu.PrefetchScalarGridSpec(
        num_scalar_prefetch=2,
        grid=(ne,),
        in_specs=[pl.BlockSpec(memory_space=pl.ANY),
                  pl.BlockSpec((1, d, dff2), exp3), pl.BlockSpec((1, 1, dff2), exp3),
                  pl.BlockSpec((1, dff, d), exp3), pl.BlockSpec((1, 1, d), exp3)],
        out_specs=pl.BlockSpec(memory_space=pl.ANY),
        scratch_shapes=[pltpu.VMEM((d, dff2), BF16), pltpu.VMEM((dff, d), BF16),
                        pltpu.VMEM((2, bm, d), BF16), pltpu.VMEM((2, bm, d), BF16),
                        pltpu.SemaphoreType.DMA((2,)), pltpu.SemaphoreType.DMA((2,))],
    )
    return pl.pallas_call(
        _expert_kernel,
        grid_spec=grid_spec,
        out_shape=jax.ShapeDtypeStruct((n_rows, d), BF16),
        compiler_params=_params(("arbitrary",)),
        name="experts",
    )(row0, nblk, xs, w_gu, b_gu.reshape(ne, 1, dff2), w_dn, b_dn.reshape(ne, 1, d))


def _combine_kernel(tab_ref, ys_hbm, route_ref, toff_ref, x1_ref, gf_ref, o_ref, buf_ref, sem, *, n_chunks, tile0):
    step = pl.program_id(0)
    n_steps = pl.num_programs(0)
    t = tile0 + step
    slot = step % 2
    nb, tr, d = x1_ref.shape
    m = nb * tr
    rows = buf_ref.shape[1]
    g = BF16_ROWS

    def fetch(tile, which):
        for c in range(n_chunks):
            pltpu.make_async_copy(
                ys_hbm.at[pl.ds(pl.multiple_of(tab_ref[tile, c], g), g)],
                buf_ref.at[which, pl.ds(c * g, g)], sem.at[which]).start()

    @pl.when(step == 0)
    def _():
        fetch(t, slot)

    @pl.when(step + 1 < n_steps)
    def _():
        fetch(t + 1, 1 - slot)

    pos = _slot_rows(route_ref, toff_ref, m)
    gates = [route_ref[0, 2 * TOP_K + k:2 * TOP_K + k + 1, :] for k in range(TOP_K)]
    stacked = jnp.concatenate(pos + gates + [jnp.zeros((LANES - 2 * TOP_K, m), F32)], axis=0)
    cols = stacked.T
    for _ in range(n_chunks):
        pltpu.make_async_copy(ys_hbm.at[pl.ds(0, g)], buf_ref.at[slot, pl.ds(0, g)], sem.at[slot]).wait()

    acc = jnp.zeros((m, d), F32)
    for rc in range(rows // m):
        c_iota = (lax.broadcasted_iota(jnp.int32, (m, m), 1) + rc * m).astype(F32)
        weights = jnp.zeros((m, m), F32)
        for k in range(TOP_K):
            weights = jnp.where(c_iota == cols[:, k:k + 1], cols[:, TOP_K + k:TOP_K + k + 1], weights)
        acc = acc + jnp.dot(weights.astype(BF16), buf_ref[slot, rc * m:(rc + 1) * m, :],
                            preferred_element_type=F32)
    o_ref[...] = x1_ref[...] + gf_ref[...] * acc.reshape(nb, tr, d)


def _combine(table, ys, route, toff_b, x1, gf, nb, tr, tile0, n_chunks):
    nbt, t, d = x1.shape
    m = nb * tr
    assert m == TOKEN_TILE
    rows = _tile_rows(m)
    tiles_per_seq = t // tr
    n_steps = (nbt // nb) * tiles_per_seq
    xmap = lambda s, tab: (s // tiles_per_seq, s % tiles_per_seq, 0)
    grid_spec = pltpu.PrefetchScalarGridSpec(
        num_scalar_prefetch=1,
        grid=(n_steps,),
        in_specs=[pl.BlockSpec(memory_space=pl.ANY),
                  pl.BlockSpec((1, 4 * TOP_K, m), lambda s, tab: (tile0 + s, 0, 0)),
                  pl.BlockSpec((1, N_EXPERTS, LANES), lambda s, tab: (tile0 + s, 0, 0)),
                  pl.BlockSpec((nb, tr, d), xmap),
                  pl.BlockSpec((nb, 1, d), lambda s, tab: (s // tiles_per_seq, 0, 0))],
        out_specs=pl.BlockSpec((nb, tr, d), xmap),
        scratch_shapes=[pltpu.VMEM((2, rows, d), BF16), pltpu.SemaphoreType.DMA((2,))],
    )
    return pl.pallas_call(
        functools.partial(_combine_kernel, n_chunks=n_chunks, tile0=tile0),
        grid_spec=grid_spec,
        out_shape=jax.ShapeDtypeStruct((nbt, t, d), F32),
        compiler_params=_params(("arbitrary",)),
        name="combine",
    )(table, ys, route, toff_b, x1, gf)


def _block_diag(w, groups):
    n, k, _ = w.shape
    w = w.reshape(n // groups, groups, k, k)
    eye = jnp.eye(groups, dtype=w.dtype)
    return jnp.einsum("ngij,gh->ngihj", w, eye).reshape(n // groups, groups * k, groups * k)


def _layer(xp, xs, mod, k_cache, v_cache, conv_state, lru_state, lw):
    (ln_mix, ln_ffn, w_in, q_norm, k_norm, rel_bias, conv_w, conv_b, w_rg, b_rg, w_ig, b_ig, lam,
     w_out, w_router, b_router, w_gu, b_gu, w_dn, b_dn) = lw
    bp, s, d = xp.shape
    bs, ts, _ = xs.shape
    aw = w_out.shape[0] // 2
    nh = aw // HEAD_DIM
    m = TOKEN_TILE
    assert s % m == 0 and bs * ts == m and s % ATTN_Q_TILE == 0 and s % LRU_TILE == 0

    terms = [mod[:, i * d:(i + 1) * d][:, None, :] for i in range(6)]
    tp = [t[:bp] for t in terms]
    tsm = [t[bp:] for t in terms]

    w_in_bf = w_in.astype(BF16)
    w_out_bf = w_out.astype(BF16)
    qn_t = jnp.tile(q_norm, nh).reshape(1, aw)
    kn_t = jnp.tile(k_norm, nh).reshape(1, aw)
    head_mean = _block_diag(jnp.full((nh, HEAD_DIM, HEAD_DIM), 1.0 / HEAD_DIM, F32), nh)[0].astype(BF16)
    groups = MXU_DIM // w_rg.shape[-1]
    wa_bd = _block_diag(w_rg, groups).astype(BF16)
    wx_bd = _block_diag(w_ig, groups).astype(BF16)
    lw_c = b_rg.size
    b_a = b_rg.reshape(1, lw_c)
    b_x = b_ig.reshape(1, lw_c)
    lam2 = lam.reshape(1, lw_c)
    cb2 = conv_b.reshape(1, lw_c)
    ln_mix2 = ln_mix.reshape(1, d)
    ln_ffn2 = ln_ffn.reshape(1, d)
    wr_t = w_router.T
    br = b_router.reshape(-1, 1)
    tab_p = _bias_table(rel_bias, 3 * ATTN_Q_TILE - 1)
    r_cache = k_cache.shape[1]
    tab_s = _bias_table(rel_bias, r_cache + ts - 1)

    qp, kp, vp, xrp, ygp, k32p, v32p = _mixin(xp, tp[0], tp[1], ln_mix2, w_in_bf, qn_t, kn_t, head_mean, 1, m)
    qs, ks, vs, xrs, ygs, k32s, v32s = _mixin(xs, tsm[0], tsm[1], ln_mix2, w_in_bf, qn_t, kn_t, head_mean, bs, ts)
    attn_p = _attn_prompt(qp, kp, vp, tab_p)
    attn_s = _attn_step(qs, ks, vs, jnp.transpose(k_cache, (0, 2, 3, 1)), jnp.transpose(v_cache, (0, 2, 3, 1)),
                        tab_s, ATTN_STEP_BATCH)

    zeros_pre = jnp.zeros((bp, SUBLANES, lw_c), F32)
    zeros_h = jnp.zeros((bp, 1, lw_c), F32)
    pre_s = jnp.pad(conv_state, ((0, 0), (SUBLANES - (CONV_WIDTH - 1), 0), (0, 0)))
    lru_p, tail_p, hl_p = _lru(xrp, ygp, zeros_pre, zeros_h, conv_w, cb2, wa_bd, wx_bd, b_a, b_x, lam2, 1, LRU_TILE)
    lru_s, tail_s, hl_s = _lru(xrs, ygs, pre_s, lru_state[:, None, :], conv_w, cb2, wa_bd, wx_bd, b_a, b_x, lam2,
                               bs, ts)

    n_tiles = bp * (s // m) + 1
    x1p, h2, route, cnt = _outproj(attn_p, lru_p, xp, tp[2], tp[3], tp[4], ln_ffn2, w_out_bf, wr_t, br,
                                   1, m, n_tiles, 0, None)
    x1s, h2, route, cnt = _outproj(attn_s, lru_s, xs, tsm[2], tsm[3], tsm[4], ln_ffn2, w_out_bf, wr_t, br,
                                   bs, ts, n_tiles, n_tiles - 1, (h2, route, cnt))

    n_chunks, _, _, n_rows = _table_sizes(n_tiles)
    dispatch_tab, combine_tab, toff_b, row0, nblk = _route_tables(cnt[:, :, 0].astype(jnp.int32))
    xs_sorted = _dispatch(dispatch_tab, h2, route, toff_b, n_rows, n_chunks)
    ys_sorted = _experts(row0, nblk, xs_sorted, w_gu, b_gu, w_dn, b_dn)
    yp = _combine(combine_tab, ys_sorted, route, toff_b, x1p, tp[5], 1, m, 0, n_chunks)
    ysm = _combine(combine_tab, ys_sorted, route, toff_b, x1s, tsm[5], bs, ts, n_tiles - 1, n_chunks)

    keep = k32p.shape[1]
    new = (k32p.reshape(bp, keep, nh, HEAD_DIM), v32p.reshape(bp, keep, nh, HEAD_DIM),
           tail_p[:, SUBLANES - (CONV_WIDTH - 1):, :], hl_p[:, 0, :],
           k32s.reshape(bs, ts, nh, HEAD_DIM), v32s.reshape(bs, ts, nh, HEAD_DIM),
           tail_s[:, SUBLANES - (CONV_WIDTH - 1):, :], hl_s[:, 0, :])
    return yp, ysm, new


def kernel(x_prompt, x_sample, c_prompt, c_sample, cache_k, cache_v, state_conv, state_lru, ln_mix_w, ln_ffn_w, w_ada, b_ada, w_in, q_norm_w, k_norm_w, rel_bias, conv_w, conv_b, w_rgate, b_rgate, w_igate, b_igate, lru_lambda, w_out, w_router, b_router, w_gate_up, b_gate_up, w_down, b_down):
    depth = w_in.shape[0]
    yp, ys = x_prompt, x_sample
    c_all = jnp.concatenate([c_prompt, c_sample], axis=0)
    collected = [[] for _ in range(8)]
    for l in range(depth):
        mod = _ada(c_all, w_ada[l], b_ada[l])
        lw = (ln_mix_w[l], ln_ffn_w[l], w_in[l], q_norm_w[l], k_norm_w[l], rel_bias[l], conv_w[l], conv_b[l],
              w_rgate[l], b_rgate[l], w_igate[l], b_igate[l], lru_lambda[l], w_out[l], w_router[l], b_router[l],
              w_gate_up[l], b_gate_up[l], w_down[l], b_down[l])
        yp, ys, new = _layer(yp, ys, mod, cache_k[l], cache_v[l], state_conv[l], state_lru[l], lw)
        for acc, val in zip(collected, new):
            acc.append(val)
    return (yp, ys) + tuple(jnp.stack(vals) for vals in collected)
```

```python
import functools

import jax
import jax.numpy as jnp
from jax import lax
from jax.experimental import pallas as pl
from jax.experimental.pallas import tpu as pltpu

F32 = jnp.float32
BF16 = jnp.bfloat16

CHUNK = 64
N_LEFT_CHUNKS = 8
ATTN_WINDOW = N_LEFT_CHUNKS * CHUNK
HEAD_DIM = 64
REL_CLIP = 128
CONV_WIDTH = 4
LRU_C = 8.0
N_EXPERTS = 32
TOP_K = 4
SWIGLU_LIMIT = 7.0
SWIGLU_ALPHA = 1.702
NORM_EPS = 1e-6
NEG_INF = -1e30

LANES = 128
SUBLANES = 8
BF16_ROWS = 16
MXU_DIM = 256

TOKEN_TILE = 512
ATTN_Q_TILE = 256
ATTN_STEP_BATCH = 4
EXPERT_ROWS = 256
BIAS_TABLE = 1024
VMEM_LIMIT = 56 * 1024 * 1024


def _params(sem, vmem=VMEM_LIMIT):
    return pltpu.CompilerParams(dimension_semantics=sem, vmem_limit_bytes=vmem)


def _ada_kernel(c_ref, w_ref, b_ref, o_ref):
    c = c_ref[...]
    s = (c * jax.nn.sigmoid(c)).astype(BF16)
    o_ref[...] = jnp.dot(s, w_ref[...].astype(BF16), preferred_element_type=F32) + b_ref[...]


def _ada(c_all, w_ada, b_ada):
    n, d = c_all.shape
    nout = w_ada.shape[1]
    tn = 1024
    return pl.pallas_call(
        _ada_kernel,
        grid=(nout // tn,),
        in_specs=[pl.BlockSpec((n, d), lambda j: (0, 0)),
                  pl.BlockSpec((d, tn), lambda j: (0, j)),
                  pl.BlockSpec((1, tn), lambda j: (0, j))],
        out_specs=pl.BlockSpec((n, tn), lambda j: (0, j)),
        out_shape=jax.ShapeDtypeStruct((n, nout), F32),
        compiler_params=_params(("arbitrary",)),
        name="ada",
    )(c_all, w_ada, b_ada.reshape(1, nout))


def _mixin_kernel(x_ref, sh_ref, sc_ref, ln_ref, win_ref, qn_ref, kn_ref, bd_ref,
                  pre_ref, h0_ref, cw_ref, cb_ref, wa_ref, wx_ref, ba_ref, bx_ref, lam_ref,
                  q_ref, k_ref, v_ref, k32_ref, v32_ref, lru_ref, tail_ref, hl_ref, cx_ref, ch_ref):
    nb, tr, d = x_ref.shape
    m = nb * tr
    aw = q_ref.shape[-1]
    x = x_ref[...]
    ms = jnp.mean(x * x, axis=-1, keepdims=True)
    h = x * lax.rsqrt(ms + NORM_EPS) * ln_ref[...]
    h = h * (1.0 + sc_ref[...]) + sh_ref[...]
    hb = h.reshape(m, d).astype(BF16)

    def proj(part):
        return jnp.dot(hb, win_ref[:, part * aw:(part + 1) * aw], preferred_element_type=F32)

    def head_norm(t, w_ref):
        msq = jnp.dot((t * t).astype(BF16), bd_ref[...], preferred_element_type=F32)
        return t * lax.rsqrt(msq + NORM_EPS) * w_ref[...]

    q = head_norm(proj(0), qn_ref)
    k = head_norm(proj(1), kn_ref)
    v = proj(2)
    q_ref[...] = (q * (HEAD_DIM ** -0.5)).astype(BF16).reshape(nb, tr, aw)
    k_ref[...] = k.astype(BF16).reshape(nb, tr, aw)
    v_ref[...] = v.astype(BF16).reshape(nb, tr, aw)
    k32_ref[...] = k.reshape(nb, tr, aw)
    v32_ref[...] = v.reshape(nb, tr, aw)
    lru_out, new_tail, h_last = _lru_branch(
        proj(3).reshape(nb, tr, aw), proj(4).reshape(nb, tr, aw), pre_ref, h0_ref, cw_ref, cb_ref,
        wa_ref, wx_ref, ba_ref, bx_ref, lam_ref, cx_ref, ch_ref)
    lru_ref[...] = lru_out
    tail_ref[...] = new_tail
    hl_ref[...] = h_last


def _mixin(x, sh, sc, ln_w, w_in_bf, qn_t, kn_t, bd, pre, h0, conv_w, conv_b, wa_bd, wx_bd, b_a, b_x, lam,
           nb, tr):
    nbt, t, d = x.shape
    aw = qn_t.shape[-1]
    c = pre.shape[-1]
    assert c == aw
    keep = min(ATTN_WINDOW, t)
    assert tr == keep or t == tr
    grid = (nbt // nb, t // tr)
    xmap = lambda b, i: (b, i, 0)
    mmap = lambda b, i: (b, 0, 0)
    cmap = lambda b, i: (0, 0)
    cmap3 = lambda b, i: (0, 0, 0)
    tmap = lambda b, i: (b, 0, 0)
    big = pl.BlockSpec((nb, tr, aw), xmap)
    tail = pl.BlockSpec((nb, keep, aw), tmap)
    row = pl.BlockSpec((1, c), cmap)
    return pl.pallas_call(
        _mixin_kernel,
        grid=grid,
        in_specs=[pl.BlockSpec((nb, tr, d), xmap),
                  pl.BlockSpec((nb, 1, d), mmap), pl.BlockSpec((nb, 1, d), mmap),
                  pl.BlockSpec((1, d), cmap),
                  pl.BlockSpec(w_in_bf.shape, cmap),
                  pl.BlockSpec((1, aw), cmap), pl.BlockSpec((1, aw), cmap),
                  pl.BlockSpec(bd.shape, cmap),
                  pl.BlockSpec((nb, SUBLANES, c), tmap), pl.BlockSpec((nb, 1, c), tmap),
                  pl.BlockSpec(conv_w.shape, cmap), row,
                  pl.BlockSpec(wa_bd.shape, cmap3), pl.BlockSpec(wx_bd.shape, cmap3),
                  row, row, row],
        out_specs=[big, big, big, tail, tail, big,
                   pl.BlockSpec((nb, SUBLANES, c), tmap), pl.BlockSpec((nb, 1, c), tmap)],
        out_shape=[jax.ShapeDtypeStruct((nbt, t, aw), BF16)] * 3
        + [jax.ShapeDtypeStruct((nbt, keep, aw), F32)] * 2
        + [jax.ShapeDtypeStruct((nbt, t, c), BF16),
           jax.ShapeDtypeStruct((nbt, SUBLANES, c), F32),
           jax.ShapeDtypeStruct((nbt, 1, c), F32)],
        scratch_shapes=[pltpu.VMEM((nb, SUBLANES, c), F32), pltpu.VMEM((nb, 1, c), F32)],
        compiler_params=_params(("arbitrary", "arbitrary")),
        name="mixin",
    )(x, sh, sc, ln_w, w_in_bf, qn_t, kn_t, bd, pre, h0, conv_w, conv_b, wa_bd, wx_bd, b_a, b_x, lam)


def _bias_table(rel_bias, off):
    h = rel_bias.shape[0]
    left = off - REL_CLIP
    right = BIAS_TABLE - left - (2 * REL_CLIP + 1)
    assert left >= 0 and right >= 0
    return jnp.concatenate([jnp.broadcast_to(rel_bias[:, :1], (h, left)), rel_bias,
                            jnp.broadcast_to(rel_bias[:, -1:], (h, right))], axis=1)


def _toeplitz(tab_row, rows, cols):
    t = jnp.broadcast_to(tab_row, (rows, BIAS_TABLE))
    t = pltpu.roll(t, BIAS_TABLE - (rows - 1), 1, stride=1, stride_axis=0)
    return t[:, :cols]


def _attn_kernel(q_ref, k0_ref, k1_ref, k2_ref, v0_ref, v1_ref, v2_ref, tab_ref, o_ref, bias_ref):
    b = pl.program_id(0)
    s = pl.program_id(1)
    qt = q_ref.shape[1]
    nk = 3 * qt
    nh = bias_ref.shape[0]

    @pl.when((b == 0) & (s == 0))
    def _():
        qi = lax.broadcasted_iota(jnp.int32, (qt, nk), 0) // CHUNK
        kc = lax.broadcasted_iota(jnp.int32, (qt, nk), 1) // CHUNK
        for h in range(nh):
            band = jnp.where(kc <= qi + N_LEFT_CHUNKS, _toeplitz(tab_ref[h:h + 1, :], qt, nk), NEG_INF)
            bias_ref[h] = jnp.where(kc >= qi, band, NEG_INF)

    q = q_ref[0]
    kcat = jnp.concatenate([k0_ref[0], k1_ref[0], k2_ref[0]], axis=0)
    vcat = jnp.concatenate([v0_ref[0], v1_ref[0], v2_ref[0]], axis=0)
    in_seq = lax.broadcasted_iota(jnp.int32, (qt, nk), 1) >= (2 - s) * qt
    outs = []
    for h in range(nh):
        sl = slice(h * HEAD_DIM, (h + 1) * HEAD_DIM)
        sc = lax.dot_general(q[:, sl], kcat[:, sl], (((1,), (1,)), ((), ())), preferred_element_type=F32)
        sc = jnp.where(in_seq, sc + bias_ref[h], NEG_INF)
        mx = jnp.max(sc, axis=-1, keepdims=True)
        p = jnp.exp(sc - mx)
        l = jnp.sum(p, axis=-1, keepdims=True)
        o = jnp.dot(p.astype(BF16), vcat[:, sl], preferred_element_type=F32)
        outs.append(o / l)
    o_ref[0] = jnp.concatenate(outs, axis=-1).astype(BF16)


def _attn_prompt(q, k, v, tab):
    b, s, aw = q.shape
    qt = ATTN_Q_TILE
    nh = aw // HEAD_DIM
    qspec = pl.BlockSpec((1, qt, aw), lambda i, j: (i, j, 0))

    def kspec(back):
        return pl.BlockSpec((1, qt, aw), lambda i, j: (i, jnp.maximum(j - back, 0), 0))

    return pl.pallas_call(
        _attn_kernel,
        grid=(b, s // qt),
        in_specs=[qspec, kspec(2), kspec(1), kspec(0), kspec(2), kspec(1), kspec(0),
                  pl.BlockSpec(tab.shape, lambda i, j: (0, 0))],
        out_specs=qspec,
        out_shape=jax.ShapeDtypeStruct((b, s, aw), BF16),
        scratch_shapes=[pltpu.VMEM((nh, qt, 3 * qt), F32)],
        compiler_params=_params(("arbitrary", "arbitrary")),
        name="attn_prompt",
    )(q, k, k, k, v, v, v, tab)


def _attn_step_kernel(q_ref, kn_ref, vn_ref, ck_ref, cv_ref, tab_ref, o_ref, bias_ref):
    step = pl.program_id(0)
    nbs, t, _ = q_ref.shape
    nh = bias_ref.shape[0]
    r = ck_ref.shape[-1]
    nk = r + LANES
    nt_dims = (((1,), (1,)), ((), ()))

    @pl.when(step == 0)
    def _():
        ok = lax.broadcasted_iota(jnp.int32, (t, nk), 1) < r + t
        for h in range(nh):
            bias_ref[h] = jnp.where(ok, _toeplitz(tab_ref[h:h + 1, :], t, nk), NEG_INF)

    pad = jnp.zeros((LANES - t, HEAD_DIM), BF16)
    for b in range(nbs):
        q = q_ref[b]
        kn = kn_ref[b]
        vn = vn_ref[b]
        outs = []
        for h in range(nh):
            sl = slice(h * HEAD_DIM, (h + 1) * HEAD_DIM)
            qh = q[:, sl]
            k_new = jnp.concatenate([kn[:, sl], pad], axis=0)
            v_new = jnp.concatenate([vn[:, sl], pad], axis=0)
            s_old = jnp.dot(qh, ck_ref[b, h].astype(BF16), preferred_element_type=F32) + bias_ref[h, :, :r]
            s_new = lax.dot_general(qh, k_new, nt_dims, preferred_element_type=F32) + bias_ref[h, :, r:]
            mx = jnp.maximum(jnp.max(s_old, axis=-1, keepdims=True), jnp.max(s_new, axis=-1, keepdims=True))
            p_old = jnp.exp(s_old - mx)
            p_new = jnp.exp(s_new - mx)
            l = jnp.sum(p_old, axis=-1, keepdims=True) + jnp.sum(p_new, axis=-1, keepdims=True)
            o = (lax.dot_general(p_old.astype(BF16), cv_ref[b, h].astype(BF16), nt_dims,
                                 preferred_element_type=F32)
                 + jnp.dot(p_new.astype(BF16), v_new, preferred_element_type=F32))
            outs.append(o / l)
        o_ref[b] = jnp.concatenate(outs, axis=-1).astype(BF16)


def _attn_step(q, kn, vn, ck, cv, tab, nbs):
    b, t, aw = q.shape
    nh = aw // HEAD_DIM
    r = ck.shape[-1]
    new = pl.BlockSpec((nbs, t, aw), lambda i: (i, 0, 0))
    old = pl.BlockSpec((nbs, nh, HEAD_DIM, r), lambda i: (i, 0, 0, 0))
    return pl.pallas_call(
        _attn_step_kernel,
        grid=(b // nbs,),
        in_specs=[new, new, new, old, old, pl.BlockSpec(tab.shape, lambda i: (0, 0))],
        out_specs=new,
        out_shape=jax.ShapeDtypeStruct((b, t, aw), BF16),
        scratch_shapes=[pltpu.VMEM((nh, t, r + LANES), F32)],
        compiler_params=_params(("arbitrary",)),
        name="attn_step",
    )(q, kn, vn, ck, cv, tab)


def _gelu_tanh(x):
    return x * (0.5 * (1.0 + jnp.tanh(0.7978845608028654 * (x + 0.044715 * (x * x * x)))))


def _lru_branch(x, yg, pre_ref, h0_ref, cw_ref, cb_ref, wa_ref, wx_ref, ba_ref, bx_ref, lam_ref, cx_ref, ch_ref):
    step = pl.program_id(1)
    nb, tr, c = x.shape
    m = nb * tr
    half = c // 2

    @pl.when(step == 0)
    def _():
        cx_ref[...] = pre_ref[...]
        ch_ref[...] = h0_ref[...]

    xp = jnp.concatenate([cx_ref[...], x], axis=1)
    new_tail = xp[:, tr:tr + SUBLANES, :]
    xp2 = xp.reshape(nb * (tr + SUBLANES), c)
    y = cb_ref[...] + cw_ref[CONV_WIDTH - 1:CONV_WIDTH, :] * x
    for back in range(1, CONV_WIDTH):
        shifted = pltpu.roll(xp2, back, 0).reshape(nb, tr + SUBLANES, c)[:, SUBLANES:, :]
        y = y + cw_ref[CONV_WIDTH - 1 - back:CONV_WIDTH - back, :] * shifted
    y2 = y.reshape(m, c)
    yb = y2.astype(BF16)

    def gate(w_ref, b_ref):
        g = jnp.concatenate(
            [jnp.dot(yb[:, :half], w_ref[0], preferred_element_type=F32),
             jnp.dot(yb[:, half:], w_ref[1], preferred_element_type=F32)], axis=1)
        return jax.nn.sigmoid(g + b_ref[...])

    rg = gate(wa_ref, ba_ref)
    ig = gate(wx_ref, bx_ref)
    lam = lam_ref[...]
    log_sig = jnp.minimum(lam, 0.0) - jnp.log1p(jnp.exp(-jnp.abs(lam)))
    log_a = LRU_C * rg * log_sig
    a_cum = jnp.exp(log_a)
    b_cum = jnp.sqrt(-jnp.tanh(log_a) * (a_cum * a_cum + 1.0)) * (ig * y2)
    row = lax.broadcasted_iota(jnp.int32, (m, c), 0) % SUBLANES
    dist = 1
    while dist < SUBLANES:
        keep = row >= dist
        a_sh = jnp.where(keep, pltpu.roll(a_cum, dist, 0), 1.0)
        b_sh = jnp.where(keep, pltpu.roll(b_cum, dist, 0), 0.0)
        b_cum = a_cum * b_sh + b_cum
        a_cum = a_cum * a_sh
        dist *= 2
    groups = tr // SUBLANES
    a_grp = a_cum.reshape(nb, groups, SUBLANES, c)
    b_grp = b_cum.reshape(nb, groups, SUBLANES, c)
    carry = ch_ref[...]
    pieces = []
    for grp in range(groups):
        h_grp = a_grp[:, grp] * carry + b_grp[:, grp]
        carry = h_grp[:, SUBLANES - 1:SUBLANES, :]
        pieces.append(h_grp)
    h = jnp.concatenate(pieces, axis=1)
    ch_ref[...] = carry
    cx_ref[...] = new_tail
    return (h * _gelu_tanh(yg)).astype(BF16), new_tail, carry


def _outproj_kernel(*refs, aliased):
    (at_ref, lr_ref, x_ref, gm_ref, shf_ref, scf_ref, lnf_ref, wo_ref, wr_ref, br_ref) = refs[:10]
    x1_ref, h2_ref, route_ref, cnt_ref = refs[10 + aliased:]
    nb, tr, d = x_ref.shape
    m = nb * tr
    aw = at_ref.shape[-1]
    ne = wr_ref.shape[0]
    at = at_ref[...].reshape(m, aw)
    lr = lr_ref[...].reshape(m, aw)
    mix = (jnp.dot(at, wo_ref[0:aw, :], preferred_element_type=F32)
           + jnp.dot(lr, wo_ref[aw:2 * aw, :], preferred_element_type=F32))
    x1 = x_ref[...] + gm_ref[...] * mix.reshape(nb, tr, d)
    x1_ref[...] = x1
    ms = jnp.mean(x1 * x1, axis=-1, keepdims=True)
    h2 = x1 * lax.rsqrt(ms + NORM_EPS) * lnf_ref[...]
    h2 = (h2 * (1.0 + scf_ref[...]) + shf_ref[...]).reshape(m, d)
    h2_ref[...] = h2.astype(BF16)

    logits = lax.dot_general(wr_ref[...], h2.astype(BF16), (((1,), (1,)), ((), ())),
                             preferred_element_type=F32) + br_ref[...]
    e_iota = lax.broadcasted_iota(jnp.int32, (ne, m), 0).astype(F32)
    vals = logits
    top_v, sels = [], []
    for k in range(TOP_K):
        mx = jnp.max(vals, axis=0, keepdims=True)
        idx = jnp.min(jnp.where(vals == mx, e_iota, float(ne)), axis=0, keepdims=True)
        sel = e_iota == idx
        vals = jnp.where(sel, -jnp.inf, vals)
        top_v.append(mx)
        sels.append(sel)
        route_ref[0, k:k + 1, :] = idx
    ex = [jnp.exp(v - top_v[0]) for v in top_v]
    den = ex[0] + ex[1] + ex[2] + ex[3]
    chosen = jnp.zeros((ne, m), F32)
    for k in range(TOP_K):
        route_ref[0, 2 * TOP_K + k:2 * TOP_K + k + 1, :] = ex[k] / den
        chosen = chosen + jnp.where(sels[k], 1.0, 0.0)
    before = (lax.broadcasted_iota(jnp.int32, (m, m), 0) < lax.broadcasted_iota(jnp.int32, (m, m), 1))
    rank = jnp.dot(chosen.astype(BF16), jnp.where(before, 1.0, 0.0).astype(BF16), preferred_element_type=F32)
    for k in range(TOP_K):
        route_ref[0, TOP_K + k:TOP_K + k + 1, :] = jnp.sum(jnp.where(sels[k], rank, 0.0), axis=0, keepdims=True)
    route_ref[0, 3 * TOP_K:4 * TOP_K, :] = jnp.zeros((TOP_K, m), F32)
    cnt_ref[0] = jnp.broadcast_to(jnp.sum(chosen, axis=1, keepdims=True), (ne, LANES))


def _outproj(attn, lru_o, x, gm, shf, scf, lnf, w_out_bf, wr_t, br, nb, tr, n_tiles, tile0, prev):
    nbt, t, d = x.shape
    aw = attn.shape[-1]
    m = nb * tr
    assert m == TOKEN_TILE
    ne = wr_t.shape[0]
    tiles_per_seq = t // tr
    xmap = lambda b, i: (b, i, 0)
    mmap = lambda b, i: (b, 0, 0)
    c2 = lambda b, i: (0, 0)
    tile = lambda b, i: (tile0 + b * tiles_per_seq + i, 0)
    tile3 = lambda b, i: (tile0 + b * tiles_per_seq + i, 0, 0)
    mod = pl.BlockSpec((nb, 1, d), mmap)
    in_specs = [pl.BlockSpec((nb, tr, aw), xmap), pl.BlockSpec((nb, tr, aw), xmap),
                pl.BlockSpec((nb, tr, d), xmap), mod, mod, mod,
                pl.BlockSpec((1, d), c2), pl.BlockSpec(w_out_bf.shape, c2),
                pl.BlockSpec(wr_t.shape, c2), pl.BlockSpec((ne, 1), c2)]
    args = [attn, lru_o, x, gm, shf, scf, lnf, w_out_bf, wr_t, br]
    aliases = {}
    if prev is not None:
        in_specs += [pl.BlockSpec(memory_space=pl.ANY)] * 3
        args += list(prev)
        aliases = {10: 1, 11: 2, 12: 3}
    return pl.pallas_call(
        functools.partial(_outproj_kernel, aliased=len(aliases)),
        grid=(nbt // nb, tiles_per_seq),
        in_specs=in_specs,
        out_specs=[pl.BlockSpec((nb, tr, d), xmap), pl.BlockSpec((m, d), tile),
                   pl.BlockSpec((1, 4 * TOP_K, m), tile3), pl.BlockSpec((1, ne, LANES), tile3)],
        out_shape=[jax.ShapeDtypeStruct((nbt, t, d), F32),
                   jax.ShapeDtypeStruct((n_tiles * m, d), BF16),
                   jax.ShapeDtypeStruct((n_tiles, 4 * TOP_K, m), F32),
                   jax.ShapeDtypeStruct((n_tiles, ne, LANES), F32)],
        input_output_aliases=aliases,
        compiler_params=_params(("arbitrary", "arbitrary")),
        name="outproj",
    )(*args)


def _tile_rows(m):
    cap = TOP_K * m + N_EXPERTS * (BF16_ROWS - 1) + BF16_ROWS
    return -(-cap // TOKEN_TILE) * TOKEN_TILE


def _table_sizes(nt):
    g = BF16_ROWS
    m = TOKEN_TILE
    n_chunks = _tile_rows(m) // g
    n_gap = -(-(N_EXPERTS * (EXPERT_ROWS // g - 1)) // nt)
    bound = TOP_K * m * nt + nt * N_EXPERTS * (g - 1) + N_EXPERTS * (EXPERT_ROWS - g)
    n_sorted = -(-bound // EXPERT_ROWS) * EXPERT_ROWS
    return n_chunks, n_gap, n_sorted, n_sorted + 2 * (n_chunks + n_gap) * g


def _route_tables(cnt):
    nt = cnt.shape[0]
    g = BF16_ROWS
    bm = EXPERT_ROWS
    n_chunks, n_gap, n_sorted, _ = _table_sizes(nt)
    e_ids = jnp.arange(N_EXPERTS, dtype=jnp.int32)
    t_ids = jnp.arange(nt, dtype=jnp.int32)
    upto = (e_ids[:, None] <= e_ids[None, :]).astype(jnp.int32)
    pc = (cnt + g - 1) // g * g
    ctile = jnp.sum(pc[:, :, None] * upto[None], axis=1)
    toff = ctile - pc
    trow = ctile[:, -1]
    tot = jnp.sum(pc, axis=0)
    reg = (tot + bm - 1) // bm * bm
    creg = jnp.sum(reg[:, None] * upto, axis=0)
    base = creg - reg
    earlier = (t_ids[:, None] < t_ids[None, :]).astype(jnp.int32)
    goff = base[None, :] + jnp.sum(pc[:, None, :] * earlier[:, :, None], axis=0)
    r = jnp.arange(n_chunks, dtype=jnp.int32) * g
    r3 = r[None, :, None]
    in_seg = (toff[:, None, :] <= r3) & (r3 < ctile[:, None, :])
    dst = jnp.sum(jnp.where(in_seg, (goff - toff)[:, None, :], 0), axis=2) + r[None, :]
    dst = jnp.where(r[None, :] < trow[:, None], dst, -1)
    gcnt = (reg - tot) // g
    gcum = jnp.sum(gcnt[:, None] * upto, axis=0)
    gstart = gcum - gcnt
    s = jnp.arange(nt * n_gap, dtype=jnp.int32)
    in_gap = (gstart[None, :] <= s[:, None]) & (s[:, None] < gcum[None, :])
    gdst = jnp.sum(jnp.where(in_gap, (base + tot - g * gstart)[None, :] + g * s[:, None], 0), axis=1)
    gdst = jnp.where(s < gcum[-1], gdst, -1).reshape(nt, n_gap)
    table = jnp.concatenate([dst, gdst], axis=1).astype(jnp.int32)
    n_entries = n_chunks + n_gap
    spare = n_sorted + ((t_ids % 2)[:, None] * n_entries + jnp.arange(n_entries, dtype=jnp.int32)[None, :]) * g
    dispatch_tab = jnp.where(table >= 0, table, spare).astype(jnp.int32)
    combine_tab = jnp.maximum(dst, 0).astype(jnp.int32)
    toff_b = jnp.broadcast_to(toff.astype(F32)[:, :, None], (nt, N_EXPERTS, LANES))
    return dispatch_tab, combine_tab, toff_b, base.astype(jnp.int32), (reg // bm).astype(jnp.int32)


def _slot_rows(route_ref, toff_ref, m):
    ne = toff_ref.shape[1]
    e_iota = lax.broadcasted_iota(jnp.int32, (ne, m), 0).astype(F32)
    toff_col = toff_ref[0][:, 0:1]
    pos = []
    for k in range(TOP_K):
        sel = e_iota == route_ref[0, k:k + 1, :]
        start = jnp.sum(jnp.where(sel, toff_col, 0.0), axis=0, keepdims=True)
        pos.append(start + route_ref[0, TOP_K + k:TOP_K + k + 1, :])
    return pos


def _dispatch_kernel(tab_ref, h2_ref, route_ref, toff_ref, xs_hbm, buf_ref, sem, *, n_chunks, n_tiles):
    t = pl.program_id(0)
    slot = t % 2
    m = h2_ref.shape[0]
    rows = buf_ref.shape[1]
    n_entries = tab_ref.shape[1]
    g = BF16_ROWS
    per_chunk = m // g

    def start(c):
        src = c * g if c < n_chunks else rows - g
        pltpu.make_async_copy(
            buf_ref.at[slot, pl.ds(src, g)],
            xs_hbm.at[pl.ds(pl.multiple_of(tab_ref[t, c], g), g)], sem.at[slot]).start()

    def wait_all(which):
        for _ in range(n_entries):
            pltpu.make_async_copy(buf_ref.at[which, pl.ds(0, g)], xs_hbm.at[pl.ds(0, g)], sem.at[which]).wait()

    @pl.when(t >= 2)
    def _():
        wait_all(slot)

    pos = _slot_rows(route_ref, toff_ref, m)
    h2 = h2_ref[...]
    for rc in range(rows // m):
        r_iota = (lax.broadcasted_iota(jnp.int32, (m, m), 0) + rc * m).astype(F32)
        onehot = jnp.zeros((m, m), F32)
        for k in range(TOP_K):
            onehot = jnp.where(r_iota == pos[k], 1.0, onehot)
        buf_ref[slot, rc * m:(rc + 1) * m, :] = jnp.dot(
            onehot.astype(BF16), h2, preferred_element_type=F32).astype(BF16)
        for c in range((rc - 1) * per_chunk, rc * per_chunk) if rc > 0 else ():
            start(c)
    for c in range(n_chunks - per_chunk, n_entries):
        start(c)

    @pl.when(t == n_tiles - 1)
    def _():
        if n_tiles > 1:
            wait_all(1 - slot)
        wait_all(slot)


def _dispatch(table, h2, route, toff_b, n_rows, n_chunks):
    nt = route.shape[0]
    m = TOKEN_TILE
    d = h2.shape[1]
    rows = _tile_rows(m)
    grid_spec = pltpu.PrefetchScalarGridSpec(
        num_scalar_prefetch=1,
        grid=(nt,),
        in_specs=[pl.BlockSpec((m, d), lambda t, tab: (t, 0)),
                  pl.BlockSpec((1, 4 * TOP_K, m), lambda t, tab: (t, 0, 0)),
                  pl.BlockSpec((1, N_EXPERTS, LANES), lambda t, tab: (t, 0, 0))],
        out_specs=pl.BlockSpec(memory_space=pl.ANY),
        scratch_shapes=[pltpu.VMEM((2, rows, d), BF16), pltpu.SemaphoreType.DMA((2,))],
    )
    return pl.pallas_call(
        functools.partial(_dispatch_kernel, n_chunks=n_chunks, n_tiles=nt),
        grid_spec=grid_spec,
        out_shape=jax.ShapeDtypeStruct((n_rows, d), BF16),
        compiler_params=_params(("arbitrary",)),
        name="dispatch",
    )(table, h2, route, toff_b)


def _expert_kernel(row0_ref, nblk_ref, xs_hbm, wgu_ref, bgu_ref, wdn_ref, bdn_ref, ys_hbm,
                   wgu_bf, wdn_bf, xbuf, ybuf, sem_in, sem_out):
    e = pl.program_id(0)
    ne = pl.num_programs(0)
    bm = xbuf.shape[1]
    dff = wdn_ref.shape[1]
    nblk = nblk_ref[e]

    def in_copy(expert, j, slot):
        start = pl.multiple_of(row0_ref[expert] + j * bm, bm)
        return pltpu.make_async_copy(xs_hbm.at[pl.ds(start, bm)], xbuf.at[slot], sem_in.at[slot])

    def out_copy(j, slot):
        start = pl.multiple_of(row0_ref[e] + j * bm, bm)
        return pltpu.make_async_copy(ybuf.at[slot], ys_hbm.at[pl.ds(start, bm)], sem_out.at[slot])

    @pl.when((e == 0) & (nblk > 0))
    def _():
        in_copy(e, 0, 0).start()

    wgu_bf[...] = wgu_ref[0].astype(BF16)
    wdn_bf[...] = wdn_ref[0].astype(BF16)

    def block(j, carry):
        slot = j % 2
        in_copy(e, j, slot).wait()

        @pl.when(j + 1 < nblk)
        def _():
            in_copy(e, j + 1, 1 - slot).start()

        @pl.when(j >= 2)
        def _():
            out_copy(j - 2, slot).wait()

        gu = jnp.dot(xbuf[slot], wgu_bf[...], preferred_element_type=F32) + bgu_ref[0]
        gate = jnp.minimum(gu[:, :dff], SWIGLU_LIMIT)
        up = jnp.clip(gu[:, dff:], -SWIGLU_LIMIT, SWIGLU_LIMIT)
        glu = gate * jax.nn.sigmoid(gate * SWIGLU_ALPHA)
        act = ((up + 1.0) * glu).astype(BF16)
        ybuf[slot] = (jnp.dot(act, wdn_bf[...], preferred_element_type=F32) + bdn_ref[0]).astype(BF16)
        out_copy(j, slot).start()
        return carry

    lax.fori_loop(0, nblk, block, 0)

    nxt = jnp.minimum(e + 1, ne - 1)

    @pl.when((e + 1 < ne) & (nblk_ref[nxt] > 0))
    def _():
        in_copy(nxt, 0, 0).start()

    @pl.when(nblk >= 2)
    def _():
        out_copy(nblk - 2, nblk % 2).wait()

    @pl.when(nblk >= 1)
    def _():
        out_copy(nblk - 1, (nblk - 1) % 2).wait()


def _experts(row0, nblk, xs, w_gu, b_gu, w_dn, b_dn):
    n_rows, d = xs.shape
    bm = EXPERT_ROWS
    ne, _, dff2 = w_gu.shape
    dff = w_dn.shape[1]
    exp3 = lambda e, r0, nb: (e, 0, 0)
    grid_spec = pltpu.PrefetchScalarGridSpec(
        num_scalar_prefetch=2,
        grid=(ne,),
        in_specs=[pl.BlockSpec(memory_space=pl.ANY),
                  pl.BlockSpec((1, d, dff2), exp3), pl.BlockSpec((1, 1, dff2), exp3),
                  pl.BlockSpec((1, dff, d), exp3), pl.BlockSpec((1, 1, d), exp3)],
        out_specs=pl.BlockSpec(memory_space=pl.ANY),
        scratch_shapes=[pltpu.VMEM((d, dff2), BF16), pltpu.VMEM((dff, d), BF16),
                        pltpu.VMEM((2, bm, d), BF16), pltpu.VMEM((2, bm, d), BF16),
                        pltpu.SemaphoreType.DMA((2,)), pltpu.SemaphoreType.DMA((2,))],
    )
    return pl.pallas_call(
        _expert_kernel,
        grid_spec=grid_spec,
        out_shape=jax.ShapeDtypeStruct((n_rows, d), BF16),
        compiler_params=_params(("arbitrary",)),
        name="experts",
    )(row0, nblk, xs, w_gu, b_gu.reshape(ne, 1, dff2), w_dn, b_dn.reshape(ne, 1, d))


def _combine_kernel(tab_ref, ys_hbm, route_ref, toff_ref, x1_ref, gf_ref, o_ref, buf_ref, sem, *, n_chunks, tile0):
    step = pl.program_id(0)
    n_steps = pl.num_programs(0)
    t = tile0 + step
    slot = step % 2
    nb, tr, d = x1_ref.shape
    m = nb * tr
    rows = buf_ref.shape[1]
    g = BF16_ROWS

    def fetch(tile, which):
        for c in range(n_chunks):
            pltpu.make_async_copy(
                ys_hbm.at[pl.ds(pl.multiple_of(tab_ref[tile, c], g), g)],
                buf_ref.at[which, pl.ds(c * g, g)], sem.at[which]).start()

    @pl.when(step == 0)
    def _():
        fetch(t, slot)

    @pl.when(step + 1 < n_steps)
    def _():
        fetch(t + 1, 1 - slot)

    pos = _slot_rows(route_ref, toff_ref, m)
    gates = [route_ref[0, 2 * TOP_K + k:2 * TOP_K + k + 1, :] for k in range(TOP_K)]
    stacked = jnp.concatenate(pos + gates + [jnp.zeros((LANES - 2 * TOP_K, m), F32)], axis=0)
    cols = stacked.T
    for _ in range(n_chunks):
        pltpu.make_async_copy(ys_hbm.at[pl.ds(0, g)], buf_ref.at[slot, pl.ds(0, g)], sem.at[slot]).wait()

    acc = jnp.zeros((m, d), F32)
    for rc in range(rows // m):
        c_iota = (lax.broadcasted_iota(jnp.int32, (m, m), 1) + rc * m).astype(F32)
        weights = jnp.zeros((m, m), F32)
        for k in range(TOP_K):
            weights = jnp.where(c_iota == cols[:, k:k + 1], cols[:, TOP_K + k:TOP_K + k + 1], weights)
        acc = acc + jnp.dot(weights.astype(BF16), buf_ref[slot, rc * m:(rc + 1) * m, :],
                            preferred_element_type=F32)
    o_ref[...] = x1_ref[...] + gf_ref[...] * acc.reshape(nb, tr, d)


def _combine(table, ys, route, toff_b, x1, gf, nb, tr, tile0, n_chunks):
    nbt, t, d = x1.shape
    m = nb * tr
    assert m == TOKEN_TILE
    rows = _tile_rows(m)
    tiles_per_seq = t // tr
    n_steps = (nbt // nb) * tiles_per_seq
    xmap = lambda s, tab: (s // tiles_per_seq, s % tiles_per_seq, 0)
    grid_spec = pltpu.PrefetchScalarGridSpec(
        num_scalar_prefetch=1,
        grid=(n_steps,),
        in_specs=[pl.BlockSpec(memory_space=pl.ANY),
                  pl.BlockSpec((1, 4 * TOP_K, m), lambda s, tab: (tile0 + s, 0, 0)),
                  pl.BlockSpec((1, N_EXPERTS, LANES), lambda s, tab: (tile0 + s, 0, 0)),
                  pl.BlockSpec((nb, tr, d), xmap),
                  pl.BlockSpec((nb, 1, d), lambda s, tab: (s // tiles_per_seq, 0, 0))],
        out_specs=pl.BlockSpec((nb, tr, d), xmap),
        scratch_shapes=[pltpu.VMEM((2, rows, d), BF16), pltpu.SemaphoreType.DMA((2,))],
    )
    return pl.pallas_call(
        functools.partial(_combine_kernel, n_chunks=n_chunks, tile0=tile0),
        grid_spec=grid_spec,
        out_shape=jax.ShapeDtypeStruct((nbt, t, d), F32),
        compiler_params=_params(("arbitrary",)),
        name="combine",
    )(table, ys, route, toff_b, x1, gf)


def _block_diag(w, groups):
    n, k, _ = w.shape
    w = w.reshape(n // groups, groups, k, k)
    eye = jnp.eye(groups, dtype=w.dtype)
    return jnp.einsum("ngij,gh->ngihj", w, eye).reshape(n // groups, groups * k, groups * k)


def _layer(xp, xs, mod, k_cache, v_cache, conv_state, lru_state, lw):
    (ln_mix, ln_ffn, w_in, q_norm, k_norm, rel_bias, conv_w, conv_b, w_rg, b_rg, w_ig, b_ig, lam,
     w_out, w_router, b_router, w_gu, b_gu, w_dn, b_dn) = lw
    bp, s, d = xp.shape
    bs, ts, _ = xs.shape
    aw = w_out.shape[0] // 2
    nh = aw // HEAD_DIM
    m = TOKEN_TILE
    assert s % m == 0 and bs * ts == m and s % ATTN_Q_TILE == 0

    terms = [mod[:, i * d:(i + 1) * d][:, None, :] for i in range(6)]
    tp = [t[:bp] for t in terms]
    tsm = [t[bp:] for t in terms]

    w_in_bf = w_in.astype(BF16)
    w_out_bf = w_out.astype(BF16)
    qn_t = jnp.tile(q_norm, nh).reshape(1, aw)
    kn_t = jnp.tile(k_norm, nh).reshape(1, aw)
    head_mean = _block_diag(jnp.full((nh, HEAD_DIM, HEAD_DIM), 1.0 / HEAD_DIM, F32), nh)[0].astype(BF16)
    groups = MXU_DIM // w_rg.shape[-1]
    wa_bd = _block_diag(w_rg, groups).astype(BF16)
    wx_bd = _block_diag(w_ig, groups).astype(BF16)
    lw_c = b_rg.size
    b_a = b_rg.reshape(1, lw_c)
    b_x = b_ig.reshape(1, lw_c)
    lam2 = lam.reshape(1, lw_c)
    cb2 = conv_b.reshape(1, lw_c)
    ln_mix2 = ln_mix.reshape(1, d)
    ln_ffn2 = ln_ffn.reshape(1, d)
    wr_t = w_router.T.astype(BF16)
    br = b_router.reshape(-1, 1)
    tab_p = _bias_table(rel_bias, 3 * ATTN_Q_TILE - 1)
    r_cache = k_cache.shape[1]
    tab_s = _bias_table(rel_bias, r_cache + ts - 1)

    zeros_pre = jnp.zeros((bp, SUBLANES, lw_c), F32)
    zeros_h = jnp.zeros((bp, 1, lw_c), F32)
    pre_s = jnp.pad(conv_state, ((0, 0), (SUBLANES - (CONV_WIDTH - 1), 0), (0, 0)))
    lru_w = (conv_w, cb2, wa_bd, wx_bd, b_a, b_x, lam2)
    qp, kp, vp, k32p, v32p, lru_p, tail_p, hl_p = _mixin(
        xp, tp[0], tp[1], ln_mix2, w_in_bf, qn_t, kn_t, head_mean, zeros_pre, zeros_h, *lru_w, 1, m)
    qs, ks, vs, k32s, v32s, lru_s, tail_s, hl_s = _mixin(
        xs, tsm[0], tsm[1], ln_mix2, w_in_bf, qn_t, kn_t, head_mean, pre_s, lru_state[:, None, :], *lru_w, bs, ts)
    attn_p = _attn_prompt(qp, kp, vp, tab_p)
    attn_s = _attn_step(qs, ks, vs, jnp.transpose(k_cache, (0, 2, 3, 1)), jnp.transpose(v_cache, (0, 2, 3, 1)),
                        tab_s, ATTN_STEP_BATCH)

    n_tiles = bp * (s // m) + 1
    x1p, h2, route, cnt = _outproj(attn_p, lru_p, xp, tp[2], tp[3], tp[4], ln_ffn2, w_out_bf, wr_t, br,
                                   1, m, n_tiles, 0, None)
    x1s, h2, route, cnt = _outproj(attn_s, lru_s, xs, tsm[2], tsm[3], tsm[4], ln_ffn2, w_out_bf, wr_t, br,
                                   bs, ts, n_tiles, n_tiles - 1, (h2, route, cnt))

    n_chunks, _, _, n_rows = _table_sizes(n_tiles)
    dispatch_tab, combine_tab, toff_b, row0, nblk = _route_tables(cnt[:, :, 0].astype(jnp.int32))
    xs_sorted = _dispatch(dispatch_tab, h2, route, toff_b, n_rows, n_chunks)
    ys_sorted = _experts(row0, nblk, xs_sorted, w_gu, b_gu, w_dn, b_dn)
    yp = _combine(combine_tab, ys_sorted, route, toff_b, x1p, tp[5], 1, m, 0, n_chunks)
    ysm = _combine(combine_tab, ys_sorted, route, toff_b, x1s, tsm[5], bs, ts, n_tiles - 1, n_chunks)

    keep = k32p.shape[1]
    new = (k32p.reshape(bp, keep, nh, HEAD_DIM), v32p.reshape(bp, keep, nh, HEAD_DIM),
           tail_p[:, SUBLANES - (CONV_WIDTH - 1):, :], hl_p[:, 0, :],
           k32s.reshape(bs, ts, nh, HEAD_DIM), v32s.reshape(bs, ts, nh, HEAD_DIM),
           tail_s[:, SUBLANES - (CONV_WIDTH - 1):, :], hl_s[:, 0, :])
    return yp, ysm, new


def kernel(x_prompt, x_sample, c_prompt, c_sample, cache_k, cache_v, state_conv, state_lru, ln_mix_w, ln_ffn_w, w_ada, b_ada, w_in, q_norm_w, k_norm_w, rel_bias, conv_w, conv_b, w_rgate, b_rgate, w_igate, b_igate, lru_lambda, w_out, w_router, b_router, w_gate_up, b_gate_up, w_down, b_down):
    depth = w_in.shape[0]
    yp, ys = x_prompt, x_sample
    c_all = jnp.concatenate([c_prompt, c_sample], axis=0)
    collected = [[] for _ in range(8)]
    for l in range(depth):
        mod = _ada(c_all, w_ada[l], b_ada[l])
        lw = (ln_mix_w[l], ln_ffn_w[l], w_in[l], q_norm_w[l], k_norm_w[l], rel_bias[l], conv_w[l], conv_b[l],
              w_rgate[l], b_rgate[l], w_igate[l], b_igate[l], lru_lambda[l], w_out[l], w_router[l], b_router[l],
              w_gate_up[l], b_gate_up[l], w_down[l], b_down[l])
        yp, ys, new = _layer(yp, ys, mod, cache_k[l], cache_v[l], state_conv[l], state_lru[l], lw)
        for acc, val in zip(collected, new):
            acc.append(val)
    return (yp, ys) + tuple(jnp.stack(vals) for vals in collected)
```

```python
import functools

import jax
import jax.numpy as jnp
from jax import lax
from jax.experimental import pallas as pl
from jax.experimental.pallas import tpu as pltpu

F32 = jnp.float32
BF16 = jnp.bfloat16

CHUNK = 64
N_LEFT_CHUNKS = 8
ATTN_WINDOW = N_LEFT_CHUNKS * CHUNK
HEAD_DIM = 64
REL_CLIP = 128
CONV_WIDTH = 4
LRU_C = 8.0
N_EXPERTS = 32
TOP_K = 4
SWIGLU_LIMIT = 7.0
SWIGLU_ALPHA = 1.702
NORM_EPS = 1e-6
NEG_INF = -1e30

LANES = 128
SUBLANES = 8
BF16_ROWS = 16
MXU_DIM = 256

TOKEN_TILE = 512
ATTN_Q_TILE = 256
ATTN_STEP_BATCH = 4
EXPERT_ROWS = 256
EXPERT_UNIT_BLOCKS = 4
BIAS_TABLE = 1024
VMEM_LIMIT = 56 * 1024 * 1024


def _params(sem, vmem=VMEM_LIMIT):
    return pltpu.CompilerParams(dimension_semantics=sem, vmem_limit_bytes=vmem)


def _ada_kernel(c_ref, w_ref, b_ref, o_ref):
    c = c_ref[...]
    s = (c * jax.nn.sigmoid(c)).astype(BF16)
    o_ref[...] = jnp.dot(s, w_ref[...].astype(BF16), preferred_element_type=F32) + b_ref[...]


def _ada(c_all, w_ada, b_ada):
    n, d = c_all.shape
    nout = w_ada.shape[1]
    tn = 1024
    return pl.pallas_call(
        _ada_kernel,
        grid=(nout // tn,),
        in_specs=[pl.BlockSpec((n, d), lambda j: (0, 0)),
                  pl.BlockSpec((d, tn), lambda j: (0, j)),
                  pl.BlockSpec((1, tn), lambda j: (0, j))],
        out_specs=pl.BlockSpec((n, tn), lambda j: (0, j)),
        out_shape=jax.ShapeDtypeStruct((n, nout), F32),
        compiler_params=_params(("arbitrary",)),
        name="ada",
    )(c_all, w_ada, b_ada.reshape(1, nout))


def _mixin_kernel(x_ref, sh_ref, sc_ref, ln_ref, win_ref, qn_ref, kn_ref, bd_ref,
                  pre_ref, h0_ref, cw_ref, cb_ref, wa_ref, wx_ref, ba_ref, bx_ref, lam_ref,
                  q_ref, k_ref, v_ref, k32_ref, v32_ref, lru_ref, tail_ref, hl_ref, cx_ref, ch_ref):
    nb, tr, d = x_ref.shape
    m = nb * tr
    aw = q_ref.shape[-1]

    @pl.when(pl.program_id(1) == 0)
    def _():
        cx_ref[...] = pre_ref[...]
        ch_ref[...] = h0_ref[...]

    x = x_ref[...]
    ms = jnp.mean(x * x, axis=-1, keepdims=True)
    h = x * lax.rsqrt(ms + NORM_EPS) * ln_ref[...]
    h = h * (1.0 + sc_ref[...]) + sh_ref[...]
    hb = h.reshape(m, d).astype(BF16)

    def proj(part):
        return jnp.dot(hb, win_ref[:, part * aw:(part + 1) * aw], preferred_element_type=F32)

    def head_norm(t, w_ref):
        msq = jnp.dot((t * t).astype(BF16), bd_ref[...], preferred_element_type=F32)
        return t * lax.rsqrt(msq + NORM_EPS) * w_ref[...]

    lru_out, new_tail, h_last = _lru_branch(
        proj(3).reshape(nb, tr, aw), proj(4).reshape(nb, tr, aw), cw_ref, cb_ref,
        wa_ref, wx_ref, ba_ref, bx_ref, lam_ref, cx_ref, ch_ref)
    lru_ref[...] = lru_out
    tail_ref[...] = new_tail
    hl_ref[...] = h_last
    q = head_norm(proj(0), qn_ref)
    k = head_norm(proj(1), kn_ref)
    v = proj(2)
    q_ref[...] = (q * (HEAD_DIM ** -0.5)).astype(BF16).reshape(nb, tr, aw)
    k_ref[...] = k.astype(BF16).reshape(nb, tr, aw)
    v_ref[...] = v.astype(BF16).reshape(nb, tr, aw)
    k32_ref[...] = k.reshape(nb, tr, aw)
    v32_ref[...] = v.reshape(nb, tr, aw)


def _mixin(x, sh, sc, ln_w, w_in_bf, qn_t, kn_t, bd, pre, h0, conv_w, conv_b, wa_bd, wx_bd, b_a, b_x, lam,
           nb, tr):
    nbt, t, d = x.shape
    aw = qn_t.shape[-1]
    c = pre.shape[-1]
    assert c == aw
    keep = min(ATTN_WINDOW, t)
    assert tr == keep or t == tr
    grid = (nbt // nb, t // tr)
    xmap = lambda b, i: (b, i, 0)
    mmap = lambda b, i: (b, 0, 0)
    cmap = lambda b, i: (0, 0)
    cmap3 = lambda b, i: (0, 0, 0)
    tmap = lambda b, i: (b, 0, 0)
    big = pl.BlockSpec((nb, tr, aw), xmap)
    tail = pl.BlockSpec((nb, keep, aw), tmap)
    row = pl.BlockSpec((1, c), cmap)
    return pl.pallas_call(
        _mixin_kernel,
        grid=grid,
        in_specs=[pl.BlockSpec((nb, tr, d), xmap),
                  pl.BlockSpec((nb, 1, d), mmap), pl.BlockSpec((nb, 1, d), mmap),
                  pl.BlockSpec((1, d), cmap),
                  pl.BlockSpec(w_in_bf.shape, cmap),
                  pl.BlockSpec((1, aw), cmap), pl.BlockSpec((1, aw), cmap),
                  pl.BlockSpec(bd.shape, cmap),
                  pl.BlockSpec((nb, SUBLANES, c), tmap), pl.BlockSpec((nb, 1, c), tmap),
                  pl.BlockSpec(conv_w.shape, cmap), row,
                  pl.BlockSpec(wa_bd.shape, cmap3), pl.BlockSpec(wx_bd.shape, cmap3),
                  row, row, row],
        out_specs=[big, big, big, tail, tail, big,
                   pl.BlockSpec((nb, SUBLANES, c), tmap), pl.BlockSpec((nb, 1, c), tmap)],
        out_shape=[jax.ShapeDtypeStruct((nbt, t, aw), BF16)] * 3
        + [jax.ShapeDtypeStruct((nbt, keep, aw), F32)] * 2
        + [jax.ShapeDtypeStruct((nbt, t, c), BF16),
           jax.ShapeDtypeStruct((nbt, SUBLANES, c), F32),
           jax.ShapeDtypeStruct((nbt, 1, c), F32)],
        scratch_shapes=[pltpu.VMEM((nb, SUBLANES, c), F32), pltpu.VMEM((nb, 1, c), F32)],
        compiler_params=_params(("arbitrary", "arbitrary")),
        name="mixin",
    )(x, sh, sc, ln_w, w_in_bf, qn_t, kn_t, bd, pre, h0, conv_w, conv_b, wa_bd, wx_bd, b_a, b_x, lam)


def _bias_table(rel_bias, off):
    h = rel_bias.shape[0]
    left = off - REL_CLIP
    right = BIAS_TABLE - left - (2 * REL_CLIP + 1)
    assert left >= 0 and right >= 0
    return jnp.concatenate([jnp.broadcast_to(rel_bias[:, :1], (h, left)), rel_bias,
                            jnp.broadcast_to(rel_bias[:, -1:], (h, right))], axis=1)


def _toeplitz(tab_row, rows, cols):
    t = jnp.broadcast_to(tab_row, (rows, BIAS_TABLE))
    t = pltpu.roll(t, BIAS_TABLE - (rows - 1), 1, stride=1, stride_axis=0)
    return t[:, :cols]


def _attn_kernel(q_ref, k0_ref, k1_ref, k2_ref, v0_ref, v1_ref, v2_ref, tab_ref, o_ref, bias_ref):
    b = pl.program_id(0)
    s = pl.program_id(1)
    qt = q_ref.shape[1]
    nk = 3 * qt
    nh = bias_ref.shape[0]

    @pl.when((b == 0) & (s == 0))
    def _():
        qi = lax.broadcasted_iota(jnp.int32, (qt, nk), 0) // CHUNK
        kc = lax.broadcasted_iota(jnp.int32, (qt, nk), 1) // CHUNK
        for h in range(nh):
            band = jnp.where(kc <= qi + N_LEFT_CHUNKS, _toeplitz(tab_ref[h:h + 1, :], qt, nk), NEG_INF)
            bias_ref[h] = jnp.where(kc >= qi, band, NEG_INF)

    q = q_ref[0]
    kcat = jnp.concatenate([k0_ref[0], k1_ref[0], k2_ref[0]], axis=0)
    vcat = jnp.concatenate([v0_ref[0], v1_ref[0], v2_ref[0]], axis=0)
    in_seq = lax.broadcasted_iota(jnp.int32, (qt, nk), 1) >= (2 - s) * qt
    outs = []
    for h in range(nh):
        sl = slice(h * HEAD_DIM, (h + 1) * HEAD_DIM)
        sc = lax.dot_general(q[:, sl], kcat[:, sl], (((1,), (1,)), ((), ())), preferred_element_type=F32)
        sc = jnp.where(in_seq, sc + bias_ref[h], NEG_INF)
        mx = jnp.max(sc, axis=-1, keepdims=True)
        p = jnp.exp(sc - mx)
        l = jnp.sum(p, axis=-1, keepdims=True)
        o = jnp.dot(p.astype(BF16), vcat[:, sl], preferred_element_type=F32)
        outs.append(o / l)
    o_ref[0] = jnp.concatenate(outs, axis=-1).astype(BF16)


def _attn_prompt(q, k, v, tab):
    b, s, aw = q.shape
    qt = ATTN_Q_TILE
    nh = aw // HEAD_DIM
    qspec = pl.BlockSpec((1, qt, aw), lambda i, j: (i, j, 0))

    def kspec(back):
        return pl.BlockSpec((1, qt, aw), lambda i, j: (i, jnp.maximum(j - back, 0), 0))

    return pl.pallas_call(
        _attn_kernel,
        grid=(b, s // qt),
        in_specs=[qspec, kspec(2), kspec(1), kspec(0), kspec(2), kspec(1), kspec(0),
                  pl.BlockSpec(tab.shape, lambda i, j: (0, 0))],
        out_specs=qspec,
        out_shape=jax.ShapeDtypeStruct((b, s, aw), BF16),
        scratch_shapes=[pltpu.VMEM((nh, qt, 3 * qt), F32)],
        compiler_params=_params(("arbitrary", "arbitrary")),
        name="attn_prompt",
    )(q, k, k, k, v, v, v, tab)


def _attn_step_kernel(q_ref, kn_ref, vn_ref, ck_ref, cv_ref, tab_ref, o_ref, bias_ref):
    step = pl.program_id(0)
    nbs, t, _ = q_ref.shape
    nh = bias_ref.shape[0]
    r = ck_ref.shape[-1]
    nk = r + LANES
    nt_dims = (((1,), (1,)), ((), ()))

    @pl.when(step == 0)
    def _():
        ok = lax.broadcasted_iota(jnp.int32, (t, nk), 1) < r + t
        for h in range(nh):
            bias_ref[h] = jnp.where(ok, _toeplitz(tab_ref[h:h + 1, :], t, nk), NEG_INF)

    pad = jnp.zeros((LANES - t, HEAD_DIM), BF16)
    for b in range(nbs):
        q = q_ref[b]
        kn = kn_ref[b]
        vn = vn_ref[b]
        outs = []
        for h in range(nh):
            sl = slice(h * HEAD_DIM, (h + 1) * HEAD_DIM)
            qh = q[:, sl]
            k_new = jnp.concatenate([kn[:, sl], pad], axis=0)
            v_new = jnp.concatenate([vn[:, sl], pad], axis=0)
            s_old = jnp.dot(qh, ck_ref[b, h].astype(BF16), preferred_element_type=F32) + bias_ref[h, :, :r]
            s_new = lax.dot_general(qh, k_new, nt_dims, preferred_element_type=F32) + bias_ref[h, :, r:]
            mx = jnp.maximum(jnp.max(s_old, axis=-1, keepdims=True), jnp.max(s_new, axis=-1, keepdims=True))
            p_old = jnp.exp(s_old - mx)
            p_new = jnp.exp(s_new - mx)
            l = jnp.sum(p_old, axis=-1, keepdims=True) + jnp.sum(p_new, axis=-1, keepdims=True)
            o = (lax.dot_general(p_old.astype(BF16), cv_ref[b, h].astype(BF16), nt_dims,
                                 preferred_element_type=F32)
                 + jnp.dot(p_new.astype(BF16), v_new, preferred_element_type=F32))
            outs.append(o / l)
        o_ref[b] = jnp.concatenate(outs, axis=-1).astype(BF16)


def _attn_step(q, kn, vn, ck, cv, tab, nbs):
    b, t, aw = q.shape
    nh = aw // HEAD_DIM
    r = ck.shape[-1]
    new = pl.BlockSpec((nbs, t, aw), lambda i: (i, 0, 0))
    old = pl.BlockSpec((nbs, nh, HEAD_DIM, r), lambda i: (i, 0, 0, 0))
    return pl.pallas_call(
        _attn_step_kernel,
        grid=(b // nbs,),
        in_specs=[new, new, new, old, old, pl.BlockSpec(tab.shape, lambda i: (0, 0))],
        out_specs=new,
        out_shape=jax.ShapeDtypeStruct((b, t, aw), BF16),
        scratch_shapes=[pltpu.VMEM((nh, t, r + LANES), F32)],
        compiler_params=_params(("arbitrary",)),
        name="attn_step",
    )(q, kn, vn, ck, cv, tab)


def _gelu_tanh(x):
    return x * (0.5 * (1.0 + jnp.tanh(0.7978845608028654 * (x + 0.044715 * (x * x * x)))))


def _lru_branch(x, yg, cw_ref, cb_ref, wa_ref, wx_ref, ba_ref, bx_ref, lam_ref, cx_ref, ch_ref):
    nb, tr, c = x.shape
    m = nb * tr
    half = c // 2
    xp = jnp.concatenate([cx_ref[...], x], axis=1)
    new_tail = xp[:, tr:tr + SUBLANES, :]
    xp2 = xp.reshape(nb * (tr + SUBLANES), c)
    y = cb_ref[...] + cw_ref[CONV_WIDTH - 1:CONV_WIDTH, :] * x
    for back in range(1, CONV_WIDTH):
        shifted = pltpu.roll(xp2, back, 0).reshape(nb, tr + SUBLANES, c)[:, SUBLANES:, :]
        y = y + cw_ref[CONV_WIDTH - 1 - back:CONV_WIDTH - back, :] * shifted
    y2 = y.reshape(m, c)
    yb = y2.astype(BF16)

    def gate(w_ref, b_ref):
        g = jnp.concatenate(
            [jnp.dot(yb[:, :half], w_ref[0], preferred_element_type=F32),
             jnp.dot(yb[:, half:], w_ref[1], preferred_element_type=F32)], axis=1)
        return jax.nn.sigmoid(g + b_ref[...])

    rg = gate(wa_ref, ba_ref)
    ig = gate(wx_ref, bx_ref)
    lam = lam_ref[...]
    log_sig = jnp.minimum(lam, 0.0) - jnp.log1p(jnp.exp(-jnp.abs(lam)))
    log_a = LRU_C * rg * log_sig
    a_cum = jnp.exp(log_a)
    b_cum = jnp.sqrt(-jnp.tanh(log_a) * (a_cum * a_cum + 1.0)) * (ig * y2)
    row = lax.broadcasted_iota(jnp.int32, (m, c), 0) % SUBLANES
    dist = 1
    while dist < SUBLANES:
        keep = row >= dist
        a_sh = jnp.where(keep, pltpu.roll(a_cum, dist, 0), 1.0)
        b_sh = jnp.where(keep, pltpu.roll(b_cum, dist, 0), 0.0)
        b_cum = a_cum * b_sh + b_cum
        a_cum = a_cum * a_sh
        dist *= 2
    groups = tr // SUBLANES
    a_grp = a_cum.reshape(nb, groups, SUBLANES, c)
    b_grp = b_cum.reshape(nb, groups, SUBLANES, c)
    carry = ch_ref[...]
    pieces = []
    for grp in range(groups):
        h_grp = a_grp[:, grp] * carry + b_grp[:, grp]
        carry = h_grp[:, SUBLANES - 1:SUBLANES, :]
        pieces.append(h_grp)
    h = jnp.concatenate(pieces, axis=1)
    ch_ref[...] = carry
    cx_ref[...] = new_tail
    return (h * _gelu_tanh(yg)).astype(BF16), new_tail, carry


def _outproj_kernel(*refs, aliased):
    (at_ref, lr_ref, x_ref, gm_ref, shf_ref, scf_ref, lnf_ref, wo_ref, wr_ref, br_ref) = refs[:10]
    x1_ref, h2_ref, route_ref, cnt_ref = refs[10 + aliased:]
    nb, tr, d = x_ref.shape
    m = nb * tr
    aw = at_ref.shape[-1]
    ne = wr_ref.shape[0]
    at = at_ref[...].reshape(m, aw)
    lr = lr_ref[...].reshape(m, aw)
    mix = (jnp.dot(at, wo_ref[0:aw, :], preferred_element_type=F32)
           + jnp.dot(lr, wo_ref[aw:2 * aw, :], preferred_element_type=F32))
    x1 = x_ref[...] + gm_ref[...] * mix.reshape(nb, tr, d)
    x1_ref[...] = x1
    ms = jnp.mean(x1 * x1, axis=-1, keepdims=True)
    h2 = x1 * lax.rsqrt(ms + NORM_EPS) * lnf_ref[...]
    h2 = (h2 * (1.0 + scf_ref[...]) + shf_ref[...]).reshape(m, d)
    h2_ref[...] = h2.astype(BF16)

    logits = lax.dot_general(wr_ref[...], h2.astype(BF16), (((1,), (1,)), ((), ())),
                             preferred_element_type=F32) + br_ref[...]
    e_iota = lax.broadcasted_iota(jnp.int32, (ne, m), 0).astype(F32)
    vals = logits
    top_v, sels = [], []
    for k in range(TOP_K):
        mx = jnp.max(vals, axis=0, keepdims=True)
        idx = jnp.min(jnp.where(vals == mx, e_iota, float(ne)), axis=0, keepdims=True)
        sel = e_iota == idx
        vals = jnp.where(sel, -jnp.inf, vals)
        top_v.append(mx)
        sels.append(sel)
        route_ref[0, k:k + 1, :] = idx
    ex = [jnp.exp(v - top_v[0]) for v in top_v]
    den = ex[0] + ex[1] + ex[2] + ex[3]
    chosen = jnp.zeros((ne, m), F32)
    for k in range(TOP_K):
        route_ref[0, 2 * TOP_K + k:2 * TOP_K + k + 1, :] = ex[k] / den
        chosen = chosen + jnp.where(sels[k], 1.0, 0.0)
    before = (lax.broadcasted_iota(jnp.int32, (m, m), 0) < lax.broadcasted_iota(jnp.int32, (m, m), 1))
    rank = jnp.dot(chosen.astype(BF16), jnp.where(before, 1.0, 0.0).astype(BF16), preferred_element_type=F32)
    for k in range(TOP_K):
        route_ref[0, TOP_K + k:TOP_K + k + 1, :] = jnp.sum(jnp.where(sels[k], rank, 0.0), axis=0, keepdims=True)
    route_ref[0, 3 * TOP_K:4 * TOP_K, :] = jnp.zeros((TOP_K, m), F32)
    cnt_ref[0] = jnp.broadcast_to(jnp.sum(chosen, axis=1, keepdims=True), (ne, LANES))


def _outproj(attn, lru_o, x, gm, shf, scf, lnf, w_out_bf, wr_t, br, nb, tr, n_tiles, tile0, prev):
    nbt, t, d = x.shape
    aw = attn.shape[-1]
    m = nb * tr
    assert m == TOKEN_TILE
    ne = wr_t.shape[0]
    tiles_per_seq = t // tr
    xmap = lambda b, i: (b, i, 0)
    mmap = lambda b, i: (b, 0, 0)
    c2 = lambda b, i: (0, 0)
    tile = lambda b, i: (tile0 + b * tiles_per_seq + i, 0)
    tile3 = lambda b, i: (tile0 + b * tiles_per_seq + i, 0, 0)
    mod = pl.BlockSpec((nb, 1, d), mmap)
    in_specs = [pl.BlockSpec((nb, tr, aw), xmap), pl.BlockSpec((nb, tr, aw), xmap),
                pl.BlockSpec((nb, tr, d), xmap), mod, mod, mod,
                pl.BlockSpec((1, d), c2), pl.BlockSpec(w_out_bf.shape, c2),
                pl.BlockSpec(wr_t.shape, c2), pl.BlockSpec((ne, 1), c2)]
    args = [attn, lru_o, x, gm, shf, scf, lnf, w_out_bf, wr_t, br]
    aliases = {}
    if prev is not None:
        in_specs += [pl.BlockSpec(memory_space=pl.ANY)] * 3
        args += list(prev)
        aliases = {10: 1, 11: 2, 12: 3}
    return pl.pallas_call(
        functools.partial(_outproj_kernel, aliased=len(aliases)),
        grid=(nbt // nb, tiles_per_seq),
        in_specs=in_specs,
        out_specs=[pl.BlockSpec((nb, tr, d), xmap), pl.BlockSpec((m, d), tile),
                   pl.BlockSpec((1, 4 * TOP_K, m), tile3), pl.BlockSpec((1, ne, LANES), tile3)],
        out_shape=[jax.ShapeDtypeStruct((nbt, t, d), F32),
                   jax.ShapeDtypeStruct((n_tiles * m, d), BF16),
                   jax.ShapeDtypeStruct((n_tiles, 4 * TOP_K, m), F32),
                   jax.ShapeDtypeStruct((n_tiles, ne, LANES), F32)],
        input_output_aliases=aliases,
        compiler_params=_params(("arbitrary", "arbitrary")),
        name="outproj",
    )(*args)


def _tile_rows(m):
    cap = TOP_K * m + N_EXPERTS * (BF16_ROWS - 1) + BF16_ROWS
    return -(-cap // TOKEN_TILE) * TOKEN_TILE


def _table_sizes(nt):
    g = BF16_ROWS
    m = TOKEN_TILE
    n_chunks = _tile_rows(m) // g
    n_gap = -(-(N_EXPERTS * (EXPERT_ROWS // g - 1)) // nt)
    bound = TOP_K * m * nt + nt * N_EXPERTS * (g - 1) + N_EXPERTS * (EXPERT_ROWS - g)
    n_sorted = -(-bound // EXPERT_ROWS) * EXPERT_ROWS
    return n_chunks, n_gap, n_sorted, n_sorted + 2 * (n_chunks + n_gap) * g


def _route_tables(cnt):
    nt = cnt.shape[0]
    g = BF16_ROWS
    bm = EXPERT_ROWS
    n_chunks, n_gap, n_sorted, _ = _table_sizes(nt)
    e_ids = jnp.arange(N_EXPERTS, dtype=jnp.int32)
    t_ids = jnp.arange(nt, dtype=jnp.int32)
    upto = (e_ids[:, None] <= e_ids[None, :]).astype(jnp.int32)
    pc = (cnt + g - 1) // g * g
    ctile = jnp.sum(pc[:, :, None] * upto[None], axis=1)
    toff = ctile - pc
    trow = ctile[:, -1]
    tot = jnp.sum(pc, axis=0)
    reg = (tot + bm - 1) // bm * bm
    creg = jnp.sum(reg[:, None] * upto, axis=0)
    base = creg - reg
    earlier = (t_ids[:, None] < t_ids[None, :]).astype(jnp.int32)
    goff = base[None, :] + jnp.sum(pc[:, None, :] * earlier[:, :, None], axis=0)
    r = jnp.arange(n_chunks, dtype=jnp.int32) * g
    r3 = r[None, :, None]
    in_seg = (toff[:, None, :] <= r3) & (r3 < ctile[:, None, :])
    dst = jnp.sum(jnp.where(in_seg, (goff - toff)[:, None, :], 0), axis=2) + r[None, :]
    dst = jnp.where(r[None, :] < trow[:, None], dst, -1)
    gcnt = (reg - tot) // g
    gcum = jnp.sum(gcnt[:, None] * upto, axis=0)
    gstart = gcum - gcnt
    s = jnp.arange(nt * n_gap, dtype=jnp.int32)
    in_gap = (gstart[None, :] <= s[:, None]) & (s[:, None] < gcum[None, :])
    gdst = jnp.sum(jnp.where(in_gap, (base + tot - g * gstart)[None, :] + g * s[:, None], 0), axis=1)
    gdst = jnp.where(s < gcum[-1], gdst, -1).reshape(nt, n_gap)
    table = jnp.concatenate([dst, gdst], axis=1).astype(jnp.int32)
    n_entries = n_chunks + n_gap
    spare = n_sorted + ((t_ids % 2)[:, None] * n_entries + jnp.arange(n_entries, dtype=jnp.int32)[None, :]) * g
    dispatch_tab = jnp.where(table >= 0, table, spare).astype(jnp.int32)
    combine_tab = jnp.maximum(dst, 0).astype(jnp.int32)
    toff_b = jnp.broadcast_to(toff.astype(F32)[:, :, None], (nt, N_EXPERTS, LANES))
    limit = (creg[-1] - EXPERT_UNIT_BLOCKS * bm).astype(jnp.int32).reshape(1)
    return dispatch_tab, combine_tab, toff_b, base.astype(jnp.int32), (reg // bm).astype(jnp.int32), limit


def _slot_rows(route_ref, toff_ref, m):
    ne = toff_ref.shape[1]
    e_iota = lax.broadcasted_iota(jnp.int32, (ne, m), 0).astype(F32)
    toff_col = toff_ref[0][:, 0:1]
    pos = []
    for k in range(TOP_K):
        sel = e_iota == route_ref[0, k:k + 1, :]
        start = jnp.sum(jnp.where(sel, toff_col, 0.0), axis=0, keepdims=True)
        pos.append(start + route_ref[0, TOP_K + k:TOP_K + k + 1, :])
    return pos


def _dispatch_kernel(tab_ref, h2_ref, route_ref, toff_ref, xs_hbm, buf_ref, sem, *, n_chunks, n_tiles):
    t = pl.program_id(0)
    slot = t % 2
    m = h2_ref.shape[0]
    rows = buf_ref.shape[1]
    n_entries = tab_ref.shape[1]
    g = BF16_ROWS
    per_chunk = m // g

    def start(c):
        src = c * g if c < n_chunks else rows - g
        pltpu.make_async_copy(
            buf_ref.at[slot, pl.ds(src, g)],
            xs_hbm.at[pl.ds(pl.multiple_of(tab_ref[t, c], g), g)], sem.at[slot]).start()

    def wait_all(which):
        for _ in range(n_entries):
            pltpu.make_async_copy(buf_ref.at[which, pl.ds(0, g)], xs_hbm.at[pl.ds(0, g)], sem.at[which]).wait()

    @pl.when(t >= 2)
    def _():
        wait_all(slot)

    pos = _slot_rows(route_ref, toff_ref, m)
    h2 = h2_ref[...]
    for rc in range(rows // m):
        r_iota = (lax.broadcasted_iota(jnp.int32, (m, m), 0) + rc * m).astype(F32)
        onehot = jnp.zeros((m, m), F32)
        for k in range(TOP_K):
            onehot = jnp.where(r_iota == pos[k], 1.0, onehot)
        buf_ref[slot, rc * m:(rc + 1) * m, :] = jnp.dot(
            onehot.astype(BF16), h2, preferred_element_type=F32).astype(BF16)
        for c in range((rc - 1) * per_chunk, rc * per_chunk) if rc > 0 else ():
            start(c)
    for c in range(n_chunks - per_chunk, n_entries):
        start(c)

    @pl.when(t == n_tiles - 1)
    def _():
        if n_tiles > 1:
            wait_all(1 - slot)
        wait_all(slot)


def _dispatch(table, h2, route, toff_b, n_rows, n_chunks):
    nt = route.shape[0]
    m = TOKEN_TILE
    d = h2.shape[1]
    rows = _tile_rows(m)
    grid_spec = pltpu.PrefetchScalarGridSpec(
        num_scalar_prefetch=1,
        grid=(nt,),
        in_specs=[pl.BlockSpec((m, d), lambda t, tab: (t, 0)),
                  pl.BlockSpec((1, 4 * TOP_K, m), lambda t, tab: (t, 0, 0)),
                  pl.BlockSpec((1, N_EXPERTS, LANES), lambda t, tab: (t, 0, 0))],
        out_specs=pl.BlockSpec(memory_space=pl.ANY),
        scratch_shapes=[pltpu.VMEM((2, rows, d), BF16), pltpu.SemaphoreType.DMA((2,))],
    )
    return pl.pallas_call(
        functools.partial(_dispatch_kernel, n_chunks=n_chunks, n_tiles=nt),
        grid_spec=grid_spec,
        out_shape=jax.ShapeDtypeStruct((n_rows, d), BF16),
        compiler_params=_params(("arbitrary",)),
        name="dispatch",
    )(table, h2, route, toff_b)


def _expert_kernel(row0_ref, nblk_ref, limit_ref, xs_hbm, wgu_ref, bgu_ref, wdn_ref, bdn_ref, ys_hbm,
                   wgu_bf, wdn_bf, xbuf, ybuf, sem_in, sem_out):
    e = pl.program_id(0)
    ne = pl.num_programs(0)
    bm = EXPERT_ROWS
    unit = xbuf.shape[1]
    per_unit = unit // bm
    dff = wdn_ref.shape[1]
    nblk = nblk_ref[e]
    n_units = (nblk + per_unit - 1) // per_unit

    def unit_start(expert, s):
        true = row0_ref[expert] + s * unit
        start = jnp.minimum(true, limit_ref[0])
        return pl.multiple_of(start, bm), pl.multiple_of(true - start, bm)

    def in_copy(expert, s, slot):
        start, _ = unit_start(expert, s)
        return pltpu.make_async_copy(xs_hbm.at[pl.ds(start, unit)], xbuf.at[slot], sem_in.at[slot])

    def out_copy(j, slot, i):
        start = pl.multiple_of(row0_ref[e] + j * bm, bm)
        return pltpu.make_async_copy(ybuf.at[slot, pl.ds(i * bm, bm)], ys_hbm.at[pl.ds(start, bm)],
                                     sem_out.at[slot])

    @pl.when((e == 0) & (nblk > 0))
    def _():
        in_copy(e, 0, 0).start()

    wgu_bf[...] = wgu_ref[0].astype(BF16)
    wdn_bf[...] = wdn_ref[0].astype(BF16)

    def run_unit(s, carry):
        slot = s % 2
        in_copy(e, s, slot).wait()

        @pl.when(s + 1 < n_units)
        def _():
            in_copy(e, s + 1, 1 - slot).start()

        @pl.when(s >= 2)
        def _():
            for i in range(per_unit):
                out_copy(0, slot, i).wait()

        _, lead = unit_start(e, s)
        for i in range(per_unit):
            @pl.when(s * per_unit + i < nblk)
            def _():
                x = xbuf[slot, pl.ds(pl.multiple_of(lead + i * bm, bm), bm), :]
                gu = jnp.dot(x, wgu_bf[...], preferred_element_type=F32) + bgu_ref[0]
                gate = jnp.minimum(gu[:, :dff], SWIGLU_LIMIT)
                up = jnp.clip(gu[:, dff:], -SWIGLU_LIMIT, SWIGLU_LIMIT)
                glu = gate * jax.nn.sigmoid(gate * SWIGLU_ALPHA)
                act = ((up + 1.0) * glu).astype(BF16)
                ybuf[slot, i * bm:(i + 1) * bm, :] = (
                    jnp.dot(act, wdn_bf[...], preferred_element_type=F32) + bdn_ref[0]).astype(BF16)
                out_copy(s * per_unit + i, slot, i).start()
        return carry

    lax.fori_loop(0, n_units, run_unit, 0)

    nxt = jnp.minimum(e + 1, ne - 1)

    @pl.when((e + 1 < ne) & (nblk_ref[nxt] > 0))
    def _():
        in_copy(nxt, 0, 0).start()

    @pl.when(n_units >= 2)
    def _():
        for i in range(per_unit):
            out_copy(0, n_units % 2, i).wait()

    in_last = nblk - (n_units - 1) * per_unit
    for i in range(per_unit):
        @pl.when((n_units >= 1) & (i < in_last))
        def _():
            out_copy(0, (n_units - 1) % 2, i).wait()


def _experts(row0, nblk, limit, xs, w_gu, b_gu, w_dn, b_dn):
    n_rows, d = xs.shape
    bm = EXPERT_ROWS
    unit = EXPERT_UNIT_BLOCKS * bm
    ne, _, dff2 = w_gu.shape
    dff = w_dn.shape[1]
    exp3 = lambda e, r0, nb, lim: (e, 0, 0)
    grid_spec = pltpu.PrefetchScalarGridSpec(
        num_scalar_prefetch=3,
        grid=(ne,),
        in_specs=[pl.BlockSpec(memory_space=pl.ANY),
                  pl.BlockSpec((1, d, dff2), exp3), pl.BlockSpec((1, 1, dff2), exp3),
                  pl.BlockSpec((1, dff, d), exp3), pl.BlockSpec((1, 1, d), exp3)],
        out_specs=pl.BlockSpec(memory_space=pl.ANY),
        scratch_shapes=[pltpu.VMEM((d, dff2), BF16), pltpu.VMEM((dff, d), BF16),
                        pltpu.VMEM((2, unit, d), BF16), pltpu.VMEM((2, unit, d), BF16),
                        pltpu.SemaphoreType.DMA((2,)), pltpu.SemaphoreType.DMA((2,))],
    )
    return pl.pallas_call(
        _expert_kernel,
        grid_spec=grid_spec,
        out_shape=jax.ShapeDtypeStruct((n_rows, d), BF16),
        compiler_params=_params(("arbitrary",)),
        name="experts",
    )(row0, nblk, limit, xs, w_gu, b_gu.reshape(ne, 1, dff2), w_dn, b_dn.reshape(ne, 1, d))


def _combine_kernel(tab_ref, ys_hbm, route_ref, toff_ref, x1_ref, gf_ref, o_ref, buf_ref, sem, *, n_chunks, tile0):
    step = pl.program_id(0)
    n_steps = pl.num_programs(0)
    t = tile0 + step
    slot = step % 2
    nb, tr, d = x1_ref.shape
    m = nb * tr
    rows = buf_ref.shape[1]
    g = BF16_ROWS

    def fetch(tile, which):
        for c in range(n_chunks):
            pltpu.make_async_copy(
                ys_hbm.at[pl.ds(pl.multiple_of(tab_ref[tile, c], g), g)],
                buf_ref.at[which, pl.ds(c * g, g)], sem.at[which]).start()

    @pl.when(step == 0)
    def _():
        fetch(t, slot)

    @pl.when(step + 1 < n_steps)
    def _():
        fetch(t + 1, 1 - slot)

    pos = _slot_rows(route_ref, toff_ref, m)
    gates = [route_ref[0, 2 * TOP_K + k:2 * TOP_K + k + 1, :] for k in range(TOP_K)]
    stacked = jnp.concatenate(pos + gates + [jnp.zeros((LANES - 2 * TOP_K, m), F32)], axis=0)
    cols = stacked.T
    for _ in range(n_chunks):
        pltpu.make_async_copy(ys_hbm.at[pl.ds(0, g)], buf_ref.at[slot, pl.ds(0, g)], sem.at[slot]).wait()

    acc = jnp.zeros((m, d), F32)
    for rc in range(rows // m):
        c_iota = (lax.broadcasted_iota(jnp.int32, (m, m), 1) + rc * m).astype(F32)
        weights = jnp.zeros((m, m), F32)
        for k in range(TOP_K):
            weights = jnp.where(c_iota == cols[:, k:k + 1], cols[:, TOP_K + k:TOP_K + k + 1], weights)
        acc = acc + jnp.dot(weights.astype(BF16), buf_ref[slot, rc * m:(rc + 1) * m, :],
                            preferred_element_type=F32)
    o_ref[...] = x1_ref[...] + gf_ref[...] * acc.reshape(nb, tr, d)


def _combine(table, ys, route, toff_b, x1, gf, nb, tr, tile0, n_chunks):
    nbt, t, d = x1.shape
    m = nb * tr
    assert m == TOKEN_TILE
    rows = _tile_rows(m)
    tiles_per_seq = t // tr
    n_steps = (nbt // nb) * tiles_per_seq
    xmap = lambda s, tab: (s // tiles_per_seq, s % tiles_per_seq, 0)
    grid_spec = pltpu.PrefetchScalarGridSpec(
        num_scalar_prefetch=1,
        grid=(n_steps,),
        in_specs=[pl.BlockSpec(memory_space=pl.ANY),
                  pl.BlockSpec((1, 4 * TOP_K, m), lambda s, tab: (tile0 + s, 0, 0)),
                  pl.BlockSpec((1, N_EXPERTS, LANES), lambda s, tab: (tile0 + s, 0, 0)),
                  pl.BlockSpec((nb, tr, d), xmap),
                  pl.BlockSpec((nb, 1, d), lambda s, tab: (s // tiles_per_seq, 0, 0))],
        out_specs=pl.BlockSpec((nb, tr, d), xmap),
        scratch_shapes=[pltpu.VMEM((2, rows, d), BF16), pltpu.SemaphoreType.DMA((2,))],
    )
    return pl.pallas_call(
        functools.partial(_combine_kernel, n_chunks=n_chunks, tile0=tile0),
        grid_spec=grid_spec,
        out_shape=jax.ShapeDtypeStruct((nbt, t, d), F32),
        compiler_params=_params(("arbitrary",)),
        name="combine",
    )(table, ys, route, toff_b, x1, gf)


def _block_diag(w, groups):
    n, k, _ = w.shape
    w = w.reshape(n // groups, groups, k, k)
    eye = jnp.eye(groups, dtype=w.dtype)
    return jnp.einsum("ngij,gh->ngihj", w, eye).reshape(n // groups, groups * k, groups * k)


def _layer(xp, xs, mod, k_cache, v_cache, conv_state, lru_state, lw):
    (ln_mix, ln_ffn, w_in, q_norm, k_norm, rel_bias, conv_w, conv_b, w_rg, b_rg, w_ig, b_ig, lam,
     w_out, w_router, b_router, w_gu, b_gu, w_dn, b_dn) = lw
    bp, s, d = xp.shape
    bs, ts, _ = xs.shape
    aw = w_out.shape[0] // 2
    nh = aw // HEAD_DIM
    m = TOKEN_TILE
    assert s % m == 0 and bs * ts == m and s % ATTN_Q_TILE == 0

    terms = [mod[:, i * d:(i + 1) * d][:, None, :] for i in range(6)]
    tp = [t[:bp] for t in terms]
    tsm = [t[bp:] for t in terms]

    w_in_bf = w_in.astype(BF16)
    w_out_bf = w_out.astype(BF16)
    qn_t = jnp.tile(q_norm, nh).reshape(1, aw)
    kn_t = jnp.tile(k_norm, nh).reshape(1, aw)
    head_mean = _block_diag(jnp.full((nh, HEAD_DIM, HEAD_DIM), 1.0 / HEAD_DIM, F32), nh)[0].astype(BF16)
    groups = MXU_DIM // w_rg.shape[-1]
    wa_bd = _block_diag(w_rg, groups).astype(BF16)
    wx_bd = _block_diag(w_ig, groups).astype(BF16)
    lw_c = b_rg.size
    b_a = b_rg.reshape(1, lw_c)
    b_x = b_ig.reshape(1, lw_c)
    lam2 = lam.reshape(1, lw_c)
    cb2 = conv_b.reshape(1, lw_c)
    ln_mix2 = ln_mix.reshape(1, d)
    ln_ffn2 = ln_ffn.reshape(1, d)
    wr_t = w_router.T.astype(BF16)
    br = b_router.reshape(-1, 1)
    tab_p = _bias_table(rel_bias, 3 * ATTN_Q_TILE - 1)
    r_cache = k_cache.shape[1]
    tab_s = _bias_table(rel_bias, r_cache + ts - 1)

    zeros_pre = jnp.zeros((bp, SUBLANES, lw_c), F32)
    zeros_h = jnp.zeros((bp, 1, lw_c), F32)
    pre_s = jnp.pad(conv_state, ((0, 0), (SUBLANES - (CONV_WIDTH - 1), 0), (0, 0)))
    lru_w = (conv_w, cb2, wa_bd, wx_bd, b_a, b_x, lam2)
    qp, kp, vp, k32p, v32p, lru_p, tail_p, hl_p = _mixin(
        xp, tp[0], tp[1], ln_mix2, w_in_bf, qn_t, kn_t, head_mean, zeros_pre, zeros_h, *lru_w, 1, m)
    qs, ks, vs, k32s, v32s, lru_s, tail_s, hl_s = _mixin(
        xs, tsm[0], tsm[1], ln_mix2, w_in_bf, qn_t, kn_t, head_mean, pre_s, lru_state[:, None, :], *lru_w, bs, ts)
    attn_p = _attn_prompt(qp, kp, vp, tab_p)
    attn_s = _attn_step(qs, ks, vs, jnp.transpose(k_cache, (0, 2, 3, 1)), jnp.transpose(v_cache, (0, 2, 3, 1)),
                        tab_s, ATTN_STEP_BATCH)

    n_tiles = bp * (s // m) + 1
    x1p, h2, route, cnt = _outproj(attn_p, lru_p, xp, tp[2], tp[3], tp[4], ln_ffn2, w_out_bf, wr_t, br,
                                   1, m, n_tiles, 0, None)
    x1s, h2, route, cnt = _outproj(attn_s, lru_s, xs, tsm[2], tsm[3], tsm[4], ln_ffn2, w_out_bf, wr_t, br,
                                   bs, ts, n_tiles, n_tiles - 1, (h2, route, cnt))

    n_chunks, _, _, n_rows = _table_sizes(n_tiles)
    assert TOP_K * m * n_tiles >= EXPERT_UNIT_BLOCKS * EXPERT_ROWS
    dispatch_tab, combine_tab, toff_b, row0, nblk, limit = _route_tables(cnt[:, :, 0].astype(jnp.int32))
    xs_sorted = _dispatch(dispatch_tab, h2, route, toff_b, n_rows, n_chunks)
    ys_sorted = _experts(row0, nblk, limit, xs_sorted, w_gu, b_gu, w_dn, b_dn)
    yp = _combine(combine_tab, ys_sorted, route, toff_b, x1p, tp[5], 1, m, 0, n_chunks)
    ysm = _combine(combine_tab, ys_sorted, route, toff_b, x1s, tsm[5], bs, ts, n_tiles - 1, n_chunks)

    keep = k32p.shape[1]
    new = (k32p.reshape(bp, keep, nh, HEAD_DIM), v32p.reshape(bp, keep, nh, HEAD_DIM),
           tail_p[:, SUBLANES - (CONV_WIDTH - 1):, :], hl_p[:, 0, :],
           k32s.reshape(bs, ts, nh, HEAD_DIM), v32s.reshape(bs, ts, nh, HEAD_DIM),
           tail_s[:, SUBLANES - (CONV_WIDTH - 1):, :], hl_s[:, 0, :])
    return yp, ysm, new


def kernel(x_prompt, x_sample, c_prompt, c_sample, cache_k, cache_v, state_conv, state_lru, ln_mix_w, ln_ffn_w, w_ada, b_ada, w_in, q_norm_w, k_norm_w, rel_bias, conv_w, conv_b, w_rgate, b_rgate, w_igate, b_igate, lru_lambda, w_out, w_router, b_router, w_gate_up, b_gate_up, w_down, b_down):
    depth = w_in.shape[0]
    yp, ys = x_prompt, x_sample
    c_all = jnp.concatenate([c_prompt, c_sample], axis=0)
    collected = [[] for _ in range(8)]
    for l in range(depth):
        mod = _ada(c_all, w_ada[l], b_ada[l])
        lw = (ln_mix_w[l], ln_ffn_w[l], w_in[l], q_norm_w[l], k_norm_w[l], rel_bias[l], conv_w[l], conv_b[l],
              w_rgate[l], b_rgate[l], w_igate[l], b_igate[l], lru_lambda[l], w_out[l], w_router[l], b_router[l],
              w_gate_up[l], b_gate_up[l], w_down[l], b_down[l])
        yp, ys, new = _layer(yp, ys, mod, cache_k[l], cache_v[l], state_conv[l], state_lru[l], lw)
        for acc, val in zip(collected, new):
            acc.append(val)
    return (yp, ys) + tuple(jnp.stack(vals) for vals in collected)
```

```python
import functools

import jax
import jax.numpy as jnp
from jax import lax
from jax.experimental import pallas as pl
from jax.experimental.pallas import tpu as pltpu

F32 = jnp.float32
BF16 = jnp.bfloat16

CHUNK = 64
N_LEFT_CHUNKS = 8
ATTN_WINDOW = N_LEFT_CHUNKS * CHUNK
HEAD_DIM = 64
REL_CLIP = 128
CONV_WIDTH = 4
LRU_C = 8.0
N_EXPERTS = 32
TOP_K = 4
SWIGLU_LIMIT = 7.0
SWIGLU_ALPHA = 1.702
NORM_EPS = 1e-6
NEG_INF = -1e30

LANES = 128
SUBLANES = 8
BF16_ROWS = 16
MXU_DIM = 256

TOKEN_TILE = 512
ATTN_Q_TILE = 256
ATTN_STEP_BATCH = 4
EXPERT_ROWS = 256
EXPERT_UNIT_BLOCKS = 4
BIAS_TABLE = 1024
VMEM_LIMIT = 56 * 1024 * 1024


def _params(sem, vmem=VMEM_LIMIT):
    return pltpu.CompilerParams(dimension_semantics=sem, vmem_limit_bytes=vmem)


def _ada_kernel(c_ref, w_ref, b_ref, o_ref):
    c = c_ref[...]
    s = (c * jax.nn.sigmoid(c)).astype(BF16)
    o_ref[...] = jnp.dot(s, w_ref[...].astype(BF16), preferred_element_type=F32) + b_ref[...]


def _ada(c_all, w_ada, b_ada):
    n, d = c_all.shape
    nout = w_ada.shape[1]
    tn = 1024
    return pl.pallas_call(
        _ada_kernel,
        grid=(nout // tn,),
        in_specs=[pl.BlockSpec((n, d), lambda j: (0, 0)),
                  pl.BlockSpec((d, tn), lambda j: (0, j)),
                  pl.BlockSpec((1, tn), lambda j: (0, j))],
        out_specs=pl.BlockSpec((n, tn), lambda j: (0, j)),
        out_shape=jax.ShapeDtypeStruct((n, nout), F32),
        compiler_params=_params(("arbitrary",)),
        name="ada",
    )(c_all, w_ada, b_ada.reshape(1, nout))


def _mixin_kernel(x_ref, sh_ref, sc_ref, ln_ref, win_ref, qn_ref, kn_ref, bd_ref,
                  pre_ref, h0_ref, cw_ref, cb_ref, wa_ref, wx_ref, ba_ref, bx_ref, lam_ref,
                  q_ref, k_ref, v_ref, k32_ref, v32_ref, lru_ref, tail_ref, hl_ref, cx_ref, ch_ref):
    nb, tr, d = x_ref.shape
    m = nb * tr
    aw = q_ref.shape[-1]

    @pl.when(pl.program_id(1) == 0)
    def _():
        cx_ref[...] = pre_ref[...]
        ch_ref[...] = h0_ref[...]

    x = x_ref[...]
    ms = jnp.mean(x * x, axis=-1, keepdims=True)
    h = x * lax.rsqrt(ms + NORM_EPS) * ln_ref[...]
    h = h * (1.0 + sc_ref[...]) + sh_ref[...]
    hb = h.reshape(m, d).astype(BF16)

    def proj(part):
        return jnp.dot(hb, win_ref[:, part * aw:(part + 1) * aw], preferred_element_type=F32)

    def head_norm(t, w_ref):
        msq = jnp.dot((t * t).astype(BF16), bd_ref[...], preferred_element_type=F32)
        return t * lax.rsqrt(msq + NORM_EPS) * w_ref[...]

    lru_out, new_tail, h_last = _lru_branch(
        proj(3).reshape(nb, tr, aw), proj(4).reshape(nb, tr, aw), cw_ref, cb_ref,
        wa_ref, wx_ref, ba_ref, bx_ref, lam_ref, cx_ref, ch_ref)
    lru_ref[...] = lru_out
    tail_ref[...] = new_tail
    hl_ref[...] = h_last
    q = head_norm(proj(0), qn_ref)
    k = head_norm(proj(1), kn_ref)
    v = proj(2)
    q_ref[...] = (q * (HEAD_DIM ** -0.5)).astype(BF16).reshape(nb, tr, aw)
    k_ref[...] = k.astype(BF16).reshape(nb, tr, aw)
    v_ref[...] = v.astype(BF16).reshape(nb, tr, aw)
    k32_ref[...] = k.reshape(nb, tr, aw)
    v32_ref[...] = v.reshape(nb, tr, aw)


def _mixin(x, sh, sc, ln_w, w_in_bf, qn_t, kn_t, bd, pre, h0, conv_w, conv_b, wa_bd, wx_bd, b_a, b_x, lam,
           nb, tr):
    nbt, t, d = x.shape
    aw = qn_t.shape[-1]
    c = pre.shape[-1]
    assert c == aw
    keep = min(ATTN_WINDOW, t)
    assert tr == keep or t == tr
    grid = (nbt // nb, t // tr)
    xmap = lambda b, i: (b, i, 0)
    mmap = lambda b, i: (b, 0, 0)
    cmap = lambda b, i: (0, 0)
    cmap3 = lambda b, i: (0, 0, 0)
    tmap = lambda b, i: (b, 0, 0)
    big = pl.BlockSpec((nb, tr, aw), xmap)
    tail = pl.BlockSpec((nb, keep, aw), tmap)
    row = pl.BlockSpec((1, c), cmap)
    return pl.pallas_call(
        _mixin_kernel,
        grid=grid,
        in_specs=[pl.BlockSpec((nb, tr, d), xmap),
                  pl.BlockSpec((nb, 1, d), mmap), pl.BlockSpec((nb, 1, d), mmap),
                  pl.BlockSpec((1, d), cmap),
                  pl.BlockSpec(w_in_bf.shape, cmap),
                  pl.BlockSpec((1, aw), cmap), pl.BlockSpec((1, aw), cmap),
                  pl.BlockSpec(bd.shape, cmap),
                  pl.BlockSpec((nb, SUBLANES, c), tmap), pl.BlockSpec((nb, 1, c), tmap),
                  pl.BlockSpec(conv_w.shape, cmap), row,
                  pl.BlockSpec(wa_bd.shape, cmap3), pl.BlockSpec(wx_bd.shape, cmap3),
                  row, row, row],
        out_specs=[big, big, big, tail, tail, big,
                   pl.BlockSpec((nb, SUBLANES, c), tmap), pl.BlockSpec((nb, 1, c), tmap)],
        out_shape=[jax.ShapeDtypeStruct((nbt, t, aw), BF16)] * 3
        + [jax.ShapeDtypeStruct((nbt, keep, aw), F32)] * 2
        + [jax.ShapeDtypeStruct((nbt, t, c), BF16),
           jax.ShapeDtypeStruct((nbt, SUBLANES, c), F32),
           jax.ShapeDtypeStruct((nbt, 1, c), F32)],
        scratch_shapes=[pltpu.VMEM((nb, SUBLANES, c), F32), pltpu.VMEM((nb, 1, c), F32)],
        compiler_params=_params(("arbitrary", "arbitrary")),
        name="mixin",
    )(x, sh, sc, ln_w, w_in_bf, qn_t, kn_t, bd, pre, h0, conv_w, conv_b, wa_bd, wx_bd, b_a, b_x, lam)


def _bias_table(rel_bias, off):
    h = rel_bias.shape[0]
    left = off - REL_CLIP
    right = BIAS_TABLE - left - (2 * REL_CLIP + 1)
    assert left >= 0 and right >= 0
    return jnp.concatenate([jnp.broadcast_to(rel_bias[:, :1], (h, left)), rel_bias,
                            jnp.broadcast_to(rel_bias[:, -1:], (h, right))], axis=1)


def _toeplitz(tab_row, rows, cols):
    t = jnp.broadcast_to(tab_row, (rows, BIAS_TABLE))
    t = pltpu.roll(t, BIAS_TABLE - (rows - 1), 1, stride=1, stride_axis=0)
    return t[:, :cols]


def _attn_kernel(q_ref, k0_ref, k1_ref, k2_ref, v0_ref, v1_ref, v2_ref, tab_ref, o_ref, bias_ref):
    b = pl.program_id(0)
    s = pl.program_id(1)
    qt = q_ref.shape[1]
    nk = 3 * qt
    nh = bias_ref.shape[0]

    @pl.when((b == 0) & (s == 0))
    def _():
        qi = lax.broadcasted_iota(jnp.int32, (qt, nk), 0) // CHUNK
        kc = lax.broadcasted_iota(jnp.int32, (qt, nk), 1) // CHUNK
        for h in range(nh):
            band = jnp.where(kc <= qi + N_LEFT_CHUNKS, _toeplitz(tab_ref[h:h + 1, :], qt, nk), NEG_INF)
            bias_ref[h] = jnp.where(kc >= qi, band, NEG_INF)

    pair_w = 2 * HEAD_DIM

    def attend(mask_start):
        q = q_ref[0]
        kcat = jnp.concatenate([k0_ref[0], k1_ref[0], k2_ref[0]], axis=0)
        vcat = jnp.concatenate([v0_ref[0], v1_ref[0], v2_ref[0]], axis=0)
        first = lax.broadcasted_iota(jnp.int32, (qt, pair_w), 1) < HEAD_DIM
        keep = [jnp.where(first, 1.0, 0.0).astype(BF16), jnp.where(first, 0.0, 1.0).astype(BF16)]
        if mask_start:
            in_seq = lax.broadcasted_iota(jnp.int32, (qt, nk), 1) >= (2 - s) * qt
        outs = []
        for pair in range(nh // 2):
            sl = slice(pair * pair_w, (pair + 1) * pair_w)
            q2, k2, v2 = q[:, sl], kcat[:, sl], vcat[:, sl]
            per_head = []
            for sub in range(2):
                sc = lax.dot_general(q2 * keep[sub], k2, (((1,), (1,)), ((), ())), preferred_element_type=F32)
                sc = sc + bias_ref[2 * pair + sub]
                if mask_start:
                    sc = jnp.where(in_seq, sc, NEG_INF)
                mx = jnp.max(sc, axis=-1, keepdims=True)
                p = jnp.exp(sc - mx)
                l = jnp.sum(p, axis=-1, keepdims=True)
                per_head.append(jnp.dot(p.astype(BF16), v2, preferred_element_type=F32) / l)
            outs.append(jnp.where(first, per_head[0], per_head[1]))
        o_ref[0] = jnp.concatenate(outs, axis=-1).astype(BF16)

    @pl.when(s < 2)
    def _():
        attend(True)

    @pl.when(s >= 2)
    def _():
        attend(False)


def _attn_prompt(q, k, v, tab):
    b, s, aw = q.shape
    qt = ATTN_Q_TILE
    nh = aw // HEAD_DIM
    qspec = pl.BlockSpec((1, qt, aw), lambda i, j: (i, j, 0))

    def kspec(back):
        return pl.BlockSpec((1, qt, aw), lambda i, j: (i, jnp.maximum(j - back, 0), 0))

    return pl.pallas_call(
        _attn_kernel,
        grid=(b, s // qt),
        in_specs=[qspec, kspec(2), kspec(1), kspec(0), kspec(2), kspec(1), kspec(0),
                  pl.BlockSpec(tab.shape, lambda i, j: (0, 0))],
        out_specs=qspec,
        out_shape=jax.ShapeDtypeStruct((b, s, aw), BF16),
        scratch_shapes=[pltpu.VMEM((nh, qt, 3 * qt), F32)],
        compiler_params=_params(("arbitrary", "arbitrary")),
        name="attn_prompt",
    )(q, k, k, k, v, v, v, tab)


def _attn_step_kernel(q_ref, kn_ref, vn_ref, ck_ref, cv_ref, tab_ref, o_ref, bias_ref):
    step = pl.program_id(0)
    nbs, t, _ = q_ref.shape
    nh = bias_ref.shape[0]
    r = ck_ref.shape[-1]
    nk = r + LANES
    nt_dims = (((1,), (1,)), ((), ()))

    @pl.when(step == 0)
    def _():
        ok = lax.broadcasted_iota(jnp.int32, (t, nk), 1) < r + t
        for h in range(nh):
            bias_ref[h] = jnp.where(ok, _toeplitz(tab_ref[h:h + 1, :], t, nk), NEG_INF)

    pad = jnp.zeros((LANES - t, HEAD_DIM), BF16)
    for b in range(nbs):
        q = q_ref[b]
        kn = kn_ref[b]
        vn = vn_ref[b]
        outs = []
        for h in range(nh):
            sl = slice(h * HEAD_DIM, (h + 1) * HEAD_DIM)
            qh = q[:, sl]
            k_new = jnp.concatenate([kn[:, sl], pad], axis=0)
            v_new = jnp.concatenate([vn[:, sl], pad], axis=0)
            s_old = jnp.dot(qh, ck_ref[b, h].astype(BF16), preferred_element_type=F32) + bias_ref[h, :, :r]
            s_new = lax.dot_general(qh, k_new, nt_dims, preferred_element_type=F32) + bias_ref[h, :, r:]
            mx = jnp.maximum(jnp.max(s_old, axis=-1, keepdims=True), jnp.max(s_new, axis=-1, keepdims=True))
            p_old = jnp.exp(s_old - mx)
            p_new = jnp.exp(s_new - mx)
            l = jnp.sum(p_old, axis=-1, keepdims=True) + jnp.sum(p_new, axis=-1, keepdims=True)
            o = (lax.dot_general(p_old.astype(BF16), cv_ref[b, h].astype(BF16), nt_dims,
                                 preferred_element_type=F32)
                 + jnp.dot(p_new.astype(BF16), v_new, preferred_element_type=F32))
            outs.append(o / l)
        o_ref[b] = jnp.concatenate(outs, axis=-1).astype(BF16)


def _attn_step(q, kn, vn, ck, cv, tab, nbs):
    b, t, aw = q.shape
    nh = aw // HEAD_DIM
    r = ck.shape[-1]
    new = pl.BlockSpec((nbs, t, aw), lambda i: (i, 0, 0))
    old = pl.BlockSpec((nbs, nh, HEAD_DIM, r), lambda i: (i, 0, 0, 0))
    return pl.pallas_call(
        _attn_step_kernel,
        grid=(b // nbs,),
        in_specs=[new, new, new, old, old, pl.BlockSpec(tab.shape, lambda i: (0, 0))],
        out_specs=new,
        out_shape=jax.ShapeDtypeStruct((b, t, aw), BF16),
        scratch_shapes=[pltpu.VMEM((nh, t, r + LANES), F32)],
        compiler_params=_params(("arbitrary",)),
        name="attn_step",
    )(q, kn, vn, ck, cv, tab)


def _gelu_tanh(x):
    return x * (0.5 * (1.0 + jnp.tanh(0.7978845608028654 * (x + 0.044715 * (x * x * x)))))


def _lru_branch(x, yg, cw_ref, cb_ref, wa_ref, wx_ref, ba_ref, bx_ref, lam_ref, cx_ref, ch_ref):
    nb, tr, c = x.shape
    m = nb * tr
    half = c // 2
    xp = jnp.concatenate([cx_ref[...], x], axis=1)
    new_tail = xp[:, tr:tr + SUBLANES, :]
    xp2 = xp.reshape(nb * (tr + SUBLANES), c)
    y = cb_ref[...] + cw_ref[CONV_WIDTH - 1:CONV_WIDTH, :] * x
    for back in range(1, CONV_WIDTH):
        shifted = pltpu.roll(xp2, back, 0).reshape(nb, tr + SUBLANES, c)[:, SUBLANES:, :]
        y = y + cw_ref[CONV_WIDTH - 1 - back:CONV_WIDTH - back, :] * shifted
    y2 = y.reshape(m, c)
    yb = y2.astype(BF16)

    def gate(w_ref, b_ref):
        g = jnp.concatenate(
            [jnp.dot(yb[:, :half], w_ref[0], preferred_element_type=F32),
             jnp.dot(yb[:, half:], w_ref[1], preferred_element_type=F32)], axis=1)
        return jax.nn.sigmoid(g + b_ref[...])

    rg = gate(wa_ref, ba_ref)
    ig = gate(wx_ref, bx_ref)
    lam = lam_ref[...]
    log_sig = jnp.minimum(lam, 0.0) - jnp.log1p(jnp.exp(-jnp.abs(lam)))
    log_a = LRU_C * rg * log_sig
    a_cum = jnp.exp(log_a)
    b_cum = jnp.sqrt(-jnp.tanh(log_a) * (a_cum * a_cum + 1.0)) * (ig * y2)
    row = lax.broadcasted_iota(jnp.int32, (m, c), 0) % SUBLANES
    dist = 1
    while dist < SUBLANES:
        keep = row >= dist
        a_sh = jnp.where(keep, pltpu.roll(a_cum, dist, 0), 1.0)
        b_sh = jnp.where(keep, pltpu.roll(b_cum, dist, 0), 0.0)
        b_cum = a_cum * b_sh + b_cum
        a_cum = a_cum * a_sh
        dist *= 2
    groups = tr // SUBLANES
    a_grp = a_cum.reshape(nb, groups, SUBLANES, c)
    b_grp = b_cum.reshape(nb, groups, SUBLANES, c)
    carry = ch_ref[...]
    pieces = []
    for grp in range(groups):
        h_grp = a_grp[:, grp] * carry + b_grp[:, grp]
        carry = h_grp[:, SUBLANES - 1:SUBLANES, :]
        pieces.append(h_grp)
    h = jnp.concatenate(pieces, axis=1)
    ch_ref[...] = carry
    cx_ref[...] = new_tail
    return (h * _gelu_tanh(yg)).astype(BF16), new_tail, carry


def _outproj_kernel(*refs, aliased):
    (at_ref, lr_ref, x_ref, gm_ref, shf_ref, scf_ref, lnf_ref, wo_ref, wr_ref, br_ref) = refs[:10]
    x1_ref, h2_ref, route_ref, cnt_ref = refs[10 + aliased:]
    nb, tr, d = x_ref.shape
    m = nb * tr
    aw = at_ref.shape[-1]
    ne = wr_ref.shape[0]
    at = at_ref[...].reshape(m, aw)
    lr = lr_ref[...].reshape(m, aw)
    mix = (jnp.dot(at, wo_ref[0:aw, :], preferred_element_type=F32)
           + jnp.dot(lr, wo_ref[aw:2 * aw, :], preferred_element_type=F32))
    x1 = x_ref[...] + gm_ref[...] * mix.reshape(nb, tr, d)
    x1_ref[...] = x1
    ms = jnp.mean(x1 * x1, axis=-1, keepdims=True)
    h2 = x1 * lax.rsqrt(ms + NORM_EPS) * lnf_ref[...]
    h2 = (h2 * (1.0 + scf_ref[...]) + shf_ref[...]).reshape(m, d)
    h2_ref[...] = h2.astype(BF16)

    logits = lax.dot_general(wr_ref[...], h2.astype(BF16), (((1,), (1,)), ((), ())),
                             preferred_element_type=F32) + br_ref[...]
    e_iota = lax.broadcasted_iota(jnp.int32, (ne, m), 0).astype(F32)
    vals = logits
    top_v, sels = [], []
    for k in range(TOP_K):
        mx = jnp.max(vals, axis=0, keepdims=True)
        idx = jnp.min(jnp.where(vals == mx, e_iota, float(ne)), axis=0, keepdims=True)
        sel = e_iota == idx
        vals = jnp.where(sel, -jnp.inf, vals)
        top_v.append(mx)
        sels.append(sel)
        route_ref[0, k:k + 1, :] = idx
    ex = [jnp.exp(v - top_v[0]) for v in top_v]
    den = ex[0] + ex[1] + ex[2] + ex[3]
    chosen = jnp.zeros((ne, m), F32)
    for k in range(TOP_K):
        route_ref[0, 2 * TOP_K + k:2 * TOP_K + k + 1, :] = ex[k] / den
        chosen = chosen + jnp.where(sels[k], 1.0, 0.0)
    before = (lax.broadcasted_iota(jnp.int32, (m, m), 0) < lax.broadcasted_iota(jnp.int32, (m, m), 1))
    rank = jnp.dot(chosen.astype(BF16), jnp.where(before, 1.0, 0.0).astype(BF16), preferred_element_type=F32)
    for k in range(TOP_K):
        route_ref[0, TOP_K + k:TOP_K + k + 1, :] = jnp.sum(jnp.where(sels[k], rank, 0.0), axis=0, keepdims=True)
    route_ref[0, 3 * TOP_K:4 * TOP_K, :] = jnp.zeros((TOP_K, m), F32)
    cnt_ref[0] = jnp.broadcast_to(jnp.sum(chosen, axis=1, keepdims=True), (ne, LANES))


def _outproj(attn, lru_o, x, gm, shf, scf, lnf, w_out_bf, wr_t, br, nb, tr, n_tiles, tile0, prev):
    nbt, t, d = x.shape
    aw = attn.shape[-1]
    m = nb * tr
    assert m == TOKEN_TILE
    ne = wr_t.shape[0]
    tiles_per_seq = t // tr
    xmap = lambda b, i: (b, i, 0)
    mmap = lambda b, i: (b, 0, 0)
    c2 = lambda b, i: (0, 0)
    tile = lambda b, i: (tile0 + b * tiles_per_seq + i, 0)
    tile3 = lambda b, i: (tile0 + b * tiles_per_seq + i, 0, 0)
    mod = pl.BlockSpec((nb, 1, d), mmap)
    in_specs = [pl.BlockSpec((nb, tr, aw), xmap), pl.BlockSpec((nb, tr, aw), xmap),
                pl.BlockSpec((nb, tr, d), xmap), mod, mod, mod,
                pl.BlockSpec((1, d), c2), pl.BlockSpec(w_out_bf.shape, c2),
                pl.BlockSpec(wr_t.shape, c2), pl.BlockSpec((ne, 1), c2)]
    args = [attn, lru_o, x, gm, shf, scf, lnf, w_out_bf, wr_t, br]
    aliases = {}
    if prev is not None:
        in_specs += [pl.BlockSpec(memory_space=pl.ANY)] * 3
        args += list(prev)
        aliases = {10: 1, 11: 2, 12: 3}
    return pl.pallas_call(
        functools.partial(_outproj_kernel, aliased=len(aliases)),
        grid=(nbt // nb, tiles_per_seq),
        in_specs=in_specs,
        out_specs=[pl.BlockSpec((nb, tr, d), xmap), pl.BlockSpec((m, d), tile),
                   pl.BlockSpec((1, 4 * TOP_K, m), tile3), pl.BlockSpec((1, ne, LANES), tile3)],
        out_shape=[jax.ShapeDtypeStruct((nbt, t, d), F32),
                   jax.ShapeDtypeStruct((n_tiles * m, d), BF16),
                   jax.ShapeDtypeStruct((n_tiles, 4 * TOP_K, m), F32),
                   jax.ShapeDtypeStruct((n_tiles, ne, LANES), F32)],
        input_output_aliases=aliases,
        compiler_params=_params(("arbitrary", "arbitrary")),
        name="outproj",
    )(*args)


def _tile_rows(m):
    cap = TOP_K * m + N_EXPERTS * (BF16_ROWS - 1) + BF16_ROWS
    return -(-cap // TOKEN_TILE) * TOKEN_TILE


def _table_sizes(nt):
    g = BF16_ROWS
    m = TOKEN_TILE
    n_chunks = _tile_rows(m) // g
    n_gap = -(-(N_EXPERTS * (EXPERT_ROWS // g - 1)) // nt)
    bound = TOP_K * m * nt + nt * N_EXPERTS * (g - 1) + N_EXPERTS * (EXPERT_ROWS - g)
    n_sorted = -(-bound // EXPERT_ROWS) * EXPERT_ROWS
    return n_chunks, n_gap, n_sorted, n_sorted + 2 * (n_chunks + n_gap) * g


def _route_tables(cnt):
    nt = cnt.shape[0]
    g = BF16_ROWS
    bm = EXPERT_ROWS
    n_chunks, n_gap, n_sorted, _ = _table_sizes(nt)
    e_ids = jnp.arange(N_EXPERTS, dtype=jnp.int32)
    t_ids = jnp.arange(nt, dtype=jnp.int32)
    upto = (e_ids[:, None] <= e_ids[None, :]).astype(jnp.int32)
    pc = (cnt + g - 1) // g * g
    ctile = jnp.sum(pc[:, :, None] * upto[None], axis=1)
    toff = ctile - pc
    trow = ctile[:, -1]
    tot = jnp.sum(pc, axis=0)
    reg = (tot + bm - 1) // bm * bm
    creg = jnp.sum(reg[:, None] * upto, axis=0)
    base = creg - reg
    earlier = (t_ids[:, None] < t_ids[None, :]).astype(jnp.int32)
    goff = base[None, :] + jnp.sum(pc[:, None, :] * earlier[:, :, None], axis=0)
    r = jnp.arange(n_chunks, dtype=jnp.int32) * g
    r3 = r[None, :, None]
    in_seg = (toff[:, None, :] <= r3) & (r3 < ctile[:, None, :])
    dst = jnp.sum(jnp.where(in_seg, (goff - toff)[:, None, :], 0), axis=2) + r[None, :]
    dst = jnp.where(r[None, :] < trow[:, None], dst, -1)
    gcnt = (reg - tot) // g
    gcum = jnp.sum(gcnt[:, None] * upto, axis=0)
    gstart = gcum - gcnt
    s = jnp.arange(nt * n_gap, dtype=jnp.int32)
    in_gap = (gstart[None, :] <= s[:, None]) & (s[:, None] < gcum[None, :])
    gdst = jnp.sum(jnp.where(in_gap, (base + tot - g * gstart)[None, :] + g * s[:, None], 0), axis=1)
    gdst = jnp.where(s < gcum[-1], gdst, -1).reshape(nt, n_gap)
    table = jnp.concatenate([dst, gdst], axis=1).astype(jnp.int32)
    n_entries = n_chunks + n_gap
    spare = n_sorted + ((t_ids % 2)[:, None] * n_entries + jnp.arange(n_entries, dtype=jnp.int32)[None, :]) * g
    dispatch_tab = jnp.where(table >= 0, table, spare).astype(jnp.int32)
    combine_tab = jnp.maximum(dst, 0).astype(jnp.int32)
    toff_b = jnp.broadcast_to(toff.astype(F32)[:, :, None], (nt, N_EXPERTS, LANES))
    limit = (creg[-1] - EXPERT_UNIT_BLOCKS * bm).astype(jnp.int32).reshape(1)
    return dispatch_tab, combine_tab, toff_b, base.astype(jnp.int32), (reg // bm).astype(jnp.int32), limit


def _slot_rows(route_ref, toff_ref, m):
    ne = toff_ref.shape[1]
    e_iota = lax.broadcasted_iota(jnp.int32, (ne, m), 0).astype(F32)
    toff_col = toff_ref[0][:, 0:1]
    pos = []
    for k in range(TOP_K):
        sel = e_iota == route_ref[0, k:k + 1, :]
        start = jnp.sum(jnp.where(sel, toff_col, 0.0), axis=0, keepdims=True)
        pos.append(start + route_ref[0, TOP_K + k:TOP_K + k + 1, :])
    return pos


def _dispatch_kernel(tab_ref, h2_ref, route_ref, toff_ref, xs_hbm, buf_ref, sem, *, n_chunks, n_tiles):
    t = pl.program_id(0)
    slot = t % 2
    m = h2_ref.shape[0]
    rows = buf_ref.shape[1]
    n_entries = tab_ref.shape[1]
    g = BF16_ROWS
    per_chunk = m // g

    def start(c):
        src = c * g if c < n_chunks else rows - g
        pltpu.make_async_copy(
            buf_ref.at[slot, pl.ds(src, g)],
            xs_hbm.at[pl.ds(pl.multiple_of(tab_ref[t, c], g), g)], sem.at[slot]).start()

    def wait_all(which):
        for _ in range(n_entries):
            pltpu.make_async_copy(buf_ref.at[which, pl.ds(0, g)], xs_hbm.at[pl.ds(0, g)], sem.at[which]).wait()

    @pl.when(t >= 2)
    def _():
        wait_all(slot)

    pos = _slot_rows(route_ref, toff_ref, m)
    h2 = h2_ref[...]
    for rc in range(rows // m):
        r_iota = (lax.broadcasted_iota(jnp.int32, (m, m), 0) + rc * m).astype(F32)
        onehot = jnp.zeros((m, m), F32)
        for k in range(TOP_K):
            onehot = jnp.where(r_iota == pos[k], 1.0, onehot)
        buf_ref[slot, rc * m:(rc + 1) * m, :] = jnp.dot(
            onehot.astype(BF16), h2, preferred_element_type=F32).astype(BF16)
        for c in range((rc - 1) * per_chunk, rc * per_chunk) if rc > 0 else ():
            start(c)
    for c in range(n_chunks - per_chunk, n_entries):
        start(c)

    @pl.when(t == n_tiles - 1)
    def _():
        if n_tiles > 1:
            wait_all(1 - slot)
        wait_all(slot)


def _dispatch(table, h2, route, toff_b, n_rows, n_chunks):
    nt = route.shape[0]
    m = TOKEN_TILE
    d = h2.shape[1]
    rows = _tile_rows(m)
    grid_spec = pltpu.PrefetchScalarGridSpec(
        num_scalar_prefetch=1,
        grid=(nt,),
        in_specs=[pl.BlockSpec((m, d), lambda t, tab: (t, 0)),
                  pl.BlockSpec((1, 4 * TOP_K, m), lambda t, tab: (t, 0, 0)),
                  pl.BlockSpec((1, N_EXPERTS, LANES), lambda t, tab: (t, 0, 0))],
        out_specs=pl.BlockSpec(memory_space=pl.ANY),
        scratch_shapes=[pltpu.VMEM((2, rows, d), BF16), pltpu.SemaphoreType.DMA((2,))],
    )
    return pl.pallas_call(
        functools.partial(_dispatch_kernel, n_chunks=n_chunks, n_tiles=nt),
        grid_spec=grid_spec,
        out_shape=jax.ShapeDtypeStruct((n_rows, d), BF16),
        compiler_params=_params(("arbitrary",)),
        name="dispatch",
    )(table, h2, route, toff_b)


def _expert_kernel(row0_ref, nblk_ref, limit_ref, xs_hbm, wgu_ref, bgu_ref, wdn_ref, bdn_ref, ys_hbm,
                   wgu_bf, wdn_bf, xbuf, ybuf, sem_in, sem_out):
    e = pl.program_id(0)
    ne = pl.num_programs(0)
    bm = EXPERT_ROWS
    unit = xbuf.shape[1]
    per_unit = unit // bm
    dff = wdn_ref.shape[1]
    nblk = nblk_ref[e]
    n_units = (nblk + per_unit - 1) // per_unit

    def unit_start(expert, s):
        true = row0_ref[expert] + s * unit
        start = jnp.minimum(true, limit_ref[0])
        return pl.multiple_of(start, bm), pl.multiple_of(true - start, bm)

    def in_copy(expert, s, slot):
        start, _ = unit_start(expert, s)
        return pltpu.make_async_copy(xs_hbm.at[pl.ds(start, unit)], xbuf.at[slot], sem_in.at[slot])

    def out_copy(j, slot, i):
        start = pl.multiple_of(row0_ref[e] + j * bm, bm)
        return pltpu.make_async_copy(ybuf.at[slot, pl.ds(i * bm, bm)], ys_hbm.at[pl.ds(start, bm)],
                                     sem_out.at[slot])

    @pl.when((e == 0) & (nblk > 0))
    def _():
        in_copy(e, 0, 0).start()

    wgu_bf[...] = wgu_ref[0].astype(BF16)
    wdn_bf[...] = wdn_ref[0].astype(BF16)

    def run_unit(s, carry):
        slot = s % 2
        in_copy(e, s, slot).wait()

        @pl.when(s + 1 < n_units)
        def _():
            in_copy(e, s + 1, 1 - slot).start()

        @pl.when(s >= 2)
        def _():
            for i in range(per_unit):
                out_copy(0, slot, i).wait()

        _, lead = unit_start(e, s)
        for i in range(per_unit):
            @pl.when(s * per_unit + i < nblk)
            def _():
                x = xbuf[slot, pl.ds(pl.multiple_of(lead + i * bm, bm), bm), :]
                gu = jnp.dot(x, wgu_bf[...], preferred_element_type=F32) + bgu_ref[0]
                gate = jnp.minimum(gu[:, :dff], SWIGLU_LIMIT)
                up = jnp.clip(gu[:, dff:], -SWIGLU_LIMIT, SWIGLU_LIMIT)
                glu = gate * jax.nn.sigmoid(gate * SWIGLU_ALPHA)
                act = ((up + 1.0) * glu).astype(BF16)
                ybuf[slot, i * bm:(i + 1) * bm, :] = (
                    jnp.dot(act, wdn_bf[...], preferred_element_type=F32) + bdn_ref[0]).astype(BF16)
                out_copy(s * per_unit + i, slot, i).start()
        return carry

    lax.fori_loop(0, n_units, run_unit, 0)

    nxt = jnp.minimum(e + 1, ne - 1)

    @pl.when((e + 1 < ne) & (nblk_ref[nxt] > 0))
    def _():
        in_copy(nxt, 0, 0).start()

    @pl.when(n_units >= 2)
    def _():
        for i in range(per_unit):
            out_copy(0, n_units % 2, i).wait()

    in_last = nblk - (n_units - 1) * per_unit
    for i in range(per_unit):
        @pl.when((n_units >= 1) & (i < in_last))
        def _():
            out_copy(0, (n_units - 1) % 2, i).wait()


def _experts(row0, nblk, limit, xs, w_gu, b_gu, w_dn, b_dn):
    n_rows, d = xs.shape
    bm = EXPERT_ROWS
    unit = EXPERT_UNIT_BLOCKS * bm
    ne, _, dff2 = w_gu.shape
    dff = w_dn.shape[1]
    exp3 = lambda e, r0, nb, lim: (e, 0, 0)
    grid_spec = pltpu.PrefetchScalarGridSpec(
        num_scalar_prefetch=3,
        grid=(ne,),
        in_specs=[pl.BlockSpec(memory_space=pl.ANY),
                  pl.BlockSpec((1, d, dff2), exp3), pl.BlockSpec((1, 1, dff2), exp3),
                  pl.BlockSpec((1, dff, d), exp3), pl.BlockSpec((1, 1, d), exp3)],
        out_specs=pl.BlockSpec(memory_space=pl.ANY),
        scratch_shapes=[pltpu.VMEM((d, dff2), BF16), pltpu.VMEM((dff, d), BF16),
                        pltpu.VMEM((2, unit, d), BF16), pltpu.VMEM((2, unit, d), BF16),
                        pltpu.SemaphoreType.DMA((2,)), pltpu.SemaphoreType.DMA((2,))],
    )
    return pl.pallas_call(
        _expert_kernel,
        grid_spec=grid_spec,
        out_shape=jax.ShapeDtypeStruct((n_rows, d), BF16),
        compiler_params=_params(("arbitrary",)),
        name="experts",
    )(row0, nblk, limit, xs, w_gu, b_gu.reshape(ne, 1, dff2), w_dn, b_dn.reshape(ne, 1, d))


def _combine_kernel(tab_ref, ys_hbm, route_ref, toff_ref, x1_ref, gf_ref, o_ref, buf_ref, sem, *, n_chunks, tile0):
    step = pl.program_id(0)
    n_steps = pl.num_programs(0)
    t = tile0 + step
    slot = step % 2
    nb, tr, d = x1_ref.shape
    m = nb * tr
    rows = buf_ref.shape[1]
    g = BF16_ROWS

    def fetch(tile, which):
        for c in range(n_chunks):
            pltpu.make_async_copy(
                ys_hbm.at[pl.ds(pl.multiple_of(tab_ref[tile, c], g), g)],
                buf_ref.at[which, pl.ds(c * g, g)], sem.at[which]).start()

    @pl.when(step == 0)
    def _():
        fetch(t, slot)

    @pl.when(step + 1 < n_steps)
    def _():
        fetch(t + 1, 1 - slot)

    pos = _slot_rows(route_ref, toff_ref, m)
    gates = [route_ref[0, 2 * TOP_K + k:2 * TOP_K + k + 1, :] for k in range(TOP_K)]
    stacked = jnp.concatenate(pos + gates + [jnp.zeros((LANES - 2 * TOP_K, m), F32)], axis=0)
    cols = stacked.T
    for _ in range(n_chunks):
        pltpu.make_async_copy(ys_hbm.at[pl.ds(0, g)], buf_ref.at[slot, pl.ds(0, g)], sem.at[slot]).wait()

    acc = jnp.zeros((m, d), F32)
    for rc in range(rows // m):
        c_iota = (lax.broadcasted_iota(jnp.int32, (m, m), 1) + rc * m).astype(F32)
        weights = jnp.zeros((m, m), F32)
        for k in range(TOP_K):
            weights = jnp.where(c_iota == cols[:, k:k + 1], cols[:, TOP_K + k:TOP_K + k + 1], weights)
        acc = acc + jnp.dot(weights.astype(BF16), buf_ref[slot, rc * m:(rc + 1) * m, :],
                            preferred_element_type=F32)
    o_ref[...] = x1_ref[...] + gf_ref[...] * acc.reshape(nb, tr, d)


def _combine(table, ys, route, toff_b, x1, gf, nb, tr, tile0, n_chunks):
    nbt, t, d = x1.shape
    m = nb * tr
    assert m == TOKEN_TILE
    rows = _tile_rows(m)
    tiles_per_seq = t // tr
    n_steps = (nbt // nb) * tiles_per_seq
    xmap = lambda s, tab: (s // tiles_per_seq, s % tiles_per_seq, 0)
    grid_spec = pltpu.PrefetchScalarGridSpec(
        num_scalar_prefetch=1,
        grid=(n_steps,),
        in_specs=[pl.BlockSpec(memory_space=pl.ANY),
                  pl.BlockSpec((1, 4 * TOP_K, m), lambda s, tab: (tile0 + s, 0, 0)),
                  pl.BlockSpec((1, N_EXPERTS, LANES), lambda s, tab: (tile0 + s, 0, 0)),
                  pl.BlockSpec((nb, tr, d), xmap),
                  pl.BlockSpec((nb, 1, d), lambda s, tab: (s // tiles_per_seq, 0, 0))],
        out_specs=pl.BlockSpec((nb, tr, d), xmap),
        scratch_shapes=[pltpu.VMEM((2, rows, d), BF16), pltpu.SemaphoreType.DMA((2,))],
    )
    return pl.pallas_call(
        functools.partial(_combine_kernel, n_chunks=n_chunks, tile0=tile0),
        grid_spec=grid_spec,
        out_shape=jax.ShapeDtypeStruct((nbt, t, d), F32),
        compiler_params=_params(("arbitrary",)),
        name="combine",
    )(table, ys, route, toff_b, x1, gf)


def _block_diag(w, groups):
    n, k, _ = w.shape
    w = w.reshape(n // groups, groups, k, k)
    eye = jnp.eye(groups, dtype=w.dtype)
    return jnp.einsum("ngij,gh->ngihj", w, eye).reshape(n // groups, groups * k, groups * k)


def _layer(xp, xs, mod, k_cache, v_cache, conv_state, lru_state, lw):
    (ln_mix, ln_ffn, w_in, q_norm, k_norm, rel_bias, conv_w, conv_b, w_rg, b_rg, w_ig, b_ig, lam,
     w_out, w_router, b_router, w_gu, b_gu, w_dn, b_dn) = lw
    bp, s, d = xp.shape
    bs, ts, _ = xs.shape
    aw = w_out.shape[0] // 2
    nh = aw // HEAD_DIM
    m = TOKEN_TILE
    assert s % m == 0 and bs * ts == m and s % ATTN_Q_TILE == 0

    terms = [mod[:, i * d:(i + 1) * d][:, None, :] for i in range(6)]
    tp = [t[:bp] for t in terms]
    tsm = [t[bp:] for t in terms]

    w_in_bf = w_in.astype(BF16)
    w_out_bf = w_out.astype(BF16)
    qn_t = jnp.tile(q_norm, nh).reshape(1, aw)
    kn_t = jnp.tile(k_norm, nh).reshape(1, aw)
    head_mean = _block_diag(jnp.full((nh, HEAD_DIM, HEAD_DIM), 1.0 / HEAD_DIM, F32), nh)[0].astype(BF16)
    groups = MXU_DIM // w_rg.shape[-1]
    wa_bd = _block_diag(w_rg, groups).astype(BF16)
    wx_bd = _block_diag(w_ig, groups).astype(BF16)
    lw_c = b_rg.size
    b_a = b_rg.reshape(1, lw_c)
    b_x = b_ig.reshape(1, lw_c)
    lam2 = lam.reshape(1, lw_c)
    cb2 = conv_b.reshape(1, lw_c)
    ln_mix2 = ln_mix.reshape(1, d)
    ln_ffn2 = ln_ffn.reshape(1, d)
    wr_t = w_router.T.astype(BF16)
    br = b_router.reshape(-1, 1)
    tab_p = _bias_table(rel_bias, 3 * ATTN_Q_TILE - 1)
    r_cache = k_cache.shape[1]
    tab_s = _bias_table(rel_bias, r_cache + ts - 1)

    zeros_pre = jnp.zeros((bp, SUBLANES, lw_c), F32)
    zeros_h = jnp.zeros((bp, 1, lw_c), F32)
    pre_s = jnp.pad(conv_state, ((0, 0), (SUBLANES - (CONV_WIDTH - 1), 0), (0, 0)))
    lru_w = (conv_w, cb2, wa_bd, wx_bd, b_a, b_x, lam2)
    qp, kp, vp, k32p, v32p, lru_p, tail_p, hl_p = _mixin(
        xp, tp[0], tp[1], ln_mix2, w_in_bf, qn_t, kn_t, head_mean, zeros_pre, zeros_h, *lru_w, 1, m)
    qs, ks, vs, k32s, v32s, lru_s, tail_s, hl_s = _mixin(
        xs, tsm[0], tsm[1], ln_mix2, w_in_bf, qn_t, kn_t, head_mean, pre_s, lru_state[:, None, :], *lru_w, bs, ts)
    attn_p = _attn_prompt(qp, kp, vp, tab_p)
    attn_s = _attn_step(qs, ks, vs, jnp.transpose(k_cache, (0, 2, 3, 1)), jnp.transpose(v_cache, (0, 2, 3, 1)),
                        tab_s, ATTN_STEP_BATCH)

    n_tiles = bp * (s // m) + 1
    x1p, h2, route, cnt = _outproj(attn_p, lru_p, xp, tp[2], tp[3], tp[4], ln_ffn2, w_out_bf, wr_t, br,
                                   1, m, n_tiles, 0, None)
    x1s, h2, route, cnt = _outproj(attn_s, lru_s, xs, tsm[2], tsm[3], tsm[4], ln_ffn2, w_out_bf, wr_t, br,
                                   bs, ts, n_tiles, n_tiles - 1, (h2, route, cnt))

    n_chunks, _, _, n_rows = _table_sizes(n_tiles)
    assert TOP_K * m * n_tiles >= EXPERT_UNIT_BLOCKS * EXPERT_ROWS
    dispatch_tab, combine_tab, toff_b, row0, nblk, limit = _route_tables(cnt[:, :, 0].astype(jnp.int32))
    xs_sorted = _dispatch(dispatch_tab, h2, route, toff_b, n_rows, n_chunks)
    ys_sorted = _experts(row0, nblk, limit, xs_sorted, w_gu, b_gu, w_dn, b_dn)
    yp = _combine(combine_tab, ys_sorted, route, toff_b, x1p, tp[5], 1, m, 0, n_chunks)
    ysm = _combine(combine_tab, ys_sorted, route, toff_b, x1s, tsm[5], bs, ts, n_tiles - 1, n_chunks)

    keep = k32p.shape[1]
    new = (k32p.reshape(bp, keep, nh, HEAD_DIM), v32p.reshape(bp, keep, nh, HEAD_DIM),
           tail_p[:, SUBLANES - (CONV_WIDTH - 1):, :], hl_p[:, 0, :],
           k32s.reshape(bs, ts, nh, HEAD_DIM), v32s.reshape(bs, ts, nh, HEAD_DIM),
           tail_s[:, SUBLANES - (CONV_WIDTH - 1):, :], hl_s[:, 0, :])
    return yp, ysm, new


def kernel(x_prompt, x_sample, c_prompt, c_sample, cache_k, cache_v, state_conv, state_lru, ln_mix_w, ln_ffn_w, w_ada, b_ada, w_in, q_norm_w, k_norm_w, rel_bias, conv_w, conv_b, w_rgate, b_rgate, w_igate, b_igate, lru_lambda, w_out, w_router, b_router, w_gate_up, b_gate_up, w_down, b_down):
    depth = w_in.shape[0]
    yp, ys = x_prompt, x_sample
    c_all = jnp.concatenate([c_prompt, c_sample], axis=0)
    collected = [[] for _ in range(8)]
    for l in range(depth):
        mod = _ada(c_all, w_ada[l], b_ada[l])
        lw = (ln_mix_w[l], ln_ffn_w[l], w_in[l], q_norm_w[l], k_norm_w[l], rel_bias[l], conv_w[l], conv_b[l],
              w_rgate[l], b_rgate[l], w_igate[l], b_igate[l], lru_lambda[l], w_out[l], w_router[l], b_router[l],
              w_gate_up[l], b_gate_up[l], w_down[l], b_down[l])
        yp, ys, new = _layer(yp, ys, mod, cache_k[l], cache_v[l], state_conv[l], state_lru[l], lw)
        for acc, val in zip(collected, new):
            acc.append(val)
    return (yp, ys) + tuple(jnp.stack(vals) for vals in collected)
```

```python
import functools

import jax
import jax.numpy as jnp
from jax import lax
from jax.experimental import pallas as pl
from jax.experimental.pallas import tpu as pltpu

F32 = jnp.float32
BF16 = jnp.bfloat16

CHUNK = 64
N_LEFT_CHUNKS = 8
ATTN_WINDOW = N_LEFT_CHUNKS * CHUNK
HEAD_DIM = 64
REL_CLIP = 128
CONV_WIDTH = 4
LRU_C = 8.0
N_EXPERTS = 32
TOP_K = 4
SWIGLU_LIMIT = 7.0
SWIGLU_ALPHA = 1.702
NORM_EPS = 1e-6
NEG_INF = -1e30

LANES = 128
SUBLANES = 8
BF16_ROWS = 16
MXU_DIM = 256

TOKEN_TILE = 512
ATTN_Q_TILE = 256
ATTN_STEP_BATCH = 4
EXPERT_ROWS = 256
EXPERT_UNIT_BLOCKS = 4
BIAS_TABLE = 1024
VMEM_LIMIT = 56 * 1024 * 1024


def _params(sem, vmem=VMEM_LIMIT):
    return pltpu.CompilerParams(dimension_semantics=sem, vmem_limit_bytes=vmem)


def _ada_kernel(c_ref, w_ref, b_ref, o_ref):
    c = c_ref[...]
    s = (c * jax.nn.sigmoid(c)).astype(BF16)
    o_ref[...] = jnp.dot(s, w_ref[...].astype(BF16), preferred_element_type=F32) + b_ref[...]


def _ada(c_all, w_ada, b_ada):
    n, d = c_all.shape
    nout = w_ada.shape[1]
    tn = 1024
    return pl.pallas_call(
        _ada_kernel,
        grid=(nout // tn,),
        in_specs=[pl.BlockSpec((n, d), lambda j: (0, 0)),
                  pl.BlockSpec((d, tn), lambda j: (0, j)),
                  pl.BlockSpec((1, tn), lambda j: (0, j))],
        out_specs=pl.BlockSpec((n, tn), lambda j: (0, j)),
        out_shape=jax.ShapeDtypeStruct((n, nout), F32),
        compiler_params=_params(("arbitrary",)),
        name="ada",
    )(c_all, w_ada, b_ada.reshape(1, nout))


def _mixin_kernel(x_ref, sh_ref, sc_ref, ln_ref, win_ref, qn_ref, kn_ref, bd_ref,
                  pre_ref, h0_ref, cw_ref, cb_ref, wa_ref, wx_ref, ba_ref, bx_ref, lam_ref,
                  q_ref, k_ref, v_ref, k32_ref, v32_ref, lru_ref, tail_ref, hl_ref, cx_ref, ch_ref):
    nb, tr, d = x_ref.shape
    m = nb * tr
    aw = q_ref.shape[-1]

    @pl.when(pl.program_id(1) == 0)
    def _():
        cx_ref[...] = pre_ref[...]
        ch_ref[...] = h0_ref[...]

    x = x_ref[...]
    ms = jnp.mean(x * x, axis=-1, keepdims=True)
    h = x * lax.rsqrt(ms + NORM_EPS) * ln_ref[...]
    h = h * (1.0 + sc_ref[...]) + sh_ref[...]
    hb = h.reshape(m, d).astype(BF16)

    def proj(part):
        return jnp.dot(hb, win_ref[:, part * aw:(part + 1) * aw], preferred_element_type=F32)

    def head_norm(t, w_ref):
        msq = jnp.dot((t * t).astype(BF16), bd_ref[...], preferred_element_type=F32)
        return t * lax.rsqrt(msq + NORM_EPS) * w_ref[...]

    lru_out, new_tail, h_last = _lru_branch(
        proj(3).reshape(nb, tr, aw), proj(4).reshape(nb, tr, aw), cw_ref, cb_ref,
        wa_ref, wx_ref, ba_ref, bx_ref, lam_ref, cx_ref, ch_ref)
    lru_ref[...] = lru_out
    tail_ref[...] = new_tail
    hl_ref[...] = h_last
    q = head_norm(proj(0), qn_ref)
    k = head_norm(proj(1), kn_ref)
    v = proj(2)
    q_ref[...] = (q * (HEAD_DIM ** -0.5)).astype(BF16).reshape(nb, tr, aw)
    k_ref[...] = k.astype(BF16).reshape(nb, tr, aw)
    v_ref[...] = v.astype(BF16).reshape(nb, tr, aw)
    k32_ref[...] = k.reshape(nb, tr, aw)
    v32_ref[...] = v.reshape(nb, tr, aw)


def _mixin(x, sh, sc, ln_w, w_in_bf, qn_t, kn_t, bd, pre, h0, conv_w, conv_b, wa_bd, wx_bd, b_a, b_x, lam,
           nb, tr):
    nbt, t, d = x.shape
    aw = qn_t.shape[-1]
    c = pre.shape[-1]
    assert c == aw
    keep = min(ATTN_WINDOW, t)
    assert tr == keep or t == tr
    grid = (nbt // nb, t // tr)
    xmap = lambda b, i: (b, i, 0)
    mmap = lambda b, i: (b, 0, 0)
    cmap = lambda b, i: (0, 0)
    cmap3 = lambda b, i: (0, 0, 0)
    tmap = lambda b, i: (b, 0, 0)
    big = pl.BlockSpec((nb, tr, aw), xmap)
    tail = pl.BlockSpec((nb, keep, aw), tmap)
    row = pl.BlockSpec((1, c), cmap)
    return pl.pallas_call(
        _mixin_kernel,
        grid=grid,
        in_specs=[pl.BlockSpec((nb, tr, d), xmap),
                  pl.BlockSpec((nb, 1, d), mmap), pl.BlockSpec((nb, 1, d), mmap),
                  pl.BlockSpec((1, d), cmap),
                  pl.BlockSpec(w_in_bf.shape, cmap),
                  pl.BlockSpec((1, aw), cmap), pl.BlockSpec((1, aw), cmap),
                  pl.BlockSpec(bd.shape, cmap),
                  pl.BlockSpec((nb, SUBLANES, c), tmap), pl.BlockSpec((nb, 1, c), tmap),
                  pl.BlockSpec(conv_w.shape, cmap), row,
                  pl.BlockSpec(wa_bd.shape, cmap3), pl.BlockSpec(wx_bd.shape, cmap3),
                  row, row, row],
        out_specs=[big, big, big, tail, tail, big,
                   pl.BlockSpec((nb, SUBLANES, c), tmap), pl.BlockSpec((nb, 1, c), tmap)],
        out_shape=[jax.ShapeDtypeStruct((nbt, t, aw), BF16)] * 3
        + [jax.ShapeDtypeStruct((nbt, keep, aw), F32)] * 2
        + [jax.ShapeDtypeStruct((nbt, t, c), BF16),
           jax.ShapeDtypeStruct((nbt, SUBLANES, c), F32),
           jax.ShapeDtypeStruct((nbt, 1, c), F32)],
        scratch_shapes=[pltpu.VMEM((nb, SUBLANES, c), F32), pltpu.VMEM((nb, 1, c), F32)],
        compiler_params=_params(("arbitrary", "arbitrary")),
        name="mixin",
    )(x, sh, sc, ln_w, w_in_bf, qn_t, kn_t, bd, pre, h0, conv_w, conv_b, wa_bd, wx_bd, b_a, b_x, lam)


def _bias_table(rel_bias, off):
    h = rel_bias.shape[0]
    left = off - REL_CLIP
    right = BIAS_TABLE - left - (2 * REL_CLIP + 1)
    assert left >= 0 and right >= 0
    return jnp.concatenate([jnp.broadcast_to(rel_bias[:, :1], (h, left)), rel_bias,
                            jnp.broadcast_to(rel_bias[:, -1:], (h, right))], axis=1)


def _toeplitz(tab_row, rows, cols):
    t = jnp.broadcast_to(tab_row, (rows, BIAS_TABLE))
    t = pltpu.roll(t, BIAS_TABLE - (rows - 1), 1, stride=1, stride_axis=0)
    return t[:, :cols]


def _attn_kernel(q_ref, k0_ref, k1_ref, k2_ref, v0_ref, v1_ref, v2_ref, tab_ref, o_ref, bias_ref):
    b = pl.program_id(0)
    s = pl.program_id(1)
    qt = q_ref.shape[1]
    nk = 3 * qt
    nh = bias_ref.shape[0]

    @pl.when((b == 0) & (s == 0))
    def _():
        qi = lax.broadcasted_iota(jnp.int32, (qt, nk), 0) // CHUNK
        kc = lax.broadcasted_iota(jnp.int32, (qt, nk), 1) // CHUNK
        for h in range(nh):
            band = jnp.where(kc <= qi + N_LEFT_CHUNKS, _toeplitz(tab_ref[h:h + 1, :], qt, nk), NEG_INF)
            bias_ref[h] = jnp.where(kc >= qi, band, NEG_INF)

    pair_w = 2 * HEAD_DIM

    def attend(mask_start):
        q = q_ref[0]
        kcat = jnp.concatenate([k0_ref[0], k1_ref[0], k2_ref[0]], axis=0)
        vcat = jnp.concatenate([v0_ref[0], v1_ref[0], v2_ref[0]], axis=0)
        first = lax.broadcasted_iota(jnp.int32, (qt, pair_w), 1) < HEAD_DIM
        keep = [jnp.where(first, 1.0, 0.0).astype(BF16), jnp.where(first, 0.0, 1.0).astype(BF16)]
        if mask_start:
            in_seq = lax.broadcasted_iota(jnp.int32, (qt, nk), 1) >= (2 - s) * qt
        outs = []
        for pair in range(nh // 2):
            sl = slice(pair * pair_w, (pair + 1) * pair_w)
            q2, k2, v2 = q[:, sl], kcat[:, sl], vcat[:, sl]
            per_head = []
            for sub in range(2):
                sc = lax.dot_general(q2 * keep[sub], k2, (((1,), (1,)), ((), ())), preferred_element_type=F32)
                sc = sc + bias_ref[2 * pair + sub]
                if mask_start:
                    sc = jnp.where(in_seq, sc, NEG_INF)
                mx = jnp.max(sc, axis=-1, keepdims=True)
                p = jnp.exp(sc - mx)
                l = jnp.sum(p, axis=-1, keepdims=True)
                per_head.append(jnp.dot(p.astype(BF16), v2, preferred_element_type=F32) / l)
            outs.append(jnp.where(first, per_head[0], per_head[1]))
        o_ref[0] = jnp.concatenate(outs, axis=-1).astype(BF16)

    @pl.when(s < 2)
    def _():
        attend(True)

    @pl.when(s >= 2)
    def _():
        attend(False)


def _attn_prompt(q, k, v, tab):
    b, s, aw = q.shape
    qt = ATTN_Q_TILE
    nh = aw // HEAD_DIM
    qspec = pl.BlockSpec((1, qt, aw), lambda i, j: (i, j, 0))

    def kspec(back):
        return pl.BlockSpec((1, qt, aw), lambda i, j: (i, jnp.maximum(j - back, 0), 0))

    return pl.pallas_call(
        _attn_kernel,
        grid=(b, s // qt),
        in_specs=[qspec, kspec(2), kspec(1), kspec(0), kspec(2), kspec(1), kspec(0),
                  pl.BlockSpec(tab.shape, lambda i, j: (0, 0))],
        out_specs=qspec,
        out_shape=jax.ShapeDtypeStruct((b, s, aw), BF16),
        scratch_shapes=[pltpu.VMEM((nh, qt, 3 * qt), F32)],
        compiler_params=_params(("arbitrary", "arbitrary")),
        name="attn_prompt",
    )(q, k, k, k, v, v, v, tab)


def _attn_step_kernel(q_ref, kn_ref, vn_ref, ck_ref, cv_ref, tab_ref, o_ref, bias_ref):
    step = pl.program_id(0)
    nbs, t, _ = q_ref.shape
    nh = bias_ref.shape[0]
    r = ck_ref.shape[-1]
    nk = r + LANES
    nt_dims = (((1,), (1,)), ((), ()))

    @pl.when(step == 0)
    def _():
        ok = lax.broadcasted_iota(jnp.int32, (t, nk), 1) < r + t
        for h in range(nh):
            bias_ref[h] = jnp.where(ok, _toeplitz(tab_ref[h:h + 1, :], t, nk), NEG_INF)

    pad = jnp.zeros((LANES - t, HEAD_DIM), BF16)
    for b in range(nbs):
        q = q_ref[b]
        kn = kn_ref[b]
        vn = vn_ref[b]
        outs = []
        for h in range(nh):
            sl = slice(h * HEAD_DIM, (h + 1) * HEAD_DIM)
            qh = q[:, sl]
            k_new = jnp.concatenate([kn[:, sl], pad], axis=0)
            v_new = jnp.concatenate([vn[:, sl], pad], axis=0)
            s_old = jnp.dot(qh, ck_ref[b, h].astype(BF16), preferred_element_type=F32) + bias_ref[h, :, :r]
            s_new = lax.dot_general(qh, k_new, nt_dims, preferred_element_type=F32) + bias_ref[h, :, r:]
            mx = jnp.maximum(jnp.max(s_old, axis=-1, keepdims=True), jnp.max(s_new, axis=-1, keepdims=True))
            p_old = jnp.exp(s_old - mx)
            p_new = jnp.exp(s_new - mx)
            l = jnp.sum(p_old, axis=-1, keepdims=True) + jnp.sum(p_new, axis=-1, keepdims=True)
            o = (lax.dot_general(p_old.astype(BF16), cv_ref[b, h].astype(BF16), nt_dims,
                                 preferred_element_type=F32)
                 + jnp.dot(p_new.astype(BF16), v_new, preferred_element_type=F32))
            outs.append(o / l)
        o_ref[b] = jnp.concatenate(outs, axis=-1).astype(BF16)


def _attn_step(q, kn, vn, ck, cv, tab, nbs):
    b, t, aw = q.shape
    nh = aw // HEAD_DIM
    r = ck.shape[-1]
    new = pl.BlockSpec((nbs, t, aw), lambda i: (i, 0, 0))
    old = pl.BlockSpec((nbs, nh, HEAD_DIM, r), lambda i: (i, 0, 0, 0))
    return pl.pallas_call(
        _attn_step_kernel,
        grid=(b // nbs,),
        in_specs=[new, new, new, old, old, pl.BlockSpec(tab.shape, lambda i: (0, 0))],
        out_specs=new,
        out_shape=jax.ShapeDtypeStruct((b, t, aw), BF16),
        scratch_shapes=[pltpu.VMEM((nh, t, r + LANES), F32)],
        compiler_params=_params(("arbitrary",)),
        name="attn_step",
    )(q, kn, vn, ck, cv, tab)


def _gelu_tanh(x):
    return x * (0.5 * (1.0 + jnp.tanh(0.7978845608028654 * (x + 0.044715 * (x * x * x)))))


def _lru_branch(x, yg, cw_ref, cb_ref, wa_ref, wx_ref, ba_ref, bx_ref, lam_ref, cx_ref, ch_ref):
    nb, tr, c = x.shape
    m = nb * tr
    half = c // 2
    xp = jnp.concatenate([cx_ref[...], x], axis=1)
    new_tail = xp[:, tr:tr + SUBLANES, :]
    xp2 = xp.reshape(nb * (tr + SUBLANES), c)
    y = cb_ref[...] + cw_ref[CONV_WIDTH - 1:CONV_WIDTH, :] * x
    for back in range(1, CONV_WIDTH):
        shifted = pltpu.roll(xp2, back, 0).reshape(nb, tr + SUBLANES, c)[:, SUBLANES:, :]
        y = y + cw_ref[CONV_WIDTH - 1 - back:CONV_WIDTH - back, :] * shifted
    y2 = y.reshape(m, c)
    yb = y2.astype(BF16)

    def gate(w_ref, b_ref):
        g = jnp.concatenate(
            [jnp.dot(yb[:, :half], w_ref[0], preferred_element_type=F32),
             jnp.dot(yb[:, half:], w_ref[1], preferred_element_type=F32)], axis=1)
        return jax.nn.sigmoid(g + b_ref[...])

    rg = gate(wa_ref, ba_ref)
    ig = gate(wx_ref, bx_ref)
    lam = lam_ref[...]
    log_sig = jnp.minimum(lam, 0.0) - jnp.log1p(jnp.exp(-jnp.abs(lam)))
    log_a = LRU_C * rg * log_sig
    a_cum = jnp.exp(log_a)
    b_cum = jnp.sqrt(-jnp.tanh(log_a) * (a_cum * a_cum + 1.0)) * (ig * y2)
    row = lax.broadcasted_iota(jnp.int32, (m, c), 0) % SUBLANES
    dist = 1
    while dist < SUBLANES:
        keep = row >= dist
        a_sh = jnp.where(keep, pltpu.roll(a_cum, dist, 0), 1.0)
        b_sh = jnp.where(keep, pltpu.roll(b_cum, dist, 0), 0.0)
        b_cum = a_cum * b_sh + b_cum
        a_cum = a_cum * a_sh
        dist *= 2
    groups = tr // SUBLANES
    a_grp = a_cum.reshape(nb, groups, SUBLANES, c)
    b_grp = b_cum.reshape(nb, groups, SUBLANES, c)
    carry = ch_ref[...]
    pieces = []
    for grp in range(groups):
        h_grp = a_grp[:, grp] * carry + b_grp[:, grp]
        carry = h_grp[:, SUBLANES - 1:SUBLANES, :]
        pieces.append(h_grp)
    h = jnp.concatenate(pieces, axis=1)
    ch_ref[...] = carry
    cx_ref[...] = new_tail
    return (h * _gelu_tanh(yg)).astype(BF16), new_tail, carry


def _outproj_kernel(*refs, aliased):
    (at_ref, lr_ref, x_ref, gm_ref, shf_ref, scf_ref, lnf_ref, wo_ref, wr_ref, br_ref) = refs[:10]
    x1_ref, h2_ref, route_ref, cnt_ref = refs[10 + aliased:]
    nb, tr, d = x_ref.shape
    m = nb * tr
    aw = at_ref.shape[-1]
    ne = wr_ref.shape[0]
    at = at_ref[...].reshape(m, aw)
    lr = lr_ref[...].reshape(m, aw)
    mix = (jnp.dot(at, wo_ref[0:aw, :], preferred_element_type=F32)
           + jnp.dot(lr, wo_ref[aw:2 * aw, :], preferred_element_type=F32))
    x1 = x_ref[...] + gm_ref[...] * mix.reshape(nb, tr, d)
    x1_ref[...] = x1
    ms = jnp.mean(x1 * x1, axis=-1, keepdims=True)
    h2 = x1 * lax.rsqrt(ms + NORM_EPS) * lnf_ref[...]
    h2 = (h2 * (1.0 + scf_ref[...]) + shf_ref[...]).reshape(m, d)
    h2_ref[...] = h2.astype(BF16)

    logits = lax.dot_general(wr_ref[...], h2.astype(BF16), (((1,), (1,)), ((), ())),
                             preferred_element_type=F32) + br_ref[...]
    e_iota = lax.broadcasted_iota(jnp.int32, (ne, m), 0).astype(F32)
    vals = logits
    top_v, sels = [], []
    for k in range(TOP_K):
        mx = jnp.max(vals, axis=0, keepdims=True)
        idx = jnp.min(jnp.where(vals == mx, e_iota, float(ne)), axis=0, keepdims=True)
        sel = e_iota == idx
        vals = jnp.where(sel, -jnp.inf, vals)
        top_v.append(mx)
        sels.append(sel)
        route_ref[0, k:k + 1, :] = idx
    ex = [jnp.exp(v - top_v[0]) for v in top_v]
    den = ex[0] + ex[1] + ex[2] + ex[3]
    chosen = jnp.zeros((ne, m), F32)
    for k in range(TOP_K):
        route_ref[0, 2 * TOP_K + k:2 * TOP_K + k + 1, :] = ex[k] / den
        chosen = chosen + jnp.where(sels[k], 1.0, 0.0)
    before = (lax.broadcasted_iota(jnp.int32, (m, m), 0) < lax.broadcasted_iota(jnp.int32, (m, m), 1))
    rank = jnp.dot(chosen.astype(BF16), jnp.where(before, 1.0, 0.0).astype(BF16), preferred_element_type=F32)
    for k in range(TOP_K):
        route_ref[0, TOP_K + k:TOP_K + k + 1, :] = jnp.sum(jnp.where(sels[k], rank, 0.0), axis=0, keepdims=True)
    route_ref[0, 3 * TOP_K:4 * TOP_K, :] = jnp.zeros((TOP_K, m), F32)
    cnt_ref[0] = jnp.broadcast_to(jnp.sum(chosen, axis=1, keepdims=True), (ne, LANES))


def _outproj(attn, lru_o, x, gm, shf, scf, lnf, w_out_bf, wr_t, br, nb, tr, n_tiles, tile0, prev):
    nbt, t, d = x.shape
    aw = attn.shape[-1]
    m = nb * tr
    assert m == TOKEN_TILE
    ne = wr_t.shape[0]
    tiles_per_seq = t // tr
    xmap = lambda b, i: (b, i, 0)
    mmap = lambda b, i: (b, 0, 0)
    c2 = lambda b, i: (0, 0)
    tile = lambda b, i: (tile0 + b * tiles_per_seq + i, 0)
    tile3 = lambda b, i: (tile0 + b * tiles_per_seq + i, 0, 0)
    mod = pl.BlockSpec((nb, 1, d), mmap)
    in_specs = [pl.BlockSpec((nb, tr, aw), xmap), pl.BlockSpec((nb, tr, aw), xmap),
                pl.BlockSpec((nb, tr, d), xmap), mod, mod, mod,
                pl.BlockSpec((1, d), c2), pl.BlockSpec(w_out_bf.shape, c2),
                pl.BlockSpec(wr_t.shape, c2), pl.BlockSpec((ne, 1), c2)]
    args = [attn, lru_o, x, gm, shf, scf, lnf, w_out_bf, wr_t, br]
    aliases = {}
    if prev is not None:
        in_specs += [pl.BlockSpec(memory_space=pl.ANY)] * 3
        args += list(prev)
        aliases = {10: 1, 11: 2, 12: 3}
    return pl.pallas_call(
        functools.partial(_outproj_kernel, aliased=len(aliases)),
        grid=(nbt // nb, tiles_per_seq),
        in_specs=in_specs,
        out_specs=[pl.BlockSpec((nb, tr, d), xmap), pl.BlockSpec((m, d), tile),
                   pl.BlockSpec((1, 4 * TOP_K, m), tile3), pl.BlockSpec((1, ne, LANES), tile3)],
        out_shape=[jax.ShapeDtypeStruct((nbt, t, d), F32),
                   jax.ShapeDtypeStruct((n_tiles * m, d), BF16),
                   jax.ShapeDtypeStruct((n_tiles, 4 * TOP_K, m), F32),
                   jax.ShapeDtypeStruct((n_tiles, ne, LANES), F32)],
        input_output_aliases=aliases,
        compiler_params=_params(("arbitrary", "arbitrary")),
        name="outproj",
    )(*args)


def _tile_rows(m):
    cap = TOP_K * m + N_EXPERTS * (BF16_ROWS - 1) + BF16_ROWS
    return -(-cap // TOKEN_TILE) * TOKEN_TILE


def _table_sizes(nt):
    g = BF16_ROWS
    m = TOKEN_TILE
    n_chunks = _tile_rows(m) // g
    n_gap = -(-(N_EXPERTS * (EXPERT_ROWS // g - 1)) // nt)
    bound = TOP_K * m * nt + nt * N_EXPERTS * (g - 1) + N_EXPERTS * (EXPERT_ROWS - g)
    n_sorted = -(-bound // EXPERT_ROWS) * EXPERT_ROWS
    return n_chunks, n_gap, n_sorted, n_sorted + 2 * (n_chunks + n_gap) * g


def _route_tables(cnt):
    nt = cnt.shape[0]
    g = BF16_ROWS
    bm = EXPERT_ROWS
    n_chunks, n_gap, n_sorted, _ = _table_sizes(nt)
    e_ids = jnp.arange(N_EXPERTS, dtype=jnp.int32)
    t_ids = jnp.arange(nt, dtype=jnp.int32)
    upto = (e_ids[:, None] <= e_ids[None, :]).astype(jnp.int32)
    pc = (cnt + g - 1) // g * g
    ctile = jnp.sum(pc[:, :, None] * upto[None], axis=1)
    toff = ctile - pc
    trow = ctile[:, -1]
    tot = jnp.sum(pc, axis=0)
    reg = (tot + bm - 1) // bm * bm
    creg = jnp.sum(reg[:, None] * upto, axis=0)
    base = creg - reg
    earlier = (t_ids[:, None] < t_ids[None, :]).astype(jnp.int32)
    goff = base[None, :] + jnp.sum(pc[:, None, :] * earlier[:, :, None], axis=0)
    r = jnp.arange(n_chunks, dtype=jnp.int32) * g
    r3 = r[None, :, None]
    in_seg = (toff[:, None, :] <= r3) & (r3 < ctile[:, None, :])
    dst = jnp.sum(jnp.where(in_seg, (goff - toff)[:, None, :], 0), axis=2) + r[None, :]
    dst = jnp.where(r[None, :] < trow[:, None], dst, -1)
    gcnt = (reg - tot) // g
    gcum = jnp.sum(gcnt[:, None] * upto, axis=0)
    gstart = gcum - gcnt
    s = jnp.arange(nt * n_gap, dtype=jnp.int32)
    in_gap = (gstart[None, :] <= s[:, None]) & (s[:, None] < gcum[None, :])
    gdst = jnp.sum(jnp.where(in_gap, (base + tot - g * gstart)[None, :] + g * s[:, None], 0), axis=1)
    gdst = jnp.where(s < gcum[-1], gdst, -1).reshape(nt, n_gap)
    table = jnp.concatenate([dst, gdst], axis=1).astype(jnp.int32)
    n_entries = n_chunks + n_gap
    spare = n_sorted + ((t_ids % 2)[:, None] * n_entries + jnp.arange(n_entries, dtype=jnp.int32)[None, :]) * g
    dispatch_tab = jnp.where(table >= 0, table, spare).astype(jnp.int32)
    combine_tab = jnp.maximum(dst, 0).astype(jnp.int32)
    toff_b = jnp.broadcast_to(toff.astype(F32)[:, :, None], (nt, N_EXPERTS, LANES))
    limit = (creg[-1] - EXPERT_UNIT_BLOCKS * bm).astype(jnp.int32).reshape(1)
    return dispatch_tab, combine_tab, toff_b, base.astype(jnp.int32), (reg // bm).astype(jnp.int32), limit


def _slot_rows(route_ref, toff_ref, m):
    ne = toff_ref.shape[1]
    e_iota = lax.broadcasted_iota(jnp.int32, (ne, m), 0).astype(F32)
    toff_col = toff_ref[0][:, 0:1]
    pos = []
    for k in range(TOP_K):
        sel = e_iota == route_ref[0, k:k + 1, :]
        start = jnp.sum(jnp.where(sel, toff_col, 0.0), axis=0, keepdims=True)
        pos.append(start + route_ref[0, TOP_K + k:TOP_K + k + 1, :])
    return pos


def _dispatch_kernel(tab_ref, h2_ref, route_ref, toff_ref, xs_hbm, buf_ref, sem, *, n_chunks, n_tiles):
    t = pl.program_id(0)
    slot = t % 2
    m = h2_ref.shape[0]
    rows = buf_ref.shape[1]
    n_entries = tab_ref.shape[1]
    g = BF16_ROWS
    per_chunk = m // g

    def start(c):
        src = c * g if c < n_chunks else rows - g
        pltpu.make_async_copy(
            buf_ref.at[slot, pl.ds(src, g)],
            xs_hbm.at[pl.ds(pl.multiple_of(tab_ref[t, c], g), g)], sem.at[slot]).start()

    def wait_all(which):
        for _ in range(n_entries):
            pltpu.make_async_copy(buf_ref.at[which, pl.ds(0, g)], xs_hbm.at[pl.ds(0, g)], sem.at[which]).wait()

    @pl.when(t >= 2)
    def _():
        wait_all(slot)

    pos = _slot_rows(route_ref, toff_ref, m)
    h2 = h2_ref[...]
    for rc in range(rows // m):
        r_iota = (lax.broadcasted_iota(jnp.int32, (m, m), 0) + rc * m).astype(F32)
        onehot = jnp.zeros((m, m), F32)
        for k in range(TOP_K):
            onehot = jnp.where(r_iota == pos[k], 1.0, onehot)
        buf_ref[slot, rc * m:(rc + 1) * m, :] = jnp.dot(
            onehot.astype(BF16), h2, preferred_element_type=F32).astype(BF16)
        for c in range((rc - 1) * per_chunk, rc * per_chunk) if rc > 0 else ():
            start(c)
    for c in range(n_chunks - per_chunk, n_entries):
        start(c)

    @pl.when(t == n_tiles - 1)
    def _():
        if n_tiles > 1:
            wait_all(1 - slot)
        wait_all(slot)


def _dispatch(table, h2, route, toff_b, n_rows, n_chunks):
    nt = route.shape[0]
    m = TOKEN_TILE
    d = h2.shape[1]
    rows = _tile_rows(m)
    grid_spec = pltpu.PrefetchScalarGridSpec(
        num_scalar_prefetch=1,
        grid=(nt,),
        in_specs=[pl.BlockSpec((m, d), lambda t, tab: (t, 0)),
                  pl.BlockSpec((1, 4 * TOP_K, m), lambda t, tab: (t, 0, 0)),
                  pl.BlockSpec((1, N_EXPERTS, LANES), lambda t, tab: (t, 0, 0))],
        out_specs=pl.BlockSpec(memory_space=pl.ANY),
        scratch_shapes=[pltpu.VMEM((2, rows, d), BF16), pltpu.SemaphoreType.DMA((2,))],
    )
    return pl.pallas_call(
        functools.partial(_dispatch_kernel, n_chunks=n_chunks, n_tiles=nt),
        grid_spec=grid_spec,
        out_shape=jax.ShapeDtypeStruct((n_rows, d), BF16),
        compiler_params=_params(("arbitrary",)),
        name="dispatch",
    )(table, h2, route, toff_b)


def _expert_kernel(row0_ref, nblk_ref, limit_ref, xs_hbm, wgu_ref, bgu_ref, wdn_ref, bdn_ref, ys_hbm,
                   wgu_bf, wdn_bf, xbuf, ybuf, sem_in, sem_out):
    e = pl.program_id(0)
    ne = pl.num_programs(0)
    bm = EXPERT_ROWS
    unit = xbuf.shape[1]
    per_unit = unit // bm
    dff = wdn_ref.shape[1]
    nblk = nblk_ref[e]
    n_units = (nblk + per_unit - 1) // per_unit

    def unit_start(expert, s):
        true = row0_ref[expert] + s * unit
        start = jnp.minimum(true, limit_ref[0])
        return pl.multiple_of(start, bm), pl.multiple_of(true - start, bm)

    def in_copy(expert, s, slot):
        start, _ = unit_start(expert, s)
        return pltpu.make_async_copy(xs_hbm.at[pl.ds(start, unit)], xbuf.at[slot], sem_in.at[slot])

    def out_copy(j, slot, i):
        start = pl.multiple_of(row0_ref[e] + j * bm, bm)
        return pltpu.make_async_copy(ybuf.at[slot, pl.ds(i * bm, bm)], ys_hbm.at[pl.ds(start, bm)],
                                     sem_out.at[slot])

    @pl.when((e == 0) & (nblk > 0))
    def _():
        in_copy(e, 0, 0).start()

    wgu_bf[...] = wgu_ref[0].astype(BF16)
    wdn_bf[...] = wdn_ref[0].astype(BF16)

    def run_unit(s, carry):
        slot = s % 2
        in_copy(e, s, slot).wait()

        @pl.when(s + 1 < n_units)
        def _():
            in_copy(e, s + 1, 1 - slot).start()

        @pl.when(s >= 2)
        def _():
            for i in range(per_unit):
                out_copy(0, slot, i).wait()

        _, lead = unit_start(e, s)

        def ffn(first, count):
            rows = count * bm
            x = xbuf[slot, pl.ds(pl.multiple_of(lead + first * bm, bm), rows), :]
            gu = jnp.dot(x, wgu_bf[...], preferred_element_type=F32) + bgu_ref[0]
            gate = jnp.minimum(gu[:, :dff], SWIGLU_LIMIT)
            up = jnp.clip(gu[:, dff:], -SWIGLU_LIMIT, SWIGLU_LIMIT)
            glu = gate * jax.nn.sigmoid(gate * SWIGLU_ALPHA)
            act = ((up + 1.0) * glu).astype(BF16)
            ybuf[slot, first * bm:first * bm + rows, :] = (
                jnp.dot(act, wdn_bf[...], preferred_element_type=F32) + bdn_ref[0]).astype(BF16)
            for i in range(first, first + count):
                out_copy(s * per_unit + i, slot, i).start()

        for first in range(0, per_unit, 2):
            have = nblk - s * per_unit - first

            @pl.when(have >= 2)
            def _():
                ffn(first, 2)

            @pl.when(have == 1)
            def _():
                ffn(first, 1)
        return carry

    lax.fori_loop(0, n_units, run_unit, 0)

    nxt = jnp.minimum(e + 1, ne - 1)

    @pl.when((e + 1 < ne) & (nblk_ref[nxt] > 0))
    def _():
        in_copy(nxt, 0, 0).start()

    @pl.when(n_units >= 2)
    def _():
        for i in range(per_unit):
            out_copy(0, n_units % 2, i).wait()

    in_last = nblk - (n_units - 1) * per_unit
    for i in range(per_unit):
        @pl.when((n_units >= 1) & (i < in_last))
        def _():
            out_copy(0, (n_units - 1) % 2, i).wait()


def _experts(row0, nblk, limit, xs, w_gu, b_gu, w_dn, b_dn):
    n_rows, d = xs.shape
    bm = EXPERT_ROWS
    unit = EXPERT_UNIT_BLOCKS * bm
    ne, _, dff2 = w_gu.shape
    dff = w_dn.shape[1]
    exp3 = lambda e, r0, nb, lim: (e, 0, 0)
    grid_spec = pltpu.PrefetchScalarGridSpec(
        num_scalar_prefetch=3,
        grid=(ne,),
        in_specs=[pl.BlockSpec(memory_space=pl.ANY),
                  pl.BlockSpec((1, d, dff2), exp3), pl.BlockSpec((1, 1, dff2), exp3),
                  pl.BlockSpec((1, dff, d), exp3), pl.BlockSpec((1, 1, d), exp3)],
        out_specs=pl.BlockSpec(memory_space=pl.ANY),
        scratch_shapes=[pltpu.VMEM((d, dff2), BF16), pltpu.VMEM((dff, d), BF16),
                        pltpu.VMEM((2, unit, d), BF16), pltpu.VMEM((2, unit, d), BF16),
                        pltpu.SemaphoreType.DMA((2,)), pltpu.SemaphoreType.DMA((2,))],
    )
    return pl.pallas_call(
        _expert_kernel,
        grid_spec=grid_spec,
        out_shape=jax.ShapeDtypeStruct((n_rows, d), BF16),
        compiler_params=_params(("arbitrary",)),
        name="experts",
    )(row0, nblk, limit, xs, w_gu, b_gu.reshape(ne, 1, dff2), w_dn, b_dn.reshape(ne, 1, d))


def _combine_kernel(tab_ref, ys_hbm, route_ref, toff_ref, x1_ref, gf_ref, o_ref, buf_ref, sem, *, n_chunks, tile0):
    step = pl.program_id(0)
    n_steps = pl.num_programs(0)
    t = tile0 + step
    slot = step % 2
    nb, tr, d = x1_ref.shape
    m = nb * tr
    rows = buf_ref.shape[1]
    g = BF16_ROWS

    def fetch(tile, which):
        for c in range(n_chunks):
            pltpu.make_async_copy(
                ys_hbm.at[pl.ds(pl.multiple_of(tab_ref[tile, c], g), g)],
                buf_ref.at[which, pl.ds(c * g, g)], sem.at[which]).start()

    @pl.when(step == 0)
    def _():
        fetch(t, slot)

    @pl.when(step + 1 < n_steps)
    def _():
        fetch(t + 1, 1 - slot)

    pos = _slot_rows(route_ref, toff_ref, m)
    gates = [route_ref[0, 2 * TOP_K + k:2 * TOP_K + k + 1, :] for k in range(TOP_K)]
    stacked = jnp.concatenate(pos + gates + [jnp.zeros((LANES - 2 * TOP_K, m), F32)], axis=0)
    cols = stacked.T
    for _ in range(n_chunks):
        pltpu.make_async_copy(ys_hbm.at[pl.ds(0, g)], buf_ref.at[slot, pl.ds(0, g)], sem.at[slot]).wait()

    acc = jnp.zeros((m, d), F32)
    for rc in range(rows // m):
        c_iota = (lax.broadcasted_iota(jnp.int32, (m, m), 1) + rc * m).astype(F32)
        weights = jnp.zeros((m, m), F32)
        for k in range(TOP_K):
            weights = jnp.where(c_iota == cols[:, k:k + 1], cols[:, TOP_K + k:TOP_K + k + 1], weights)
        acc = acc + jnp.dot(weights.astype(BF16), buf_ref[slot, rc * m:(rc + 1) * m, :],
                            preferred_element_type=F32)
    o_ref[...] = x1_ref[...] + gf_ref[...] * acc.reshape(nb, tr, d)


def _combine(table, ys, route, toff_b, x1, gf, nb, tr, tile0, n_chunks):
    nbt, t, d = x1.shape
    m = nb * tr
    assert m == TOKEN_TILE
    rows = _tile_rows(m)
    tiles_per_seq = t // tr
    n_steps = (nbt // nb) * tiles_per_seq
    xmap = lambda s, tab: (s // tiles_per_seq, s % tiles_per_seq, 0)
    grid_spec = pltpu.PrefetchScalarGridSpec(
        num_scalar_prefetch=1,
        grid=(n_steps,),
        in_specs=[pl.BlockSpec(memory_space=pl.ANY),
                  pl.BlockSpec((1, 4 * TOP_K, m), lambda s, tab: (tile0 + s, 0, 0)),
                  pl.BlockSpec((1, N_EXPERTS, LANES), lambda s, tab: (tile0 + s, 0, 0)),
                  pl.BlockSpec((nb, tr, d), xmap),
                  pl.BlockSpec((nb, 1, d), lambda s, tab: (s // tiles_per_seq, 0, 0))],
        out_specs=pl.BlockSpec((nb, tr, d), xmap),
        scratch_shapes=[pltpu.VMEM((2, rows, d), BF16), pltpu.SemaphoreType.DMA((2,))],
    )
    return pl.pallas_call(
        functools.partial(_combine_kernel, n_chunks=n_chunks, tile0=tile0),
        grid_spec=grid_spec,
        out_shape=jax.ShapeDtypeStruct((nbt, t, d), F32),
        compiler_params=_params(("arbitrary",)),
        name="combine",
    )(table, ys, route, toff_b, x1, gf)


def _block_diag(w, groups):
    n, k, _ = w.shape
    w = w.reshape(n // groups, groups, k, k)
    eye = jnp.eye(groups, dtype=w.dtype)
    return jnp.einsum("ngij,gh->ngihj", w, eye).reshape(n // groups, groups * k, groups * k)


def _layer(xp, xs, mod, k_cache, v_cache, conv_state, lru_state, lw):
    (ln_mix, ln_ffn, w_in, q_norm, k_norm, rel_bias, conv_w, conv_b, w_rg, b_rg, w_ig, b_ig, lam,
     w_out, w_router, b_router, w_gu, b_gu, w_dn, b_dn) = lw
    bp, s, d = xp.shape
    bs, ts, _ = xs.shape
    aw = w_out.shape[0] // 2
    nh = aw // HEAD_DIM
    m = TOKEN_TILE
    assert s % m == 0 and bs * ts == m and s % ATTN_Q_TILE == 0

    terms = [mod[:, i * d:(i + 1) * d][:, None, :] for i in range(6)]
    tp = [t[:bp] for t in terms]
    tsm = [t[bp:] for t in terms]

    w_in_bf = w_in.astype(BF16)
    w_out_bf = w_out.astype(BF16)
    qn_t = jnp.tile(q_norm, nh).reshape(1, aw)
    kn_t = jnp.tile(k_norm, nh).reshape(1, aw)
    head_mean = _block_diag(jnp.full((nh, HEAD_DIM, HEAD_DIM), 1.0 / HEAD_DIM, F32), nh)[0].astype(BF16)
    groups = MXU_DIM // w_rg.shape[-1]
    wa_bd = _block_diag(w_rg, groups).astype(BF16)
    wx_bd = _block_diag(w_ig, groups).astype(BF16)
    lw_c = b_rg.size
    b_a = b_rg.reshape(1, lw_c)
    b_x = b_ig.reshape(1, lw_c)
    lam2 = lam.reshape(1, lw_c)
    cb2 = conv_b.reshape(1, lw_c)
    ln_mix2 = ln_mix.reshape(1, d)
    ln_ffn2 = ln_ffn.reshape(1, d)
    wr_t = w_router.T.astype(BF16)
    br = b_router.reshape(-1, 1)
    tab_p = _bias_table(rel_bias, 3 * ATTN_Q_TILE - 1)
    r_cache = k_cache.shape[1]
    tab_s = _bias_table(rel_bias, r_cache + ts - 1)

    zeros_pre = jnp.zeros((bp, SUBLANES, lw_c), F32)
    zeros_h = jnp.zeros((bp, 1, lw_c), F32)
    pre_s = jnp.pad(conv_state, ((0, 0), (SUBLANES - (CONV_WIDTH - 1), 0), (0, 0)))
    lru_w = (conv_w, cb2, wa_bd, wx_bd, b_a, b_x, lam2)
    qp, kp, vp, k32p, v32p, lru_p, tail_p, hl_p = _mixin(
        xp, tp[0], tp[1], ln_mix2, w_in_bf, qn_t, kn_t, head_mean, zeros_pre, zeros_h, *lru_w, 1, m)
    qs, ks, vs, k32s, v32s, lru_s, tail_s, hl_s = _mixin(
        xs, tsm[0], tsm[1], ln_mix2, w_in_bf, qn_t, kn_t, head_mean, pre_s, lru_state[:, None, :], *lru_w, bs, ts)
    attn_p = _attn_prompt(qp, kp, vp, tab_p)
    attn_s = _attn_step(qs, ks, vs, jnp.transpose(k_cache, (0, 2, 3, 1)), jnp.transpose(v_cache, (0, 2, 3, 1)),
                        tab_s, ATTN_STEP_BATCH)

    n_tiles = bp * (s // m) + 1
    x1p, h2, route, cnt = _outproj(attn_p, lru_p, xp, tp[2], tp[3], tp[4], ln_ffn2, w_out_bf, wr_t, br,
                                   1, m, n_tiles, 0, None)
    x1s, h2, route, cnt = _outproj(attn_s, lru_s, xs, tsm[2], tsm[3], tsm[4], ln_ffn2, w_out_bf, wr_t, br,
                                   bs, ts, n_tiles, n_tiles - 1, (h2, route, cnt))

    n_chunks, _, _, n_rows = _table_sizes(n_tiles)
    assert TOP_K * m * n_tiles >= EXPERT_UNIT_BLOCKS * EXPERT_ROWS
    dispatch_tab, combine_tab, toff_b, row0, nblk, limit = _route_tables(cnt[:, :, 0].astype(jnp.int32))
    xs_sorted = _dispatch(dispatch_tab, h2, route, toff_b, n_rows, n_chunks)
    ys_sorted = _experts(row0, nblk, limit, xs_sorted, w_gu, b_gu, w_dn, b_dn)
    yp = _combine(combine_tab, ys_sorted, route, toff_b, x1p, tp[5], 1, m, 0, n_chunks)
    ysm = _combine(combine_tab, ys_sorted, route, toff_b, x1s, tsm[5], bs, ts, n_tiles - 1, n_chunks)

    keep = k32p.shape[1]
    new = (k32p.reshape(bp, keep, nh, HEAD_DIM), v32p.reshape(bp, keep, nh, HEAD_DIM),
           tail_p[:, SUBLANES - (CONV_WIDTH - 1):, :], hl_p[:, 0, :],
           k32s.reshape(bs, ts, nh, HEAD_DIM), v32s.reshape(bs, ts, nh, HEAD_DIM),
           tail_s[:, SUBLANES - (CONV_WIDTH - 1):, :], hl_s[:, 0, :])
    return yp, ysm, new


def kernel(x_prompt, x_sample, c_prompt, c_sample, cache_k, cache_v, state_conv, state_lru, ln_mix_w, ln_ffn_w, w_ada, b_ada, w_in, q_norm_w, k_norm_w, rel_bias, conv_w, conv_b, w_rgate, b_rgate, w_igate, b_igate, lru_lambda, w_out, w_router, b_router, w_gate_up, b_gate_up, w_down, b_down):
    depth = w_in.shape[0]
    yp, ys = x_prompt, x_sample
    c_all = jnp.concatenate([c_prompt, c_sample], axis=0)
    collected = [[] for _ in range(8)]
    for l in range(depth):
        mod = _ada(c_all, w_ada[l], b_ada[l])
        lw = (ln_mix_w[l], ln_ffn_w[l], w_in[l], q_norm_w[l], k_norm_w[l], rel_bias[l], conv_w[l], conv_b[l],
              w_rgate[l], b_rgate[l], w_igate[l], b_igate[l], lru_lambda[l], w_out[l], w_router[l], b_router[l],
              w_gate_up[l], b_gate_up[l], w_down[l], b_down[l])
        yp, ys, new = _layer(yp, ys, mod, cache_k[l], cache_v[l], state_conv[l], state_lru[l], lw)
        for acc, val in zip(collected, new):
            acc.append(val)
    return (yp, ys) + tuple(jnp.stack(vals) for vals in collected)
```

```python
import functools

import jax
import jax.numpy as jnp
from jax import lax
from jax.experimental import pallas as pl
from jax.experimental.pallas import tpu as pltpu

F32 = jnp.float32
BF16 = jnp.bfloat16

CHUNK = 64
N_LEFT_CHUNKS = 8
ATTN_WINDOW = N_LEFT_CHUNKS * CHUNK
HEAD_DIM = 64
REL_CLIP = 128
CONV_WIDTH = 4
LRU_C = 8.0
N_EXPERTS = 32
TOP_K = 4
SWIGLU_LIMIT = 7.0
SWIGLU_ALPHA = 1.702
NORM_EPS = 1e-6
NEG_INF = -1e30

LANES = 128
SUBLANES = 8
BF16_ROWS = 16
MXU_DIM = 256

TOKEN_TILE = 512
ATTN_Q_TILE = 256
ATTN_STEP_BATCH = 4
EXPERT_ROWS = 256
EXPERT_UNIT_BLOCKS = 4
BIAS_TABLE = 1024
VMEM_LIMIT = 56 * 1024 * 1024


def _params(sem, vmem=VMEM_LIMIT):
    return pltpu.CompilerParams(dimension_semantics=sem, vmem_limit_bytes=vmem)


def _ada_kernel(c_ref, w_ref, b_ref, o_ref):
    c = c_ref[...]
    s = (c * jax.nn.sigmoid(c)).astype(BF16)
    o_ref[...] = jnp.dot(s, w_ref[...].astype(BF16), preferred_element_type=F32) + b_ref[...]


def _ada(c_all, w_ada, b_ada):
    n, d = c_all.shape
    nout = w_ada.shape[1]
    tn = 1024
    return pl.pallas_call(
        _ada_kernel,
        grid=(nout // tn,),
        in_specs=[pl.BlockSpec((n, d), lambda j: (0, 0)),
                  pl.BlockSpec((d, tn), lambda j: (0, j)),
                  pl.BlockSpec((1, tn), lambda j: (0, j))],
        out_specs=pl.BlockSpec((n, tn), lambda j: (0, j)),
        out_shape=jax.ShapeDtypeStruct((n, nout), F32),
        compiler_params=_params(("arbitrary",)),
        name="ada",
    )(c_all, w_ada, b_ada.reshape(1, nout))


def _mixin_kernel(x_ref, sh_ref, sc_ref, ln_ref, win_ref, qn_ref, kn_ref, bd_ref,
                  pre_ref, h0_ref, cw_ref, cb_ref, wa_ref, wx_ref, ba_ref, bx_ref, lam_ref,
                  q_ref, k_ref, v_ref, k32_ref, v32_ref, lru_ref, tail_ref, hl_ref, cx_ref, ch_ref):
    nb, tr, d = x_ref.shape
    m = nb * tr
    aw = q_ref.shape[-1]

    @pl.when(pl.program_id(1) == 0)
    def _():
        cx_ref[...] = pre_ref[...]
        ch_ref[...] = h0_ref[...]

    x = x_ref[...]
    ms = jnp.mean(x * x, axis=-1, keepdims=True)
    h = x * lax.rsqrt(ms + NORM_EPS) * ln_ref[...]
    h = h * (1.0 + sc_ref[...]) + sh_ref[...]
    hb = h.reshape(m, d).astype(BF16)

    def proj(part):
        return jnp.dot(hb, win_ref[:, part * aw:(part + 1) * aw], preferred_element_type=F32)

    def head_norm(t, w_ref):
        msq = jnp.dot((t * t).astype(BF16), bd_ref[...], preferred_element_type=F32)
        return t * lax.rsqrt(msq + NORM_EPS) * w_ref[...]

    lru_out, new_tail, h_last = _lru_branch(
        proj(3).reshape(nb, tr, aw), proj(4).reshape(nb, tr, aw), cw_ref, cb_ref,
        wa_ref, wx_ref, ba_ref, bx_ref, lam_ref, cx_ref, ch_ref)
    lru_ref[...] = lru_out
    tail_ref[...] = new_tail
    hl_ref[...] = h_last
    q = head_norm(proj(0), qn_ref)
    k = head_norm(proj(1), kn_ref)
    v = proj(2)
    q_ref[...] = (q * (HEAD_DIM ** -0.5)).astype(BF16).reshape(nb, tr, aw)
    k_ref[...] = k.astype(BF16).reshape(nb, tr, aw)
    v_ref[...] = v.astype(BF16).reshape(nb, tr, aw)
    k32_ref[...] = k.reshape(nb, tr, aw)
    v32_ref[...] = v.reshape(nb, tr, aw)


def _mixin(x, sh, sc, ln_w, w_in_bf, qn_t, kn_t, bd, pre, h0, conv_w, conv_b, wa_bd, wx_bd, b_a, b_x, lam,
           nb, tr):
    nbt, t, d = x.shape
    aw = qn_t.shape[-1]
    c = pre.shape[-1]
    assert c == aw
    keep = min(ATTN_WINDOW, t)
    assert tr == keep or t == tr
    grid = (nbt // nb, t // tr)
    xmap = lambda b, i: (b, i, 0)
    mmap = lambda b, i: (b, 0, 0)
    cmap = lambda b, i: (0, 0)
    cmap3 = lambda b, i: (0, 0, 0)
    tmap = lambda b, i: (b, 0, 0)
    big = pl.BlockSpec((nb, tr, aw), xmap)
    tail = pl.BlockSpec((nb, keep, aw), tmap)
    row = pl.BlockSpec((1, c), cmap)
    return pl.pallas_call(
        _mixin_kernel,
        grid=grid,
        in_specs=[pl.BlockSpec((nb, tr, d), xmap),
                  pl.BlockSpec((nb, 1, d), mmap), pl.BlockSpec((nb, 1, d), mmap),
                  pl.BlockSpec((1, d), cmap),
                  pl.BlockSpec(w_in_bf.shape, cmap),
                  pl.BlockSpec((1, aw), cmap), pl.BlockSpec((1, aw), cmap),
                  pl.BlockSpec(bd.shape, cmap),
                  pl.BlockSpec((nb, SUBLANES, c), tmap), pl.BlockSpec((nb, 1, c), tmap),
                  pl.BlockSpec(conv_w.shape, cmap), row,
                  pl.BlockSpec(wa_bd.shape, cmap3), pl.BlockSpec(wx_bd.shape, cmap3),
                  row, row, row],
        out_specs=[big, big, big, tail, tail, big,
                   pl.BlockSpec((nb, SUBLANES, c), tmap), pl.BlockSpec((nb, 1, c), tmap)],
        out_shape=[jax.ShapeDtypeStruct((nbt, t, aw), BF16)] * 3
        + [jax.ShapeDtypeStruct((nbt, keep, aw), F32)] * 2
        + [jax.ShapeDtypeStruct((nbt, t, c), BF16),
           jax.ShapeDtypeStruct((nbt, SUBLANES, c), F32),
           jax.ShapeDtypeStruct((nbt, 1, c), F32)],
        scratch_shapes=[pltpu.VMEM((nb, SUBLANES, c), F32), pltpu.VMEM((nb, 1, c), F32)],
        compiler_params=_params(("arbitrary", "arbitrary")),
        name="mixin",
    )(x, sh, sc, ln_w, w_in_bf, qn_t, kn_t, bd, pre, h0, conv_w, conv_b, wa_bd, wx_bd, b_a, b_x, lam)


def _bias_table(rel_bias, off):
    h = rel_bias.shape[0]
    left = off - REL_CLIP
    right = BIAS_TABLE - left - (2 * REL_CLIP + 1)
    assert left >= 0 and right >= 0
    return jnp.concatenate([jnp.broadcast_to(rel_bias[:, :1], (h, left)), rel_bias,
                            jnp.broadcast_to(rel_bias[:, -1:], (h, right))], axis=1)


def _toeplitz(tab_row, rows, cols):
    t = jnp.broadcast_to(tab_row, (rows, BIAS_TABLE))
    t = pltpu.roll(t, BIAS_TABLE - (rows - 1), 1, stride=1, stride_axis=0)
    return t[:, :cols]


def _attn_kernel(q_ref, k0_ref, k1_ref, k2_ref, v0_ref, v1_ref, v2_ref, tab_ref, o_ref, bias_ref):
    b = pl.program_id(0)
    s = pl.program_id(1)
    qt = q_ref.shape[1]
    nk = 3 * qt
    nh = bias_ref.shape[0]

    @pl.when((b == 0) & (s == 0))
    def _():
        qi = lax.broadcasted_iota(jnp.int32, (qt, nk), 0) // CHUNK
        kc = lax.broadcasted_iota(jnp.int32, (qt, nk), 1) // CHUNK
        for h in range(nh):
            band = jnp.where(kc <= qi + N_LEFT_CHUNKS, _toeplitz(tab_ref[h:h + 1, :], qt, nk), NEG_INF)
            bias_ref[h] = jnp.where(kc >= qi, band, NEG_INF)

    pair_w = 2 * HEAD_DIM

    def attend(mask_start):
        q = q_ref[0]
        kcat = jnp.concatenate([k0_ref[0], k1_ref[0], k2_ref[0]], axis=0)
        vcat = jnp.concatenate([v0_ref[0], v1_ref[0], v2_ref[0]], axis=0)
        first = lax.broadcasted_iota(jnp.int32, (qt, pair_w), 1) < HEAD_DIM
        keep = [jnp.where(first, 1.0, 0.0).astype(BF16), jnp.where(first, 0.0, 1.0).astype(BF16)]
        if mask_start:
            in_seq = lax.broadcasted_iota(jnp.int32, (qt, nk), 1) >= (2 - s) * qt
        outs = []
        for pair in range(nh // 2):
            sl = slice(pair * pair_w, (pair + 1) * pair_w)
            q2, k2, v2 = q[:, sl], kcat[:, sl], vcat[:, sl]
            per_head = []
            for sub in range(2):
                sc = lax.dot_general(q2 * keep[sub], k2, (((1,), (1,)), ((), ())), preferred_element_type=F32)
                sc = sc + bias_ref[2 * pair + sub]
                if mask_start:
                    sc = jnp.where(in_seq, sc, NEG_INF)
                mx = jnp.max(sc, axis=-1, keepdims=True)
                p = jnp.exp(sc - mx)
                l = jnp.sum(p, axis=-1, keepdims=True)
                per_head.append(jnp.dot(p.astype(BF16), v2, preferred_element_type=F32) / l)
            outs.append(jnp.where(first, per_head[0], per_head[1]))
        o_ref[0] = jnp.concatenate(outs, axis=-1).astype(BF16)

    @pl.when(s < 2)
    def _():
        attend(True)

    @pl.when(s >= 2)
    def _():
        attend(False)


def _attn_prompt(q, k, v, tab):
    b, s, aw = q.shape
    qt = ATTN_Q_TILE
    nh = aw // HEAD_DIM
    qspec = pl.BlockSpec((1, qt, aw), lambda i, j: (i, j, 0))

    def kspec(back):
        return pl.BlockSpec((1, qt, aw), lambda i, j: (i, jnp.maximum(j - back, 0), 0))

    return pl.pallas_call(
        _attn_kernel,
        grid=(b, s // qt),
        in_specs=[qspec, kspec(2), kspec(1), kspec(0), kspec(2), kspec(1), kspec(0),
                  pl.BlockSpec(tab.shape, lambda i, j: (0, 0))],
        out_specs=qspec,
        out_shape=jax.ShapeDtypeStruct((b, s, aw), BF16),
        scratch_shapes=[pltpu.VMEM((nh, qt, 3 * qt), F32)],
        compiler_params=_params(("arbitrary", "arbitrary")),
        name="attn_prompt",
    )(q, k, k, k, v, v, v, tab)


def _attn_step_kernel(q_ref, kn_ref, vn_ref, ck_ref, cv_ref, tab_ref, o_ref, bias_ref):
    step = pl.program_id(0)
    nbs, t, aw = q_ref.shape
    nh = aw // HEAD_DIM
    rows = nh * t
    r = ck_ref.shape[-1]
    nk = r + LANES
    nt_dims = (((1,), (1,)), ((), ()))

    @pl.when(step == 0)
    def _():
        ok = lax.broadcasted_iota(jnp.int32, (t, nk), 1) < r + t
        for h in range(nh):
            bias_ref[h * t:(h + 1) * t, :] = jnp.where(ok, _toeplitz(tab_ref[h:h + 1, :], t, nk), NEG_INF)

    own = (lax.broadcasted_iota(jnp.int32, (rows, aw), 0) // t
           == lax.broadcasted_iota(jnp.int32, (rows, aw), 1) // HEAD_DIM)
    own_f = jnp.where(own, 1.0, 0.0)
    own_bf = own_f.astype(BF16)
    pad = jnp.zeros((LANES - t, aw), BF16)
    for b in range(nbs):
        q_bd = jnp.concatenate([q_ref[b]] * nh, axis=0) * own_bf
        k_old = ck_ref[b].reshape(aw, r).astype(BF16)
        v_old = cv_ref[b].reshape(aw, r).astype(BF16)
        k_new = jnp.concatenate([kn_ref[b], pad], axis=0)
        v_new = jnp.concatenate([vn_ref[b], pad], axis=0)
        s_old = jnp.dot(q_bd, k_old, preferred_element_type=F32) + bias_ref[:, :r]
        s_new = lax.dot_general(q_bd, k_new, nt_dims, preferred_element_type=F32) + bias_ref[:, r:]
        mx = jnp.maximum(jnp.max(s_old, axis=-1, keepdims=True), jnp.max(s_new, axis=-1, keepdims=True))
        p_old = jnp.exp(s_old - mx)
        p_new = jnp.exp(s_new - mx)
        l = jnp.sum(p_old, axis=-1, keepdims=True) + jnp.sum(p_new, axis=-1, keepdims=True)
        o_all = (lax.dot_general(p_old.astype(BF16), v_old, nt_dims, preferred_element_type=F32)
                 + jnp.dot(p_new.astype(BF16), v_new, preferred_element_type=F32))
        o_all = o_all * own_f / l
        out = o_all[0:t]
        for h in range(1, nh):
            out = out + o_all[h * t:(h + 1) * t]
        o_ref[b] = out.astype(BF16)


def _attn_step(q, kn, vn, ck, cv, tab, nbs):
    b, t, aw = q.shape
    nh = aw // HEAD_DIM
    r = ck.shape[-1]
    new = pl.BlockSpec((nbs, t, aw), lambda i: (i, 0, 0))
    old = pl.BlockSpec((nbs, nh, HEAD_DIM, r), lambda i: (i, 0, 0, 0))
    return pl.pallas_call(
        _attn_step_kernel,
        grid=(b // nbs,),
        in_specs=[new, new, new, old, old, pl.BlockSpec(tab.shape, lambda i: (0, 0))],
        out_specs=new,
        out_shape=jax.ShapeDtypeStruct((b, t, aw), BF16),
        scratch_shapes=[pltpu.VMEM((nh * t, r + LANES), F32)],
        compiler_params=_params(("arbitrary",)),
        name="attn_step",
    )(q, kn, vn, ck, cv, tab)


def _gelu_tanh(x):
    return x * (0.5 * (1.0 + jnp.tanh(0.7978845608028654 * (x + 0.044715 * (x * x * x)))))


def _lru_branch(x, yg, cw_ref, cb_ref, wa_ref, wx_ref, ba_ref, bx_ref, lam_ref, cx_ref, ch_ref):
    nb, tr, c = x.shape
    m = nb * tr
    half = c // 2
    xp = jnp.concatenate([cx_ref[...], x], axis=1)
    new_tail = xp[:, tr:tr + SUBLANES, :]
    xp2 = xp.reshape(nb * (tr + SUBLANES), c)
    y = cb_ref[...] + cw_ref[CONV_WIDTH - 1:CONV_WIDTH, :] * x
    for back in range(1, CONV_WIDTH):
        shifted = pltpu.roll(xp2, back, 0).reshape(nb, tr + SUBLANES, c)[:, SUBLANES:, :]
        y = y + cw_ref[CONV_WIDTH - 1 - back:CONV_WIDTH - back, :] * shifted
    y2 = y.reshape(m, c)
    yb = y2.astype(BF16)

    def gate(w_ref, b_ref):
        g = jnp.concatenate(
            [jnp.dot(yb[:, :half], w_ref[0], preferred_element_type=F32),
             jnp.dot(yb[:, half:], w_ref[1], preferred_element_type=F32)], axis=1)
        return jax.nn.sigmoid(g + b_ref[...])

    rg = gate(wa_ref, ba_ref)
    ig = gate(wx_ref, bx_ref)
    lam = lam_ref[...]
    log_sig = jnp.minimum(lam, 0.0) - jnp.log1p(jnp.exp(-jnp.abs(lam)))
    log_a = LRU_C * rg * log_sig
    a_cum = jnp.exp(log_a)
    b_cum = jnp.sqrt(-jnp.tanh(log_a) * (a_cum * a_cum + 1.0)) * (ig * y2)
    row = lax.broadcasted_iota(jnp.int32, (m, c), 0) % SUBLANES
    dist = 1
    while dist < SUBLANES:
        keep = row >= dist
        a_sh = jnp.where(keep, pltpu.roll(a_cum, dist, 0), 1.0)
        b_sh = jnp.where(keep, pltpu.roll(b_cum, dist, 0), 0.0)
        b_cum = a_cum * b_sh + b_cum
        a_cum = a_cum * a_sh
        dist *= 2
    groups = tr // SUBLANES
    a_grp = a_cum.reshape(nb, groups, SUBLANES, c)
    b_grp = b_cum.reshape(nb, groups, SUBLANES, c)
    carry = ch_ref[...]
    pieces = []
    for grp in range(groups):
        h_grp = a_grp[:, grp] * carry + b_grp[:, grp]
        carry = h_grp[:, SUBLANES - 1:SUBLANES, :]
        pieces.append(h_grp)
    h = jnp.concatenate(pieces, axis=1)
    ch_ref[...] = carry
    cx_ref[...] = new_tail
    return (h * _gelu_tanh(yg)).astype(BF16), new_tail, carry


def _outproj_kernel(*refs, aliased):
    (at_ref, lr_ref, x_ref, gm_ref, shf_ref, scf_ref, lnf_ref, wo_ref, wr_ref, br_ref) = refs[:10]
    x1_ref, h2_ref, route_ref, cnt_ref = refs[10 + aliased:]
    nb, tr, d = x_ref.shape
    m = nb * tr
    aw = at_ref.shape[-1]
    ne = wr_ref.shape[0]
    at = at_ref[...].reshape(m, aw)
    lr = lr_ref[...].reshape(m, aw)
    mix = (jnp.dot(at, wo_ref[0:aw, :], preferred_element_type=F32)
           + jnp.dot(lr, wo_ref[aw:2 * aw, :], preferred_element_type=F32))
    x1 = x_ref[...] + gm_ref[...] * mix.reshape(nb, tr, d)
    x1_ref[...] = x1
    ms = jnp.mean(x1 * x1, axis=-1, keepdims=True)
    h2 = x1 * lax.rsqrt(ms + NORM_EPS) * lnf_ref[...]
    h2 = (h2 * (1.0 + scf_ref[...]) + shf_ref[...]).reshape(m, d)
    h2_ref[...] = h2.astype(BF16)

    logits = lax.dot_general(wr_ref[...], h2.astype(BF16), (((1,), (1,)), ((), ())),
                             preferred_element_type=F32) + br_ref[...]
    e_iota = lax.broadcasted_iota(jnp.int32, (ne, m), 0).astype(F32)
    vals = logits
    top_v, sels = [], []
    for k in range(TOP_K):
        mx = jnp.max(vals, axis=0, keepdims=True)
        idx = jnp.min(jnp.where(vals == mx, e_iota, float(ne)), axis=0, keepdims=True)
        sel = e_iota == idx
        vals = jnp.where(sel, -jnp.inf, vals)
        top_v.append(mx)
        sels.append(sel)
        route_ref[0, k:k + 1, :] = idx
    ex = [jnp.exp(v - top_v[0]) for v in top_v]
    den = ex[0] + ex[1] + ex[2] + ex[3]
    chosen = jnp.zeros((ne, m), F32)
    for k in range(TOP_K):
        route_ref[0, 2 * TOP_K + k:2 * TOP_K + k + 1, :] = ex[k] / den
        chosen = chosen + jnp.where(sels[k], 1.0, 0.0)
    before = (lax.broadcasted_iota(jnp.int32, (m, m), 0) < lax.broadcasted_iota(jnp.int32, (m, m), 1))
    rank = jnp.dot(chosen.astype(BF16), jnp.where(before, 1.0, 0.0).astype(BF16), preferred_element_type=F32)
    for k in range(TOP_K):
        route_ref[0, TOP_K + k:TOP_K + k + 1, :] = jnp.sum(jnp.where(sels[k], rank, 0.0), axis=0, keepdims=True)
    route_ref[0, 3 * TOP_K:4 * TOP_K, :] = jnp.zeros((TOP_K, m), F32)
    cnt_ref[0] = jnp.broadcast_to(jnp.sum(chosen, axis=1, keepdims=True), (ne, LANES))


def _outproj(attn, lru_o, x, gm, shf, scf, lnf, w_out_bf, wr_t, br, nb, tr, n_tiles, tile0, prev):
    nbt, t, d = x.shape
    aw = attn.shape[-1]
    m = nb * tr
    assert m == TOKEN_TILE
    ne = wr_t.shape[0]
    tiles_per_seq = t // tr
    xmap = lambda b, i: (b, i, 0)
    mmap = lambda b, i: (b, 0, 0)
    c2 = lambda b, i: (0, 0)
    tile = lambda b, i: (tile0 + b * tiles_per_seq + i, 0)
    tile3 = lambda b, i: (tile0 + b * tiles_per_seq + i, 0, 0)
    mod = pl.BlockSpec((nb, 1, d), mmap)
    in_specs = [pl.BlockSpec((nb, tr, aw), xmap), pl.BlockSpec((nb, tr, aw), xmap),
                pl.BlockSpec((nb, tr, d), xmap), mod, mod, mod,
                pl.BlockSpec((1, d), c2), pl.BlockSpec(w_out_bf.shape, c2),
                pl.BlockSpec(wr_t.shape, c2), pl.BlockSpec((ne, 1), c2)]
    args = [attn, lru_o, x, gm, shf, scf, lnf, w_out_bf, wr_t, br]
    aliases = {}
    if prev is not None:
        in_specs += [pl.BlockSpec(memory_space=pl.ANY)] * 3
        args += list(prev)
        aliases = {10: 1, 11: 2, 12: 3}
    return pl.pallas_call(
        functools.partial(_outproj_kernel, aliased=len(aliases)),
        grid=(nbt // nb, tiles_per_seq),
        in_specs=in_specs,
        out_specs=[pl.BlockSpec((nb, tr, d), xmap), pl.BlockSpec((m, d), tile),
                   pl.BlockSpec((1, 4 * TOP_K, m), tile3), pl.BlockSpec((1, ne, LANES), tile3)],
        out_shape=[jax.ShapeDtypeStruct((nbt, t, d), F32),
                   jax.ShapeDtypeStruct((n_tiles * m, d), BF16),
                   jax.ShapeDtypeStruct((n_tiles, 4 * TOP_K, m), F32),
                   jax.ShapeDtypeStruct((n_tiles, ne, LANES), F32)],
        input_output_aliases=aliases,
        compiler_params=_params(("arbitrary", "arbitrary")),
        name="outproj",
    )(*args)


def _tile_rows(m):
    cap = TOP_K * m + N_EXPERTS * (BF16_ROWS - 1) + BF16_ROWS
    return -(-cap // TOKEN_TILE) * TOKEN_TILE


def _table_sizes(nt):
    g = BF16_ROWS
    m = TOKEN_TILE
    n_chunks = _tile_rows(m) // g
    n_gap = -(-(N_EXPERTS * (EXPERT_ROWS // g - 1)) // nt)
    bound = TOP_K * m * nt + nt * N_EXPERTS * (g - 1) + N_EXPERTS * (EXPERT_ROWS - g)
    n_sorted = -(-bound // EXPERT_ROWS) * EXPERT_ROWS
    return n_chunks, n_gap, n_sorted, n_sorted + 2 * (n_chunks + n_gap) * g


def _route_tables(cnt):
    nt = cnt.shape[0]
    g = BF16_ROWS
    bm = EXPERT_ROWS
    n_chunks, n_gap, n_sorted, _ = _table_sizes(nt)
    e_ids = jnp.arange(N_EXPERTS, dtype=jnp.int32)
    t_ids = jnp.arange(nt, dtype=jnp.int32)
    upto = (e_ids[:, None] <= e_ids[None, :]).astype(jnp.int32)
    pc = (cnt + g - 1) // g * g
    ctile = jnp.sum(pc[:, :, None] * upto[None], axis=1)
    toff = ctile - pc
    trow = ctile[:, -1]
    tot = jnp.sum(pc, axis=0)
    reg = (tot + bm - 1) // bm * bm
    creg = jnp.sum(reg[:, None] * upto, axis=0)
    base = creg - reg
    earlier = (t_ids[:, None] < t_ids[None, :]).astype(jnp.int32)
    goff = base[None, :] + jnp.sum(pc[:, None, :] * earlier[:, :, None], axis=0)
    r = jnp.arange(n_chunks, dtype=jnp.int32) * g
    r3 = r[None, :, None]
    in_seg = (toff[:, None, :] <= r3) & (r3 < ctile[:, None, :])
    dst = jnp.sum(jnp.where(in_seg, (goff - toff)[:, None, :], 0), axis=2) + r[None, :]
    dst = jnp.where(r[None, :] < trow[:, None], dst, -1)
    gcnt = (reg - tot) // g
    gcum = jnp.sum(gcnt[:, None] * upto, axis=0)
    gstart = gcum - gcnt
    s = jnp.arange(nt * n_gap, dtype=jnp.int32)
    in_gap = (gstart[None, :] <= s[:, None]) & (s[:, None] < gcum[None, :])
    gdst = jnp.sum(jnp.where(in_gap, (base + tot - g * gstart)[None, :] + g * s[:, None], 0), axis=1)
    gdst = jnp.where(s < gcum[-1], gdst, -1).reshape(nt, n_gap)
    table = jnp.concatenate([dst, gdst], axis=1).astype(jnp.int32)
    n_entries = n_chunks + n_gap
    spare = n_sorted + ((t_ids % 2)[:, None] * n_entries + jnp.arange(n_entries, dtype=jnp.int32)[None, :]) * g
    dispatch_tab = jnp.where(table >= 0, table, spare).astype(jnp.int32)
    combine_tab = jnp.maximum(dst, 0).astype(jnp.int32)
    toff_b = jnp.broadcast_to(toff.astype(F32)[:, :, None], (nt, N_EXPERTS, LANES))
    limit = (creg[-1] - EXPERT_UNIT_BLOCKS * bm).astype(jnp.int32).reshape(1)
    return dispatch_tab, combine_tab, toff_b, base.astype(jnp.int32), (reg // bm).astype(jnp.int32), limit


def _slot_rows(route_ref, toff_ref, m):
    ne = toff_ref.shape[1]
    e_iota = lax.broadcasted_iota(jnp.int32, (ne, m), 0).astype(F32)
    toff_col = toff_ref[0][:, 0:1]
    pos = []
    for k in range(TOP_K):
        sel = e_iota == route_ref[0, k:k + 1, :]
        start = jnp.sum(jnp.where(sel, toff_col, 0.0), axis=0, keepdims=True)
        pos.append(start + route_ref[0, TOP_K + k:TOP_K + k + 1, :])
    return pos


def _dispatch_kernel(tab_ref, h2_ref, route_ref, toff_ref, xs_hbm, buf_ref, sem, *, n_chunks, n_tiles):
    t = pl.program_id(0)
    slot = t % 2
    m = h2_ref.shape[0]
    rows = buf_ref.shape[1]
    n_entries = tab_ref.shape[1]
    g = BF16_ROWS
    per_chunk = m // g

    def start(c):
        src = c * g if c < n_chunks else rows - g
        pltpu.make_async_copy(
            buf_ref.at[slot, pl.ds(src, g)],
            xs_hbm.at[pl.ds(pl.multiple_of(tab_ref[t, c], g), g)], sem.at[slot]).start()

    def wait_all(which):
        for _ in range(n_entries):
            pltpu.make_async_copy(buf_ref.at[which, pl.ds(0, g)], xs_hbm.at[pl.ds(0, g)], sem.at[which]).wait()

    @pl.when(t >= 2)
    def _():
        wait_all(slot)

    pos = _slot_rows(route_ref, toff_ref, m)
    h2 = h2_ref[...]
    for rc in range(rows // m):
        r_iota = (lax.broadcasted_iota(jnp.int32, (m, m), 0) + rc * m).astype(F32)
        onehot = jnp.zeros((m, m), F32)
        for k in range(TOP_K):
            onehot = jnp.where(r_iota == pos[k], 1.0, onehot)
        buf_ref[slot, rc * m:(rc + 1) * m, :] = jnp.dot(
            onehot.astype(BF16), h2, preferred_element_type=F32).astype(BF16)
        for c in range((rc - 1) * per_chunk, rc * per_chunk) if rc > 0 else ():
            start(c)
    for c in range(n_chunks - per_chunk, n_entries):
        start(c)

    @pl.when(t == n_tiles - 1)
    def _():
        if n_tiles > 1:
            wait_all(1 - slot)
        wait_all(slot)


def _dispatch(table, h2, route, toff_b, n_rows, n_chunks):
    nt = route.shape[0]
    m = TOKEN_TILE
    d = h2.shape[1]
    rows = _tile_rows(m)
    grid_spec = pltpu.PrefetchScalarGridSpec(
        num_scalar_prefetch=1,
        grid=(nt,),
        in_specs=[pl.BlockSpec((m, d), lambda t, tab: (t, 0)),
                  pl.BlockSpec((1, 4 * TOP_K, m), lambda t, tab: (t, 0, 0)),
                  pl.BlockSpec((1, N_EXPERTS, LANES), lambda t, tab: (t, 0, 0))],
        out_specs=pl.BlockSpec(memory_space=pl.ANY),
        scratch_shapes=[pltpu.VMEM((2, rows, d), BF16), pltpu.SemaphoreType.DMA((2,))],
    )
    return pl.pallas_call(
        functools.partial(_dispatch_kernel, n_chunks=n_chunks, n_tiles=nt),
        grid_spec=grid_spec,
        out_shape=jax.ShapeDtypeStruct((n_rows, d), BF16),
        compiler_params=_params(("arbitrary",)),
        name="dispatch",
    )(table, h2, route, toff_b)


def _expert_kernel(row0_ref, nblk_ref, limit_ref, xs_hbm, wgu_ref, bgu_ref, wdn_ref, bdn_ref, ys_hbm,
                   wgu_bf, wdn_bf, xbuf, ybuf, sem_in, sem_out):
    e = pl.program_id(0)
    ne = pl.num_programs(0)
    bm = EXPERT_ROWS
    unit = xbuf.shape[1]
    per_unit = unit // bm
    dff = wdn_ref.shape[1]
    nblk = nblk_ref[e]
    n_units = (nblk + per_unit - 1) // per_unit

    def unit_start(expert, s):
        true = row0_ref[expert] + s * unit
        start = jnp.minimum(true, limit_ref[0])
        return pl.multiple_of(start, bm), pl.multiple_of(true - start, bm)

    def in_copy(expert, s, slot):
        start, _ = unit_start(expert, s)
        return pltpu.make_async_copy(xs_hbm.at[pl.ds(start, unit)], xbuf.at[slot], sem_in.at[slot])

    def out_copy(j, slot, i):
        start = pl.multiple_of(row0_ref[e] + j * bm, bm)
        return pltpu.make_async_copy(ybuf.at[slot, pl.ds(i * bm, bm)], ys_hbm.at[pl.ds(start, bm)],
                                     sem_out.at[slot])

    @pl.when((e == 0) & (nblk > 0))
    def _():
        in_copy(e, 0, 0).start()

    wgu_bf[...] = wgu_ref[0].astype(BF16)
    wdn_bf[...] = wdn_ref[0].astype(BF16)

    def run_unit(s, carry):
        slot = s % 2
        in_copy(e, s, slot).wait()

        @pl.when(s + 1 < n_units)
        def _():
            in_copy(e, s + 1, 1 - slot).start()

        @pl.when(s >= 2)
        def _():
            for i in range(per_unit):
                out_copy(0, slot, i).wait()

        _, lead = unit_start(e, s)

        def ffn(first, count):
            rows = count * bm
            x = xbuf[slot, pl.ds(pl.multiple_of(lead + first * bm, bm), rows), :]
            gu = jnp.dot(x, wgu_bf[...], preferred_element_type=F32) + bgu_ref[0]
            gate = jnp.minimum(gu[:, :dff], SWIGLU_LIMIT)
            up = jnp.clip(gu[:, dff:], -SWIGLU_LIMIT, SWIGLU_LIMIT)
            glu = gate * jax.nn.sigmoid(gate * SWIGLU_ALPHA)
            act = ((up + 1.0) * glu).astype(BF16)
            ybuf[slot, first * bm:first * bm + rows, :] = (
                jnp.dot(act, wdn_bf[...], preferred_element_type=F32) + bdn_ref[0]).astype(BF16)
            for i in range(first, first + count):
                out_copy(s * per_unit + i, slot, i).start()

        for first in range(0, per_unit, 2):
            have = nblk - s * per_unit - first

            @pl.when(have >= 2)
            def _():
                ffn(first, 2)

            @pl.when(have == 1)
            def _():
                ffn(first, 1)
        return carry

    lax.fori_loop(0, n_units, run_unit, 0)

    nxt = jnp.minimum(e + 1, ne - 1)

    @pl.when((e + 1 < ne) & (nblk_ref[nxt] > 0))
    def _():
        in_copy(nxt, 0, 0).start()

    @pl.when(n_units >= 2)
    def _():
        for i in range(per_unit):
            out_copy(0, n_units % 2, i).wait()

    in_last = nblk - (n_units - 1) * per_unit
    for i in range(per_unit):
        @pl.when((n_units >= 1) & (i < in_last))
        def _():
            out_copy(0, (n_units - 1) % 2, i).wait()


def _experts(row0, nblk, limit, xs, w_gu, b_gu, w_dn, b_dn):
    n_rows, d = xs.shape
    bm = EXPERT_ROWS
    unit = EXPERT_UNIT_BLOCKS * bm
    ne, _, dff2 = w_gu.shape
    dff = w_dn.shape[1]
    exp3 = lambda e, r0, nb, lim: (e, 0, 0)
    grid_spec = pltpu.PrefetchScalarGridSpec(
        num_scalar_prefetch=3,
        grid=(ne,),
        in_specs=[pl.BlockSpec(memory_space=pl.ANY),
                  pl.BlockSpec((1, d, dff2), exp3), pl.BlockSpec((1, 1, dff2), exp3),
                  pl.BlockSpec((1, dff, d), exp3), pl.BlockSpec((1, 1, d), exp3)],
        out_specs=pl.BlockSpec(memory_space=pl.ANY),
        scratch_shapes=[pltpu.VMEM((d, dff2), BF16), pltpu.VMEM((dff, d), BF16),
                        pltpu.VMEM((2, unit, d), BF16), pltpu.VMEM((2, unit, d), BF16),
                        pltpu.SemaphoreType.DMA((2,)), pltpu.SemaphoreType.DMA((2,))],
    )
    return pl.pallas_call(
        _expert_kernel,
        grid_spec=grid_spec,
        out_shape=jax.ShapeDtypeStruct((n_rows, d), BF16),
        compiler_params=_params(("arbitrary",)),
        name="experts",
    )(row0, nblk, limit, xs, w_gu, b_gu.reshape(ne, 1, dff2), w_dn, b_dn.reshape(ne, 1, d))


def _combine_kernel(tab_ref, ys_hbm, route_ref, toff_ref, x1_ref, gf_ref, o_ref, buf_ref, sem, *, n_chunks, tile0):
    step = pl.program_id(0)
    n_steps = pl.num_programs(0)
    t = tile0 + step
    slot = step % 2
    nb, tr, d = x1_ref.shape
    m = nb * tr
    rows = buf_ref.shape[1]
    g = BF16_ROWS

    def fetch(tile, which):
        for c in range(n_chunks):
            pltpu.make_async_copy(
                ys_hbm.at[pl.ds(pl.multiple_of(tab_ref[tile, c], g), g)],
                buf_ref.at[which, pl.ds(c * g, g)], sem.at[which]).start()

    @pl.when(step == 0)
    def _():
        fetch(t, slot)

    @pl.when(step + 1 < n_steps)
    def _():
        fetch(t + 1, 1 - slot)

    pos = _slot_rows(route_ref, toff_ref, m)
    gates = [route_ref[0, 2 * TOP_K + k:2 * TOP_K + k + 1, :] for k in range(TOP_K)]
    stacked = jnp.concatenate(pos + gates + [jnp.zeros((LANES - 2 * TOP_K, m), F32)], axis=0)
    cols = stacked.T
    for _ in range(n_chunks):
        pltpu.make_async_copy(ys_hbm.at[pl.ds(0, g)], buf_ref.at[slot, pl.ds(0, g)], sem.at[slot]).wait()

    acc = jnp.zeros((m, d), F32)
    for rc in range(rows // m):
        c_iota = (lax.broadcasted_iota(jnp.int32, (m, m), 1) + rc * m).astype(F32)
        weights = jnp.zeros((m, m), F32)
        for k in range(TOP_K):
            weights = jnp.where(c_iota == cols[:, k:k + 1], cols[:, TOP_K + k:TOP_K + k + 1], weights)
        acc = acc + jnp.dot(weights.astype(BF16), buf_ref[slot, rc * m:(rc + 1) * m, :],
                            preferred_element_type=F32)
    o_ref[...] = x1_ref[...] + gf_ref[...] * acc.reshape(nb, tr, d)


def _combine(table, ys, route, toff_b, x1, gf, nb, tr, tile0, n_chunks):
    nbt, t, d = x1.shape
    m = nb * tr
    assert m == TOKEN_TILE
    rows = _tile_rows(m)
    tiles_per_seq = t // tr
    n_steps = (nbt // nb) * tiles_per_seq
    xmap = lambda s, tab: (s // tiles_per_seq, s % tiles_per_seq, 0)
    grid_spec = pltpu.PrefetchScalarGridSpec(
        num_scalar_prefetch=1,
        grid=(n_steps,),
        in_specs=[pl.BlockSpec(memory_space=pl.ANY),
                  pl.BlockSpec((1, 4 * TOP_K, m), lambda s, tab: (tile0 + s, 0, 0)),
                  pl.BlockSpec((1, N_EXPERTS, LANES), lambda s, tab: (tile0 + s, 0, 0)),
                  pl.BlockSpec((nb, tr, d), xmap),
                  pl.BlockSpec((nb, 1, d), lambda s, tab: (s // tiles_per_seq, 0, 0))],
        out_specs=pl.BlockSpec((nb, tr, d), xmap),
        scratch_shapes=[pltpu.VMEM((2, rows, d), BF16), pltpu.SemaphoreType.DMA((2,))],
    )
    return pl.pallas_call(
        functools.partial(_combine_kernel, n_chunks=n_chunks, tile0=tile0),
        grid_spec=grid_spec,
        out_shape=jax.ShapeDtypeStruct((nbt, t, d), F32),
        compiler_params=_params(("arbitrary",)),
        name="combine",
    )(table, ys, route, toff_b, x1, gf)


def _block_diag(w, groups):
    n, k, _ = w.shape
    w = w.reshape(n // groups, groups, k, k)
    eye = jnp.eye(groups, dtype=w.dtype)
    return jnp.einsum("ngij,gh->ngihj", w, eye).reshape(n // groups, groups * k, groups * k)


def _layer(xp, xs, mod, k_cache, v_cache, conv_state, lru_state, lw):
    (ln_mix, ln_ffn, w_in, q_norm, k_norm, rel_bias, conv_w, conv_b, w_rg, b_rg, w_ig, b_ig, lam,
     w_out, w_router, b_router, w_gu, b_gu, w_dn, b_dn) = lw
    bp, s, d = xp.shape
    bs, ts, _ = xs.shape
    aw = w_out.shape[0] // 2
    nh = aw // HEAD_DIM
    m = TOKEN_TILE
    assert s % m == 0 and bs * ts == m and s % ATTN_Q_TILE == 0

    terms = [mod[:, i * d:(i + 1) * d][:, None, :] for i in range(6)]
    tp = [t[:bp] for t in terms]
    tsm = [t[bp:] for t in terms]

    w_in_bf = w_in.astype(BF16)
    w_out_bf = w_out.astype(BF16)
    qn_t = jnp.tile(q_norm, nh).reshape(1, aw)
    kn_t = jnp.tile(k_norm, nh).reshape(1, aw)
    head_mean = _block_diag(jnp.full((nh, HEAD_DIM, HEAD_DIM), 1.0 / HEAD_DIM, F32), nh)[0].astype(BF16)
    groups = MXU_DIM // w_rg.shape[-1]
    wa_bd = _block_diag(w_rg, groups).astype(BF16)
    wx_bd = _block_diag(w_ig, groups).astype(BF16)
    lw_c = b_rg.size
    b_a = b_rg.reshape(1, lw_c)
    b_x = b_ig.reshape(1, lw_c)
    lam2 = lam.reshape(1, lw_c)
    cb2 = conv_b.reshape(1, lw_c)
    ln_mix2 = ln_mix.reshape(1, d)
    ln_ffn2 = ln_ffn.reshape(1, d)
    wr_t = w_router.T.astype(BF16)
    br = b_router.reshape(-1, 1)
    tab_p = _bias_table(rel_bias, 3 * ATTN_Q_TILE - 1)
    r_cache = k_cache.shape[1]
    tab_s = _bias_table(rel_bias, r_cache + ts - 1)

    zeros_pre = jnp.zeros((bp, SUBLANES, lw_c), F32)
    zeros_h = jnp.zeros((bp, 1, lw_c), F32)
    pre_s = jnp.pad(conv_state, ((0, 0), (SUBLANES - (CONV_WIDTH - 1), 0), (0, 0)))
    lru_w = (conv_w, cb2, wa_bd, wx_bd, b_a, b_x, lam2)
    qp, kp, vp, k32p, v32p, lru_p, tail_p, hl_p = _mixin(
        xp, tp[0], tp[1], ln_mix2, w_in_bf, qn_t, kn_t, head_mean, zeros_pre, zeros_h, *lru_w, 1, m)
    qs, ks, vs, k32s, v32s, lru_s, tail_s, hl_s = _mixin(
        xs, tsm[0], tsm[1], ln_mix2, w_in_bf, qn_t, kn_t, head_mean, pre_s, lru_state[:, None, :], *lru_w, bs, ts)
    attn_p = _attn_prompt(qp, kp, vp, tab_p)
    attn_s = _attn_step(qs, ks, vs, jnp.transpose(k_cache, (0, 2, 3, 1)), jnp.transpose(v_cache, (0, 2, 3, 1)),
                        tab_s, ATTN_STEP_BATCH)

    n_tiles = bp * (s // m) + 1
    x1p, h2, route, cnt = _outproj(attn_p, lru_p, xp, tp[2], tp[3], tp[4], ln_ffn2, w_out_bf, wr_t, br,
                                   1, m, n_tiles, 0, None)
    x1s, h2, route, cnt = _outproj(attn_s, lru_s, xs, tsm[2], tsm[3], tsm[4], ln_ffn2, w_out_bf, wr_t, br,
                                   bs, ts, n_tiles, n_tiles - 1, (h2, route, cnt))

    n_chunks, _, _, n_rows = _table_sizes(n_tiles)
    assert TOP_K * m * n_tiles >= EXPERT_UNIT_BLOCKS * EXPERT_ROWS
    dispatch_tab, combine_tab, toff_b, row0, nblk, limit = _route_tables(cnt[:, :, 0].astype(jnp.int32))
    xs_sorted = _dispatch(dispatch_tab, h2, route, toff_b, n_rows, n_chunks)
    ys_sorted = _experts(row0, nblk, limit, xs_sorted, w_gu, b_gu, w_dn, b_dn)
    yp = _combine(combine_tab, ys_sorted, route, toff_b, x1p, tp[5], 1, m, 0, n_chunks)
    ysm = _combine(combine_tab, ys_sorted, route, toff_b, x1s, tsm[5], bs, ts, n_tiles - 1, n_chunks)

    keep = k32p.shape[1]
    new = (k32p.reshape(bp, keep, nh, HEAD_DIM), v32p.reshape(bp, keep, nh, HEAD_DIM),
           tail_p[:, SUBLANES - (CONV_WIDTH - 1):, :], hl_p[:, 0, :],
           k32s.reshape(bs, ts, nh, HEAD_DIM), v32s.reshape(bs, ts, nh, HEAD_DIM),
           tail_s[:, SUBLANES - (CONV_WIDTH - 1):, :], hl_s[:, 0, :])
    return yp, ysm, new


def kernel(x_prompt, x_sample, c_prompt, c_sample, cache_k, cache_v, state_conv, state_lru, ln_mix_w, ln_ffn_w, w_ada, b_ada, w_in, q_norm_w, k_norm_w, rel_bias, conv_w, conv_b, w_rgate, b_rgate, w_igate, b_igate, lru_lambda, w_out, w_router, b_router, w_gate_up, b_gate_up, w_down, b_down):
    depth = w_in.shape[0]
    yp, ys = x_prompt, x_sample
    c_all = jnp.concatenate([c_prompt, c_sample], axis=0)
    collected = [[] for _ in range(8)]
    for l in range(depth):
        mod = _ada(c_all, w_ada[l], b_ada[l])
        lw = (ln_mix_w[l], ln_ffn_w[l], w_in[l], q_norm_w[l], k_norm_w[l], rel_bias[l], conv_w[l], conv_b[l],
              w_rgate[l], b_rgate[l], w_igate[l], b_igate[l], lru_lambda[l], w_out[l], w_router[l], b_router[l],
              w_gate_up[l], b_gate_up[l], w_down[l], b_down[l])
        yp, ys, new = _layer(yp, ys, mod, cache_k[l], cache_v[l], state_conv[l], state_lru[l], lw)
        for acc, val in zip(collected, new):
            acc.append(val)
    return (yp, ys) + tuple(jnp.stack(vals) for vals in collected)
```

```python
import functools

import jax
import jax.numpy as jnp
from jax import lax
from jax.experimental import pallas as pl
from jax.experimental.pallas import tpu as pltpu

F32 = jnp.float32
BF16 = jnp.bfloat16

CHUNK = 64
N_LEFT_CHUNKS = 8
ATTN_WINDOW = N_LEFT_CHUNKS * CHUNK
HEAD_DIM = 64
REL_CLIP = 128
CONV_WIDTH = 4
LRU_C = 8.0
N_EXPERTS = 32
TOP_K = 4
SWIGLU_LIMIT = 7.0
SWIGLU_ALPHA = 1.702
NORM_EPS = 1e-6
NEG_INF = -1e30

LANES = 128
SUBLANES = 8
BF16_ROWS = 16
MXU_DIM = 256

TOKEN_TILE = 512
ATTN_Q_TILE = 256
ATTN_STEP_BATCH = 4
EXPERT_ROWS = 256
EXPERT_UNIT_BLOCKS = 4
BIAS_TABLE = 1024
VMEM_LIMIT = 56 * 1024 * 1024


def _params(sem, vmem=VMEM_LIMIT):
    return pltpu.CompilerParams(dimension_semantics=sem, vmem_limit_bytes=vmem)


def _ada_kernel(cp_ref, cs_ref, w_ref, b_ref, op_ref, os_ref):
    w = w_ref[...].astype(BF16)
    for c_ref, o_ref in ((cp_ref, op_ref), (cs_ref, os_ref)):
        c = c_ref[...]
        s = (c * jax.nn.sigmoid(c)).astype(BF16)
        o_ref[...] = jnp.dot(s, w, preferred_element_type=F32) + b_ref[...]


def _ada(c_p, c_s, w_ada, b_ada):
    (n_p, d), n_s = c_p.shape, c_s.shape[0]
    nout = w_ada.shape[1]
    tn = 1024
    mod_p, mod_s = pl.pallas_call(
        _ada_kernel,
        grid=(nout // tn,),
        in_specs=[pl.BlockSpec((n_p, d), lambda j: (0, 0)), pl.BlockSpec((n_s, d), lambda j: (0, 0)),
                  pl.BlockSpec((d, tn), lambda j: (0, j)),
                  pl.BlockSpec((1, tn), lambda j: (0, j))],
        out_specs=[pl.BlockSpec((n_p, tn), lambda j: (0, j)), pl.BlockSpec((n_s, tn), lambda j: (0, j))],
        out_shape=[jax.ShapeDtypeStruct((n_p, nout), F32), jax.ShapeDtypeStruct((n_s, nout), F32)],
        compiler_params=_params(("arbitrary",)),
        name="ada",
    )(c_p, c_s, w_ada, b_ada.reshape(1, nout))
    return mod_p.reshape(n_p, 1, nout), mod_s.reshape(n_s, 1, nout)


MOD_SHIFT_MIX, MOD_SCALE_MIX, MOD_GATE_MIX, MOD_SHIFT_FFN, MOD_SCALE_FFN, MOD_GATE_FFN = range(6)


def _mod_spec(nb, d, term):
    return pl.BlockSpec((nb, 1, d), lambda b, i: (b, 0, term))


def _mixin_kernel(x_ref, sh_ref, sc_ref, ln_ref, win_ref, qn_ref, kn_ref, bd_ref,
                  pre_ref, h0_ref, cw_ref, cb_ref, wa_ref, wx_ref, ba_ref, bx_ref, lam_ref,
                  q_ref, k_ref, v_ref, k32_ref, v32_ref, lru_ref, tail_ref, hl_ref, cx_ref, ch_ref):
    nb, tr, d = x_ref.shape
    m = nb * tr
    aw = q_ref.shape[-1]

    @pl.when(pl.program_id(1) == 0)
    def _():
        cx_ref[...] = pre_ref[...]
        ch_ref[...] = h0_ref[...]

    x = x_ref[...]
    ms = jnp.mean(x * x, axis=-1, keepdims=True)
    h = x * lax.rsqrt(ms + NORM_EPS) * ln_ref[...]
    h = h * (1.0 + sc_ref[...]) + sh_ref[...]
    hb = h.reshape(m, d).astype(BF16)

    def proj(part):
        return jnp.dot(hb, win_ref[:, part * aw:(part + 1) * aw], preferred_element_type=F32)

    def head_norm(t, w_ref):
        msq = jnp.dot((t * t).astype(BF16), bd_ref[...], preferred_element_type=F32)
        return t * lax.rsqrt(msq + NORM_EPS) * w_ref[...]

    lru_out, new_tail, h_last = _lru_branch(
        proj(3).reshape(nb, tr, aw), proj(4).reshape(nb, tr, aw), cw_ref, cb_ref,
        wa_ref, wx_ref, ba_ref, bx_ref, lam_ref, cx_ref, ch_ref)
    lru_ref[...] = lru_out
    tail_ref[...] = new_tail
    hl_ref[...] = h_last
    q = head_norm(proj(0), qn_ref)
    k = head_norm(proj(1), kn_ref)
    v = proj(2)
    q_ref[...] = (q * (HEAD_DIM ** -0.5)).astype(BF16).reshape(nb, tr, aw)
    k_ref[...] = k.astype(BF16).reshape(nb, tr, aw)
    v_ref[...] = v.astype(BF16).reshape(nb, tr, aw)
    k32_ref[...] = k.reshape(nb, tr, aw)
    v32_ref[...] = v.reshape(nb, tr, aw)


def _mixin(x, mod, ln_w, w_in_bf, qn_t, kn_t, bd, pre, h0, conv_w, conv_b, wa_bd, wx_bd, b_a, b_x, lam,
           nb, tr):
    nbt, t, d = x.shape
    aw = qn_t.shape[-1]
    c = pre.shape[-1]
    assert c == aw
    keep = min(ATTN_WINDOW, t)
    assert tr == keep or t == tr
    grid = (nbt // nb, t // tr)
    xmap = lambda b, i: (b, i, 0)
    mmap = lambda b, i: (b, 0, 0)
    cmap = lambda b, i: (0, 0)
    cmap3 = lambda b, i: (0, 0, 0)
    tmap = lambda b, i: (b, 0, 0)
    big = pl.BlockSpec((nb, tr, aw), xmap)
    tail = pl.BlockSpec((nb, keep, aw), tmap)
    row = pl.BlockSpec((1, c), cmap)
    return pl.pallas_call(
        _mixin_kernel,
        grid=grid,
        in_specs=[pl.BlockSpec((nb, tr, d), xmap),
                  _mod_spec(nb, d, MOD_SHIFT_MIX), _mod_spec(nb, d, MOD_SCALE_MIX),
                  pl.BlockSpec((1, d), cmap),
                  pl.BlockSpec(w_in_bf.shape, cmap),
                  pl.BlockSpec((1, aw), cmap), pl.BlockSpec((1, aw), cmap),
                  pl.BlockSpec(bd.shape, cmap),
                  pl.BlockSpec((nb, SUBLANES, c), tmap), pl.BlockSpec((nb, 1, c), tmap),
                  pl.BlockSpec(conv_w.shape, cmap), row,
                  pl.BlockSpec(wa_bd.shape, cmap3), pl.BlockSpec(wx_bd.shape, cmap3),
                  row, row, row],
        out_specs=[big, big, big, tail, tail, big,
                   pl.BlockSpec((nb, SUBLANES, c), tmap), pl.BlockSpec((nb, 1, c), tmap)],
        out_shape=[jax.ShapeDtypeStruct((nbt, t, aw), BF16)] * 3
        + [jax.ShapeDtypeStruct((nbt, keep, aw), F32)] * 2
        + [jax.ShapeDtypeStruct((nbt, t, c), BF16),
           jax.ShapeDtypeStruct((nbt, SUBLANES, c), F32),
           jax.ShapeDtypeStruct((nbt, 1, c), F32)],
        scratch_shapes=[pltpu.VMEM((nb, SUBLANES, c), F32), pltpu.VMEM((nb, 1, c), F32)],
        compiler_params=_params(("arbitrary", "arbitrary")),
        name="mixin",
    )(x, mod, mod, ln_w, w_in_bf, qn_t, kn_t, bd, pre, h0, conv_w, conv_b, wa_bd, wx_bd, b_a, b_x, lam)


def _bias_table(rel_bias, off):
    h = rel_bias.shape[0]
    left = off - REL_CLIP
    right = BIAS_TABLE - left - (2 * REL_CLIP + 1)
    assert left >= 0 and right >= 0
    return jnp.concatenate([jnp.broadcast_to(rel_bias[:, :1], (h, left)), rel_bias,
                            jnp.broadcast_to(rel_bias[:, -1:], (h, right))], axis=1)


def _toeplitz(tab_row, rows, cols):
    t = jnp.broadcast_to(tab_row, (rows, BIAS_TABLE))
    t = pltpu.roll(t, BIAS_TABLE - (rows - 1), 1, stride=1, stride_axis=0)
    return t[:, :cols]


def _attn_kernel(q_ref, k0_ref, k1_ref, k2_ref, v0_ref, v1_ref, v2_ref, tab_ref, o_ref, bias_ref):
    b = pl.program_id(0)
    s = pl.program_id(1)
    qt = q_ref.shape[1]
    nk = 3 * qt
    nh = bias_ref.shape[0]

    @pl.when((b == 0) & (s == 0))
    def _():
        qi = lax.broadcasted_iota(jnp.int32, (qt, nk), 0) // CHUNK
        kc = lax.broadcasted_iota(jnp.int32, (qt, nk), 1) // CHUNK
        for h in range(nh):
            band = jnp.where(kc <= qi + N_LEFT_CHUNKS, _toeplitz(tab_ref[h:h + 1, :], qt, nk), NEG_INF)
            bias_ref[h] = jnp.where(kc >= qi, band, NEG_INF)

    pair_w = 2 * HEAD_DIM

    def attend(mask_start):
        q = q_ref[0]
        kcat = jnp.concatenate([k0_ref[0], k1_ref[0], k2_ref[0]], axis=0)
        vcat = jnp.concatenate([v0_ref[0], v1_ref[0], v2_ref[0]], axis=0)
        first = lax.broadcasted_iota(jnp.int32, (qt, pair_w), 1) < HEAD_DIM
        keep = [jnp.where(first, 1.0, 0.0).astype(BF16), jnp.where(first, 0.0, 1.0).astype(BF16)]
        if mask_start:
            in_seq = lax.broadcasted_iota(jnp.int32, (qt, nk), 1) >= (2 - s) * qt
        outs = []
        for pair in range(nh // 2):
            sl = slice(pair * pair_w, (pair + 1) * pair_w)
            q2, k2, v2 = q[:, sl], kcat[:, sl], vcat[:, sl]
            per_head = []
            for sub in range(2):
                sc = lax.dot_general(q2 * keep[sub], k2, (((1,), (1,)), ((), ())), preferred_element_type=F32)
                sc = sc + bias_ref[2 * pair + sub]
                if mask_start:
                    sc = jnp.where(in_seq, sc, NEG_INF)
                mx = jnp.max(sc, axis=-1, keepdims=True)
                p = jnp.exp(sc - mx)
                l = jnp.sum(p, axis=-1, keepdims=True)
                per_head.append(jnp.dot(p.astype(BF16), v2, preferred_element_type=F32) / l)
            outs.append(jnp.where(first, per_head[0], per_head[1]))
        o_ref[0] = jnp.concatenate(outs, axis=-1).astype(BF16)

    @pl.when(s < 2)
    def _():
        attend(True)

    @pl.when(s >= 2)
    def _():
        attend(False)


def _attn_prompt(q, k, v, tab):
    b, s, aw = q.shape
    qt = ATTN_Q_TILE
    nh = aw // HEAD_DIM
    qspec = pl.BlockSpec((1, qt, aw), lambda i, j: (i, j, 0))

    def kspec(back):
        return pl.BlockSpec((1, qt, aw), lambda i, j: (i, jnp.maximum(j - back, 0), 0))

    return pl.pallas_call(
        _attn_kernel,
        grid=(b, s // qt),
        in_specs=[qspec, kspec(2), kspec(1), kspec(0), kspec(2), kspec(1), kspec(0),
                  pl.BlockSpec(tab.shape, lambda i, j: (0, 0))],
        out_specs=qspec,
        out_shape=jax.ShapeDtypeStruct((b, s, aw), BF16),
        scratch_shapes=[pltpu.VMEM((nh, qt, 3 * qt), F32)],
        compiler_params=_params(("arbitrary", "arbitrary")),
        name="attn_prompt",
    )(q, k, k, k, v, v, v, tab)


def _attn_step_kernel(q_ref, kn_ref, vn_ref, ck_ref, cv_ref, tab_ref, o_ref, bias_ref):
    step = pl.program_id(0)
    nbs, t, aw = q_ref.shape
    nh = aw // HEAD_DIM
    rows = nh * t
    r = ck_ref.shape[-1]
    nk = r + LANES
    nt_dims = (((1,), (1,)), ((), ()))

    @pl.when(step == 0)
    def _():
        ok = lax.broadcasted_iota(jnp.int32, (t, nk), 1) < r + t
        for h in range(nh):
            bias_ref[h * t:(h + 1) * t, :] = jnp.where(ok, _toeplitz(tab_ref[h:h + 1, :], t, nk), NEG_INF)

    own = (lax.broadcasted_iota(jnp.int32, (rows, aw), 0) // t
           == lax.broadcasted_iota(jnp.int32, (rows, aw), 1) // HEAD_DIM)
    own_f = jnp.where(own, 1.0, 0.0)
    own_bf = own_f.astype(BF16)
    pad = jnp.zeros((LANES - t, aw), BF16)
    for b in range(nbs):
        q_bd = jnp.concatenate([q_ref[b]] * nh, axis=0) * own_bf
        k_old = ck_ref[b].reshape(aw, r).astype(BF16)
        v_old = cv_ref[b].reshape(aw, r).astype(BF16)
        k_new = jnp.concatenate([kn_ref[b], pad], axis=0)
        v_new = jnp.concatenate([vn_ref[b], pad], axis=0)
        s_old = jnp.dot(q_bd, k_old, preferred_element_type=F32) + bias_ref[:, :r]
        s_new = lax.dot_general(q_bd, k_new, nt_dims, preferred_element_type=F32) + bias_ref[:, r:]
        mx = jnp.maximum(jnp.max(s_old, axis=-1, keepdims=True), jnp.max(s_new, axis=-1, keepdims=True))
        p_old = jnp.exp(s_old - mx)
        p_new = jnp.exp(s_new - mx)
        l = jnp.sum(p_old, axis=-1, keepdims=True) + jnp.sum(p_new, axis=-1, keepdims=True)
        o_all = (lax.dot_general(p_old.astype(BF16), v_old, nt_dims, preferred_element_type=F32)
                 + jnp.dot(p_new.astype(BF16), v_new, preferred_element_type=F32))
        o_all = o_all * own_f / l
        out = o_all[0:t]
        for h in range(1, nh):
            out = out + o_all[h * t:(h + 1) * t]
        o_ref[b] = out.astype(BF16)


def _attn_step(q, kn, vn, ck, cv, tab, nbs):
    b, t, aw = q.shape
    nh = aw // HEAD_DIM
    r = ck.shape[-1]
    new = pl.BlockSpec((nbs, t, aw), lambda i: (i, 0, 0))
    old = pl.BlockSpec((nbs, nh, HEAD_DIM, r), lambda i: (i, 0, 0, 0))
    return pl.pallas_call(
        _attn_step_kernel,
        grid=(b // nbs,),
        in_specs=[new, new, new, old, old, pl.BlockSpec(tab.shape, lambda i: (0, 0))],
        out_specs=new,
        out_shape=jax.ShapeDtypeStruct((b, t, aw), BF16),
        scratch_shapes=[pltpu.VMEM((nh * t, r + LANES), F32)],
        compiler_params=_params(("arbitrary",)),
        name="attn_step",
    )(q, kn, vn, ck, cv, tab)


def _gelu_tanh(x):
    return x * (0.5 * (1.0 + jnp.tanh(0.7978845608028654 * (x + 0.044715 * (x * x * x)))))


def _lru_branch(x, yg, cw_ref, cb_ref, wa_ref, wx_ref, ba_ref, bx_ref, lam_ref, cx_ref, ch_ref):
    nb, tr, c = x.shape
    m = nb * tr
    half = c // 2
    xp = jnp.concatenate([cx_ref[...], x], axis=1)
    new_tail = xp[:, tr:tr + SUBLANES, :]
    xp2 = xp.reshape(nb * (tr + SUBLANES), c)
    y = cb_ref[...] + cw_ref[CONV_WIDTH - 1:CONV_WIDTH, :] * x
    for back in range(1, CONV_WIDTH):
        shifted = pltpu.roll(xp2, back, 0).reshape(nb, tr + SUBLANES, c)[:, SUBLANES:, :]
        y = y + cw_ref[CONV_WIDTH - 1 - back:CONV_WIDTH - back, :] * shifted
    y2 = y.reshape(m, c)
    yb = y2.astype(BF16)

    def gate(w_ref, b_ref):
        g = jnp.concatenate(
            [jnp.dot(yb[:, :half], w_ref[0], preferred_element_type=F32),
             jnp.dot(yb[:, half:], w_ref[1], preferred_element_type=F32)], axis=1)
        return jax.nn.sigmoid(g + b_ref[...])

    rg = gate(wa_ref, ba_ref)
    ig = gate(wx_ref, bx_ref)
    lam = lam_ref[...]
    log_sig = jnp.minimum(lam, 0.0) - jnp.log1p(jnp.exp(-jnp.abs(lam)))
    log_a = LRU_C * rg * log_sig
    a_cum = jnp.exp(log_a)
    b_cum = jnp.sqrt(-jnp.tanh(log_a) * (a_cum * a_cum + 1.0)) * (ig * y2)
    row = lax.broadcasted_iota(jnp.int32, (m, c), 0) % SUBLANES
    dist = 1
    while dist < SUBLANES:
        keep = row >= dist
        a_sh = jnp.where(keep, pltpu.roll(a_cum, dist, 0), 1.0)
        b_sh = jnp.where(keep, pltpu.roll(b_cum, dist, 0), 0.0)
        b_cum = a_cum * b_sh + b_cum
        a_cum = a_cum * a_sh
        dist *= 2
    groups = tr // SUBLANES
    a_grp = a_cum.reshape(nb, groups, SUBLANES, c)
    b_grp = b_cum.reshape(nb, groups, SUBLANES, c)
    carry = ch_ref[...]
    pieces = []
    for grp in range(groups):
        h_grp = a_grp[:, grp] * carry + b_grp[:, grp]
        carry = h_grp[:, SUBLANES - 1:SUBLANES, :]
        pieces.append(h_grp)
    h = jnp.concatenate(pieces, axis=1)
    ch_ref[...] = carry
    cx_ref[...] = new_tail
    return (h * _gelu_tanh(yg)).astype(BF16), new_tail, carry


def _outproj_kernel(*refs, aliased):
    (at_ref, lr_ref, x_ref, gm_ref, shf_ref, scf_ref, lnf_ref, wo_ref, wr_ref, br_ref) = refs[:10]
    x1_ref, h2_ref, route_ref, cnt_ref = refs[10 + aliased:]
    nb, tr, d = x_ref.shape
    m = nb * tr
    aw = at_ref.shape[-1]
    ne = wr_ref.shape[0]
    at = at_ref[...].reshape(m, aw)
    lr = lr_ref[...].reshape(m, aw)
    mix = (jnp.dot(at, wo_ref[0:aw, :], preferred_element_type=F32)
           + jnp.dot(lr, wo_ref[aw:2 * aw, :], preferred_element_type=F32))
    x1 = x_ref[...] + gm_ref[...] * mix.reshape(nb, tr, d)
    x1_ref[...] = x1
    ms = jnp.mean(x1 * x1, axis=-1, keepdims=True)
    h2 = x1 * lax.rsqrt(ms + NORM_EPS) * lnf_ref[...]
    h2 = (h2 * (1.0 + scf_ref[...]) + shf_ref[...]).reshape(m, d)
    h2_ref[...] = h2.astype(BF16)

    logits = lax.dot_general(wr_ref[...], h2.astype(BF16), (((1,), (1,)), ((), ())),
                             preferred_element_type=F32) + br_ref[...]
    e_iota = lax.broadcasted_iota(jnp.int32, (ne, m), 0).astype(F32)
    vals = logits
    top_v, sels = [], []
    for k in range(TOP_K):
        mx = jnp.max(vals, axis=0, keepdims=True)
        idx = jnp.min(jnp.where(vals == mx, e_iota, float(ne)), axis=0, keepdims=True)
        sel = e_iota == idx
        vals = jnp.where(sel, -jnp.inf, vals)
        top_v.append(mx)
        sels.append(sel)
        route_ref[0, k:k + 1, :] = idx
    ex = [jnp.exp(v - top_v[0]) for v in top_v]
    den = ex[0] + ex[1] + ex[2] + ex[3]
    chosen = jnp.zeros((ne, m), F32)
    for k in range(TOP_K):
        route_ref[0, 2 * TOP_K + k:2 * TOP_K + k + 1, :] = ex[k] / den
        chosen = chosen + jnp.where(sels[k], 1.0, 0.0)
    before = (lax.broadcasted_iota(jnp.int32, (m, m), 0) < lax.broadcasted_iota(jnp.int32, (m, m), 1))
    rank = jnp.dot(chosen.astype(BF16), jnp.where(before, 1.0, 0.0).astype(BF16), preferred_element_type=F32)
    for k in range(TOP_K):
        route_ref[0, TOP_K + k:TOP_K + k + 1, :] = jnp.sum(jnp.where(sels[k], rank, 0.0), axis=0, keepdims=True)
    route_ref[0, 3 * TOP_K:4 * TOP_K, :] = jnp.zeros((TOP_K, m), F32)
    cnt_ref[0] = jnp.broadcast_to(jnp.sum(chosen, axis=1, keepdims=True), (ne, LANES))


def _outproj(attn, lru_o, x, mod, lnf, w_out_bf, wr_t, br, nb, tr, n_tiles, tile0, prev):
    nbt, t, d = x.shape
    aw = attn.shape[-1]
    m = nb * tr
    assert m == TOKEN_TILE
    ne = wr_t.shape[0]
    tiles_per_seq = t // tr
    xmap = lambda b, i: (b, i, 0)
    c2 = lambda b, i: (0, 0)
    tile = lambda b, i: (tile0 + b * tiles_per_seq + i, 0)
    tile3 = lambda b, i: (tile0 + b * tiles_per_seq + i, 0, 0)
    in_specs = [pl.BlockSpec((nb, tr, aw), xmap), pl.BlockSpec((nb, tr, aw), xmap),
                pl.BlockSpec((nb, tr, d), xmap),
                _mod_spec(nb, d, MOD_GATE_MIX), _mod_spec(nb, d, MOD_SHIFT_FFN), _mod_spec(nb, d, MOD_SCALE_FFN),
                pl.BlockSpec((1, d), c2), pl.BlockSpec(w_out_bf.shape, c2),
                pl.BlockSpec(wr_t.shape, c2), pl.BlockSpec((ne, 1), c2)]
    args = [attn, lru_o, x, mod, mod, mod, lnf, w_out_bf, wr_t, br]
    aliases = {}
    if prev is not None:
        in_specs += [pl.BlockSpec(memory_space=pl.ANY)] * 3
        args += list(prev)
        aliases = {10: 1, 11: 2, 12: 3}
    return pl.pallas_call(
        functools.partial(_outproj_kernel, aliased=len(aliases)),
        grid=(nbt // nb, tiles_per_seq),
        in_specs=in_specs,
        out_specs=[pl.BlockSpec((nb, tr, d), xmap), pl.BlockSpec((m, d), tile),
                   pl.BlockSpec((1, 4 * TOP_K, m), tile3), pl.BlockSpec((1, ne, LANES), tile3)],
        out_shape=[jax.ShapeDtypeStruct((nbt, t, d), F32),
                   jax.ShapeDtypeStruct((n_tiles * m, d), BF16),
                   jax.ShapeDtypeStruct((n_tiles, 4 * TOP_K, m), F32),
                   jax.ShapeDtypeStruct((n_tiles, ne, LANES), F32)],
        input_output_aliases=aliases,
        compiler_params=_params(("arbitrary", "arbitrary")),
        name="outproj",
    )(*args)


def _tile_rows(m):
    cap = TOP_K * m + N_EXPERTS * (BF16_ROWS - 1) + BF16_ROWS
    return -(-cap // TOKEN_TILE) * TOKEN_TILE


def _table_sizes(nt):
    g = BF16_ROWS
    m = TOKEN_TILE
    n_chunks = _tile_rows(m) // g
    n_gap = -(-(N_EXPERTS * (EXPERT_ROWS // g - 1)) // nt)
    bound = TOP_K * m * nt + nt * N_EXPERTS * (g - 1) + N_EXPERTS * (EXPERT_ROWS - g)
    n_sorted = -(-bound // EXPERT_ROWS) * EXPERT_ROWS
    return n_chunks, n_gap, n_sorted, n_sorted + 2 * (n_chunks + n_gap) * g


def _route_tables(cnt):
    nt = cnt.shape[0]
    g = BF16_ROWS
    bm = EXPERT_ROWS
    n_chunks, n_gap, n_sorted, _ = _table_sizes(nt)
    e_ids = jnp.arange(N_EXPERTS, dtype=jnp.int32)
    t_ids = jnp.arange(nt, dtype=jnp.int32)
    upto = (e_ids[:, None] <= e_ids[None, :]).astype(jnp.int32)
    pc = (cnt + g - 1) // g * g
    ctile = jnp.sum(pc[:, :, None] * upto[None], axis=1)
    toff = ctile - pc
    trow = ctile[:, -1]
    tot = jnp.sum(pc, axis=0)
    reg = (tot + bm - 1) // bm * bm
    creg = jnp.sum(reg[:, None] * upto, axis=0)
    base = creg - reg
    earlier = (t_ids[:, None] < t_ids[None, :]).astype(jnp.int32)
    goff = base[None, :] + jnp.sum(pc[:, None, :] * earlier[:, :, None], axis=0)
    r = jnp.arange(n_chunks, dtype=jnp.int32) * g
    r3 = r[None, :, None]
    in_seg = (toff[:, None, :] <= r3) & (r3 < ctile[:, None, :])
    dst = jnp.sum(jnp.where(in_seg, (goff - toff)[:, None, :], 0), axis=2) + r[None, :]
    dst = jnp.where(r[None, :] < trow[:, None], dst, -1)
    gcnt = (reg - tot) // g
    gcum = jnp.sum(gcnt[:, None] * upto, axis=0)
    gstart = gcum - gcnt
    s = jnp.arange(nt * n_gap, dtype=jnp.int32)
    in_gap = (gstart[None, :] <= s[:, None]) & (s[:, None] < gcum[None, :])
    gdst = jnp.sum(jnp.where(in_gap, (base + tot - g * gstart)[None, :] + g * s[:, None], 0), axis=1)
    gdst = jnp.where(s < gcum[-1], gdst, -1).reshape(nt, n_gap)
    table = jnp.concatenate([dst, gdst], axis=1).astype(jnp.int32)
    n_entries = n_chunks + n_gap
    spare = n_sorted + ((t_ids % 2)[:, None] * n_entries + jnp.arange(n_entries, dtype=jnp.int32)[None, :]) * g
    dispatch_tab = jnp.where(table >= 0, table, spare).astype(jnp.int32)
    combine_tab = jnp.maximum(dst, 0).astype(jnp.int32)
    toff_b = jnp.broadcast_to(toff.astype(F32)[:, :, None], (nt, N_EXPERTS, LANES))
    limit = (creg[-1] - EXPERT_UNIT_BLOCKS * bm).astype(jnp.int32).reshape(1)
    return dispatch_tab, combine_tab, toff_b, base.astype(jnp.int32), (reg // bm).astype(jnp.int32), limit


def _slot_rows(route_ref, toff_ref, m):
    ne = toff_ref.shape[1]
    e_iota = lax.broadcasted_iota(jnp.int32, (ne, m), 0).astype(F32)
    toff_col = toff_ref[0][:, 0:1]
    pos = []
    for k in range(TOP_K):
        sel = e_iota == route_ref[0, k:k + 1, :]
        start = jnp.sum(jnp.where(sel, toff_col, 0.0), axis=0, keepdims=True)
        pos.append(start + route_ref[0, TOP_K + k:TOP_K + k + 1, :])
    return pos


def _dispatch_kernel(tab_ref, h2_ref, route_ref, toff_ref, xs_hbm, buf_ref, sem, *, n_chunks, n_tiles):
    t = pl.program_id(0)
    slot = t % 2
    m = h2_ref.shape[0]
    rows = buf_ref.shape[1]
    n_entries = tab_ref.shape[1]
    g = BF16_ROWS
    per_chunk = m // g

    def start(c):
        src = c * g if c < n_chunks else rows - g
        pltpu.make_async_copy(
            buf_ref.at[slot, pl.ds(src, g)],
            xs_hbm.at[pl.ds(pl.multiple_of(tab_ref[t, c], g), g)], sem.at[slot]).start()

    def wait_all(which):
        for _ in range(n_entries):
            pltpu.make_async_copy(buf_ref.at[which, pl.ds(0, g)], xs_hbm.at[pl.ds(0, g)], sem.at[which]).wait()

    @pl.when(t >= 2)
    def _():
        wait_all(slot)

    pos = _slot_rows(route_ref, toff_ref, m)
    h2 = h2_ref[...]
    for rc in range(rows // m):
        r_iota = (lax.broadcasted_iota(jnp.int32, (m, m), 0) + rc * m).astype(F32)
        onehot = jnp.zeros((m, m), F32)
        for k in range(TOP_K):
            onehot = jnp.where(r_iota == pos[k], 1.0, onehot)
        buf_ref[slot, rc * m:(rc + 1) * m, :] = jnp.dot(
            onehot.astype(BF16), h2, preferred_element_type=F32).astype(BF16)
        for c in range((rc - 1) * per_chunk, rc * per_chunk) if rc > 0 else ():
            start(c)
    for c in range(n_chunks - per_chunk, n_entries):
        start(c)

    @pl.when(t == n_tiles - 1)
    def _():
        if n_tiles > 1:
            wait_all(1 - slot)
        wait_all(slot)


def _dispatch(table, h2, route, toff_b, n_rows, n_chunks):
    nt = route.shape[0]
    m = TOKEN_TILE
    d = h2.shape[1]
    rows = _tile_rows(m)
    grid_spec = pltpu.PrefetchScalarGridSpec(
        num_scalar_prefetch=1,
        grid=(nt,),
        in_specs=[pl.BlockSpec((m, d), lambda t, tab: (t, 0)),
                  pl.BlockSpec((1, 4 * TOP_K, m), lambda t, tab: (t, 0, 0)),
                  pl.BlockSpec((1, N_EXPERTS, LANES), lambda t, tab: (t, 0, 0))],
        out_specs=pl.BlockSpec(memory_space=pl.ANY),
        scratch_shapes=[pltpu.VMEM((2, rows, d), BF16), pltpu.SemaphoreType.DMA((2,))],
    )
    return pl.pallas_call(
        functools.partial(_dispatch_kernel, n_chunks=n_chunks, n_tiles=nt),
        grid_spec=grid_spec,
        out_shape=jax.ShapeDtypeStruct((n_rows, d), BF16),
        compiler_params=_params(("arbitrary",)),
        name="dispatch",
    )(table, h2, route, toff_b)


def _expert_kernel(row0_ref, nblk_ref, limit_ref, xs_hbm, wgu_ref, bgu_ref, wdn_ref, bdn_ref, ys_hbm,
                   wgu_bf, wdn_bf, xbuf, ybuf, sem_in, sem_out):
    e = pl.program_id(0)
    ne = pl.num_programs(0)
    bm = EXPERT_ROWS
    unit = xbuf.shape[1]
    per_unit = unit // bm
    dff = wdn_ref.shape[1]
    nblk = nblk_ref[e]
    n_units = (nblk + per_unit - 1) // per_unit

    def unit_start(expert, s):
        true = row0_ref[expert] + s * unit
        start = jnp.minimum(true, limit_ref[0])
        return pl.multiple_of(start, bm), pl.multiple_of(true - start, bm)

    def in_copy(expert, s, slot):
        start, _ = unit_start(expert, s)
        return pltpu.make_async_copy(xs_hbm.at[pl.ds(start, unit)], xbuf.at[slot], sem_in.at[slot])

    def out_copy(j, slot, i):
        start = pl.multiple_of(row0_ref[e] + j * bm, bm)
        return pltpu.make_async_copy(ybuf.at[slot, pl.ds(i * bm, bm)], ys_hbm.at[pl.ds(start, bm)],
                                     sem_out.at[slot])

    @pl.when((e == 0) & (nblk > 0))
    def _():
        in_copy(e, 0, 0).start()

    wgu_bf[...] = wgu_ref[0].astype(BF16)
    wdn_bf[...] = wdn_ref[0].astype(BF16)

    def run_unit(s, carry):
        slot = s % 2
        in_copy(e, s, slot).wait()

        @pl.when(s + 1 < n_units)
        def _():
            in_copy(e, s + 1, 1 - slot).start()

        @pl.when(s >= 2)
        def _():
            for i in range(per_unit):
                out_copy(0, slot, i).wait()

        _, lead = unit_start(e, s)

        def ffn(first, count):
            rows = count * bm
            x = xbuf[slot, pl.ds(pl.multiple_of(lead + first * bm, bm), rows), :]
            gu = jnp.dot(x, wgu_bf[...], preferred_element_type=F32) + bgu_ref[0]
            gate = jnp.minimum(gu[:, :dff], SWIGLU_LIMIT)
            up = jnp.clip(gu[:, dff:], -SWIGLU_LIMIT, SWIGLU_LIMIT)
            glu = gate * jax.nn.sigmoid(gate * SWIGLU_ALPHA)
            act = ((up + 1.0) * glu).astype(BF16)
            ybuf[slot, first * bm:first * bm + rows, :] = (
                jnp.dot(act, wdn_bf[...], preferred_element_type=F32) + bdn_ref[0]).astype(BF16)
            for i in range(first, first + count):
                out_copy(s * per_unit + i, slot, i).start()

        for first in range(0, per_unit, 2):
            have = nblk - s * per_unit - first

            @pl.when(have >= 2)
            def _():
                ffn(first, 2)

            @pl.when(have == 1)
            def _():
                ffn(first, 1)
        return carry

    lax.fori_loop(0, n_units, run_unit, 0)

    nxt = jnp.minimum(e + 1, ne - 1)

    @pl.when((e + 1 < ne) & (nblk_ref[nxt] > 0))
    def _():
        in_copy(nxt, 0, 0).start()

    @pl.when(n_units >= 2)
    def _():
        for i in range(per_unit):
            out_copy(0, n_units % 2, i).wait()

    in_last = nblk - (n_units - 1) * per_unit
    for i in range(per_unit):
        @pl.when((n_units >= 1) & (i < in_last))
        def _():
            out_copy(0, (n_units - 1) % 2, i).wait()


def _experts(row0, nblk, limit, xs, w_gu, b_gu, w_dn, b_dn):
    n_rows, d = xs.shape
    bm = EXPERT_ROWS
    unit = EXPERT_UNIT_BLOCKS * bm
    ne, _, dff2 = w_gu.shape
    dff = w_dn.shape[1]
    exp3 = lambda e, r0, nb, lim: (e, 0, 0)
    grid_spec = pltpu.PrefetchScalarGridSpec(
        num_scalar_prefetch=3,
        grid=(ne,),
        in_specs=[pl.BlockSpec(memory_space=pl.ANY),
                  pl.BlockSpec((1, d, dff2), exp3), pl.BlockSpec((1, 1, dff2), exp3),
                  pl.BlockSpec((1, dff, d), exp3), pl.BlockSpec((1, 1, d), exp3)],
        out_specs=pl.BlockSpec(memory_space=pl.ANY),
        scratch_shapes=[pltpu.VMEM((d, dff2), BF16), pltpu.VMEM((dff, d), BF16),
                        pltpu.VMEM((2, unit, d), BF16), pltpu.VMEM((2, unit, d), BF16),
                        pltpu.SemaphoreType.DMA((2,)), pltpu.SemaphoreType.DMA((2,))],
    )
    return pl.pallas_call(
        _expert_kernel,
        grid_spec=grid_spec,
        out_shape=jax.ShapeDtypeStruct((n_rows, d), BF16),
        compiler_params=_params(("arbitrary",)),
        name="experts",
    )(row0, nblk, limit, xs, w_gu, b_gu.reshape(ne, 1, dff2), w_dn, b_dn.reshape(ne, 1, d))


def _combine_kernel(tab_ref, ys_hbm, route_ref, toff_ref, x1_ref, gf_ref, o_ref, buf_ref, sem, *, n_chunks, tile0):
    step = pl.program_id(0)
    n_steps = pl.num_programs(0)
    t = tile0 + step
    slot = step % 2
    nb, tr, d = x1_ref.shape
    m = nb * tr
    rows = buf_ref.shape[1]
    g = BF16_ROWS

    def fetch(tile, which):
        for c in range(n_chunks):
            pltpu.make_async_copy(
                ys_hbm.at[pl.ds(pl.multiple_of(tab_ref[tile, c], g), g)],
                buf_ref.at[which, pl.ds(c * g, g)], sem.at[which]).start()

    @pl.when(step == 0)
    def _():
        fetch(t, slot)

    @pl.when(step + 1 < n_steps)
    def _():
        fetch(t + 1, 1 - slot)

    pos = _slot_rows(route_ref, toff_ref, m)
    gates = [route_ref[0, 2 * TOP_K + k:2 * TOP_K + k + 1, :] for k in range(TOP_K)]
    stacked = jnp.concatenate(pos + gates + [jnp.zeros((LANES - 2 * TOP_K, m), F32)], axis=0)
    cols = stacked.T
    for _ in range(n_chunks):
        pltpu.make_async_copy(ys_hbm.at[pl.ds(0, g)], buf_ref.at[slot, pl.ds(0, g)], sem.at[slot]).wait()

    acc = jnp.zeros((m, d), F32)
    for rc in range(rows // m):
        c_iota = (lax.broadcasted_iota(jnp.int32, (m, m), 1) + rc * m).astype(F32)
        weights = jnp.zeros((m, m), F32)
        for k in range(TOP_K):
            weights = jnp.where(c_iota == cols[:, k:k + 1], cols[:, TOP_K + k:TOP_K + k + 1], weights)
        acc = acc + jnp.dot(weights.astype(BF16), buf_ref[slot, rc * m:(rc + 1) * m, :],
                            preferred_element_type=F32)
    o_ref[...] = x1_ref[...] + gf_ref[...] * acc.reshape(nb, tr, d)


def _combine(table, ys, route, toff_b, x1, gf, nb, tr, tile0, n_chunks):
    nbt, t, d = x1.shape
    m = nb * tr
    assert m == TOKEN_TILE
    rows = _tile_rows(m)
    tiles_per_seq = t // tr
    n_steps = (nbt // nb) * tiles_per_seq
    xmap = lambda s, tab: (s // tiles_per_seq, s % tiles_per_seq, 0)
    grid_spec = pltpu.PrefetchScalarGridSpec(
        num_scalar_prefetch=1,
        grid=(n_steps,),
        in_specs=[pl.BlockSpec(memory_space=pl.ANY),
                  pl.BlockSpec((1, 4 * TOP_K, m), lambda s, tab: (tile0 + s, 0, 0)),
                  pl.BlockSpec((1, N_EXPERTS, LANES), lambda s, tab: (tile0 + s, 0, 0)),
                  pl.BlockSpec((nb, tr, d), xmap),
                  pl.BlockSpec((nb, 1, d), lambda s, tab: (s // tiles_per_seq, 0, MOD_GATE_FFN))],
        out_specs=pl.BlockSpec((nb, tr, d), xmap),
        scratch_shapes=[pltpu.VMEM((2, rows, d), BF16), pltpu.SemaphoreType.DMA((2,))],
    )
    return pl.pallas_call(
        functools.partial(_combine_kernel, n_chunks=n_chunks, tile0=tile0),
        grid_spec=grid_spec,
        out_shape=jax.ShapeDtypeStruct((nbt, t, d), F32),
        compiler_params=_params(("arbitrary",)),
        name="combine",
    )(table, ys, route, toff_b, x1, gf)


def _block_diag(w, groups):
    n, k, _ = w.shape
    w = w.reshape(n // groups, groups, k, k)
    eye = jnp.eye(groups, dtype=w.dtype)
    return jnp.einsum("ngij,gh->ngihj", w, eye).reshape(n // groups, groups * k, groups * k)


def _layer(xp, xs, mod_p, mod_s, k_cache, v_cache, conv_state, lru_state, lw):
    (ln_mix, ln_ffn, w_in, q_norm, k_norm, rel_bias, conv_w, conv_b, w_rg, b_rg, w_ig, b_ig, lam,
     w_out, w_router, b_router, w_gu, b_gu, w_dn, b_dn) = lw
    bp, s, d = xp.shape
    bs, ts, _ = xs.shape
    aw = w_out.shape[0] // 2
    nh = aw // HEAD_DIM
    m = TOKEN_TILE
    assert s % m == 0 and bs * ts == m and s % ATTN_Q_TILE == 0

    w_in_bf = w_in.astype(BF16)
    w_out_bf = w_out.astype(BF16)
    qn_t = jnp.tile(q_norm, nh).reshape(1, aw)
    kn_t = jnp.tile(k_norm, nh).reshape(1, aw)
    head_mean = _block_diag(jnp.full((nh, HEAD_DIM, HEAD_DIM), 1.0 / HEAD_DIM, F32), nh)[0].astype(BF16)
    groups = MXU_DIM // w_rg.shape[-1]
    wa_bd = _block_diag(w_rg, groups).astype(BF16)
    wx_bd = _block_diag(w_ig, groups).astype(BF16)
    lw_c = b_rg.size
    b_a = b_rg.reshape(1, lw_c)
    b_x = b_ig.reshape(1, lw_c)
    lam2 = lam.reshape(1, lw_c)
    cb2 = conv_b.reshape(1, lw_c)
    ln_mix2 = ln_mix.reshape(1, d)
    ln_ffn2 = ln_ffn.reshape(1, d)
    wr_t = w_router.T.astype(BF16)
    br = b_router.reshape(-1, 1)
    tab_p = _bias_table(rel_bias, 3 * ATTN_Q_TILE - 1)
    r_cache = k_cache.shape[1]
    tab_s = _bias_table(rel_bias, r_cache + ts - 1)

    zeros_pre = jnp.zeros((bp, SUBLANES, lw_c), F32)
    zeros_h = jnp.zeros((bp, 1, lw_c), F32)
    pre_s = jnp.pad(conv_state, ((0, 0), (SUBLANES - (CONV_WIDTH - 1), 0), (0, 0)))
    lru_w = (conv_w, cb2, wa_bd, wx_bd, b_a, b_x, lam2)
    qp, kp, vp, k32p, v32p, lru_p, tail_p, hl_p = _mixin(
        xp, mod_p,ln_mix2, w_in_bf, qn_t, kn_t, head_mean, zeros_pre, zeros_h, *lru_w, 1, m)
    qs, ks, vs, k32s, v32s, lru_s, tail_s, hl_s = _mixin(
        xs, mod_s,ln_mix2, w_in_bf, qn_t, kn_t, head_mean, pre_s, lru_state[:, None, :], *lru_w, bs, ts)
    attn_p = _attn_prompt(qp, kp, vp, tab_p)
    attn_s = _attn_step(qs, ks, vs, jnp.transpose(k_cache, (0, 2, 3, 1)), jnp.transpose(v_cache, (0, 2, 3, 1)),
                        tab_s, ATTN_STEP_BATCH)

    n_tiles = bp * (s // m) + 1
    x1p, h2, route, cnt = _outproj(attn_p, lru_p, xp, mod_p,ln_ffn2, w_out_bf, wr_t, br,
                                   1, m, n_tiles, 0, None)
    x1s, h2, route, cnt = _outproj(attn_s, lru_s, xs, mod_s,ln_ffn2, w_out_bf, wr_t, br,
                                   bs, ts, n_tiles, n_tiles - 1, (h2, route, cnt))

    n_chunks, _, _, n_rows = _table_sizes(n_tiles)
    assert TOP_K * m * n_tiles >= EXPERT_UNIT_BLOCKS * EXPERT_ROWS
    dispatch_tab, combine_tab, toff_b, row0, nblk, limit = _route_tables(cnt[:, :, 0].astype(jnp.int32))
    xs_sorted = _dispatch(dispatch_tab, h2, route, toff_b, n_rows, n_chunks)
    ys_sorted = _experts(row0, nblk, limit, xs_sorted, w_gu, b_gu, w_dn, b_dn)
    yp = _combine(combine_tab, ys_sorted, route, toff_b, x1p, mod_p, 1, m, 0, n_chunks)
    ysm = _combine(combine_tab, ys_sorted, route, toff_b, x1s, mod_s, bs, ts, n_tiles - 1, n_chunks)

    keep = k32p.shape[1]
    new = (k32p.reshape(bp, keep, nh, HEAD_DIM), v32p.reshape(bp, keep, nh, HEAD_DIM),
           tail_p[:, SUBLANES - (CONV_WIDTH - 1):, :], hl_p[:, 0, :],
           k32s.reshape(bs, ts, nh, HEAD_DIM), v32s.reshape(bs, ts, nh, HEAD_DIM),
           tail_s[:, SUBLANES - (CONV_WIDTH - 1):, :], hl_s[:, 0, :])
    return yp, ysm, new


def kernel(x_prompt, x_sample, c_prompt, c_sample, cache_k, cache_v, state_conv, state_lru, ln_mix_w, ln_ffn_w, w_ada, b_ada, w_in, q_norm_w, k_norm_w, rel_bias, conv_w, conv_b, w_rgate, b_rgate, w_igate, b_igate, lru_lambda, w_out, w_router, b_router, w_gate_up, b_gate_up, w_down, b_down):
    depth = w_in.shape[0]
    yp, ys = x_prompt, x_sample
    collected = [[] for _ in range(8)]
    for l in range(depth):
        mod_p, mod_s = _ada(c_prompt, c_sample, w_ada[l], b_ada[l])
        lw = (ln_mix_w[l], ln_ffn_w[l], w_in[l], q_norm_w[l], k_norm_w[l], rel_bias[l], conv_w[l], conv_b[l],
              w_rgate[l], b_rgate[l], w_igate[l], b_igate[l], lru_lambda[l], w_out[l], w_router[l], b_router[l],
              w_gate_up[l], b_gate_up[l], w_down[l], b_down[l])
        yp, ys, new = _layer(yp, ys, mod_p, mod_s, cache_k[l], cache_v[l], state_conv[l], state_lru[l], lw)
        for acc, val in zip(collected, new):
            acc.append(val)
    return (yp, ys) + tuple(jnp.stack(vals) for vals in collected)
```

```python
import functools

import jax
import jax.numpy as jnp
from jax import lax
from jax.experimental import pallas as pl
from jax.experimental.pallas import tpu as pltpu

F32 = jnp.float32
BF16 = jnp.bfloat16

CHUNK = 64
N_LEFT_CHUNKS = 8
ATTN_WINDOW = N_LEFT_CHUNKS * CHUNK
HEAD_DIM = 64
REL_CLIP = 128
CONV_WIDTH = 4
LRU_C = 8.0
N_EXPERTS = 32
TOP_K = 4
SWIGLU_LIMIT = 7.0
SWIGLU_ALPHA = 1.702
NORM_EPS = 1e-6
NEG_INF = -1e30

LANES = 128
SUBLANES = 8
BF16_ROWS = 16
MXU_DIM = 256

TOKEN_TILE = 512
ATTN_Q_TILE = 256
ATTN_STEP_BATCH = 4
EXPERT_ROWS = 256
EXPERT_UNIT_BLOCKS = 4
BIAS_TABLE = 1024
VMEM_LIMIT = 56 * 1024 * 1024


def _params(sem, vmem=VMEM_LIMIT):
    return pltpu.CompilerParams(dimension_semantics=sem, vmem_limit_bytes=vmem)


def _ada_kernel(cp_ref, cs_ref, w_ref, b_ref, op_ref, os_ref):
    w = w_ref[...].astype(BF16)
    for c_ref, o_ref in ((cp_ref, op_ref), (cs_ref, os_ref)):
        c = c_ref[...]
        s = (c * jax.nn.sigmoid(c)).astype(BF16)
        o_ref[...] = jnp.dot(s, w, preferred_element_type=F32) + b_ref[...]


def _ada(c_p, c_s, w_ada, b_ada):
    (n_p, d), n_s = c_p.shape, c_s.shape[0]
    nout = w_ada.shape[1]
    tn = 1024
    mod_p, mod_s = pl.pallas_call(
        _ada_kernel,
        grid=(nout // tn,),
        in_specs=[pl.BlockSpec((n_p, d), lambda j: (0, 0)), pl.BlockSpec((n_s, d), lambda j: (0, 0)),
                  pl.BlockSpec((d, tn), lambda j: (0, j)),
                  pl.BlockSpec((1, tn), lambda j: (0, j))],
        out_specs=[pl.BlockSpec((n_p, tn), lambda j: (0, j)), pl.BlockSpec((n_s, tn), lambda j: (0, j))],
        out_shape=[jax.ShapeDtypeStruct((n_p, nout), F32), jax.ShapeDtypeStruct((n_s, nout), F32)],
        compiler_params=_params(("arbitrary",)),
        name="ada",
    )(c_p, c_s, w_ada, b_ada.reshape(1, nout))
    return mod_p.reshape(n_p, 1, nout), mod_s.reshape(n_s, 1, nout)


MOD_SHIFT_MIX, MOD_SCALE_MIX, MOD_GATE_MIX, MOD_SHIFT_FFN, MOD_SCALE_FFN, MOD_GATE_FFN = range(6)


def _mod_spec(nb, d, term):
    return pl.BlockSpec((nb, 1, d), lambda b, i: (b, 0, term))


def _mixin_kernel(x_ref, sh_ref, sc_ref, ln_ref, win_ref, qn_ref, kn_ref, bd_ref,
                  pre_ref, h0_ref, cw_ref, cb_ref, wa_ref, wx_ref, ba_ref, bx_ref, lam_ref,
                  q_ref, k_ref, v_ref, k32_ref, v32_ref, lru_ref, tail_ref, hl_ref, cx_ref, ch_ref):
    nb, tr, d = x_ref.shape
    m = nb * tr
    aw = q_ref.shape[-1]

    @pl.when(pl.program_id(1) == 0)
    def _():
        cx_ref[...] = pre_ref[...]
        ch_ref[...] = h0_ref[...]

    x = x_ref[...]
    ms = jnp.mean(x * x, axis=-1, keepdims=True)
    h = x * lax.rsqrt(ms + NORM_EPS) * (ln_ref[...] * (1.0 + sc_ref[...])) + sh_ref[...]
    hb = h.reshape(m, d).astype(BF16)

    def proj(part):
        return jnp.dot(hb, win_ref[:, part * aw:(part + 1) * aw], preferred_element_type=F32)

    def head_norm(t, w_ref):
        msq = jnp.dot((t * t).astype(BF16), bd_ref[...], preferred_element_type=F32)
        return t * lax.rsqrt(msq + NORM_EPS) * w_ref[...]

    lru_out, new_tail, h_last = _lru_branch(
        proj(3).reshape(nb, tr, aw), proj(4).reshape(nb, tr, aw), cw_ref, cb_ref,
        wa_ref, wx_ref, ba_ref, bx_ref, lam_ref, cx_ref, ch_ref)
    lru_ref[...] = lru_out
    tail_ref[...] = new_tail
    hl_ref[...] = h_last
    q = head_norm(proj(0), qn_ref)
    k = head_norm(proj(1), kn_ref)
    v = proj(2)
    q_ref[...] = q.astype(BF16).reshape(nb, tr, aw)
    k_ref[...] = k.astype(BF16).reshape(nb, tr, aw)
    v_ref[...] = v.astype(BF16).reshape(nb, tr, aw)
    k32_ref[...] = k.reshape(nb, tr, aw)
    v32_ref[...] = v.reshape(nb, tr, aw)


def _mixin(x, mod, ln_w, w_in_bf, qn_t, kn_t, bd, pre, h0, conv_w, conv_b, wa_bd, wx_bd, b_a, b_x, lam,
           nb, tr):
    nbt, t, d = x.shape
    aw = qn_t.shape[-1]
    c = pre.shape[-1]
    assert c == aw
    keep = min(ATTN_WINDOW, t)
    assert tr == keep or t == tr
    grid = (nbt // nb, t // tr)
    xmap = lambda b, i: (b, i, 0)
    mmap = lambda b, i: (b, 0, 0)
    cmap = lambda b, i: (0, 0)
    cmap3 = lambda b, i: (0, 0, 0)
    tmap = lambda b, i: (b, 0, 0)
    big = pl.BlockSpec((nb, tr, aw), xmap)
    tail = pl.BlockSpec((nb, keep, aw), tmap)
    row = pl.BlockSpec((1, c), cmap)
    return pl.pallas_call(
        _mixin_kernel,
        grid=grid,
        in_specs=[pl.BlockSpec((nb, tr, d), xmap),
                  _mod_spec(nb, d, MOD_SHIFT_MIX), _mod_spec(nb, d, MOD_SCALE_MIX),
                  pl.BlockSpec((1, d), cmap),
                  pl.BlockSpec(w_in_bf.shape, cmap),
                  pl.BlockSpec((1, aw), cmap), pl.BlockSpec((1, aw), cmap),
                  pl.BlockSpec(bd.shape, cmap),
                  pl.BlockSpec((nb, SUBLANES, c), tmap), pl.BlockSpec((nb, 1, c), tmap),
                  pl.BlockSpec(conv_w.shape, cmap), row,
                  pl.BlockSpec(wa_bd.shape, cmap3), pl.BlockSpec(wx_bd.shape, cmap3),
                  row, row, row],
        out_specs=[big, big, big, tail, tail, big,
                   pl.BlockSpec((nb, SUBLANES, c), tmap), pl.BlockSpec((nb, 1, c), tmap)],
        out_shape=[jax.ShapeDtypeStruct((nbt, t, aw), BF16)] * 3
        + [jax.ShapeDtypeStruct((nbt, keep, aw), F32)] * 2
        + [jax.ShapeDtypeStruct((nbt, t, c), BF16),
           jax.ShapeDtypeStruct((nbt, SUBLANES, c), F32),
           jax.ShapeDtypeStruct((nbt, 1, c), F32)],
        scratch_shapes=[pltpu.VMEM((nb, SUBLANES, c), F32), pltpu.VMEM((nb, 1, c), F32)],
        compiler_params=_params(("arbitrary", "arbitrary")),
        name="mixin",
    )(x, mod, mod, ln_w, w_in_bf, qn_t, kn_t, bd, pre, h0, conv_w, conv_b, wa_bd, wx_bd, b_a, b_x, lam)


def _bias_table(rel_bias, off):
    h = rel_bias.shape[0]
    left = off - REL_CLIP
    right = BIAS_TABLE - left - (2 * REL_CLIP + 1)
    assert left >= 0 and right >= 0
    return jnp.concatenate([jnp.broadcast_to(rel_bias[:, :1], (h, left)), rel_bias,
                            jnp.broadcast_to(rel_bias[:, -1:], (h, right))], axis=1)


def _toeplitz(tab_row, rows, cols):
    t = jnp.broadcast_to(tab_row, (rows, BIAS_TABLE))
    t = pltpu.roll(t, BIAS_TABLE - (rows - 1), 1, stride=1, stride_axis=0)
    return t[:, :cols]


def _attn_kernel(q_ref, k0_ref, k1_ref, k2_ref, v0_ref, v1_ref, v2_ref, tab_ref, o_ref, bias_ref):
    b = pl.program_id(0)
    s = pl.program_id(1)
    qt = q_ref.shape[1]
    nk = 3 * qt
    nh = bias_ref.shape[0]

    @pl.when((b == 0) & (s == 0))
    def _():
        qi = lax.broadcasted_iota(jnp.int32, (qt, nk), 0) // CHUNK
        kc = lax.broadcasted_iota(jnp.int32, (qt, nk), 1) // CHUNK
        for h in range(nh):
            band = jnp.where(kc <= qi + N_LEFT_CHUNKS, _toeplitz(tab_ref[h:h + 1, :], qt, nk), NEG_INF)
            bias_ref[h] = jnp.where(kc >= qi, band, NEG_INF)

    pair_w = 2 * HEAD_DIM

    def attend(mask_start):
        q = q_ref[0]
        kcat = jnp.concatenate([k0_ref[0], k1_ref[0], k2_ref[0]], axis=0)
        vcat = jnp.concatenate([v0_ref[0], v1_ref[0], v2_ref[0]], axis=0)
        first = lax.broadcasted_iota(jnp.int32, (qt, pair_w), 1) < HEAD_DIM
        keep = [jnp.where(first, 1.0, 0.0).astype(BF16), jnp.where(first, 0.0, 1.0).astype(BF16)]
        if mask_start:
            in_seq = lax.broadcasted_iota(jnp.int32, (qt, nk), 1) >= (2 - s) * qt
        outs = []
        for pair in range(nh // 2):
            sl = slice(pair * pair_w, (pair + 1) * pair_w)
            q2, k2, v2 = q[:, sl], kcat[:, sl], vcat[:, sl]
            per_head = []
            for sub in range(2):
                sc = lax.dot_general(q2 * keep[sub], k2, (((1,), (1,)), ((), ())), preferred_element_type=F32)
                sc = sc + bias_ref[2 * pair + sub]
                if mask_start:
                    sc = jnp.where(in_seq, sc, NEG_INF)
                mx = jnp.max(sc, axis=-1, keepdims=True)
                p = jnp.exp(sc - mx)
                l = jnp.sum(p, axis=-1, keepdims=True)
                per_head.append(jnp.dot(p.astype(BF16), v2, preferred_element_type=F32) / l)
            outs.append(jnp.where(first, per_head[0], per_head[1]))
        o_ref[0] = jnp.concatenate(outs, axis=-1).astype(BF16)

    @pl.when(s < 2)
    def _():
        attend(True)

    @pl.when(s >= 2)
    def _():
        attend(False)


def _attn_prompt(q, k, v, tab):
    b, s, aw = q.shape
    qt = ATTN_Q_TILE
    nh = aw // HEAD_DIM
    qspec = pl.BlockSpec((1, qt, aw), lambda i, j: (i, j, 0))

    def kspec(back):
        return pl.BlockSpec((1, qt, aw), lambda i, j: (i, jnp.maximum(j - back, 0), 0))

    return pl.pallas_call(
        _attn_kernel,
        grid=(b, s // qt),
        in_specs=[qspec, kspec(2), kspec(1), kspec(0), kspec(2), kspec(1), kspec(0),
                  pl.BlockSpec(tab.shape, lambda i, j: (0, 0))],
        out_specs=qspec,
        out_shape=jax.ShapeDtypeStruct((b, s, aw), BF16),
        scratch_shapes=[pltpu.VMEM((nh, qt, 3 * qt), F32)],
        compiler_params=_params(("arbitrary", "arbitrary")),
        name="attn_prompt",
    )(q, k, k, k, v, v, v, tab)


def _attn_step_kernel(q_ref, kn_ref, vn_ref, ck_ref, cv_ref, tab_ref, o_ref, bias_ref):
    step = pl.program_id(0)
    nbs, t, aw = q_ref.shape
    nh = aw // HEAD_DIM
    rows = nh * t
    r = ck_ref.shape[-1]
    nk = r + LANES
    nt_dims = (((1,), (1,)), ((), ()))

    @pl.when(step == 0)
    def _():
        ok = lax.broadcasted_iota(jnp.int32, (t, nk), 1) < r + t
        for h in range(nh):
            bias_ref[h * t:(h + 1) * t, :] = jnp.where(ok, _toeplitz(tab_ref[h:h + 1, :], t, nk), NEG_INF)

    own = (lax.broadcasted_iota(jnp.int32, (rows, aw), 0) // t
           == lax.broadcasted_iota(jnp.int32, (rows, aw), 1) // HEAD_DIM)
    own_f = jnp.where(own, 1.0, 0.0)
    own_bf = own_f.astype(BF16)
    pad = jnp.zeros((LANES - t, aw), BF16)
    for b in range(nbs):
        q_bd = jnp.concatenate([q_ref[b]] * nh, axis=0) * own_bf
        k_old = ck_ref[b].reshape(aw, r).astype(BF16)
        v_old = cv_ref[b].reshape(aw, r).astype(BF16)
        k_new = jnp.concatenate([kn_ref[b], pad], axis=0)
        v_new = jnp.concatenate([vn_ref[b], pad], axis=0)
        s_old = jnp.dot(q_bd, k_old, preferred_element_type=F32) + bias_ref[:, :r]
        s_new = lax.dot_general(q_bd, k_new, nt_dims, preferred_element_type=F32) + bias_ref[:, r:]
        mx = jnp.maximum(jnp.max(s_old, axis=-1, keepdims=True), jnp.max(s_new, axis=-1, keepdims=True))
        p_old = jnp.exp(s_old - mx)
        p_new = jnp.exp(s_new - mx)
        l = jnp.sum(p_old, axis=-1, keepdims=True) + jnp.sum(p_new, axis=-1, keepdims=True)
        o_all = (lax.dot_general(p_old.astype(BF16), v_old, nt_dims, preferred_element_type=F32)
                 + jnp.dot(p_new.astype(BF16), v_new, preferred_element_type=F32))
        o_all = o_all * own_f / l
        out = o_all[0:t]
        for h in range(1, nh):
            out = out + o_all[h * t:(h + 1) * t]
        o_ref[b] = out.astype(BF16)


def _attn_step(q, kn, vn, ck, cv, tab, nbs):
    b, t, aw = q.shape
    nh = aw // HEAD_DIM
    r = ck.shape[-1]
    new = pl.BlockSpec((nbs, t, aw), lambda i: (i, 0, 0))
    old = pl.BlockSpec((nbs, nh, HEAD_DIM, r), lambda i: (i, 0, 0, 0))
    return pl.pallas_call(
        _attn_step_kernel,
        grid=(b // nbs,),
        in_specs=[new, new, new, old, old, pl.BlockSpec(tab.shape, lambda i: (0, 0))],
        out_specs=new,
        out_shape=jax.ShapeDtypeStruct((b, t, aw), BF16),
        scratch_shapes=[pltpu.VMEM((nh * t, r + LANES), F32)],
        compiler_params=_params(("arbitrary",)),
        name="attn_step",
    )(q, kn, vn, ck, cv, tab)


def _gelu_tanh(x):
    return x * (0.5 * (1.0 + jnp.tanh(0.7978845608028654 * (x + 0.044715 * (x * x * x)))))


def _lru_branch(x, yg, cw_ref, cb_ref, wa_ref, wx_ref, ba_ref, bx_ref, lam_ref, cx_ref, ch_ref):
    nb, tr, c = x.shape
    m = nb * tr
    half = c // 2
    xp = jnp.concatenate([cx_ref[...], x], axis=1)
    new_tail = xp[:, tr:tr + SUBLANES, :]
    groups = tr // SUBLANES
    xg = xp.reshape(nb * (groups + 1), SUBLANES, c)
    first_rows = lax.broadcasted_iota(jnp.int32, (nb, groups, SUBLANES, c), 2)
    y = cb_ref[...] + cw_ref[CONV_WIDTH - 1:CONV_WIDTH, :] * x
    for back in range(1, CONV_WIDTH):
        rot = pltpu.roll(xg, back, 1).reshape(nb, groups + 1, SUBLANES, c)
        shifted = jnp.where(first_rows >= back, rot[:, 1:], rot[:, :groups]).reshape(nb, tr, c)
        y = y + cw_ref[CONV_WIDTH - 1 - back:CONV_WIDTH - back, :] * shifted
    y2 = y.reshape(m, c)
    yb = y2.astype(BF16)

    def gate(w_ref, b_ref):
        g = jnp.concatenate(
            [jnp.dot(yb[:, :half], w_ref[0], preferred_element_type=F32),
             jnp.dot(yb[:, half:], w_ref[1], preferred_element_type=F32)], axis=1)
        return jax.nn.sigmoid(g + b_ref[...])

    rg = gate(wa_ref, ba_ref)
    ig = gate(wx_ref, bx_ref)
    lam = lam_ref[...]
    log_sig = jnp.minimum(lam, 0.0) - jnp.log1p(jnp.exp(-jnp.abs(lam)))
    log_a = rg * (LRU_C * log_sig)
    a_cum = jnp.exp(log_a)
    b_cum = jnp.sqrt(-jnp.tanh(log_a) * (a_cum * a_cum + 1.0)) * (ig * y2)
    a_cum = a_cum.reshape(nb * groups, SUBLANES, c)
    b_cum = b_cum.reshape(nb * groups, SUBLANES, c)
    row = lax.broadcasted_iota(jnp.int32, a_cum.shape, 1)
    dist = 1
    while dist < SUBLANES:
        keep = row >= dist
        a_sh = jnp.where(keep, pltpu.roll(a_cum, dist, 1), 1.0)
        b_sh = jnp.where(keep, pltpu.roll(b_cum, dist, 1), 0.0)
        b_cum = a_cum * b_sh + b_cum
        a_cum = a_cum * a_sh
        dist *= 2
    a_grp = a_cum.reshape(nb, groups, SUBLANES, c)
    b_grp = b_cum.reshape(nb, groups, SUBLANES, c)
    carry = ch_ref[...]
    pieces = []
    for grp in range(groups):
        h_grp = a_grp[:, grp] * carry + b_grp[:, grp]
        carry = h_grp[:, SUBLANES - 1:SUBLANES, :]
        pieces.append(h_grp)
    h = jnp.concatenate(pieces, axis=1)
    ch_ref[...] = carry
    cx_ref[...] = new_tail
    return (h * _gelu_tanh(yg)).astype(BF16), new_tail, carry


def _outproj_kernel(*refs, aliased):
    (at_ref, lr_ref, x_ref, gm_ref, shf_ref, scf_ref, lnf_ref, wo_ref, wr_ref, br_ref) = refs[:10]
    x1_ref, h2_ref, route_ref, cnt_ref = refs[10 + aliased:]
    nb, tr, d = x_ref.shape
    m = nb * tr
    aw = at_ref.shape[-1]
    ne = wr_ref.shape[0]
    at = at_ref[...].reshape(m, aw)
    lr = lr_ref[...].reshape(m, aw)
    mix = (jnp.dot(at, wo_ref[0:aw, :], preferred_element_type=F32)
           + jnp.dot(lr, wo_ref[aw:2 * aw, :], preferred_element_type=F32))
    x1 = x_ref[...] + gm_ref[...] * mix.reshape(nb, tr, d)
    x1_ref[...] = x1
    ms = jnp.mean(x1 * x1, axis=-1, keepdims=True)
    h2 = (x1 * lax.rsqrt(ms + NORM_EPS) * (lnf_ref[...] * (1.0 + scf_ref[...])) + shf_ref[...]).reshape(m, d)
    h2_ref[...] = h2.astype(BF16)

    logits = lax.dot_general(wr_ref[...], h2.astype(BF16), (((1,), (1,)), ((), ())),
                             preferred_element_type=F32) + br_ref[...]
    e_iota = lax.broadcasted_iota(jnp.int32, (ne, m), 0).astype(F32)
    vals = logits
    top_v, sels = [], []
    for k in range(TOP_K):
        mx = jnp.max(vals, axis=0, keepdims=True)
        idx = jnp.min(jnp.where(vals == mx, e_iota, float(ne)), axis=0, keepdims=True)
        sel = e_iota == idx
        vals = jnp.where(sel, -jnp.inf, vals)
        top_v.append(mx)
        sels.append(sel)
        route_ref[0, k:k + 1, :] = idx
    ex = [jnp.exp(v - top_v[0]) for v in top_v]
    den = ex[0] + ex[1] + ex[2] + ex[3]
    chosen = jnp.zeros((ne, m), F32)
    for k in range(TOP_K):
        route_ref[0, 2 * TOP_K + k:2 * TOP_K + k + 1, :] = ex[k] / den
        chosen = chosen + jnp.where(sels[k], 1.0, 0.0)
    before = (lax.broadcasted_iota(jnp.int32, (m, m), 0) < lax.broadcasted_iota(jnp.int32, (m, m), 1))
    rank = jnp.dot(chosen.astype(BF16), jnp.where(before, 1.0, 0.0).astype(BF16), preferred_element_type=F32)
    for k in range(TOP_K):
        route_ref[0, TOP_K + k:TOP_K + k + 1, :] = jnp.sum(jnp.where(sels[k], rank, 0.0), axis=0, keepdims=True)
    route_ref[0, 3 * TOP_K:4 * TOP_K, :] = jnp.zeros((TOP_K, m), F32)
    cnt_ref[0] = jnp.broadcast_to(jnp.sum(chosen, axis=1, keepdims=True), (ne, LANES))


def _outproj(attn, lru_o, x, mod, lnf, w_out_bf, wr_t, br, nb, tr, n_tiles, tile0, prev):
    nbt, t, d = x.shape
    aw = attn.shape[-1]
    m = nb * tr
    assert m == TOKEN_TILE
    ne = wr_t.shape[0]
    tiles_per_seq = t // tr
    xmap = lambda b, i: (b, i, 0)
    c2 = lambda b, i: (0, 0)
    tile = lambda b, i: (tile0 + b * tiles_per_seq + i, 0)
    tile3 = lambda b, i: (tile0 + b * tiles_per_seq + i, 0, 0)
    in_specs = [pl.BlockSpec((nb, tr, aw), xmap), pl.BlockSpec((nb, tr, aw), xmap),
                pl.BlockSpec((nb, tr, d), xmap),
                _mod_spec(nb, d, MOD_GATE_MIX), _mod_spec(nb, d, MOD_SHIFT_FFN), _mod_spec(nb, d, MOD_SCALE_FFN),
                pl.BlockSpec((1, d), c2), pl.BlockSpec(w_out_bf.shape, c2),
                pl.BlockSpec(wr_t.shape, c2), pl.BlockSpec((ne, 1), c2)]
    args = [attn, lru_o, x, mod, mod, mod, lnf, w_out_bf, wr_t, br]
    aliases = {}
    if prev is not None:
        in_specs += [pl.BlockSpec(memory_space=pl.ANY)] * 3
        args += list(prev)
        aliases = {10: 1, 11: 2, 12: 3}
    return pl.pallas_call(
        functools.partial(_outproj_kernel, aliased=len(aliases)),
        grid=(nbt // nb, tiles_per_seq),
        in_specs=in_specs,
        out_specs=[pl.BlockSpec((nb, tr, d), xmap), pl.BlockSpec((m, d), tile),
                   pl.BlockSpec((1, 4 * TOP_K, m), tile3), pl.BlockSpec((1, ne, LANES), tile3)],
        out_shape=[jax.ShapeDtypeStruct((nbt, t, d), F32),
                   jax.ShapeDtypeStruct((n_tiles * m, d), BF16),
                   jax.ShapeDtypeStruct((n_tiles, 4 * TOP_K, m), F32),
                   jax.ShapeDtypeStruct((n_tiles, ne, LANES), F32)],
        input_output_aliases=aliases,
        compiler_params=_params(("arbitrary", "arbitrary")),
        name="outproj",
    )(*args)


def _tile_rows(m):
    cap = TOP_K * m + N_EXPERTS * (BF16_ROWS - 1) + BF16_ROWS
    return -(-cap // TOKEN_TILE) * TOKEN_TILE


def _table_sizes(nt):
    g = BF16_ROWS
    m = TOKEN_TILE
    n_chunks = _tile_rows(m) // g
    n_gap = -(-(N_EXPERTS * (EXPERT_ROWS // g - 1)) // nt)
    bound = TOP_K * m * nt + nt * N_EXPERTS * (g - 1) + N_EXPERTS * (EXPERT_ROWS - g)
    n_sorted = -(-bound // EXPERT_ROWS) * EXPERT_ROWS
    return n_chunks, n_gap, n_sorted, n_sorted + 2 * (n_chunks + n_gap) * g


def _route_tables(cnt):
    nt = cnt.shape[0]
    g = BF16_ROWS
    bm = EXPERT_ROWS
    n_chunks, n_gap, n_sorted, _ = _table_sizes(nt)
    e_ids = jnp.arange(N_EXPERTS, dtype=jnp.int32)
    t_ids = jnp.arange(nt, dtype=jnp.int32)
    upto = (e_ids[:, None] <= e_ids[None, :]).astype(jnp.int32)
    pc = (cnt + g - 1) // g * g
    ctile = jnp.sum(pc[:, :, None] * upto[None], axis=1)
    toff = ctile - pc
    trow = ctile[:, -1]
    tot = jnp.sum(pc, axis=0)
    reg = (tot + bm - 1) // bm * bm
    creg = jnp.sum(reg[:, None] * upto, axis=0)
    base = creg - reg
    earlier = (t_ids[:, None] < t_ids[None, :]).astype(jnp.int32)
    goff = base[None, :] + jnp.sum(pc[:, None, :] * earlier[:, :, None], axis=0)
    r = jnp.arange(n_chunks, dtype=jnp.int32) * g
    r3 = r[None, :, None]
    in_seg = (toff[:, None, :] <= r3) & (r3 < ctile[:, None, :])
    dst = jnp.sum(jnp.where(in_seg, (goff - toff)[:, None, :], 0), axis=2) + r[None, :]
    dst = jnp.where(r[None, :] < trow[:, None], dst, -1)
    gcnt = (reg - tot) // g
    gcum = jnp.sum(gcnt[:, None] * upto, axis=0)
    gstart = gcum - gcnt
    s = jnp.arange(nt * n_gap, dtype=jnp.int32)
    in_gap = (gstart[None, :] <= s[:, None]) & (s[:, None] < gcum[None, :])
    gdst = jnp.sum(jnp.where(in_gap, (base + tot - g * gstart)[None, :] + g * s[:, None], 0), axis=1)
    gdst = jnp.where(s < gcum[-1], gdst, -1).reshape(nt, n_gap)
    table = jnp.concatenate([dst, gdst], axis=1).astype(jnp.int32)
    n_entries = n_chunks + n_gap
    spare = n_sorted + ((t_ids % 2)[:, None] * n_entries + jnp.arange(n_entries, dtype=jnp.int32)[None, :]) * g
    dispatch_tab = jnp.where(table >= 0, table, spare).astype(jnp.int32)
    combine_tab = jnp.maximum(dst, 0).astype(jnp.int32)
    toff_b = jnp.broadcast_to(toff.astype(F32)[:, :, None], (nt, N_EXPERTS, LANES))
    limit = (creg[-1] - EXPERT_UNIT_BLOCKS * bm).astype(jnp.int32).reshape(1)
    nblk = reg // bm
    units = (nblk + EXPERT_UNIT_BLOCKS - 1) // EXPERT_UNIT_BLOCKS
    ubase = jnp.sum(units[:, None] * upto, axis=0) - units
    return (dispatch_tab, combine_tab, toff_b, base.astype(jnp.int32), nblk.astype(jnp.int32),
            ubase.astype(jnp.int32), limit)


def _slot_rows(route_ref, toff_ref, m):
    ne = toff_ref.shape[1]
    e_iota = lax.broadcasted_iota(jnp.int32, (ne, m), 0).astype(F32)
    toff_col = toff_ref[0][:, 0:1]
    pos = []
    for k in range(TOP_K):
        sel = e_iota == route_ref[0, k:k + 1, :]
        start = jnp.sum(jnp.where(sel, toff_col, 0.0), axis=0, keepdims=True)
        pos.append(start + route_ref[0, TOP_K + k:TOP_K + k + 1, :])
    return pos


def _dispatch_kernel(tab_ref, h2_ref, route_ref, toff_ref, xs_hbm, buf_ref, sem, *, n_chunks, n_tiles):
    t = pl.program_id(0)
    slot = t % 2
    m = h2_ref.shape[0]
    rows = buf_ref.shape[1]
    n_entries = tab_ref.shape[1]
    g = BF16_ROWS
    per_chunk = m // g

    def start(c):
        src = c * g if c < n_chunks else rows - g
        pltpu.make_async_copy(
            buf_ref.at[slot, pl.ds(src, g)],
            xs_hbm.at[pl.ds(pl.multiple_of(tab_ref[t, c], g), g)], sem.at[slot]).start()

    def wait_all(which):
        for _ in range(n_entries):
            pltpu.make_async_copy(buf_ref.at[which, pl.ds(0, g)], xs_hbm.at[pl.ds(0, g)], sem.at[which]).wait()

    @pl.when(t >= 2)
    def _():
        wait_all(slot)

    pos = _slot_rows(route_ref, toff_ref, m)
    h2 = h2_ref[...]
    for rc in range(rows // m):
        r_iota = (lax.broadcasted_iota(jnp.int32, (m, m), 0) + rc * m).astype(F32)
        onehot = jnp.zeros((m, m), F32)
        for k in range(TOP_K):
            onehot = jnp.where(r_iota == pos[k], 1.0, onehot)
        buf_ref[slot, rc * m:(rc + 1) * m, :] = jnp.dot(
            onehot.astype(BF16), h2, preferred_element_type=F32).astype(BF16)
        for c in range((rc - 1) * per_chunk, rc * per_chunk) if rc > 0 else ():
            start(c)
    for c in range(n_chunks - per_chunk, n_entries):
        start(c)

    @pl.when(t == n_tiles - 1)
    def _():
        if n_tiles > 1:
            wait_all(1 - slot)
        wait_all(slot)


def _dispatch(table, h2, route, toff_b, n_rows, n_chunks):
    nt = route.shape[0]
    m = TOKEN_TILE
    d = h2.shape[1]
    rows = _tile_rows(m)
    grid_spec = pltpu.PrefetchScalarGridSpec(
        num_scalar_prefetch=1,
        grid=(nt,),
        in_specs=[pl.BlockSpec((m, d), lambda t, tab: (t, 0)),
                  pl.BlockSpec((1, 4 * TOP_K, m), lambda t, tab: (t, 0, 0)),
                  pl.BlockSpec((1, N_EXPERTS, LANES), lambda t, tab: (t, 0, 0))],
        out_specs=pl.BlockSpec(memory_space=pl.ANY),
        scratch_shapes=[pltpu.VMEM((2, rows, d), BF16), pltpu.SemaphoreType.DMA((2,))],
    )
    return pl.pallas_call(
        functools.partial(_dispatch_kernel, n_chunks=n_chunks, n_tiles=nt),
        grid_spec=grid_spec,
        out_shape=jax.ShapeDtypeStruct((n_rows, d), BF16),
        compiler_params=_params(("arbitrary",)),
        name="dispatch",
    )(table, h2, route, toff_b)


def _expert_kernel(row0_ref, nblk_ref, ubase_ref, limit_ref, xs_hbm, wgu_ref, bgu_ref, wdn_ref, bdn_ref, ys_hbm,
                   wgu_bf, wdn_bf, xbuf, ybuf, sem_in, sem_out, pend_ref):
    e = pl.program_id(0)
    ne = pl.num_programs(0)
    bm = EXPERT_ROWS
    unit = xbuf.shape[1]
    per_unit = unit // bm
    dff = wdn_ref.shape[1]
    nblk = nblk_ref[e]
    n_units = (nblk + per_unit - 1) // per_unit
    base = ubase_ref[e]

    def unit_start(expert, s):
        true = row0_ref[expert] + s * unit
        start = jnp.minimum(true, limit_ref[0])
        return pl.multiple_of(start, bm), pl.multiple_of(true - start, bm)

    def in_copy(expert, s, slot):
        start, _ = unit_start(expert, s)
        return pltpu.make_async_copy(xs_hbm.at[pl.ds(start, unit)], xbuf.at[slot], sem_in.at[slot])

    def out_copy(j, slot, i):
        start = pl.multiple_of(row0_ref[e] + j * bm, bm)
        return pltpu.make_async_copy(ybuf.at[slot, pl.ds(i * bm, bm)], ys_hbm.at[pl.ds(start, bm)],
                                     sem_out.at[slot])

    def wait_pending(slot):
        count = pend_ref[slot]

        @pl.when(count == per_unit)
        def _():
            for i in range(per_unit):
                out_copy(0, slot, i).wait()

        for i in range(per_unit - 1):
            @pl.when((count < per_unit) & (i < count))
            def _():
                out_copy(0, slot, i).wait()
        pend_ref[slot] = 0

    @pl.when(e == 0)
    def _():
        pend_ref[0] = 0
        pend_ref[1] = 0

    @pl.when((e == 0) & (nblk > 0))
    def _():
        in_copy(e, 0, 0).start()

    wgu_bf[...] = wgu_ref[0].astype(BF16)
    wdn_bf[...] = wdn_ref[0].astype(BF16)

    def run_unit(s, carry):
        slot = (base + s) % 2
        in_copy(e, s, slot).wait()

        @pl.when(s + 1 < n_units)
        def _():
            in_copy(e, s + 1, 1 - slot).start()

        wait_pending(slot)
        _, lead = unit_start(e, s)

        def ffn(first, count):
            rows = count * bm
            x = xbuf[slot, pl.ds(pl.multiple_of(lead + first * bm, bm), rows), :]
            gu = jnp.dot(x, wgu_bf[...], preferred_element_type=F32) + bgu_ref[0]
            gate = jnp.minimum(gu[:, :dff], SWIGLU_LIMIT)
            up = jnp.clip(gu[:, dff:], -SWIGLU_LIMIT, SWIGLU_LIMIT)
            glu = gate * jax.nn.sigmoid(gate * SWIGLU_ALPHA)
            act = ((up + 1.0) * glu).astype(BF16)
            ybuf[slot, first * bm:first * bm + rows, :] = (
                jnp.dot(act, wdn_bf[...], preferred_element_type=F32) + bdn_ref[0]).astype(BF16)
            for i in range(first, first + count):
                out_copy(s * per_unit + i, slot, i).start()

        for first in range(0, per_unit, 2):
            have = nblk - s * per_unit - first

            @pl.when(have >= 2)
            def _():
                ffn(first, 2)

            @pl.when(have == 1)
            def _():
                ffn(first, 1)
        pend_ref[slot] = jnp.minimum(nblk - s * per_unit, per_unit)
        return carry

    lax.fori_loop(0, n_units, run_unit, 0)

    nxt = jnp.minimum(e + 1, ne - 1)

    @pl.when((e + 1 < ne) & (nblk_ref[nxt] > 0))
    def _():
        in_copy(nxt, 0, (base + n_units) % 2).start()

    @pl.when(e == ne - 1)
    def _():
        wait_pending(0)
        wait_pending(1)


def _experts(row0, nblk, ubase, limit, xs, w_gu, b_gu, w_dn, b_dn):
    n_rows, d = xs.shape
    bm = EXPERT_ROWS
    unit = EXPERT_UNIT_BLOCKS * bm
    ne, _, dff2 = w_gu.shape
    dff = w_dn.shape[1]
    exp3 = lambda e, r0, nb, ub, lim: (e, 0, 0)
    grid_spec = pltpu.PrefetchScalarGridSpec(
        num_scalar_prefetch=4,
        grid=(ne,),
        in_specs=[pl.BlockSpec(memory_space=pl.ANY),
                  pl.BlockSpec((1, d, dff2), exp3), pl.BlockSpec((1, 1, dff2), exp3),
                  pl.BlockSpec((1, dff, d), exp3), pl.BlockSpec((1, 1, d), exp3)],
        out_specs=pl.BlockSpec(memory_space=pl.ANY),
        scratch_shapes=[pltpu.VMEM((d, dff2), BF16), pltpu.VMEM((dff, d), BF16),
                        pltpu.VMEM((2, unit, d), BF16), pltpu.VMEM((2, unit, d), BF16),
                        pltpu.SemaphoreType.DMA((2,)), pltpu.SemaphoreType.DMA((2,)),
                        pltpu.SMEM((2,), jnp.int32)],
    )
    return pl.pallas_call(
        _expert_kernel,
        grid_spec=grid_spec,
        out_shape=jax.ShapeDtypeStruct((n_rows, d), BF16),
        compiler_params=_params(("arbitrary",)),
        name="experts",
    )(row0, nblk, ubase, limit, xs, w_gu, b_gu.reshape(ne, 1, dff2), w_dn, b_dn.reshape(ne, 1, d))


def _combine_kernel(tab_ref, ys_hbm, route_ref, toff_ref, x1_ref, gf_ref, o_ref, buf_ref, sem, *, n_chunks, tile0):
    step = pl.program_id(0)
    n_steps = pl.num_programs(0)
    t = tile0 + step
    slot = step % 2
    nb, tr, d = x1_ref.shape
    m = nb * tr
    rows = buf_ref.shape[1]
    g = BF16_ROWS

    def fetch(tile, which):
        for c in range(n_chunks):
            pltpu.make_async_copy(
                ys_hbm.at[pl.ds(pl.multiple_of(tab_ref[tile, c], g), g)],
                buf_ref.at[which, pl.ds(c * g, g)], sem.at[which]).start()

    @pl.when(step == 0)
    def _():
        fetch(t, slot)

    @pl.when(step + 1 < n_steps)
    def _():
        fetch(t + 1, 1 - slot)

    pos = _slot_rows(route_ref, toff_ref, m)
    gates = [route_ref[0, 2 * TOP_K + k:2 * TOP_K + k + 1, :] for k in range(TOP_K)]
    stacked = jnp.concatenate(pos + gates + [jnp.zeros((LANES - 2 * TOP_K, m), F32)], axis=0)
    cols = stacked.T
    for _ in range(n_chunks):
        pltpu.make_async_copy(ys_hbm.at[pl.ds(0, g)], buf_ref.at[slot, pl.ds(0, g)], sem.at[slot]).wait()

    acc = jnp.zeros((m, d), F32)
    for rc in range(rows // m):
        c_iota = (lax.broadcasted_iota(jnp.int32, (m, m), 1) + rc * m).astype(F32)
        weights = jnp.zeros((m, m), F32)
        for k in range(TOP_K):
            weights = jnp.where(c_iota == cols[:, k:k + 1], cols[:, TOP_K + k:TOP_K + k + 1], weights)
        acc = acc + jnp.dot(weights.astype(BF16), buf_ref[slot, rc * m:(rc + 1) * m, :],
                            preferred_element_type=F32)
    o_ref[...] = x1_ref[...] + gf_ref[...] * acc.reshape(nb, tr, d)


def _combine(table, ys, route, toff_b, x1, gf, nb, tr, tile0, n_chunks):
    nbt, t, d = x1.shape
    m = nb * tr
    assert m == TOKEN_TILE
    rows = _tile_rows(m)
    tiles_per_seq = t // tr
    n_steps = (nbt // nb) * tiles_per_seq
    xmap = lambda s, tab: (s // tiles_per_seq, s % tiles_per_seq, 0)
    grid_spec = pltpu.PrefetchScalarGridSpec(
        num_scalar_prefetch=1,
        grid=(n_steps,),
        in_specs=[pl.BlockSpec(memory_space=pl.ANY),
                  pl.BlockSpec((1, 4 * TOP_K, m), lambda s, tab: (tile0 + s, 0, 0)),
                  pl.BlockSpec((1, N_EXPERTS, LANES), lambda s, tab: (tile0 + s, 0, 0)),
                  pl.BlockSpec((nb, tr, d), xmap),
                  pl.BlockSpec((nb, 1, d), lambda s, tab: (s // tiles_per_seq, 0, MOD_GATE_FFN))],
        out_specs=pl.BlockSpec((nb, tr, d), xmap),
        scratch_shapes=[pltpu.VMEM((2, rows, d), BF16), pltpu.SemaphoreType.DMA((2,))],
    )
    return pl.pallas_call(
        functools.partial(_combine_kernel, n_chunks=n_chunks, tile0=tile0),
        grid_spec=grid_spec,
        out_shape=jax.ShapeDtypeStruct((nbt, t, d), F32),
        compiler_params=_params(("arbitrary",)),
        name="combine",
    )(table, ys, route, toff_b, x1, gf)


def _block_diag(w, groups):
    n, k, _ = w.shape
    w = w.reshape(n // groups, groups, k, k)
    eye = jnp.eye(groups, dtype=w.dtype)
    return jnp.einsum("ngij,gh->ngihj", w, eye).reshape(n // groups, groups * k, groups * k)


def _layer(xp, xs, mod_p, mod_s, k_cache, v_cache, conv_state, lru_state, lw):
    (ln_mix, ln_ffn, w_in, q_norm, k_norm, rel_bias, conv_w, conv_b, w_rg, b_rg, w_ig, b_ig, lam,
     w_out, w_router, b_router, w_gu, b_gu, w_dn, b_dn) = lw
    bp, s, d = xp.shape
    bs, ts, _ = xs.shape
    aw = w_out.shape[0] // 2
    nh = aw // HEAD_DIM
    m = TOKEN_TILE
    assert s % m == 0 and bs * ts == m and s % ATTN_Q_TILE == 0

    w_in_bf = w_in.astype(BF16)
    w_out_bf = w_out.astype(BF16)
    qn_t = jnp.tile(q_norm * (HEAD_DIM ** -0.5), nh).reshape(1, aw)
    kn_t = jnp.tile(k_norm, nh).reshape(1, aw)
    head_mean = _block_diag(jnp.full((nh, HEAD_DIM, HEAD_DIM), 1.0 / HEAD_DIM, F32), nh)[0].astype(BF16)
    groups = MXU_DIM // w_rg.shape[-1]
    wa_bd = _block_diag(w_rg, groups).astype(BF16)
    wx_bd = _block_diag(w_ig, groups).astype(BF16)
    lw_c = b_rg.size
    b_a = b_rg.reshape(1, lw_c)
    b_x = b_ig.reshape(1, lw_c)
    lam2 = lam.reshape(1, lw_c)
    cb2 = conv_b.reshape(1, lw_c)
    ln_mix2 = ln_mix.reshape(1, d)
    ln_ffn2 = ln_ffn.reshape(1, d)
    wr_t = w_router.T.astype(BF16)
    br = b_router.reshape(-1, 1)
    tab_p = _bias_table(rel_bias, 3 * ATTN_Q_TILE - 1)
    r_cache = k_cache.shape[1]
    tab_s = _bias_table(rel_bias, r_cache + ts - 1)

    zeros_pre = jnp.zeros((bp, SUBLANES, lw_c), F32)
    zeros_h = jnp.zeros((bp, 1, lw_c), F32)
    pre_s = jnp.pad(conv_state, ((0, 0), (SUBLANES - (CONV_WIDTH - 1), 0), (0, 0)))
    lru_w = (conv_w, cb2, wa_bd, wx_bd, b_a, b_x, lam2)
    qp, kp, vp, k32p, v32p, lru_p, tail_p, hl_p = _mixin(
        xp, mod_p,ln_mix2, w_in_bf, qn_t, kn_t, head_mean, zeros_pre, zeros_h, *lru_w, 1, m)
    qs, ks, vs, k32s, v32s, lru_s, tail_s, hl_s = _mixin(
        xs, mod_s,ln_mix2, w_in_bf, qn_t, kn_t, head_mean, pre_s, lru_state[:, None, :], *lru_w, bs, ts)
    attn_p = _attn_prompt(qp, kp, vp, tab_p)
    attn_s = _attn_step(qs, ks, vs, jnp.transpose(k_cache, (0, 2, 3, 1)), jnp.transpose(v_cache, (0, 2, 3, 1)),
                        tab_s, ATTN_STEP_BATCH)

    n_tiles = bp * (s // m) + 1
    x1p, h2, route, cnt = _outproj(attn_p, lru_p, xp, mod_p,ln_ffn2, w_out_bf, wr_t, br,
                                   1, m, n_tiles, 0, None)
    x1s, h2, route, cnt = _outproj(attn_s, lru_s, xs, mod_s,ln_ffn2, w_out_bf, wr_t, br,
                                   bs, ts, n_tiles, n_tiles - 1, (h2, route, cnt))

    n_chunks, _, _, n_rows = _table_sizes(n_tiles)
    assert TOP_K * m * n_tiles >= EXPERT_UNIT_BLOCKS * EXPERT_ROWS
    dispatch_tab, combine_tab, toff_b, row0, nblk, ubase, limit = _route_tables(cnt[:, :, 0].astype(jnp.int32))
    xs_sorted = _dispatch(dispatch_tab, h2, route, toff_b, n_rows, n_chunks)
    ys_sorted = _experts(row0, nblk, ubase, limit, xs_sorted, w_gu, b_gu, w_dn, b_dn)
    yp = _combine(combine_tab, ys_sorted, route, toff_b, x1p, mod_p, 1, m, 0, n_chunks)
    ysm = _combine(combine_tab, ys_sorted, route, toff_b, x1s, mod_s, bs, ts, n_tiles - 1, n_chunks)

    keep = k32p.shape[1]
    new = (k32p.reshape(bp, keep, nh, HEAD_DIM), v32p.reshape(bp, keep, nh, HEAD_DIM),
           tail_p[:, SUBLANES - (CONV_WIDTH - 1):, :], hl_p[:, 0, :],
           k32s.reshape(bs, ts, nh, HEAD_DIM), v32s.reshape(bs, ts, nh, HEAD_DIM),
           tail_s[:, SUBLANES - (CONV_WIDTH - 1):, :], hl_s[:, 0, :])
    return yp, ysm, new


def kernel(x_prompt, x_sample, c_prompt, c_sample, cache_k, cache_v, state_conv, state_lru, ln_mix_w, ln_ffn_w, w_ada, b_ada, w_in, q_norm_w, k_norm_w, rel_bias, conv_w, conv_b, w_rgate, b_rgate, w_igate, b_igate, lru_lambda, w_out, w_router, b_router, w_gate_up, b_gate_up, w_down, b_down):
    depth = w_in.shape[0]
    yp, ys = x_prompt, x_sample
    collected = [[] for _ in range(8)]
    for l in range(depth):
        mod_p, mod_s = _ada(c_prompt, c_sample, w_ada[l], b_ada[l])
        lw = (ln_mix_w[l], ln_ffn_w[l], w_in[l], q_norm_w[l], k_norm_w[l], rel_bias[l], conv_w[l], conv_b[l],
              w_rgate[l], b_rgate[l], w_igate[l], b_igate[l], lru_lambda[l], w_out[l], w_router[l], b_router[l],
              w_gate_up[l], b_gate_up[l], w_down[l], b_down[l])
        yp, ys, new = _layer(yp, ys, mod_p, mod_s, cache_k[l], cache_v[l], state_conv[l], state_lru[l], lw)
        for acc, val in zip(collected, new):
            acc.append(val)
    return (yp, ys) + tuple(jnp.stack(vals) for vals in collected)
```

```python
import functools

import jax
import jax.numpy as jnp
from jax import lax
from jax.experimental import pallas as pl
from jax.experimental.pallas import tpu as pltpu

F32 = jnp.float32
BF16 = jnp.bfloat16

CHUNK = 64
N_LEFT_CHUNKS = 8
ATTN_WINDOW = N_LEFT_CHUNKS * CHUNK
HEAD_DIM = 64
REL_CLIP = 128
CONV_WIDTH = 4
LRU_C = 8.0
N_EXPERTS = 32
TOP_K = 4
SWIGLU_LIMIT = 7.0
SWIGLU_ALPHA = 1.702
NORM_EPS = 1e-6
NEG_INF = -1e30

LANES = 128
SUBLANES = 8
BF16_ROWS = 16
MXU_DIM = 256

TOKEN_TILE = 512
ATTN_Q_TILE = 256
ATTN_STEP_BATCH = 4
EXPERT_ROWS = 256
EXPERT_UNIT_BLOCKS = 4
BIAS_TABLE = 1024
VMEM_LIMIT = 56 * 1024 * 1024


def _params(sem, vmem=VMEM_LIMIT):
    return pltpu.CompilerParams(dimension_semantics=sem, vmem_limit_bytes=vmem)


def _ada_kernel(cp_ref, cs_ref, w_ref, b_ref, op_ref, os_ref):
    w = w_ref[...].astype(BF16)
    for c_ref, o_ref in ((cp_ref, op_ref), (cs_ref, os_ref)):
        c = c_ref[...]
        s = (c * jax.nn.sigmoid(c)).astype(BF16)
        o_ref[...] = jnp.dot(s, w, preferred_element_type=F32) + b_ref[...]


def _ada(c_p, c_s, w_ada, b_ada):
    (n_p, d), n_s = c_p.shape, c_s.shape[0]
    nout = w_ada.shape[1]
    tn = 1024
    mod_p, mod_s = pl.pallas_call(
        _ada_kernel,
        grid=(nout // tn,),
        in_specs=[pl.BlockSpec((n_p, d), lambda j: (0, 0)), pl.BlockSpec((n_s, d), lambda j: (0, 0)),
                  pl.BlockSpec((d, tn), lambda j: (0, j)),
                  pl.BlockSpec((1, tn), lambda j: (0, j))],
        out_specs=[pl.BlockSpec((n_p, tn), lambda j: (0, j)), pl.BlockSpec((n_s, tn), lambda j: (0, j))],
        out_shape=[jax.ShapeDtypeStruct((n_p, nout), F32), jax.ShapeDtypeStruct((n_s, nout), F32)],
        compiler_params=_params(("arbitrary",)),
        name="ada",
    )(c_p, c_s, w_ada, b_ada.reshape(1, nout))
    return mod_p.reshape(n_p, 1, nout), mod_s.reshape(n_s, 1, nout)


MOD_SHIFT_MIX, MOD_SCALE_MIX, MOD_GATE_MIX, MOD_SHIFT_FFN, MOD_SCALE_FFN, MOD_GATE_FFN = range(6)


def _mod_spec(nb, d, term):
    return pl.BlockSpec((nb, 1, d), lambda b, i: (b, 0, term))


def _mixin_kernel(x_ref, sh_ref, sc_ref, ln_ref, win_ref, qn_ref, kn_ref, bd_ref,
                  pre_ref, h0_ref, cw_ref, cb_ref, wa_ref, wx_ref, ba_ref, bx_ref, lam_ref,
                  q_ref, k_ref, v_ref, k32_ref, v32_ref, lru_ref, tail_ref, hl_ref, cx_ref, ch_ref):
    nb, tr, d = x_ref.shape
    m = nb * tr
    aw = q_ref.shape[-1]

    @pl.when(pl.program_id(1) == 0)
    def _():
        cx_ref[...] = pre_ref[...]
        ch_ref[...] = h0_ref[...]

    x = x_ref[...]
    ms = jnp.mean(x * x, axis=-1, keepdims=True)
    h = x * lax.rsqrt(ms + NORM_EPS) * (ln_ref[...] * (1.0 + sc_ref[...])) + sh_ref[...]
    hb = h.reshape(m, d).astype(BF16)

    def proj(part):
        return jnp.dot(hb, win_ref[:, part * aw:(part + 1) * aw], preferred_element_type=F32)

    def head_norm(t, w_ref):
        msq = jnp.dot((t * t).astype(BF16), bd_ref[...], preferred_element_type=F32)
        return t * lax.rsqrt(msq + NORM_EPS) * w_ref[...]

    lru_out, new_tail, h_last = _lru_branch(
        proj(3).reshape(nb, tr, aw), proj(4).reshape(nb, tr, aw), cw_ref, cb_ref,
        wa_ref, wx_ref, ba_ref, bx_ref, lam_ref, cx_ref, ch_ref)
    lru_ref[...] = lru_out
    tail_ref[...] = new_tail
    hl_ref[...] = h_last
    q = head_norm(proj(0), qn_ref)
    k = head_norm(proj(1), kn_ref)
    v = proj(2)
    q_ref[...] = q.astype(BF16).reshape(nb, tr, aw)
    k_ref[...] = k.astype(BF16).reshape(nb, tr, aw)
    v_ref[...] = v.astype(BF16).reshape(nb, tr, aw)
    k32_ref[...] = k.reshape(nb, tr, aw)
    v32_ref[...] = v.reshape(nb, tr, aw)


def _mixin(x, mod, ln_w, w_in_bf, qn_t, kn_t, bd, pre, h0, conv_w, conv_b, wa_bd, wx_bd, b_a, b_x, lam,
           nb, tr):
    nbt, t, d = x.shape
    aw = qn_t.shape[-1]
    c = pre.shape[-1]
    assert c == aw
    keep = min(ATTN_WINDOW, t)
    assert tr == keep or t == tr
    grid = (nbt // nb, t // tr)
    xmap = lambda b, i: (b, i, 0)
    mmap = lambda b, i: (b, 0, 0)
    cmap = lambda b, i: (0, 0)
    cmap3 = lambda b, i: (0, 0, 0)
    tmap = lambda b, i: (b, 0, 0)
    big = pl.BlockSpec((nb, tr, aw), xmap)
    tail = pl.BlockSpec((nb, keep, aw), tmap)
    row = pl.BlockSpec((1, c), cmap)
    return pl.pallas_call(
        _mixin_kernel,
        grid=grid,
        in_specs=[pl.BlockSpec((nb, tr, d), xmap),
                  _mod_spec(nb, d, MOD_SHIFT_MIX), _mod_spec(nb, d, MOD_SCALE_MIX),
                  pl.BlockSpec((1, d), cmap),
                  pl.BlockSpec(w_in_bf.shape, cmap),
                  pl.BlockSpec((1, aw), cmap), pl.BlockSpec((1, aw), cmap),
                  pl.BlockSpec(bd.shape, cmap),
                  pl.BlockSpec((nb, SUBLANES, c), tmap), pl.BlockSpec((nb, 1, c), tmap),
                  pl.BlockSpec(conv_w.shape, cmap), row,
                  pl.BlockSpec(wa_bd.shape, cmap3), pl.BlockSpec(wx_bd.shape, cmap3),
                  row, row, row],
        out_specs=[big, big, big, tail, tail, big,
                   pl.BlockSpec((nb, SUBLANES, c), tmap), pl.BlockSpec((nb, 1, c), tmap)],
        out_shape=[jax.ShapeDtypeStruct((nbt, t, aw), BF16)] * 3
        + [jax.ShapeDtypeStruct((nbt, keep, aw), F32)] * 2
        + [jax.ShapeDtypeStruct((nbt, t, c), BF16),
           jax.ShapeDtypeStruct((nbt, SUBLANES, c), F32),
           jax.ShapeDtypeStruct((nbt, 1, c), F32)],
        scratch_shapes=[pltpu.VMEM((nb, SUBLANES, c), F32), pltpu.VMEM((nb, 1, c), F32)],
        compiler_params=_params(("arbitrary", "arbitrary")),
        name="mixin",
    )(x, mod, mod, ln_w, w_in_bf, qn_t, kn_t, bd, pre, h0, conv_w, conv_b, wa_bd, wx_bd, b_a, b_x, lam)


def _bias_table(rel_bias, off):
    h = rel_bias.shape[0]
    left = off - REL_CLIP
    right = BIAS_TABLE - left - (2 * REL_CLIP + 1)
    assert left >= 0 and right >= 0
    return jnp.concatenate([jnp.broadcast_to(rel_bias[:, :1], (h, left)), rel_bias,
                            jnp.broadcast_to(rel_bias[:, -1:], (h, right))], axis=1)


def _toeplitz(tab_row, rows, cols):
    t = jnp.broadcast_to(tab_row, (rows, BIAS_TABLE))
    t = pltpu.roll(t, BIAS_TABLE - (rows - 1), 1, stride=1, stride_axis=0)
    return t[:, :cols]


def _attn_kernel(q_ref, k0_ref, k1_ref, k2_ref, v0_ref, v1_ref, v2_ref, tab_ref, o_ref, bias_ref):
    b = pl.program_id(0)
    s = pl.program_id(1)
    qt = q_ref.shape[1]
    nk = 3 * qt
    nh = bias_ref.shape[0]

    @pl.when((b == 0) & (s == 0))
    def _():
        qi = lax.broadcasted_iota(jnp.int32, (qt, nk), 0) // CHUNK
        kc = lax.broadcasted_iota(jnp.int32, (qt, nk), 1) // CHUNK
        for h in range(nh):
            band = jnp.where(kc <= qi + N_LEFT_CHUNKS, _toeplitz(tab_ref[h:h + 1, :], qt, nk), NEG_INF)
            bias_ref[h] = jnp.where(kc >= qi, band, NEG_INF)

    pair_w = 2 * HEAD_DIM

    def attend(mask_start):
        q = q_ref[0]
        kcat = jnp.concatenate([k0_ref[0], k1_ref[0], k2_ref[0]], axis=0)
        vcat = jnp.concatenate([v0_ref[0], v1_ref[0], v2_ref[0]], axis=0)
        first = lax.broadcasted_iota(jnp.int32, (qt, pair_w), 1) < HEAD_DIM
        keep = [jnp.where(first, 1.0, 0.0).astype(BF16), jnp.where(first, 0.0, 1.0).astype(BF16)]
        if mask_start:
            in_seq = lax.broadcasted_iota(jnp.int32, (qt, nk), 1) >= (2 - s) * qt
        outs = []
        for pair in range(nh // 2):
            sl = slice(pair * pair_w, (pair + 1) * pair_w)
            q2, k2, v2 = q[:, sl], kcat[:, sl], vcat[:, sl]
            per_head = []
            for sub in range(2):
                sc = lax.dot_general(q2 * keep[sub], k2, (((1,), (1,)), ((), ())), preferred_element_type=F32)
                sc = sc + bias_ref[2 * pair + sub]
                if mask_start:
                    sc = jnp.where(in_seq, sc, NEG_INF)
                mx = jnp.max(sc, axis=-1, keepdims=True)
                p = jnp.exp(sc - mx)
                l = jnp.sum(p, axis=-1, keepdims=True)
                per_head.append(jnp.dot(p.astype(BF16), v2, preferred_element_type=F32) / l)
            outs.append(jnp.where(first, per_head[0], per_head[1]))
        o_ref[0] = jnp.concatenate(outs, axis=-1).astype(BF16)

    @pl.when(s < 2)
    def _():
        attend(True)

    @pl.when(s >= 2)
    def _():
        attend(False)


def _attn_prompt(q, k, v, tab):
    b, s, aw = q.shape
    qt = ATTN_Q_TILE
    nh = aw // HEAD_DIM
    qspec = pl.BlockSpec((1, qt, aw), lambda i, j: (i, j, 0))

    def kspec(back):
        return pl.BlockSpec((1, qt, aw), lambda i, j: (i, jnp.maximum(j - back, 0), 0))

    return pl.pallas_call(
        _attn_kernel,
        grid=(b, s // qt),
        in_specs=[qspec, kspec(2), kspec(1), kspec(0), kspec(2), kspec(1), kspec(0),
                  pl.BlockSpec(tab.shape, lambda i, j: (0, 0))],
        out_specs=qspec,
        out_shape=jax.ShapeDtypeStruct((b, s, aw), BF16),
        scratch_shapes=[pltpu.VMEM((nh, qt, 3 * qt), F32)],
        compiler_params=_params(("arbitrary", "arbitrary")),
        name="attn_prompt",
    )(q, k, k, k, v, v, v, tab)


def _attn_step_kernel(q_ref, kn_ref, vn_ref, ck_ref, cv_ref, tab_ref, o_ref, bias_ref):
    step = pl.program_id(0)
    nbs, t, aw = q_ref.shape
    nh = aw // HEAD_DIM
    rows = nh * t
    r = ck_ref.shape[-1]
    nk = r + LANES
    nt_dims = (((1,), (1,)), ((), ()))

    @pl.when(step == 0)
    def _():
        ok = lax.broadcasted_iota(jnp.int32, (t, nk), 1) < r + t
        for h in range(nh):
            bias_ref[h * t:(h + 1) * t, :] = jnp.where(ok, _toeplitz(tab_ref[h:h + 1, :], t, nk), NEG_INF)

    own = (lax.broadcasted_iota(jnp.int32, (rows, aw), 0) // t
           == lax.broadcasted_iota(jnp.int32, (rows, aw), 1) // HEAD_DIM)
    own_f = jnp.where(own, 1.0, 0.0)
    own_bf = own_f.astype(BF16)
    pad = jnp.zeros((LANES - t, aw), BF16)
    for b in range(nbs):
        q_bd = jnp.concatenate([q_ref[b]] * nh, axis=0) * own_bf
        k_old = ck_ref[b].reshape(aw, r).astype(BF16)
        v_old = cv_ref[b].reshape(aw, r).astype(BF16)
        k_new = jnp.concatenate([kn_ref[b], pad], axis=0)
        v_new = jnp.concatenate([vn_ref[b], pad], axis=0)
        s_old = jnp.dot(q_bd, k_old, preferred_element_type=F32) + bias_ref[:, :r]
        s_new = lax.dot_general(q_bd, k_new, nt_dims, preferred_element_type=F32) + bias_ref[:, r:]
        mx = jnp.maximum(jnp.max(s_old, axis=-1, keepdims=True), jnp.max(s_new, axis=-1, keepdims=True))
        p_old = jnp.exp(s_old - mx)
        p_new = jnp.exp(s_new - mx)
        l = jnp.sum(p_old, axis=-1, keepdims=True) + jnp.sum(p_new, axis=-1, keepdims=True)
        o_all = (lax.dot_general(p_old.astype(BF16), v_old, nt_dims, preferred_element_type=F32)
                 + jnp.dot(p_new.astype(BF16), v_new, preferred_element_type=F32))
        o_all = o_all * own_f / l
        out = o_all[0:t]
        for h in range(1, nh):
            out = out + o_all[h * t:(h + 1) * t]
        o_ref[b] = out.astype(BF16)


def _attn_step(q, kn, vn, ck, cv, tab, nbs):
    b, t, aw = q.shape
    nh = aw // HEAD_DIM
    r = ck.shape[-1]
    new = pl.BlockSpec((nbs, t, aw), lambda i: (i, 0, 0))
    old = pl.BlockSpec((nbs, nh, HEAD_DIM, r), lambda i: (i, 0, 0, 0))
    return pl.pallas_call(
        _attn_step_kernel,
        grid=(b // nbs,),
        in_specs=[new, new, new, old, old, pl.BlockSpec(tab.shape, lambda i: (0, 0))],
        out_specs=new,
        out_shape=jax.ShapeDtypeStruct((b, t, aw), BF16),
        scratch_shapes=[pltpu.VMEM((nh * t, r + LANES), F32)],
        compiler_params=_params(("arbitrary",)),
        name="attn_step",
    )(q, kn, vn, ck, cv, tab)


def _gelu_tanh(x):
    return x * (0.5 * (1.0 + jnp.tanh(0.7978845608028654 * (x + 0.044715 * (x * x * x)))))


def _lru_branch(x, yg, cw_ref, cb_ref, wa_ref, wx_ref, ba_ref, bx_ref, lam_ref, cx_ref, ch_ref):
    nb, tr, c = x.shape
    m = nb * tr
    half = c // 2
    xp = jnp.concatenate([cx_ref[...], x], axis=1)
    new_tail = xp[:, tr:tr + SUBLANES, :]
    groups = tr // SUBLANES
    xg = xp.reshape(nb * (groups + 1), SUBLANES, c)
    first_rows = lax.broadcasted_iota(jnp.int32, (nb, groups, SUBLANES, c), 2)
    y = cb_ref[...] + cw_ref[CONV_WIDTH - 1:CONV_WIDTH, :] * x
    for back in range(1, CONV_WIDTH):
        rot = pltpu.roll(xg, back, 1).reshape(nb, groups + 1, SUBLANES, c)
        shifted = jnp.where(first_rows >= back, rot[:, 1:], rot[:, :groups]).reshape(nb, tr, c)
        y = y + cw_ref[CONV_WIDTH - 1 - back:CONV_WIDTH - back, :] * shifted
    y2 = y.reshape(m, c)
    yb = y2.astype(BF16)

    def gate(w_ref, b_ref):
        g = jnp.concatenate(
            [jnp.dot(yb[:, :half], w_ref[0], preferred_element_type=F32),
             jnp.dot(yb[:, half:], w_ref[1], preferred_element_type=F32)], axis=1)
        return jax.nn.sigmoid(g + b_ref[...])

    rg = gate(wa_ref, ba_ref)
    ig = gate(wx_ref, bx_ref)
    lam = lam_ref[...]
    log_sig = jnp.minimum(lam, 0.0) - jnp.log1p(jnp.exp(-jnp.abs(lam)))
    log_a = rg * (LRU_C * log_sig)
    a_cum = jnp.exp(log_a)
    b_cum = jnp.sqrt(-jnp.tanh(log_a) * (a_cum * a_cum + 1.0)) * (ig * y2)
    a_cum = a_cum.reshape(nb * groups, SUBLANES, c)
    b_cum = b_cum.reshape(nb * groups, SUBLANES, c)
    row = lax.broadcasted_iota(jnp.int32, a_cum.shape, 1)
    dist = 1
    while dist < SUBLANES:
        keep = row >= dist
        a_sh = jnp.where(keep, pltpu.roll(a_cum, dist, 1), 1.0)
        b_sh = jnp.where(keep, pltpu.roll(b_cum, dist, 1), 0.0)
        b_cum = a_cum * b_sh + b_cum
        a_cum = a_cum * a_sh
        dist *= 2
    a_grp = a_cum.reshape(nb, groups, SUBLANES, c)
    b_grp = b_cum.reshape(nb, groups, SUBLANES, c)
    carry = ch_ref[...]
    pieces = []
    for grp in range(groups):
        h_grp = a_grp[:, grp] * carry + b_grp[:, grp]
        carry = h_grp[:, SUBLANES - 1:SUBLANES, :]
        pieces.append(h_grp)
    h = jnp.concatenate(pieces, axis=1)
    ch_ref[...] = carry
    cx_ref[...] = new_tail
    return (h * _gelu_tanh(yg)).astype(BF16), new_tail, carry


def _outproj_kernel(*refs, aliased):
    (at_ref, lr_ref, x_ref, gm_ref, shf_ref, scf_ref, lnf_ref, wo_ref, wr_ref, br_ref) = refs[:10]
    x1_ref, h2_ref, route_ref, cnt_ref = refs[10 + aliased:]
    nb, tr, d = x_ref.shape
    m = nb * tr
    aw = at_ref.shape[-1]
    ne = wr_ref.shape[0]
    at = at_ref[...].reshape(m, aw)
    lr = lr_ref[...].reshape(m, aw)
    mix = (jnp.dot(at, wo_ref[0:aw, :], preferred_element_type=F32)
           + jnp.dot(lr, wo_ref[aw:2 * aw, :], preferred_element_type=F32))
    x1 = x_ref[...] + gm_ref[...] * mix.reshape(nb, tr, d)
    x1_ref[...] = x1
    ms = jnp.mean(x1 * x1, axis=-1, keepdims=True)
    h2 = (x1 * lax.rsqrt(ms + NORM_EPS) * (lnf_ref[...] * (1.0 + scf_ref[...])) + shf_ref[...]).reshape(m, d)
    h2_ref[...] = h2.astype(BF16)

    logits = lax.dot_general(wr_ref[...], h2.astype(BF16), (((1,), (1,)), ((), ())),
                             preferred_element_type=F32) + br_ref[...]
    e_iota = lax.broadcasted_iota(jnp.int32, (ne, m), 0).astype(F32)
    vals = logits
    top_v, sels = [], []
    for k in range(TOP_K):
        mx = jnp.max(vals, axis=0, keepdims=True)
        idx = jnp.min(jnp.where(vals == mx, e_iota, float(ne)), axis=0, keepdims=True)
        sel = e_iota == idx
        vals = jnp.where(sel, -jnp.inf, vals)
        top_v.append(mx)
        sels.append(sel)
        route_ref[0, k:k + 1, :] = idx
    ex = [jnp.exp(v - top_v[0]) for v in top_v]
    den = ex[0] + ex[1] + ex[2] + ex[3]
    chosen = jnp.zeros((ne, m), F32)
    for k in range(TOP_K):
        route_ref[0, 2 * TOP_K + k:2 * TOP_K + k + 1, :] = ex[k] / den
        chosen = chosen + jnp.where(sels[k], 1.0, 0.0)
    before = (lax.broadcasted_iota(jnp.int32, (m, m), 0) < lax.broadcasted_iota(jnp.int32, (m, m), 1))
    rank = jnp.dot(chosen.astype(BF16), jnp.where(before, 1.0, 0.0).astype(BF16), preferred_element_type=F32)
    for k in range(TOP_K):
        route_ref[0, TOP_K + k:TOP_K + k + 1, :] = jnp.sum(jnp.where(sels[k], rank, 0.0), axis=0, keepdims=True)
    route_ref[0, 3 * TOP_K:4 * TOP_K, :] = jnp.zeros((TOP_K, m), F32)
    cnt_ref[0] = jnp.broadcast_to(jnp.sum(chosen, axis=1, keepdims=True), (ne, LANES))


def _outproj(attn, lru_o, x, mod, lnf, w_out_bf, wr_t, br, nb, tr, n_tiles, tile0, prev):
    nbt, t, d = x.shape
    aw = attn.shape[-1]
    m = nb * tr
    assert m == TOKEN_TILE
    ne = wr_t.shape[0]
    tiles_per_seq = t // tr
    xmap = lambda b, i: (b, i, 0)
    c2 = lambda b, i: (0, 0)
    tile = lambda b, i: (tile0 + b * tiles_per_seq + i, 0)
    tile3 = lambda b, i: (tile0 + b * tiles_per_seq + i, 0, 0)
    in_specs = [pl.BlockSpec((nb, tr, aw), xmap), pl.BlockSpec((nb, tr, aw), xmap),
                pl.BlockSpec((nb, tr, d), xmap),
                _mod_spec(nb, d, MOD_GATE_MIX), _mod_spec(nb, d, MOD_SHIFT_FFN), _mod_spec(nb, d, MOD_SCALE_FFN),
                pl.BlockSpec((1, d), c2), pl.BlockSpec(w_out_bf.shape, c2),
                pl.BlockSpec(wr_t.shape, c2), pl.BlockSpec((ne, 1), c2)]
    args = [attn, lru_o, x, mod, mod, mod, lnf, w_out_bf, wr_t, br]
    aliases = {}
    if prev is not None:
        in_specs += [pl.BlockSpec(memory_space=pl.ANY)] * 3
        args += list(prev)
        aliases = {10: 1, 11: 2, 12: 3}
    return pl.pallas_call(
        functools.partial(_outproj_kernel, aliased=len(aliases)),
        grid=(nbt // nb, tiles_per_seq),
        in_specs=in_specs,
        out_specs=[pl.BlockSpec((nb, tr, d), xmap), pl.BlockSpec((m, d), tile),
                   pl.BlockSpec((1, 4 * TOP_K, m), tile3), pl.BlockSpec((1, ne, LANES), tile3)],
        out_shape=[jax.ShapeDtypeStruct((nbt, t, d), F32),
                   jax.ShapeDtypeStruct((n_tiles * m, d), BF16),
                   jax.ShapeDtypeStruct((n_tiles, 4 * TOP_K, m), F32),
                   jax.ShapeDtypeStruct((n_tiles, ne, LANES), F32)],
        input_output_aliases=aliases,
        compiler_params=_params(("arbitrary", "arbitrary")),
        name="outproj",
    )(*args)


def _tile_rows(m):
    cap = TOP_K * m + N_EXPERTS * (BF16_ROWS - 1) + BF16_ROWS
    return -(-cap // TOKEN_TILE) * TOKEN_TILE


def _table_sizes(nt):
    g = BF16_ROWS
    m = TOKEN_TILE
    n_chunks = _tile_rows(m) // g
    n_gap = -(-(N_EXPERTS * (EXPERT_ROWS // g - 1)) // nt)
    bound = TOP_K * m * nt + nt * N_EXPERTS * (g - 1) + N_EXPERTS * (EXPERT_ROWS - g)
    n_sorted = -(-bound // EXPERT_ROWS) * EXPERT_ROWS
    return n_chunks, n_gap, n_sorted, n_sorted + 2 * (n_chunks + n_gap) * g


def _route_tables(cnt):
    nt = cnt.shape[0]
    g = BF16_ROWS
    bm = EXPERT_ROWS
    n_chunks, n_gap, n_sorted, _ = _table_sizes(nt)
    e_ids = jnp.arange(N_EXPERTS, dtype=jnp.int32)
    t_ids = jnp.arange(nt, dtype=jnp.int32)
    upto = (e_ids[:, None] <= e_ids[None, :]).astype(jnp.int32)
    pc = (cnt + g - 1) // g * g
    ctile = jnp.sum(pc[:, :, None] * upto[None], axis=1)
    toff = ctile - pc
    trow = ctile[:, -1]
    tot = jnp.sum(pc, axis=0)
    reg = (tot + bm - 1) // bm * bm
    creg = jnp.sum(reg[:, None] * upto, axis=0)
    base = creg - reg
    earlier = (t_ids[:, None] < t_ids[None, :]).astype(jnp.int32)
    goff = base[None, :] + jnp.sum(pc[:, None, :] * earlier[:, :, None], axis=0)
    r = jnp.arange(n_chunks, dtype=jnp.int32) * g
    r3 = r[None, :, None]
    in_seg = (toff[:, None, :] <= r3) & (r3 < ctile[:, None, :])
    dst = jnp.sum(jnp.where(in_seg, (goff - toff)[:, None, :], 0), axis=2) + r[None, :]
    dst = jnp.where(r[None, :] < trow[:, None], dst, -1)
    gcnt = (reg - tot) // g
    gcum = jnp.sum(gcnt[:, None] * upto, axis=0)
    gstart = gcum - gcnt
    s = jnp.arange(nt * n_gap, dtype=jnp.int32)
    in_gap = (gstart[None, :] <= s[:, None]) & (s[:, None] < gcum[None, :])
    gdst = jnp.sum(jnp.where(in_gap, (base + tot - g * gstart)[None, :] + g * s[:, None], 0), axis=1)
    gdst = jnp.where(s < gcum[-1], gdst, -1).reshape(nt, n_gap)
    table = jnp.concatenate([dst, gdst], axis=1).astype(jnp.int32)
    n_entries = n_chunks + n_gap
    spare = n_sorted + ((t_ids % 2)[:, None] * n_entries + jnp.arange(n_entries, dtype=jnp.int32)[None, :]) * g
    dispatch_tab = jnp.where(table >= 0, table, spare).astype(jnp.int32)
    combine_tab = jnp.maximum(dst, 0).astype(jnp.int32)
    toff_b = jnp.broadcast_to(toff.astype(F32)[:, :, None], (nt, N_EXPERTS, LANES))
    limit = (creg[-1] - EXPERT_UNIT_BLOCKS * bm).astype(jnp.int32).reshape(1)
    nblk = reg // bm
    units = (nblk + EXPERT_UNIT_BLOCKS - 1) // EXPERT_UNIT_BLOCKS
    ubase = jnp.sum(units[:, None] * upto, axis=0) - units
    return (dispatch_tab, combine_tab, toff_b, base.astype(jnp.int32), nblk.astype(jnp.int32),
            ubase.astype(jnp.int32), limit)


def _slot_rows(route_ref, toff_ref, m):
    ne = toff_ref.shape[1]
    e_iota = lax.broadcasted_iota(jnp.int32, (ne, m), 0).astype(F32)
    toff_col = toff_ref[0][:, 0:1]
    pos = []
    for k in range(TOP_K):
        sel = e_iota == route_ref[0, k:k + 1, :]
        start = jnp.sum(jnp.where(sel, toff_col, 0.0), axis=0, keepdims=True)
        pos.append(start + route_ref[0, TOP_K + k:TOP_K + k + 1, :])
    return pos


def _dispatch_kernel(tab_ref, h2_ref, route_ref, toff_ref, xs_hbm, buf_ref, sem, *, n_chunks, n_tiles):
    t = pl.program_id(0)
    slot = t % 2
    m = h2_ref.shape[0]
    rows = buf_ref.shape[1]
    n_entries = tab_ref.shape[1]
    g = BF16_ROWS
    per_chunk = m // g

    def start(c):
        src = c * g if c < n_chunks else rows - g
        pltpu.make_async_copy(
            buf_ref.at[slot, pl.ds(src, g)],
            xs_hbm.at[pl.ds(pl.multiple_of(tab_ref[t, c], g), g)], sem.at[slot]).start()

    def wait_all(which):
        for _ in range(n_entries):
            pltpu.make_async_copy(buf_ref.at[which, pl.ds(0, g)], xs_hbm.at[pl.ds(0, g)], sem.at[which]).wait()

    @pl.when(t >= 2)
    def _():
        wait_all(slot)

    pos = _slot_rows(route_ref, toff_ref, m)
    h2 = h2_ref[...]
    for rc in range(rows // m):
        r_iota = (lax.broadcasted_iota(jnp.int32, (m, m), 0) + rc * m).astype(F32)
        onehot = jnp.zeros((m, m), F32)
        for k in range(TOP_K):
            onehot = jnp.where(r_iota == pos[k], 1.0, onehot)
        buf_ref[slot, rc * m:(rc + 1) * m, :] = jnp.dot(
            onehot.astype(BF16), h2, preferred_element_type=F32).astype(BF16)
        for c in range((rc - 1) * per_chunk, rc * per_chunk) if rc > 0 else ():
            start(c)
    for c in range(n_chunks - per_chunk, n_entries):
        start(c)

    @pl.when(t == n_tiles - 1)
    def _():
        if n_tiles > 1:
            wait_all(1 - slot)
        wait_all(slot)


def _dispatch(table, h2, route, toff_b, n_rows, n_chunks):
    nt = route.shape[0]
    m = TOKEN_TILE
    d = h2.shape[1]
    rows = _tile_rows(m)
    grid_spec = pltpu.PrefetchScalarGridSpec(
        num_scalar_prefetch=1,
        grid=(nt,),
        in_specs=[pl.BlockSpec((m, d), lambda t, tab: (t, 0)),
                  pl.BlockSpec((1, 4 * TOP_K, m), lambda t, tab: (t, 0, 0)),
                  pl.BlockSpec((1, N_EXPERTS, LANES), lambda t, tab: (t, 0, 0))],
        out_specs=pl.BlockSpec(memory_space=pl.ANY),
        scratch_shapes=[pltpu.VMEM((2, rows, d), BF16), pltpu.SemaphoreType.DMA((2,))],
    )
    return pl.pallas_call(
        functools.partial(_dispatch_kernel, n_chunks=n_chunks, n_tiles=nt),
        grid_spec=grid_spec,
        out_shape=jax.ShapeDtypeStruct((n_rows, d), BF16),
        compiler_params=_params(("arbitrary",)),
        name="dispatch",
    )(table, h2, route, toff_b)


def _expert_kernel(row0_ref, nblk_ref, ubase_ref, limit_ref, xs_hbm, wgu_ref, bgu_ref, wdn_ref, bdn_ref, ys_hbm,
                   wgu_bf, wdn_bf, xbuf, ybuf, sem_in, sem_out, pend_ref):
    e = pl.program_id(0)
    ne = pl.num_programs(0)
    bm = EXPERT_ROWS
    unit = xbuf.shape[1]
    per_unit = unit // bm
    dff = wdn_ref.shape[1]
    nblk = nblk_ref[e]
    n_units = (nblk + per_unit - 1) // per_unit
    base = ubase_ref[e]

    def unit_start(expert, s):
        true = row0_ref[expert] + s * unit
        start = jnp.minimum(true, limit_ref[0])
        return pl.multiple_of(start, bm), pl.multiple_of(true - start, bm)

    def in_copy(expert, s, slot):
        start, _ = unit_start(expert, s)
        return pltpu.make_async_copy(xs_hbm.at[pl.ds(start, unit)], xbuf.at[slot], sem_in.at[slot])

    def out_copy(j, slot, i):
        start = pl.multiple_of(row0_ref[e] + j * bm, bm)
        return pltpu.make_async_copy(ybuf.at[slot, pl.ds(i * bm, bm)], ys_hbm.at[pl.ds(start, bm)],
                                     sem_out.at[slot])

    def wait_pending(slot):
        count = pend_ref[slot]

        @pl.when(count == per_unit)
        def _():
            for i in range(per_unit):
                out_copy(0, slot, i).wait()

        for i in range(per_unit - 1):
            @pl.when((count < per_unit) & (i < count))
            def _():
                out_copy(0, slot, i).wait()
        pend_ref[slot] = 0

    @pl.when(e == 0)
    def _():
        pend_ref[0] = 0
        pend_ref[1] = 0

    @pl.when((e == 0) & (nblk > 0))
    def _():
        in_copy(e, 0, 0).start()

    wgu_bf[:, :dff] = wgu_ref[0, :, :dff].astype(BF16)

    def run_unit(s, carry):
        slot = (base + s) % 2
        in_copy(e, s, slot).wait()

        @pl.when(s + 1 < n_units)
        def _():
            in_copy(e, s + 1, 1 - slot).start()

        wait_pending(slot)
        _, lead = unit_start(e, s)

        def ffn(first, count, convert):
            rows = count * bm
            x = xbuf[slot, pl.ds(pl.multiple_of(lead + first * bm, bm), rows), :]
            gate = jnp.dot(x, wgu_bf[:, :dff], preferred_element_type=F32) + bgu_ref[0, :, :dff]
            if convert:
                wgu_bf[:, dff:] = wgu_ref[0, :, dff:].astype(BF16)
            up = jnp.dot(x, wgu_bf[:, dff:], preferred_element_type=F32) + bgu_ref[0, :, dff:]
            if convert:
                wdn_bf[...] = wdn_ref[0].astype(BF16)
            gate = jnp.minimum(gate, SWIGLU_LIMIT)
            up = jnp.clip(up, -SWIGLU_LIMIT, SWIGLU_LIMIT)
            glu = gate * jax.nn.sigmoid(gate * SWIGLU_ALPHA)
            act = ((up + 1.0) * glu).astype(BF16)
            ybuf[slot, first * bm:first * bm + rows, :] = (
                jnp.dot(act, wdn_bf[...], preferred_element_type=F32) + bdn_ref[0]).astype(BF16)
            for i in range(first, first + count):
                out_copy(s * per_unit + i, slot, i).start()

        for first in range(0, per_unit, 2):
            have = nblk - s * per_unit - first
            for count, cond in ((2, have >= 2), (1, have == 1)):
                if first == 0:
                    @pl.when(cond & (s == 0))
                    def _():
                        ffn(first, count, True)

                    @pl.when(cond & (s > 0))
                    def _():
                        ffn(first, count, False)
                else:
                    @pl.when(cond)
                    def _():
                        ffn(first, count, False)
        pend_ref[slot] = jnp.minimum(nblk - s * per_unit, per_unit)
        return carry

    lax.fori_loop(0, n_units, run_unit, 0)

    nxt = jnp.minimum(e + 1, ne - 1)

    @pl.when((e + 1 < ne) & (nblk_ref[nxt] > 0))
    def _():
        in_copy(nxt, 0, (base + n_units) % 2).start()

    @pl.when(e == ne - 1)
    def _():
        wait_pending(0)
        wait_pending(1)


def _experts(row0, nblk, ubase, limit, xs, w_gu, b_gu, w_dn, b_dn):
    n_rows, d = xs.shape
    bm = EXPERT_ROWS
    unit = EXPERT_UNIT_BLOCKS * bm
    ne, _, dff2 = w_gu.shape
    dff = w_dn.shape[1]
    exp3 = lambda e, r0, nb, ub, lim: (e, 0, 0)
    grid_spec = pltpu.PrefetchScalarGridSpec(
        num_scalar_prefetch=4,
        grid=(ne,),
        in_specs=[pl.BlockSpec(memory_space=pl.ANY),
                  pl.BlockSpec((1, d, dff2), exp3), pl.BlockSpec((1, 1, dff2), exp3),
                  pl.BlockSpec((1, dff, d), exp3), pl.BlockSpec((1, 1, d), exp3)],
        out_specs=pl.BlockSpec(memory_space=pl.ANY),
        scratch_shapes=[pltpu.VMEM((d, dff2), BF16), pltpu.VMEM((dff, d), BF16),
                        pltpu.VMEM((2, unit, d), BF16), pltpu.VMEM((2, unit, d), BF16),
                        pltpu.SemaphoreType.DMA((2,)), pltpu.SemaphoreType.DMA((2,)),
                        pltpu.SMEM((2,), jnp.int32)],
    )
    return pl.pallas_call(
        _expert_kernel,
        grid_spec=grid_spec,
        out_shape=jax.ShapeDtypeStruct((n_rows, d), BF16),
        compiler_params=_params(("arbitrary",)),
        name="experts",
    )(row0, nblk, ubase, limit, xs, w_gu, b_gu.reshape(ne, 1, dff2), w_dn, b_dn.reshape(ne, 1, d))


def _combine_kernel(tab_ref, ys_hbm, route_ref, toff_ref, x1_ref, gf_ref, o_ref, buf_ref, sem, *, n_chunks, tile0):
    step = pl.program_id(0)
    n_steps = pl.num_programs(0)
    t = tile0 + step
    slot = step % 2
    nb, tr, d = x1_ref.shape
    m = nb * tr
    rows = buf_ref.shape[1]
    g = BF16_ROWS

    def fetch(tile, which):
        for c in range(n_chunks):
            pltpu.make_async_copy(
                ys_hbm.at[pl.ds(pl.multiple_of(tab_ref[tile, c], g), g)],
                buf_ref.at[which, pl.ds(c * g, g)], sem.at[which]).start()

    @pl.when(step == 0)
    def _():
        fetch(t, slot)

    @pl.when(step + 1 < n_steps)
    def _():
        fetch(t + 1, 1 - slot)

    pos = _slot_rows(route_ref, toff_ref, m)
    gates = [route_ref[0, 2 * TOP_K + k:2 * TOP_K + k + 1, :] for k in range(TOP_K)]
    stacked = jnp.concatenate(pos + gates + [jnp.zeros((LANES - 2 * TOP_K, m), F32)], axis=0)
    cols = stacked.T
    for _ in range(n_chunks):
        pltpu.make_async_copy(ys_hbm.at[pl.ds(0, g)], buf_ref.at[slot, pl.ds(0, g)], sem.at[slot]).wait()

    acc = jnp.zeros((m, d), F32)
    for rc in range(rows // m):
        c_iota = (lax.broadcasted_iota(jnp.int32, (m, m), 1) + rc * m).astype(F32)
        weights = jnp.zeros((m, m), F32)
        for k in range(TOP_K):
            weights = jnp.where(c_iota == cols[:, k:k + 1], cols[:, TOP_K + k:TOP_K + k + 1], weights)
        acc = acc + jnp.dot(weights.astype(BF16), buf_ref[slot, rc * m:(rc + 1) * m, :],
                            preferred_element_type=F32)
    o_ref[...] = x1_ref[...] + gf_ref[...] * acc.reshape(nb, tr, d)


def _combine(table, ys, route, toff_b, x1, gf, nb, tr, tile0, n_chunks):
    nbt, t, d = x1.shape
    m = nb * tr
    assert m == TOKEN_TILE
    rows = _tile_rows(m)
    tiles_per_seq = t // tr
    n_steps = (nbt // nb) * tiles_per_seq
    xmap = lambda s, tab: (s // tiles_per_seq, s % tiles_per_seq, 0)
    grid_spec = pltpu.PrefetchScalarGridSpec(
        num_scalar_prefetch=1,
        grid=(n_steps,),
        in_specs=[pl.BlockSpec(memory_space=pl.ANY),
                  pl.BlockSpec((1, 4 * TOP_K, m), lambda s, tab: (tile0 + s, 0, 0)),
                  pl.BlockSpec((1, N_EXPERTS, LANES), lambda s, tab: (tile0 + s, 0, 0)),
                  pl.BlockSpec((nb, tr, d), xmap),
                  pl.BlockSpec((nb, 1, d), lambda s, tab: (s // tiles_per_seq, 0, MOD_GATE_FFN))],
        out_specs=pl.BlockSpec((nb, tr, d), xmap),
        scratch_shapes=[pltpu.VMEM((2, rows, d), BF16), pltpu.SemaphoreType.DMA((2,))],
    )
    return pl.pallas_call(
        functools.partial(_combine_kernel, n_chunks=n_chunks, tile0=tile0),
        grid_spec=grid_spec,
        out_shape=jax.ShapeDtypeStruct((nbt, t, d), F32),
        compiler_params=_params(("arbitrary",)),
        name="combine",
    )(table, ys, route, toff_b, x1, gf)


def _block_diag(w, groups):
    n, k, _ = w.shape
    w = w.reshape(n // groups, groups, k, k)
    eye = jnp.eye(groups, dtype=w.dtype)
    return jnp.einsum("ngij,gh->ngihj", w, eye).reshape(n // groups, groups * k, groups * k)


def _layer(xp, xs, mod_p, mod_s, k_cache, v_cache, conv_state, lru_state, lw):
    (ln_mix, ln_ffn, w_in, q_norm, k_norm, rel_bias, conv_w, conv_b, w_rg, b_rg, w_ig, b_ig, lam,
     w_out, w_router, b_router, w_gu, b_gu, w_dn, b_dn) = lw
    bp, s, d = xp.shape
    bs, ts, _ = xs.shape
    aw = w_out.shape[0] // 2
    nh = aw // HEAD_DIM
    m = TOKEN_TILE
    assert s % m == 0 and bs * ts == m and s % ATTN_Q_TILE == 0

    w_in_bf = w_in.astype(BF16)
    w_out_bf = w_out.astype(BF16)
    qn_t = jnp.tile(q_norm * (HEAD_DIM ** -0.5), nh).reshape(1, aw)
    kn_t = jnp.tile(k_norm, nh).reshape(1, aw)
    head_mean = _block_diag(jnp.full((nh, HEAD_DIM, HEAD_DIM), 1.0 / HEAD_DIM, F32), nh)[0].astype(BF16)
    groups = MXU_DIM // w_rg.shape[-1]
    wa_bd = _block_diag(w_rg, groups).astype(BF16)
    wx_bd = _block_diag(w_ig, groups).astype(BF16)
    lw_c = b_rg.size
    b_a = b_rg.reshape(1, lw_c)
    b_x = b_ig.reshape(1, lw_c)
    lam2 = lam.reshape(1, lw_c)
    cb2 = conv_b.reshape(1, lw_c)
    ln_mix2 = ln_mix.reshape(1, d)
    ln_ffn2 = ln_ffn.reshape(1, d)
    wr_t = w_router.T.astype(BF16)
    br = b_router.reshape(-1, 1)
    tab_p = _bias_table(rel_bias, 3 * ATTN_Q_TILE - 1)
    r_cache = k_cache.shape[1]
    tab_s = _bias_table(rel_bias, r_cache + ts - 1)

    zeros_pre = jnp.zeros((bp, SUBLANES, lw_c), F32)
    zeros_h = jnp.zeros((bp, 1, lw_c), F32)
    pre_s = jnp.pad(conv_state, ((0, 0), (SUBLANES - (CONV_WIDTH - 1), 0), (0, 0)))
    lru_w = (conv_w, cb2, wa_bd, wx_bd, b_a, b_x, lam2)
    qp, kp, vp, k32p, v32p, lru_p, tail_p, hl_p = _mixin(
        xp, mod_p,ln_mix2, w_in_bf, qn_t, kn_t, head_mean, zeros_pre, zeros_h, *lru_w, 1, m)
    qs, ks, vs, k32s, v32s, lru_s, tail_s, hl_s = _mixin(
        xs, mod_s,ln_mix2, w_in_bf, qn_t, kn_t, head_mean, pre_s, lru_state[:, None, :], *lru_w, bs, ts)
    attn_p = _attn_prompt(qp, kp, vp, tab_p)
    attn_s = _attn_step(qs, ks, vs, jnp.transpose(k_cache, (0, 2, 3, 1)), jnp.transpose(v_cache, (0, 2, 3, 1)),
                        tab_s, ATTN_STEP_BATCH)

    n_tiles = bp * (s // m) + 1
    x1p, h2, route, cnt = _outproj(attn_p, lru_p, xp, mod_p,ln_ffn2, w_out_bf, wr_t, br,
                                   1, m, n_tiles, 0, None)
    x1s, h2, route, cnt = _outproj(attn_s, lru_s, xs, mod_s,ln_ffn2, w_out_bf, wr_t, br,
                                   bs, ts, n_tiles, n_tiles - 1, (h2, route, cnt))

    n_chunks, _, _, n_rows = _table_sizes(n_tiles)
    assert TOP_K * m * n_tiles >= EXPERT_UNIT_BLOCKS * EXPERT_ROWS
    dispatch_tab, combine_tab, toff_b, row0, nblk, ubase, limit = _route_tables(cnt[:, :, 0].astype(jnp.int32))
    xs_sorted = _dispatch(dispatch_tab, h2, route, toff_b, n_rows, n_chunks)
    ys_sorted = _experts(row0, nblk, ubase, limit, xs_sorted, w_gu, b_gu, w_dn, b_dn)
    yp = _combine(combine_tab, ys_sorted, route, toff_b, x1p, mod_p, 1, m, 0, n_chunks)
    ysm = _combine(combine_tab, ys_sorted, route, toff_b, x1s, mod_s, bs, ts, n_tiles - 1, n_chunks)

    keep = k32p.shape[1]
    new = (k32p.reshape(bp, keep, nh, HEAD_DIM), v32p.reshape(bp, keep, nh, HEAD_DIM),
           tail_p[:, SUBLANES - (CONV_WIDTH - 1):, :], hl_p[:, 0, :],
           k32s.reshape(bs, ts, nh, HEAD_DIM), v32s.reshape(bs, ts, nh, HEAD_DIM),
           tail_s[:, SUBLANES - (CONV_WIDTH - 1):, :], hl_s[:, 0, :])
    return yp, ysm, new


def kernel(x_prompt, x_sample, c_prompt, c_sample, cache_k, cache_v, state_conv, state_lru, ln_mix_w, ln_ffn_w, w_ada, b_ada, w_in, q_norm_w, k_norm_w, rel_bias, conv_w, conv_b, w_rgate, b_rgate, w_igate, b_igate, lru_lambda, w_out, w_router, b_router, w_gate_up, b_gate_up, w_down, b_down):
    depth = w_in.shape[0]
    yp, ys = x_prompt, x_sample
    collected = [[] for _ in range(8)]
    for l in range(depth):
        mod_p, mod_s = _ada(c_prompt, c_sample, w_ada[l], b_ada[l])
        lw = (ln_mix_w[l], ln_ffn_w[l], w_in[l], q_norm_w[l], k_norm_w[l], rel_bias[l], conv_w[l], conv_b[l],
              w_rgate[l], b_rgate[l], w_igate[l], b_igate[l], lru_lambda[l], w_out[l], w_router[l], b_router[l],
              w_gate_up[l], b_gate_up[l], w_down[l], b_down[l])
        yp, ys, new = _layer(yp, ys, mod_p, mod_s, cache_k[l], cache_v[l], state_conv[l], state_lru[l], lw)
        for acc, val in zip(collected, new):
            acc.append(val)
    return (yp, ys) + tuple(jnp.stack(vals) for vals in collected)
```

```python
import functools

import jax
import jax.numpy as jnp
from jax import lax
from jax.experimental import pallas as pl
from jax.experimental.pallas import tpu as pltpu

F32 = jnp.float32
BF16 = jnp.bfloat16

CHUNK = 64
N_LEFT_CHUNKS = 8
ATTN_WINDOW = N_LEFT_CHUNKS * CHUNK
HEAD_DIM = 64
REL_CLIP = 128
CONV_WIDTH = 4
LRU_C = 8.0
N_EXPERTS = 32
TOP_K = 4
SWIGLU_LIMIT = 7.0
SWIGLU_ALPHA = 1.702
NORM_EPS = 1e-6
NEG_INF = -1e30

LANES = 128
SUBLANES = 8
BF16_ROWS = 16
MXU_DIM = 256

TOKEN_TILE = 512
ATTN_Q_TILE = 256
ATTN_STEP_BATCH = 4
EXPERT_ROWS = 256
EXPERT_UNIT_BLOCKS = 8
BIAS_TABLE = 1024
VMEM_LIMIT = 56 * 1024 * 1024


def _params(sem, vmem=VMEM_LIMIT):
    return pltpu.CompilerParams(dimension_semantics=sem, vmem_limit_bytes=vmem)


def _ada_kernel(cp_ref, cs_ref, w_ref, b_ref, op_ref, os_ref):
    w = w_ref[...].astype(BF16)
    for c_ref, o_ref in ((cp_ref, op_ref), (cs_ref, os_ref)):
        c = c_ref[...]
        s = (c * jax.nn.sigmoid(c)).astype(BF16)
        o_ref[...] = jnp.dot(s, w, preferred_element_type=F32) + b_ref[...]


def _ada(c_p, c_s, w_ada, b_ada):
    (n_p, d), n_s = c_p.shape, c_s.shape[0]
    nout = w_ada.shape[1]
    tn = 1024
    mod_p, mod_s = pl.pallas_call(
        _ada_kernel,
        grid=(nout // tn,),
        in_specs=[pl.BlockSpec((n_p, d), lambda j: (0, 0)), pl.BlockSpec((n_s, d), lambda j: (0, 0)),
                  pl.BlockSpec((d, tn), lambda j: (0, j)),
                  pl.BlockSpec((1, tn), lambda j: (0, j))],
        out_specs=[pl.BlockSpec((n_p, tn), lambda j: (0, j)), pl.BlockSpec((n_s, tn), lambda j: (0, j))],
        out_shape=[jax.ShapeDtypeStruct((n_p, nout), F32), jax.ShapeDtypeStruct((n_s, nout), F32)],
        compiler_params=_params(("arbitrary",)),
        name="ada",
    )(c_p, c_s, w_ada, b_ada.reshape(1, nout))
    return mod_p.reshape(n_p, 1, nout), mod_s.reshape(n_s, 1, nout)


MOD_SHIFT_MIX, MOD_SCALE_MIX, MOD_GATE_MIX, MOD_SHIFT_FFN, MOD_SCALE_FFN, MOD_GATE_FFN = range(6)


def _mod_spec(nb, d, term):
    return pl.BlockSpec((nb, 1, d), lambda b, i: (b, 0, term))


def _mixin_kernel(x_ref, sh_ref, sc_ref, ln_ref, win_ref, qn_ref, kn_ref, bd_ref,
                  pre_ref, h0_ref, cw_ref, cb_ref, wa_ref, wx_ref, ba_ref, bx_ref, lam_ref,
                  q_ref, k_ref, v_ref, k32_ref, v32_ref, lru_ref, tail_ref, hl_ref, cx_ref, ch_ref):
    nb, tr, d = x_ref.shape
    m = nb * tr
    aw = q_ref.shape[-1]

    @pl.when(pl.program_id(1) == 0)
    def _():
        cx_ref[...] = pre_ref[...]
        ch_ref[...] = h0_ref[...]

    x = x_ref[...]
    ms = jnp.mean(x * x, axis=-1, keepdims=True)
    h = x * lax.rsqrt(ms + NORM_EPS) * (ln_ref[...] * (1.0 + sc_ref[...])) + sh_ref[...]
    hb = h.reshape(m, d).astype(BF16)

    def proj(part):
        return jnp.dot(hb, win_ref[:, part * aw:(part + 1) * aw], preferred_element_type=F32)

    def head_norm(t, w_ref):
        msq = jnp.dot((t * t).astype(BF16), bd_ref[...], preferred_element_type=F32)
        return t * lax.rsqrt(msq + NORM_EPS) * w_ref[...]

    lru_out, new_tail, h_last = _lru_branch(
        proj(3).reshape(nb, tr, aw), proj(4).reshape(nb, tr, aw), cw_ref, cb_ref,
        wa_ref, wx_ref, ba_ref, bx_ref, lam_ref, cx_ref, ch_ref)
    lru_ref[...] = lru_out
    tail_ref[...] = new_tail
    hl_ref[...] = h_last
    q = head_norm(proj(0), qn_ref)
    k = head_norm(proj(1), kn_ref)
    v = proj(2)
    q_ref[...] = q.astype(BF16).reshape(nb, tr, aw)
    k_ref[...] = k.astype(BF16).reshape(nb, tr, aw)
    v_ref[...] = v.astype(BF16).reshape(nb, tr, aw)
    k32_ref[...] = k.reshape(nb, tr, aw)
    v32_ref[...] = v.reshape(nb, tr, aw)


def _mixin(x, mod, ln_w, w_in_bf, qn_t, kn_t, bd, pre, h0, conv_w, conv_b, wa_bd, wx_bd, b_a, b_x, lam,
           nb, tr):
    nbt, t, d = x.shape
    aw = qn_t.shape[-1]
    c = pre.shape[-1]
    assert c == aw
    keep = min(ATTN_WINDOW, t)
    assert tr == keep or t == tr
    grid = (nbt // nb, t // tr)
    xmap = lambda b, i: (b, i, 0)
    mmap = lambda b, i: (b, 0, 0)
    cmap = lambda b, i: (0, 0)
    cmap3 = lambda b, i: (0, 0, 0)
    tmap = lambda b, i: (b, 0, 0)
    big = pl.BlockSpec((nb, tr, aw), xmap)
    tail = pl.BlockSpec((nb, keep, aw), tmap)
    row = pl.BlockSpec((1, c), cmap)
    return pl.pallas_call(
        _mixin_kernel,
        grid=grid,
        in_specs=[pl.BlockSpec((nb, tr, d), xmap),
                  _mod_spec(nb, d, MOD_SHIFT_MIX), _mod_spec(nb, d, MOD_SCALE_MIX),
                  pl.BlockSpec((1, d), cmap),
                  pl.BlockSpec(w_in_bf.shape, cmap),
                  pl.BlockSpec((1, aw), cmap), pl.BlockSpec((1, aw), cmap),
                  pl.BlockSpec(bd.shape, cmap),
                  pl.BlockSpec((nb, SUBLANES, c), tmap), pl.BlockSpec((nb, 1, c), tmap),
                  pl.BlockSpec(conv_w.shape, cmap), row,
                  pl.BlockSpec(wa_bd.shape, cmap3), pl.BlockSpec(wx_bd.shape, cmap3),
                  row, row, row],
        out_specs=[big, big, big, tail, tail, big,
                   pl.BlockSpec((nb, SUBLANES, c), tmap), pl.BlockSpec((nb, 1, c), tmap)],
        out_shape=[jax.ShapeDtypeStruct((nbt, t, aw), BF16)] * 3
        + [jax.ShapeDtypeStruct((nbt, keep, aw), F32)] * 2
        + [jax.ShapeDtypeStruct((nbt, t, c), BF16),
           jax.ShapeDtypeStruct((nbt, SUBLANES, c), F32),
           jax.ShapeDtypeStruct((nbt, 1, c), F32)],
        scratch_shapes=[pltpu.VMEM((nb, SUBLANES, c), F32), pltpu.VMEM((nb, 1, c), F32)],
        compiler_params=_params(("arbitrary", "arbitrary")),
        name="mixin",
    )(x, mod, mod, ln_w, w_in_bf, qn_t, kn_t, bd, pre, h0, conv_w, conv_b, wa_bd, wx_bd, b_a, b_x, lam)


def _bias_table(rel_bias, off):
    h = rel_bias.shape[0]
    left = off - REL_CLIP
    right = BIAS_TABLE - left - (2 * REL_CLIP + 1)
    assert left >= 0 and right >= 0
    return jnp.concatenate([jnp.broadcast_to(rel_bias[:, :1], (h, left)), rel_bias,
                            jnp.broadcast_to(rel_bias[:, -1:], (h, right))], axis=1)


def _toeplitz(tab_row, rows, cols):
    t = jnp.broadcast_to(tab_row, (rows, BIAS_TABLE))
    t = pltpu.roll(t, BIAS_TABLE - (rows - 1), 1, stride=1, stride_axis=0)
    return t[:, :cols]


def _attn_kernel(q_ref, k0_ref, k1_ref, k2_ref, v0_ref, v1_ref, v2_ref, tab_ref, o_ref, bias_ref):
    b = pl.program_id(0)
    s = pl.program_id(1)
    qt = q_ref.shape[1]
    nk = 3 * qt
    nh = bias_ref.shape[0]

    @pl.when((b == 0) & (s == 0))
    def _():
        qi = lax.broadcasted_iota(jnp.int32, (qt, nk), 0) // CHUNK
        kc = lax.broadcasted_iota(jnp.int32, (qt, nk), 1) // CHUNK
        for h in range(nh):
            band = jnp.where(kc <= qi + N_LEFT_CHUNKS, _toeplitz(tab_ref[h:h + 1, :], qt, nk), NEG_INF)
            bias_ref[h] = jnp.where(kc >= qi, band, NEG_INF)

    pair_w = 2 * HEAD_DIM

    def attend(mask_start):
        q = q_ref[0]
        kcat = jnp.concatenate([k0_ref[0], k1_ref[0], k2_ref[0]], axis=0)
        vcat = jnp.concatenate([v0_ref[0], v1_ref[0], v2_ref[0]], axis=0)
        first = lax.broadcasted_iota(jnp.int32, (qt, pair_w), 1) < HEAD_DIM
        keep = [jnp.where(first, 1.0, 0.0).astype(BF16), jnp.where(first, 0.0, 1.0).astype(BF16)]
        if mask_start:
            in_seq = lax.broadcasted_iota(jnp.int32, (qt, nk), 1) >= (2 - s) * qt
        outs = []
        for pair in range(nh // 2):
            sl = slice(pair * pair_w, (pair + 1) * pair_w)
            q2, k2, v2 = q[:, sl], kcat[:, sl], vcat[:, sl]
            per_head = []
            for sub in range(2):
                sc = lax.dot_general(q2 * keep[sub], k2, (((1,), (1,)), ((), ())), preferred_element_type=F32)
                sc = sc + bias_ref[2 * pair + sub]
                if mask_start:
                    sc = jnp.where(in_seq, sc, NEG_INF)
                mx = jnp.max(sc, axis=-1, keepdims=True)
                p = jnp.exp(sc - mx)
                l = jnp.sum(p, axis=-1, keepdims=True)
                per_head.append(jnp.dot(p.astype(BF16), v2, preferred_element_type=F32) / l)
            outs.append(jnp.where(first, per_head[0], per_head[1]))
        o_ref[0] = jnp.concatenate(outs, axis=-1).astype(BF16)

    @pl.when(s < 2)
    def _():
        attend(True)

    @pl.when(s >= 2)
    def _():
        attend(False)


def _attn_prompt(q, k, v, tab):
    b, s, aw = q.shape
    qt = ATTN_Q_TILE
    nh = aw // HEAD_DIM
    qspec = pl.BlockSpec((1, qt, aw), lambda i, j: (i, j, 0))

    def kspec(back):
        return pl.BlockSpec((1, qt, aw), lambda i, j: (i, jnp.maximum(j - back, 0), 0))

    return pl.pallas_call(
        _attn_kernel,
        grid=(b, s // qt),
        in_specs=[qspec, kspec(2), kspec(1), kspec(0), kspec(2), kspec(1), kspec(0),
                  pl.BlockSpec(tab.shape, lambda i, j: (0, 0))],
        out_specs=qspec,
        out_shape=jax.ShapeDtypeStruct((b, s, aw), BF16),
        scratch_shapes=[pltpu.VMEM((nh, qt, 3 * qt), F32)],
        compiler_params=_params(("arbitrary", "arbitrary")),
        name="attn_prompt",
    )(q, k, k, k, v, v, v, tab)


def _attn_step_kernel(q_ref, kn_ref, vn_ref, ck_ref, cv_ref, tab_ref, o_ref, bias_ref):
    step = pl.program_id(0)
    nbs, t, aw = q_ref.shape
    nh = aw // HEAD_DIM
    rows = nh * t
    r = ck_ref.shape[-1]
    nk = r + LANES
    nt_dims = (((1,), (1,)), ((), ()))

    @pl.when(step == 0)
    def _():
        ok = lax.broadcasted_iota(jnp.int32, (t, nk), 1) < r + t
        for h in range(nh):
            bias_ref[h * t:(h + 1) * t, :] = jnp.where(ok, _toeplitz(tab_ref[h:h + 1, :], t, nk), NEG_INF)

    own = (lax.broadcasted_iota(jnp.int32, (rows, aw), 0) // t
           == lax.broadcasted_iota(jnp.int32, (rows, aw), 1) // HEAD_DIM)
    own_f = jnp.where(own, 1.0, 0.0)
    own_bf = own_f.astype(BF16)
    pad = jnp.zeros((LANES - t, aw), BF16)
    for b in range(nbs):
        q_bd = jnp.concatenate([q_ref[b]] * nh, axis=0) * own_bf
        k_old = ck_ref[b].reshape(aw, r).astype(BF16)
        v_old = cv_ref[b].reshape(aw, r).astype(BF16)
        k_new = jnp.concatenate([kn_ref[b], pad], axis=0)
        v_new = jnp.concatenate([vn_ref[b], pad], axis=0)
        s_old = jnp.dot(q_bd, k_old, preferred_element_type=F32) + bias_ref[:, :r]
        s_new = lax.dot_general(q_bd, k_new, nt_dims, preferred_element_type=F32) + bias_ref[:, r:]
        mx = jnp.maximum(jnp.max(s_old, axis=-1, keepdims=True), jnp.max(s_new, axis=-1, keepdims=True))
        p_old = jnp.exp(s_old - mx)
        p_new = jnp.exp(s_new - mx)
        l = jnp.sum(p_old, axis=-1, keepdims=True) + jnp.sum(p_new, axis=-1, keepdims=True)
        o_all = (lax.dot_general(p_old.astype(BF16), v_old, nt_dims, preferred_element_type=F32)
                 + jnp.dot(p_new.astype(BF16), v_new, preferred_element_type=F32))
        o_all = o_all * own_f / l
        out = o_all[0:t]
        for h in range(1, nh):
            out = out + o_all[h * t:(h + 1) * t]
        o_ref[b] = out.astype(BF16)


def _attn_step(q, kn, vn, ck, cv, tab, nbs):
    b, t, aw = q.shape
    nh = aw // HEAD_DIM
    r = ck.shape[-1]
    new = pl.BlockSpec((nbs, t, aw), lambda i: (i, 0, 0))
    old = pl.BlockSpec((nbs, nh, HEAD_DIM, r), lambda i: (i, 0, 0, 0))
    return pl.pallas_call(
        _attn_step_kernel,
        grid=(b // nbs,),
        in_specs=[new, new, new, old, old, pl.BlockSpec(tab.shape, lambda i: (0, 0))],
        out_specs=new,
        out_shape=jax.ShapeDtypeStruct((b, t, aw), BF16),
        scratch_shapes=[pltpu.VMEM((nh * t, r + LANES), F32)],
        compiler_params=_params(("arbitrary",)),
        name="attn_step",
    )(q, kn, vn, ck, cv, tab)


def _gelu_tanh(x):
    return x * (0.5 * (1.0 + jnp.tanh(0.7978845608028654 * (x + 0.044715 * (x * x * x)))))


def _lru_branch(x, yg, cw_ref, cb_ref, wa_ref, wx_ref, ba_ref, bx_ref, lam_ref, cx_ref, ch_ref):
    nb, tr, c = x.shape
    m = nb * tr
    half = c // 2
    xp = jnp.concatenate([cx_ref[...], x], axis=1)
    new_tail = xp[:, tr:tr + SUBLANES, :]
    groups = tr // SUBLANES
    xg = xp.reshape(nb * (groups + 1), SUBLANES, c)
    first_rows = lax.broadcasted_iota(jnp.int32, (nb, groups, SUBLANES, c), 2)
    y = cb_ref[...] + cw_ref[CONV_WIDTH - 1:CONV_WIDTH, :] * x
    for back in range(1, CONV_WIDTH):
        rot = pltpu.roll(xg, back, 1).reshape(nb, groups + 1, SUBLANES, c)
        shifted = jnp.where(first_rows >= back, rot[:, 1:], rot[:, :groups]).reshape(nb, tr, c)
        y = y + cw_ref[CONV_WIDTH - 1 - back:CONV_WIDTH - back, :] * shifted
    y2 = y.reshape(m, c)
    yb = y2.astype(BF16)

    def gate(w_ref, b_ref):
        g = jnp.concatenate(
            [jnp.dot(yb[:, :half], w_ref[0], preferred_element_type=F32),
             jnp.dot(yb[:, half:], w_ref[1], preferred_element_type=F32)], axis=1)
        return jax.nn.sigmoid(g + b_ref[...])

    rg = gate(wa_ref, ba_ref)
    ig = gate(wx_ref, bx_ref)
    lam = lam_ref[...]
    log_sig = jnp.minimum(lam, 0.0) - jnp.log1p(jnp.exp(-jnp.abs(lam)))
    log_a = rg * (LRU_C * log_sig)
    a_cum = jnp.exp(log_a)
    b_cum = jnp.sqrt(-jnp.tanh(log_a) * (a_cum * a_cum + 1.0)) * (ig * y2)
    a_cum = a_cum.reshape(nb * groups, SUBLANES, c)
    b_cum = b_cum.reshape(nb * groups, SUBLANES, c)
    row = lax.broadcasted_iota(jnp.int32, a_cum.shape, 1)
    dist = 1
    while dist < SUBLANES:
        keep = row >= dist
        a_sh = jnp.where(keep, pltpu.roll(a_cum, dist, 1), 1.0)
        b_sh = jnp.where(keep, pltpu.roll(b_cum, dist, 1), 0.0)
        b_cum = a_cum * b_sh + b_cum
        a_cum = a_cum * a_sh
        dist *= 2
    a_grp = a_cum.reshape(nb, groups, SUBLANES, c)
    b_grp = b_cum.reshape(nb, groups, SUBLANES, c)
    carry = ch_ref[...]
    pieces = []
    for grp in range(groups):
        h_grp = a_grp[:, grp] * carry + b_grp[:, grp]
        carry = h_grp[:, SUBLANES - 1:SUBLANES, :]
        pieces.append(h_grp)
    h = jnp.concatenate(pieces, axis=1)
    ch_ref[...] = carry
    cx_ref[...] = new_tail
    return (h * _gelu_tanh(yg)).astype(BF16), new_tail, carry


def _outproj_kernel(*refs, aliased):
    (at_ref, lr_ref, x_ref, gm_ref, shf_ref, scf_ref, lnf_ref, wo_ref, wr_ref, br_ref) = refs[:10]
    x1_ref, h2_ref, route_ref, cnt_ref = refs[10 + aliased:]
    nb, tr, d = x_ref.shape
    m = nb * tr
    aw = at_ref.shape[-1]
    ne = wr_ref.shape[0]
    at = at_ref[...].reshape(m, aw)
    lr = lr_ref[...].reshape(m, aw)
    mix = (jnp.dot(at, wo_ref[0:aw, :], preferred_element_type=F32)
           + jnp.dot(lr, wo_ref[aw:2 * aw, :], preferred_element_type=F32))
    x1 = x_ref[...] + gm_ref[...] * mix.reshape(nb, tr, d)
    x1_ref[...] = x1
    ms = jnp.mean(x1 * x1, axis=-1, keepdims=True)
    h2 = (x1 * lax.rsqrt(ms + NORM_EPS) * (lnf_ref[...] * (1.0 + scf_ref[...])) + shf_ref[...]).reshape(m, d)
    h2_ref[...] = h2.astype(BF16)

    logits = lax.dot_general(wr_ref[...], h2.astype(BF16), (((1,), (1,)), ((), ())),
                             preferred_element_type=F32) + br_ref[...]
    e_iota = lax.broadcasted_iota(jnp.int32, (ne, m), 0).astype(F32)
    vals = logits
    top_v, sels = [], []
    for k in range(TOP_K):
        mx = jnp.max(vals, axis=0, keepdims=True)
        idx = jnp.min(jnp.where(vals == mx, e_iota, float(ne)), axis=0, keepdims=True)
        sel = e_iota == idx
        vals = jnp.where(sel, -jnp.inf, vals)
        top_v.append(mx)
        sels.append(sel)
        route_ref[0, k:k + 1, :] = idx
    ex = [jnp.exp(v - top_v[0]) for v in top_v]
    den = ex[0] + ex[1] + ex[2] + ex[3]
    chosen = jnp.zeros((ne, m), F32)
    for k in range(TOP_K):
        route_ref[0, 2 * TOP_K + k:2 * TOP_K + k + 1, :] = ex[k] / den
        chosen = chosen + jnp.where(sels[k], 1.0, 0.0)
    before = (lax.broadcasted_iota(jnp.int32, (m, m), 0) < lax.broadcasted_iota(jnp.int32, (m, m), 1))
    rank = jnp.dot(chosen.astype(BF16), jnp.where(before, 1.0, 0.0).astype(BF16), preferred_element_type=F32)
    for k in range(TOP_K):
        route_ref[0, TOP_K + k:TOP_K + k + 1, :] = jnp.sum(jnp.where(sels[k], rank, 0.0), axis=0, keepdims=True)
    route_ref[0, 3 * TOP_K:4 * TOP_K, :] = jnp.zeros((TOP_K, m), F32)
    cnt_ref[0] = jnp.broadcast_to(jnp.sum(chosen, axis=1, keepdims=True), (ne, LANES))


def _outproj(attn, lru_o, x, mod, lnf, w_out_bf, wr_t, br, nb, tr, n_tiles, tile0, prev):
    nbt, t, d = x.shape
    aw = attn.shape[-1]
    m = nb * tr
    assert m == TOKEN_TILE
    ne = wr_t.shape[0]
    tiles_per_seq = t // tr
    xmap = lambda b, i: (b, i, 0)
    c2 = lambda b, i: (0, 0)
    tile = lambda b, i: (tile0 + b * tiles_per_seq + i, 0)
    tile3 = lambda b, i: (tile0 + b * tiles_per_seq + i, 0, 0)
    in_specs = [pl.BlockSpec((nb, tr, aw), xmap), pl.BlockSpec((nb, tr, aw), xmap),
                pl.BlockSpec((nb, tr, d), xmap),
                _mod_spec(nb, d, MOD_GATE_MIX), _mod_spec(nb, d, MOD_SHIFT_FFN), _mod_spec(nb, d, MOD_SCALE_FFN),
                pl.BlockSpec((1, d), c2), pl.BlockSpec(w_out_bf.shape, c2),
                pl.BlockSpec(wr_t.shape, c2), pl.BlockSpec((ne, 1), c2)]
    args = [attn, lru_o, x, mod, mod, mod, lnf, w_out_bf, wr_t, br]
    aliases = {}
    if prev is not None:
        in_specs += [pl.BlockSpec(memory_space=pl.ANY)] * 3
        args += list(prev)
        aliases = {10: 1, 11: 2, 12: 3}
    return pl.pallas_call(
        functools.partial(_outproj_kernel, aliased=len(aliases)),
        grid=(nbt // nb, tiles_per_seq),
        in_specs=in_specs,
        out_specs=[pl.BlockSpec((nb, tr, d), xmap), pl.BlockSpec((m, d), tile),
                   pl.BlockSpec((1, 4 * TOP_K, m), tile3), pl.BlockSpec((1, ne, LANES), tile3)],
        out_shape=[jax.ShapeDtypeStruct((nbt, t, d), F32),
                   jax.ShapeDtypeStruct((n_tiles * m, d), BF16),
                   jax.ShapeDtypeStruct((n_tiles, 4 * TOP_K, m), F32),
                   jax.ShapeDtypeStruct((n_tiles, ne, LANES), F32)],
        input_output_aliases=aliases,
        compiler_params=_params(("arbitrary", "arbitrary")),
        name="outproj",
    )(*args)


def _tile_rows(m):
    cap = TOP_K * m + N_EXPERTS * (BF16_ROWS - 1) + BF16_ROWS
    return -(-cap // TOKEN_TILE) * TOKEN_TILE


def _table_sizes(nt):
    g = BF16_ROWS
    m = TOKEN_TILE
    n_chunks = _tile_rows(m) // g
    n_gap = -(-(N_EXPERTS * (EXPERT_ROWS // g - 1)) // nt)
    bound = TOP_K * m * nt + nt * N_EXPERTS * (g - 1) + N_EXPERTS * (EXPERT_ROWS - g)
    n_sorted = -(-bound // EXPERT_ROWS) * EXPERT_ROWS
    return n_chunks, n_gap, n_sorted, n_sorted + 2 * (n_chunks + n_gap) * g


def _route_tables(cnt):
    nt = cnt.shape[0]
    g = BF16_ROWS
    bm = EXPERT_ROWS
    n_chunks, n_gap, n_sorted, _ = _table_sizes(nt)
    e_ids = jnp.arange(N_EXPERTS, dtype=jnp.int32)
    t_ids = jnp.arange(nt, dtype=jnp.int32)
    upto = (e_ids[:, None] <= e_ids[None, :]).astype(jnp.int32)
    pc = (cnt + g - 1) // g * g
    ctile = jnp.sum(pc[:, :, None] * upto[None], axis=1)
    toff = ctile - pc
    trow = ctile[:, -1]
    tot = jnp.sum(pc, axis=0)
    reg = (tot + bm - 1) // bm * bm
    creg = jnp.sum(reg[:, None] * upto, axis=0)
    base = creg - reg
    earlier = (t_ids[:, None] < t_ids[None, :]).astype(jnp.int32)
    goff = base[None, :] + jnp.sum(pc[:, None, :] * earlier[:, :, None], axis=0)
    r = jnp.arange(n_chunks, dtype=jnp.int32) * g
    r3 = r[None, :, None]
    in_seg = (toff[:, None, :] <= r3) & (r3 < ctile[:, None, :])
    dst = jnp.sum(jnp.where(in_seg, (goff - toff)[:, None, :], 0), axis=2) + r[None, :]
    dst = jnp.where(r[None, :] < trow[:, None], dst, -1)
    gcnt = (reg - tot) // g
    gcum = jnp.sum(gcnt[:, None] * upto, axis=0)
    gstart = gcum - gcnt
    s = jnp.arange(nt * n_gap, dtype=jnp.int32)
    in_gap = (gstart[None, :] <= s[:, None]) & (s[:, None] < gcum[None, :])
    gdst = jnp.sum(jnp.where(in_gap, (base + tot - g * gstart)[None, :] + g * s[:, None], 0), axis=1)
    gdst = jnp.where(s < gcum[-1], gdst, -1).reshape(nt, n_gap)
    table = jnp.concatenate([dst, gdst], axis=1).astype(jnp.int32)
    n_entries = n_chunks + n_gap
    spare = n_sorted + ((t_ids % 2)[:, None] * n_entries + jnp.arange(n_entries, dtype=jnp.int32)[None, :]) * g
    dispatch_tab = jnp.where(table >= 0, table, spare).astype(jnp.int32)
    combine_tab = jnp.maximum(dst, 0).astype(jnp.int32)
    toff_b = jnp.broadcast_to(toff.astype(F32)[:, :, None], (nt, N_EXPERTS, LANES))
    limit = (creg[-1] - EXPERT_UNIT_BLOCKS * bm).astype(jnp.int32).reshape(1)
    nblk = reg // bm
    units = (nblk + EXPERT_UNIT_BLOCKS - 1) // EXPERT_UNIT_BLOCKS
    ubase = jnp.sum(units[:, None] * upto, axis=0) - units
    return (dispatch_tab, combine_tab, toff_b, base.astype(jnp.int32), nblk.astype(jnp.int32),
            ubase.astype(jnp.int32), limit)


def _slot_rows(route_ref, toff_ref, m):
    ne = toff_ref.shape[1]
    e_iota = lax.broadcasted_iota(jnp.int32, (ne, m), 0).astype(F32)
    toff_col = toff_ref[0][:, 0:1]
    pos = []
    for k in range(TOP_K):
        sel = e_iota == route_ref[0, k:k + 1, :]
        start = jnp.sum(jnp.where(sel, toff_col, 0.0), axis=0, keepdims=True)
        pos.append(start + route_ref[0, TOP_K + k:TOP_K + k + 1, :])
    return pos


def _dispatch_kernel(tab_ref, h2_ref, route_ref, toff_ref, xs_hbm, buf_ref, sem, *, n_chunks, n_tiles):
    t = pl.program_id(0)
    slot = t % 2
    m = h2_ref.shape[0]
    rows = buf_ref.shape[1]
    n_entries = tab_ref.shape[1]
    g = BF16_ROWS
    per_chunk = m // g

    def start(c):
        src = c * g if c < n_chunks else rows - g
        pltpu.make_async_copy(
            buf_ref.at[slot, pl.ds(src, g)],
            xs_hbm.at[pl.ds(pl.multiple_of(tab_ref[t, c], g), g)], sem.at[slot]).start()

    def wait_all(which):
        for _ in range(n_entries):
            pltpu.make_async_copy(buf_ref.at[which, pl.ds(0, g)], xs_hbm.at[pl.ds(0, g)], sem.at[which]).wait()

    @pl.when(t >= 2)
    def _():
        wait_all(slot)

    pos = _slot_rows(route_ref, toff_ref, m)
    h2 = h2_ref[...]
    for rc in range(rows // m):
        r_iota = (lax.broadcasted_iota(jnp.int32, (m, m), 0) + rc * m).astype(F32)
        onehot = jnp.zeros((m, m), F32)
        for k in range(TOP_K):
            onehot = jnp.where(r_iota == pos[k], 1.0, onehot)
        buf_ref[slot, rc * m:(rc + 1) * m, :] = jnp.dot(
            onehot.astype(BF16), h2, preferred_element_type=F32).astype(BF16)
        for c in range((rc - 1) * per_chunk, rc * per_chunk) if rc > 0 else ():
            start(c)
    for c in range(n_chunks - per_chunk, n_entries):
        start(c)

    @pl.when(t == n_tiles - 1)
    def _():
        if n_tiles > 1:
            wait_all(1 - slot)
        wait_all(slot)


def _dispatch(table, h2, route, toff_b, n_rows, n_chunks):
    nt = route.shape[0]
    m = TOKEN_TILE
    d = h2.shape[1]
    rows = _tile_rows(m)
    grid_spec = pltpu.PrefetchScalarGridSpec(
        num_scalar_prefetch=1,
        grid=(nt,),
        in_specs=[pl.BlockSpec((m, d), lambda t, tab: (t, 0)),
                  pl.BlockSpec((1, 4 * TOP_K, m), lambda t, tab: (t, 0, 0)),
                  pl.BlockSpec((1, N_EXPERTS, LANES), lambda t, tab: (t, 0, 0))],
        out_specs=pl.BlockSpec(memory_space=pl.ANY),
        scratch_shapes=[pltpu.VMEM((2, rows, d), BF16), pltpu.SemaphoreType.DMA((2,))],
    )
    return pl.pallas_call(
        functools.partial(_dispatch_kernel, n_chunks=n_chunks, n_tiles=nt),
        grid_spec=grid_spec,
        out_shape=jax.ShapeDtypeStruct((n_rows, d), BF16),
        compiler_params=_params(("arbitrary",)),
        name="dispatch",
    )(table, h2, route, toff_b)


def _expert_kernel(row0_ref, nblk_ref, ubase_ref, limit_ref, xs_hbm, wgu_ref, bgu_ref, wdn_ref, bdn_ref, ys_hbm,
                   wgu_bf, wdn_bf, xbuf, ybuf, sem_in, sem_out, pend_ref):
    e = pl.program_id(0)
    ne = pl.num_programs(0)
    bm = EXPERT_ROWS
    unit = xbuf.shape[1]
    per_unit = unit // bm
    dff = wdn_ref.shape[1]
    nblk = nblk_ref[e]
    n_units = (nblk + per_unit - 1) // per_unit
    base = ubase_ref[e]

    def unit_start(expert, s):
        true = row0_ref[expert] + s * unit
        start = jnp.minimum(true, limit_ref[0])
        return pl.multiple_of(start, bm), pl.multiple_of(true - start, bm)

    def in_copy(expert, s, slot):
        start, _ = unit_start(expert, s)
        return pltpu.make_async_copy(xs_hbm.at[pl.ds(start, unit)], xbuf.at[slot], sem_in.at[slot])

    def out_copy(j, slot, i):
        start = pl.multiple_of(row0_ref[e] + j * bm, bm)
        return pltpu.make_async_copy(ybuf.at[slot, pl.ds(i * bm, bm)], ys_hbm.at[pl.ds(start, bm)],
                                     sem_out.at[slot])

    def wait_pending(slot):
        count = pend_ref[slot]

        @pl.when(count == per_unit)
        def _():
            for i in range(per_unit):
                out_copy(0, slot, i).wait()

        for i in range(per_unit - 1):
            @pl.when((count < per_unit) & (i < count))
            def _():
                out_copy(0, slot, i).wait()
        pend_ref[slot] = 0

    @pl.when(e == 0)
    def _():
        pend_ref[0] = 0
        pend_ref[1] = 0

    @pl.when((e == 0) & (nblk > 0))
    def _():
        in_copy(e, 0, 0).start()

    wgu_bf[...] = wgu_ref[0].astype(BF16)
    wdn_bf[...] = wdn_ref[0].astype(BF16)

    def run_unit(s, carry):
        slot = (base + s) % 2
        in_copy(e, s, slot).wait()

        @pl.when(s + 1 < n_units)
        def _():
            in_copy(e, s + 1, 1 - slot).start()

        wait_pending(slot)
        _, lead = unit_start(e, s)

        def ffn(first, count):
            rows = count * bm
            x = xbuf[slot, pl.ds(pl.multiple_of(lead + first * bm, bm), rows), :]
            gu = jnp.dot(x, wgu_bf[...], preferred_element_type=F32) + bgu_ref[0]
            gate = jnp.minimum(gu[:, :dff], SWIGLU_LIMIT)
            up = jnp.clip(gu[:, dff:], -SWIGLU_LIMIT, SWIGLU_LIMIT)
            glu = gate * jax.nn.sigmoid(gate * SWIGLU_ALPHA)
            act = ((up + 1.0) * glu).astype(BF16)
            ybuf[slot, first * bm:first * bm + rows, :] = (
                jnp.dot(act, wdn_bf[...], preferred_element_type=F32) + bdn_ref[0]).astype(BF16)
            for i in range(first, first + count):
                out_copy(s * per_unit + i, slot, i).start()

        for first in range(0, per_unit, 2):
            have = nblk - s * per_unit - first

            @pl.when(have >= 2)
            def _():
                ffn(first, 2)

            @pl.when(have == 1)
            def _():
                ffn(first, 1)
        pend_ref[slot] = jnp.minimum(nblk - s * per_unit, per_unit)
        return carry

    lax.fori_loop(0, n_units, run_unit, 0)

    nxt = jnp.minimum(e + 1, ne - 1)

    @pl.when((e + 1 < ne) & (nblk_ref[nxt] > 0))
    def _():
        in_copy(nxt, 0, (base + n_units) % 2).start()

    @pl.when(e == ne - 1)
    def _():
        wait_pending(0)
        wait_pending(1)


def _experts(row0, nblk, ubase, limit, xs, w_gu, b_gu, w_dn, b_dn):
    n_rows, d = xs.shape
    bm = EXPERT_ROWS
    unit = EXPERT_UNIT_BLOCKS * bm
    ne, _, dff2 = w_gu.shape
    dff = w_dn.shape[1]
    exp3 = lambda e, r0, nb, ub, lim: (e, 0, 0)
    grid_spec = pltpu.PrefetchScalarGridSpec(
        num_scalar_prefetch=4,
        grid=(ne,),
        in_specs=[pl.BlockSpec(memory_space=pl.ANY),
                  pl.BlockSpec((1, d, dff2), exp3), pl.BlockSpec((1, 1, dff2), exp3),
                  pl.BlockSpec((1, dff, d), exp3), pl.BlockSpec((1, 1, d), exp3)],
        out_specs=pl.BlockSpec(memory_space=pl.ANY),
        scratch_shapes=[pltpu.VMEM((d, dff2), BF16), pltpu.VMEM((dff, d), BF16),
                        pltpu.VMEM((2, unit, d), BF16), pltpu.VMEM((2, unit, d), BF16),
                        pltpu.SemaphoreType.DMA((2,)), pltpu.SemaphoreType.DMA((2,)),
                        pltpu.SMEM((2,), jnp.int32)],
    )
    return pl.pallas_call(
        _expert_kernel,
        grid_spec=grid_spec,
        out_shape=jax.ShapeDtypeStruct((n_rows, d), BF16),
        compiler_params=_params(("arbitrary",)),
        name="experts",
    )(row0, nblk, ubase, limit, xs, w_gu, b_gu.reshape(ne, 1, dff2), w_dn, b_dn.reshape(ne, 1, d))


def _combine_kernel(tab_ref, ys_hbm, route_ref, toff_ref, x1_ref, gf_ref, o_ref, buf_ref, sem, *, n_chunks, tile0):
    step = pl.program_id(0)
    n_steps = pl.num_programs(0)
    t = tile0 + step
    slot = step % 2
    nb, tr, d = x1_ref.shape
    m = nb * tr
    rows = buf_ref.shape[1]
    g = BF16_ROWS

    def fetch(tile, which):
        for c in range(n_chunks):
            pltpu.make_async_copy(
                ys_hbm.at[pl.ds(pl.multiple_of(tab_ref[tile, c], g), g)],
                buf_ref.at[which, pl.ds(c * g, g)], sem.at[which]).start()

    @pl.when(step == 0)
    def _():
        fetch(t, slot)

    @pl.when(step + 1 < n_steps)
    def _():
        fetch(t + 1, 1 - slot)

    pos = _slot_rows(route_ref, toff_ref, m)
    gates = [route_ref[0, 2 * TOP_K + k:2 * TOP_K + k + 1, :] for k in range(TOP_K)]
    stacked = jnp.concatenate(pos + gates + [jnp.zeros((LANES - 2 * TOP_K, m), F32)], axis=0)
    cols = stacked.T
    for _ in range(n_chunks):
        pltpu.make_async_copy(ys_hbm.at[pl.ds(0, g)], buf_ref.at[slot, pl.ds(0, g)], sem.at[slot]).wait()

    acc = jnp.zeros((m, d), F32)
    for rc in range(rows // m):
        c_iota = (lax.broadcasted_iota(jnp.int32, (m, m), 1) + rc * m).astype(F32)
        weights = jnp.zeros((m, m), F32)
        for k in range(TOP_K):
            weights = jnp.where(c_iota == cols[:, k:k + 1], cols[:, TOP_K + k:TOP_K + k + 1], weights)
        acc = acc + jnp.dot(weights.astype(BF16), buf_ref[slot, rc * m:(rc + 1) * m, :],
                            preferred_element_type=F32)
    o_ref[...] = x1_ref[...] + gf_ref[...] * acc.reshape(nb, tr, d)


def _combine(table, ys, route, toff_b, x1, gf, nb, tr, tile0, n_chunks):
    nbt, t, d = x1.shape
    m = nb * tr
    assert m == TOKEN_TILE
    rows = _tile_rows(m)
    tiles_per_seq = t // tr
    n_steps = (nbt // nb) * tiles_per_seq
    xmap = lambda s, tab: (s // tiles_per_seq, s % tiles_per_seq, 0)
    grid_spec = pltpu.PrefetchScalarGridSpec(
        num_scalar_prefetch=1,
        grid=(n_steps,),
        in_specs=[pl.BlockSpec(memory_space=pl.ANY),
                  pl.BlockSpec((1, 4 * TOP_K, m), lambda s, tab: (tile0 + s, 0, 0)),
                  pl.BlockSpec((1, N_EXPERTS, LANES), lambda s, tab: (tile0 + s, 0, 0)),
                  pl.BlockSpec((nb, tr, d), xmap),
                  pl.BlockSpec((nb, 1, d), lambda s, tab: (s // tiles_per_seq, 0, MOD_GATE_FFN))],
        out_specs=pl.BlockSpec((nb, tr, d), xmap),
        scratch_shapes=[pltpu.VMEM((2, rows, d), BF16), pltpu.SemaphoreType.DMA((2,))],
    )
    return pl.pallas_call(
        functools.partial(_combine_kernel, n_chunks=n_chunks, tile0=tile0),
        grid_spec=grid_spec,
        out_shape=jax.ShapeDtypeStruct((nbt, t, d), F32),
        compiler_params=_params(("arbitrary",)),
        name="combine",
    )(table, ys, route, toff_b, x1, gf)


def _block_diag(w, groups):
    n, k, _ = w.shape
    w = w.reshape(n // groups, groups, k, k)
    eye = jnp.eye(groups, dtype=w.dtype)
    return jnp.einsum("ngij,gh->ngihj", w, eye).reshape(n // groups, groups * k, groups * k)


def _layer(xp, xs, mod_p, mod_s, k_cache, v_cache, conv_state, lru_state, lw):
    (ln_mix, ln_ffn, w_in, q_norm, k_norm, rel_bias, conv_w, conv_b, w_rg, b_rg, w_ig, b_ig, lam,
     w_out, w_router, b_router, w_gu, b_gu, w_dn, b_dn) = lw
    bp, s, d = xp.shape
    bs, ts, _ = xs.shape
    aw = w_out.shape[0] // 2
    nh = aw // HEAD_DIM
    m = TOKEN_TILE
    assert s % m == 0 and bs * ts == m and s % ATTN_Q_TILE == 0

    w_in_bf = w_in.astype(BF16)
    w_out_bf = w_out.astype(BF16)
    qn_t = jnp.tile(q_norm * (HEAD_DIM ** -0.5), nh).reshape(1, aw)
    kn_t = jnp.tile(k_norm, nh).reshape(1, aw)
    head_mean = _block_diag(jnp.full((nh, HEAD_DIM, HEAD_DIM), 1.0 / HEAD_DIM, F32), nh)[0].astype(BF16)
    groups = MXU_DIM // w_rg.shape[-1]
    wa_bd = _block_diag(w_rg, groups).astype(BF16)
    wx_bd = _block_diag(w_ig, groups).astype(BF16)
    lw_c = b_rg.size
    b_a = b_rg.reshape(1, lw_c)
    b_x = b_ig.reshape(1, lw_c)
    lam2 = lam.reshape(1, lw_c)
    cb2 = conv_b.reshape(1, lw_c)
    ln_mix2 = ln_mix.reshape(1, d)
    ln_ffn2 = ln_ffn.reshape(1, d)
    wr_t = w_router.T.astype(BF16)
    br = b_router.reshape(-1, 1)
    tab_p = _bias_table(rel_bias, 3 * ATTN_Q_TILE - 1)
    r_cache = k_cache.shape[1]
    tab_s = _bias_table(rel_bias, r_cache + ts - 1)

    zeros_pre = jnp.zeros((bp, SUBLANES, lw_c), F32)
    zeros_h = jnp.zeros((bp, 1, lw_c), F32)
    pre_s = jnp.pad(conv_state, ((0, 0), (SUBLANES - (CONV_WIDTH - 1), 0), (0, 0)))
    lru_w = (conv_w, cb2, wa_bd, wx_bd, b_a, b_x, lam2)
    qp, kp, vp, k32p, v32p, lru_p, tail_p, hl_p = _mixin(
        xp, mod_p,ln_mix2, w_in_bf, qn_t, kn_t, head_mean, zeros_pre, zeros_h, *lru_w, 1, m)
    qs, ks, vs, k32s, v32s, lru_s, tail_s, hl_s = _mixin(
        xs, mod_s,ln_mix2, w_in_bf, qn_t, kn_t, head_mean, pre_s, lru_state[:, None, :], *lru_w, bs, ts)
    attn_p = _attn_prompt(qp, kp, vp, tab_p)
    attn_s = _attn_step(qs, ks, vs, jnp.transpose(k_cache, (0, 2, 3, 1)), jnp.transpose(v_cache, (0, 2, 3, 1)),
                        tab_s, ATTN_STEP_BATCH)

    n_tiles = bp * (s // m) + 1
    x1p, h2, route, cnt = _outproj(attn_p, lru_p, xp, mod_p,ln_ffn2, w_out_bf, wr_t, br,
                                   1, m, n_tiles, 0, None)
    x1s, h2, route, cnt = _outproj(attn_s, lru_s, xs, mod_s,ln_ffn2, w_out_bf, wr_t, br,
                                   bs, ts, n_tiles, n_tiles - 1, (h2, route, cnt))

    n_chunks, _, _, n_rows = _table_sizes(n_tiles)
    assert TOP_K * m * n_tiles >= EXPERT_UNIT_BLOCKS * EXPERT_ROWS
    dispatch_tab, combine_tab, toff_b, row0, nblk, ubase, limit = _route_tables(cnt[:, :, 0].astype(jnp.int32))
    xs_sorted = _dispatch(dispatch_tab, h2, route, toff_b, n_rows, n_chunks)
    ys_sorted = _experts(row0, nblk, ubase, limit, xs_sorted, w_gu, b_gu, w_dn, b_dn)
    yp = _combine(combine_tab, ys_sorted, route, toff_b, x1p, mod_p, 1, m, 0, n_chunks)
    ysm = _combine(combine_tab, ys_sorted, route, toff_b, x1s, mod_s, bs, ts, n_tiles - 1, n_chunks)

    keep = k32p.shape[1]
    new = (k32p.reshape(bp, keep, nh, HEAD_DIM), v32p.reshape(bp, keep, nh, HEAD_DIM),
           tail_p[:, SUBLANES - (CONV_WIDTH - 1):, :], hl_p[:, 0, :],
           k32s.reshape(bs, ts, nh, HEAD_DIM), v32s.reshape(bs, ts, nh, HEAD_DIM),
           tail_s[:, SUBLANES - (CONV_WIDTH - 1):, :], hl_s[:, 0, :])
    return yp, ysm, new


def kernel(x_prompt, x_sample, c_prompt, c_sample, cache_k, cache_v, state_conv, state_lru, ln_mix_w, ln_ffn_w, w_ada, b_ada, w_in, q_norm_w, k_norm_w, rel_bias, conv_w, conv_b, w_rgate, b_rgate, w_igate, b_igate, lru_lambda, w_out, w_router, b_router, w_gate_up, b_gate_up, w_down, b_down):
    depth = w_in.shape[0]
    yp, ys = x_prompt, x_sample
    collected = [[] for _ in range(8)]
    for l in range(depth):
        mod_p, mod_s = _ada(c_prompt, c_sample, w_ada[l], b_ada[l])
        lw = (ln_mix_w[l], ln_ffn_w[l], w_in[l], q_norm_w[l], k_norm_w[l], rel_bias[l], conv_w[l], conv_b[l],
              w_rgate[l], b_rgate[l], w_igate[l], b_igate[l], lru_lambda[l], w_out[l], w_router[l], b_router[l],
              w_gate_up[l], b_gate_up[l], w_down[l], b_down[l])
        yp, ys, new = _layer(yp, ys, mod_p, mod_s, cache_k[l], cache_v[l], state_conv[l], state_lru[l], lw)
        for acc, val in zip(collected, new):
            acc.append(val)
    return (yp, ys) + tuple(jnp.stack(vals) for vals in collected)
```

```python
import functools

import jax
import jax.numpy as jnp
from jax import lax
from jax.experimental import pallas as pl
from jax.experimental.pallas import tpu as pltpu

F32 = jnp.float32
BF16 = jnp.bfloat16

CHUNK = 64
N_LEFT_CHUNKS = 8
ATTN_WINDOW = N_LEFT_CHUNKS * CHUNK
HEAD_DIM = 64
REL_CLIP = 128
CONV_WIDTH = 4
LRU_C = 8.0
N_EXPERTS = 32
TOP_K = 4
SWIGLU_LIMIT = 7.0
SWIGLU_ALPHA = 1.702
NORM_EPS = 1e-6
NEG_INF = -1e30

LANES = 128
SUBLANES = 8
BF16_ROWS = 16
MXU_DIM = 256

TOKEN_TILE = 512
ATTN_Q_TILE = 256
ATTN_STEP_BATCH = 4
EXPERT_ROWS = 256
EXPERT_UNIT_BLOCKS = 4
BIAS_TABLE = 1024
VMEM_LIMIT = 56 * 1024 * 1024


def _params(sem, vmem=VMEM_LIMIT):
    return pltpu.CompilerParams(dimension_semantics=sem, vmem_limit_bytes=vmem)


def _ada_kernel(cp_ref, cs_ref, w_ref, b_ref, op_ref, os_ref):
    w = w_ref[...].astype(BF16)
    for c_ref, o_ref in ((cp_ref, op_ref), (cs_ref, os_ref)):
        c = c_ref[...]
        s = (c * jax.nn.sigmoid(c)).astype(BF16)
        o_ref[...] = jnp.dot(s, w, preferred_element_type=F32) + b_ref[...]


def _ada(c_p, c_s, w_ada, b_ada):
    (n_p, d), n_s = c_p.shape, c_s.shape[0]
    nout = w_ada.shape[1]
    tn = 1024
    mod_p, mod_s = pl.pallas_call(
        _ada_kernel,
        grid=(nout // tn,),
        in_specs=[pl.BlockSpec((n_p, d), lambda j: (0, 0)), pl.BlockSpec((n_s, d), lambda j: (0, 0)),
                  pl.BlockSpec((d, tn), lambda j: (0, j)),
                  pl.BlockSpec((1, tn), lambda j: (0, j))],
        out_specs=[pl.BlockSpec((n_p, tn), lambda j: (0, j)), pl.BlockSpec((n_s, tn), lambda j: (0, j))],
        out_shape=[jax.ShapeDtypeStruct((n_p, nout), F32), jax.ShapeDtypeStruct((n_s, nout), F32)],
        compiler_params=_params(("arbitrary",)),
        name="ada",
    )(c_p, c_s, w_ada, b_ada.reshape(1, nout))
    return mod_p.reshape(n_p, 1, nout), mod_s.reshape(n_s, 1, nout)


MOD_SHIFT_MIX, MOD_SCALE_MIX, MOD_GATE_MIX, MOD_SHIFT_FFN, MOD_SCALE_FFN, MOD_GATE_FFN = range(6)


def _mod_spec(nb, d, term):
    return pl.BlockSpec((nb, 1, d), lambda b, i: (b, 0, term))


def _mixin_kernel(x_ref, sh_ref, sc_ref, ln_ref, win_ref, qn_ref, kn_ref, bd_ref,
                  pre_ref, h0_ref, cw_ref, cb_ref, wa_ref, wx_ref, ba_ref, bx_ref, lam_ref,
                  q_ref, k_ref, v_ref, k32_ref, v32_ref, lru_ref, tail_ref, hl_ref, cx_ref, ch_ref):
    nb, tr, d = x_ref.shape
    m = nb * tr
    aw = q_ref.shape[-1]

    @pl.when(pl.program_id(1) == 0)
    def _():
        cx_ref[...] = pre_ref[...]
        ch_ref[...] = h0_ref[...]

    x = x_ref[...]
    ms = jnp.mean(x * x, axis=-1, keepdims=True)
    h = x * lax.rsqrt(ms + NORM_EPS) * (ln_ref[...] * (1.0 + sc_ref[...])) + sh_ref[...]
    hb = h.reshape(m, d).astype(BF16)

    def proj(part):
        return jnp.dot(hb, win_ref[:, part * aw:(part + 1) * aw], preferred_element_type=F32)

    def head_norm(t, w_ref):
        msq = jnp.dot((t * t).astype(BF16), bd_ref[...], preferred_element_type=F32)
        return t * lax.rsqrt(msq + NORM_EPS) * w_ref[...]

    lru_out, new_tail, h_last = _lru_branch(
        proj(3).reshape(nb, tr, aw), proj(4).reshape(nb, tr, aw), cw_ref, cb_ref,
        wa_ref, wx_ref, ba_ref, bx_ref, lam_ref, cx_ref, ch_ref)
    lru_ref[...] = lru_out
    tail_ref[...] = new_tail
    hl_ref[...] = h_last
    q = head_norm(proj(0), qn_ref)
    k = head_norm(proj(1), kn_ref)
    v = proj(2)
    q_ref[...] = q.astype(BF16).reshape(nb, tr, aw)
    k_ref[...] = k.astype(BF16).reshape(nb, tr, aw)
    v_ref[...] = v.astype(BF16).reshape(nb, tr, aw)
    k32_ref[...] = k.reshape(nb, tr, aw)
    v32_ref[...] = v.reshape(nb, tr, aw)


def _mixin(x, mod, ln_w, w_in_bf, qn_t, kn_t, bd, pre, h0, conv_w, conv_b, wa_bd, wx_bd, b_a, b_x, lam,
           nb, tr):
    nbt, t, d = x.shape
    aw = qn_t.shape[-1]
    c = pre.shape[-1]
    assert c == aw
    keep = min(ATTN_WINDOW, t)
    assert tr == keep or t == tr
    grid = (nbt // nb, t // tr)
    xmap = lambda b, i: (b, i, 0)
    mmap = lambda b, i: (b, 0, 0)
    cmap = lambda b, i: (0, 0)
    cmap3 = lambda b, i: (0, 0, 0)
    tmap = lambda b, i: (b, 0, 0)
    big = pl.BlockSpec((nb, tr, aw), xmap)
    tail = pl.BlockSpec((nb, keep, aw), tmap)
    row = pl.BlockSpec((1, c), cmap)
    return pl.pallas_call(
        _mixin_kernel,
        grid=grid,
        in_specs=[pl.BlockSpec((nb, tr, d), xmap),
                  _mod_spec(nb, d, MOD_SHIFT_MIX), _mod_spec(nb, d, MOD_SCALE_MIX),
                  pl.BlockSpec((1, d), cmap),
                  pl.BlockSpec(w_in_bf.shape, cmap),
                  pl.BlockSpec((1, aw), cmap), pl.BlockSpec((1, aw), cmap),
                  pl.BlockSpec(bd.shape, cmap),
                  pl.BlockSpec((nb, SUBLANES, c), tmap), pl.BlockSpec((nb, 1, c), tmap),
                  pl.BlockSpec(conv_w.shape, cmap), row,
                  pl.BlockSpec(wa_bd.shape, cmap3), pl.BlockSpec(wx_bd.shape, cmap3),
                  row, row, row],
        out_specs=[big, big, big, tail, tail, big,
                   pl.BlockSpec((nb, SUBLANES, c), tmap), pl.BlockSpec((nb, 1, c), tmap)],
        out_shape=[jax.ShapeDtypeStruct((nbt, t, aw), BF16)] * 3
        + [jax.ShapeDtypeStruct((nbt, keep, aw), F32)] * 2
        + [jax.ShapeDtypeStruct((nbt, t, c), BF16),
           jax.ShapeDtypeStruct((nbt, SUBLANES, c), F32),
           jax.ShapeDtypeStruct((nbt, 1, c), F32)],
        scratch_shapes=[pltpu.VMEM((nb, SUBLANES, c), F32), pltpu.VMEM((nb, 1, c), F32)],
        compiler_params=_params(("arbitrary", "arbitrary")),
        name="mixin",
    )(x, mod, mod, ln_w, w_in_bf, qn_t, kn_t, bd, pre, h0, conv_w, conv_b, wa_bd, wx_bd, b_a, b_x, lam)


def _bias_table(rel_bias, off):
    h = rel_bias.shape[0]
    left = off - REL_CLIP
    right = BIAS_TABLE - left - (2 * REL_CLIP + 1)
    assert left >= 0 and right >= 0
    return jnp.concatenate([jnp.broadcast_to(rel_bias[:, :1], (h, left)), rel_bias,
                            jnp.broadcast_to(rel_bias[:, -1:], (h, right))], axis=1)


def _toeplitz(tab_row, rows, cols):
    t = jnp.broadcast_to(tab_row, (rows, BIAS_TABLE))
    t = pltpu.roll(t, BIAS_TABLE - (rows - 1), 1, stride=1, stride_axis=0)
    return t[:, :cols]


def _attn_kernel(q_ref, k0_ref, k1_ref, k2_ref, v0_ref, v1_ref, v2_ref, tab_ref, o_ref, bias_ref):
    b = pl.program_id(0)
    s = pl.program_id(1)
    qt = q_ref.shape[1]
    nk = 3 * qt
    nh = bias_ref.shape[0]

    @pl.when((b == 0) & (s == 0))
    def _():
        qi = lax.broadcasted_iota(jnp.int32, (qt, nk), 0) // CHUNK
        kc = lax.broadcasted_iota(jnp.int32, (qt, nk), 1) // CHUNK
        for h in range(nh):
            band = jnp.where(kc <= qi + N_LEFT_CHUNKS, _toeplitz(tab_ref[h:h + 1, :], qt, nk), NEG_INF)
            bias_ref[h] = jnp.where(kc >= qi, band, NEG_INF)

    pair_w = 2 * HEAD_DIM

    def attend(mask_start):
        q = q_ref[0]
        kcat = jnp.concatenate([k0_ref[0], k1_ref[0], k2_ref[0]], axis=0)
        vcat = jnp.concatenate([v0_ref[0], v1_ref[0], v2_ref[0]], axis=0)
        first = lax.broadcasted_iota(jnp.int32, (qt, pair_w), 1) < HEAD_DIM
        keep = [jnp.where(first, 1.0, 0.0).astype(BF16), jnp.where(first, 0.0, 1.0).astype(BF16)]
        if mask_start:
            in_seq = lax.broadcasted_iota(jnp.int32, (qt, nk), 1) >= (2 - s) * qt
        outs = []
        for pair in range(nh // 2):
            sl = slice(pair * pair_w, (pair + 1) * pair_w)
            q2, k2, v2 = q[:, sl], kcat[:, sl], vcat[:, sl]
            per_head = []
            for sub in range(2):
                sc = lax.dot_general(q2 * keep[sub], k2, (((1,), (1,)), ((), ())), preferred_element_type=F32)
                sc = sc + bias_ref[2 * pair + sub]
                if mask_start:
                    sc = jnp.where(in_seq, sc, NEG_INF)
                mx = jnp.max(sc, axis=-1, keepdims=True)
                p = jnp.exp(sc - mx)
                l = jnp.sum(p, axis=-1, keepdims=True)
                per_head.append(jnp.dot(p.astype(BF16), v2, preferred_element_type=F32) / l)
            outs.append(jnp.where(first, per_head[0], per_head[1]))
        o_ref[0] = jnp.concatenate(outs, axis=-1).astype(BF16)

    @pl.when(s < 2)
    def _():
        attend(True)

    @pl.when(s >= 2)
    def _():
        attend(False)


def _attn_prompt(q, k, v, tab):
    b, s, aw = q.shape
    qt = ATTN_Q_TILE
    nh = aw // HEAD_DIM
    qspec = pl.BlockSpec((1, qt, aw), lambda i, j: (i, j, 0))

    def kspec(back):
        return pl.BlockSpec((1, qt, aw), lambda i, j: (i, jnp.maximum(j - back, 0), 0))

    return pl.pallas_call(
        _attn_kernel,
        grid=(b, s // qt),
        in_specs=[qspec, kspec(2), kspec(1), kspec(0), kspec(2), kspec(1), kspec(0),
                  pl.BlockSpec(tab.shape, lambda i, j: (0, 0))],
        out_specs=qspec,
        out_shape=jax.ShapeDtypeStruct((b, s, aw), BF16),
        scratch_shapes=[pltpu.VMEM((nh, qt, 3 * qt), F32)],
        compiler_params=_params(("arbitrary", "arbitrary")),
        name="attn_prompt",
    )(q, k, k, k, v, v, v, tab)


def _attn_step_kernel(q_ref, kn_ref, vn_ref, ck_ref, cv_ref, tab_ref, o_ref, bias_ref):
    step = pl.program_id(0)
    nbs, t, aw = q_ref.shape
    nh = aw // HEAD_DIM
    rows = nh * t
    r = ck_ref.shape[-1]
    nk = r + LANES
    nt_dims = (((1,), (1,)), ((), ()))

    @pl.when(step == 0)
    def _():
        ok = lax.broadcasted_iota(jnp.int32, (t, nk), 1) < r + t
        for h in range(nh):
            bias_ref[h * t:(h + 1) * t, :] = jnp.where(ok, _toeplitz(tab_ref[h:h + 1, :], t, nk), NEG_INF)

    own = (lax.broadcasted_iota(jnp.int32, (rows, aw), 0) // t
           == lax.broadcasted_iota(jnp.int32, (rows, aw), 1) // HEAD_DIM)
    own_f = jnp.where(own, 1.0, 0.0)
    own_bf = own_f.astype(BF16)
    pad = jnp.zeros((LANES - t, aw), BF16)
    for b in range(nbs):
        q_bd = jnp.concatenate([q_ref[b]] * nh, axis=0) * own_bf
        k_old = ck_ref[b].reshape(aw, r).astype(BF16)
        v_old = cv_ref[b].reshape(aw, r).astype(BF16)
        k_new = jnp.concatenate([kn_ref[b], pad], axis=0)
        v_new = jnp.concatenate([vn_ref[b], pad], axis=0)
        s_old = jnp.dot(q_bd, k_old, preferred_element_type=F32) + bias_ref[:, :r]
        s_new = lax.dot_general(q_bd, k_new, nt_dims, preferred_element_type=F32) + bias_ref[:, r:]
        mx = jnp.maximum(jnp.max(s_old, axis=-1, keepdims=True), jnp.max(s_new, axis=-1, keepdims=True))
        p_old = jnp.exp(s_old - mx)
        p_new = jnp.exp(s_new - mx)
        l = jnp.sum(p_old, axis=-1, keepdims=True) + jnp.sum(p_new, axis=-1, keepdims=True)
        o_all = (lax.dot_general(p_old.astype(BF16), v_old, nt_dims, preferred_element_type=F32)
                 + jnp.dot(p_new.astype(BF16), v_new, preferred_element_type=F32))
        o_all = o_all * own_f / l
        out = o_all[0:t]
        for h in range(1, nh):
            out = out + o_all[h * t:(h + 1) * t]
        o_ref[b] = out.astype(BF16)


def _attn_step(q, kn, vn, ck, cv, tab, nbs):
    b, t, aw = q.shape
    nh = aw // HEAD_DIM
    r = ck.shape[-1]
    new = pl.BlockSpec((nbs, t, aw), lambda i: (i, 0, 0))
    old = pl.BlockSpec((nbs, nh, HEAD_DIM, r), lambda i: (i, 0, 0, 0))
    return pl.pallas_call(
        _attn_step_kernel,
        grid=(b // nbs,),
        in_specs=[new, new, new, old, old, pl.BlockSpec(tab.shape, lambda i: (0, 0))],
        out_specs=new,
        out_shape=jax.ShapeDtypeStruct((b, t, aw), BF16),
        scratch_shapes=[pltpu.VMEM((nh * t, r + LANES), F32)],
        compiler_params=_params(("arbitrary",)),
        name="attn_step",
    )(q, kn, vn, ck, cv, tab)


def _gelu_tanh(x):
    return x * (0.5 * (1.0 + jnp.tanh(0.7978845608028654 * (x + 0.044715 * (x * x * x)))))


def _lru_branch(x, yg, cw_ref, cb_ref, wa_ref, wx_ref, ba_ref, bx_ref, lam_ref, cx_ref, ch_ref):
    nb, tr, c = x.shape
    m = nb * tr
    half = c // 2
    xp = jnp.concatenate([cx_ref[...], x], axis=1)
    new_tail = xp[:, tr:tr + SUBLANES, :]
    groups = tr // SUBLANES
    xg = xp.reshape(nb * (groups + 1), SUBLANES, c)
    first_rows = lax.broadcasted_iota(jnp.int32, (nb, groups, SUBLANES, c), 2)
    y = cb_ref[...] + cw_ref[CONV_WIDTH - 1:CONV_WIDTH, :] * x
    for back in range(1, CONV_WIDTH):
        rot = pltpu.roll(xg, back, 1).reshape(nb, groups + 1, SUBLANES, c)
        shifted = jnp.where(first_rows >= back, rot[:, 1:], rot[:, :groups]).reshape(nb, tr, c)
        y = y + cw_ref[CONV_WIDTH - 1 - back:CONV_WIDTH - back, :] * shifted
    y2 = y.reshape(m, c)
    yb = y2.astype(BF16)

    def gate(w_ref, b_ref):
        g = jnp.concatenate(
            [jnp.dot(yb[:, :half], w_ref[0], preferred_element_type=F32),
             jnp.dot(yb[:, half:], w_ref[1], preferred_element_type=F32)], axis=1)
        return jax.nn.sigmoid(g + b_ref[...])

    rg = gate(wa_ref, ba_ref)
    ig = gate(wx_ref, bx_ref)
    lam = lam_ref[...]
    log_sig = jnp.minimum(lam, 0.0) - jnp.log1p(jnp.exp(-jnp.abs(lam)))
    log_a = rg * (LRU_C * log_sig)
    a_cum = jnp.exp(log_a)
    b_cum = jnp.sqrt(-jnp.tanh(log_a) * (a_cum * a_cum + 1.0)) * (ig * y2)
    a_cum = a_cum.reshape(nb * groups, SUBLANES, c)
    b_cum = b_cum.reshape(nb * groups, SUBLANES, c)
    row = lax.broadcasted_iota(jnp.int32, a_cum.shape, 1)
    dist = 1
    while dist < SUBLANES:
        keep = row >= dist
        a_sh = jnp.where(keep, pltpu.roll(a_cum, dist, 1), 1.0)
        b_sh = jnp.where(keep, pltpu.roll(b_cum, dist, 1), 0.0)
        b_cum = a_cum * b_sh + b_cum
        a_cum = a_cum * a_sh
        dist *= 2
    a_grp = a_cum.reshape(nb, groups, SUBLANES, c)
    b_grp = b_cum.reshape(nb, groups, SUBLANES, c)
    carry = ch_ref[...]
    pieces = []
    for grp in range(groups):
        h_grp = a_grp[:, grp] * carry + b_grp[:, grp]
        carry = h_grp[:, SUBLANES - 1:SUBLANES, :]
        pieces.append(h_grp)
    h = jnp.concatenate(pieces, axis=1)
    ch_ref[...] = carry
    cx_ref[...] = new_tail
    return (h * _gelu_tanh(yg)).astype(BF16), new_tail, carry


def _outproj_kernel(*refs, aliased):
    (at_ref, lr_ref, x_ref, gm_ref, shf_ref, scf_ref, lnf_ref, wo_ref, wr_ref, br_ref) = refs[:10]
    x1_ref, h2_ref, route_ref, cnt_ref = refs[10 + aliased:]
    nb, tr, d = x_ref.shape
    m = nb * tr
    aw = at_ref.shape[-1]
    ne = wr_ref.shape[0]
    at = at_ref[...].reshape(m, aw)
    lr = lr_ref[...].reshape(m, aw)
    mix = (jnp.dot(at, wo_ref[0:aw, :], preferred_element_type=F32)
           + jnp.dot(lr, wo_ref[aw:2 * aw, :], preferred_element_type=F32))
    x1 = x_ref[...] + gm_ref[...] * mix.reshape(nb, tr, d)
    x1_ref[...] = x1
    ms = jnp.mean(x1 * x1, axis=-1, keepdims=True)
    h2 = (x1 * lax.rsqrt(ms + NORM_EPS) * (lnf_ref[...] * (1.0 + scf_ref[...])) + shf_ref[...]).reshape(m, d)
    h2_ref[...] = h2.astype(BF16)

    logits = lax.dot_general(wr_ref[...], h2.astype(BF16), (((1,), (1,)), ((), ())),
                             preferred_element_type=F32) + br_ref[...]
    e_iota = lax.broadcasted_iota(jnp.int32, (ne, m), 0).astype(F32)
    vals = logits
    top_v, sels = [], []
    for k in range(TOP_K):
        mx = jnp.max(vals, axis=0, keepdims=True)
        idx = jnp.min(jnp.where(vals == mx, e_iota, float(ne)), axis=0, keepdims=True)
        sel = e_iota == idx
        vals = jnp.where(sel, -jnp.inf, vals)
        top_v.append(mx)
        sels.append(sel)
        route_ref[0, k:k + 1, :] = idx
    ex = [jnp.exp(v - top_v[0]) for v in top_v]
    den = ex[0] + ex[1] + ex[2] + ex[3]
    chosen = jnp.zeros((ne, m), F32)
    for k in range(TOP_K):
        route_ref[0, 2 * TOP_K + k:2 * TOP_K + k + 1, :] = ex[k] / den
        chosen = chosen + jnp.where(sels[k], 1.0, 0.0)
    before = (lax.broadcasted_iota(jnp.int32, (m, m), 0) < lax.broadcasted_iota(jnp.int32, (m, m), 1))
    rank = jnp.dot(chosen.astype(BF16), jnp.where(before, 1.0, 0.0).astype(BF16), preferred_element_type=F32)
    for k in range(TOP_K):
        route_ref[0, TOP_K + k:TOP_K + k + 1, :] = jnp.sum(jnp.where(sels[k], rank, 0.0), axis=0, keepdims=True)
    route_ref[0, 3 * TOP_K:4 * TOP_K, :] = jnp.zeros((TOP_K, m), F32)
    cnt_ref[0] = jnp.broadcast_to(jnp.sum(chosen, axis=1, keepdims=True), (ne, LANES))


def _outproj(attn, lru_o, x, mod, lnf, w_out_bf, wr_t, br, nb, tr, n_tiles, tile0, prev):
    nbt, t, d = x.shape
    aw = attn.shape[-1]
    m = nb * tr
    assert m == TOKEN_TILE
    ne = wr_t.shape[0]
    tiles_per_seq = t // tr
    xmap = lambda b, i: (b, i, 0)
    c2 = lambda b, i: (0, 0)
    tile = lambda b, i: (tile0 + b * tiles_per_seq + i, 0)
    tile3 = lambda b, i: (tile0 + b * tiles_per_seq + i, 0, 0)
    in_specs = [pl.BlockSpec((nb, tr, aw), xmap), pl.BlockSpec((nb, tr, aw), xmap),
                pl.BlockSpec((nb, tr, d), xmap),
                _mod_spec(nb, d, MOD_GATE_MIX), _mod_spec(nb, d, MOD_SHIFT_FFN), _mod_spec(nb, d, MOD_SCALE_FFN),
                pl.BlockSpec((1, d), c2), pl.BlockSpec(w_out_bf.shape, c2),
                pl.BlockSpec(wr_t.shape, c2), pl.BlockSpec((ne, 1), c2)]
    args = [attn, lru_o, x, mod, mod, mod, lnf, w_out_bf, wr_t, br]
    aliases = {}
    if prev is not None:
        in_specs += [pl.BlockSpec(memory_space=pl.ANY)] * 3
        args += list(prev)
        aliases = {10: 1, 11: 2, 12: 3}
    return pl.pallas_call(
        functools.partial(_outproj_kernel, aliased=len(aliases)),
        grid=(nbt // nb, tiles_per_seq),
        in_specs=in_specs,
        out_specs=[pl.BlockSpec((nb, tr, d), xmap), pl.BlockSpec((m, d), tile),
                   pl.BlockSpec((1, 4 * TOP_K, m), tile3), pl.BlockSpec((1, ne, LANES), tile3)],
        out_shape=[jax.ShapeDtypeStruct((nbt, t, d), F32),
                   jax.ShapeDtypeStruct((n_tiles * m, d), BF16),
                   jax.ShapeDtypeStruct((n_tiles, 4 * TOP_K, m), F32),
                   jax.ShapeDtypeStruct((n_tiles, ne, LANES), F32)],
        input_output_aliases=aliases,
        compiler_params=_params(("arbitrary", "arbitrary")),
        name="outproj",
    )(*args)


def _tile_rows(m):
    cap = TOP_K * m + N_EXPERTS * (BF16_ROWS - 1) + BF16_ROWS
    return -(-cap // TOKEN_TILE) * TOKEN_TILE


def _table_sizes(nt):
    g = BF16_ROWS
    m = TOKEN_TILE
    n_chunks = _tile_rows(m) // g
    n_gap = -(-(N_EXPERTS * (EXPERT_ROWS // g - 1)) // nt)
    bound = TOP_K * m * nt + nt * N_EXPERTS * (g - 1) + N_EXPERTS * (EXPERT_ROWS - g)
    n_sorted = -(-bound // EXPERT_ROWS) * EXPERT_ROWS
    return n_chunks, n_gap, n_sorted, n_sorted + 2 * (n_chunks + n_gap) * g


def _route_tables(cnt):
    nt = cnt.shape[0]
    g = BF16_ROWS
    bm = EXPERT_ROWS
    n_chunks, n_gap, n_sorted, _ = _table_sizes(nt)
    e_ids = jnp.arange(N_EXPERTS, dtype=jnp.int32)
    t_ids = jnp.arange(nt, dtype=jnp.int32)
    upto = (e_ids[:, None] <= e_ids[None, :]).astype(jnp.int32)
    pc = (cnt + g - 1) // g * g
    ctile = jnp.sum(pc[:, :, None] * upto[None], axis=1)
    toff = ctile - pc
    trow = ctile[:, -1]
    tot = jnp.sum(pc, axis=0)
    reg = (tot + bm - 1) // bm * bm
    creg = jnp.sum(reg[:, None] * upto, axis=0)
    base = creg - reg
    earlier = (t_ids[:, None] < t_ids[None, :]).astype(jnp.int32)
    goff = base[None, :] + jnp.sum(pc[:, None, :] * earlier[:, :, None], axis=0)
    r = jnp.arange(n_chunks, dtype=jnp.int32) * g
    r3 = r[None, :, None]
    in_seg = (toff[:, None, :] <= r3) & (r3 < ctile[:, None, :])
    dst = jnp.sum(jnp.where(in_seg, (goff - toff)[:, None, :], 0), axis=2) + r[None, :]
    dst = jnp.where(r[None, :] < trow[:, None], dst, -1)
    gcnt = (reg - tot) // g
    gcum = jnp.sum(gcnt[:, None] * upto, axis=0)
    gstart = gcum - gcnt
    s = jnp.arange(nt * n_gap, dtype=jnp.int32)
    in_gap = (gstart[None, :] <= s[:, None]) & (s[:, None] < gcum[None, :])
    gdst = jnp.sum(jnp.where(in_gap, (base + tot - g * gstart)[None, :] + g * s[:, None], 0), axis=1)
    gdst = jnp.where(s < gcum[-1], gdst, -1).reshape(nt, n_gap)
    table = jnp.concatenate([dst, gdst], axis=1).astype(jnp.int32)
    n_entries = n_chunks + n_gap
    spare = n_sorted + ((t_ids % 2)[:, None] * n_entries + jnp.arange(n_entries, dtype=jnp.int32)[None, :]) * g
    dispatch_tab = jnp.where(table >= 0, table, spare).astype(jnp.int32)
    combine_tab = jnp.maximum(dst, 0).astype(jnp.int32)
    toff_b = jnp.broadcast_to(toff.astype(F32)[:, :, None], (nt, N_EXPERTS, LANES))
    limit = (creg[-1] - EXPERT_UNIT_BLOCKS * bm).astype(jnp.int32).reshape(1)
    nblk = reg // bm
    units = (nblk + EXPERT_UNIT_BLOCKS - 1) // EXPERT_UNIT_BLOCKS
    ubase = jnp.sum(units[:, None] * upto, axis=0) - units
    return (dispatch_tab, combine_tab, toff_b, base.astype(jnp.int32), nblk.astype(jnp.int32),
            ubase.astype(jnp.int32), limit)


def _slot_rows(route_ref, toff_ref, m):
    ne = toff_ref.shape[1]
    e_iota = lax.broadcasted_iota(jnp.int32, (ne, m), 0).astype(F32)
    toff_col = toff_ref[0][:, 0:1]
    pos = []
    for k in range(TOP_K):
        sel = e_iota == route_ref[0, k:k + 1, :]
        start = jnp.sum(jnp.where(sel, toff_col, 0.0), axis=0, keepdims=True)
        pos.append(start + route_ref[0, TOP_K + k:TOP_K + k + 1, :])
    return pos


def _dispatch_kernel(tab_ref, h2_ref, route_ref, toff_ref, xs_hbm, buf_ref, sem, *, n_chunks, n_tiles):
    t = pl.program_id(0)
    slot = t % 2
    m = h2_ref.shape[0]
    rows = buf_ref.shape[1]
    n_entries = tab_ref.shape[1]
    g = BF16_ROWS
    per_chunk = m // g

    def start(c):
        src = c * g if c < n_chunks else rows - g
        pltpu.make_async_copy(
            buf_ref.at[slot, pl.ds(src, g)],
            xs_hbm.at[pl.ds(pl.multiple_of(tab_ref[t, c], g), g)], sem.at[slot]).start()

    def wait_all(which):
        for _ in range(n_entries):
            pltpu.make_async_copy(buf_ref.at[which, pl.ds(0, g)], xs_hbm.at[pl.ds(0, g)], sem.at[which]).wait()

    @pl.when(t >= 2)
    def _():
        wait_all(slot)

    pos = _slot_rows(route_ref, toff_ref, m)
    h2 = h2_ref[...]
    for rc in range(rows // m):
        r_iota = (lax.broadcasted_iota(jnp.int32, (m, m), 0) + rc * m).astype(F32)
        onehot = jnp.zeros((m, m), F32)
        for k in range(TOP_K):
            onehot = jnp.where(r_iota == pos[k], 1.0, onehot)
        buf_ref[slot, rc * m:(rc + 1) * m, :] = jnp.dot(
            onehot.astype(BF16), h2, preferred_element_type=F32).astype(BF16)
        for c in range((rc - 1) * per_chunk, rc * per_chunk) if rc > 0 else ():
            start(c)
    for c in range(n_chunks - per_chunk, n_entries):
        start(c)

    @pl.when(t == n_tiles - 1)
    def _():
        if n_tiles > 1:
            wait_all(1 - slot)
        wait_all(slot)


def _dispatch(table, h2, route, toff_b, n_rows, n_chunks):
    nt = route.shape[0]
    m = TOKEN_TILE
    d = h2.shape[1]
    rows = _tile_rows(m)
    grid_spec = pltpu.PrefetchScalarGridSpec(
        num_scalar_prefetch=1,
        grid=(nt,),
        in_specs=[pl.BlockSpec((m, d), lambda t, tab: (t, 0)),
                  pl.BlockSpec((1, 4 * TOP_K, m), lambda t, tab: (t, 0, 0)),
                  pl.BlockSpec((1, N_EXPERTS, LANES), lambda t, tab: (t, 0, 0))],
        out_specs=pl.BlockSpec(memory_space=pl.ANY),
        scratch_shapes=[pltpu.VMEM((2, rows, d), BF16), pltpu.SemaphoreType.DMA((2,))],
    )
    return pl.pallas_call(
        functools.partial(_dispatch_kernel, n_chunks=n_chunks, n_tiles=nt),
        grid_spec=grid_spec,
        out_shape=jax.ShapeDtypeStruct((n_rows, d), BF16),
        compiler_params=_params(("arbitrary",)),
        name="dispatch",
    )(table, h2, route, toff_b)


def _expert_kernel(row0_ref, nblk_ref, ubase_ref, limit_ref, xs_hbm, wgu_ref, bgu_ref, wdn_ref, bdn_ref, ys_hbm,
                   wgu_bf, wdn_bf, xbuf, ybuf, sem_in, sem_out, pend_ref):
    e = pl.program_id(0)
    ne = pl.num_programs(0)
    bm = EXPERT_ROWS
    unit = xbuf.shape[1]
    per_unit = unit // bm
    dff = wdn_ref.shape[1]
    nblk = nblk_ref[e]
    n_units = (nblk + per_unit - 1) // per_unit
    base = ubase_ref[e]

    def unit_start(expert, s):
        true = row0_ref[expert] + s * unit
        start = jnp.minimum(true, limit_ref[0])
        return pl.multiple_of(start, bm), pl.multiple_of(true - start, bm)

    def in_copy(expert, s, slot):
        start, _ = unit_start(expert, s)
        return pltpu.make_async_copy(xs_hbm.at[pl.ds(start, unit)], xbuf.at[slot], sem_in.at[slot])

    def out_copy(j, slot, i):
        start = pl.multiple_of(row0_ref[e] + j * bm, bm)
        return pltpu.make_async_copy(ybuf.at[slot, pl.ds(i * bm, bm)], ys_hbm.at[pl.ds(start, bm)],
                                     sem_out.at[slot])

    def wait_pending(slot):
        count = pend_ref[slot]

        @pl.when(count == per_unit)
        def _():
            for i in range(per_unit):
                out_copy(0, slot, i).wait()

        for i in range(per_unit - 1):
            @pl.when((count < per_unit) & (i < count))
            def _():
                out_copy(0, slot, i).wait()
        pend_ref[slot] = 0

    @pl.when(e == 0)
    def _():
        pend_ref[0] = 0
        pend_ref[1] = 0

    @pl.when((e == 0) & (nblk > 0))
    def _():
        in_copy(e, 0, 0).start()

    wgu_bf[...] = wgu_ref[0].astype(BF16)
    wdn_bf[...] = wdn_ref[0].astype(BF16)

    def run_unit(s, carry):
        slot = (base + s) % 2
        in_copy(e, s, slot).wait()

        @pl.when(s + 1 < n_units)
        def _():
            in_copy(e, s + 1, 1 - slot).start()

        wait_pending(slot)
        _, lead = unit_start(e, s)

        def ffn(first, count):
            rows = count * bm
            x = xbuf[slot, pl.ds(pl.multiple_of(lead + first * bm, bm), rows), :]
            gu = jnp.dot(x, wgu_bf[...], preferred_element_type=F32) + bgu_ref[0]
            gate = jnp.minimum(gu[:, :dff], SWIGLU_LIMIT)
            up = jnp.clip(gu[:, dff:], -SWIGLU_LIMIT, SWIGLU_LIMIT)
            glu = gate * jax.nn.sigmoid(gate * SWIGLU_ALPHA)
            act = ((up + 1.0) * glu).astype(BF16)
            ybuf[slot, first * bm:first * bm + rows, :] = (
                jnp.dot(act, wdn_bf[...], preferred_element_type=F32) + bdn_ref[0]).astype(BF16)
            for i in range(first, first + count):
                out_copy(s * per_unit + i, slot, i).start()

        for first in range(0, per_unit, 2):
            have = nblk - s * per_unit - first

            @pl.when(have >= 2)
            def _():
                ffn(first, 2)

            @pl.when(have == 1)
            def _():
                ffn(first, 1)
        pend_ref[slot] = jnp.minimum(nblk - s * per_unit, per_unit)
        return carry

    lax.fori_loop(0, n_units, run_unit, 0)

    nxt = jnp.minimum(e + 1, ne - 1)

    @pl.when((e + 1 < ne) & (nblk_ref[nxt] > 0))
    def _():
        in_copy(nxt, 0, (base + n_units) % 2).start()

    @pl.when(e == ne - 1)
    def _():
        wait_pending(0)
        wait_pending(1)


def _experts(row0, nblk, ubase, limit, xs, w_gu, b_gu, w_dn, b_dn):
    n_rows, d = xs.shape
    bm = EXPERT_ROWS
    unit = EXPERT_UNIT_BLOCKS * bm
    ne, _, dff2 = w_gu.shape
    dff = w_dn.shape[1]
    exp3 = lambda e, r0, nb, ub, lim: (e, 0, 0)
    grid_spec = pltpu.PrefetchScalarGridSpec(
        num_scalar_prefetch=4,
        grid=(ne,),
        in_specs=[pl.BlockSpec(memory_space=pl.ANY),
                  pl.BlockSpec((1, d, dff2), exp3), pl.BlockSpec((1, 1, dff2), exp3),
                  pl.BlockSpec((1, dff, d), exp3), pl.BlockSpec((1, 1, d), exp3)],
        out_specs=pl.BlockSpec(memory_space=pl.ANY),
        scratch_shapes=[pltpu.VMEM((d, dff2), BF16), pltpu.VMEM((dff, d), BF16),
                        pltpu.VMEM((2, unit, d), BF16), pltpu.VMEM((2, unit, d), BF16),
                        pltpu.SemaphoreType.DMA((2,)), pltpu.SemaphoreType.DMA((2,)),
                        pltpu.SMEM((2,), jnp.int32)],
    )
    return pl.pallas_call(
        _expert_kernel,
        grid_spec=grid_spec,
        out_shape=jax.ShapeDtypeStruct((n_rows, d), BF16),
        compiler_params=_params(("arbitrary",)),
        name="experts",
    )(row0, nblk, ubase, limit, xs, w_gu, b_gu.reshape(ne, 1, dff2), w_dn, b_dn.reshape(ne, 1, d))


def _combine_kernel(tab_ref, ys_hbm, route_ref, toff_ref, x1_ref, gf_ref, o_ref, buf_ref, sem, *, n_chunks, tile0):
    step = pl.program_id(0)
    n_steps = pl.num_programs(0)
    t = tile0 + step
    slot = step % 2
    nb, tr, d = x1_ref.shape
    m = nb * tr
    rows = buf_ref.shape[1]
    g = BF16_ROWS

    def fetch(tile, which):
        for c in range(n_chunks):
            pltpu.make_async_copy(
                ys_hbm.at[pl.ds(pl.multiple_of(tab_ref[tile, c], g), g)],
                buf_ref.at[which, pl.ds(c * g, g)], sem.at[which]).start()

    @pl.when(step == 0)
    def _():
        fetch(t, slot)

    @pl.when(step + 1 < n_steps)
    def _():
        fetch(t + 1, 1 - slot)

    pos = _slot_rows(route_ref, toff_ref, m)
    gates = [route_ref[0, 2 * TOP_K + k:2 * TOP_K + k + 1, :] for k in range(TOP_K)]
    stacked = jnp.concatenate(pos + gates + [jnp.zeros((LANES - 2 * TOP_K, m), F32)], axis=0)
    cols = stacked.T
    for _ in range(n_chunks):
        pltpu.make_async_copy(ys_hbm.at[pl.ds(0, g)], buf_ref.at[slot, pl.ds(0, g)], sem.at[slot]).wait()

    acc = jnp.zeros((m, d), F32)
    for rc in range(rows // m):
        c_iota = (lax.broadcasted_iota(jnp.int32, (m, m), 1) + rc * m).astype(F32)
        weights = jnp.zeros((m, m), F32)
        for k in range(TOP_K):
            weights = jnp.where(c_iota == cols[:, k:k + 1], cols[:, TOP_K + k:TOP_K + k + 1], weights)
        acc = acc + jnp.dot(weights.astype(BF16), buf_ref[slot, rc * m:(rc + 1) * m, :],
                            preferred_element_type=F32)
    o_ref[...] = x1_ref[...] + gf_ref[...] * acc.reshape(nb, tr, d)


def _combine(table, ys, route, toff_b, x1, gf, nb, tr, tile0, n_chunks):
    nbt, t, d = x1.shape
    m = nb * tr
    assert m == TOKEN_TILE
    rows = _tile_rows(m)
    tiles_per_seq = t // tr
    n_steps = (nbt // nb) * tiles_per_seq
    xmap = lambda s, tab: (s // tiles_per_seq, s % tiles_per_seq, 0)
    grid_spec = pltpu.PrefetchScalarGridSpec(
        num_scalar_prefetch=1,
        grid=(n_steps,),
        in_specs=[pl.BlockSpec(memory_space=pl.ANY),
                  pl.BlockSpec((1, 4 * TOP_K, m), lambda s, tab: (tile0 + s, 0, 0)),
                  pl.BlockSpec((1, N_EXPERTS, LANES), lambda s, tab: (tile0 + s, 0, 0)),
                  pl.BlockSpec((nb, tr, d), xmap),
                  pl.BlockSpec((nb, 1, d), lambda s, tab: (s // tiles_per_seq, 0, MOD_GATE_FFN))],
        out_specs=pl.BlockSpec((nb, tr, d), xmap),
        scratch_shapes=[pltpu.VMEM((2, rows, d), BF16), pltpu.SemaphoreType.DMA((2,))],
    )
    return pl.pallas_call(
        functools.partial(_combine_kernel, n_chunks=n_chunks, tile0=tile0),
        grid_spec=grid_spec,
        out_shape=jax.ShapeDtypeStruct((nbt, t, d), F32),
        compiler_params=_params(("arbitrary",)),
        name="combine",
    )(table, ys, route, toff_b, x1, gf)


def _block_diag(w, groups):
    n, k, _ = w.shape
    w = w.reshape(n // groups, groups, k, k)
    eye = jnp.eye(groups, dtype=w.dtype)
    return jnp.einsum("ngij,gh->ngihj", w, eye).reshape(n // groups, groups * k, groups * k)


def _layer(xp, xs, mod_p, mod_s, k_cache, v_cache, conv_state, lru_state, lw):
    (ln_mix, ln_ffn, w_in, q_norm, k_norm, rel_bias, conv_w, conv_b, w_rg, b_rg, w_ig, b_ig, lam,
     w_out, w_router, b_router, w_gu, b_gu, w_dn, b_dn) = lw
    bp, s, d = xp.shape
    bs, ts, _ = xs.shape
    aw = w_out.shape[0] // 2
    nh = aw // HEAD_DIM
    m = TOKEN_TILE
    assert s % m == 0 and bs * ts == m and s % ATTN_Q_TILE == 0

    w_in_bf = w_in.astype(BF16)
    w_out_bf = w_out.astype(BF16)
    qn_t = jnp.tile(q_norm * (HEAD_DIM ** -0.5), nh).reshape(1, aw)
    kn_t = jnp.tile(k_norm, nh).reshape(1, aw)
    head_mean = _block_diag(jnp.full((nh, HEAD_DIM, HEAD_DIM), 1.0 / HEAD_DIM, F32), nh)[0].astype(BF16)
    groups = MXU_DIM // w_rg.shape[-1]
    wa_bd = _block_diag(w_rg, groups).astype(BF16)
    wx_bd = _block_diag(w_ig, groups).astype(BF16)
    lw_c = b_rg.size
    b_a = b_rg.reshape(1, lw_c)
    b_x = b_ig.reshape(1, lw_c)
    lam2 = lam.reshape(1, lw_c)
    cb2 = conv_b.reshape(1, lw_c)
    ln_mix2 = ln_mix.reshape(1, d)
    ln_ffn2 = ln_ffn.reshape(1, d)
    wr_t = w_router.T.astype(BF16)
    br = b_router.reshape(-1, 1)
    tab_p = _bias_table(rel_bias, 3 * ATTN_Q_TILE - 1)
    r_cache = k_cache.shape[1]
    tab_s = _bias_table(rel_bias, r_cache + ts - 1)

    zeros_pre = jnp.zeros((bp, SUBLANES, lw_c), F32)
    zeros_h = jnp.zeros((bp, 1, lw_c), F32)
    pre_s = jnp.pad(conv_state, ((0, 0), (SUBLANES - (CONV_WIDTH - 1), 0), (0, 0)))
    lru_w = (conv_w, cb2, wa_bd, wx_bd, b_a, b_x, lam2)
    qp, kp, vp, k32p, v32p, lru_p, tail_p, hl_p = _mixin(
        xp, mod_p,ln_mix2, w_in_bf, qn_t, kn_t, head_mean, zeros_pre, zeros_h, *lru_w, 1, m)
    qs, ks, vs, k32s, v32s, lru_s, tail_s, hl_s = _mixin(
        xs, mod_s,ln_mix2, w_in_bf, qn_t, kn_t, head_mean, pre_s, lru_state[:, None, :], *lru_w, bs, ts)
    attn_p = _attn_prompt(qp, kp, vp, tab_p)
    attn_s = _attn_step(qs, ks, vs, jnp.transpose(k_cache, (0, 2, 3, 1)), jnp.transpose(v_cache, (0, 2, 3, 1)),
                        tab_s, ATTN_STEP_BATCH)

    n_tiles = bp * (s // m) + 1
    x1p, h2, route, cnt = _outproj(attn_p, lru_p, xp, mod_p,ln_ffn2, w_out_bf, wr_t, br,
                                   1, m, n_tiles, 0, None)
    x1s, h2, route, cnt = _outproj(attn_s, lru_s, xs, mod_s,ln_ffn2, w_out_bf, wr_t, br,
                                   bs, ts, n_tiles, n_tiles - 1, (h2, route, cnt))

    n_chunks, _, _, n_rows = _table_sizes(n_tiles)
    assert TOP_K * m * n_tiles >= EXPERT_UNIT_BLOCKS * EXPERT_ROWS
    dispatch_tab, combine_tab, toff_b, row0, nblk, ubase, limit = _route_tables(cnt[:, :, 0].astype(jnp.int32))
    xs_sorted = _dispatch(dispatch_tab, h2, route, toff_b, n_rows, n_chunks)
    ys_sorted = _experts(row0, nblk, ubase, limit, xs_sorted, w_gu, b_gu, w_dn, b_dn)
    yp = _combine(combine_tab, ys_sorted, route, toff_b, x1p, mod_p, 1, m, 0, n_chunks)
    ysm = _combine(combine_tab, ys_sorted, route, toff_b, x1s, mod_s, bs, ts, n_tiles - 1, n_chunks)

    keep = k32p.shape[1]
    new = (k32p.reshape(bp, keep, nh, HEAD_DIM), v32p.reshape(bp, keep, nh, HEAD_DIM),
           tail_p[:, SUBLANES - (CONV_WIDTH - 1):, :], hl_p[:, 0, :],
           k32s.reshape(bs, ts, nh, HEAD_DIM), v32s.reshape(bs, ts, nh, HEAD_DIM),
           tail_s[:, SUBLANES - (CONV_WIDTH - 1):, :], hl_s[:, 0, :])
    return yp, ysm, new


def kernel(x_prompt, x_sample, c_prompt, c_sample, cache_k, cache_v, state_conv, state_lru, ln_mix_w, ln_ffn_w, w_ada, b_ada, w_in, q_norm_w, k_norm_w, rel_bias, conv_w, conv_b, w_rgate, b_rgate, w_igate, b_igate, lru_lambda, w_out, w_router, b_router, w_gate_up, b_gate_up, w_down, b_down):
    depth = w_in.shape[0]
    yp, ys = x_prompt, x_sample
    collected = [[] for _ in range(8)]
    for l in range(depth):
        mod_p, mod_s = _ada(c_prompt, c_sample, w_ada[l], b_ada[l])
        lw = (ln_mix_w[l], ln_ffn_w[l], w_in[l], q_norm_w[l], k_norm_w[l], rel_bias[l], conv_w[l], conv_b[l],
              w_rgate[l], b_rgate[l], w_igate[l], b_igate[l], lru_lambda[l], w_out[l], w_router[l], b_router[l],
              w_gate_up[l], b_gate_up[l], w_down[l], b_down[l])
        yp, ys, new = _layer(yp, ys, mod_p, mod_s, cache_k[l], cache_v[l], state_conv[l], state_lru[l], lw)
        for acc, val in zip(collected, new):
            acc.append(val)
    return (yp, ys) + tuple(jnp.stack(vals) for vals in collected)
```

```python
import functools

import jax
import jax.numpy as jnp
from jax import lax
from jax.experimental import pallas as pl
from jax.experimental.pallas import tpu as pltpu

F32 = jnp.float32
BF16 = jnp.bfloat16

CHUNK = 64
N_LEFT_CHUNKS = 8
ATTN_WINDOW = N_LEFT_CHUNKS * CHUNK
HEAD_DIM = 64
REL_CLIP = 128
CONV_WIDTH = 4
LRU_C = 8.0
N_EXPERTS = 32
TOP_K = 4
SWIGLU_LIMIT = 7.0
SWIGLU_ALPHA = 1.702
NORM_EPS = 1e-6
NEG_INF = -1e30

LANES = 128
SUBLANES = 8
BF16_ROWS = 16
MXU_DIM = 256

TOKEN_TILE = 512
ATTN_Q_TILE = 256
ATTN_STEP_BATCH = 4
EXPERT_ROWS = 256
EXPERT_UNIT_BLOCKS = 4
BIAS_TABLE = 1024
VMEM_LIMIT = 56 * 1024 * 1024


def _params(sem, vmem=VMEM_LIMIT):
    return pltpu.CompilerParams(dimension_semantics=sem, vmem_limit_bytes=vmem)


def _ada_kernel(cp_ref, cs_ref, w_ref, b_ref, op_ref, os_ref):
    w = w_ref[...].astype(BF16)
    for c_ref, o_ref in ((cp_ref, op_ref), (cs_ref, os_ref)):
        c = c_ref[...]
        s = (c * jax.nn.sigmoid(c)).astype(BF16)
        o_ref[...] = jnp.dot(s, w, preferred_element_type=F32) + b_ref[...]


def _ada(c_p, c_s, w_ada, b_ada):
    (n_p, d), n_s = c_p.shape, c_s.shape[0]
    nout = w_ada.shape[1]
    tn = 1024
    mod_p, mod_s = pl.pallas_call(
        _ada_kernel,
        grid=(nout // tn,),
        in_specs=[pl.BlockSpec((n_p, d), lambda j: (0, 0)), pl.BlockSpec((n_s, d), lambda j: (0, 0)),
                  pl.BlockSpec((d, tn), lambda j: (0, j)),
                  pl.BlockSpec((1, tn), lambda j: (0, j))],
        out_specs=[pl.BlockSpec((n_p, tn), lambda j: (0, j)), pl.BlockSpec((n_s, tn), lambda j: (0, j))],
        out_shape=[jax.ShapeDtypeStruct((n_p, nout), F32), jax.ShapeDtypeStruct((n_s, nout), F32)],
        compiler_params=_params(("arbitrary",)),
        name="ada",
    )(c_p, c_s, w_ada, b_ada.reshape(1, nout))
    return mod_p.reshape(n_p, 1, nout), mod_s.reshape(n_s, 1, nout)


MOD_SHIFT_MIX, MOD_SCALE_MIX, MOD_GATE_MIX, MOD_SHIFT_FFN, MOD_SCALE_FFN, MOD_GATE_FFN = range(6)


def _mod_spec(nb, d, term):
    return pl.BlockSpec((nb, 1, d), lambda b, i: (b, 0, term))


def _mixin_kernel(x_ref, sh_ref, sc_ref, ln_ref, win_ref, qn_ref, kn_ref, bd_ref,
                  pre_ref, h0_ref, cw_ref, cb_ref, wa_ref, wx_ref, ba_ref, bx_ref, lam_ref,
                  q_ref, k_ref, v_ref, k32_ref, v32_ref, lru_ref, tail_ref, hl_ref, cx_ref, ch_ref):
    nb, tr, d = x_ref.shape
    m = nb * tr
    aw = q_ref.shape[-1]

    @pl.when(pl.program_id(1) == 0)
    def _():
        cx_ref[...] = pre_ref[...]
        ch_ref[...] = h0_ref[...]

    x = x_ref[...]
    ms = jnp.mean(x * x, axis=-1, keepdims=True)
    h = x * lax.rsqrt(ms + NORM_EPS) * (ln_ref[...] * (1.0 + sc_ref[...])) + sh_ref[...]
    hb = h.reshape(m, d).astype(BF16)

    def proj(part):
        return jnp.dot(hb, win_ref[:, part * aw:(part + 1) * aw], preferred_element_type=F32)

    def head_norm(t, w_ref):
        msq = jnp.dot((t * t).astype(BF16), bd_ref[...], preferred_element_type=F32)
        return t * lax.rsqrt(msq + NORM_EPS) * w_ref[...]

    lru_out, new_tail, h_last = _lru_branch(
        proj(3).reshape(nb, tr, aw), proj(4).reshape(nb, tr, aw), cw_ref, cb_ref,
        wa_ref, wx_ref, ba_ref, bx_ref, lam_ref, cx_ref, ch_ref)
    lru_ref[...] = lru_out
    tail_ref[...] = new_tail
    hl_ref[...] = h_last
    q = head_norm(proj(0), qn_ref)
    k = head_norm(proj(1), kn_ref)
    v = proj(2)
    q_ref[...] = q.astype(BF16).reshape(nb, tr, aw)
    k_ref[...] = k.astype(BF16).reshape(nb, tr, aw)
    v_ref[...] = v.astype(BF16).reshape(nb, tr, aw)
    k32_ref[...] = k.reshape(nb, tr, aw)
    v32_ref[...] = v.reshape(nb, tr, aw)


def _mixin(x, mod, ln_w, w_in_bf, qn_t, kn_t, bd, pre, h0, conv_w, conv_b, wa_bd, wx_bd, b_a, b_x, lam,
           nb, tr):
    nbt, t, d = x.shape
    aw = qn_t.shape[-1]
    c = pre.shape[-1]
    assert c == aw
    keep = min(ATTN_WINDOW, t)
    assert tr == keep or t == tr
    grid = (nbt // nb, t // tr)
    xmap = lambda b, i: (b, i, 0)
    mmap = lambda b, i: (b, 0, 0)
    cmap = lambda b, i: (0, 0)
    cmap3 = lambda b, i: (0, 0, 0)
    tmap = lambda b, i: (b, 0, 0)
    big = pl.BlockSpec((nb, tr, aw), xmap)
    tail = pl.BlockSpec((nb, keep, aw), tmap)
    row = pl.BlockSpec((1, c), cmap)
    return pl.pallas_call(
        _mixin_kernel,
        grid=grid,
        in_specs=[pl.BlockSpec((nb, tr, d), xmap),
                  _mod_spec(nb, d, MOD_SHIFT_MIX), _mod_spec(nb, d, MOD_SCALE_MIX),
                  pl.BlockSpec((1, d), cmap),
                  pl.BlockSpec(w_in_bf.shape, cmap),
                  pl.BlockSpec((1, aw), cmap), pl.BlockSpec((1, aw), cmap),
                  pl.BlockSpec(bd.shape, cmap),
                  pl.BlockSpec((nb, SUBLANES, c), tmap), pl.BlockSpec((nb, 1, c), tmap),
                  pl.BlockSpec(conv_w.shape, cmap), row,
                  pl.BlockSpec(wa_bd.shape, cmap3), pl.BlockSpec(wx_bd.shape, cmap3),
                  row, row, row],
        out_specs=[big, big, big, tail, tail, big,
                   pl.BlockSpec((nb, SUBLANES, c), tmap), pl.BlockSpec((nb, 1, c), tmap)],
        out_shape=[jax.ShapeDtypeStruct((nbt, t, aw), BF16)] * 3
        + [jax.ShapeDtypeStruct((nbt, keep, aw), F32)] * 2
        + [jax.ShapeDtypeStruct((nbt, t, c), BF16),
           jax.ShapeDtypeStruct((nbt, SUBLANES, c), F32),
           jax.ShapeDtypeStruct((nbt, 1, c), F32)],
        scratch_shapes=[pltpu.VMEM((nb, SUBLANES, c), F32), pltpu.VMEM((nb, 1, c), F32)],
        compiler_params=_params(("arbitrary", "arbitrary")),
        name="mixin",
    )(x, mod, mod, ln_w, w_in_bf, qn_t, kn_t, bd, pre, h0, conv_w, conv_b, wa_bd, wx_bd, b_a, b_x, lam)


def _bias_table(rel_bias, off):
    h = rel_bias.shape[0]
    left = off - REL_CLIP
    right = BIAS_TABLE - left - (2 * REL_CLIP + 1)
    assert left >= 0 and right >= 0
    return jnp.concatenate([jnp.broadcast_to(rel_bias[:, :1], (h, left)), rel_bias,
                            jnp.broadcast_to(rel_bias[:, -1:], (h, right))], axis=1)


def _toeplitz(tab_row, rows, cols):
    t = jnp.broadcast_to(tab_row, (rows, BIAS_TABLE))
    t = pltpu.roll(t, BIAS_TABLE - (rows - 1), 1, stride=1, stride_axis=0)
    return t[:, :cols]


def _attn_kernel(q_ref, k_ref, v_ref, tab_ref, o_ref, bias_ref, kring, vring):
    b = pl.program_id(0)
    s = pl.program_id(1)
    qt = q_ref.shape[1]
    nk = 3 * qt
    nh = bias_ref.shape[0]

    @pl.when((b == 0) & (s == 0))
    def _():
        qi = lax.broadcasted_iota(jnp.int32, (qt, nk), 0) // CHUNK
        kc = lax.broadcasted_iota(jnp.int32, (qt, nk), 1) // CHUNK
        for h in range(nh):
            band = jnp.where(kc <= qi + N_LEFT_CHUNKS, _toeplitz(tab_ref[h:h + 1, :], qt, nk), NEG_INF)
            bias_ref[h] = jnp.where(kc >= qi, band, NEG_INF)
        kring[...] = jnp.zeros(kring.shape, BF16)
        vring[...] = jnp.zeros(vring.shape, BF16)

    for ring, new_ref in ((kring, k_ref), (vring, v_ref)):
        ring[0:qt, :] = ring[qt:2 * qt, :]
        ring[qt:2 * qt, :] = ring[2 * qt:3 * qt, :]
        ring[2 * qt:3 * qt, :] = new_ref[0]

    pair_w = 2 * HEAD_DIM

    def attend(mask_start):
        q = q_ref[0]
        kcat = kring[...]
        vcat = vring[...]
        first = lax.broadcasted_iota(jnp.int32, (qt, pair_w), 1) < HEAD_DIM
        keep = [jnp.where(first, 1.0, 0.0).astype(BF16), jnp.where(first, 0.0, 1.0).astype(BF16)]
        if mask_start:
            in_seq = lax.broadcasted_iota(jnp.int32, (qt, nk), 1) >= (2 - s) * qt
        outs = []
        for pair in range(nh // 2):
            sl = slice(pair * pair_w, (pair + 1) * pair_w)
            q2, k2, v2 = q[:, sl], kcat[:, sl], vcat[:, sl]
            per_head = []
            for sub in range(2):
                sc = lax.dot_general(q2 * keep[sub], k2, (((1,), (1,)), ((), ())), preferred_element_type=F32)
                sc = sc + bias_ref[2 * pair + sub]
                if mask_start:
                    sc = jnp.where(in_seq, sc, NEG_INF)
                mx = jnp.max(sc, axis=-1, keepdims=True)
                p = jnp.exp(sc - mx)
                l = jnp.sum(p, axis=-1, keepdims=True)
                per_head.append(jnp.dot(p.astype(BF16), v2, preferred_element_type=F32) / l)
            outs.append(jnp.where(first, per_head[0], per_head[1]))
        o_ref[0] = jnp.concatenate(outs, axis=-1).astype(BF16)

    @pl.when(s < 2)
    def _():
        attend(True)

    @pl.when(s >= 2)
    def _():
        attend(False)


def _attn_prompt(q, k, v, tab):
    b, s, aw = q.shape
    qt = ATTN_Q_TILE
    nh = aw // HEAD_DIM
    qspec = pl.BlockSpec((1, qt, aw), lambda i, j: (i, j, 0))

    return pl.pallas_call(
        _attn_kernel,
        grid=(b, s // qt),
        in_specs=[qspec, qspec, qspec,
                  pl.BlockSpec(tab.shape, lambda i, j: (0, 0))],
        out_specs=qspec,
        out_shape=jax.ShapeDtypeStruct((b, s, aw), BF16),
        scratch_shapes=[pltpu.VMEM((nh, qt, 3 * qt), F32),
                        pltpu.VMEM((3 * qt, aw), BF16), pltpu.VMEM((3 * qt, aw), BF16)],
        compiler_params=_params(("arbitrary", "arbitrary")),
        name="attn_prompt",
    )(q, k, v, tab)


def _attn_step_kernel(q_ref, kn_ref, vn_ref, ck_ref, cv_ref, tab_ref, o_ref, bias_ref):
    step = pl.program_id(0)
    nbs, t, aw = q_ref.shape
    nh = aw // HEAD_DIM
    rows = nh * t
    r = ck_ref.shape[-1]
    nk = r + LANES
    nt_dims = (((1,), (1,)), ((), ()))

    @pl.when(step == 0)
    def _():
        ok = lax.broadcasted_iota(jnp.int32, (t, nk), 1) < r + t
        for h in range(nh):
            bias_ref[h * t:(h + 1) * t, :] = jnp.where(ok, _toeplitz(tab_ref[h:h + 1, :], t, nk), NEG_INF)

    own = (lax.broadcasted_iota(jnp.int32, (rows, aw), 0) // t
           == lax.broadcasted_iota(jnp.int32, (rows, aw), 1) // HEAD_DIM)
    own_f = jnp.where(own, 1.0, 0.0)
    own_bf = own_f.astype(BF16)
    pad = jnp.zeros((LANES - t, aw), BF16)
    for b in range(nbs):
        q_bd = jnp.concatenate([q_ref[b]] * nh, axis=0) * own_bf
        k_old = ck_ref[b].reshape(aw, r).astype(BF16)
        v_old = cv_ref[b].reshape(aw, r).astype(BF16)
        k_new = jnp.concatenate([kn_ref[b], pad], axis=0)
        v_new = jnp.concatenate([vn_ref[b], pad], axis=0)
        s_old = jnp.dot(q_bd, k_old, preferred_element_type=F32) + bias_ref[:, :r]
        s_new = lax.dot_general(q_bd, k_new, nt_dims, preferred_element_type=F32) + bias_ref[:, r:]
        mx = jnp.maximum(jnp.max(s_old, axis=-1, keepdims=True), jnp.max(s_new, axis=-1, keepdims=True))
        p_old = jnp.exp(s_old - mx)
        p_new = jnp.exp(s_new - mx)
        l = jnp.sum(p_old, axis=-1, keepdims=True) + jnp.sum(p_new, axis=-1, keepdims=True)
        o_all = (lax.dot_general(p_old.astype(BF16), v_old, nt_dims, preferred_element_type=F32)
                 + jnp.dot(p_new.astype(BF16), v_new, preferred_element_type=F32))
        o_all = o_all * own_f / l
        out = o_all[0:t]
        for h in range(1, nh):
            out = out + o_all[h * t:(h + 1) * t]
        o_ref[b] = out.astype(BF16)


def _attn_step(q, kn, vn, ck, cv, tab, nbs):
    b, t, aw = q.shape
    nh = aw // HEAD_DIM
    r = ck.shape[-1]
    new = pl.BlockSpec((nbs, t, aw), lambda i: (i, 0, 0))
    old = pl.BlockSpec((nbs, nh, HEAD_DIM, r), lambda i: (i, 0, 0, 0))
    return pl.pallas_call(
        _attn_step_kernel,
        grid=(b // nbs,),
        in_specs=[new, new, new, old, old, pl.BlockSpec(tab.shape, lambda i: (0, 0))],
        out_specs=new,
        out_shape=jax.ShapeDtypeStruct((b, t, aw), BF16),
        scratch_shapes=[pltpu.VMEM((nh * t, r + LANES), F32)],
        compiler_params=_params(("arbitrary",)),
        name="attn_step",
    )(q, kn, vn, ck, cv, tab)


def _gelu_tanh(x):
    return x * (0.5 * (1.0 + jnp.tanh(0.7978845608028654 * (x + 0.044715 * (x * x * x)))))


def _lru_branch(x, yg, cw_ref, cb_ref, wa_ref, wx_ref, ba_ref, bx_ref, lam_ref, cx_ref, ch_ref):
    nb, tr, c = x.shape
    m = nb * tr
    half = c // 2
    xp = jnp.concatenate([cx_ref[...], x], axis=1)
    new_tail = xp[:, tr:tr + SUBLANES, :]
    groups = tr // SUBLANES
    xg = xp.reshape(nb * (groups + 1), SUBLANES, c)
    first_rows = lax.broadcasted_iota(jnp.int32, (nb, groups, SUBLANES, c), 2)
    y = cb_ref[...] + cw_ref[CONV_WIDTH - 1:CONV_WIDTH, :] * x
    for back in range(1, CONV_WIDTH):
        rot = pltpu.roll(xg, back, 1).reshape(nb, groups + 1, SUBLANES, c)
        shifted = jnp.where(first_rows >= back, rot[:, 1:], rot[:, :groups]).reshape(nb, tr, c)
        y = y + cw_ref[CONV_WIDTH - 1 - back:CONV_WIDTH - back, :] * shifted
    y2 = y.reshape(m, c)
    yb = y2.astype(BF16)

    def gate(w_ref, b_ref):
        g = jnp.concatenate(
            [jnp.dot(yb[:, :half], w_ref[0], preferred_element_type=F32),
             jnp.dot(yb[:, half:], w_ref[1], preferred_element_type=F32)], axis=1)
        return jax.nn.sigmoid(g + b_ref[...])

    rg = gate(wa_ref, ba_ref)
    ig = gate(wx_ref, bx_ref)
    lam = lam_ref[...]
    log_sig = jnp.minimum(lam, 0.0) - jnp.log1p(jnp.exp(-jnp.abs(lam)))
    log_a = rg * (LRU_C * log_sig)
    a_cum = jnp.exp(log_a)
    b_cum = jnp.sqrt(-jnp.tanh(log_a) * (a_cum * a_cum + 1.0)) * (ig * y2)
    a_cum = a_cum.reshape(nb * groups, SUBLANES, c)
    b_cum = b_cum.reshape(nb * groups, SUBLANES, c)
    row = lax.broadcasted_iota(jnp.int32, a_cum.shape, 1)
    dist = 1
    while dist < SUBLANES:
        keep = row >= dist
        a_sh = jnp.where(keep, pltpu.roll(a_cum, dist, 1), 1.0)
        b_sh = jnp.where(keep, pltpu.roll(b_cum, dist, 1), 0.0)
        b_cum = a_cum * b_sh + b_cum
        a_cum = a_cum * a_sh
        dist *= 2
    a_grp = a_cum.reshape(nb, groups, SUBLANES, c)
    b_grp = b_cum.reshape(nb, groups, SUBLANES, c)
    carry = ch_ref[...]
    pieces = []
    for grp in range(groups):
        h_grp = a_grp[:, grp] * carry + b_grp[:, grp]
        carry = h_grp[:, SUBLANES - 1:SUBLANES, :]
        pieces.append(h_grp)
    h = jnp.concatenate(pieces, axis=1)
    ch_ref[...] = carry
    cx_ref[...] = new_tail
    return (h * _gelu_tanh(yg)).astype(BF16), new_tail, carry


def _outproj_kernel(*refs, aliased):
    (at_ref, lr_ref, x_ref, gm_ref, shf_ref, scf_ref, lnf_ref, wo_ref, wr_ref, br_ref) = refs[:10]
    x1_ref, h2_ref, route_ref, cnt_ref = refs[10 + aliased:]
    nb, tr, d = x_ref.shape
    m = nb * tr
    aw = at_ref.shape[-1]
    ne = wr_ref.shape[0]
    at = at_ref[...].reshape(m, aw)
    lr = lr_ref[...].reshape(m, aw)
    mix = (jnp.dot(at, wo_ref[0:aw, :], preferred_element_type=F32)
           + jnp.dot(lr, wo_ref[aw:2 * aw, :], preferred_element_type=F32))
    x1 = x_ref[...] + gm_ref[...] * mix.reshape(nb, tr, d)
    x1_ref[...] = x1
    ms = jnp.mean(x1 * x1, axis=-1, keepdims=True)
    h2 = (x1 * lax.rsqrt(ms + NORM_EPS) * (lnf_ref[...] * (1.0 + scf_ref[...])) + shf_ref[...]).reshape(m, d)
    h2_ref[...] = h2.astype(BF16)

    logits = lax.dot_general(wr_ref[...], h2.astype(BF16), (((1,), (1,)), ((), ())),
                             preferred_element_type=F32) + br_ref[...]
    e_iota = lax.broadcasted_iota(jnp.int32, (ne, m), 0).astype(F32)
    vals = logits
    top_v, sels = [], []
    for k in range(TOP_K):
        mx = jnp.max(vals, axis=0, keepdims=True)
        idx = jnp.min(jnp.where(vals == mx, e_iota, float(ne)), axis=0, keepdims=True)
        sel = e_iota == idx
        vals = jnp.where(sel, -jnp.inf, vals)
        top_v.append(mx)
        sels.append(sel)
        route_ref[0, k:k + 1, :] = idx
    ex = [jnp.exp(v - top_v[0]) for v in top_v]
    den = ex[0] + ex[1] + ex[2] + ex[3]
    chosen = jnp.zeros((ne, m), F32)
    for k in range(TOP_K):
        route_ref[0, 2 * TOP_K + k:2 * TOP_K + k + 1, :] = ex[k] / den
        chosen = chosen + jnp.where(sels[k], 1.0, 0.0)
    before = (lax.broadcasted_iota(jnp.int32, (m, m), 0) < lax.broadcasted_iota(jnp.int32, (m, m), 1))
    rank = jnp.dot(chosen.astype(BF16), jnp.where(before, 1.0, 0.0).astype(BF16), preferred_element_type=F32)
    for k in range(TOP_K):
        route_ref[0, TOP_K + k:TOP_K + k + 1, :] = jnp.sum(jnp.where(sels[k], rank, 0.0), axis=0, keepdims=True)
    route_ref[0, 3 * TOP_K:4 * TOP_K, :] = jnp.zeros((TOP_K, m), F32)
    cnt_ref[0] = jnp.broadcast_to(jnp.sum(chosen, axis=1, keepdims=True), (ne, LANES))


def _outproj(attn, lru_o, x, mod, lnf, w_out_bf, wr_t, br, nb, tr, n_tiles, tile0, prev):
    nbt, t, d = x.shape
    aw = attn.shape[-1]
    m = nb * tr
    assert m == TOKEN_TILE
    ne = wr_t.shape[0]
    tiles_per_seq = t // tr
    xmap = lambda b, i: (b, i, 0)
    c2 = lambda b, i: (0, 0)
    tile = lambda b, i: (tile0 + b * tiles_per_seq + i, 0)
    tile3 = lambda b, i: (tile0 + b * tiles_per_seq + i, 0, 0)
    in_specs = [pl.BlockSpec((nb, tr, aw), xmap), pl.BlockSpec((nb, tr, aw), xmap),
                pl.BlockSpec((nb, tr, d), xmap),
                _mod_spec(nb, d, MOD_GATE_MIX), _mod_spec(nb, d, MOD_SHIFT_FFN), _mod_spec(nb, d, MOD_SCALE_FFN),
                pl.BlockSpec((1, d), c2), pl.BlockSpec(w_out_bf.shape, c2),
                pl.BlockSpec(wr_t.shape, c2), pl.BlockSpec((ne, 1), c2)]
    args = [attn, lru_o, x, mod, mod, mod, lnf, w_out_bf, wr_t, br]
    aliases = {}
    if prev is not None:
        in_specs += [pl.BlockSpec(memory_space=pl.ANY)] * 3
        args += list(prev)
        aliases = {10: 1, 11: 2, 12: 3}
    return pl.pallas_call(
        functools.partial(_outproj_kernel, aliased=len(aliases)),
        grid=(nbt // nb, tiles_per_seq),
        in_specs=in_specs,
        out_specs=[pl.BlockSpec((nb, tr, d), xmap), pl.BlockSpec((m, d), tile),
                   pl.BlockSpec((1, 4 * TOP_K, m), tile3), pl.BlockSpec((1, ne, LANES), tile3)],
        out_shape=[jax.ShapeDtypeStruct((nbt, t, d), F32),
                   jax.ShapeDtypeStruct((n_tiles * m, d), BF16),
                   jax.ShapeDtypeStruct((n_tiles, 4 * TOP_K, m), F32),
                   jax.ShapeDtypeStruct((n_tiles, ne, LANES), F32)],
        input_output_aliases=aliases,
        compiler_params=_params(("arbitrary", "arbitrary")),
        name="outproj",
    )(*args)


def _tile_rows(m):
    cap = TOP_K * m + N_EXPERTS * (BF16_ROWS - 1) + BF16_ROWS
    return -(-cap // TOKEN_TILE) * TOKEN_TILE


def _table_sizes(nt):
    g = BF16_ROWS
    m = TOKEN_TILE
    n_chunks = _tile_rows(m) // g
    n_gap = -(-(N_EXPERTS * (EXPERT_ROWS // g - 1)) // nt)
    bound = TOP_K * m * nt + nt * N_EXPERTS * (g - 1) + N_EXPERTS * (EXPERT_ROWS - g)
    n_sorted = -(-bound // EXPERT_ROWS) * EXPERT_ROWS
    return n_chunks, n_gap, n_sorted, n_sorted + 2 * (n_chunks + n_gap) * g


def _route_tables(cnt):
    nt = cnt.shape[0]
    g = BF16_ROWS
    bm = EXPERT_ROWS
    n_chunks, n_gap, n_sorted, _ = _table_sizes(nt)
    e_ids = jnp.arange(N_EXPERTS, dtype=jnp.int32)
    t_ids = jnp.arange(nt, dtype=jnp.int32)
    upto = (e_ids[:, None] <= e_ids[None, :]).astype(jnp.int32)
    pc = (cnt + g - 1) // g * g
    ctile = jnp.sum(pc[:, :, None] * upto[None], axis=1)
    toff = ctile - pc
    trow = ctile[:, -1]
    tot = jnp.sum(pc, axis=0)
    reg = (tot + bm - 1) // bm * bm
    creg = jnp.sum(reg[:, None] * upto, axis=0)
    base = creg - reg
    earlier = (t_ids[:, None] < t_ids[None, :]).astype(jnp.int32)
    goff = base[None, :] + jnp.sum(pc[:, None, :] * earlier[:, :, None], axis=0)
    r = jnp.arange(n_chunks, dtype=jnp.int32) * g
    r3 = r[None, :, None]
    in_seg = (toff[:, None, :] <= r3) & (r3 < ctile[:, None, :])
    dst = jnp.sum(jnp.where(in_seg, (goff - toff)[:, None, :], 0), axis=2) + r[None, :]
    dst = jnp.where(r[None, :] < trow[:, None], dst, -1)
    gcnt = (reg - tot) // g
    gcum = jnp.sum(gcnt[:, None] * upto, axis=0)
    gstart = gcum - gcnt
    s = jnp.arange(nt * n_gap, dtype=jnp.int32)
    in_gap = (gstart[None, :] <= s[:, None]) & (s[:, None] < gcum[None, :])
    gdst = jnp.sum(jnp.where(in_gap, (base + tot - g * gstart)[None, :] + g * s[:, None], 0), axis=1)
    gdst = jnp.where(s < gcum[-1], gdst, -1).reshape(nt, n_gap)
    table = jnp.concatenate([dst, gdst], axis=1).astype(jnp.int32)
    n_entries = n_chunks + n_gap
    spare = n_sorted + ((t_ids % 2)[:, None] * n_entries + jnp.arange(n_entries, dtype=jnp.int32)[None, :]) * g
    dispatch_tab = jnp.where(table >= 0, table, spare).astype(jnp.int32)
    combine_tab = jnp.maximum(dst, 0).astype(jnp.int32)
    toff_b = jnp.broadcast_to(toff.astype(F32)[:, :, None], (nt, N_EXPERTS, LANES))
    limit = (creg[-1] - EXPERT_UNIT_BLOCKS * bm).astype(jnp.int32).reshape(1)
    nblk = reg // bm
    units = (nblk + EXPERT_UNIT_BLOCKS - 1) // EXPERT_UNIT_BLOCKS
    ubase = jnp.sum(units[:, None] * upto, axis=0) - units
    return (dispatch_tab, combine_tab, toff_b, base.astype(jnp.int32), nblk.astype(jnp.int32),
            ubase.astype(jnp.int32), limit)


def _slot_rows(route_ref, toff_ref, m):
    ne = toff_ref.shape[1]
    e_iota = lax.broadcasted_iota(jnp.int32, (ne, m), 0).astype(F32)
    toff_col = toff_ref[0][:, 0:1]
    pos = []
    for k in range(TOP_K):
        sel = e_iota == route_ref[0, k:k + 1, :]
        start = jnp.sum(jnp.where(sel, toff_col, 0.0), axis=0, keepdims=True)
        pos.append(start + route_ref[0, TOP_K + k:TOP_K + k + 1, :])
    return pos


def _dispatch_kernel(tab_ref, h2_ref, route_ref, toff_ref, xs_hbm, buf_ref, sem, *, n_chunks, n_tiles):
    t = pl.program_id(0)
    slot = t % 2
    m = h2_ref.shape[0]
    rows = buf_ref.shape[1]
    n_entries = tab_ref.shape[1]
    g = BF16_ROWS
    per_chunk = m // g

    def start(c):
        src = c * g if c < n_chunks else rows - g
        pltpu.make_async_copy(
            buf_ref.at[slot, pl.ds(src, g)],
            xs_hbm.at[pl.ds(pl.multiple_of(tab_ref[t, c], g), g)], sem.at[slot]).start()

    def wait_all(which):
        for _ in range(n_entries):
            pltpu.make_async_copy(buf_ref.at[which, pl.ds(0, g)], xs_hbm.at[pl.ds(0, g)], sem.at[which]).wait()

    @pl.when(t >= 2)
    def _():
        wait_all(slot)

    pos = _slot_rows(route_ref, toff_ref, m)
    h2 = h2_ref[...]
    for rc in range(rows // m):
        r_iota = (lax.broadcasted_iota(jnp.int32, (m, m), 0) + rc * m).astype(F32)
        onehot = jnp.zeros((m, m), F32)
        for k in range(TOP_K):
            onehot = jnp.where(r_iota == pos[k], 1.0, onehot)
        buf_ref[slot, rc * m:(rc + 1) * m, :] = jnp.dot(
            onehot.astype(BF16), h2, preferred_element_type=F32).astype(BF16)
        for c in range((rc - 1) * per_chunk, rc * per_chunk) if rc > 0 else ():
            start(c)
    for c in range(n_chunks - per_chunk, n_entries):
        start(c)

    @pl.when(t == n_tiles - 1)
    def _():
        if n_tiles > 1:
            wait_all(1 - slot)
        wait_all(slot)


def _dispatch(table, h2, route, toff_b, n_rows, n_chunks):
    nt = route.shape[0]
    m = TOKEN_TILE
    d = h2.shape[1]
    rows = _tile_rows(m)
    grid_spec = pltpu.PrefetchScalarGridSpec(
        num_scalar_prefetch=1,
        grid=(nt,),
        in_specs=[pl.BlockSpec((m, d), lambda t, tab: (t, 0)),
                  pl.BlockSpec((1, 4 * TOP_K, m), lambda t, tab: (t, 0, 0)),
                  pl.BlockSpec((1, N_EXPERTS, LANES), lambda t, tab: (t, 0, 0))],
        out_specs=pl.BlockSpec(memory_space=pl.ANY),
        scratch_shapes=[pltpu.VMEM((2, rows, d), BF16), pltpu.SemaphoreType.DMA((2,))],
    )
    return pl.pallas_call(
        functools.partial(_dispatch_kernel, n_chunks=n_chunks, n_tiles=nt),
        grid_spec=grid_spec,
        out_shape=jax.ShapeDtypeStruct((n_rows, d), BF16),
        compiler_params=_params(("arbitrary",)),
        name="dispatch",
    )(table, h2, route, toff_b)


def _expert_kernel(row0_ref, nblk_ref, ubase_ref, limit_ref, xs_hbm, wgu_ref, bgu_ref, wdn_ref, bdn_ref, ys_hbm,
                   wgu_bf, wdn_bf, xbuf, ybuf, sem_in, sem_out, pend_ref):
    e = pl.program_id(0)
    ne = pl.num_programs(0)
    bm = EXPERT_ROWS
    unit = xbuf.shape[1]
    per_unit = unit // bm
    dff = wdn_ref.shape[1]
    nblk = nblk_ref[e]
    n_units = (nblk + per_unit - 1) // per_unit
    base = ubase_ref[e]

    def unit_start(expert, s):
        true = row0_ref[expert] + s * unit
        start = jnp.minimum(true, limit_ref[0])
        return pl.multiple_of(start, bm), pl.multiple_of(true - start, bm)

    def in_copy(expert, s, slot):
        start, _ = unit_start(expert, s)
        return pltpu.make_async_copy(xs_hbm.at[pl.ds(start, unit)], xbuf.at[slot], sem_in.at[slot])

    def out_copy(j, slot, i):
        start = pl.multiple_of(row0_ref[e] + j * bm, bm)
        return pltpu.make_async_copy(ybuf.at[slot, pl.ds(i * bm, bm)], ys_hbm.at[pl.ds(start, bm)],
                                     sem_out.at[slot])

    def wait_pending(slot):
        count = pend_ref[slot]

        @pl.when(count == per_unit)
        def _():
            for i in range(per_unit):
                out_copy(0, slot, i).wait()

        for i in range(per_unit - 1):
            @pl.when((count < per_unit) & (i < count))
            def _():
                out_copy(0, slot, i).wait()
        pend_ref[slot] = 0

    @pl.when(e == 0)
    def _():
        pend_ref[0] = 0
        pend_ref[1] = 0

    @pl.when((e == 0) & (nblk > 0))
    def _():
        in_copy(e, 0, 0).start()

    wgu_bf[...] = wgu_ref[0].astype(BF16)
    wdn_bf[...] = wdn_ref[0].astype(BF16)

    def run_unit(s, carry):
        slot = (base + s) % 2
        in_copy(e, s, slot).wait()

        @pl.when(s + 1 < n_units)
        def _():
            in_copy(e, s + 1, 1 - slot).start()

        wait_pending(slot)
        _, lead = unit_start(e, s)

        def ffn(first, count):
            rows = count * bm
            x = xbuf[slot, pl.ds(pl.multiple_of(lead + first * bm, bm), rows), :]
            gu = jnp.dot(x, wgu_bf[...], preferred_element_type=F32) + bgu_ref[0]
            gate = jnp.minimum(gu[:, :dff], SWIGLU_LIMIT)
            up = jnp.clip(gu[:, dff:], -SWIGLU_LIMIT, SWIGLU_LIMIT)
            glu = gate * jax.nn.sigmoid(gate * SWIGLU_ALPHA)
            act = ((up + 1.0) * glu).astype(BF16)
            ybuf[slot, first * bm:first * bm + rows, :] = (
                jnp.dot(act, wdn_bf[...], preferred_element_type=F32) + bdn_ref[0]).astype(BF16)
            for i in range(first, first + count):
                out_copy(s * per_unit + i, slot, i).start()

        for first in range(0, per_unit, 2):
            have = nblk - s * per_unit - first

            @pl.when(have >= 2)
            def _():
                ffn(first, 2)

            @pl.when(have == 1)
            def _():
                ffn(first, 1)
        pend_ref[slot] = jnp.minimum(nblk - s * per_unit, per_unit)
        return carry

    lax.fori_loop(0, n_units, run_unit, 0)

    nxt = jnp.minimum(e + 1, ne - 1)

    @pl.when((e + 1 < ne) & (nblk_ref[nxt] > 0))
    def _():
        in_copy(nxt, 0, (base + n_units) % 2).start()

    @pl.when(e == ne - 1)
    def _():
        wait_pending(0)
        wait_pending(1)


def _experts(row0, nblk, ubase, limit, xs, w_gu, b_gu, w_dn, b_dn):
    n_rows, d = xs.shape
    bm = EXPERT_ROWS
    unit = EXPERT_UNIT_BLOCKS * bm
    ne, _, dff2 = w_gu.shape
    dff = w_dn.shape[1]
    exp3 = lambda e, r0, nb, ub, lim: (e, 0, 0)
    grid_spec = pltpu.PrefetchScalarGridSpec(
        num_scalar_prefetch=4,
        grid=(ne,),
        in_specs=[pl.BlockSpec(memory_space=pl.ANY),
                  pl.BlockSpec((1, d, dff2), exp3), pl.BlockSpec((1, 1, dff2), exp3),
                  pl.BlockSpec((1, dff, d), exp3), pl.BlockSpec((1, 1, d), exp3)],
        out_specs=pl.BlockSpec(memory_space=pl.ANY),
        scratch_shapes=[pltpu.VMEM((d, dff2), BF16), pltpu.VMEM((dff, d), BF16),
                        pltpu.VMEM((2, unit, d), BF16), pltpu.VMEM((2, unit, d), BF16),
                        pltpu.SemaphoreType.DMA((2,)), pltpu.SemaphoreType.DMA((2,)),
                        pltpu.SMEM((2,), jnp.int32)],
    )
    return pl.pallas_call(
        _expert_kernel,
        grid_spec=grid_spec,
        out_shape=jax.ShapeDtypeStruct((n_rows, d), BF16),
        compiler_params=_params(("arbitrary",)),
        name="experts",
    )(row0, nblk, ubase, limit, xs, w_gu, b_gu.reshape(ne, 1, dff2), w_dn, b_dn.reshape(ne, 1, d))


def _combine_kernel(tab_ref, ys_hbm, route_ref, toff_ref, x1_ref, gf_ref, o_ref, buf_ref, sem, *, n_chunks, tile0):
    step = pl.program_id(0)
    n_steps = pl.num_programs(0)
    t = tile0 + step
    slot = step % 2
    nb, tr, d = x1_ref.shape
    m = nb * tr
    rows = buf_ref.shape[1]
    g = BF16_ROWS

    def fetch(tile, which):
        for c in range(n_chunks):
            pltpu.make_async_copy(
                ys_hbm.at[pl.ds(pl.multiple_of(tab_ref[tile, c], g), g)],
                buf_ref.at[which, pl.ds(c * g, g)], sem.at[which]).start()

    @pl.when(step == 0)
    def _():
        fetch(t, slot)

    @pl.when(step + 1 < n_steps)
    def _():
        fetch(t + 1, 1 - slot)

    pos = _slot_rows(route_ref, toff_ref, m)
    gates = [route_ref[0, 2 * TOP_K + k:2 * TOP_K + k + 1, :] for k in range(TOP_K)]
    stacked = jnp.concatenate(pos + gates + [jnp.zeros((LANES - 2 * TOP_K, m), F32)], axis=0)
    cols = stacked.T
    for _ in range(n_chunks):
        pltpu.make_async_copy(ys_hbm.at[pl.ds(0, g)], buf_ref.at[slot, pl.ds(0, g)], sem.at[slot]).wait()

    acc = jnp.zeros((m, d), F32)
    for rc in range(rows // m):
        c_iota = (lax.broadcasted_iota(jnp.int32, (m, m), 1) + rc * m).astype(F32)
        weights = jnp.zeros((m, m), F32)
        for k in range(TOP_K):
            weights = jnp.where(c_iota == cols[:, k:k + 1], cols[:, TOP_K + k:TOP_K + k + 1], weights)
        acc = acc + jnp.dot(weights.astype(BF16), buf_ref[slot, rc * m:(rc + 1) * m, :],
                            preferred_element_type=F32)
    o_ref[...] = x1_ref[...] + gf_ref[...] * acc.reshape(nb, tr, d)


def _combine(table, ys, route, toff_b, x1, gf, nb, tr, tile0, n_chunks):
    nbt, t, d = x1.shape
    m = nb * tr
    assert m == TOKEN_TILE
    rows = _tile_rows(m)
    tiles_per_seq = t // tr
    n_steps = (nbt // nb) * tiles_per_seq
    xmap = lambda s, tab: (s // tiles_per_seq, s % tiles_per_seq, 0)
    grid_spec = pltpu.PrefetchScalarGridSpec(
        num_scalar_prefetch=1,
        grid=(n_steps,),
        in_specs=[pl.BlockSpec(memory_space=pl.ANY),
                  pl.BlockSpec((1, 4 * TOP_K, m), lambda s, tab: (tile0 + s, 0, 0)),
                  pl.BlockSpec((1, N_EXPERTS, LANES), lambda s, tab: (tile0 + s, 0, 0)),
                  pl.BlockSpec((nb, tr, d), xmap),
                  pl.BlockSpec((nb, 1, d), lambda s, tab: (s // tiles_per_seq, 0, MOD_GATE_FFN))],
        out_specs=pl.BlockSpec((nb, tr, d), xmap),
        scratch_shapes=[pltpu.VMEM((2, rows, d), BF16), pltpu.SemaphoreType.DMA((2,))],
    )
    return pl.pallas_call(
        functools.partial(_combine_kernel, n_chunks=n_chunks, tile0=tile0),
        grid_spec=grid_spec,
        out_shape=jax.ShapeDtypeStruct((nbt, t, d), F32),
        compiler_params=_params(("arbitrary",)),
        name="combine",
    )(table, ys, route, toff_b, x1, gf)


def _block_diag(w, groups):
    n, k, _ = w.shape
    w = w.reshape(n // groups, groups, k, k)
    eye = jnp.eye(groups, dtype=w.dtype)
    return jnp.einsum("ngij,gh->ngihj", w, eye).reshape(n // groups, groups * k, groups * k)


def _layer(xp, xs, mod_p, mod_s, k_cache, v_cache, conv_state, lru_state, lw):
    (ln_mix, ln_ffn, w_in, q_norm, k_norm, rel_bias, conv_w, conv_b, w_rg, b_rg, w_ig, b_ig, lam,
     w_out, w_router, b_router, w_gu, b_gu, w_dn, b_dn) = lw
    bp, s, d = xp.shape
    bs, ts, _ = xs.shape
    aw = w_out.shape[0] // 2
    nh = aw // HEAD_DIM
    m = TOKEN_TILE
    assert s % m == 0 and bs * ts == m and s % ATTN_Q_TILE == 0

    w_in_bf = w_in.astype(BF16)
    w_out_bf = w_out.astype(BF16)
    qn_t = jnp.tile(q_norm * (HEAD_DIM ** -0.5), nh).reshape(1, aw)
    kn_t = jnp.tile(k_norm, nh).reshape(1, aw)
    head_mean = _block_diag(jnp.full((nh, HEAD_DIM, HEAD_DIM), 1.0 / HEAD_DIM, F32), nh)[0].astype(BF16)
    groups = MXU_DIM // w_rg.shape[-1]
    wa_bd = _block_diag(w_rg, groups).astype(BF16)
    wx_bd = _block_diag(w_ig, groups).astype(BF16)
    lw_c = b_rg.size
    b_a = b_rg.reshape(1, lw_c)
    b_x = b_ig.reshape(1, lw_c)
    lam2 = lam.reshape(1, lw_c)
    cb2 = conv_b.reshape(1, lw_c)
    ln_mix2 = ln_mix.reshape(1, d)
    ln_ffn2 = ln_ffn.reshape(1, d)
    wr_t = w_router.T.astype(BF16)
    br = b_router.reshape(-1, 1)
    tab_p = _bias_table(rel_bias, 3 * ATTN_Q_TILE - 1)
    r_cache = k_cache.shape[1]
    tab_s = _bias_table(rel_bias, r_cache + ts - 1)

    zeros_pre = jnp.zeros((bp, SUBLANES, lw_c), F32)
    zeros_h = jnp.zeros((bp, 1, lw_c), F32)
    pre_s = jnp.pad(conv_state, ((0, 0), (SUBLANES - (CONV_WIDTH - 1), 0), (0, 0)))
    lru_w = (conv_w, cb2, wa_bd, wx_bd, b_a, b_x, lam2)
    qp, kp, vp, k32p, v32p, lru_p, tail_p, hl_p = _mixin(
        xp, mod_p,ln_mix2, w_in_bf, qn_t, kn_t, head_mean, zeros_pre, zeros_h, *lru_w, 1, m)
    qs, ks, vs, k32s, v32s, lru_s, tail_s, hl_s = _mixin(
        xs, mod_s,ln_mix2, w_in_bf, qn_t, kn_t, head_mean, pre_s, lru_state[:, None, :], *lru_w, bs, ts)
    attn_p = _attn_prompt(qp, kp, vp, tab_p)
    attn_s = _attn_step(qs, ks, vs, jnp.transpose(k_cache, (0, 2, 3, 1)), jnp.transpose(v_cache, (0, 2, 3, 1)),
                        tab_s, ATTN_STEP_BATCH)

    n_tiles = bp * (s // m) + 1
    x1p, h2, route, cnt = _outproj(attn_p, lru_p, xp, mod_p,ln_ffn2, w_out_bf, wr_t, br,
                                   1, m, n_tiles, 0, None)
    x1s, h2, route, cnt = _outproj(attn_s, lru_s, xs, mod_s,ln_ffn2, w_out_bf, wr_t, br,
                                   bs, ts, n_tiles, n_tiles - 1, (h2, route, cnt))

    n_chunks, _, _, n_rows = _table_sizes(n_tiles)
    assert TOP_K * m * n_tiles >= EXPERT_UNIT_BLOCKS * EXPERT_ROWS
    dispatch_tab, combine_tab, toff_b, row0, nblk, ubase, limit = _route_tables(cnt[:, :, 0].astype(jnp.int32))
    xs_sorted = _dispatch(dispatch_tab, h2, route, toff_b, n_rows, n_chunks)
    ys_sorted = _experts(row0, nblk, ubase, limit, xs_sorted, w_gu, b_gu, w_dn, b_dn)
    yp = _combine(combine_tab, ys_sorted, route, toff_b, x1p, mod_p, 1, m, 0, n_chunks)
    ysm = _combine(combine_tab, ys_sorted, route, toff_b, x1s, mod_s, bs, ts, n_tiles - 1, n_chunks)

    keep = k32p.shape[1]
    new = (k32p.reshape(bp, keep, nh, HEAD_DIM), v32p.reshape(bp, keep, nh, HEAD_DIM),
           tail_p[:, SUBLANES - (CONV_WIDTH - 1):, :], hl_p[:, 0, :],
           k32s.reshape(bs, ts, nh, HEAD_DIM), v32s.reshape(bs, ts, nh, HEAD_DIM),
           tail_s[:, SUBLANES - (CONV_WIDTH - 1):, :], hl_s[:, 0, :])
    return yp, ysm, new


def kernel(x_prompt, x_sample, c_prompt, c_sample, cache_k, cache_v, state_conv, state_lru, ln_mix_w, ln_ffn_w, w_ada, b_ada, w_in, q_norm_w, k_norm_w, rel_bias, conv_w, conv_b, w_rgate, b_rgate, w_igate, b_igate, lru_lambda, w_out, w_router, b_router, w_gate_up, b_gate_up, w_down, b_down):
    depth = w_in.shape[0]
    yp, ys = x_prompt, x_sample
    collected = [[] for _ in range(8)]
    for l in range(depth):
        mod_p, mod_s = _ada(c_prompt, c_sample, w_ada[l], b_ada[l])
        lw = (ln_mix_w[l], ln_ffn_w[l], w_in[l], q_norm_w[l], k_norm_w[l], rel_bias[l], conv_w[l], conv_b[l],
              w_rgate[l], b_rgate[l], w_igate[l], b_igate[l], lru_lambda[l], w_out[l], w_router[l], b_router[l],
              w_gate_up[l], b_gate_up[l], w_down[l], b_down[l])
        yp, ys, new = _layer(yp, ys, mod_p, mod_s, cache_k[l], cache_v[l], state_conv[l], state_lru[l], lw)
        for acc, val in zip(collected, new):
            acc.append(val)
    return (yp, ys) + tuple(jnp.stack(vals) for vals in collected)
```

```python
import functools

import jax
import jax.numpy as jnp
from jax import lax
from jax.experimental import pallas as pl
from jax.experimental.pallas import tpu as pltpu

F32 = jnp.float32
BF16 = jnp.bfloat16

CHUNK = 64
N_LEFT_CHUNKS = 8
ATTN_WINDOW = N_LEFT_CHUNKS * CHUNK
HEAD_DIM = 64
REL_CLIP = 128
CONV_WIDTH = 4
LRU_C = 8.0
N_EXPERTS = 32
TOP_K = 4
SWIGLU_LIMIT = 7.0
SWIGLU_ALPHA = 1.702
NORM_EPS = 1e-6
NEG_INF = -1e30

LANES = 128
SUBLANES = 8
BF16_ROWS = 16
MXU_DIM = 256

TOKEN_TILE = 512
ATTN_Q_TILE = 256
ATTN_STEP_BATCH = 4
EXPERT_ROWS = 256
EXPERT_UNIT_BLOCKS = 4
BIAS_TABLE = 1024
VMEM_LIMIT = 32 * 1024 * 1024
EXPERT_VMEM_LIMIT = 48 * 1024 * 1024


def _params(sem, vmem=VMEM_LIMIT):
    return pltpu.CompilerParams(dimension_semantics=sem, vmem_limit_bytes=vmem)


def _ada_kernel(cp_ref, cs_ref, w_ref, b_ref, op_ref, os_ref):
    w = w_ref[...].astype(BF16)
    for c_ref, o_ref in ((cp_ref, op_ref), (cs_ref, os_ref)):
        c = c_ref[...]
        s = (c * jax.nn.sigmoid(c)).astype(BF16)
        o_ref[...] = jnp.dot(s, w, preferred_element_type=F32) + b_ref[...]


def _ada(c_p, c_s, w_ada, b_ada):
    (n_p, d), n_s = c_p.shape, c_s.shape[0]
    nout = w_ada.shape[1]
    tn = 1024
    mod_p, mod_s = pl.pallas_call(
        _ada_kernel,
        grid=(nout // tn,),
        in_specs=[pl.BlockSpec((n_p, d), lambda j: (0, 0)), pl.BlockSpec((n_s, d), lambda j: (0, 0)),
                  pl.BlockSpec((d, tn), lambda j: (0, j)),
                  pl.BlockSpec((1, tn), lambda j: (0, j))],
        out_specs=[pl.BlockSpec((n_p, tn), lambda j: (0, j)), pl.BlockSpec((n_s, tn), lambda j: (0, j))],
        out_shape=[jax.ShapeDtypeStruct((n_p, nout), F32), jax.ShapeDtypeStruct((n_s, nout), F32)],
        compiler_params=_params(("arbitrary",)),
        name="ada",
    )(c_p, c_s, w_ada, b_ada.reshape(1, nout))
    return mod_p.reshape(n_p, 1, nout), mod_s.reshape(n_s, 1, nout)


MOD_SHIFT_MIX, MOD_SCALE_MIX, MOD_GATE_MIX, MOD_SHIFT_FFN, MOD_SCALE_FFN, MOD_GATE_FFN = range(6)


def _mod_spec(nb, d, term):
    return pl.BlockSpec((nb, 1, d), lambda b, i: (b, 0, term))


def _mixin_kernel(x_ref, sh_ref, sc_ref, ln_ref, win_ref, qn_ref, kn_ref, bd_ref,
                  pre_ref, h0_ref, cw_ref, cb_ref, wa_ref, wx_ref, ba_ref, bx_ref, lam_ref,
                  q_ref, k_ref, v_ref, k32_ref, v32_ref, lru_ref, tail_ref, hl_ref, cx_ref, ch_ref):
    nb, tr, d = x_ref.shape
    m = nb * tr
    aw = q_ref.shape[-1]

    @pl.when(pl.program_id(1) == 0)
    def _():
        cx_ref[...] = pre_ref[...]
        ch_ref[...] = h0_ref[...]

    x = x_ref[...]
    ms = jnp.mean(x * x, axis=-1, keepdims=True)
    h = x * lax.rsqrt(ms + NORM_EPS) * (ln_ref[...] * (1.0 + sc_ref[...])) + sh_ref[...]
    hb = h.reshape(m, d).astype(BF16)

    def proj(part):
        return jnp.dot(hb, win_ref[:, part * aw:(part + 1) * aw], preferred_element_type=F32)

    def head_norm(t, w_ref):
        msq = jnp.dot((t * t).astype(BF16), bd_ref[...], preferred_element_type=F32)
        return t * lax.rsqrt(msq + NORM_EPS) * w_ref[...]

    lru_out, new_tail, h_last = _lru_branch(
        proj(3).reshape(nb, tr, aw), proj(4).reshape(nb, tr, aw), cw_ref, cb_ref,
        wa_ref, wx_ref, ba_ref, bx_ref, lam_ref, cx_ref, ch_ref)
    lru_ref[...] = lru_out
    tail_ref[...] = new_tail
    hl_ref[...] = h_last
    q = head_norm(proj(0), qn_ref)
    k = head_norm(proj(1), kn_ref)
    v = proj(2)
    q_ref[...] = q.astype(BF16).reshape(nb, tr, aw)
    k_ref[...] = k.astype(BF16).reshape(nb, tr, aw)
    v_ref[...] = v.astype(BF16).reshape(nb, tr, aw)
    k32_ref[...] = k.reshape(nb, tr, aw)
    v32_ref[...] = v.reshape(nb, tr, aw)


def _mixin(x, mod, ln_w, w_in_bf, qn_t, kn_t, bd, pre, h0, conv_w, conv_b, wa_bd, wx_bd, b_a, b_x, lam,
           nb, tr):
    nbt, t, d = x.shape
    aw = qn_t.shape[-1]
    c = pre.shape[-1]
    assert c == aw
    keep = min(ATTN_WINDOW, t)
    assert tr == keep or t == tr
    grid = (nbt // nb, t // tr)
    xmap = lambda b, i: (b, i, 0)
    mmap = lambda b, i: (b, 0, 0)
    cmap = lambda b, i: (0, 0)
    cmap3 = lambda b, i: (0, 0, 0)
    tmap = lambda b, i: (b, 0, 0)
    big = pl.BlockSpec((nb, tr, aw), xmap)
    tail = pl.BlockSpec((nb, keep, aw), tmap)
    row = pl.BlockSpec((1, c), cmap)
    return pl.pallas_call(
        _mixin_kernel,
        grid=grid,
        in_specs=[pl.BlockSpec((nb, tr, d), xmap),
                  _mod_spec(nb, d, MOD_SHIFT_MIX), _mod_spec(nb, d, MOD_SCALE_MIX),
                  pl.BlockSpec((1, d), cmap),
                  pl.BlockSpec(w_in_bf.shape, cmap),
                  pl.BlockSpec((1, aw), cmap), pl.BlockSpec((1, aw), cmap),
                  pl.BlockSpec(bd.shape, cmap),
                  pl.BlockSpec((nb, SUBLANES, c), tmap), pl.BlockSpec((nb, 1, c), tmap),
                  pl.BlockSpec(conv_w.shape, cmap), row,
                  pl.BlockSpec(wa_bd.shape, cmap3), pl.BlockSpec(wx_bd.shape, cmap3),
                  row, row, row],
        out_specs=[big, big, big, tail, tail, big,
                   pl.BlockSpec((nb, SUBLANES, c), tmap), pl.BlockSpec((nb, 1, c), tmap)],
        out_shape=[jax.ShapeDtypeStruct((nbt, t, aw), BF16)] * 3
        + [jax.ShapeDtypeStruct((nbt, keep, aw), F32)] * 2
        + [jax.ShapeDtypeStruct((nbt, t, c), BF16),
           jax.ShapeDtypeStruct((nbt, SUBLANES, c), F32),
           jax.ShapeDtypeStruct((nbt, 1, c), F32)],
        scratch_shapes=[pltpu.VMEM((nb, SUBLANES, c), F32), pltpu.VMEM((nb, 1, c), F32)],
        compiler_params=_params(("arbitrary", "arbitrary")),
        name="mixin",
    )(x, mod, mod, ln_w, w_in_bf, qn_t, kn_t, bd, pre, h0, conv_w, conv_b, wa_bd, wx_bd, b_a, b_x, lam)


def _bias_table(rel_bias, off):
    h = rel_bias.shape[0]
    left = off - REL_CLIP
    right = BIAS_TABLE - left - (2 * REL_CLIP + 1)
    assert left >= 0 and right >= 0
    return jnp.concatenate([jnp.broadcast_to(rel_bias[:, :1], (h, left)), rel_bias,
                            jnp.broadcast_to(rel_bias[:, -1:], (h, right))], axis=1)


def _toeplitz(tab_row, rows, cols):
    t = jnp.broadcast_to(tab_row, (rows, BIAS_TABLE))
    t = pltpu.roll(t, BIAS_TABLE - (rows - 1), 1, stride=1, stride_axis=0)
    return t[:, :cols]


def _attn_kernel(q_ref, k0_ref, k1_ref, k2_ref, v0_ref, v1_ref, v2_ref, tab_ref, o_ref, bias_ref):
    b = pl.program_id(0)
    s = pl.program_id(1)
    qt = q_ref.shape[1]
    nk = 3 * qt
    nh = bias_ref.shape[0]

    @pl.when((b == 0) & (s == 0))
    def _():
        qi = lax.broadcasted_iota(jnp.int32, (qt, nk), 0) // CHUNK
        kc = lax.broadcasted_iota(jnp.int32, (qt, nk), 1) // CHUNK
        for h in range(nh):
            band = jnp.where(kc <= qi + N_LEFT_CHUNKS, _toeplitz(tab_ref[h:h + 1, :], qt, nk), NEG_INF)
            bias_ref[h] = jnp.where(kc >= qi, band, NEG_INF)

    pair_w = 2 * HEAD_DIM

    def attend(mask_start):
        q = q_ref[0]
        kcat = jnp.concatenate([k0_ref[0], k1_ref[0], k2_ref[0]], axis=0)
        vcat = jnp.concatenate([v0_ref[0], v1_ref[0], v2_ref[0]], axis=0)
        first = lax.broadcasted_iota(jnp.int32, (qt, pair_w), 1) < HEAD_DIM
        keep = [jnp.where(first, 1.0, 0.0).astype(BF16), jnp.where(first, 0.0, 1.0).astype(BF16)]
        if mask_start:
            in_seq = lax.broadcasted_iota(jnp.int32, (qt, nk), 1) >= (2 - s) * qt
        outs = []
        for pair in range(nh // 2):
            sl = slice(pair * pair_w, (pair + 1) * pair_w)
            q2, k2, v2 = q[:, sl], kcat[:, sl], vcat[:, sl]
            per_head = []
            for sub in range(2):
                sc = lax.dot_general(q2 * keep[sub], k2, (((1,), (1,)), ((), ())), preferred_element_type=F32)
                sc = sc + bias_ref[2 * pair + sub]
                if mask_start:
                    sc = jnp.where(in_seq, sc, NEG_INF)
                mx = jnp.max(sc, axis=-1, keepdims=True)
                p = jnp.exp(sc - mx)
                l = jnp.sum(p, axis=-1, keepdims=True)
                per_head.append(jnp.dot(p.astype(BF16), v2, preferred_element_type=F32) / l)
            outs.append(jnp.where(first, per_head[0], per_head[1]))
        o_ref[0] = jnp.concatenate(outs, axis=-1).astype(BF16)

    @pl.when(s < 2)
    def _():
        attend(True)

    @pl.when(s >= 2)
    def _():
        attend(False)


def _attn_prompt(q, k, v, tab):
    b, s, aw = q.shape
    qt = ATTN_Q_TILE
    nh = aw // HEAD_DIM
    qspec = pl.BlockSpec((1, qt, aw), lambda i, j: (i, j, 0))

    def kspec(back):
        return pl.BlockSpec((1, qt, aw), lambda i, j: (i, jnp.maximum(j - back, 0), 0))

    return pl.pallas_call(
        _attn_kernel,
        grid=(b, s // qt),
        in_specs=[qspec, kspec(2), kspec(1), kspec(0), kspec(2), kspec(1), kspec(0),
                  pl.BlockSpec(tab.shape, lambda i, j: (0, 0))],
        out_specs=qspec,
        out_shape=jax.ShapeDtypeStruct((b, s, aw), BF16),
        scratch_shapes=[pltpu.VMEM((nh, qt, 3 * qt), F32)],
        compiler_params=_params(("arbitrary", "arbitrary")),
        name="attn_prompt",
    )(q, k, k, k, v, v, v, tab)


def _attn_step_kernel(q_ref, kn_ref, vn_ref, ck_ref, cv_ref, tab_ref, o_ref, bias_ref):
    step = pl.program_id(0)
    nbs, t, aw = q_ref.shape
    nh = aw // HEAD_DIM
    rows = nh * t
    r = ck_ref.shape[-1]
    nk = r + LANES
    nt_dims = (((1,), (1,)), ((), ()))

    @pl.when(step == 0)
    def _():
        ok = lax.broadcasted_iota(jnp.int32, (t, nk), 1) < r + t
        for h in range(nh):
            bias_ref[h * t:(h + 1) * t, :] = jnp.where(ok, _toeplitz(tab_ref[h:h + 1, :], t, nk), NEG_INF)

    own = (lax.broadcasted_iota(jnp.int32, (rows, aw), 0) // t
           == lax.broadcasted_iota(jnp.int32, (rows, aw), 1) // HEAD_DIM)
    own_f = jnp.where(own, 1.0, 0.0)
    own_bf = own_f.astype(BF16)
    pad = jnp.zeros((LANES - t, aw), BF16)
    for b in range(nbs):
        q_bd = jnp.concatenate([q_ref[b]] * nh, axis=0) * own_bf
        k_old = ck_ref[b].reshape(aw, r).astype(BF16)
        v_old = cv_ref[b].reshape(aw, r).astype(BF16)
        k_new = jnp.concatenate([kn_ref[b], pad], axis=0)
        v_new = jnp.concatenate([vn_ref[b], pad], axis=0)
        s_old = jnp.dot(q_bd, k_old, preferred_element_type=F32) + bias_ref[:, :r]
        s_new = lax.dot_general(q_bd, k_new, nt_dims, preferred_element_type=F32) + bias_ref[:, r:]
        mx = jnp.maximum(jnp.max(s_old, axis=-1, keepdims=True), jnp.max(s_new, axis=-1, keepdims=True))
        p_old = jnp.exp(s_old - mx)
        p_new = jnp.exp(s_new - mx)
        l = jnp.sum(p_old, axis=-1, keepdims=True) + jnp.sum(p_new, axis=-1, keepdims=True)
        o_all = (lax.dot_general(p_old.astype(BF16), v_old, nt_dims, preferred_element_type=F32)
                 + jnp.dot(p_new.astype(BF16), v_new, preferred_element_type=F32))
        o_all = o_all * own_f / l
        out = o_all[0:t]
        for h in range(1, nh):
            out = out + o_all[h * t:(h + 1) * t]
        o_ref[b] = out.astype(BF16)


def _attn_step(q, kn, vn, ck, cv, tab, nbs):
    b, t, aw = q.shape
    nh = aw // HEAD_DIM
    r = ck.shape[-1]
    new = pl.BlockSpec((nbs, t, aw), lambda i: (i, 0, 0))
    old = pl.BlockSpec((nbs, nh, HEAD_DIM, r), lambda i: (i, 0, 0, 0))
    return pl.pallas_call(
        _attn_step_kernel,
        grid=(b // nbs,),
        in_specs=[new, new, new, old, old, pl.BlockSpec(tab.shape, lambda i: (0, 0))],
        out_specs=new,
        out_shape=jax.ShapeDtypeStruct((b, t, aw), BF16),
        scratch_shapes=[pltpu.VMEM((nh * t, r + LANES), F32)],
        compiler_params=_params(("arbitrary",)),
        name="attn_step",
    )(q, kn, vn, ck, cv, tab)


def _gelu_tanh(x):
    return x * (0.5 * (1.0 + jnp.tanh(0.7978845608028654 * (x + 0.044715 * (x * x * x)))))


def _lru_branch(x, yg, cw_ref, cb_ref, wa_ref, wx_ref, ba_ref, bx_ref, lam_ref, cx_ref, ch_ref):
    nb, tr, c = x.shape
    m = nb * tr
    half = c // 2
    xp = jnp.concatenate([cx_ref[...], x], axis=1)
    new_tail = xp[:, tr:tr + SUBLANES, :]
    groups = tr // SUBLANES
    xg = xp.reshape(nb * (groups + 1), SUBLANES, c)
    first_rows = lax.broadcasted_iota(jnp.int32, (nb, groups, SUBLANES, c), 2)
    y = cb_ref[...] + cw_ref[CONV_WIDTH - 1:CONV_WIDTH, :] * x
    for back in range(1, CONV_WIDTH):
        rot = pltpu.roll(xg, back, 1).reshape(nb, groups + 1, SUBLANES, c)
        shifted = jnp.where(first_rows >= back, rot[:, 1:], rot[:, :groups]).reshape(nb, tr, c)
        y = y + cw_ref[CONV_WIDTH - 1 - back:CONV_WIDTH - back, :] * shifted
    y2 = y.reshape(m, c)
    yb = y2.astype(BF16)

    def gate(w_ref, b_ref):
        g = jnp.concatenate(
            [jnp.dot(yb[:, :half], w_ref[0], preferred_element_type=F32),
             jnp.dot(yb[:, half:], w_ref[1], preferred_element_type=F32)], axis=1)
        return jax.nn.sigmoid(g + b_ref[...])

    rg = gate(wa_ref, ba_ref)
    ig = gate(wx_ref, bx_ref)
    lam = lam_ref[...]
    log_sig = jnp.minimum(lam, 0.0) - jnp.log1p(jnp.exp(-jnp.abs(lam)))
    log_a = rg * (LRU_C * log_sig)
    a_cum = jnp.exp(log_a)
    b_cum = jnp.sqrt(-jnp.tanh(log_a) * (a_cum * a_cum + 1.0)) * (ig * y2)
    a_cum = a_cum.reshape(nb * groups, SUBLANES, c)
    b_cum = b_cum.reshape(nb * groups, SUBLANES, c)
    row = lax.broadcasted_iota(jnp.int32, a_cum.shape, 1)
    dist = 1
    while dist < SUBLANES:
        keep = row >= dist
        a_sh = jnp.where(keep, pltpu.roll(a_cum, dist, 1), 1.0)
        b_sh = jnp.where(keep, pltpu.roll(b_cum, dist, 1), 0.0)
        b_cum = a_cum * b_sh + b_cum
        a_cum = a_cum * a_sh
        dist *= 2
    a_grp = a_cum.reshape(nb, groups, SUBLANES, c)
    b_grp = b_cum.reshape(nb, groups, SUBLANES, c)
    carry = ch_ref[...]
    pieces = []
    for grp in range(groups):
        h_grp = a_grp[:, grp] * carry + b_grp[:, grp]
        carry = h_grp[:, SUBLANES - 1:SUBLANES, :]
        pieces.append(h_grp)
    h = jnp.concatenate(pieces, axis=1)
    ch_ref[...] = carry
    cx_ref[...] = new_tail
    return (h * _gelu_tanh(yg)).astype(BF16), new_tail, carry


def _outproj_kernel(*refs, aliased):
    (at_ref, lr_ref, x_ref, gm_ref, shf_ref, scf_ref, lnf_ref, wo_ref, wr_ref, br_ref) = refs[:10]
    x1_ref, h2_ref, route_ref, cnt_ref = refs[10 + aliased:]
    nb, tr, d = x_ref.shape
    m = nb * tr
    aw = at_ref.shape[-1]
    ne = wr_ref.shape[0]
    at = at_ref[...].reshape(m, aw)
    lr = lr_ref[...].reshape(m, aw)
    mix = (jnp.dot(at, wo_ref[0:aw, :], preferred_element_type=F32)
           + jnp.dot(lr, wo_ref[aw:2 * aw, :], preferred_element_type=F32))
    x1 = x_ref[...] + gm_ref[...] * mix.reshape(nb, tr, d)
    x1_ref[...] = x1
    ms = jnp.mean(x1 * x1, axis=-1, keepdims=True)
    h2 = (x1 * lax.rsqrt(ms + NORM_EPS) * (lnf_ref[...] * (1.0 + scf_ref[...])) + shf_ref[...]).reshape(m, d)
    h2_ref[...] = h2.astype(BF16)

    logits = lax.dot_general(wr_ref[...], h2.astype(BF16), (((1,), (1,)), ((), ())),
                             preferred_element_type=F32) + br_ref[...]
    e_iota = lax.broadcasted_iota(jnp.int32, (ne, m), 0).astype(F32)
    vals = logits
    top_v, sels = [], []
    for k in range(TOP_K):
        mx = jnp.max(vals, axis=0, keepdims=True)
        idx = jnp.min(jnp.where(vals == mx, e_iota, float(ne)), axis=0, keepdims=True)
        sel = e_iota == idx
        vals = jnp.where(sel, -jnp.inf, vals)
        top_v.append(mx)
        sels.append(sel)
        route_ref[0, k:k + 1, :] = idx
    ex = [jnp.exp(v - top_v[0]) for v in top_v]
    den = ex[0] + ex[1] + ex[2] + ex[3]
    chosen = jnp.zeros((ne, m), F32)
    for k in range(TOP_K):
        route_ref[0, 2 * TOP_K + k:2 * TOP_K + k + 1, :] = ex[k] / den
        chosen = chosen + jnp.where(sels[k], 1.0, 0.0)
    before = (lax.broadcasted_iota(jnp.int32, (m, m), 0) < lax.broadcasted_iota(jnp.int32, (m, m), 1))
    rank = jnp.dot(chosen.astype(BF16), jnp.where(before, 1.0, 0.0).astype(BF16), preferred_element_type=F32)
    for k in range(TOP_K):
        route_ref[0, TOP_K + k:TOP_K + k + 1, :] = jnp.sum(jnp.where(sels[k], rank, 0.0), axis=0, keepdims=True)
    route_ref[0, 3 * TOP_K:4 * TOP_K, :] = jnp.zeros((TOP_K, m), F32)
    cnt_ref[0] = jnp.broadcast_to(jnp.sum(chosen, axis=1, keepdims=True), (ne, LANES))


def _outproj(attn, lru_o, x, mod, lnf, w_out_bf, wr_t, br, nb, tr, n_tiles, tile0, prev):
    nbt, t, d = x.shape
    aw = attn.shape[-1]
    m = nb * tr
    assert m == TOKEN_TILE
    ne = wr_t.shape[0]
    tiles_per_seq = t // tr
    xmap = lambda b, i: (b, i, 0)
    c2 = lambda b, i: (0, 0)
    tile = lambda b, i: (tile0 + b * tiles_per_seq + i, 0)
    tile3 = lambda b, i: (tile0 + b * tiles_per_seq + i, 0, 0)
    in_specs = [pl.BlockSpec((nb, tr, aw), xmap), pl.BlockSpec((nb, tr, aw), xmap),
                pl.BlockSpec((nb, tr, d), xmap),
                _mod_spec(nb, d, MOD_GATE_MIX), _mod_spec(nb, d, MOD_SHIFT_FFN), _mod_spec(nb, d, MOD_SCALE_FFN),
                pl.BlockSpec((1, d), c2), pl.BlockSpec(w_out_bf.shape, c2),
                pl.BlockSpec(wr_t.shape, c2), pl.BlockSpec((ne, 1), c2)]
    args = [attn, lru_o, x, mod, mod, mod, lnf, w_out_bf, wr_t, br]
    aliases = {}
    if prev is not None:
        in_specs += [pl.BlockSpec(memory_space=pl.ANY)] * 3
        args += list(prev)
        aliases = {10: 1, 11: 2, 12: 3}
    return pl.pallas_call(
        functools.partial(_outproj_kernel, aliased=len(aliases)),
        grid=(nbt // nb, tiles_per_seq),
        in_specs=in_specs,
        out_specs=[pl.BlockSpec((nb, tr, d), xmap), pl.BlockSpec((m, d), tile),
                   pl.BlockSpec((1, 4 * TOP_K, m), tile3), pl.BlockSpec((1, ne, LANES), tile3)],
        out_shape=[jax.ShapeDtypeStruct((nbt, t, d), F32),
                   jax.ShapeDtypeStruct((n_tiles * m, d), BF16),
                   jax.ShapeDtypeStruct((n_tiles, 4 * TOP_K, m), F32),
                   jax.ShapeDtypeStruct((n_tiles, ne, LANES), F32)],
        input_output_aliases=aliases,
        compiler_params=_params(("arbitrary", "arbitrary")),
        name="outproj",
    )(*args)


def _tile_rows(m):
    cap = TOP_K * m + N_EXPERTS * (BF16_ROWS - 1) + BF16_ROWS
    return -(-cap // TOKEN_TILE) * TOKEN_TILE


def _table_sizes(nt):
    g = BF16_ROWS
    m = TOKEN_TILE
    n_chunks = _tile_rows(m) // g
    n_gap = -(-(N_EXPERTS * (EXPERT_ROWS // g - 1)) // nt)
    bound = TOP_K * m * nt + nt * N_EXPERTS * (g - 1) + N_EXPERTS * (EXPERT_ROWS - g)
    n_sorted = -(-bound // EXPERT_ROWS) * EXPERT_ROWS
    return n_chunks, n_gap, n_sorted, n_sorted + 2 * (n_chunks + n_gap) * g


def _route_tables(cnt):
    nt = cnt.shape[0]
    g = BF16_ROWS
    bm = EXPERT_ROWS
    n_chunks, n_gap, n_sorted, _ = _table_sizes(nt)
    e_ids = jnp.arange(N_EXPERTS, dtype=jnp.int32)
    t_ids = jnp.arange(nt, dtype=jnp.int32)
    upto = (e_ids[:, None] <= e_ids[None, :]).astype(jnp.int32)
    pc = (cnt + g - 1) // g * g
    ctile = jnp.sum(pc[:, :, None] * upto[None], axis=1)
    toff = ctile - pc
    trow = ctile[:, -1]
    tot = jnp.sum(pc, axis=0)
    reg = (tot + bm - 1) // bm * bm
    creg = jnp.sum(reg[:, None] * upto, axis=0)
    base = creg - reg
    earlier = (t_ids[:, None] < t_ids[None, :]).astype(jnp.int32)
    goff = base[None, :] + jnp.sum(pc[:, None, :] * earlier[:, :, None], axis=0)
    r = jnp.arange(n_chunks, dtype=jnp.int32) * g
    r3 = r[None, :, None]
    in_seg = (toff[:, None, :] <= r3) & (r3 < ctile[:, None, :])
    dst = jnp.sum(jnp.where(in_seg, (goff - toff)[:, None, :], 0), axis=2) + r[None, :]
    dst = jnp.where(r[None, :] < trow[:, None], dst, -1)
    gcnt = (reg - tot) // g
    gcum = jnp.sum(gcnt[:, None] * upto, axis=0)
    gstart = gcum - gcnt
    s = jnp.arange(nt * n_gap, dtype=jnp.int32)
    in_gap = (gstart[None, :] <= s[:, None]) & (s[:, None] < gcum[None, :])
    gdst = jnp.sum(jnp.where(in_gap, (base + tot - g * gstart)[None, :] + g * s[:, None], 0), axis=1)
    gdst = jnp.where(s < gcum[-1], gdst, -1).reshape(nt, n_gap)
    table = jnp.concatenate([dst, gdst], axis=1).astype(jnp.int32)
    n_entries = n_chunks + n_gap
    spare = n_sorted + ((t_ids % 2)[:, None] * n_entries + jnp.arange(n_entries, dtype=jnp.int32)[None, :]) * g
    dispatch_tab = jnp.where(table >= 0, table, spare).astype(jnp.int32)
    combine_tab = jnp.maximum(dst, 0).astype(jnp.int32)
    toff_b = jnp.broadcast_to(toff.astype(F32)[:, :, None], (nt, N_EXPERTS, LANES))
    limit = (creg[-1] - EXPERT_UNIT_BLOCKS * bm).astype(jnp.int32).reshape(1)
    nblk = reg // bm
    units = (nblk + EXPERT_UNIT_BLOCKS - 1) // EXPERT_UNIT_BLOCKS
    ubase = jnp.sum(units[:, None] * upto, axis=0) - units
    return (dispatch_tab, combine_tab, toff_b, base.astype(jnp.int32), nblk.astype(jnp.int32),
            ubase.astype(jnp.int32), limit)


def _slot_rows(route_ref, toff_ref, m):
    ne = toff_ref.shape[1]
    e_iota = lax.broadcasted_iota(jnp.int32, (ne, m), 0).astype(F32)
    toff_col = toff_ref[0][:, 0:1]
    pos = []
    for k in range(TOP_K):
        sel = e_iota == route_ref[0, k:k + 1, :]
        start = jnp.sum(jnp.where(sel, toff_col, 0.0), axis=0, keepdims=True)
        pos.append(start + route_ref[0, TOP_K + k:TOP_K + k + 1, :])
    return pos


def _dispatch_kernel(tab_ref, h2_ref, route_ref, toff_ref, xs_hbm, buf_ref, sem, *, n_chunks, n_tiles):
    t = pl.program_id(0)
    slot = t % 2
    m = h2_ref.shape[0]
    rows = buf_ref.shape[1]
    n_entries = tab_ref.shape[1]
    g = BF16_ROWS
    per_chunk = m // g

    def start(c):
        src = c * g if c < n_chunks else rows - g
        pltpu.make_async_copy(
            buf_ref.at[slot, pl.ds(src, g)],
            xs_hbm.at[pl.ds(pl.multiple_of(tab_ref[t, c], g), g)], sem.at[slot]).start()

    def wait_all(which):
        for _ in range(n_entries):
            pltpu.make_async_copy(buf_ref.at[which, pl.ds(0, g)], xs_hbm.at[pl.ds(0, g)], sem.at[which]).wait()

    @pl.when(t >= 2)
    def _():
        wait_all(slot)

    pos = _slot_rows(route_ref, toff_ref, m)
    h2 = h2_ref[...]
    for rc in range(rows // m):
        r_iota = (lax.broadcasted_iota(jnp.int32, (m, m), 0) + rc * m).astype(F32)
        onehot = jnp.zeros((m, m), F32)
        for k in range(TOP_K):
            onehot = jnp.where(r_iota == pos[k], 1.0, onehot)
        buf_ref[slot, rc * m:(rc + 1) * m, :] = jnp.dot(
            onehot.astype(BF16), h2, preferred_element_type=F32).astype(BF16)
        for c in range((rc - 1) * per_chunk, rc * per_chunk) if rc > 0 else ():
            start(c)
    for c in range(n_chunks - per_chunk, n_entries):
        start(c)

    @pl.when(t == n_tiles - 1)
    def _():
        if n_tiles > 1:
            wait_all(1 - slot)
        wait_all(slot)


def _dispatch(table, h2, route, toff_b, n_rows, n_chunks):
    nt = route.shape[0]
    m = TOKEN_TILE
    d = h2.shape[1]
    rows = _tile_rows(m)
    grid_spec = pltpu.PrefetchScalarGridSpec(
        num_scalar_prefetch=1,
        grid=(nt,),
        in_specs=[pl.BlockSpec((m, d), lambda t, tab: (t, 0)),
                  pl.BlockSpec((1, 4 * TOP_K, m), lambda t, tab: (t, 0, 0)),
                  pl.BlockSpec((1, N_EXPERTS, LANES), lambda t, tab: (t, 0, 0))],
        out_specs=pl.BlockSpec(memory_space=pl.ANY),
        scratch_shapes=[pltpu.VMEM((2, rows, d), BF16), pltpu.SemaphoreType.DMA((2,))],
    )
    return pl.pallas_call(
        functools.partial(_dispatch_kernel, n_chunks=n_chunks, n_tiles=nt),
        grid_spec=grid_spec,
        out_shape=jax.ShapeDtypeStruct((n_rows, d), BF16),
        compiler_params=_params(("arbitrary",)),
        name="dispatch",
    )(table, h2, route, toff_b)


def _expert_kernel(row0_ref, nblk_ref, ubase_ref, limit_ref, xs_hbm, wgu_ref, bgu_ref, wdn_ref, bdn_ref, ys_hbm,
                   wgu_bf, wdn_bf, xbuf, ybuf, sem_in, sem_out, pend_ref):
    e = pl.program_id(0)
    ne = pl.num_programs(0)
    bm = EXPERT_ROWS
    unit = xbuf.shape[1]
    per_unit = unit // bm
    dff = wdn_ref.shape[1]
    nblk = nblk_ref[e]
    n_units = (nblk + per_unit - 1) // per_unit
    base = ubase_ref[e]

    def unit_start(expert, s):
        true = row0_ref[expert] + s * unit
        start = jnp.minimum(true, limit_ref[0])
        return pl.multiple_of(start, bm), pl.multiple_of(true - start, bm)

    def in_copy(expert, s, slot):
        start, _ = unit_start(expert, s)
        return pltpu.make_async_copy(xs_hbm.at[pl.ds(start, unit)], xbuf.at[slot], sem_in.at[slot])

    def out_copy(j, slot, i):
        start = pl.multiple_of(row0_ref[e] + j * bm, bm)
        return pltpu.make_async_copy(ybuf.at[slot, pl.ds(i * bm, bm)], ys_hbm.at[pl.ds(start, bm)],
                                     sem_out.at[slot])

    def wait_pending(slot):
        count = pend_ref[slot]

        @pl.when(count == per_unit)
        def _():
            for i in range(per_unit):
                out_copy(0, slot, i).wait()

        for i in range(per_unit - 1):
            @pl.when((count < per_unit) & (i < count))
            def _():
                out_copy(0, slot, i).wait()
        pend_ref[slot] = 0

    @pl.when(e == 0)
    def _():
        pend_ref[0] = 0
        pend_ref[1] = 0

    @pl.when((e == 0) & (nblk > 0))
    def _():
        in_copy(e, 0, 0).start()

    wgu_bf[...] = wgu_ref[0].astype(BF16)
    wdn_bf[...] = wdn_ref[0].astype(BF16)

    def run_unit(s, carry):
        slot = (base + s) % 2
        in_copy(e, s, slot).wait()

        @pl.when(s + 1 < n_units)
        def _():
            in_copy(e, s + 1, 1 - slot).start()

        wait_pending(slot)
        _, lead = unit_start(e, s)

        def ffn(first, count):
            rows = count * bm
            x = xbuf[slot, pl.ds(pl.multiple_of(lead + first * bm, bm), rows), :]
            gu = jnp.dot(x, wgu_bf[...], preferred_element_type=F32) + bgu_ref[0]
            gate = jnp.minimum(gu[:, :dff], SWIGLU_LIMIT)
            up = jnp.clip(gu[:, dff:], -SWIGLU_LIMIT, SWIGLU_LIMIT)
            glu = gate * jax.nn.sigmoid(gate * SWIGLU_ALPHA)
            act = ((up + 1.0) * glu).astype(BF16)
            ybuf[slot, first * bm:first * bm + rows, :] = (
                jnp.dot(act, wdn_bf[...], preferred_element_type=F32) + bdn_ref[0]).astype(BF16)
            for i in range(first, first + count):
                out_copy(s * per_unit + i, slot, i).start()

        for first in range(0, per_unit, 2):
            have = nblk - s * per_unit - first

            @pl.when(have >= 2)
            def _():
                ffn(first, 2)

            @pl.when(have == 1)
            def _():
                ffn(first, 1)
        pend_ref[slot] = jnp.minimum(nblk - s * per_unit, per_unit)
        return carry

    lax.fori_loop(0, n_units, run_unit, 0)

    nxt = jnp.minimum(e + 1, ne - 1)

    @pl.when((e + 1 < ne) & (nblk_ref[nxt] > 0))
    def _():
        in_copy(nxt, 0, (base + n_units) % 2).start()

    @pl.when(e == ne - 1)
    def _():
        wait_pending(0)
        wait_pending(1)


def _experts(row0, nblk, ubase, limit, xs, w_gu, b_gu, w_dn, b_dn):
    n_rows, d = xs.shape
    bm = EXPERT_ROWS
    unit = EXPERT_UNIT_BLOCKS * bm
    ne, _, dff2 = w_gu.shape
    dff = w_dn.shape[1]
    exp3 = lambda e, r0, nb, ub, lim: (e, 0, 0)
    grid_spec = pltpu.PrefetchScalarGridSpec(
        num_scalar_prefetch=4,
        grid=(ne,),
        in_specs=[pl.BlockSpec(memory_space=pl.ANY),
                  pl.BlockSpec((1, d, dff2), exp3), pl.BlockSpec((1, 1, dff2), exp3),
                  pl.BlockSpec((1, dff, d), exp3), pl.BlockSpec((1, 1, d), exp3)],
        out_specs=pl.BlockSpec(memory_space=pl.ANY),
        scratch_shapes=[pltpu.VMEM((d, dff2), BF16), pltpu.VMEM((dff, d), BF16),
                        pltpu.VMEM((2, unit, d), BF16), pltpu.VMEM((2, unit, d), BF16),
                        pltpu.SemaphoreType.DMA((2,)), pltpu.SemaphoreType.DMA((2,)),
                        pltpu.SMEM((2,), jnp.int32)],
    )
    return pl.pallas_call(
        _expert_kernel,
        grid_spec=grid_spec,
        out_shape=jax.ShapeDtypeStruct((n_rows, d), BF16),
        compiler_params=_params(("arbitrary",), EXPERT_VMEM_LIMIT),
        name="experts",
    )(row0, nblk, ubase, limit, xs, w_gu, b_gu.reshape(ne, 1, dff2), w_dn, b_dn.reshape(ne, 1, d))


def _combine_kernel(tab_ref, ys_hbm, route_ref, toff_ref, x1_ref, gf_ref, o_ref, buf_ref, sem, *, n_chunks, tile0):
    step = pl.program_id(0)
    n_steps = pl.num_programs(0)
    t = tile0 + step
    slot = step % 2
    nb, tr, d = x1_ref.shape
    m = nb * tr
    rows = buf_ref.shape[1]
    g = BF16_ROWS

    def fetch(tile, which):
        for c in range(n_chunks):
            pltpu.make_async_copy(
                ys_hbm.at[pl.ds(pl.multiple_of(tab_ref[tile, c], g), g)],
                buf_ref.at[which, pl.ds(c * g, g)], sem.at[which]).start()

    @pl.when(step == 0)
    def _():
        fetch(t, slot)

    @pl.when(step + 1 < n_steps)
    def _():
        fetch(t + 1, 1 - slot)

    pos = _slot_rows(route_ref, toff_ref, m)
    gates = [route_ref[0, 2 * TOP_K + k:2 * TOP_K + k + 1, :] for k in range(TOP_K)]
    stacked = jnp.concatenate(pos + gates + [jnp.zeros((LANES - 2 * TOP_K, m), F32)], axis=0)
    cols = stacked.T
    for _ in range(n_chunks):
        pltpu.make_async_copy(ys_hbm.at[pl.ds(0, g)], buf_ref.at[slot, pl.ds(0, g)], sem.at[slot]).wait()

    acc = jnp.zeros((m, d), F32)
    for rc in range(rows // m):
        c_iota = (lax.broadcasted_iota(jnp.int32, (m, m), 1) + rc * m).astype(F32)
        weights = jnp.zeros((m, m), F32)
        for k in range(TOP_K):
            weights = jnp.where(c_iota == cols[:, k:k + 1], cols[:, TOP_K + k:TOP_K + k + 1], weights)
        acc = acc + jnp.dot(weights.astype(BF16), buf_ref[slot, rc * m:(rc + 1) * m, :],
                            preferred_element_type=F32)
    o_ref[...] = x1_ref[...] + gf_ref[...] * acc.reshape(nb, tr, d)


def _combine(table, ys, route, toff_b, x1, gf, nb, tr, tile0, n_chunks):
    nbt, t, d = x1.shape
    m = nb * tr
    assert m == TOKEN_TILE
    rows = _tile_rows(m)
    tiles_per_seq = t // tr
    n_steps = (nbt // nb) * tiles_per_seq
    xmap = lambda s, tab: (s // tiles_per_seq, s % tiles_per_seq, 0)
    grid_spec = pltpu.PrefetchScalarGridSpec(
        num_scalar_prefetch=1,
        grid=(n_steps,),
        in_specs=[pl.BlockSpec(memory_space=pl.ANY),
                  pl.BlockSpec((1, 4 * TOP_K, m), lambda s, tab: (tile0 + s, 0, 0)),
                  pl.BlockSpec((1, N_EXPERTS, LANES), lambda s, tab: (tile0 + s, 0, 0)),
                  pl.BlockSpec((nb, tr, d), xmap),
                  pl.BlockSpec((nb, 1, d), lambda s, tab: (s // tiles_per_seq, 0, MOD_GATE_FFN))],
        out_specs=pl.BlockSpec((nb, tr, d), xmap),
        scratch_shapes=[pltpu.VMEM((2, rows, d), BF16), pltpu.SemaphoreType.DMA((2,))],
    )
    return pl.pallas_call(
        functools.partial(_combine_kernel, n_chunks=n_chunks, tile0=tile0),
        grid_spec=grid_spec,
        out_shape=jax.ShapeDtypeStruct((nbt, t, d), F32),
        compiler_params=_params(("arbitrary",)),
        name="combine",
    )(table, ys, route, toff_b, x1, gf)


def _block_diag(w, groups):
    n, k, _ = w.shape
    w = w.reshape(n // groups, groups, k, k)
    eye = jnp.eye(groups, dtype=w.dtype)
    return jnp.einsum("ngij,gh->ngihj", w, eye).reshape(n // groups, groups * k, groups * k)


def _layer(xp, xs, mod_p, mod_s, k_cache, v_cache, conv_state, lru_state, lw):
    (ln_mix, ln_ffn, w_in, q_norm, k_norm, rel_bias, conv_w, conv_b, w_rg, b_rg, w_ig, b_ig, lam,
     w_out, w_router, b_router, w_gu, b_gu, w_dn, b_dn) = lw
    bp, s, d = xp.shape
    bs, ts, _ = xs.shape
    aw = w_out.shape[0] // 2
    nh = aw // HEAD_DIM
    m = TOKEN_TILE
    assert s % m == 0 and bs * ts == m and s % ATTN_Q_TILE == 0

    w_in_bf = w_in.astype(BF16)
    w_out_bf = w_out.astype(BF16)
    qn_t = jnp.tile(q_norm * (HEAD_DIM ** -0.5), nh).reshape(1, aw)
    kn_t = jnp.tile(k_norm, nh).reshape(1, aw)
    head_mean = _block_diag(jnp.full((nh, HEAD_DIM, HEAD_DIM), 1.0 / HEAD_DIM, F32), nh)[0].astype(BF16)
    groups = MXU_DIM // w_rg.shape[-1]
    wa_bd = _block_diag(w_rg, groups).astype(BF16)
    wx_bd = _block_diag(w_ig, groups).astype(BF16)
    lw_c = b_rg.size
    b_a = b_rg.reshape(1, lw_c)
    b_x = b_ig.reshape(1, lw_c)
    lam2 = lam.reshape(1, lw_c)
    cb2 = conv_b.reshape(1, lw_c)
    ln_mix2 = ln_mix.reshape(1, d)
    ln_ffn2 = ln_ffn.reshape(1, d)
    wr_t = w_router.T.astype(BF16)
    br = b_router.reshape(-1, 1)
    tab_p = _bias_table(rel_bias, 3 * ATTN_Q_TILE - 1)
    r_cache = k_cache.shape[1]
    tab_s = _bias_table(rel_bias, r_cache + ts - 1)

    zeros_pre = jnp.zeros((bp, SUBLANES, lw_c), F32)
    zeros_h = jnp.zeros((bp, 1, lw_c), F32)
    pre_s = jnp.pad(conv_state, ((0, 0), (SUBLANES - (CONV_WIDTH - 1), 0), (0, 0)))
    lru_w = (conv_w, cb2, wa_bd, wx_bd, b_a, b_x, lam2)
    qp, kp, vp, k32p, v32p, lru_p, tail_p, hl_p = _mixin(
        xp, mod_p,ln_mix2, w_in_bf, qn_t, kn_t, head_mean, zeros_pre, zeros_h, *lru_w, 1, m)
    qs, ks, vs, k32s, v32s, lru_s, tail_s, hl_s = _mixin(
        xs, mod_s,ln_mix2, w_in_bf, qn_t, kn_t, head_mean, pre_s, lru_state[:, None, :], *lru_w, bs, ts)
    attn_p = _attn_prompt(qp, kp, vp, tab_p)
    attn_s = _attn_step(qs, ks, vs, jnp.transpose(k_cache, (0, 2, 3, 1)), jnp.transpose(v_cache, (0, 2, 3, 1)),
                        tab_s, ATTN_STEP_BATCH)

    n_tiles = bp * (s // m) + 1
    x1p, h2, route, cnt = _outproj(attn_p, lru_p, xp, mod_p,ln_ffn2, w_out_bf, wr_t, br,
                                   1, m, n_tiles, 0, None)
    x1s, h2, route, cnt = _outproj(attn_s, lru_s, xs, mod_s,ln_ffn2, w_out_bf, wr_t, br,
                                   bs, ts, n_tiles, n_tiles - 1, (h2, route, cnt))

    n_chunks, _, _, n_rows = _table_sizes(n_tiles)
    assert TOP_K * m * n_tiles >= EXPERT_UNIT_BLOCKS * EXPERT_ROWS
    dispatch_tab, combine_tab, toff_b, row0, nblk, ubase, limit = _route_tables(cnt[:, :, 0].astype(jnp.int32))
    xs_sorted = _dispatch(dispatch_tab, h2, route, toff_b, n_rows, n_chunks)
    ys_sorted = _experts(row0, nblk, ubase, limit, xs_sorted, w_gu, b_gu, w_dn, b_dn)
    yp = _combine(combine_tab, ys_sorted, route, toff_b, x1p, mod_p, 1, m, 0, n_chunks)
    ysm = _combine(combine_tab, ys_sorted, route, toff_b, x1s, mod_s, bs, ts, n_tiles - 1, n_chunks)

    keep = k32p.shape[1]
    new = (k32p.reshape(bp, keep, nh, HEAD_DIM), v32p.reshape(bp, keep, nh, HEAD_DIM),
           tail_p[:, SUBLANES - (CONV_WIDTH - 1):, :], hl_p[:, 0, :],
           k32s.reshape(bs, ts, nh, HEAD_DIM), v32s.reshape(bs, ts, nh, HEAD_DIM),
           tail_s[:, SUBLANES - (CONV_WIDTH - 1):, :], hl_s[:, 0, :])
    return yp, ysm, new


def kernel(x_prompt, x_sample, c_prompt, c_sample, cache_k, cache_v, state_conv, state_lru, ln_mix_w, ln_ffn_w, w_ada, b_ada, w_in, q_norm_w, k_norm_w, rel_bias, conv_w, conv_b, w_rgate, b_rgate, w_igate, b_igate, lru_lambda, w_out, w_router, b_router, w_gate_up, b_gate_up, w_down, b_down):
    depth = w_in.shape[0]
    yp, ys = x_prompt, x_sample
    collected = [[] for _ in range(8)]
    for l in range(depth):
        mod_p, mod_s = _ada(c_prompt, c_sample, w_ada[l], b_ada[l])
        lw = (ln_mix_w[l], ln_ffn_w[l], w_in[l], q_norm_w[l], k_norm_w[l], rel_bias[l], conv_w[l], conv_b[l],
              w_rgate[l], b_rgate[l], w_igate[l], b_igate[l], lru_lambda[l], w_out[l], w_router[l], b_router[l],
              w_gate_up[l], b_gate_up[l], w_down[l], b_down[l])
        yp, ys, new = _layer(yp, ys, mod_p, mod_s, cache_k[l], cache_v[l], state_conv[l], state_lru[l], lw)
        for acc, val in zip(collected, new):
            acc.append(val)
    return (yp, ys) + tuple(jnp.stack(vals) for vals in collected)
```

```python
import functools

import jax
import jax.numpy as jnp
from jax import lax
from jax.experimental import pallas as pl
from jax.experimental.pallas import tpu as pltpu

F32 = jnp.float32
BF16 = jnp.bfloat16

CHUNK = 64
N_LEFT_CHUNKS = 8
ATTN_WINDOW = N_LEFT_CHUNKS * CHUNK
HEAD_DIM = 64
REL_CLIP = 128
CONV_WIDTH = 4
LRU_C = 8.0
N_EXPERTS = 32
TOP_K = 4
SWIGLU_LIMIT = 7.0
SWIGLU_ALPHA = 1.702
NORM_EPS = 1e-6
NEG_INF = -1e30

LANES = 128
SUBLANES = 8
BF16_ROWS = 16
MXU_DIM = 256

TOKEN_TILE = 512
ATTN_Q_TILE = 256
ATTN_STEP_BATCH = 4
EXPERT_ROWS = 256
EXPERT_UNIT_BLOCKS = 4
BIAS_TABLE = 1024
VMEM_LIMIT = 32 * 1024 * 1024
EXPERT_VMEM_LIMIT = 48 * 1024 * 1024


def _params(sem, vmem=VMEM_LIMIT):
    return pltpu.CompilerParams(dimension_semantics=sem, vmem_limit_bytes=vmem)


def _ada_kernel(cp_ref, cs_ref, w_ref, b_ref, op_ref, os_ref):
    w = w_ref[...].astype(BF16)
    for c_ref, o_ref in ((cp_ref, op_ref), (cs_ref, os_ref)):
        c = c_ref[...]
        s = (c * jax.nn.sigmoid(c)).astype(BF16)
        o_ref[...] = jnp.dot(s, w, preferred_element_type=F32) + b_ref[...]


def _ada(c_p, c_s, w_ada, b_ada):
    (n_p, d), n_s = c_p.shape, c_s.shape[0]
    nout = w_ada.shape[1]
    tn = 2048
    mod_p, mod_s = pl.pallas_call(
        _ada_kernel,
        grid=(nout // tn,),
        in_specs=[pl.BlockSpec((n_p, d), lambda j: (0, 0)), pl.BlockSpec((n_s, d), lambda j: (0, 0)),
                  pl.BlockSpec((d, tn), lambda j: (0, j)),
                  pl.BlockSpec((1, tn), lambda j: (0, j))],
        out_specs=[pl.BlockSpec((n_p, tn), lambda j: (0, j)), pl.BlockSpec((n_s, tn), lambda j: (0, j))],
        out_shape=[jax.ShapeDtypeStruct((n_p, nout), F32), jax.ShapeDtypeStruct((n_s, nout), F32)],
        compiler_params=_params(("arbitrary",)),
        name="ada",
    )(c_p, c_s, w_ada, b_ada.reshape(1, nout))
    return mod_p.reshape(n_p, 1, nout), mod_s.reshape(n_s, 1, nout)


MOD_SHIFT_MIX, MOD_SCALE_MIX, MOD_GATE_MIX, MOD_SHIFT_FFN, MOD_SCALE_FFN, MOD_GATE_FFN = range(6)


def _mod_spec(nb, d, term):
    return pl.BlockSpec((nb, 1, d), lambda b, i: (b, 0, term))


def _mixin_kernel(x_ref, sh_ref, sc_ref, ln_ref, win_ref, qn_ref, kn_ref, bd_ref,
                  pre_ref, h0_ref, cw_ref, cb_ref, wa_ref, wx_ref, ba_ref, bx_ref, lam_ref,
                  q_ref, k_ref, v_ref, k32_ref, v32_ref, lru_ref, tail_ref, hl_ref, cx_ref, ch_ref):
    nb, tr, d = x_ref.shape
    m = nb * tr
    aw = q_ref.shape[-1]

    @pl.when(pl.program_id(1) == 0)
    def _():
        cx_ref[...] = pre_ref[...]
        ch_ref[...] = h0_ref[...]

    x = x_ref[...]
    ms = jnp.mean(x * x, axis=-1, keepdims=True)
    h = x * lax.rsqrt(ms + NORM_EPS) * (ln_ref[...] * (1.0 + sc_ref[...])) + sh_ref[...]
    hb = h.reshape(m, d).astype(BF16)

    def proj(part):
        return jnp.dot(hb, win_ref[:, part * aw:(part + 1) * aw], preferred_element_type=F32)

    def head_norm(t, w_ref):
        msq = jnp.dot((t * t).astype(BF16), bd_ref[...], preferred_element_type=F32)
        return t * lax.rsqrt(msq + NORM_EPS) * w_ref[...]

    lru_out, new_tail, h_last = _lru_branch(
        proj(3).reshape(nb, tr, aw), proj(4).reshape(nb, tr, aw), cw_ref, cb_ref,
        wa_ref, wx_ref, ba_ref, bx_ref, lam_ref, cx_ref, ch_ref)
    lru_ref[...] = lru_out
    tail_ref[...] = new_tail
    hl_ref[...] = h_last
    q = head_norm(proj(0), qn_ref)
    k = head_norm(proj(1), kn_ref)
    v = proj(2)
    q_ref[...] = q.astype(BF16).reshape(nb, tr, aw)
    k_ref[...] = k.astype(BF16).reshape(nb, tr, aw)
    v_ref[...] = v.astype(BF16).reshape(nb, tr, aw)
    k32_ref[...] = k.reshape(nb, tr, aw)
    v32_ref[...] = v.reshape(nb, tr, aw)


def _mixin(x, mod, ln_w, w_in_bf, qn_t, kn_t, bd, pre, h0, conv_w, conv_b, wa_bd, wx_bd, b_a, b_x, lam,
           nb, tr):
    nbt, t, d = x.shape
    aw = qn_t.shape[-1]
    c = pre.shape[-1]
    assert c == aw
    keep = min(ATTN_WINDOW, t)
    assert tr == keep or t == tr
    grid = (nbt // nb, t // tr)
    xmap = lambda b, i: (b, i, 0)
    mmap = lambda b, i: (b, 0, 0)
    cmap = lambda b, i: (0, 0)
    cmap3 = lambda b, i: (0, 0, 0)
    tmap = lambda b, i: (b, 0, 0)
    big = pl.BlockSpec((nb, tr, aw), xmap)
    tail = pl.BlockSpec((nb, keep, aw), tmap)
    row = pl.BlockSpec((1, c), cmap)
    return pl.pallas_call(
        _mixin_kernel,
        grid=grid,
        in_specs=[pl.BlockSpec((nb, tr, d), xmap),
                  _mod_spec(nb, d, MOD_SHIFT_MIX), _mod_spec(nb, d, MOD_SCALE_MIX),
                  pl.BlockSpec((1, d), cmap),
                  pl.BlockSpec(w_in_bf.shape, cmap),
                  pl.BlockSpec((1, aw), cmap), pl.BlockSpec((1, aw), cmap),
                  pl.BlockSpec(bd.shape, cmap),
                  pl.BlockSpec((nb, SUBLANES, c), tmap), pl.BlockSpec((nb, 1, c), tmap),
                  pl.BlockSpec(conv_w.shape, cmap), row,
                  pl.BlockSpec(wa_bd.shape, cmap3), pl.BlockSpec(wx_bd.shape, cmap3),
                  row, row, row],
        out_specs=[big, big, big, tail, tail, big,
                   pl.BlockSpec((nb, SUBLANES, c), tmap), pl.BlockSpec((nb, 1, c), tmap)],
        out_shape=[jax.ShapeDtypeStruct((nbt, t, aw), BF16)] * 3
        + [jax.ShapeDtypeStruct((nbt, keep, aw), F32)] * 2
        + [jax.ShapeDtypeStruct((nbt, t, c), BF16),
           jax.ShapeDtypeStruct((nbt, SUBLANES, c), F32),
           jax.ShapeDtypeStruct((nbt, 1, c), F32)],
        scratch_shapes=[pltpu.VMEM((nb, SUBLANES, c), F32), pltpu.VMEM((nb, 1, c), F32)],
        compiler_params=_params(("arbitrary", "arbitrary")),
        name="mixin",
    )(x, mod, mod, ln_w, w_in_bf, qn_t, kn_t, bd, pre, h0, conv_w, conv_b, wa_bd, wx_bd, b_a, b_x, lam)


def _bias_table(rel_bias, off):
    h = rel_bias.shape[0]
    left = off - REL_CLIP
    right = BIAS_TABLE - left - (2 * REL_CLIP + 1)
    assert left >= 0 and right >= 0
    return jnp.concatenate([jnp.broadcast_to(rel_bias[:, :1], (h, left)), rel_bias,
                            jnp.broadcast_to(rel_bias[:, -1:], (h, right))], axis=1)


def _toeplitz(tab_row, rows, cols):
    t = jnp.broadcast_to(tab_row, (rows, BIAS_TABLE))
    t = pltpu.roll(t, BIAS_TABLE - (rows - 1), 1, stride=1, stride_axis=0)
    return t[:, :cols]


def _attn_kernel(q_ref, k0_ref, k1_ref, k2_ref, v0_ref, v1_ref, v2_ref, tab_ref, o_ref, bias_ref):
    b = pl.program_id(0)
    s = pl.program_id(1)
    qt = q_ref.shape[1]
    nk = 3 * qt
    nh = bias_ref.shape[0]

    @pl.when((b == 0) & (s == 0))
    def _():
        qi = lax.broadcasted_iota(jnp.int32, (qt, nk), 0) // CHUNK
        kc = lax.broadcasted_iota(jnp.int32, (qt, nk), 1) // CHUNK
        for h in range(nh):
            band = jnp.where(kc <= qi + N_LEFT_CHUNKS, _toeplitz(tab_ref[h:h + 1, :], qt, nk), NEG_INF)
            bias_ref[h] = jnp.where(kc >= qi, band, NEG_INF)

    pair_w = 2 * HEAD_DIM

    def attend(mask_start):
        q = q_ref[0]
        kcat = jnp.concatenate([k0_ref[0], k1_ref[0], k2_ref[0]], axis=0)
        vcat = jnp.concatenate([v0_ref[0], v1_ref[0], v2_ref[0]], axis=0)
        first = lax.broadcasted_iota(jnp.int32, (qt, pair_w), 1) < HEAD_DIM
        keep = [jnp.where(first, 1.0, 0.0).astype(BF16), jnp.where(first, 0.0, 1.0).astype(BF16)]
        if mask_start:
            in_seq = lax.broadcasted_iota(jnp.int32, (qt, nk), 1) >= (2 - s) * qt
        outs = []
        for pair in range(nh // 2):
            sl = slice(pair * pair_w, (pair + 1) * pair_w)
            q2, k2, v2 = q[:, sl], kcat[:, sl], vcat[:, sl]
            per_head = []
            for sub in range(2):
                sc = lax.dot_general(q2 * keep[sub], k2, (((1,), (1,)), ((), ())), preferred_element_type=F32)
                sc = sc + bias_ref[2 * pair + sub]
                if mask_start:
                    sc = jnp.where(in_seq, sc, NEG_INF)
                mx = jnp.max(sc, axis=-1, keepdims=True)
                p = jnp.exp(sc - mx)
                l = jnp.sum(p, axis=-1, keepdims=True)
                per_head.append(jnp.dot(p.astype(BF16), v2, preferred_element_type=F32) / l)
            outs.append(jnp.where(first, per_head[0], per_head[1]))
        o_ref[0] = jnp.concatenate(outs, axis=-1).astype(BF16)

    @pl.when(s < 2)
    def _():
        attend(True)

    @pl.when(s >= 2)
    def _():
        attend(False)


def _attn_prompt(q, k, v, tab):
    b, s, aw = q.shape
    qt = ATTN_Q_TILE
    nh = aw // HEAD_DIM
    qspec = pl.BlockSpec((1, qt, aw), lambda i, j: (i, j, 0))

    def kspec(back):
        return pl.BlockSpec((1, qt, aw), lambda i, j: (i, jnp.maximum(j - back, 0), 0))

    return pl.pallas_call(
        _attn_kernel,
        grid=(b, s // qt),
        in_specs=[qspec, kspec(2), kspec(1), kspec(0), kspec(2), kspec(1), kspec(0),
                  pl.BlockSpec(tab.shape, lambda i, j: (0, 0))],
        out_specs=qspec,
        out_shape=jax.ShapeDtypeStruct((b, s, aw), BF16),
        scratch_shapes=[pltpu.VMEM((nh, qt, 3 * qt), F32)],
        compiler_params=_params(("arbitrary", "arbitrary")),
        name="attn_prompt",
    )(q, k, k, k, v, v, v, tab)


def _attn_step_kernel(q_ref, kn_ref, vn_ref, ck_ref, cv_ref, tab_ref, o_ref, bias_ref):
    step = pl.program_id(0)
    nbs, t, aw = q_ref.shape
    nh = aw // HEAD_DIM
    rows = nh * t
    r = ck_ref.shape[-1]
    nk = r + LANES
    nt_dims = (((1,), (1,)), ((), ()))

    @pl.when(step == 0)
    def _():
        ok = lax.broadcasted_iota(jnp.int32, (t, nk), 1) < r + t
        for h in range(nh):
            bias_ref[h * t:(h + 1) * t, :] = jnp.where(ok, _toeplitz(tab_ref[h:h + 1, :], t, nk), NEG_INF)

    own = (lax.broadcasted_iota(jnp.int32, (rows, aw), 0) // t
           == lax.broadcasted_iota(jnp.int32, (rows, aw), 1) // HEAD_DIM)
    own_f = jnp.where(own, 1.0, 0.0)
    own_bf = own_f.astype(BF16)
    pad = jnp.zeros((LANES - t, aw), BF16)
    for b in range(nbs):
        q_bd = jnp.concatenate([q_ref[b]] * nh, axis=0) * own_bf
        k_old = ck_ref[b].reshape(aw, r).astype(BF16)
        v_old = cv_ref[b].reshape(aw, r).astype(BF16)
        k_new = jnp.concatenate([kn_ref[b], pad], axis=0)
        v_new = jnp.concatenate([vn_ref[b], pad], axis=0)
        s_old = jnp.dot(q_bd, k_old, preferred_element_type=F32) + bias_ref[:, :r]
        s_new = lax.dot_general(q_bd, k_new, nt_dims, preferred_element_type=F32) + bias_ref[:, r:]
        mx = jnp.maximum(jnp.max(s_old, axis=-1, keepdims=True), jnp.max(s_new, axis=-1, keepdims=True))
        p_old = jnp.exp(s_old - mx)
        p_new = jnp.exp(s_new - mx)
        l = jnp.sum(p_old, axis=-1, keepdims=True) + jnp.sum(p_new, axis=-1, keepdims=True)
        o_all = (lax.dot_general(p_old.astype(BF16), v_old, nt_dims, preferred_element_type=F32)
                 + jnp.dot(p_new.astype(BF16), v_new, preferred_element_type=F32))
        o_all = o_all * own_f / l
        out = o_all[0:t]
        for h in range(1, nh):
            out = out + o_all[h * t:(h + 1) * t]
        o_ref[b] = out.astype(BF16)


def _attn_step(q, kn, vn, ck, cv, tab, nbs):
    b, t, aw = q.shape
    nh = aw // HEAD_DIM
    r = ck.shape[-1]
    new = pl.BlockSpec((nbs, t, aw), lambda i: (i, 0, 0))
    old = pl.BlockSpec((nbs, nh, HEAD_DIM, r), lambda i: (i, 0, 0, 0))
    return pl.pallas_call(
        _attn_step_kernel,
        grid=(b // nbs,),
        in_specs=[new, new, new, old, old, pl.BlockSpec(tab.shape, lambda i: (0, 0))],
        out_specs=new,
        out_shape=jax.ShapeDtypeStruct((b, t, aw), BF16),
        scratch_shapes=[pltpu.VMEM((nh * t, r + LANES), F32)],
        compiler_params=_params(("arbitrary",)),
        name="attn_step",
    )(q, kn, vn, ck, cv, tab)


def _gelu_tanh(x):
    return x * (0.5 * (1.0 + jnp.tanh(0.7978845608028654 * (x + 0.044715 * (x * x * x)))))


def _lru_branch(x, yg, cw_ref, cb_ref, wa_ref, wx_ref, ba_ref, bx_ref, lam_ref, cx_ref, ch_ref):
    nb, tr, c = x.shape
    m = nb * tr
    half = c // 2
    xp = jnp.concatenate([cx_ref[...], x], axis=1)
    new_tail = xp[:, tr:tr + SUBLANES, :]
    groups = tr // SUBLANES
    xg = xp.reshape(nb * (groups + 1), SUBLANES, c)
    first_rows = lax.broadcasted_iota(jnp.int32, (nb, groups, SUBLANES, c), 2)
    y = cb_ref[...] + cw_ref[CONV_WIDTH - 1:CONV_WIDTH, :] * x
    for back in range(1, CONV_WIDTH):
        rot = pltpu.roll(xg, back, 1).reshape(nb, groups + 1, SUBLANES, c)
        shifted = jnp.where(first_rows >= back, rot[:, 1:], rot[:, :groups]).reshape(nb, tr, c)
        y = y + cw_ref[CONV_WIDTH - 1 - back:CONV_WIDTH - back, :] * shifted
    y2 = y.reshape(m, c)
    yb = y2.astype(BF16)

    def gate(w_ref, b_ref):
        g = jnp.concatenate(
            [jnp.dot(yb[:, :half], w_ref[0], preferred_element_type=F32),
             jnp.dot(yb[:, half:], w_ref[1], preferred_element_type=F32)], axis=1)
        return jax.nn.sigmoid(g + b_ref[...])

    rg = gate(wa_ref, ba_ref)
    ig = gate(wx_ref, bx_ref)
    lam = lam_ref[...]
    log_sig = jnp.minimum(lam, 0.0) - jnp.log1p(jnp.exp(-jnp.abs(lam)))
    log_a = rg * (LRU_C * log_sig)
    a_cum = jnp.exp(log_a)
    b_cum = jnp.sqrt(-jnp.tanh(log_a) * (a_cum * a_cum + 1.0)) * (ig * y2)
    a_cum = a_cum.reshape(nb * groups, SUBLANES, c)
    b_cum = b_cum.reshape(nb * groups, SUBLANES, c)
    row = lax.broadcasted_iota(jnp.int32, a_cum.shape, 1)
    dist = 1
    while dist < SUBLANES:
        keep = row >= dist
        a_sh = jnp.where(keep, pltpu.roll(a_cum, dist, 1), 1.0)
        b_sh = jnp.where(keep, pltpu.roll(b_cum, dist, 1), 0.0)
        b_cum = a_cum * b_sh + b_cum
        a_cum = a_cum * a_sh
        dist *= 2
    a_grp = a_cum.reshape(nb, groups, SUBLANES, c)
    b_grp = b_cum.reshape(nb, groups, SUBLANES, c)
    carry = ch_ref[...]
    pieces = []
    for grp in range(groups):
        h_grp = a_grp[:, grp] * carry + b_grp[:, grp]
        carry = h_grp[:, SUBLANES - 1:SUBLANES, :]
        pieces.append(h_grp)
    h = jnp.concatenate(pieces, axis=1)
    ch_ref[...] = carry
    cx_ref[...] = new_tail
    return (h * _gelu_tanh(yg)).astype(BF16), new_tail, carry


def _outproj_kernel(*refs, aliased):
    (at_ref, lr_ref, x_ref, gm_ref, shf_ref, scf_ref, lnf_ref, wo_ref, wr_ref, br_ref) = refs[:10]
    x1_ref, h2_ref, route_ref, cnt_ref = refs[10 + aliased:]
    nb, tr, d = x_ref.shape
    m = nb * tr
    aw = at_ref.shape[-1]
    ne = wr_ref.shape[0]
    at = at_ref[...].reshape(m, aw)
    lr = lr_ref[...].reshape(m, aw)
    mix = (jnp.dot(at, wo_ref[0:aw, :], preferred_element_type=F32)
           + jnp.dot(lr, wo_ref[aw:2 * aw, :], preferred_element_type=F32))
    x1 = x_ref[...] + gm_ref[...] * mix.reshape(nb, tr, d)
    x1_ref[...] = x1
    ms = jnp.mean(x1 * x1, axis=-1, keepdims=True)
    h2 = (x1 * lax.rsqrt(ms + NORM_EPS) * (lnf_ref[...] * (1.0 + scf_ref[...])) + shf_ref[...]).reshape(m, d)
    h2_ref[...] = h2.astype(BF16)

    logits = lax.dot_general(wr_ref[...], h2.astype(BF16), (((1,), (1,)), ((), ())),
                             preferred_element_type=F32) + br_ref[...]
    e_iota = lax.broadcasted_iota(jnp.int32, (ne, m), 0).astype(F32)
    vals = logits
    top_v, sels = [], []
    for k in range(TOP_K):
        mx = jnp.max(vals, axis=0, keepdims=True)
        idx = jnp.min(jnp.where(vals == mx, e_iota, float(ne)), axis=0, keepdims=True)
        sel = e_iota == idx
        vals = jnp.where(sel, -jnp.inf, vals)
        top_v.append(mx)
        sels.append(sel)
        route_ref[0, k:k + 1, :] = idx
    ex = [jnp.exp(v - top_v[0]) for v in top_v]
    den = ex[0] + ex[1] + ex[2] + ex[3]
    chosen = jnp.zeros((ne, m), F32)
    for k in range(TOP_K):
        route_ref[0, 2 * TOP_K + k:2 * TOP_K + k + 1, :] = ex[k] / den
        chosen = chosen + jnp.where(sels[k], 1.0, 0.0)
    before = (lax.broadcasted_iota(jnp.int32, (m, m), 0) < lax.broadcasted_iota(jnp.int32, (m, m), 1))
    rank = jnp.dot(chosen.astype(BF16), jnp.where(before, 1.0, 0.0).astype(BF16), preferred_element_type=F32)
    for k in range(TOP_K):
        route_ref[0, TOP_K + k:TOP_K + k + 1, :] = jnp.sum(jnp.where(sels[k], rank, 0.0), axis=0, keepdims=True)
    route_ref[0, 3 * TOP_K:4 * TOP_K, :] = jnp.zeros((TOP_K, m), F32)
    cnt_ref[0] = jnp.broadcast_to(jnp.sum(chosen, axis=1, keepdims=True), (ne, LANES))


def _outproj(attn, lru_o, x, mod, lnf, w_out_bf, wr_t, br, nb, tr, n_tiles, tile0, prev):
    nbt, t, d = x.shape
    aw = attn.shape[-1]
    m = nb * tr
    assert m == TOKEN_TILE
    ne = wr_t.shape[0]
    tiles_per_seq = t // tr
    xmap = lambda b, i: (b, i, 0)
    c2 = lambda b, i: (0, 0)
    tile = lambda b, i: (tile0 + b * tiles_per_seq + i, 0)
    tile3 = lambda b, i: (tile0 + b * tiles_per_seq + i, 0, 0)
    in_specs = [pl.BlockSpec((nb, tr, aw), xmap), pl.BlockSpec((nb, tr, aw), xmap),
                pl.BlockSpec((nb, tr, d), xmap),
                _mod_spec(nb, d, MOD_GATE_MIX), _mod_spec(nb, d, MOD_SHIFT_FFN), _mod_spec(nb, d, MOD_SCALE_FFN),
                pl.BlockSpec((1, d), c2), pl.BlockSpec(w_out_bf.shape, c2),
                pl.BlockSpec(wr_t.shape, c2), pl.BlockSpec((ne, 1), c2)]
    args = [attn, lru_o, x, mod, mod, mod, lnf, w_out_bf, wr_t, br]
    aliases = {}
    if prev is not None:
        in_specs += [pl.BlockSpec(memory_space=pl.ANY)] * 3
        args += list(prev)
        aliases = {10: 1, 11: 2, 12: 3}
    return pl.pallas_call(
        functools.partial(_outproj_kernel, aliased=len(aliases)),
        grid=(nbt // nb, tiles_per_seq),
        in_specs=in_specs,
        out_specs=[pl.BlockSpec((nb, tr, d), xmap), pl.BlockSpec((m, d), tile),
                   pl.BlockSpec((1, 4 * TOP_K, m), tile3), pl.BlockSpec((1, ne, LANES), tile3)],
        out_shape=[jax.ShapeDtypeStruct((nbt, t, d), F32),
                   jax.ShapeDtypeStruct((n_tiles * m, d), BF16),
                   jax.ShapeDtypeStruct((n_tiles, 4 * TOP_K, m), F32),
                   jax.ShapeDtypeStruct((n_tiles, ne, LANES), F32)],
        input_output_aliases=aliases,
        compiler_params=_params(("arbitrary", "arbitrary")),
        name="outproj",
    )(*args)


def _tile_rows(m):
    cap = TOP_K * m + N_EXPERTS * (BF16_ROWS - 1) + BF16_ROWS
    return -(-cap // TOKEN_TILE) * TOKEN_TILE


def _table_sizes(nt):
    g = BF16_ROWS
    m = TOKEN_TILE
    n_chunks = _tile_rows(m) // g
    n_gap = -(-(N_EXPERTS * (EXPERT_ROWS // g - 1)) // nt)
    bound = TOP_K * m * nt + nt * N_EXPERTS * (g - 1) + N_EXPERTS * (EXPERT_ROWS - g)
    n_sorted = -(-bound // EXPERT_ROWS) * EXPERT_ROWS
    return n_chunks, n_gap, n_sorted, n_sorted + 2 * (n_chunks + n_gap) * g


def _route_tables(cnt):
    nt = cnt.shape[0]
    g = BF16_ROWS
    bm = EXPERT_ROWS
    n_chunks, n_gap, n_sorted, _ = _table_sizes(nt)
    e_ids = jnp.arange(N_EXPERTS, dtype=jnp.int32)
    t_ids = jnp.arange(nt, dtype=jnp.int32)
    upto = (e_ids[:, None] <= e_ids[None, :]).astype(jnp.int32)
    pc = (cnt + g - 1) // g * g
    ctile = jnp.sum(pc[:, :, None] * upto[None], axis=1)
    toff = ctile - pc
    trow = ctile[:, -1]
    tot = jnp.sum(pc, axis=0)
    reg = (tot + bm - 1) // bm * bm
    creg = jnp.sum(reg[:, None] * upto, axis=0)
    base = creg - reg
    earlier = (t_ids[:, None] < t_ids[None, :]).astype(jnp.int32)
    goff = base[None, :] + jnp.sum(pc[:, None, :] * earlier[:, :, None], axis=0)
    r = jnp.arange(n_chunks, dtype=jnp.int32) * g
    r3 = r[None, :, None]
    in_seg = (toff[:, None, :] <= r3) & (r3 < ctile[:, None, :])
    dst = jnp.sum(jnp.where(in_seg, (goff - toff)[:, None, :], 0), axis=2) + r[None, :]
    dst = jnp.where(r[None, :] < trow[:, None], dst, -1)
    gcnt = (reg - tot) // g
    gcum = jnp.sum(gcnt[:, None] * upto, axis=0)
    gstart = gcum - gcnt
    s = jnp.arange(nt * n_gap, dtype=jnp.int32)
    in_gap = (gstart[None, :] <= s[:, None]) & (s[:, None] < gcum[None, :])
    gdst = jnp.sum(jnp.where(in_gap, (base + tot - g * gstart)[None, :] + g * s[:, None], 0), axis=1)
    gdst = jnp.where(s < gcum[-1], gdst, -1).reshape(nt, n_gap)
    table = jnp.concatenate([dst, gdst], axis=1).astype(jnp.int32)
    n_entries = n_chunks + n_gap
    spare = n_sorted + ((t_ids % 2)[:, None] * n_entries + jnp.arange(n_entries, dtype=jnp.int32)[None, :]) * g
    dispatch_tab = jnp.where(table >= 0, table, spare).astype(jnp.int32)
    combine_tab = jnp.maximum(dst, 0).astype(jnp.int32)
    toff_b = jnp.broadcast_to(toff.astype(F32)[:, :, None], (nt, N_EXPERTS, LANES))
    limit = (creg[-1] - EXPERT_UNIT_BLOCKS * bm).astype(jnp.int32).reshape(1)
    nblk = reg // bm
    units = (nblk + EXPERT_UNIT_BLOCKS - 1) // EXPERT_UNIT_BLOCKS
    ubase = jnp.sum(units[:, None] * upto, axis=0) - units
    return (dispatch_tab, combine_tab, toff_b, base.astype(jnp.int32), nblk.astype(jnp.int32),
            ubase.astype(jnp.int32), limit)


def _slot_rows(route_ref, toff_ref, m):
    ne = toff_ref.shape[1]
    e_iota = lax.broadcasted_iota(jnp.int32, (ne, m), 0).astype(F32)
    toff_col = toff_ref[0][:, 0:1]
    pos = []
    for k in range(TOP_K):
        sel = e_iota == route_ref[0, k:k + 1, :]
        start = jnp.sum(jnp.where(sel, toff_col, 0.0), axis=0, keepdims=True)
        pos.append(start + route_ref[0, TOP_K + k:TOP_K + k + 1, :])
    return pos


def _dispatch_kernel(tab_ref, h2_ref, route_ref, toff_ref, xs_hbm, buf_ref, sem, *, n_chunks, n_tiles):
    t = pl.program_id(0)
    slot = t % 2
    m = h2_ref.shape[0]
    rows = buf_ref.shape[1]
    n_entries = tab_ref.shape[1]
    g = BF16_ROWS
    per_chunk = m // g

    def start(c):
        src = c * g if c < n_chunks else rows - g
        pltpu.make_async_copy(
            buf_ref.at[slot, pl.ds(src, g)],
            xs_hbm.at[pl.ds(pl.multiple_of(tab_ref[t, c], g), g)], sem.at[slot]).start()

    def wait_all(which):
        for _ in range(n_entries):
            pltpu.make_async_copy(buf_ref.at[which, pl.ds(0, g)], xs_hbm.at[pl.ds(0, g)], sem.at[which]).wait()

    @pl.when(t >= 2)
    def _():
        wait_all(slot)

    pos = _slot_rows(route_ref, toff_ref, m)
    h2 = h2_ref[...]
    for rc in range(rows // m):
        r_iota = (lax.broadcasted_iota(jnp.int32, (m, m), 0) + rc * m).astype(F32)
        onehot = jnp.zeros((m, m), F32)
        for k in range(TOP_K):
            onehot = jnp.where(r_iota == pos[k], 1.0, onehot)
        buf_ref[slot, rc * m:(rc + 1) * m, :] = jnp.dot(
            onehot.astype(BF16), h2, preferred_element_type=F32).astype(BF16)
        for c in range((rc - 1) * per_chunk, rc * per_chunk) if rc > 0 else ():
            start(c)
    for c in range(n_chunks - per_chunk, n_entries):
        start(c)

    @pl.when(t == n_tiles - 1)
    def _():
        if n_tiles > 1:
            wait_all(1 - slot)
        wait_all(slot)


def _dispatch(table, h2, route, toff_b, n_rows, n_chunks):
    nt = route.shape[0]
    m = TOKEN_TILE
    d = h2.shape[1]
    rows = _tile_rows(m)
    grid_spec = pltpu.PrefetchScalarGridSpec(
        num_scalar_prefetch=1,
        grid=(nt,),
        in_specs=[pl.BlockSpec((m, d), lambda t, tab: (t, 0)),
                  pl.BlockSpec((1, 4 * TOP_K, m), lambda t, tab: (t, 0, 0)),
                  pl.BlockSpec((1, N_EXPERTS, LANES), lambda t, tab: (t, 0, 0))],
        out_specs=pl.BlockSpec(memory_space=pl.ANY),
        scratch_shapes=[pltpu.VMEM((2, rows, d), BF16), pltpu.SemaphoreType.DMA((2,))],
    )
    return pl.pallas_call(
        functools.partial(_dispatch_kernel, n_chunks=n_chunks, n_tiles=nt),
        grid_spec=grid_spec,
        out_shape=jax.ShapeDtypeStruct((n_rows, d), BF16),
        compiler_params=_params(("arbitrary",)),
        name="dispatch",
    )(table, h2, route, toff_b)


def _expert_kernel(row0_ref, nblk_ref, ubase_ref, limit_ref, xs_hbm, wgu_ref, bgu_ref, wdn_ref, bdn_ref, ys_hbm,
                   wgu_bf, wdn_bf, xbuf, ybuf, sem_in, sem_out, pend_ref):
    e = pl.program_id(0)
    ne = pl.num_programs(0)
    bm = EXPERT_ROWS
    unit = xbuf.shape[1]
    per_unit = unit // bm
    dff = wdn_ref.shape[1]
    nblk = nblk_ref[e]
    n_units = (nblk + per_unit - 1) // per_unit
    base = ubase_ref[e]

    def unit_start(expert, s):
        true = row0_ref[expert] + s * unit
        start = jnp.minimum(true, limit_ref[0])
        return pl.multiple_of(start, bm), pl.multiple_of(true - start, bm)

    def in_copy(expert, s, slot):
        start, _ = unit_start(expert, s)
        return pltpu.make_async_copy(xs_hbm.at[pl.ds(start, unit)], xbuf.at[slot], sem_in.at[slot])

    def out_copy(j, slot, i):
        start = pl.multiple_of(row0_ref[e] + j * bm, bm)
        return pltpu.make_async_copy(ybuf.at[slot, pl.ds(i * bm, bm)], ys_hbm.at[pl.ds(start, bm)],
                                     sem_out.at[slot])

    def wait_pending(slot):
        count = pend_ref[slot]

        @pl.when(count == per_unit)
        def _():
            for i in range(per_unit):
                out_copy(0, slot, i).wait()

        for i in range(per_unit - 1):
            @pl.when((count < per_unit) & (i < count))
            def _():
                out_copy(0, slot, i).wait()
        pend_ref[slot] = 0

    @pl.when(e == 0)
    def _():
        pend_ref[0] = 0
        pend_ref[1] = 0

    @pl.when((e == 0) & (nblk > 0))
    def _():
        in_copy(e, 0, 0).start()

    wgu_bf[...] = wgu_ref[0].astype(BF16)
    wdn_bf[...] = wdn_ref[0].astype(BF16)

    def run_unit(s, carry):
        slot = (base + s) % 2
        in_copy(e, s, slot).wait()

        @pl.when(s + 1 < n_units)
        def _():
            in_copy(e, s + 1, 1 - slot).start()

        wait_pending(slot)
        _, lead = unit_start(e, s)

        def ffn(first, count):
            rows = count * bm
            x = xbuf[slot, pl.ds(pl.multiple_of(lead + first * bm, bm), rows), :]
            gu = jnp.dot(x, wgu_bf[...], preferred_element_type=F32) + bgu_ref[0]
            gate = jnp.minimum(gu[:, :dff], SWIGLU_LIMIT)
            up = jnp.clip(gu[:, dff:], -SWIGLU_LIMIT, SWIGLU_LIMIT)
            glu = gate * jax.nn.sigmoid(gate * SWIGLU_ALPHA)
            act = ((up + 1.0) * glu).astype(BF16)
            ybuf[slot, first * bm:first * bm + rows, :] = (
                jnp.dot(act, wdn_bf[...], preferred_element_type=F32) + bdn_ref[0]).astype(BF16)
            for i in range(first, first + count):
                out_copy(s * per_unit + i, slot, i).start()

        for first in range(0, per_unit, 2):
            have = nblk - s * per_unit - first

            @pl.when(have >= 2)
            def _():
                ffn(first, 2)

            @pl.when(have == 1)
            def _():
                ffn(first, 1)
        pend_ref[slot] = jnp.minimum(nblk - s * per_unit, per_unit)
        return carry

    lax.fori_loop(0, n_units, run_unit, 0)

    nxt = jnp.minimum(e + 1, ne - 1)

    @pl.when((e + 1 < ne) & (nblk_ref[nxt] > 0))
    def _():
        in_copy(nxt, 0, (base + n_units) % 2).start()

    @pl.when(e == ne - 1)
    def _():
        wait_pending(0)
        wait_pending(1)


def _experts(row0, nblk, ubase, limit, xs, w_gu, b_gu, w_dn, b_dn):
    n_rows, d = xs.shape
    bm = EXPERT_ROWS
    unit = EXPERT_UNIT_BLOCKS * bm
    ne, _, dff2 = w_gu.shape
    dff = w_dn.shape[1]
    exp3 = lambda e, r0, nb, ub, lim: (e, 0, 0)
    grid_spec = pltpu.PrefetchScalarGridSpec(
        num_scalar_prefetch=4,
        grid=(ne,),
        in_specs=[pl.BlockSpec(memory_space=pl.ANY),
                  pl.BlockSpec((1, d, dff2), exp3), pl.BlockSpec((1, 1, dff2), exp3),
                  pl.BlockSpec((1, dff, d), exp3), pl.BlockSpec((1, 1, d), exp3)],
        out_specs=pl.BlockSpec(memory_space=pl.ANY),
        scratch_shapes=[pltpu.VMEM((d, dff2), BF16), pltpu.VMEM((dff, d), BF16),
                        pltpu.VMEM((2, unit, d), BF16), pltpu.VMEM((2, unit, d), BF16),
                        pltpu.SemaphoreType.DMA((2,)), pltpu.SemaphoreType.DMA((2,)),
                        pltpu.SMEM((2,), jnp.int32)],
    )
    return pl.pallas_call(
        _expert_kernel,
        grid_spec=grid_spec,
        out_shape=jax.ShapeDtypeStruct((n_rows, d), BF16),
        compiler_params=_params(("arbitrary",), EXPERT_VMEM_LIMIT),
        name="experts",
    )(row0, nblk, ubase, limit, xs, w_gu, b_gu.reshape(ne, 1, dff2), w_dn, b_dn.reshape(ne, 1, d))


def _combine_kernel(tab_ref, ys_hbm, route_ref, toff_ref, x1_ref, gf_ref, o_ref, buf_ref, sem, *, n_chunks, tile0):
    step = pl.program_id(0)
    n_steps = pl.num_programs(0)
    t = tile0 + step
    slot = step % 2
    nb, tr, d = x1_ref.shape
    m = nb * tr
    rows = buf_ref.shape[1]
    g = BF16_ROWS

    def fetch(tile, which):
        for c in range(n_chunks):
            pltpu.make_async_copy(
                ys_hbm.at[pl.ds(pl.multiple_of(tab_ref[tile, c], g), g)],
                buf_ref.at[which, pl.ds(c * g, g)], sem.at[which]).start()

    @pl.when(step == 0)
    def _():
        fetch(t, slot)

    @pl.when(step + 1 < n_steps)
    def _():
        fetch(t + 1, 1 - slot)

    pos = _slot_rows(route_ref, toff_ref, m)
    gates = [route_ref[0, 2 * TOP_K + k:2 * TOP_K + k + 1, :] for k in range(TOP_K)]
    stacked = jnp.concatenate(pos + gates + [jnp.zeros((LANES - 2 * TOP_K, m), F32)], axis=0)
    cols = stacked.T
    for _ in range(n_chunks):
        pltpu.make_async_copy(ys_hbm.at[pl.ds(0, g)], buf_ref.at[slot, pl.ds(0, g)], sem.at[slot]).wait()

    acc = jnp.zeros((m, d), F32)
    for rc in range(rows // m):
        c_iota = (lax.broadcasted_iota(jnp.int32, (m, m), 1) + rc * m).astype(F32)
        weights = jnp.zeros((m, m), F32)
        for k in range(TOP_K):
            weights = jnp.where(c_iota == cols[:, k:k + 1], cols[:, TOP_K + k:TOP_K + k + 1], weights)
        acc = acc + jnp.dot(weights.astype(BF16), buf_ref[slot, rc * m:(rc + 1) * m, :],
                            preferred_element_type=F32)
    o_ref[...] = x1_ref[...] + gf_ref[...] * acc.reshape(nb, tr, d)


def _combine(table, ys, route, toff_b, x1, gf, nb, tr, tile0, n_chunks):
    nbt, t, d = x1.shape
    m = nb * tr
    assert m == TOKEN_TILE
    rows = _tile_rows(m)
    tiles_per_seq = t // tr
    n_steps = (nbt // nb) * tiles_per_seq
    xmap = lambda s, tab: (s // tiles_per_seq, s % tiles_per_seq, 0)
    grid_spec = pltpu.PrefetchScalarGridSpec(
        num_scalar_prefetch=1,
        grid=(n_steps,),
        in_specs=[pl.BlockSpec(memory_space=pl.ANY),
                  pl.BlockSpec((1, 4 * TOP_K, m), lambda s, tab: (tile0 + s, 0, 0)),
                  pl.BlockSpec((1, N_EXPERTS, LANES), lambda s, tab: (tile0 + s, 0, 0)),
                  pl.BlockSpec((nb, tr, d), xmap),
                  pl.BlockSpec((nb, 1, d), lambda s, tab: (s // tiles_per_seq, 0, MOD_GATE_FFN))],
        out_specs=pl.BlockSpec((nb, tr, d), xmap),
        scratch_shapes=[pltpu.VMEM((2, rows, d), BF16), pltpu.SemaphoreType.DMA((2,))],
    )
    return pl.pallas_call(
        functools.partial(_combine_kernel, n_chunks=n_chunks, tile0=tile0),
        grid_spec=grid_spec,
        out_shape=jax.ShapeDtypeStruct((nbt, t, d), F32),
        compiler_params=_params(("arbitrary",)),
        name="combine",
    )(table, ys, route, toff_b, x1, gf)


def _block_diag(w, groups):
    n, k, _ = w.shape
    w = w.reshape(n // groups, groups, k, k)
    eye = jnp.eye(groups, dtype=w.dtype)
    return jnp.einsum("ngij,gh->ngihj", w, eye).reshape(n // groups, groups * k, groups * k)


def _layer(xp, xs, mod_p, mod_s, k_cache, v_cache, conv_state, lru_state, lw):
    (ln_mix, ln_ffn, w_in, q_norm, k_norm, rel_bias, conv_w, conv_b, w_rg, b_rg, w_ig, b_ig, lam,
     w_out, w_router, b_router, w_gu, b_gu, w_dn, b_dn) = lw
    bp, s, d = xp.shape
    bs, ts, _ = xs.shape
    aw = w_out.shape[0] // 2
    nh = aw // HEAD_DIM
    m = TOKEN_TILE
    assert s % m == 0 and bs * ts == m and s % ATTN_Q_TILE == 0

    w_in_bf = w_in.astype(BF16)
    w_out_bf = w_out.astype(BF16)
    qn_t = jnp.tile(q_norm * (HEAD_DIM ** -0.5), nh).reshape(1, aw)
    kn_t = jnp.tile(k_norm, nh).reshape(1, aw)
    head_mean = _block_diag(jnp.full((nh, HEAD_DIM, HEAD_DIM), 1.0 / HEAD_DIM, F32), nh)[0].astype(BF16)
    groups = MXU_DIM // w_rg.shape[-1]
    wa_bd = _block_diag(w_rg, groups).astype(BF16)
    wx_bd = _block_diag(w_ig, groups).astype(BF16)
    lw_c = b_rg.size
    b_a = b_rg.reshape(1, lw_c)
    b_x = b_ig.reshape(1, lw_c)
    lam2 = lam.reshape(1, lw_c)
    cb2 = conv_b.reshape(1, lw_c)
    ln_mix2 = ln_mix.reshape(1, d)
    ln_ffn2 = ln_ffn.reshape(1, d)
    wr_t = w_router.T.astype(BF16)
    br = b_router.reshape(-1, 1)
    tab_p = _bias_table(rel_bias, 3 * ATTN_Q_TILE - 1)
    r_cache = k_cache.shape[1]
    tab_s = _bias_table(rel_bias, r_cache + ts - 1)

    zeros_pre = jnp.zeros((bp, SUBLANES, lw_c), F32)
    zeros_h = jnp.zeros((bp, 1, lw_c), F32)
    pre_s = jnp.pad(conv_state, ((0, 0), (SUBLANES - (CONV_WIDTH - 1), 0), (0, 0)))
    lru_w = (conv_w, cb2, wa_bd, wx_bd, b_a, b_x, lam2)
    qp, kp, vp, k32p, v32p, lru_p, tail_p, hl_p = _mixin(
        xp, mod_p,ln_mix2, w_in_bf, qn_t, kn_t, head_mean, zeros_pre, zeros_h, *lru_w, 1, m)
    qs, ks, vs, k32s, v32s, lru_s, tail_s, hl_s = _mixin(
        xs, mod_s,ln_mix2, w_in_bf, qn_t, kn_t, head_mean, pre_s, lru_state[:, None, :], *lru_w, bs, ts)
    attn_p = _attn_prompt(qp, kp, vp, tab_p)
    attn_s = _attn_step(qs, ks, vs, jnp.transpose(k_cache, (0, 2, 3, 1)), jnp.transpose(v_cache, (0, 2, 3, 1)),
                        tab_s, ATTN_STEP_BATCH)

    n_tiles = bp * (s // m) + 1
    x1p, h2, route, cnt = _outproj(attn_p, lru_p, xp, mod_p,ln_ffn2, w_out_bf, wr_t, br,
                                   1, m, n_tiles, 0, None)
    x1s, h2, route, cnt = _outproj(attn_s, lru_s, xs, mod_s,ln_ffn2, w_out_bf, wr_t, br,
                                   bs, ts, n_tiles, n_tiles - 1, (h2, route, cnt))

    n_chunks, _, _, n_rows = _table_sizes(n_tiles)
    assert TOP_K * m * n_tiles >= EXPERT_UNIT_BLOCKS * EXPERT_ROWS
    dispatch_tab, combine_tab, toff_b, row0, nblk, ubase, limit = _route_tables(cnt[:, :, 0].astype(jnp.int32))
    xs_sorted = _dispatch(dispatch_tab, h2, route, toff_b, n_rows, n_chunks)
    ys_sorted = _experts(row0, nblk, ubase, limit, xs_sorted, w_gu, b_gu, w_dn, b_dn)
    yp = _combine(combine_tab, ys_sorted, route, toff_b, x1p, mod_p, 1, m, 0, n_chunks)
    ysm = _combine(combine_tab, ys_sorted, route, toff_b, x1s, mod_s, bs, ts, n_tiles - 1, n_chunks)

    keep = k32p.shape[1]
    new = (k32p.reshape(bp, keep, nh, HEAD_DIM), v32p.reshape(bp, keep, nh, HEAD_DIM),
           tail_p[:, SUBLANES - (CONV_WIDTH - 1):, :], hl_p[:, 0, :],
           k32s.reshape(bs, ts, nh, HEAD_DIM), v32s.reshape(bs, ts, nh, HEAD_DIM),
           tail_s[:, SUBLANES - (CONV_WIDTH - 1):, :], hl_s[:, 0, :])
    return yp, ysm, new


def kernel(x_prompt, x_sample, c_prompt, c_sample, cache_k, cache_v, state_conv, state_lru, ln_mix_w, ln_ffn_w, w_ada, b_ada, w_in, q_norm_w, k_norm_w, rel_bias, conv_w, conv_b, w_rgate, b_rgate, w_igate, b_igate, lru_lambda, w_out, w_router, b_router, w_gate_up, b_gate_up, w_down, b_down):
    depth = w_in.shape[0]
    yp, ys = x_prompt, x_sample
    collected = [[] for _ in range(8)]
    for l in range(depth):
        mod_p, mod_s = _ada(c_prompt, c_sample, w_ada[l], b_ada[l])
        lw = (ln_mix_w[l], ln_ffn_w[l], w_in[l], q_norm_w[l], k_norm_w[l], rel_bias[l], conv_w[l], conv_b[l],
              w_rgate[l], b_rgate[l], w_igate[l], b_igate[l], lru_lambda[l], w_out[l], w_router[l], b_router[l],
              w_gate_up[l], b_gate_up[l], w_down[l], b_down[l])
        yp, ys, new = _layer(yp, ys, mod_p, mod_s, cache_k[l], cache_v[l], state_conv[l], state_lru[l], lw)
        for acc, val in zip(collected, new):
            acc.append(val)
    return (yp, ys) + tuple(jnp.stack(vals) for vals in collected)
```

```python
import functools

import jax
import jax.numpy as jnp
from jax import lax
from jax.experimental import pallas as pl
from jax.experimental.pallas import tpu as pltpu

F32 = jnp.float32
BF16 = jnp.bfloat16

CHUNK = 64
N_LEFT_CHUNKS = 8
ATTN_WINDOW = N_LEFT_CHUNKS * CHUNK
HEAD_DIM = 64
REL_CLIP = 128
CONV_WIDTH = 4
LRU_C = 8.0
N_EXPERTS = 32
TOP_K = 4
SWIGLU_LIMIT = 7.0
SWIGLU_ALPHA = 1.702
NORM_EPS = 1e-6
NEG_INF = -1e30

LANES = 128
SUBLANES = 8
BF16_ROWS = 16
MXU_DIM = 256

TOKEN_TILE = 512
ATTN_Q_TILE = 256
ATTN_STEP_BATCH = 4
EXPERT_ROWS = 256
EXPERT_UNIT_BLOCKS = 4
BIAS_TABLE = 1024
VMEM_LIMIT = 24 * 1024 * 1024
EXPERT_VMEM_LIMIT = 48 * 1024 * 1024


def _params(sem, vmem=VMEM_LIMIT):
    return pltpu.CompilerParams(dimension_semantics=sem, vmem_limit_bytes=vmem)


def _ada_kernel(cp_ref, cs_ref, w_ref, b_ref, op_ref, os_ref):
    w = w_ref[...].astype(BF16)
    for c_ref, o_ref in ((cp_ref, op_ref), (cs_ref, os_ref)):
        c = c_ref[...]
        s = (c * jax.nn.sigmoid(c)).astype(BF16)
        o_ref[...] = jnp.dot(s, w, preferred_element_type=F32) + b_ref[...]


def _ada(c_p, c_s, w_ada, b_ada):
    (n_p, d), n_s = c_p.shape, c_s.shape[0]
    nout = w_ada.shape[1]
    tn = 2048
    mod_p, mod_s = pl.pallas_call(
        _ada_kernel,
        grid=(nout // tn,),
        in_specs=[pl.BlockSpec((n_p, d), lambda j: (0, 0)), pl.BlockSpec((n_s, d), lambda j: (0, 0)),
                  pl.BlockSpec((d, tn), lambda j: (0, j)),
                  pl.BlockSpec((1, tn), lambda j: (0, j))],
        out_specs=[pl.BlockSpec((n_p, tn), lambda j: (0, j)), pl.BlockSpec((n_s, tn), lambda j: (0, j))],
        out_shape=[jax.ShapeDtypeStruct((n_p, nout), F32), jax.ShapeDtypeStruct((n_s, nout), F32)],
        compiler_params=_params(("arbitrary",)),
        name="ada",
    )(c_p, c_s, w_ada, b_ada.reshape(1, nout))
    return mod_p.reshape(n_p, 1, nout), mod_s.reshape(n_s, 1, nout)


MOD_SHIFT_MIX, MOD_SCALE_MIX, MOD_GATE_MIX, MOD_SHIFT_FFN, MOD_SCALE_FFN, MOD_GATE_FFN = range(6)


def _mod_spec(nb, d, term):
    return pl.BlockSpec((nb, 1, d), lambda b, i: (b, 0, term))


def _mixin_kernel(x_ref, sh_ref, sc_ref, ln_ref, win_ref, qn_ref, kn_ref, bd_ref,
                  pre_ref, h0_ref, cw_ref, cb_ref, wa_ref, wx_ref, ba_ref, bx_ref, lam_ref,
                  q_ref, k_ref, v_ref, k32_ref, v32_ref, lru_ref, tail_ref, hl_ref, cx_ref, ch_ref):
    nb, tr, d = x_ref.shape
    m = nb * tr
    aw = q_ref.shape[-1]

    @pl.when(pl.program_id(1) == 0)
    def _():
        cx_ref[...] = pre_ref[...]
        ch_ref[...] = h0_ref[...]

    x = x_ref[...]
    ms = jnp.mean(x * x, axis=-1, keepdims=True)
    h = x * lax.rsqrt(ms + NORM_EPS) * (ln_ref[...] * (1.0 + sc_ref[...])) + sh_ref[...]
    hb = h.reshape(m, d).astype(BF16)

    def proj(part):
        return jnp.dot(hb, win_ref[:, part * aw:(part + 1) * aw], preferred_element_type=F32)

    def head_norm(t, w_ref):
        msq = jnp.dot((t * t).astype(BF16), bd_ref[...], preferred_element_type=F32)
        return t * lax.rsqrt(msq + NORM_EPS) * w_ref[...]

    lru_out, new_tail, h_last = _lru_branch(
        proj(3).reshape(nb, tr, aw), proj(4).reshape(nb, tr, aw), cw_ref, cb_ref,
        wa_ref, wx_ref, ba_ref, bx_ref, lam_ref, cx_ref, ch_ref)
    lru_ref[...] = lru_out
    tail_ref[...] = new_tail
    hl_ref[...] = h_last
    q = head_norm(proj(0), qn_ref)
    k = head_norm(proj(1), kn_ref)
    v = proj(2)
    q_ref[...] = q.astype(BF16).reshape(nb, tr, aw)
    k_ref[...] = k.astype(BF16).reshape(nb, tr, aw)
    v_ref[...] = v.astype(BF16).reshape(nb, tr, aw)
    k32_ref[...] = k.reshape(nb, tr, aw)
    v32_ref[...] = v.reshape(nb, tr, aw)


def _mixin(x, mod, ln_w, w_in_bf, qn_t, kn_t, bd, pre, h0, conv_w, conv_b, wa_bd, wx_bd, b_a, b_x, lam,
           nb, tr):
    nbt, t, d = x.shape
    aw = qn_t.shape[-1]
    c = pre.shape[-1]
    assert c == aw
    keep = min(ATTN_WINDOW, t)
    assert tr == keep or t == tr
    grid = (nbt // nb, t // tr)
    xmap = lambda b, i: (b, i, 0)
    mmap = lambda b, i: (b, 0, 0)
    cmap = lambda b, i: (0, 0)
    cmap3 = lambda b, i: (0, 0, 0)
    tmap = lambda b, i: (b, 0, 0)
    big = pl.BlockSpec((nb, tr, aw), xmap)
    tail = pl.BlockSpec((nb, keep, aw), tmap)
    row = pl.BlockSpec((1, c), cmap)
    return pl.pallas_call(
        _mixin_kernel,
        grid=grid,
        in_specs=[pl.BlockSpec((nb, tr, d), xmap),
                  _mod_spec(nb, d, MOD_SHIFT_MIX), _mod_spec(nb, d, MOD_SCALE_MIX),
                  pl.BlockSpec((1, d), cmap),
                  pl.BlockSpec(w_in_bf.shape, cmap),
                  pl.BlockSpec((1, aw), cmap), pl.BlockSpec((1, aw), cmap),
                  pl.BlockSpec(bd.shape, cmap),
                  pl.BlockSpec((nb, SUBLANES, c), tmap), pl.BlockSpec((nb, 1, c), tmap),
                  pl.BlockSpec(conv_w.shape, cmap), row,
                  pl.BlockSpec(wa_bd.shape, cmap3), pl.BlockSpec(wx_bd.shape, cmap3),
                  row, row, row],
        out_specs=[big, big, big, tail, tail, big,
                   pl.BlockSpec((nb, SUBLANES, c), tmap), pl.BlockSpec((nb, 1, c), tmap)],
        out_shape=[jax.ShapeDtypeStruct((nbt, t, aw), BF16)] * 3
        + [jax.ShapeDtypeStruct((nbt, keep, aw), F32)] * 2
        + [jax.ShapeDtypeStruct((nbt, t, c), BF16),
           jax.ShapeDtypeStruct((nbt, SUBLANES, c), F32),
           jax.ShapeDtypeStruct((nbt, 1, c), F32)],
        scratch_shapes=[pltpu.VMEM((nb, SUBLANES, c), F32), pltpu.VMEM((nb, 1, c), F32)],
        compiler_params=_params(("arbitrary", "arbitrary")),
        name="mixin",
    )(x, mod, mod, ln_w, w_in_bf, qn_t, kn_t, bd, pre, h0, conv_w, conv_b, wa_bd, wx_bd, b_a, b_x, lam)


def _bias_table(rel_bias, off):
    h = rel_bias.shape[0]
    left = off - REL_CLIP
    right = BIAS_TABLE - left - (2 * REL_CLIP + 1)
    assert left >= 0 and right >= 0
    return jnp.concatenate([jnp.broadcast_to(rel_bias[:, :1], (h, left)), rel_bias,
                            jnp.broadcast_to(rel_bias[:, -1:], (h, right))], axis=1)


def _toeplitz(tab_row, rows, cols):
    t = jnp.broadcast_to(tab_row, (rows, BIAS_TABLE))
    t = pltpu.roll(t, BIAS_TABLE - (rows - 1), 1, stride=1, stride_axis=0)
    return t[:, :cols]


def _attn_kernel(q_ref, k0_ref, k1_ref, k2_ref, v0_ref, v1_ref, v2_ref, tab_ref, o_ref, bias_ref):
    b = pl.program_id(0)
    s = pl.program_id(1)
    qt = q_ref.shape[1]
    nk = 3 * qt
    nh = bias_ref.shape[0]

    @pl.when((b == 0) & (s == 0))
    def _():
        qi = lax.broadcasted_iota(jnp.int32, (qt, nk), 0) // CHUNK
        kc = lax.broadcasted_iota(jnp.int32, (qt, nk), 1) // CHUNK
        for h in range(nh):
            band = jnp.where(kc <= qi + N_LEFT_CHUNKS, _toeplitz(tab_ref[h:h + 1, :], qt, nk), NEG_INF)
            bias_ref[h] = jnp.where(kc >= qi, band, NEG_INF)

    pair_w = 2 * HEAD_DIM

    def attend(mask_start):
        q = q_ref[0]
        kcat = jnp.concatenate([k0_ref[0], k1_ref[0], k2_ref[0]], axis=0)
        vcat = jnp.concatenate([v0_ref[0], v1_ref[0], v2_ref[0]], axis=0)
        first = lax.broadcasted_iota(jnp.int32, (qt, pair_w), 1) < HEAD_DIM
        keep = [jnp.where(first, 1.0, 0.0).astype(BF16), jnp.where(first, 0.0, 1.0).astype(BF16)]
        if mask_start:
            in_seq = lax.broadcasted_iota(jnp.int32, (qt, nk), 1) >= (2 - s) * qt
        outs = []
        for pair in range(nh // 2):
            sl = slice(pair * pair_w, (pair + 1) * pair_w)
            q2, k2, v2 = q[:, sl], kcat[:, sl], vcat[:, sl]
            per_head = []
            for sub in range(2):
                sc = lax.dot_general(q2 * keep[sub], k2, (((1,), (1,)), ((), ())), preferred_element_type=F32)
                sc = sc + bias_ref[2 * pair + sub]
                if mask_start:
                    sc = jnp.where(in_seq, sc, NEG_INF)
                mx = jnp.max(sc, axis=-1, keepdims=True)
                p = jnp.exp(sc - mx)
                l = jnp.sum(p, axis=-1, keepdims=True)
                per_head.append(jnp.dot(p.astype(BF16), v2, preferred_element_type=F32) / l)
            outs.append(jnp.where(first, per_head[0], per_head[1]))
        o_ref[0] = jnp.concatenate(outs, axis=-1).astype(BF16)

    @pl.when(s < 2)
    def _():
        attend(True)

    @pl.when(s >= 2)
    def _():
        attend(False)


def _attn_prompt(q, k, v, tab):
    b, s, aw = q.shape
    qt = ATTN_Q_TILE
    nh = aw // HEAD_DIM
    qspec = pl.BlockSpec((1, qt, aw), lambda i, j: (i, j, 0))

    def kspec(back):
        return pl.BlockSpec((1, qt, aw), lambda i, j: (i, jnp.maximum(j - back, 0), 0))

    return pl.pallas_call(
        _attn_kernel,
        grid=(b, s // qt),
        in_specs=[qspec, kspec(2), kspec(1), kspec(0), kspec(2), kspec(1), kspec(0),
                  pl.BlockSpec(tab.shape, lambda i, j: (0, 0))],
        out_specs=qspec,
        out_shape=jax.ShapeDtypeStruct((b, s, aw), BF16),
        scratch_shapes=[pltpu.VMEM((nh, qt, 3 * qt), F32)],
        compiler_params=_params(("arbitrary", "arbitrary")),
        name="attn_prompt",
    )(q, k, k, k, v, v, v, tab)


def _attn_step_kernel(q_ref, kn_ref, vn_ref, ck_ref, cv_ref, tab_ref, o_ref, bias_ref):
    step = pl.program_id(0)
    nbs, t, aw = q_ref.shape
    nh = aw // HEAD_DIM
    rows = nh * t
    r = ck_ref.shape[-1]
    nk = r + LANES
    nt_dims = (((1,), (1,)), ((), ()))

    @pl.when(step == 0)
    def _():
        ok = lax.broadcasted_iota(jnp.int32, (t, nk), 1) < r + t
        for h in range(nh):
            bias_ref[h * t:(h + 1) * t, :] = jnp.where(ok, _toeplitz(tab_ref[h:h + 1, :], t, nk), NEG_INF)

    own = (lax.broadcasted_iota(jnp.int32, (rows, aw), 0) // t
           == lax.broadcasted_iota(jnp.int32, (rows, aw), 1) // HEAD_DIM)
    own_f = jnp.where(own, 1.0, 0.0)
    own_bf = own_f.astype(BF16)
    pad = jnp.zeros((LANES - t, aw), BF16)
    for b in range(nbs):
        q_bd = jnp.concatenate([q_ref[b]] * nh, axis=0) * own_bf
        k_old = ck_ref[b].reshape(aw, r).astype(BF16)
        v_old = cv_ref[b].reshape(aw, r).astype(BF16)
        k_new = jnp.concatenate([kn_ref[b], pad], axis=0)
        v_new = jnp.concatenate([vn_ref[b], pad], axis=0)
        s_old = jnp.dot(q_bd, k_old, preferred_element_type=F32) + bias_ref[:, :r]
        s_new = lax.dot_general(q_bd, k_new, nt_dims, preferred_element_type=F32) + bias_ref[:, r:]
        mx = jnp.maximum(jnp.max(s_old, axis=-1, keepdims=True), jnp.max(s_new, axis=-1, keepdims=True))
        p_old = jnp.exp(s_old - mx)
        p_new = jnp.exp(s_new - mx)
        l = jnp.sum(p_old, axis=-1, keepdims=True) + jnp.sum(p_new, axis=-1, keepdims=True)
        o_all = (lax.dot_general(p_old.astype(BF16), v_old, nt_dims, preferred_element_type=F32)
                 + jnp.dot(p_new.astype(BF16), v_new, preferred_element_type=F32))
        o_all = o_all * own_f / l
        out = o_all[0:t]
        for h in range(1, nh):
            out = out + o_all[h * t:(h + 1) * t]
        o_ref[b] = out.astype(BF16)


def _attn_step(q, kn, vn, ck, cv, tab, nbs):
    b, t, aw = q.shape
    nh = aw // HEAD_DIM
    r = ck.shape[-1]
    new = pl.BlockSpec((nbs, t, aw), lambda i: (i, 0, 0))
    old = pl.BlockSpec((nbs, nh, HEAD_DIM, r), lambda i: (i, 0, 0, 0))
    return pl.pallas_call(
        _attn_step_kernel,
        grid=(b // nbs,),
        in_specs=[new, new, new, old, old, pl.BlockSpec(tab.shape, lambda i: (0, 0))],
        out_specs=new,
        out_shape=jax.ShapeDtypeStruct((b, t, aw), BF16),
        scratch_shapes=[pltpu.VMEM((nh * t, r + LANES), F32)],
        compiler_params=_params(("arbitrary",)),
        name="attn_step",
    )(q, kn, vn, ck, cv, tab)


def _gelu_tanh(x):
    return x * (0.5 * (1.0 + jnp.tanh(0.7978845608028654 * (x + 0.044715 * (x * x * x)))))


def _lru_branch(x, yg, cw_ref, cb_ref, wa_ref, wx_ref, ba_ref, bx_ref, lam_ref, cx_ref, ch_ref):
    nb, tr, c = x.shape
    m = nb * tr
    half = c // 2
    xp = jnp.concatenate([cx_ref[...], x], axis=1)
    new_tail = xp[:, tr:tr + SUBLANES, :]
    groups = tr // SUBLANES
    xg = xp.reshape(nb * (groups + 1), SUBLANES, c)
    first_rows = lax.broadcasted_iota(jnp.int32, (nb, groups, SUBLANES, c), 2)
    y = cb_ref[...] + cw_ref[CONV_WIDTH - 1:CONV_WIDTH, :] * x
    for back in range(1, CONV_WIDTH):
        rot = pltpu.roll(xg, back, 1).reshape(nb, groups + 1, SUBLANES, c)
        shifted = jnp.where(first_rows >= back, rot[:, 1:], rot[:, :groups]).reshape(nb, tr, c)
        y = y + cw_ref[CONV_WIDTH - 1 - back:CONV_WIDTH - back, :] * shifted
    y2 = y.reshape(m, c)
    yb = y2.astype(BF16)

    def gate(w_ref, b_ref):
        g = jnp.concatenate(
            [jnp.dot(yb[:, :half], w_ref[0], preferred_element_type=F32),
             jnp.dot(yb[:, half:], w_ref[1], preferred_element_type=F32)], axis=1)
        return jax.nn.sigmoid(g + b_ref[...])

    rg = gate(wa_ref, ba_ref)
    ig = gate(wx_ref, bx_ref)
    lam = lam_ref[...]
    log_sig = jnp.minimum(lam, 0.0) - jnp.log1p(jnp.exp(-jnp.abs(lam)))
    log_a = rg * (LRU_C * log_sig)
    a_cum = jnp.exp(log_a)
    b_cum = jnp.sqrt(-jnp.tanh(log_a) * (a_cum * a_cum + 1.0)) * (ig * y2)
    a_cum = a_cum.reshape(nb * groups, SUBLANES, c)
    b_cum = b_cum.reshape(nb * groups, SUBLANES, c)
    row = lax.broadcasted_iota(jnp.int32, a_cum.shape, 1)
    dist = 1
    while dist < SUBLANES:
        keep = row >= dist
        a_sh = jnp.where(keep, pltpu.roll(a_cum, dist, 1), 1.0)
        b_sh = jnp.where(keep, pltpu.roll(b_cum, dist, 1), 0.0)
        b_cum = a_cum * b_sh + b_cum
        a_cum = a_cum * a_sh
        dist *= 2
    a_grp = a_cum.reshape(nb, groups, SUBLANES, c)
    b_grp = b_cum.reshape(nb, groups, SUBLANES, c)
    carry = ch_ref[...]
    pieces = []
    for grp in range(groups):
        h_grp = a_grp[:, grp] * carry + b_grp[:, grp]
        carry = h_grp[:, SUBLANES - 1:SUBLANES, :]
        pieces.append(h_grp)
    h = jnp.concatenate(pieces, axis=1)
    ch_ref[...] = carry
    cx_ref[...] = new_tail
    return (h * _gelu_tanh(yg)).astype(BF16), new_tail, carry


def _outproj_kernel(*refs, aliased):
    (at_ref, lr_ref, x_ref, gm_ref, shf_ref, scf_ref, lnf_ref, wo_ref, wr_ref, br_ref) = refs[:10]
    x1_ref, h2_ref, route_ref, cnt_ref = refs[10 + aliased:]
    nb, tr, d = x_ref.shape
    m = nb * tr
    aw = at_ref.shape[-1]
    ne = wr_ref.shape[0]
    at = at_ref[...].reshape(m, aw)
    lr = lr_ref[...].reshape(m, aw)
    mix = (jnp.dot(at, wo_ref[0:aw, :], preferred_element_type=F32)
           + jnp.dot(lr, wo_ref[aw:2 * aw, :], preferred_element_type=F32))
    x1 = x_ref[...] + gm_ref[...] * mix.reshape(nb, tr, d)
    x1_ref[...] = x1
    ms = jnp.mean(x1 * x1, axis=-1, keepdims=True)
    h2 = (x1 * lax.rsqrt(ms + NORM_EPS) * (lnf_ref[...] * (1.0 + scf_ref[...])) + shf_ref[...]).reshape(m, d)
    h2_ref[...] = h2.astype(BF16)

    logits = lax.dot_general(wr_ref[...], h2.astype(BF16), (((1,), (1,)), ((), ())),
                             preferred_element_type=F32) + br_ref[...]
    e_iota = lax.broadcasted_iota(jnp.int32, (ne, m), 0).astype(F32)
    vals = logits
    top_v, sels = [], []
    for k in range(TOP_K):
        mx = jnp.max(vals, axis=0, keepdims=True)
        idx = jnp.min(jnp.where(vals == mx, e_iota, float(ne)), axis=0, keepdims=True)
        sel = e_iota == idx
        vals = jnp.where(sel, -jnp.inf, vals)
        top_v.append(mx)
        sels.append(sel)
        route_ref[0, k:k + 1, :] = idx
    ex = [jnp.exp(v - top_v[0]) for v in top_v]
    den = ex[0] + ex[1] + ex[2] + ex[3]
    chosen = jnp.zeros((ne, m), F32)
    for k in range(TOP_K):
        route_ref[0, 2 * TOP_K + k:2 * TOP_K + k + 1, :] = ex[k] / den
        chosen = chosen + jnp.where(sels[k], 1.0, 0.0)
    before = (lax.broadcasted_iota(jnp.int32, (m, m), 0) < lax.broadcasted_iota(jnp.int32, (m, m), 1))
    rank = jnp.dot(chosen.astype(BF16), jnp.where(before, 1.0, 0.0).astype(BF16), preferred_element_type=F32)
    for k in range(TOP_K):
        route_ref[0, TOP_K + k:TOP_K + k + 1, :] = jnp.sum(jnp.where(sels[k], rank, 0.0), axis=0, keepdims=True)
    route_ref[0, 3 * TOP_K:4 * TOP_K, :] = jnp.zeros((TOP_K, m), F32)
    cnt_ref[0] = jnp.broadcast_to(jnp.sum(chosen, axis=1, keepdims=True), (ne, LANES))


def _outproj(attn, lru_o, x, mod, lnf, w_out_bf, wr_t, br, nb, tr, n_tiles, tile0, prev):
    nbt, t, d = x.shape
    aw = attn.shape[-1]
    m = nb * tr
    assert m == TOKEN_TILE
    ne = wr_t.shape[0]
    tiles_per_seq = t // tr
    xmap = lambda b, i: (b, i, 0)
    c2 = lambda b, i: (0, 0)
    tile = lambda b, i: (tile0 + b * tiles_per_seq + i, 0)
    tile3 = lambda b, i: (tile0 + b * tiles_per_seq + i, 0, 0)
    in_specs = [pl.BlockSpec((nb, tr, aw), xmap), pl.BlockSpec((nb, tr, aw), xmap),
                pl.BlockSpec((nb, tr, d), xmap),
                _mod_spec(nb, d, MOD_GATE_MIX), _mod_spec(nb, d, MOD_SHIFT_FFN), _mod_spec(nb, d, MOD_SCALE_FFN),
                pl.BlockSpec((1, d), c2), pl.BlockSpec(w_out_bf.shape, c2),
                pl.BlockSpec(wr_t.shape, c2), pl.BlockSpec((ne, 1), c2)]
    args = [attn, lru_o, x, mod, mod, mod, lnf, w_out_bf, wr_t, br]
    aliases = {}
    if prev is not None:
        in_specs += [pl.BlockSpec(memory_space=pl.ANY)] * 3
        args += list(prev)
        aliases = {10: 1, 11: 2, 12: 3}
    return pl.pallas_call(
        functools.partial(_outproj_kernel, aliased=len(aliases)),
        grid=(nbt // nb, tiles_per_seq),
        in_specs=in_specs,
        out_specs=[pl.BlockSpec((nb, tr, d), xmap), pl.BlockSpec((m, d), tile),
                   pl.BlockSpec((1, 4 * TOP_K, m), tile3), pl.BlockSpec((1, ne, LANES), tile3)],
        out_shape=[jax.ShapeDtypeStruct((nbt, t, d), F32),
                   jax.ShapeDtypeStruct((n_tiles * m, d), BF16),
                   jax.ShapeDtypeStruct((n_tiles, 4 * TOP_K, m), F32),
                   jax.ShapeDtypeStruct((n_tiles, ne, LANES), F32)],
        input_output_aliases=aliases,
        compiler_params=_params(("arbitrary", "arbitrary")),
        name="outproj",
    )(*args)


def _tile_rows(m):
    cap = TOP_K * m + N_EXPERTS * (BF16_ROWS - 1) + BF16_ROWS
    return -(-cap // TOKEN_TILE) * TOKEN_TILE


def _table_sizes(nt):
    g = BF16_ROWS
    m = TOKEN_TILE
    n_chunks = _tile_rows(m) // g
    n_gap = -(-(N_EXPERTS * (EXPERT_ROWS // g - 1)) // nt)
    bound = TOP_K * m * nt + nt * N_EXPERTS * (g - 1) + N_EXPERTS * (EXPERT_ROWS - g)
    n_sorted = -(-bound // EXPERT_ROWS) * EXPERT_ROWS
    return n_chunks, n_gap, n_sorted, n_sorted + 2 * (n_chunks + n_gap) * g


def _route_tables(cnt):
    nt = cnt.shape[0]
    g = BF16_ROWS
    bm = EXPERT_ROWS
    n_chunks, n_gap, n_sorted, _ = _table_sizes(nt)
    e_ids = jnp.arange(N_EXPERTS, dtype=jnp.int32)
    t_ids = jnp.arange(nt, dtype=jnp.int32)
    upto = (e_ids[:, None] <= e_ids[None, :]).astype(jnp.int32)
    pc = (cnt + g - 1) // g * g
    ctile = jnp.sum(pc[:, :, None] * upto[None], axis=1)
    toff = ctile - pc
    trow = ctile[:, -1]
    tot = jnp.sum(pc, axis=0)
    reg = (tot + bm - 1) // bm * bm
    creg = jnp.sum(reg[:, None] * upto, axis=0)
    base = creg - reg
    earlier = (t_ids[:, None] < t_ids[None, :]).astype(jnp.int32)
    goff = base[None, :] + jnp.sum(pc[:, None, :] * earlier[:, :, None], axis=0)
    r = jnp.arange(n_chunks, dtype=jnp.int32) * g
    r3 = r[None, :, None]
    in_seg = (toff[:, None, :] <= r3) & (r3 < ctile[:, None, :])
    dst = jnp.sum(jnp.where(in_seg, (goff - toff)[:, None, :], 0), axis=2) + r[None, :]
    dst = jnp.where(r[None, :] < trow[:, None], dst, -1)
    gcnt = (reg - tot) // g
    gcum = jnp.sum(gcnt[:, None] * upto, axis=0)
    gstart = gcum - gcnt
    s = jnp.arange(nt * n_gap, dtype=jnp.int32)
    in_gap = (gstart[None, :] <= s[:, None]) & (s[:, None] < gcum[None, :])
    gdst = jnp.sum(jnp.where(in_gap, (base + tot - g * gstart)[None, :] + g * s[:, None], 0), axis=1)
    gdst = jnp.where(s < gcum[-1], gdst, -1).reshape(nt, n_gap)
    table = jnp.concatenate([dst, gdst], axis=1).astype(jnp.int32)
    n_entries = n_chunks + n_gap
    spare = n_sorted + ((t_ids % 2)[:, None] * n_entries + jnp.arange(n_entries, dtype=jnp.int32)[None, :]) * g
    dispatch_tab = jnp.where(table >= 0, table, spare).astype(jnp.int32)
    combine_tab = jnp.maximum(dst, 0).astype(jnp.int32)
    toff_b = jnp.broadcast_to(toff.astype(F32)[:, :, None], (nt, N_EXPERTS, LANES))
    limit = (creg[-1] - EXPERT_UNIT_BLOCKS * bm).astype(jnp.int32).reshape(1)
    nblk = reg // bm
    units = (nblk + EXPERT_UNIT_BLOCKS - 1) // EXPERT_UNIT_BLOCKS
    ubase = jnp.sum(units[:, None] * upto, axis=0) - units
    return (dispatch_tab, combine_tab, toff_b, base.astype(jnp.int32), nblk.astype(jnp.int32),
            ubase.astype(jnp.int32), limit)


def _slot_rows(route_ref, toff_ref, m):
    ne = toff_ref.shape[1]
    e_iota = lax.broadcasted_iota(jnp.int32, (ne, m), 0).astype(F32)
    toff_col = toff_ref[0][:, 0:1]
    pos = []
    for k in range(TOP_K):
        sel = e_iota == route_ref[0, k:k + 1, :]
        start = jnp.sum(jnp.where(sel, toff_col, 0.0), axis=0, keepdims=True)
        pos.append(start + route_ref[0, TOP_K + k:TOP_K + k + 1, :])
    return pos


def _dispatch_kernel(tab_ref, h2_ref, route_ref, toff_ref, xs_hbm, buf_ref, sem, *, n_chunks, n_tiles):
    t = pl.program_id(0)
    slot = t % 2
    m = h2_ref.shape[0]
    rows = buf_ref.shape[1]
    n_entries = tab_ref.shape[1]
    g = BF16_ROWS
    per_chunk = m // g

    def start(c):
        src = c * g if c < n_chunks else rows - g
        pltpu.make_async_copy(
            buf_ref.at[slot, pl.ds(src, g)],
            xs_hbm.at[pl.ds(pl.multiple_of(tab_ref[t, c], g), g)], sem.at[slot]).start()

    def wait_all(which):
        for _ in range(n_entries):
            pltpu.make_async_copy(buf_ref.at[which, pl.ds(0, g)], xs_hbm.at[pl.ds(0, g)], sem.at[which]).wait()

    @pl.when(t >= 2)
    def _():
        wait_all(slot)

    pos = _slot_rows(route_ref, toff_ref, m)
    h2 = h2_ref[...]
    for rc in range(rows // m):
        r_iota = (lax.broadcasted_iota(jnp.int32, (m, m), 0) + rc * m).astype(F32)
        onehot = jnp.zeros((m, m), F32)
        for k in range(TOP_K):
            onehot = jnp.where(r_iota == pos[k], 1.0, onehot)
        buf_ref[slot, rc * m:(rc + 1) * m, :] = jnp.dot(
            onehot.astype(BF16), h2, preferred_element_type=F32).astype(BF16)
        for c in range((rc - 1) * per_chunk, rc * per_chunk) if rc > 0 else ():
            start(c)
    for c in range(n_chunks - per_chunk, n_entries):
        start(c)

    @pl.when(t == n_tiles - 1)
    def _():
        if n_tiles > 1:
            wait_all(1 - slot)
        wait_all(slot)


def _dispatch(table, h2, route, toff_b, n_rows, n_chunks):
    nt = route.shape[0]
    m = TOKEN_TILE
    d = h2.shape[1]
    rows = _tile_rows(m)
    grid_spec = pltpu.PrefetchScalarGridSpec(
        num_scalar_prefetch=1,
        grid=(nt,),
        in_specs=[pl.BlockSpec((m, d), lambda t, tab: (t, 0)),
                  pl.BlockSpec((1, 4 * TOP_K, m), lambda t, tab: (t, 0, 0)),
                  pl.BlockSpec((1, N_EXPERTS, LANES), lambda t, tab: (t, 0, 0))],
        out_specs=pl.BlockSpec(memory_space=pl.ANY),
        scratch_shapes=[pltpu.VMEM((2, rows, d), BF16), pltpu.SemaphoreType.DMA((2,))],
    )
    return pl.pallas_call(
        functools.partial(_dispatch_kernel, n_chunks=n_chunks, n_tiles=nt),
        grid_spec=grid_spec,
        out_shape=jax.ShapeDtypeStruct((n_rows, d), BF16),
        compiler_params=_params(("arbitrary",)),
        name="dispatch",
    )(table, h2, route, toff_b)


def _expert_kernel(row0_ref, nblk_ref, ubase_ref, limit_ref, xs_hbm, wgu_ref, bgu_ref, wdn_ref, bdn_ref, ys_hbm,
                   wgu_bf, wdn_bf, xbuf, ybuf, sem_in, sem_out, pend_ref):
    e = pl.program_id(0)
    ne = pl.num_programs(0)
    bm = EXPERT_ROWS
    unit = xbuf.shape[1]
    per_unit = unit // bm
    dff = wdn_ref.shape[1]
    nblk = nblk_ref[e]
    n_units = (nblk + per_unit - 1) // per_unit
    base = ubase_ref[e]

    def unit_start(expert, s):
        true = row0_ref[expert] + s * unit
        start = jnp.minimum(true, limit_ref[0])
        return pl.multiple_of(start, bm), pl.multiple_of(true - start, bm)

    def in_copy(expert, s, slot):
        start, _ = unit_start(expert, s)
        return pltpu.make_async_copy(xs_hbm.at[pl.ds(start, unit)], xbuf.at[slot], sem_in.at[slot])

    def out_copy(j, slot, i):
        start = pl.multiple_of(row0_ref[e] + j * bm, bm)
        return pltpu.make_async_copy(ybuf.at[slot, pl.ds(i * bm, bm)], ys_hbm.at[pl.ds(start, bm)],
                                     sem_out.at[slot])

    def wait_pending(slot):
        count = pend_ref[slot]

        @pl.when(count == per_unit)
        def _():
            for i in range(per_unit):
                out_copy(0, slot, i).wait()

        for i in range(per_unit - 1):
            @pl.when((count < per_unit) & (i < count))
            def _():
                out_copy(0, slot, i).wait()
        pend_ref[slot] = 0

    @pl.when(e == 0)
    def _():
        pend_ref[0] = 0
        pend_ref[1] = 0

    @pl.when((e == 0) & (nblk > 0))
    def _():
        in_copy(e, 0, 0).start()

    wgu_bf[...] = wgu_ref[0].astype(BF16)
    wdn_bf[...] = wdn_ref[0].astype(BF16)

    def run_unit(s, carry):
        slot = (base + s) % 2
        in_copy(e, s, slot).wait()

        @pl.when(s + 1 < n_units)
        def _():
            in_copy(e, s + 1, 1 - slot).start()

        wait_pending(slot)
        _, lead = unit_start(e, s)

        def ffn(first, count):
            rows = count * bm
            x = xbuf[slot, pl.ds(pl.multiple_of(lead + first * bm, bm), rows), :]
            gu = jnp.dot(x, wgu_bf[...], preferred_element_type=F32) + bgu_ref[0]
            gate = jnp.minimum(gu[:, :dff], SWIGLU_LIMIT)
            up = jnp.clip(gu[:, dff:], -SWIGLU_LIMIT, SWIGLU_LIMIT)
            glu = gate * jax.nn.sigmoid(gate * SWIGLU_ALPHA)
            act = ((up + 1.0) * glu).astype(BF16)
            ybuf[slot, first * bm:first * bm + rows, :] = (
                jnp.dot(act, wdn_bf[...], preferred_element_type=F32) + bdn_ref[0]).astype(BF16)
            for i in range(first, first + count):
                out_copy(s * per_unit + i, slot, i).start()

        for first in range(0, per_unit, 2):
            have = nblk - s * per_unit - first

            @pl.when(have >= 2)
            def _():
                ffn(first, 2)

            @pl.when(have == 1)
            def _():
                ffn(first, 1)
        pend_ref[slot] = jnp.minimum(nblk - s * per_unit, per_unit)
        return carry

    lax.fori_loop(0, n_units, run_unit, 0)

    nxt = jnp.minimum(e + 1, ne - 1)

    @pl.when((e + 1 < ne) & (nblk_ref[nxt] > 0))
    def _():
        in_copy(nxt, 0, (base + n_units) % 2).start()

    @pl.when(e == ne - 1)
    def _():
        wait_pending(0)
        wait_pending(1)


def _experts(row0, nblk, ubase, limit, xs, w_gu, b_gu, w_dn, b_dn):
    n_rows, d = xs.shape
    bm = EXPERT_ROWS
    unit = EXPERT_UNIT_BLOCKS * bm
    ne, _, dff2 = w_gu.shape
    dff = w_dn.shape[1]
    exp3 = lambda e, r0, nb, ub, lim: (e, 0, 0)
    grid_spec = pltpu.PrefetchScalarGridSpec(
        num_scalar_prefetch=4,
        grid=(ne,),
        in_specs=[pl.BlockSpec(memory_space=pl.ANY),
                  pl.BlockSpec((1, d, dff2), exp3), pl.BlockSpec((1, 1, dff2), exp3),
                  pl.BlockSpec((1, dff, d), exp3), pl.BlockSpec((1, 1, d), exp3)],
        out_specs=pl.BlockSpec(memory_space=pl.ANY),
        scratch_shapes=[pltpu.VMEM((d, dff2), BF16), pltpu.VMEM((dff, d), BF16),
                        pltpu.VMEM((2, unit, d), BF16), pltpu.VMEM((2, unit, d), BF16),
                        pltpu.SemaphoreType.DMA((2,)), pltpu.SemaphoreType.DMA((2,)),
                        pltpu.SMEM((2,), jnp.int32)],
    )
    return pl.pallas_call(
        _expert_kernel,
        grid_spec=grid_spec,
        out_shape=jax.ShapeDtypeStruct((n_rows, d), BF16),
        compiler_params=_params(("arbitrary",), EXPERT_VMEM_LIMIT),
        name="experts",
    )(row0, nblk, ubase, limit, xs, w_gu, b_gu.reshape(ne, 1, dff2), w_dn, b_dn.reshape(ne, 1, d))


def _combine_kernel(tab_ref, ys_hbm, route_ref, toff_ref, x1_ref, gf_ref, o_ref, buf_ref, sem, *, n_chunks, tile0):
    step = pl.program_id(0)
    n_steps = pl.num_programs(0)
    t = tile0 + step
    slot = step % 2
    nb, tr, d = x1_ref.shape
    m = nb * tr
    rows = buf_ref.shape[1]
    g = BF16_ROWS

    def fetch(tile, which):
        for c in range(n_chunks):
            pltpu.make_async_copy(
                ys_hbm.at[pl.ds(pl.multiple_of(tab_ref[tile, c], g), g)],
                buf_ref.at[which, pl.ds(c * g, g)], sem.at[which]).start()

    @pl.when(step == 0)
    def _():
        fetch(t, slot)

    @pl.when(step + 1 < n_steps)
    def _():
        fetch(t + 1, 1 - slot)

    pos = _slot_rows(route_ref, toff_ref, m)
    gates = [route_ref[0, 2 * TOP_K + k:2 * TOP_K + k + 1, :] for k in range(TOP_K)]
    stacked = jnp.concatenate(pos + gates + [jnp.zeros((LANES - 2 * TOP_K, m), F32)], axis=0)
    cols = stacked.T
    for _ in range(n_chunks):
        pltpu.make_async_copy(ys_hbm.at[pl.ds(0, g)], buf_ref.at[slot, pl.ds(0, g)], sem.at[slot]).wait()

    acc = jnp.zeros((m, d), F32)
    for rc in range(rows // m):
        c_iota = (lax.broadcasted_iota(jnp.int32, (m, m), 1) + rc * m).astype(F32)
        weights = jnp.zeros((m, m), F32)
        for k in range(TOP_K):
            weights = jnp.where(c_iota == cols[:, k:k + 1], cols[:, TOP_K + k:TOP_K + k + 1], weights)
        acc = acc + jnp.dot(weights.astype(BF16), buf_ref[slot, rc * m:(rc + 1) * m, :],
                            preferred_element_type=F32)
    o_ref[...] = x1_ref[...] + gf_ref[...] * acc.reshape(nb, tr, d)


def _combine(table, ys, route, toff_b, x1, gf, nb, tr, tile0, n_chunks):
    nbt, t, d = x1.shape
    m = nb * tr
    assert m == TOKEN_TILE
    rows = _tile_rows(m)
    tiles_per_seq = t // tr
    n_steps = (nbt // nb) * tiles_per_seq
    xmap = lambda s, tab: (s // tiles_per_seq, s % tiles_per_seq, 0)
    grid_spec = pltpu.PrefetchScalarGridSpec(
        num_scalar_prefetch=1,
        grid=(n_steps,),
        in_specs=[pl.BlockSpec(memory_space=pl.ANY),
                  pl.BlockSpec((1, 4 * TOP_K, m), lambda s, tab: (tile0 + s, 0, 0)),
                  pl.BlockSpec((1, N_EXPERTS, LANES), lambda s, tab: (tile0 + s, 0, 0)),
                  pl.BlockSpec((nb, tr, d), xmap),
                  pl.BlockSpec((nb, 1, d), lambda s, tab: (s // tiles_per_seq, 0, MOD_GATE_FFN))],
        out_specs=pl.BlockSpec((nb, tr, d), xmap),
        scratch_shapes=[pltpu.VMEM((2, rows, d), BF16), pltpu.SemaphoreType.DMA((2,))],
    )
    return pl.pallas_call(
        functools.partial(_combine_kernel, n_chunks=n_chunks, tile0=tile0),
        grid_spec=grid_spec,
        out_shape=jax.ShapeDtypeStruct((nbt, t, d), F32),
        compiler_params=_params(("arbitrary",)),
        name="combine",
    )(table, ys, route, toff_b, x1, gf)


def _block_diag(w, groups):
    n, k, _ = w.shape
    w = w.reshape(n // groups, groups, k, k)
    eye = jnp.eye(groups, dtype=w.dtype)
    return jnp.einsum("ngij,gh->ngihj", w, eye).reshape(n // groups, groups * k, groups * k)


def _layer(xp, xs, mod_p, mod_s, k_cache, v_cache, conv_state, lru_state, lw):
    (ln_mix, ln_ffn, w_in, q_norm, k_norm, rel_bias, conv_w, conv_b, w_rg, b_rg, w_ig, b_ig, lam,
     w_out, w_router, b_router, w_gu, b_gu, w_dn, b_dn) = lw
    bp, s, d = xp.shape
    bs, ts, _ = xs.shape
    aw = w_out.shape[0] // 2
    nh = aw // HEAD_DIM
    m = TOKEN_TILE
    assert s % m == 0 and bs * ts == m and s % ATTN_Q_TILE == 0

    w_in_bf = w_in.astype(BF16)
    w_out_bf = w_out.astype(BF16)
    qn_t = jnp.tile(q_norm * (HEAD_DIM ** -0.5), nh).reshape(1, aw)
    kn_t = jnp.tile(k_norm, nh).reshape(1, aw)
    head_mean = _block_diag(jnp.full((nh, HEAD_DIM, HEAD_DIM), 1.0 / HEAD_DIM, F32), nh)[0].astype(BF16)
    groups = MXU_DIM // w_rg.shape[-1]
    wa_bd = _block_diag(w_rg, groups).astype(BF16)
    wx_bd = _block_diag(w_ig, groups).astype(BF16)
    lw_c = b_rg.size
    b_a = b_rg.reshape(1, lw_c)
    b_x = b_ig.reshape(1, lw_c)
    lam2 = lam.reshape(1, lw_c)
    cb2 = conv_b.reshape(1, lw_c)
    ln_mix2 = ln_mix.reshape(1, d)
    ln_ffn2 = ln_ffn.reshape(1, d)
    wr_t = w_router.T.astype(BF16)
    br = b_router.reshape(-1, 1)
    tab_p = _bias_table(rel_bias, 3 * ATTN_Q_TILE - 1)
    r_cache = k_cache.shape[1]
    tab_s = _bias_table(rel_bias, r_cache + ts - 1)

    zeros_pre = jnp.zeros((bp, SUBLANES, lw_c), F32)
    zeros_h = jnp.zeros((bp, 1, lw_c), F32)
    pre_s = jnp.pad(conv_state, ((0, 0), (SUBLANES - (CONV_WIDTH - 1), 0), (0, 0)))
    lru_w = (conv_w, cb2, wa_bd, wx_bd, b_a, b_x, lam2)
    qp, kp, vp, k32p, v32p, lru_p, tail_p, hl_p = _mixin(
        xp, mod_p,ln_mix2, w_in_bf, qn_t, kn_t, head_mean, zeros_pre, zeros_h, *lru_w, 1, m)
    qs, ks, vs, k32s, v32s, lru_s, tail_s, hl_s = _mixin(
        xs, mod_s,ln_mix2, w_in_bf, qn_t, kn_t, head_mean, pre_s, lru_state[:, None, :], *lru_w, bs, ts)
    attn_p = _attn_prompt(qp, kp, vp, tab_p)
    attn_s = _attn_step(qs, ks, vs, jnp.transpose(k_cache, (0, 2, 3, 1)), jnp.transpose(v_cache, (0, 2, 3, 1)),
                        tab_s, ATTN_STEP_BATCH)

    n_tiles = bp * (s // m) + 1
    x1p, h2, route, cnt = _outproj(attn_p, lru_p, xp, mod_p,ln_ffn2, w_out_bf, wr_t, br,
                                   1, m, n_tiles, 0, None)
    x1s, h2, route, cnt = _outproj(attn_s, lru_s, xs, mod_s,ln_ffn2, w_out_bf, wr_t, br,
                                   bs, ts, n_tiles, n_tiles - 1, (h2, route, cnt))

    n_chunks, _, _, n_rows = _table_sizes(n_tiles)
    assert TOP_K * m * n_tiles >= EXPERT_UNIT_BLOCKS * EXPERT_ROWS
    dispatch_tab, combine_tab, toff_b, row0, nblk, ubase, limit = _route_tables(cnt[:, :, 0].astype(jnp.int32))
    xs_sorted = _dispatch(dispatch_tab, h2, route, toff_b, n_rows, n_chunks)
    ys_sorted = _experts(row0, nblk, ubase, limit, xs_sorted, w_gu, b_gu, w_dn, b_dn)
    yp = _combine(combine_tab, ys_sorted, route, toff_b, x1p, mod_p, 1, m, 0, n_chunks)
    ysm = _combine(combine_tab, ys_sorted, route, toff_b, x1s, mod_s, bs, ts, n_tiles - 1, n_chunks)

    keep = k32p.shape[1]
    new = (k32p.reshape(bp, keep, nh, HEAD_DIM), v32p.reshape(bp, keep, nh, HEAD_DIM),
           tail_p[:, SUBLANES - (CONV_WIDTH - 1):, :], hl_p[:, 0, :],
           k32s.reshape(bs, ts, nh, HEAD_DIM), v32s.reshape(bs, ts, nh, HEAD_DIM),
           tail_s[:, SUBLANES - (CONV_WIDTH - 1):, :], hl_s[:, 0, :])
    return yp, ysm, new


def kernel(x_prompt, x_sample, c_prompt, c_sample, cache_k, cache_v, state_conv, state_lru, ln_mix_w, ln_ffn_w, w_ada, b_ada, w_in, q_norm_w, k_norm_w, rel_bias, conv_w, conv_b, w_rgate, b_rgate, w_igate, b_igate, lru_lambda, w_out, w_router, b_router, w_gate_up, b_gate_up, w_down, b_down):
    depth = w_in.shape[0]
    yp, ys = x_prompt, x_sample
    collected = [[] for _ in range(8)]
    for l in range(depth):
        mod_p, mod_s = _ada(c_prompt, c_sample, w_ada[l], b_ada[l])
        lw = (ln_mix_w[l], ln_ffn_w[l], w_in[l], q_norm_w[l], k_norm_w[l], rel_bias[l], conv_w[l], conv_b[l],
              w_rgate[l], b_rgate[l], w_igate[l], b_igate[l], lru_lambda[l], w_out[l], w_router[l], b_router[l],
              w_gate_up[l], b_gate_up[l], w_down[l], b_down[l])
        yp, ys, new = _layer(yp, ys, mod_p, mod_s, cache_k[l], cache_v[l], state_conv[l], state_lru[l], lw)
        for acc, val in zip(collected, new):
            acc.append(val)
    return (yp, ys) + tuple(jnp.stack(vals) for vals in collected)
```

```python
import functools

import jax
import jax.numpy as jnp
from jax import lax
from jax.experimental import pallas as pl
from jax.experimental.pallas import tpu as pltpu

F32 = jnp.float32
BF16 = jnp.bfloat16

CHUNK = 64
N_LEFT_CHUNKS = 8
ATTN_WINDOW = N_LEFT_CHUNKS * CHUNK
HEAD_DIM = 64
REL_CLIP = 128
CONV_WIDTH = 4
LRU_C = 8.0
N_EXPERTS = 32
TOP_K = 4
SWIGLU_LIMIT = 7.0
SWIGLU_ALPHA = 1.702
NORM_EPS = 1e-6
NEG_INF = -1e30

LANES = 128
SUBLANES = 8
BF16_ROWS = 16
MXU_DIM = 256

TOKEN_TILE = 512
ATTN_Q_TILE = 256
ATTN_STEP_BATCH = 4
EXPERT_ROWS = 256
EXPERT_UNIT_BLOCKS = 4
BIAS_TABLE = 1024
VMEM_LIMIT = 22 * 1024 * 1024
EXPERT_VMEM_LIMIT = 44 * 1024 * 1024


def _params(sem, vmem=VMEM_LIMIT):
    return pltpu.CompilerParams(dimension_semantics=sem, vmem_limit_bytes=vmem)


def _ada_kernel(cp_ref, cs_ref, w_ref, b_ref, op_ref, os_ref):
    w = w_ref[...].astype(BF16)
    for c_ref, o_ref in ((cp_ref, op_ref), (cs_ref, os_ref)):
        c = c_ref[...]
        s = (c * jax.nn.sigmoid(c)).astype(BF16)
        o_ref[...] = jnp.dot(s, w, preferred_element_type=F32) + b_ref[...]


def _ada(c_p, c_s, w_ada, b_ada):
    (n_p, d), n_s = c_p.shape, c_s.shape[0]
    nout = w_ada.shape[1]
    tn = 2048
    mod_p, mod_s = pl.pallas_call(
        _ada_kernel,
        grid=(nout // tn,),
        in_specs=[pl.BlockSpec((n_p, d), lambda j: (0, 0)), pl.BlockSpec((n_s, d), lambda j: (0, 0)),
                  pl.BlockSpec((d, tn), lambda j: (0, j)),
                  pl.BlockSpec((1, tn), lambda j: (0, j))],
        out_specs=[pl.BlockSpec((n_p, tn), lambda j: (0, j)), pl.BlockSpec((n_s, tn), lambda j: (0, j))],
        out_shape=[jax.ShapeDtypeStruct((n_p, nout), F32), jax.ShapeDtypeStruct((n_s, nout), F32)],
        compiler_params=_params(("arbitrary",)),
        name="ada",
    )(c_p, c_s, w_ada, b_ada.reshape(1, nout))
    return mod_p.reshape(n_p, 1, nout), mod_s.reshape(n_s, 1, nout)


MOD_SHIFT_MIX, MOD_SCALE_MIX, MOD_GATE_MIX, MOD_SHIFT_FFN, MOD_SCALE_FFN, MOD_GATE_FFN = range(6)


def _mod_spec(nb, d, term):
    return pl.BlockSpec((nb, 1, d), lambda b, i: (b, 0, term))


def _mixin_kernel(x_ref, sh_ref, sc_ref, ln_ref, win_ref, qn_ref, kn_ref, bd_ref,
                  pre_ref, h0_ref, cw_ref, cb_ref, wa_ref, wx_ref, ba_ref, bx_ref, lam_ref,
                  q_ref, k_ref, v_ref, k32_ref, v32_ref, lru_ref, tail_ref, hl_ref, cx_ref, ch_ref):
    nb, tr, d = x_ref.shape
    m = nb * tr
    aw = q_ref.shape[-1]

    @pl.when(pl.program_id(1) == 0)
    def _():
        cx_ref[...] = pre_ref[...]
        ch_ref[...] = h0_ref[...]

    x = x_ref[...]
    ms = jnp.mean(x * x, axis=-1, keepdims=True)
    h = x * lax.rsqrt(ms + NORM_EPS) * (ln_ref[...] * (1.0 + sc_ref[...])) + sh_ref[...]
    hb = h.reshape(m, d).astype(BF16)

    def proj(part):
        return jnp.dot(hb, win_ref[:, part * aw:(part + 1) * aw], preferred_element_type=F32)

    def head_norm(t, w_ref):
        msq = jnp.dot((t * t).astype(BF16), bd_ref[...], preferred_element_type=F32)
        return t * lax.rsqrt(msq + NORM_EPS) * w_ref[...]

    lru_out, new_tail, h_last = _lru_branch(
        proj(3).reshape(nb, tr, aw), proj(4).reshape(nb, tr, aw), cw_ref, cb_ref,
        wa_ref, wx_ref, ba_ref, bx_ref, lam_ref, cx_ref, ch_ref)
    lru_ref[...] = lru_out
    tail_ref[...] = new_tail
    hl_ref[...] = h_last
    q = head_norm(proj(0), qn_ref)
    k = head_norm(proj(1), kn_ref)
    v = proj(2)
    q_ref[...] = q.astype(BF16).reshape(nb, tr, aw)
    k_ref[...] = k.astype(BF16).reshape(nb, tr, aw)
    v_ref[...] = v.astype(BF16).reshape(nb, tr, aw)
    k32_ref[...] = k.reshape(nb, tr, aw)
    v32_ref[...] = v.reshape(nb, tr, aw)


def _mixin(x, mod, ln_w, w_in_bf, qn_t, kn_t, bd, pre, h0, conv_w, conv_b, wa_bd, wx_bd, b_a, b_x, lam,
           nb, tr):
    nbt, t, d = x.shape
    aw = qn_t.shape[-1]
    c = pre.shape[-1]
    assert c == aw
    keep = min(ATTN_WINDOW, t)
    assert tr == keep or t == tr
    grid = (nbt // nb, t // tr)
    xmap = lambda b, i: (b, i, 0)
    mmap = lambda b, i: (b, 0, 0)
    cmap = lambda b, i: (0, 0)
    cmap3 = lambda b, i: (0, 0, 0)
    tmap = lambda b, i: (b, 0, 0)
    big = pl.BlockSpec((nb, tr, aw), xmap)
    tail = pl.BlockSpec((nb, keep, aw), tmap)
    row = pl.BlockSpec((1, c), cmap)
    return pl.pallas_call(
        _mixin_kernel,
        grid=grid,
        in_specs=[pl.BlockSpec((nb, tr, d), xmap),
                  _mod_spec(nb, d, MOD_SHIFT_MIX), _mod_spec(nb, d, MOD_SCALE_MIX),
                  pl.BlockSpec((1, d), cmap),
                  pl.BlockSpec(w_in_bf.shape, cmap),
                  pl.BlockSpec((1, aw), cmap), pl.BlockSpec((1, aw), cmap),
                  pl.BlockSpec(bd.shape, cmap),
                  pl.BlockSpec((nb, SUBLANES, c), tmap), pl.BlockSpec((nb, 1, c), tmap),
                  pl.BlockSpec(conv_w.shape, cmap), row,
                  pl.BlockSpec(wa_bd.shape, cmap3), pl.BlockSpec(wx_bd.shape, cmap3),
                  row, row, row],
        out_specs=[big, big, big, tail, tail, big,
                   pl.BlockSpec((nb, SUBLANES, c), tmap), pl.BlockSpec((nb, 1, c), tmap)],
        out_shape=[jax.ShapeDtypeStruct((nbt, t, aw), BF16)] * 3
        + [jax.ShapeDtypeStruct((nbt, keep, aw), F32)] * 2
        + [jax.ShapeDtypeStruct((nbt, t, c), BF16),
           jax.ShapeDtypeStruct((nbt, SUBLANES, c), F32),
           jax.ShapeDtypeStruct((nbt, 1, c), F32)],
        scratch_shapes=[pltpu.VMEM((nb, SUBLANES, c), F32), pltpu.VMEM((nb, 1, c), F32)],
        compiler_params=_params(("arbitrary", "arbitrary")),
        name="mixin",
    )(x, mod, mod, ln_w, w_in_bf, qn_t, kn_t, bd, pre, h0, conv_w, conv_b, wa_bd, wx_bd, b_a, b_x, lam)


def _bias_table(rel_bias, off):
    h = rel_bias.shape[0]
    left = off - REL_CLIP
    right = BIAS_TABLE - left - (2 * REL_CLIP + 1)
    assert left >= 0 and right >= 0
    return jnp.concatenate([jnp.broadcast_to(rel_bias[:, :1], (h, left)), rel_bias,
                            jnp.broadcast_to(rel_bias[:, -1:], (h, right))], axis=1)


def _toeplitz(tab_row, rows, cols):
    t = jnp.broadcast_to(tab_row, (rows, BIAS_TABLE))
    t = pltpu.roll(t, BIAS_TABLE - (rows - 1), 1, stride=1, stride_axis=0)
    return t[:, :cols]


def _attn_kernel(q_ref, k0_ref, k1_ref, k2_ref, v0_ref, v1_ref, v2_ref, tab_ref, o_ref, bias_ref):
    b = pl.program_id(0)
    s = pl.program_id(1)
    qt = q_ref.shape[1]
    nk = 3 * qt
    nh = bias_ref.shape[0]

    @pl.when((b == 0) & (s == 0))
    def _():
        qi = lax.broadcasted_iota(jnp.int32, (qt, nk), 0) // CHUNK
        kc = lax.broadcasted_iota(jnp.int32, (qt, nk), 1) // CHUNK
        for h in range(nh):
            band = jnp.where(kc <= qi + N_LEFT_CHUNKS, _toeplitz(tab_ref[h:h + 1, :], qt, nk), NEG_INF)
            bias_ref[h] = jnp.where(kc >= qi, band, NEG_INF)

    pair_w = 2 * HEAD_DIM

    def attend(mask_start):
        q = q_ref[0]
        kcat = jnp.concatenate([k0_ref[0], k1_ref[0], k2_ref[0]], axis=0)
        vcat = jnp.concatenate([v0_ref[0], v1_ref[0], v2_ref[0]], axis=0)
        first = lax.broadcasted_iota(jnp.int32, (qt, pair_w), 1) < HEAD_DIM
        keep = [jnp.where(first, 1.0, 0.0).astype(BF16), jnp.where(first, 0.0, 1.0).astype(BF16)]
        if mask_start:
            in_seq = lax.broadcasted_iota(jnp.int32, (qt, nk), 1) >= (2 - s) * qt
        outs = []
        for pair in range(nh // 2):
            sl = slice(pair * pair_w, (pair + 1) * pair_w)
            q2, k2, v2 = q[:, sl], kcat[:, sl], vcat[:, sl]
            per_head = []
            for sub in range(2):
                sc = lax.dot_general(q2 * keep[sub], k2, (((1,), (1,)), ((), ())), preferred_element_type=F32)
                sc = sc + bias_ref[2 * pair + sub]
                if mask_start:
                    sc = jnp.where(in_seq, sc, NEG_INF)
                mx = jnp.max(sc, axis=-1, keepdims=True)
                p = jnp.exp(sc - mx)
                l = jnp.sum(p, axis=-1, keepdims=True)
                per_head.append(jnp.dot(p.astype(BF16), v2, preferred_element_type=F32) / l)
            outs.append(jnp.where(first, per_head[0], per_head[1]))
        o_ref[0] = jnp.concatenate(outs, axis=-1).astype(BF16)

    @pl.when(s < 2)
    def _():
        attend(True)

    @pl.when(s >= 2)
    def _():
        attend(False)


def _attn_prompt(q, k, v, tab):
    b, s, aw = q.shape
    qt = ATTN_Q_TILE
    nh = aw // HEAD_DIM
    qspec = pl.BlockSpec((1, qt, aw), lambda i, j: (i, j, 0))

    def kspec(back):
        return pl.BlockSpec((1, qt, aw), lambda i, j: (i, jnp.maximum(j - back, 0), 0))

    return pl.pallas_call(
        _attn_kernel,
        grid=(b, s // qt),
        in_specs=[qspec, kspec(2), kspec(1), kspec(0), kspec(2), kspec(1), kspec(0),
                  pl.BlockSpec(tab.shape, lambda i, j: (0, 0))],
        out_specs=qspec,
        out_shape=jax.ShapeDtypeStruct((b, s, aw), BF16),
        scratch_shapes=[pltpu.VMEM((nh, qt, 3 * qt), F32)],
        compiler_params=_params(("arbitrary", "arbitrary")),
        name="attn_prompt",
    )(q, k, k, k, v, v, v, tab)


def _attn_step_kernel(q_ref, kn_ref, vn_ref, ck_ref, cv_ref, tab_ref, o_ref, bias_ref):
    step = pl.program_id(0)
    nbs, t, aw = q_ref.shape
    nh = aw // HEAD_DIM
    rows = nh * t
    r = ck_ref.shape[-1]
    nk = r + LANES
    nt_dims = (((1,), (1,)), ((), ()))

    @pl.when(step == 0)
    def _():
        ok = lax.broadcasted_iota(jnp.int32, (t, nk), 1) < r + t
        for h in range(nh):
            bias_ref[h * t:(h + 1) * t, :] = jnp.where(ok, _toeplitz(tab_ref[h:h + 1, :], t, nk), NEG_INF)

    own = (lax.broadcasted_iota(jnp.int32, (rows, aw), 0) // t
           == lax.broadcasted_iota(jnp.int32, (rows, aw), 1) // HEAD_DIM)
    own_f = jnp.where(own, 1.0, 0.0)
    own_bf = own_f.astype(BF16)
    pad = jnp.zeros((LANES - t, aw), BF16)
    for b in range(nbs):
        q_bd = jnp.concatenate([q_ref[b]] * nh, axis=0) * own_bf
        k_old = ck_ref[b].reshape(aw, r).astype(BF16)
        v_old = cv_ref[b].reshape(aw, r).astype(BF16)
        k_new = jnp.concatenate([kn_ref[b], pad], axis=0)
        v_new = jnp.concatenate([vn_ref[b], pad], axis=0)
        s_old = jnp.dot(q_bd, k_old, preferred_element_type=F32) + bias_ref[:, :r]
        s_new = lax.dot_general(q_bd, k_new, nt_dims, preferred_element_type=F32) + bias_ref[:, r:]
        mx = jnp.maximum(jnp.max(s_old, axis=-1, keepdims=True), jnp.max(s_new, axis=-1, keepdims=True))
        p_old = jnp.exp(s_old - mx)
        p_new = jnp.exp(s_new - mx)
        l = jnp.sum(p_old, axis=-1, keepdims=True) + jnp.sum(p_new, axis=-1, keepdims=True)
        o_all = (lax.dot_general(p_old.astype(BF16), v_old, nt_dims, preferred_element_type=F32)
                 + jnp.dot(p_new.astype(BF16), v_new, preferred_element_type=F32))
        o_all = o_all * own_f / l
        out = o_all[0:t]
        for h in range(1, nh):
            out = out + o_all[h * t:(h + 1) * t]
        o_ref[b] = out.astype(BF16)


def _attn_step(q, kn, vn, ck, cv, tab, nbs):
    b, t, aw = q.shape
    nh = aw // HEAD_DIM
    r = ck.shape[-1]
    new = pl.BlockSpec((nbs, t, aw), lambda i: (i, 0, 0))
    old = pl.BlockSpec((nbs, nh, HEAD_DIM, r), lambda i: (i, 0, 0, 0))
    return pl.pallas_call(
        _attn_step_kernel,
        grid=(b // nbs,),
        in_specs=[new, new, new, old, old, pl.BlockSpec(tab.shape, lambda i: (0, 0))],
        out_specs=new,
        out_shape=jax.ShapeDtypeStruct((b, t, aw), BF16),
        scratch_shapes=[pltpu.VMEM((nh * t, r + LANES), F32)],
        compiler_params=_params(("arbitrary",)),
        name="attn_step",
    )(q, kn, vn, ck, cv, tab)


def _gelu_tanh(x):
    return x * (0.5 * (1.0 + jnp.tanh(0.7978845608028654 * (x + 0.044715 * (x * x * x)))))


def _lru_branch(x, yg, cw_ref, cb_ref, wa_ref, wx_ref, ba_ref, bx_ref, lam_ref, cx_ref, ch_ref):
    nb, tr, c = x.shape
    m = nb * tr
    half = c // 2
    xp = jnp.concatenate([cx_ref[...], x], axis=1)
    new_tail = xp[:, tr:tr + SUBLANES, :]
    groups = tr // SUBLANES
    xg = xp.reshape(nb * (groups + 1), SUBLANES, c)
    first_rows = lax.broadcasted_iota(jnp.int32, (nb, groups, SUBLANES, c), 2)
    y = cb_ref[...] + cw_ref[CONV_WIDTH - 1:CONV_WIDTH, :] * x
    for back in range(1, CONV_WIDTH):
        rot = pltpu.roll(xg, back, 1).reshape(nb, groups + 1, SUBLANES, c)
        shifted = jnp.where(first_rows >= back, rot[:, 1:], rot[:, :groups]).reshape(nb, tr, c)
        y = y + cw_ref[CONV_WIDTH - 1 - back:CONV_WIDTH - back, :] * shifted
    y2 = y.reshape(m, c)
    yb = y2.astype(BF16)

    def gate(w_ref, b_ref):
        g = jnp.concatenate(
            [jnp.dot(yb[:, :half], w_ref[0], preferred_element_type=F32),
             jnp.dot(yb[:, half:], w_ref[1], preferred_element_type=F32)], axis=1)
        return jax.nn.sigmoid(g + b_ref[...])

    rg = gate(wa_ref, ba_ref)
    ig = gate(wx_ref, bx_ref)
    lam = lam_ref[...]
    log_sig = jnp.minimum(lam, 0.0) - jnp.log1p(jnp.exp(-jnp.abs(lam)))
    log_a = rg * (LRU_C * log_sig)
    a_cum = jnp.exp(log_a)
    b_cum = jnp.sqrt(-jnp.tanh(log_a) * (a_cum * a_cum + 1.0)) * (ig * y2)
    a_cum = a_cum.reshape(nb * groups, SUBLANES, c)
    b_cum = b_cum.reshape(nb * groups, SUBLANES, c)
    row = lax.broadcasted_iota(jnp.int32, a_cum.shape, 1)
    dist = 1
    while dist < SUBLANES:
        keep = row >= dist
        a_sh = jnp.where(keep, pltpu.roll(a_cum, dist, 1), 1.0)
        b_sh = jnp.where(keep, pltpu.roll(b_cum, dist, 1), 0.0)
        b_cum = a_cum * b_sh + b_cum
        a_cum = a_cum * a_sh
        dist *= 2
    a_grp = a_cum.reshape(nb, groups, SUBLANES, c)
    b_grp = b_cum.reshape(nb, groups, SUBLANES, c)
    carry = ch_ref[...]
    pieces = []
    for grp in range(groups):
        h_grp = a_grp[:, grp] * carry + b_grp[:, grp]
        carry = h_grp[:, SUBLANES - 1:SUBLANES, :]
        pieces.append(h_grp)
    h = jnp.concatenate(pieces, axis=1)
    ch_ref[...] = carry
    cx_ref[...] = new_tail
    return (h * _gelu_tanh(yg)).astype(BF16), new_tail, carry


def _outproj_kernel(*refs, aliased):
    (at_ref, lr_ref, x_ref, gm_ref, shf_ref, scf_ref, lnf_ref, wo_ref, wr_ref, br_ref) = refs[:10]
    x1_ref, h2_ref, route_ref, cnt_ref = refs[10 + aliased:]
    nb, tr, d = x_ref.shape
    m = nb * tr
    aw = at_ref.shape[-1]
    ne = wr_ref.shape[0]
    at = at_ref[...].reshape(m, aw)
    lr = lr_ref[...].reshape(m, aw)
    mix = (jnp.dot(at, wo_ref[0:aw, :], preferred_element_type=F32)
           + jnp.dot(lr, wo_ref[aw:2 * aw, :], preferred_element_type=F32))
    x1 = x_ref[...] + gm_ref[...] * mix.reshape(nb, tr, d)
    x1_ref[...] = x1
    ms = jnp.mean(x1 * x1, axis=-1, keepdims=True)
    h2 = (x1 * lax.rsqrt(ms + NORM_EPS) * (lnf_ref[...] * (1.0 + scf_ref[...])) + shf_ref[...]).reshape(m, d)
    h2_ref[...] = h2.astype(BF16)

    logits = lax.dot_general(wr_ref[...], h2.astype(BF16), (((1,), (1,)), ((), ())),
                             preferred_element_type=F32) + br_ref[...]
    e_iota = lax.broadcasted_iota(jnp.int32, (ne, m), 0).astype(F32)
    vals = logits
    top_v, sels = [], []
    for k in range(TOP_K):
        mx = jnp.max(vals, axis=0, keepdims=True)
        idx = jnp.min(jnp.where(vals == mx, e_iota, float(ne)), axis=0, keepdims=True)
        sel = e_iota == idx
        vals = jnp.where(sel, -jnp.inf, vals)
        top_v.append(mx)
        sels.append(sel)
        route_ref[0, k:k + 1, :] = idx
    ex = [jnp.exp(v - top_v[0]) for v in top_v]
    den = ex[0] + ex[1] + ex[2] + ex[3]
    chosen = jnp.zeros((ne, m), F32)
    for k in range(TOP_K):
        route_ref[0, 2 * TOP_K + k:2 * TOP_K + k + 1, :] = ex[k] / den
        chosen = chosen + jnp.where(sels[k], 1.0, 0.0)
    before = (lax.broadcasted_iota(jnp.int32, (m, m), 0) < lax.broadcasted_iota(jnp.int32, (m, m), 1))
    rank = jnp.dot(chosen.astype(BF16), jnp.where(before, 1.0, 0.0).astype(BF16), preferred_element_type=F32)
    for k in range(TOP_K):
        route_ref[0, TOP_K + k:TOP_K + k + 1, :] = jnp.sum(jnp.where(sels[k], rank, 0.0), axis=0, keepdims=True)
    route_ref[0, 3 * TOP_K:4 * TOP_K, :] = jnp.zeros((TOP_K, m), F32)
    cnt_ref[0] = jnp.broadcast_to(jnp.sum(chosen, axis=1, keepdims=True), (ne, LANES))


def _outproj(attn, lru_o, x, mod, lnf, w_out_bf, wr_t, br, nb, tr, n_tiles, tile0, prev):
    nbt, t, d = x.shape
    aw = attn.shape[-1]
    m = nb * tr
    assert m == TOKEN_TILE
    ne = wr_t.shape[0]
    tiles_per_seq = t // tr
    xmap = lambda b, i: (b, i, 0)
    c2 = lambda b, i: (0, 0)
    tile = lambda b, i: (tile0 + b * tiles_per_seq + i, 0)
    tile3 = lambda b, i: (tile0 + b * tiles_per_seq + i, 0, 0)
    in_specs = [pl.BlockSpec((nb, tr, aw), xmap), pl.BlockSpec((nb, tr, aw), xmap),
                pl.BlockSpec((nb, tr, d), xmap),
                _mod_spec(nb, d, MOD_GATE_MIX), _mod_spec(nb, d, MOD_SHIFT_FFN), _mod_spec(nb, d, MOD_SCALE_FFN),
                pl.BlockSpec((1, d), c2), pl.BlockSpec(w_out_bf.shape, c2),
                pl.BlockSpec(wr_t.shape, c2), pl.BlockSpec((ne, 1), c2)]
    args = [attn, lru_o, x, mod, mod, mod, lnf, w_out_bf, wr_t, br]
    aliases = {}
    if prev is not None:
        in_specs += [pl.BlockSpec(memory_space=pl.ANY)] * 3
        args += list(prev)
        aliases = {10: 1, 11: 2, 12: 3}
    return pl.pallas_call(
        functools.partial(_outproj_kernel, aliased=len(aliases)),
        grid=(nbt // nb, tiles_per_seq),
        in_specs=in_specs,
        out_specs=[pl.BlockSpec((nb, tr, d), xmap), pl.BlockSpec((m, d), tile),
                   pl.BlockSpec((1, 4 * TOP_K, m), tile3), pl.BlockSpec((1, ne, LANES), tile3)],
        out_shape=[jax.ShapeDtypeStruct((nbt, t, d), F32),
                   jax.ShapeDtypeStruct((n_tiles * m, d), BF16),
                   jax.ShapeDtypeStruct((n_tiles, 4 * TOP_K, m), F32),
                   jax.ShapeDtypeStruct((n_tiles, ne, LANES), F32)],
        input_output_aliases=aliases,
        compiler_params=_params(("arbitrary", "arbitrary")),
        name="outproj",
    )(*args)


def _tile_rows(m):
    cap = TOP_K * m + N_EXPERTS * (BF16_ROWS - 1) + BF16_ROWS
    return -(-cap // TOKEN_TILE) * TOKEN_TILE


def _table_sizes(nt):
    g = BF16_ROWS
    m = TOKEN_TILE
    n_chunks = _tile_rows(m) // g
    n_gap = -(-(N_EXPERTS * (EXPERT_ROWS // g - 1)) // nt)
    bound = TOP_K * m * nt + nt * N_EXPERTS * (g - 1) + N_EXPERTS * (EXPERT_ROWS - g)
    n_sorted = -(-bound // EXPERT_ROWS) * EXPERT_ROWS
    return n_chunks, n_gap, n_sorted, n_sorted + 2 * (n_chunks + n_gap) * g


def _route_tables(cnt):
    nt = cnt.shape[0]
    g = BF16_ROWS
    bm = EXPERT_ROWS
    n_chunks, n_gap, n_sorted, _ = _table_sizes(nt)
    e_ids = jnp.arange(N_EXPERTS, dtype=jnp.int32)
    t_ids = jnp.arange(nt, dtype=jnp.int32)
    upto = (e_ids[:, None] <= e_ids[None, :]).astype(jnp.int32)
    pc = (cnt + g - 1) // g * g
    ctile = jnp.sum(pc[:, :, None] * upto[None], axis=1)
    toff = ctile - pc
    trow = ctile[:, -1]
    tot = jnp.sum(pc, axis=0)
    reg = (tot + bm - 1) // bm * bm
    creg = jnp.sum(reg[:, None] * upto, axis=0)
    base = creg - reg
    earlier = (t_ids[:, None] < t_ids[None, :]).astype(jnp.int32)
    goff = base[None, :] + jnp.sum(pc[:, None, :] * earlier[:, :, None], axis=0)
    r = jnp.arange(n_chunks, dtype=jnp.int32) * g
    r3 = r[None, :, None]
    in_seg = (toff[:, None, :] <= r3) & (r3 < ctile[:, None, :])
    dst = jnp.sum(jnp.where(in_seg, (goff - toff)[:, None, :], 0), axis=2) + r[None, :]
    dst = jnp.where(r[None, :] < trow[:, None], dst, -1)
    gcnt = (reg - tot) // g
    gcum = jnp.sum(gcnt[:, None] * upto, axis=0)
    gstart = gcum - gcnt
    s = jnp.arange(nt * n_gap, dtype=jnp.int32)
    in_gap = (gstart[None, :] <= s[:, None]) & (s[:, None] < gcum[None, :])
    gdst = jnp.sum(jnp.where(in_gap, (base + tot - g * gstart)[None, :] + g * s[:, None], 0), axis=1)
    gdst = jnp.where(s < gcum[-1], gdst, -1).reshape(nt, n_gap)
    table = jnp.concatenate([dst, gdst], axis=1).astype(jnp.int32)
    n_entries = n_chunks + n_gap
    spare = n_sorted + ((t_ids % 2)[:, None] * n_entries + jnp.arange(n_entries, dtype=jnp.int32)[None, :]) * g
    dispatch_tab = jnp.where(table >= 0, table, spare).astype(jnp.int32)
    combine_tab = jnp.maximum(dst, 0).astype(jnp.int32)
    toff_b = jnp.broadcast_to(toff.astype(F32)[:, :, None], (nt, N_EXPERTS, LANES))
    limit = (creg[-1] - EXPERT_UNIT_BLOCKS * bm).astype(jnp.int32).reshape(1)
    nblk = reg // bm
    units = (nblk + EXPERT_UNIT_BLOCKS - 1) // EXPERT_UNIT_BLOCKS
    ubase = jnp.sum(units[:, None] * upto, axis=0) - units
    return (dispatch_tab, combine_tab, toff_b, base.astype(jnp.int32), nblk.astype(jnp.int32),
            ubase.astype(jnp.int32), limit)


def _slot_rows(route_ref, toff_ref, m):
    ne = toff_ref.shape[1]
    e_iota = lax.broadcasted_iota(jnp.int32, (ne, m), 0).astype(F32)
    toff_col = toff_ref[0][:, 0:1]
    pos = []
    for k in range(TOP_K):
        sel = e_iota == route_ref[0, k:k + 1, :]
        start = jnp.sum(jnp.where(sel, toff_col, 0.0), axis=0, keepdims=True)
        pos.append(start + route_ref[0, TOP_K + k:TOP_K + k + 1, :])
    return pos


def _dispatch_kernel(tab_ref, h2_ref, route_ref, toff_ref, xs_hbm, buf_ref, sem, *, n_chunks, n_tiles):
    t = pl.program_id(0)
    slot = t % 2
    m = h2_ref.shape[0]
    rows = buf_ref.shape[1]
    n_entries = tab_ref.shape[1]
    g = BF16_ROWS
    per_chunk = m // g

    def start(c):
        src = c * g if c < n_chunks else rows - g
        pltpu.make_async_copy(
            buf_ref.at[slot, pl.ds(src, g)],
            xs_hbm.at[pl.ds(pl.multiple_of(tab_ref[t, c], g), g)], sem.at[slot]).start()

    def wait_all(which):
        for _ in range(n_entries):
            pltpu.make_async_copy(buf_ref.at[which, pl.ds(0, g)], xs_hbm.at[pl.ds(0, g)], sem.at[which]).wait()

    @pl.when(t >= 2)
    def _():
        wait_all(slot)

    pos = _slot_rows(route_ref, toff_ref, m)
    h2 = h2_ref[...]
    for rc in range(rows // m):
        r_iota = (lax.broadcasted_iota(jnp.int32, (m, m), 0) + rc * m).astype(F32)
        onehot = jnp.zeros((m, m), F32)
        for k in range(TOP_K):
            onehot = jnp.where(r_iota == pos[k], 1.0, onehot)
        buf_ref[slot, rc * m:(rc + 1) * m, :] = jnp.dot(
            onehot.astype(BF16), h2, preferred_element_type=F32).astype(BF16)
        for c in range((rc - 1) * per_chunk, rc * per_chunk) if rc > 0 else ():
            start(c)
    for c in range(n_chunks - per_chunk, n_entries):
        start(c)

    @pl.when(t == n_tiles - 1)
    def _():
        if n_tiles > 1:
            wait_all(1 - slot)
        wait_all(slot)


def _dispatch(table, h2, route, toff_b, n_rows, n_chunks):
    nt = route.shape[0]
    m = TOKEN_TILE
    d = h2.shape[1]
    rows = _tile_rows(m)
    grid_spec = pltpu.PrefetchScalarGridSpec(
        num_scalar_prefetch=1,
        grid=(nt,),
        in_specs=[pl.BlockSpec((m, d), lambda t, tab: (t, 0)),
                  pl.BlockSpec((1, 4 * TOP_K, m), lambda t, tab: (t, 0, 0)),
                  pl.BlockSpec((1, N_EXPERTS, LANES), lambda t, tab: (t, 0, 0))],
        out_specs=pl.BlockSpec(memory_space=pl.ANY),
        scratch_shapes=[pltpu.VMEM((2, rows, d), BF16), pltpu.SemaphoreType.DMA((2,))],
    )
    return pl.pallas_call(
        functools.partial(_dispatch_kernel, n_chunks=n_chunks, n_tiles=nt),
        grid_spec=grid_spec,
        out_shape=jax.ShapeDtypeStruct((n_rows, d), BF16),
        compiler_params=_params(("arbitrary",)),
        name="dispatch",
    )(table, h2, route, toff_b)


def _expert_kernel(row0_ref, nblk_ref, ubase_ref, limit_ref, xs_hbm, wgu_ref, bgu_ref, wdn_ref, bdn_ref, ys_hbm,
                   wgu_bf, wdn_bf, xbuf, ybuf, sem_in, sem_out, pend_ref):
    e = pl.program_id(0)
    ne = pl.num_programs(0)
    bm = EXPERT_ROWS
    unit = xbuf.shape[1]
    per_unit = unit // bm
    dff = wdn_ref.shape[1]
    nblk = nblk_ref[e]
    n_units = (nblk + per_unit - 1) // per_unit
    base = ubase_ref[e]

    def unit_start(expert, s):
        true = row0_ref[expert] + s * unit
        start = jnp.minimum(true, limit_ref[0])
        return pl.multiple_of(start, bm), pl.multiple_of(true - start, bm)

    def in_copy(expert, s, slot):
        start, _ = unit_start(expert, s)
        return pltpu.make_async_copy(xs_hbm.at[pl.ds(start, unit)], xbuf.at[slot], sem_in.at[slot])

    def out_copy(j, slot, i):
        start = pl.multiple_of(row0_ref[e] + j * bm, bm)
        return pltpu.make_async_copy(ybuf.at[slot, pl.ds(i * bm, bm)], ys_hbm.at[pl.ds(start, bm)],
                                     sem_out.at[slot])

    def wait_pending(slot):
        count = pend_ref[slot]

        @pl.when(count == per_unit)
        def _():
            for i in range(per_unit):
                out_copy(0, slot, i).wait()

        for i in range(per_unit - 1):
            @pl.when((count < per_unit) & (i < count))
            def _():
                out_copy(0, slot, i).wait()
        pend_ref[slot] = 0

    @pl.when(e == 0)
    def _():
        pend_ref[0] = 0
        pend_ref[1] = 0

    @pl.when((e == 0) & (nblk > 0))
    def _():
        in_copy(e, 0, 0).start()

    wgu_bf[...] = wgu_ref[0].astype(BF16)
    wdn_bf[...] = wdn_ref[0].astype(BF16)

    def run_unit(s, carry):
        slot = (base + s) % 2
        in_copy(e, s, slot).wait()

        @pl.when(s + 1 < n_units)
        def _():
            in_copy(e, s + 1, 1 - slot).start()

        wait_pending(slot)
        _, lead = unit_start(e, s)

        def ffn(first, count):
            rows = count * bm
            x = xbuf[slot, pl.ds(pl.multiple_of(lead + first * bm, bm), rows), :]
            gu = jnp.dot(x, wgu_bf[...], preferred_element_type=F32) + bgu_ref[0]
            gate = jnp.minimum(gu[:, :dff], SWIGLU_LIMIT)
            up = jnp.clip(gu[:, dff:], -SWIGLU_LIMIT, SWIGLU_LIMIT)
            glu = gate * jax.nn.sigmoid(gate * SWIGLU_ALPHA)
            act = ((up + 1.0) * glu).astype(BF16)
            ybuf[slot, first * bm:first * bm + rows, :] = (
                jnp.dot(act, wdn_bf[...], preferred_element_type=F32) + bdn_ref[0]).astype(BF16)
            for i in range(first, first + count):
                out_copy(s * per_unit + i, slot, i).start()

        for first in range(0, per_unit, 2):
            have = nblk - s * per_unit - first

            @pl.when(have >= 2)
            def _():
                ffn(first, 2)

            @pl.when(have == 1)
            def _():
                ffn(first, 1)
        pend_ref[slot] = jnp.minimum(nblk - s * per_unit, per_unit)
        return carry

    lax.fori_loop(0, n_units, run_unit, 0)

    nxt = jnp.minimum(e + 1, ne - 1)

    @pl.when((e + 1 < ne) & (nblk_ref[nxt] > 0))
    def _():
        in_copy(nxt, 0, (base + n_units) % 2).start()

    @pl.when(e == ne - 1)
    def _():
        wait_pending(0)
        wait_pending(1)


def _experts(row0, nblk, ubase, limit, xs, w_gu, b_gu, w_dn, b_dn):
    n_rows, d = xs.shape
    bm = EXPERT_ROWS
    unit = EXPERT_UNIT_BLOCKS * bm
    ne, _, dff2 = w_gu.shape
    dff = w_dn.shape[1]
    exp3 = lambda e, r0, nb, ub, lim: (e, 0, 0)
    grid_spec = pltpu.PrefetchScalarGridSpec(
        num_scalar_prefetch=4,
        grid=(ne,),
        in_specs=[pl.BlockSpec(memory_space=pl.ANY),
                  pl.BlockSpec((1, d, dff2), exp3), pl.BlockSpec((1, 1, dff2), exp3),
                  pl.BlockSpec((1, dff, d), exp3), pl.BlockSpec((1, 1, d), exp3)],
        out_specs=pl.BlockSpec(memory_space=pl.ANY),
        scratch_shapes=[pltpu.VMEM((d, dff2), BF16), pltpu.VMEM((dff, d), BF16),
                        pltpu.VMEM((2, unit, d), BF16), pltpu.VMEM((2, unit, d), BF16),
                        pltpu.SemaphoreType.DMA((2,)), pltpu.SemaphoreType.DMA((2,)),
                        pltpu.SMEM((2,), jnp.int32)],
    )
    return pl.pallas_call(
        _expert_kernel,
        grid_spec=grid_spec,
        out_shape=jax.ShapeDtypeStruct((n_rows, d), BF16),
        compiler_params=_params(("arbitrary",), EXPERT_VMEM_LIMIT),
        name="experts",
    )(row0, nblk, ubase, limit, xs, w_gu, b_gu.reshape(ne, 1, dff2), w_dn, b_dn.reshape(ne, 1, d))


def _combine_kernel(tab_ref, ys_hbm, route_ref, toff_ref, x1_ref, gf_ref, o_ref, buf_ref, sem, *, n_chunks, tile0):
    step = pl.program_id(0)
    n_steps = pl.num_programs(0)
    t = tile0 + step
    slot = step % 2
    nb, tr, d = x1_ref.shape
    m = nb * tr
    rows = buf_ref.shape[1]
    g = BF16_ROWS

    def fetch(tile, which):
        for c in range(n_chunks):
            pltpu.make_async_copy(
                ys_hbm.at[pl.ds(pl.multiple_of(tab_ref[tile, c], g), g)],
                buf_ref.at[which, pl.ds(c * g, g)], sem.at[which]).start()

    @pl.when(step == 0)
    def _():
        fetch(t, slot)

    @pl.when(step + 1 < n_steps)
    def _():
        fetch(t + 1, 1 - slot)

    pos = _slot_rows(route_ref, toff_ref, m)
    gates = [route_ref[0, 2 * TOP_K + k:2 * TOP_K + k + 1, :] for k in range(TOP_K)]
    stacked = jnp.concatenate(pos + gates + [jnp.zeros((LANES - 2 * TOP_K, m), F32)], axis=0)
    cols = stacked.T
    for _ in range(n_chunks):
        pltpu.make_async_copy(ys_hbm.at[pl.ds(0, g)], buf_ref.at[slot, pl.ds(0, g)], sem.at[slot]).wait()

    acc = jnp.zeros((m, d), F32)
    for rc in range(rows // m):
        c_iota = (lax.broadcasted_iota(jnp.int32, (m, m), 1) + rc * m).astype(F32)
        weights = jnp.zeros((m, m), F32)
        for k in range(TOP_K):
            weights = jnp.where(c_iota == cols[:, k:k + 1], cols[:, TOP_K + k:TOP_K + k + 1], weights)
        acc = acc + jnp.dot(weights.astype(BF16), buf_ref[slot, rc * m:(rc + 1) * m, :],
                            preferred_element_type=F32)
    o_ref[...] = x1_ref[...] + gf_ref[...] * acc.reshape(nb, tr, d)


def _combine(table, ys, route, toff_b, x1, gf, nb, tr, tile0, n_chunks):
    nbt, t, d = x1.shape
    m = nb * tr
    assert m == TOKEN_TILE
    rows = _tile_rows(m)
    tiles_per_seq = t // tr
    n_steps = (nbt // nb) * tiles_per_seq
    xmap = lambda s, tab: (s // tiles_per_seq, s % tiles_per_seq, 0)
    grid_spec = pltpu.PrefetchScalarGridSpec(
        num_scalar_prefetch=1,
        grid=(n_steps,),
        in_specs=[pl.BlockSpec(memory_space=pl.ANY),
                  pl.BlockSpec((1, 4 * TOP_K, m), lambda s, tab: (tile0 + s, 0, 0)),
                  pl.BlockSpec((1, N_EXPERTS, LANES), lambda s, tab: (tile0 + s, 0, 0)),
                  pl.BlockSpec((nb, tr, d), xmap),
                  pl.BlockSpec((nb, 1, d), lambda s, tab: (s // tiles_per_seq, 0, MOD_GATE_FFN))],
        out_specs=pl.BlockSpec((nb, tr, d), xmap),
        scratch_shapes=[pltpu.VMEM((2, rows, d), BF16), pltpu.SemaphoreType.DMA((2,))],
    )
    return pl.pallas_call(
        functools.partial(_combine_kernel, n_chunks=n_chunks, tile0=tile0),
        grid_spec=grid_spec,
        out_shape=jax.ShapeDtypeStruct((nbt, t, d), F32),
        compiler_params=_params(("arbitrary",)),
        name="combine",
    )(table, ys, route, toff_b, x1, gf)


def _block_diag(w, groups):
    n, k, _ = w.shape
    w = w.reshape(n // groups, groups, k, k)
    eye = jnp.eye(groups, dtype=w.dtype)
    return jnp.einsum("ngij,gh->ngihj", w, eye).reshape(n // groups, groups * k, groups * k)


def _layer(xp, xs, mod_p, mod_s, k_cache, v_cache, conv_state, lru_state, lw):
    (ln_mix, ln_ffn, w_in, q_norm, k_norm, rel_bias, conv_w, conv_b, w_rg, b_rg, w_ig, b_ig, lam,
     w_out, w_router, b_router, w_gu, b_gu, w_dn, b_dn) = lw
    bp, s, d = xp.shape
    bs, ts, _ = xs.shape
    aw = w_out.shape[0] // 2
    nh = aw // HEAD_DIM
    m = TOKEN_TILE
    assert s % m == 0 and bs * ts == m and s % ATTN_Q_TILE == 0

    w_in_bf = w_in.astype(BF16)
    w_out_bf = w_out.astype(BF16)
    qn_t = jnp.tile(q_norm * (HEAD_DIM ** -0.5), nh).reshape(1, aw)
    kn_t = jnp.tile(k_norm, nh).reshape(1, aw)
    head_mean = _block_diag(jnp.full((nh, HEAD_DIM, HEAD_DIM), 1.0 / HEAD_DIM, F32), nh)[0].astype(BF16)
    groups = MXU_DIM // w_rg.shape[-1]
    wa_bd = _block_diag(w_rg, groups).astype(BF16)
    wx_bd = _block_diag(w_ig, groups).astype(BF16)
    lw_c = b_rg.size
    b_a = b_rg.reshape(1, lw_c)
    b_x = b_ig.reshape(1, lw_c)
    lam2 = lam.reshape(1, lw_c)
    cb2 = conv_b.reshape(1, lw_c)
    ln_mix2 = ln_mix.reshape(1, d)
    ln_ffn2 = ln_ffn.reshape(1, d)
    wr_t = w_router.T.astype(BF16)
    br = b_router.reshape(-1, 1)
    tab_p = _bias_table(rel_bias, 3 * ATTN_Q_TILE - 1)
    r_cache = k_cache.shape[1]
    tab_s = _bias_table(rel_bias, r_cache + ts - 1)

    zeros_pre = jnp.zeros((bp, SUBLANES, lw_c), F32)
    zeros_h = jnp.zeros((bp, 1, lw_c), F32)
    pre_s = jnp.pad(conv_state, ((0, 0), (SUBLANES - (CONV_WIDTH - 1), 0), (0, 0)))
    lru_w = (conv_w, cb2, wa_bd, wx_bd, b_a, b_x, lam2)
    qp, kp, vp, k32p, v32p, lru_p, tail_p, hl_p = _mixin(
        xp, mod_p,ln_mix2, w_in_bf, qn_t, kn_t, head_mean, zeros_pre, zeros_h, *lru_w, 1, m)
    qs, ks, vs, k32s, v32s, lru_s, tail_s, hl_s = _mixin(
        xs, mod_s,ln_mix2, w_in_bf, qn_t, kn_t, head_mean, pre_s, lru_state[:, None, :], *lru_w, bs, ts)
    attn_p = _attn_prompt(qp, kp, vp, tab_p)
    attn_s = _attn_step(qs, ks, vs, jnp.transpose(k_cache, (0, 2, 3, 1)), jnp.transpose(v_cache, (0, 2, 3, 1)),
                        tab_s, ATTN_STEP_BATCH)

    n_tiles = bp * (s // m) + 1
    x1p, h2, route, cnt = _outproj(attn_p, lru_p, xp, mod_p,ln_ffn2, w_out_bf, wr_t, br,
                                   1, m, n_tiles, 0, None)
    x1s, h2, route, cnt = _outproj(attn_s, lru_s, xs, mod_s,ln_ffn2, w_out_bf, wr_t, br,
                                   bs, ts, n_tiles, n_tiles - 1, (h2, route, cnt))

    n_chunks, _, _, n_rows = _table_sizes(n_tiles)
    assert TOP_K * m * n_tiles >= EXPERT_UNIT_BLOCKS * EXPERT_ROWS
    dispatch_tab, combine_tab, toff_b, row0, nblk, ubase, limit = _route_tables(cnt[:, :, 0].astype(jnp.int32))
    xs_sorted = _dispatch(dispatch_tab, h2, route, toff_b, n_rows, n_chunks)
    ys_sorted = _experts(row0, nblk, ubase, limit, xs_sorted, w_gu, b_gu, w_dn, b_dn)
    yp = _combine(combine_tab, ys_sorted, route, toff_b, x1p, mod_p, 1, m, 0, n_chunks)
    ysm = _combine(combine_tab, ys_sorted, route, toff_b, x1s, mod_s, bs, ts, n_tiles - 1, n_chunks)

    keep = k32p.shape[1]
    new = (k32p.reshape(bp, keep, nh, HEAD_DIM), v32p.reshape(bp, keep, nh, HEAD_DIM),
           tail_p[:, SUBLANES - (CONV_WIDTH - 1):, :], hl_p[:, 0, :],
           k32s.reshape(bs, ts, nh, HEAD_DIM), v32s.reshape(bs, ts, nh, HEAD_DIM),
           tail_s[:, SUBLANES - (CONV_WIDTH - 1):, :], hl_s[:, 0, :])
    return yp, ysm, new


def kernel(x_prompt, x_sample, c_prompt, c_sample, cache_k, cache_v, state_conv, state_lru, ln_mix_w, ln_ffn_w, w_ada, b_ada, w_in, q_norm_w, k_norm_w, rel_bias, conv_w, conv_b, w_rgate, b_rgate, w_igate, b_igate, lru_lambda, w_out, w_router, b_router, w_gate_up, b_gate_up, w_down, b_down):
    depth = w_in.shape[0]
    yp, ys = x_prompt, x_sample
    collected = [[] for _ in range(8)]
    for l in range(depth):
        mod_p, mod_s = _ada(c_prompt, c_sample, w_ada[l], b_ada[l])
        lw = (ln_mix_w[l], ln_ffn_w[l], w_in[l], q_norm_w[l], k_norm_w[l], rel_bias[l], conv_w[l], conv_b[l],
              w_rgate[l], b_rgate[l], w_igate[l], b_igate[l], lru_lambda[l], w_out[l], w_router[l], b_router[l],
              w_gate_up[l], b_gate_up[l], w_down[l], b_down[l])
        yp, ys, new = _layer(yp, ys, mod_p, mod_s, cache_k[l], cache_v[l], state_conv[l], state_lru[l], lw)
        for acc, val in zip(collected, new):
            acc.append(val)
    return (yp, ys) + tuple(jnp.stack(vals) for vals in collected)
```
